```python
import jax, jax.numpy as jnp
from jax import lax
import numpy as np

D_MODEL = 1024
BATCH = 8
SEQ = 4096
DEPTH = 2

N_CONV_LAYERS = DEPTH // 2
N_ATTN_LAYERS = DEPTH - N_CONV_LAYERS
CONV_WIDTH = 3
HEAD_DIM = 64
N_HEADS = D_MODEL // HEAD_DIM
BRANCHES = ((128, 1), (512, 4), (2048, 16))
N_BRANCHES = len(BRANCHES)
Q_WIDTH = N_BRANCHES * N_HEADS * HEAD_DIM
D_FF = -(-8 * D_MODEL // (3 * 256)) * 256
ROPE_THETA = 10000.0
RMS_EPS = 1e-6
NEG_INF = -1e30

kernel_name = "yoco_shortconv_dilated_attention_trunk"


def rms_norm(x, g):
    xf = x.astype(jnp.float32)
    y = xf * lax.rsqrt(jnp.mean(xf * xf, axis=-1, keepdims=True) + RMS_EPS)
    return (y * g.astype(jnp.float32)).astype(x.dtype)


def rope(t, positions):
    half = HEAD_DIM // 2
    inv_freq = ROPE_THETA ** (-jnp.arange(half, dtype=jnp.float32) / half)
    ang = positions.astype(jnp.float32)[..., None] * inv_freq
    cos = jnp.cos(ang)[:, :, None, :]
    sin = jnp.sin(ang)[:, :, None, :]
    t1 = t[..., :half].astype(jnp.float32)
    t2 = t[..., half:].astype(jnp.float32)
    out = jnp.concatenate([t1 * cos - t2 * sin, t2 * cos + t1 * sin], axis=-1)
    return out.astype(t.dtype)


def short_conv_mixer(x, w_in, conv_w, w_out):
    b_gate, c_gate, h = jnp.split(x @ w_in, 3, axis=-1)
    u = c_gate * h
    rhs = conv_w[:, None, :].astype(u.dtype)
    conv = lax.conv_general_dilated(
        u, rhs, window_strides=(1,), padding=[(CONV_WIDTH - 1, 0)],
        dimension_numbers=("NWC", "WIO", "NWC"), feature_group_count=u.shape[-1])
    return (b_gate * conv) @ w_out


def swiglu(x, w_gate_up, w_down):
    g, u = jnp.split(x @ w_gate_up, 2, axis=-1)
    return (jax.nn.silu(g) * u) @ w_down


def dilated_branch(q, k, v, window, dilation):
    band = window // dilation
    B, S, H, Dh = q.shape
    chunk = dilation * band
    Sp = -(-S // chunk) * chunk
    nb = Sp // chunk
    pad = ((0, 0), (0, Sp - S), (0, 0), (0, 0))

    def to_blocks(t):
        t = jnp.pad(t, pad).reshape(B, nb, band, dilation, H, Dh)
        return t.transpose(0, 3, 4, 1, 2, 5)

    def with_prev(t):
        prev = jnp.pad(t, ((0, 0), (0, 0), (0, 0), (1, 0), (0, 0), (0, 0)))[:, :, :, :-1]
        return jnp.concatenate([prev, t], axis=4)

    qb = to_blocks(q * (HEAD_DIM ** -0.5))
    kk = with_prev(to_blocks(k))
    vv = with_prev(to_blocks(v))
    s = jnp.einsum("brhnqd,brhnkd->brhnqk", qb, kk).astype(jnp.float32)
    qi = jnp.arange(band)[:, None]
    kj = jnp.arange(2 * band)[None, :]
    dist = qi + band - kj
    in_band = (dist >= 0) & (dist <= band)
    has_prev = (kj >= band)[None] | (jnp.arange(nb)[:, None, None] > 0)
    mask = in_band[None] & has_prev
    s = jnp.where(mask, s, NEG_INF)
    m = jnp.max(s, axis=-1)
    p = jnp.exp(s - m[..., None])
    l = jnp.sum(p, axis=-1)
    o = jnp.einsum("brhnqk,brhnkd->brhnqd", p, vv.astype(jnp.float32)) / l[..., None]
    lse = m + jnp.log(l)
    o = o.transpose(0, 3, 4, 1, 2, 5).reshape(B, Sp, H, Dh)[:, :S]
    lse = lse.transpose(0, 3, 4, 1, 2).reshape(B, Sp, H)[:, :S]
    return o, lse


def dilated_attention_mixer(x, positions, k_sh, v_sh, w_q, w_o):
    B, S, _ = x.shape
    q = (x @ w_q).reshape(B, S, N_BRANCHES * N_HEADS, HEAD_DIM)
    q = rope(q, positions).reshape(B, S, N_BRANCHES, N_HEADS, HEAD_DIM)
    outs, lses = [], []
    for g, (window, dilation) in enumerate(BRANCHES):
        o, lse = dilated_branch(q[:, :, g], k_sh[:, :, g], v_sh[:, :, g], window, dilation)
        outs.append(o)
        lses.append(lse)
    wts = jax.nn.softmax(jnp.stack(lses, axis=0), axis=0)
    o = jnp.einsum("gbsh,gbshd->bshd", wts, jnp.stack(outs, axis=0))
    return o.astype(x.dtype).reshape(B, S, N_HEADS * HEAD_DIM) @ w_o


def shared_kv(h, positions, kv_norm, w_kv):
    B, S, _ = h.shape
    kv = (rms_norm(h, kv_norm) @ w_kv).reshape(B, S, 2, N_BRANCHES * N_HEADS, HEAD_DIM)
    k = rope(kv[:, :, 0], positions).reshape(B, S, N_BRANCHES, N_HEADS, HEAD_DIM)
    v = kv[:, :, 1].reshape(B, S, N_BRANCHES, N_HEADS, HEAD_DIM)
    return k, v


def _fwd_setup_inputs(seed: int = 0) -> dict:
    key = jax.random.key(seed)
    ks = jax.random.split(key, 20)
    f32 = jnp.float32

    def w(k, shape, fan_in):
        return jax.random.normal(k, shape, f32) * (fan_in ** -0.5)

    def gain(k, shape):
        return 1.0 + 0.05 * jax.random.normal(k, shape, f32)

    x = jax.random.normal(ks[0], (BATCH, SEQ, D_MODEL), f32)
    offset = jax.random.randint(ks[1], (BATCH, 1), 0, 4096, dtype=jnp.int32)
    positions = offset + jnp.arange(SEQ, dtype=jnp.int32)[None, :]
    nA, nB = N_CONV_LAYERS, N_ATTN_LAYERS
    return {
        "x": x,
        "positions": positions,
        "mix_norm_pre": gain(ks[2], (DEPTH, D_MODEL)),
        "mix_norm_post": gain(ks[3], (DEPTH, D_MODEL)),
        "ffn_norm_pre": gain(ks[4], (DEPTH, D_MODEL)),
        "ffn_norm_post": gain(ks[5], (DEPTH, D_MODEL)),
        "ffn_w_gate_up": w(ks[6], (DEPTH, D_MODEL, 2 * D_FF), D_MODEL),
        "ffn_w_down": w(ks[7], (DEPTH, D_FF, D_MODEL), D_FF),
        "conv_w_in": w(ks[8], (nA, D_MODEL, 3 * D_MODEL), D_MODEL),
        "conv_w": w(ks[9], (nA, CONV_WIDTH, D_MODEL), CONV_WIDTH),
        "conv_w_out": w(ks[10], (nA, D_MODEL, D_MODEL), D_MODEL),
        "kv_norm": gain(ks[11], (D_MODEL,)),
        "w_kv": w(ks[12], (D_MODEL, 2 * Q_WIDTH), D_MODEL),
        "w_q": w(ks[13], (nB, D_MODEL, Q_WIDTH), D_MODEL),
        "w_o": w(ks[14], (nB, N_HEADS * HEAD_DIM, D_MODEL), N_HEADS * HEAD_DIM),
    }


def _fwd_reference(x, positions, mix_norm_pre, mix_norm_post, ffn_norm_pre, ffn_norm_post,
              ffn_w_gate_up, ffn_w_down, conv_w_in, conv_w, conv_w_out,
              kv_norm, w_kv, w_q, w_o):
    h = x
    for layer in range(DEPTH):
        if layer == N_CONV_LAYERS:
            k_sh, v_sh = shared_kv(h, positions, kv_norm, w_kv)
        hn = rms_norm(h, mix_norm_pre[layer])
        if layer < N_CONV_LAYERS:
            y = short_conv_mixer(hn, conv_w_in[layer], conv_w[layer], conv_w_out[layer])
        else:
            j = layer - N_CONV_LAYERS
            y = dilated_attention_mixer(hn, positions, k_sh, v_sh, w_q[j], w_o[j])
        h = h + rms_norm(y, mix_norm_post[layer])
        f = swiglu(rms_norm(h, ffn_norm_pre[layer]), ffn_w_gate_up[layer], ffn_w_down[layer])
        h = h + rms_norm(f, ffn_norm_post[layer])
    return h


import jax as _jax
import jax.numpy as _jnp

TWIN_FORMAT = 'train_step'
FWD_PARAMS = ['x', 'positions', 'mix_norm_pre', 'mix_norm_post', 'ffn_norm_pre', 'ffn_norm_post', 'ffn_w_gate_up', 'ffn_w_down', 'conv_w_in', 'conv_w', 'conv_w_out', 'kv_norm', 'w_kv', 'w_q', 'w_o']
TWIN_WEIGHTS = ['mix_norm_pre', 'mix_norm_post', 'ffn_norm_pre', 'ffn_norm_post', 'ffn_w_gate_up', 'ffn_w_down', 'conv_w_in', 'conv_w', 'conv_w_out', 'kv_norm', 'w_kv', 'w_q', 'w_o']
TWIN_DIFF_INPUT = 'x'
TWIN_INPUTS = ['x', 'positions', 'mix_norm_pre', 'mix_norm_post', 'ffn_norm_pre', 'ffn_norm_post', 'ffn_w_gate_up', 'ffn_w_down', 'conv_w_in', 'conv_w', 'conv_w_out', 'kv_norm', 'w_kv', 'w_q', 'w_o', 'loss_target', 'm_mix_norm_pre', 'm_mix_norm_post', 'm_ffn_norm_pre', 'm_ffn_norm_post', 'm_ffn_w_gate_up', 'm_ffn_w_down', 'm_conv_w_in', 'm_conv_w', 'm_conv_w_out', 'm_kv_norm', 'm_w_kv', 'm_w_q', 'm_w_o', 'v_mix_norm_pre', 'v_mix_norm_post', 'v_ffn_norm_pre', 'v_ffn_norm_post', 'v_ffn_w_gate_up', 'v_ffn_w_down', 'v_conv_w_in', 'v_conv_w', 'v_conv_w_out', 'v_kv_norm', 'v_w_kv', 'v_w_q', 'v_w_o']
TWIN_OUTPUTS = ['loss', 'grad_x', 'grad_mix_norm_pre', 'grad_mix_norm_post', 'grad_ffn_norm_pre', 'grad_ffn_norm_post', 'grad_ffn_w_gate_up', 'grad_ffn_w_down', 'grad_conv_w_in', 'grad_conv_w', 'grad_conv_w_out', 'grad_kv_norm', 'grad_w_kv', 'grad_w_q', 'grad_w_o', 'delta_mix_norm_pre', 'delta_mix_norm_post', 'delta_ffn_norm_pre', 'delta_ffn_norm_post', 'delta_ffn_w_gate_up', 'delta_ffn_w_down', 'delta_conv_w_in', 'delta_conv_w', 'delta_conv_w_out', 'delta_kv_norm', 'delta_w_kv', 'delta_w_q', 'delta_w_o', 'new_m_mix_norm_pre', 'new_m_mix_norm_post', 'new_m_ffn_norm_pre', 'new_m_ffn_norm_post', 'new_m_ffn_w_gate_up', 'new_m_ffn_w_down', 'new_m_conv_w_in', 'new_m_conv_w', 'new_m_conv_w_out', 'new_m_kv_norm', 'new_m_w_kv', 'new_m_w_q', 'new_m_w_o', 'new_v_mix_norm_pre', 'new_v_mix_norm_post', 'new_v_ffn_norm_pre', 'new_v_ffn_norm_post', 'new_v_ffn_w_gate_up', 'new_v_ffn_w_down', 'new_v_conv_w_in', 'new_v_conv_w', 'new_v_conv_w_out', 'new_v_kv_norm', 'new_v_w_kv', 'new_v_w_q', 'new_v_w_o']
TWIN_LEAF_KINDS = {'loss': 'loss', 'grad_x': 'grad_x', 'grad_mix_norm_pre': 'grad_w', 'grad_mix_norm_post': 'grad_w', 'grad_ffn_norm_pre': 'grad_w', 'grad_ffn_norm_post': 'grad_w', 'grad_ffn_w_gate_up': 'grad_w', 'grad_ffn_w_down': 'grad_w', 'grad_conv_w_in': 'grad_w', 'grad_conv_w': 'grad_w', 'grad_conv_w_out': 'grad_w', 'grad_kv_norm': 'grad_w', 'grad_w_kv': 'grad_w', 'grad_w_q': 'grad_w', 'grad_w_o': 'grad_w', 'delta_mix_norm_pre': 'delta_w', 'delta_mix_norm_post': 'delta_w', 'delta_ffn_norm_pre': 'delta_w', 'delta_ffn_norm_post': 'delta_w', 'delta_ffn_w_gate_up': 'delta_w', 'delta_ffn_w_down': 'delta_w', 'delta_conv_w_in': 'delta_w', 'delta_conv_w': 'delta_w', 'delta_conv_w_out': 'delta_w', 'delta_kv_norm': 'delta_w', 'delta_w_kv': 'delta_w', 'delta_w_q': 'delta_w', 'delta_w_o': 'delta_w', 'new_m_mix_norm_pre': 'new_m', 'new_m_mix_norm_post': 'new_m', 'new_m_ffn_norm_pre': 'new_m', 'new_m_ffn_norm_post': 'new_m', 'new_m_ffn_w_gate_up': 'new_m', 'new_m_ffn_w_down': 'new_m', 'new_m_conv_w_in': 'new_m', 'new_m_conv_w': 'new_m', 'new_m_conv_w_out': 'new_m', 'new_m_kv_norm': 'new_m', 'new_m_w_kv': 'new_m', 'new_m_w_q': 'new_m', 'new_m_w_o': 'new_m', 'new_v_mix_norm_pre': 'new_v', 'new_v_mix_norm_post': 'new_v', 'new_v_ffn_norm_pre': 'new_v', 'new_v_ffn_norm_post': 'new_v', 'new_v_ffn_w_gate_up': 'new_v', 'new_v_ffn_w_down': 'new_v', 'new_v_conv_w_in': 'new_v', 'new_v_conv_w': 'new_v', 'new_v_conv_w_out': 'new_v', 'new_v_kv_norm': 'new_v', 'new_v_w_kv': 'new_v', 'new_v_w_q': 'new_v', 'new_v_w_o': 'new_v'}


def _forward(args):
    return _fwd_reference(*[args[k] for k in FWD_PARAMS])


def _output_shape():
    def fwd():
        inp = _fwd_setup_inputs(0)
        return _fwd_reference(*[inp[k] for k in FWD_PARAMS])
    out = _jax.eval_shape(fwd)
    return out.shape, out.dtype

N_MICROBATCH = 1
ADAM_LR = 0.001
ADAM_B1 = 0.9
ADAM_B2 = 0.999
ADAM_EPS = 1e-08
ADAM_WD = 0.01
ADAM_STEP = 10
PER_EXAMPLE_BATCH_AXIS = {'x': 0, 'positions': 0, 'loss_target': 0}
SHARED_INPUTS = []
_WEIGHT_DTYPES = {'mix_norm_pre': _jnp.float32, 'mix_norm_post': _jnp.float32, 'ffn_norm_pre': _jnp.float32, 'ffn_norm_post': _jnp.float32, 'ffn_w_gate_up': _jnp.float32, 'ffn_w_down': _jnp.float32, 'conv_w_in': _jnp.float32, 'conv_w': _jnp.float32, 'conv_w_out': _jnp.float32, 'kv_norm': _jnp.float32, 'w_kv': _jnp.float32, 'w_q': _jnp.float32, 'w_o': _jnp.float32}
MOMENT_SCALE = {'mix_norm_pre': 1.276483e+00, 'mix_norm_post': 3.184998e+01, 'ffn_norm_pre': 1.025715e+00, 'ffn_norm_post': 3.202064e+01, 'ffn_w_gate_up': 4.192972e-01, 'ffn_w_down': 7.822086e-01, 'conv_w_in': 9.999882e-01, 'conv_w': 1.013897e+00, 'conv_w_out': 1.095713e+00, 'kv_norm': 8.669345e-01, 'w_kv': 3.258784e-01, 'w_q': 3.124469e-01, 'w_o': 6.136607e-01}


def _to_microbatches(a, axis):
    t = _jnp.moveaxis(a, axis, 0)
    t = t.reshape((N_MICROBATCH, t.shape[0] // N_MICROBATCH) + t.shape[1:])
    return _jnp.moveaxis(t, 1, axis + 1)


def setup_inputs(seed: int = 0) -> dict:
    inp = _fwd_setup_inputs(seed)
    key = _jax.random.fold_in(_jax.random.key(seed), 7919)
    shape, _ = _output_shape()
    out = dict(inp)
    out["loss_target"] = _jax.random.normal(_jax.random.fold_in(key, 0), shape, _jnp.float32)
    for i, name in enumerate(TWIN_WEIGHTS):
        w = inp[name].astype(_jnp.float32)
        if MOMENT_SCALE is None:
            s = _jnp.sqrt(_jnp.mean(_jnp.square(w)) + 1e-30)
        else:
            s = MOMENT_SCALE[name]
        km, kv = _jax.random.split(_jax.random.fold_in(key, i + 1))
        out[name] = w
        out["m_" + name] = s * _jax.random.normal(km, w.shape, _jnp.float32)
        out["v_" + name] = (s * s) * _jax.random.uniform(kv, w.shape, _jnp.float32, 0.5, 1.5)
    if N_MICROBATCH > 1:
        for name, axis in PER_EXAMPLE_BATCH_AXIS.items():
            out[name] = _to_microbatches(out[name], axis)
    return {'x': out['x'], 'positions': out['positions'], 'mix_norm_pre': out['mix_norm_pre'], 'mix_norm_post': out['mix_norm_post'], 'ffn_norm_pre': out['ffn_norm_pre'], 'ffn_norm_post': out['ffn_norm_post'], 'ffn_w_gate_up': out['ffn_w_gate_up'], 'ffn_w_down': out['ffn_w_down'], 'conv_w_in': out['conv_w_in'], 'conv_w': out['conv_w'], 'conv_w_out': out['conv_w_out'], 'kv_norm': out['kv_norm'], 'w_kv': out['w_kv'], 'w_q': out['w_q'], 'w_o': out['w_o'], 'loss_target': out['loss_target'], 'm_mix_norm_pre': out['m_mix_norm_pre'], 'm_mix_norm_post': out['m_mix_norm_post'], 'm_ffn_norm_pre': out['m_ffn_norm_pre'], 'm_ffn_norm_post': out['m_ffn_norm_post'], 'm_ffn_w_gate_up': out['m_ffn_w_gate_up'], 'm_ffn_w_down': out['m_ffn_w_down'], 'm_conv_w_in': out['m_conv_w_in'], 'm_conv_w': out['m_conv_w'], 'm_conv_w_out': out['m_conv_w_out'], 'm_kv_norm': out['m_kv_norm'], 'm_w_kv': out['m_w_kv'], 'm_w_q': out['m_w_q'], 'm_w_o': out['m_w_o'], 'v_mix_norm_pre': out['v_mix_norm_pre'], 'v_mix_norm_post': out['v_mix_norm_post'], 'v_ffn_norm_pre': out['v_ffn_norm_pre'], 'v_ffn_norm_post': out['v_ffn_norm_post'], 'v_ffn_w_gate_up': out['v_ffn_w_gate_up'], 'v_ffn_w_down': out['v_ffn_w_down'], 'v_conv_w_in': out['v_conv_w_in'], 'v_conv_w': out['v_conv_w'], 'v_conv_w_out': out['v_conv_w_out'], 'v_kv_norm': out['v_kv_norm'], 'v_w_kv': out['v_w_kv'], 'v_w_q': out['v_w_q'], 'v_w_o': out['v_w_o']}


def _loss(weights, diff, rest, loss_target):
    with _jax.named_scope("forward"):
        args = {**rest, TWIN_DIFF_INPUT: diff, **{k: w.astype(_WEIGHT_DTYPES[k]) for k, w in weights.items()}}
        y = _forward(args)
    with _jax.named_scope("loss_head"):
        err = _jnp.square(y.astype(_jnp.float32) - loss_target)
        return 0.5 * _jnp.sum(_jnp.mean(err, axis=-1)) if err.ndim else 0.5 * err


def _adamw(w, g, m, v):
    m = ADAM_B1 * m + (1.0 - ADAM_B1) * g
    v = ADAM_B2 * v + (1.0 - ADAM_B2) * _jnp.square(g)
    m_hat = m / (1.0 - ADAM_B1 ** ADAM_STEP)
    v_hat = v / (1.0 - ADAM_B2 ** ADAM_STEP)
    delta = -ADAM_LR * (m_hat / (_jnp.sqrt(v_hat) + ADAM_EPS) + ADAM_WD * w)
    return delta, m, v


def reference(x, positions, mix_norm_pre, mix_norm_post, ffn_norm_pre, ffn_norm_post, ffn_w_gate_up, ffn_w_down, conv_w_in, conv_w, conv_w_out, kv_norm, w_kv, w_q, w_o, loss_target, m_mix_norm_pre, m_mix_norm_post, m_ffn_norm_pre, m_ffn_norm_post, m_ffn_w_gate_up, m_ffn_w_down, m_conv_w_in, m_conv_w, m_conv_w_out, m_kv_norm, m_w_kv, m_w_q, m_w_o, v_mix_norm_pre, v_mix_norm_post, v_ffn_norm_pre, v_ffn_norm_post, v_ffn_w_gate_up, v_ffn_w_down, v_conv_w_in, v_conv_w, v_conv_w_out, v_kv_norm, v_w_kv, v_w_q, v_w_o):
    given = dict(x=x, positions=positions, mix_norm_pre=mix_norm_pre, mix_norm_post=mix_norm_post, ffn_norm_pre=ffn_norm_pre, ffn_norm_post=ffn_norm_post, ffn_w_gate_up=ffn_w_gate_up, ffn_w_down=ffn_w_down, conv_w_in=conv_w_in, conv_w=conv_w, conv_w_out=conv_w_out, kv_norm=kv_norm, w_kv=w_kv, w_q=w_q, w_o=w_o, loss_target=loss_target, m_mix_norm_pre=m_mix_norm_pre, m_mix_norm_post=m_mix_norm_post, m_ffn_norm_pre=m_ffn_norm_pre, m_ffn_norm_post=m_ffn_norm_post, m_ffn_w_gate_up=m_ffn_w_gate_up, m_ffn_w_down=m_ffn_w_down, m_conv_w_in=m_conv_w_in, m_conv_w=m_conv_w, m_conv_w_out=m_conv_w_out, m_kv_norm=m_kv_norm, m_w_kv=m_w_kv, m_w_q=m_w_q, m_w_o=m_w_o, v_mix_norm_pre=v_mix_norm_pre, v_mix_norm_post=v_mix_norm_post, v_ffn_norm_pre=v_ffn_norm_pre, v_ffn_norm_post=v_ffn_norm_post, v_ffn_w_gate_up=v_ffn_w_gate_up, v_ffn_w_down=v_ffn_w_down, v_conv_w_in=v_conv_w_in, v_conv_w=v_conv_w, v_conv_w_out=v_conv_w_out, v_kv_norm=v_kv_norm, v_w_kv=v_w_kv, v_w_q=v_w_q, v_w_o=v_w_o)
    weights = {n: given[n] for n in TWIN_WEIGHTS}
    shared = {n: given[n] for n in SHARED_INPUTS}
    per_example = {n: given[n] for n in ['x', 'positions']}
    grad_fn = _jax.value_and_grad(_loss, argnums=(0, 1))

    def one_microbatch(ex, loss_target):
        ex = dict(ex)
        diff = ex.pop(TWIN_DIFF_INPUT)
        return grad_fn(weights, diff, {**shared, **ex}, loss_target)

    if N_MICROBATCH == 1:
        loss, (grad_w, grad_x) = one_microbatch(per_example, given["loss_target"])
    else:
        def body(carry, xs):
            loss_sum, grad_sum = carry
            l_k, (gw_k, gx_k) = one_microbatch(xs[0], xs[1])
            with _jax.named_scope("update"):
                return (loss_sum + l_k, _jax.tree.map(_jnp.add, grad_sum, gw_k)), gx_k

        init = (_jnp.zeros((), _jnp.float32), _jax.tree.map(_jnp.zeros_like, weights))
        (loss, grad_w), grad_x = _jax.lax.scan(body, init, (per_example, given["loss_target"]))
    with _jax.named_scope("update"):
        delta_w, new_m, new_v = {}, {}, {}
        for n in TWIN_WEIGHTS:
            delta_w[n], new_m[n], new_v[n] = _adamw(weights[n], grad_w[n], given["m_" + n], given["v_" + n])
    return (loss, grad_x, *[grad_w[n] for n in TWIN_WEIGHTS], *[delta_w[n] for n in TWIN_WEIGHTS],
            *[new_m[n] for n in TWIN_WEIGHTS], *[new_v[n] for n in TWIN_WEIGHTS])
```

```python
import functools

import jax
import jax.numpy as jnp
from jax import lax
from jax.experimental import pallas as pl
from jax.experimental.pallas import tpu as pltpu

HEAD_DIM = 64
BAND = 128
BRANCHES = ((128, 1), (512, 4), (2048, 16))
ROPE_THETA = 10000.0
RMS_EPS = 1e-6
NEG_INF = -1e30
ADAM_LR = 0.001
ADAM_B1 = 0.9
ADAM_B2 = 0.999
ADAM_EPS = 1e-08
ADAM_WD = 0.01
ADAM_STEP = 10

N_CHIPS = 4
N_DEV = 8
LANES = 128
ROW_BLOCK = 512
VMEM_LIMIT = 56 * 1024 * 1024
SMALL_ROWS = 16

BF = jnp.bfloat16
F32 = jnp.float32
MESH = pl.DeviceIdType.MESH
ANY = pl.BlockSpec(memory_space=pl.ANY)


def _cp(*sem):
    return pltpu.CompilerParams(dimension_semantics=sem, vmem_limit_bytes=VMEM_LIMIT)


def _rot_half(t, first):
    return jnp.where(first, pltpu.roll(t, 96, 1), pltpu.roll(t, 32, 1))


def _first_half_mask(rows):
    lane = lax.broadcasted_iota(jnp.int32, (rows, LANES), 1)
    return (lane % HEAD_DIM) < (HEAD_DIM // 2)


def _norm_matmul(x, gain, wg, cos2, ss2, *, name, rope_shards, scale, out_dtype):
    T, D = x.shape
    n = wg.shape[2]
    tm = min(ROW_BLOCK, T)

    def body(x_ref, g_ref, w_ref, cos_ref, ss_ref, y_ref, xn_ref, xs):
        j = pl.program_id(1)

        @pl.when(j == 0)
        def _():
            xv = x_ref[...]
            r = lax.rsqrt(jnp.mean(xv * xv, axis=-1, keepdims=True) + RMS_EPS)
            xn = (xv * r * g_ref[...]).astype(BF)
            xs[...] = xn
            xn_ref[...] = xn

        acc = jnp.dot(xs[...], w_ref[...], preferred_element_type=F32)

        def plain():
            y_ref[...] = acc.astype(out_dtype)

        def rope():
            cosv = cos_ref[...]
            ssv = ss_ref[...]
            first = _first_half_mask(tm)
            for ci in range(n // LANES):
                t = acc[:, ci * LANES:(ci + 1) * LANES]
                y = (t * cosv + _rot_half(t, first) * ssv) * scale
                y_ref[:, ci * LANES:(ci + 1) * LANES] = y.astype(out_dtype)

        if rope_shards == 0:
            plain()
        elif rope_shards == N_CHIPS:
            rope()
        else:
            pl.when(j < rope_shards)(rope)
            pl.when(j >= rope_shards)(plain)

    return pl.pallas_call(
        body, name=name,
        grid=(T // tm, N_CHIPS),
        in_specs=[
            pl.BlockSpec((tm, D), lambda i, j: (i, 0)),
            pl.BlockSpec((1, D), lambda i, j: (0, 0)),
            pl.BlockSpec((None, D, n), lambda i, j: (j, 0, 0)),
            pl.BlockSpec((tm, LANES), lambda i, j: (i, 0)),
            pl.BlockSpec((tm, LANES), lambda i, j: (i, 0)),
        ],
        out_specs=[
            pl.BlockSpec((tm, n), lambda i, j: (i, j)),
            pl.BlockSpec((tm, D), lambda i, j: (i, 0)),
        ],
        out_shape=[jax.ShapeDtypeStruct((T, N_CHIPS * n), out_dtype),
                   jax.ShapeDtypeStruct((T, D), BF)],
        scratch_shapes=[pltpu.VMEM((tm, D), BF)],
        compiler_params=_cp("parallel", "arbitrary"),
    )(x, gain, wg, cos2, ss2)


def _norm_swiglu(x, gain, wg, layer, *, name):
    T, D = x.shape
    n = wg.shape[2]
    tm = min(ROW_BLOCK, T)

    def body(x_ref, g_ref, wg_ref, wu_ref, go_ref, uo_ref, ao_ref, xn_ref, xs):
        j = pl.program_id(1)

        @pl.when(j == 0)
        def _():
            xv = x_ref[...]
            r = lax.rsqrt(jnp.mean(xv * xv, axis=-1, keepdims=True) + RMS_EPS)
            xn = (xv * r * g_ref[...]).astype(BF)
            xs[...] = xn
            xn_ref[...] = xn

        g = jnp.dot(xs[...], wg_ref[...], preferred_element_type=F32)
        u = jnp.dot(xs[...], wu_ref[...], preferred_element_type=F32)
        go_ref[...] = g.astype(BF)
        uo_ref[...] = u.astype(BF)
        ao_ref[...] = (g * jax.nn.sigmoid(g) * u).astype(BF)

    half = N_CHIPS // 2
    act = jax.ShapeDtypeStruct((T, half * n), BF)
    return pl.pallas_call(
        body, name=name,
        grid=(T // tm, half),
        in_specs=[
            pl.BlockSpec((tm, D), lambda i, j: (i, 0)),
            pl.BlockSpec((1, D), lambda i, j: (0, 0)),
            pl.BlockSpec((None, D, n), lambda i, j: (j, layer, 0)),
            pl.BlockSpec((None, D, n), lambda i, j: (j + half, layer, 0)),
        ],
        out_specs=[
            pl.BlockSpec((tm, n), lambda i, j: (i, j)),
            pl.BlockSpec((tm, n), lambda i, j: (i, j)),
            pl.BlockSpec((tm, n), lambda i, j: (i, j)),
            pl.BlockSpec((tm, D), lambda i, j: (i, 0)),
        ],
        out_shape=[act, act, act, jax.ShapeDtypeStruct((T, D), BF)],
        scratch_shapes=[pltpu.VMEM((tm, D), BF)],
        compiler_params=_cp("parallel", "arbitrary"),
    )(x, gain, wg, wg)


def _matmul_postnorm(a, w3, widx, gain, h_old, *, name):
    T, K = a.shape
    D = w3.shape[2]
    tm = min(ROW_BLOCK, T)

    def body(a_ref, w_ref, g_ref, h_ref, y_ref, hn_ref):
        y = jnp.dot(a_ref[...].astype(BF), w_ref[...], preferred_element_type=F32)
        y_ref[...] = y.astype(BF)
        r = lax.rsqrt(jnp.mean(y * y, axis=-1, keepdims=True) + RMS_EPS)
        hn_ref[...] = h_ref[...] + y * r * g_ref[...]

    return pl.pallas_call(
        body, name=name,
        grid=(T // tm,),
        in_specs=[
            pl.BlockSpec((tm, K), lambda i: (i, 0)),
            pl.BlockSpec((None, K, D), lambda i: (widx, 0, 0)),
            pl.BlockSpec((1, D), lambda i: (0, 0)),
            pl.BlockSpec((tm, D), lambda i: (i, 0)),
        ],
        out_specs=[pl.BlockSpec((tm, D), lambda i: (i, 0)),
                   pl.BlockSpec((tm, D), lambda i: (i, 0))],
        out_shape=[jax.ShapeDtypeStruct((T, D), BF), jax.ShapeDtypeStruct((T, D), F32)],
        compiler_params=_cp("parallel"),
    )(a, w3, gain, h_old)


def _loss_head(h, target, *, name):
    T, D = h.shape
    tm = min(ROW_BLOCK, T)

    def body(h_ref, t_ref, dh_ref, s_ref):
        i = pl.program_id(0)

        @pl.when(i == 0)
        def _():
            s_ref[...] = jnp.zeros_like(s_ref)

        e = h_ref[...] - t_ref[...]
        dh_ref[...] = e * (1.0 / D)
        s_ref[...] += jnp.sum(e * e)

    return pl.pallas_call(
        body, name=name,
        grid=(T // tm,),
        in_specs=[pl.BlockSpec((tm, D), lambda i: (i, 0)), pl.BlockSpec((tm, D), lambda i: (i, 0))],
        out_specs=[pl.BlockSpec((tm, D), lambda i: (i, 0)), pl.BlockSpec((8, LANES), lambda i: (0, 0))],
        out_shape=[jax.ShapeDtypeStruct((T, D), F32), jax.ShapeDtypeStruct((8, LANES), F32)],
        compiler_params=_cp("arbitrary"),
    )(h, target)


def _shift_down(u, k):
    row = lax.broadcasted_iota(jnp.int32, u.shape, 0)
    return jnp.where(row >= k, pltpu.roll(u, k, 0), 0.0)


def _shift_up(u, k):
    T = u.shape[0]
    row = lax.broadcasted_iota(jnp.int32, u.shape, 0)
    return jnp.where(row < T - k, pltpu.roll(u, T - k, 0), 0.0)


def _conv_fwd(z, cw, *, name):
    T = z.shape[0]
    D = z.shape[1] // 3
    tc = cw.shape[2]
    nb = D // tc

    def body(b_ref, c_ref, h_ref, w_ref, o_ref):
        u = c_ref[...].astype(F32) * h_ref[...].astype(F32)
        w = w_ref[...]
        conv = w[2:3] * u + w[1:2] * _shift_down(u, 1) + w[0:1] * _shift_down(u, 2)
        o_ref[...] = (b_ref[...].astype(F32) * conv).astype(BF)

    return pl.pallas_call(
        body, name=name,
        grid=(nb,),
        in_specs=[
            pl.BlockSpec((T, tc), lambda j: (0, j)),
            pl.BlockSpec((T, tc), lambda j: (0, nb + j)),
            pl.BlockSpec((T, tc), lambda j: (0, 2 * nb + j)),
            pl.BlockSpec((None, 8, tc), lambda j: (j, 0, 0)),
        ],
        out_specs=pl.BlockSpec((T, tc), lambda j: (0, j)),
        out_shape=jax.ShapeDtypeStruct((T, D), BF),
        compiler_params=_cp("parallel"),
    )(z, z, z, cw)


def _conv_bwd(z, cw, dv, *, name):
    T = z.shape[0]
    D = z.shape[1] // 3
    tc = cw.shape[2]
    nb = D // tc

    def body(b_ref, c_ref, h_ref, w_ref, dv_ref, dz_ref, dw_ref):
        p = pl.program_id(0)
        c = c_ref[...].astype(F32)
        h = h_ref[...].astype(F32)
        u = c * h
        u1 = _shift_down(u, 1)
        u2 = _shift_down(u, 2)
        w = w_ref[...]
        dvv = dv_ref[...].astype(F32)
        dconv = dvv * b_ref[...].astype(F32)
        du = w[2:3] * dconv + w[1:2] * _shift_up(dconv, 1) + w[0:1] * _shift_up(dconv, 2)
        rows = lax.broadcasted_iota(jnp.int32, (8, tc), 0)
        dw = jnp.where(rows == 0, jnp.sum(dconv * u2, axis=0, keepdims=True),
                       jnp.where(rows == 1, jnp.sum(dconv * u1, axis=0, keepdims=True),
                                 jnp.where(rows == 2, jnp.sum(dconv * u, axis=0, keepdims=True), 0.0)))
        dw_ref[...] = dw

        @pl.when(p == 0)
        def _():
            conv = w[2:3] * u + w[1:2] * u1 + w[0:1] * u2
            dz_ref[...] = (dvv * conv).astype(BF)

        @pl.when(p == 1)
        def _():
            dz_ref[...] = (du * h).astype(BF)

        @pl.when(p == 2)
        def _():
            dz_ref[...] = (du * c).astype(BF)

    return pl.pallas_call(
        body, name=name,
        grid=(3, nb),
        in_specs=[
            pl.BlockSpec((T, tc), lambda p, j: (0, j)),
            pl.BlockSpec((T, tc), lambda p, j: (0, nb + j)),
            pl.BlockSpec((T, tc), lambda p, j: (0, 2 * nb + j)),
            pl.BlockSpec((None, 8, tc), lambda p, j: (j, 0, 0)),
            pl.BlockSpec((T, tc), lambda p, j: (0, j)),
        ],
        out_specs=[pl.BlockSpec((T, tc), lambda p, j: (0, p * nb + j)),
                   pl.BlockSpec((8, tc), lambda p, j: (p, j))],
        out_shape=[jax.ShapeDtypeStruct((T, 3 * D), BF), jax.ShapeDtypeStruct((3 * 8, D), F32)],
        compiler_params=_cp("arbitrary", "arbitrary"),
    )(z, z, z, cw, dv)


def _rows(r, d):
    return pl.ds(r, BAND, stride=d) if d > 1 else pl.ds(0, BAND)


def _band_mask(n):
    qi = lax.broadcasted_iota(jnp.int32, (2 * BAND, 2 * BAND), 0) % BAND
    kj = lax.broadcasted_iota(jnp.int32, (2 * BAND, 2 * BAND), 1)
    dist = qi + BAND - kj
    return (dist >= 0) & (dist <= BAND) & ((kj >= BAND) | (n > 0))


def _head_mask():
    lane = lax.broadcasted_iota(jnp.int32, (2 * BAND, LANES), 1)
    row = lax.broadcasted_iota(jnp.int32, (2 * BAND, LANES), 0)
    return (lane < HEAD_DIM) == (row < BAND)


def _attn_fwd(q_all, kv_all, g, d, *, name):
    T = q_all.shape[0]
    Dm = q_all.shape[1] // len(BRANCHES)
    HP = Dm // LANES
    rows = BAND * d
    nc = T // rows
    NB = len(BRANCHES)

    def body(q_ref, kp_ref, kc_ref, vp_ref, vc_ref, o_ref, l_ref):
        n = pl.program_id(0)
        allowed = _band_mask(n)
        hm = _head_mask()
        low = lax.broadcasted_iota(jnp.int32, (BAND, LANES), 1) < HEAD_DIM

        def unit(r, carry):
            sl = _rows(r, d)
            q = q_ref[sl, :]
            q2 = jnp.where(hm, jnp.concatenate([q, q], axis=0), 0.0).astype(BF)
            k2 = jnp.concatenate([kp_ref[sl, :], kc_ref[sl, :]], axis=0).astype(BF)
            v2 = jnp.concatenate([vp_ref[sl, :], vc_ref[sl, :]], axis=0).astype(BF)
            s = lax.dot_general(q2, k2, (((1,), (1,)), ((), ())), preferred_element_type=F32)
            s = jnp.where(allowed, s, NEG_INF)
            m = jnp.max(s, axis=-1, keepdims=True)
            p = jnp.exp(s - m)
            l = jnp.sum(p, axis=-1, keepdims=True)
            pv = jnp.dot(p.astype(BF), v2, preferred_element_type=F32) / l
            lse = m + jnp.log(l)
            o_ref[sl, :] = jnp.where(low, pv[:BAND], pv[BAND:])
            l_ref[sl, :] = jnp.where(low, lse[:BAND], lse[BAND:])
            return carry

        lax.fori_loop(0, d, unit, 0)

    blk = (rows, LANES)
    return pl.pallas_call(
        body, name=name,
        grid=(nc, HP),
        in_specs=[
            pl.BlockSpec(blk, lambda n, hp: (n, g * HP + hp)),
            pl.BlockSpec(blk, lambda n, hp: (jnp.maximum(n - 1, 0), g * HP + hp)),
            pl.BlockSpec(blk, lambda n, hp: (n, g * HP + hp)),
            pl.BlockSpec(blk, lambda n, hp: (jnp.maximum(n - 1, 0), (NB + g) * HP + hp)),
            pl.BlockSpec(blk, lambda n, hp: (n, (NB + g) * HP + hp)),
        ],
        out_specs=[pl.BlockSpec(blk, lambda n, hp: (n, hp)), pl.BlockSpec(blk, lambda n, hp: (n, hp))],
        out_shape=[jax.ShapeDtypeStruct((T, Dm), F32), jax.ShapeDtypeStruct((T, Dm), F32)],
        compiler_params=_cp("parallel", "parallel"),
    )(q_all, kv_all, kv_all, kv_all, kv_all)


def _attn_merge(os, ls, *, name):
    T, Dm = os[0].shape
    tm = min(ROW_BLOCK, T)
    nbr = len(os)

    def body(*refs):
        o_refs = refs[:nbr]
        l_refs = refs[nbr:2 * nbr]
        o_out, l_out = refs[2 * nbr:]
        lv = [r[...] for r in l_refs]
        m = functools.reduce(jnp.maximum, lv)
        e = [jnp.exp(v - m) for v in lv]
        tot = functools.reduce(jnp.add, e)
        o = functools.reduce(jnp.add, [(ev / tot) * orf[...] for ev, orf in zip(e, o_refs)])
        o_out[...] = o
        l_out[...] = m + jnp.log(tot)

    spec = pl.BlockSpec((tm, Dm), lambda i: (i, 0))
    return pl.pallas_call(
        body, name=name,
        grid=(T // tm,),
        in_specs=[spec] * (2 * nbr),
        out_specs=[spec, spec],
        out_shape=[jax.ShapeDtypeStruct((T, Dm), F32), jax.ShapeDtypeStruct((T, Dm), F32)],
        compiler_params=_cp("parallel"),
    )(*os, *ls)


def _attn_bwd(q_all, kv_all, do, o, lse, cos2, ss2, g, d, prev, *, name):
    T = q_all.shape[0]
    NB = len(BRANCHES)
    Dm = q_all.shape[1] // NB
    HP = Dm // LANES
    rows = BAND * d
    nc = T // rows
    scale = HEAD_DIM ** -0.5

    def rope_bwd(t, cosv, ssv, first):
        return t * cosv - _rot_half(t, first) * ssv

    def body(q_ref, kp_ref, kc_ref, vp_ref, vc_ref, do_ref, o_ref, l_ref,
             cq_ref, sq_ref, ck_ref, sk_ref, *rest):
        dq_ref, dk_ref, dv_ref, ck_car, cv_car = rest[-5:]
        n = pl.program_id(1)
        first = _first_half_mask(BAND)

        @pl.when(n < nc)
        def _():
            allowed = _band_mask(n)
            hm = _head_mask()
            low = lax.broadcasted_iota(jnp.int32, (BAND, LANES), 1) < HEAD_DIM

            def unit(r, carry):
                sl = _rows(r, d)
                q = q_ref[sl, :]
                dov = do_ref[sl, :]
                ov = o_ref[sl, :]
                lv = l_ref[sl, :]
                q2 = jnp.where(hm, jnp.concatenate([q, q], axis=0), 0.0).astype(BF)
                do2 = jnp.where(hm, jnp.concatenate([dov, dov], axis=0), 0.0)
                oo = dov * ov
                delta = jnp.sum(jnp.where(hm, jnp.concatenate([oo, oo], axis=0), 0.0),
                                axis=-1, keepdims=True)
                lse2 = jnp.concatenate([lv[:, 0:1], lv[:, HEAD_DIM:HEAD_DIM + 1]], axis=0)
                do2 = do2.astype(BF)
                k2 = jnp.concatenate([kp_ref[sl, :], kc_ref[sl, :]], axis=0).astype(BF)
                v2 = jnp.concatenate([vp_ref[sl, :], vc_ref[sl, :]], axis=0).astype(BF)
                s = lax.dot_general(q2, k2, (((1,), (1,)), ((), ())), preferred_element_type=F32)
                p = jnp.where(allowed, jnp.exp(s - lse2), 0.0)
                dp = lax.dot_general(do2, v2, (((1,), (1,)), ((), ())), preferred_element_type=F32)
                ds = (p * (dp - delta)).astype(BF)
                dq2 = jnp.dot(ds, k2, preferred_element_type=F32)
                dq = jnp.where(low, dq2[:BAND], dq2[BAND:])
                dq_ref[sl, :] = rope_bwd(dq, cq_ref[sl, :], sq_ref[sl, :], first) * scale
                dk2 = jnp.dot(ds.T, q2, preferred_element_type=F32)
                dv2 = jnp.dot(p.astype(BF).T, do2, preferred_element_type=F32)

                @pl.when(n > 0)
                def _():
                    dk_ref[sl, :] = rope_bwd(ck_car[sl, :] + dk2[:BAND], ck_ref[sl, :], sk_ref[sl, :], first)
                    dv_ref[sl, :] = cv_car[sl, :] + dv2[:BAND]

                ck_car[sl, :] = dk2[BAND:]
                cv_car[sl, :] = dv2[BAND:]
                return carry

            lax.fori_loop(0, d, unit, 0)

        @pl.when(n == nc)
        def _():
            dk_ref[...] = rope_bwd(ck_car[...], ck_ref[...], sk_ref[...], _first_half_mask(rows))
            dv_ref[...] = cv_car[...]

    blk = (rows, LANES)
    cur = lambda hp, n: jnp.minimum(n, nc - 1)
    prv = lambda hp, n: jnp.clip(n - 1, 0, nc - 1)
    in_specs = [
        pl.BlockSpec(blk, lambda hp, n: (cur(hp, n), g * HP + hp)),
        pl.BlockSpec(blk, lambda hp, n: (prv(hp, n), g * HP + hp)),
        pl.BlockSpec(blk, lambda hp, n: (cur(hp, n), g * HP + hp)),
        pl.BlockSpec(blk, lambda hp, n: (prv(hp, n), (NB + g) * HP + hp)),
        pl.BlockSpec(blk, lambda hp, n: (cur(hp, n), (NB + g) * HP + hp)),
        pl.BlockSpec(blk, lambda hp, n: (cur(hp, n), hp)),
        pl.BlockSpec(blk, lambda hp, n: (cur(hp, n), hp)),
        pl.BlockSpec(blk, lambda hp, n: (cur(hp, n), hp)),
        pl.BlockSpec(blk, lambda hp, n: (cur(hp, n), 0)),
        pl.BlockSpec(blk, lambda hp, n: (cur(hp, n), 0)),
        pl.BlockSpec(blk, lambda hp, n: (prv(hp, n), 0)),
        pl.BlockSpec(blk, lambda hp, n: (prv(hp, n), 0)),
    ]
    args = [q_all, kv_all, kv_all, kv_all, kv_all, do, o, lse, cos2, ss2, cos2, ss2]
    aliases = {}
    if prev is not None:
        in_specs += [ANY, ANY, ANY]
        aliases = {len(args): 0, len(args) + 1: 1, len(args) + 2: 2}
        args += list(prev)
    wide = jax.ShapeDtypeStruct((T, NB * Dm), F32)
    return pl.pallas_call(
        body, name=name,
        grid=(HP, nc + 1),
        in_specs=in_specs,
        out_specs=[
            pl.BlockSpec(blk, lambda hp, n: (cur(hp, n), g * HP + hp)),
            pl.BlockSpec(blk, lambda hp, n: (prv(hp, n), g * HP + hp)),
            pl.BlockSpec(blk, lambda hp, n: (prv(hp, n), g * HP + hp)),
        ],
        out_shape=[wide, wide, wide],
        scratch_shapes=[pltpu.VMEM(blk, F32), pltpu.VMEM(blk, F32)],
        input_output_aliases=aliases,
        compiler_params=_cp("arbitrary", "arbitrary"),
    )(*args)


def _postnorm_bwd(dh, y, g_ref_val):
    r = lax.rsqrt(jnp.mean(y * y, axis=-1, keepdims=True) + RMS_EPS)
    yn = y * r
    dyn = dh * g_ref_val
    dy = r * (dyn - yn * jnp.mean(dyn * yn, axis=-1, keepdims=True))
    return dy, yn


def _postnorm_bwd_matmul(dh, y, gain, w3, widx, *, name, da_dtype):
    T, D = dh.shape
    K = w3.shape[1]
    tm = min(ROW_BLOCK, T)

    def body(dh_ref, y_ref, g_ref, w_ref, dy_ref, da_ref, dg_ref):
        i = pl.program_id(0)

        @pl.when(i == 0)
        def _():
            dg_ref[...] = jnp.zeros_like(dg_ref)

        dhv = dh_ref[...]
        dy, yn = _postnorm_bwd(dhv, y_ref[...].astype(F32), g_ref[...])
        dg_ref[...] += jnp.sum(dhv * yn, axis=0, keepdims=True)
        dyb = dy.astype(BF)
        dy_ref[...] = dyb
        da = lax.dot_general(dyb, w_ref[...], (((1,), (1,)), ((), ())), preferred_element_type=F32)
        da_ref[...] = da.astype(da_dtype)

    return pl.pallas_call(
        body, name=name,
        grid=(T // tm,),
        in_specs=[
            pl.BlockSpec((tm, D), lambda i: (i, 0)),
            pl.BlockSpec((tm, D), lambda i: (i, 0)),
            pl.BlockSpec((1, D), lambda i: (0, 0)),
            pl.BlockSpec((None, K, D), lambda i: (widx, 0, 0)),
        ],
        out_specs=[pl.BlockSpec((tm, D), lambda i: (i, 0)),
                   pl.BlockSpec((tm, K), lambda i: (i, 0)),
                   pl.BlockSpec((1, D), lambda i: (0, 0))],
        out_shape=[jax.ShapeDtypeStruct((T, D), BF), jax.ShapeDtypeStruct((T, K), da_dtype),
                   jax.ShapeDtypeStruct((1, D), F32)],
        compiler_params=_cp("arbitrary"),
    )(dh, y, gain, w3)


def _postnorm_bwd_swiglu(dh, y, gain, wd3, layer, g, u, *, name):
    T, D = dh.shape
    F = wd3.shape[1]
    nf = F // 2
    tm = min(ROW_BLOCK, T)

    def body(dh_ref, y_ref, g_ref, w_ref, gg_ref, uu_ref, dy_ref, dgo_ref, duo_ref, dgain_ref, dys):
        i = pl.program_id(0)
        j = pl.program_id(1)

        @pl.when((i == 0) & (j == 0))
        def _():
            dgain_ref[...] = jnp.zeros_like(dgain_ref)

        @pl.when(j == 0)
        def _():
            dhv = dh_ref[...]
            dy, yn = _postnorm_bwd(dhv, y_ref[...].astype(F32), g_ref[...])
            dgain_ref[...] += jnp.sum(dhv * yn, axis=0, keepdims=True)
            dyb = dy.astype(BF)
            dys[...] = dyb
            dy_ref[...] = dyb

        da = lax.dot_general(dys[...], w_ref[...], (((1,), (1,)), ((), ())), preferred_element_type=F32)
        gv = gg_ref[...].astype(F32)
        uv = uu_ref[...].astype(F32)
        sg = jax.nn.sigmoid(gv)
        silu = gv * sg
        dgo_ref[...] = (da * uv * (sg + silu * (1.0 - sg))).astype(BF)
        duo_ref[...] = (da * silu).astype(BF)

    act = jax.ShapeDtypeStruct((T, F), BF)
    return pl.pallas_call(
        body, name=name,
        grid=(T // tm, 2),
        in_specs=[
            pl.BlockSpec((tm, D), lambda i, j: (i, 0)),
            pl.BlockSpec((tm, D), lambda i, j: (i, 0)),
            pl.BlockSpec((1, D), lambda i, j: (0, 0)),
            pl.BlockSpec((None, nf, D), lambda i, j: (layer, j, 0)),
            pl.BlockSpec((tm, nf), lambda i, j: (i, j)),
            pl.BlockSpec((tm, nf), lambda i, j: (i, j)),
        ],
        out_specs=[pl.BlockSpec((tm, D), lambda i, j: (i, 0)),
                   pl.BlockSpec((tm, nf), lambda i, j: (i, j)),
                   pl.BlockSpec((tm, nf), lambda i, j: (i, j)),
                   pl.BlockSpec((1, D), lambda i, j: (0, 0))],
        out_shape=[jax.ShapeDtypeStruct((T, D), BF), act, act, jax.ShapeDtypeStruct((1, D), F32)],
        scratch_shapes=[pltpu.VMEM((tm, D), BF)],
        compiler_params=_cp("arbitrary", "arbitrary"),
    )(dh, y, gain, wd3, g, u)


def _matmul_prenorm_bwd(dzs, wg, layer, h, gain, dh_in, *, name):
    T, D = h.shape
    n = wg.shape[2]
    tm = min(ROW_BLOCK, T)
    pair = len(dzs) == 2
    nj = N_CHIPS // 2 if pair else N_CHIPS

    def body(*refs):
        dz_refs = refs[:len(dzs)]
        w_refs = refs[len(dzs):2 * len(dzs)]
        h_ref, g_ref, dhi_ref, dh_ref, dg_ref, acc = refs[2 * len(dzs):]
        i = pl.program_id(0)
        j = pl.program_id(1)

        @pl.when((i == 0) & (j == 0))
        def _():
            dg_ref[...] = jnp.zeros_like(dg_ref)

        part = None
        for dz_ref, w_ref in zip(dz_refs, w_refs):
            t = lax.dot_general(dz_ref[...].astype(BF), w_ref[...], (((1,), (1,)), ((), ())),
                                preferred_element_type=F32)
            part = t if part is None else part + t

        @pl.when(j == 0)
        def _():
            acc[...] = part

        @pl.when(j > 0)
        def _():
            acc[...] += part

        @pl.when(j == nj - 1)
        def _():
            dhn = acc[...]
            hv = h_ref[...]
            r = lax.rsqrt(jnp.mean(hv * hv, axis=-1, keepdims=True) + RMS_EPS)
            xh = hv * r
            dg_ref[...] += jnp.sum(dhn * xh, axis=0, keepdims=True)
            dxn = dhn * g_ref[...]
            dh_ref[...] = dhi_ref[...] + r * (dxn - xh * jnp.mean(dxn * xh, axis=-1, keepdims=True))

    in_specs = [pl.BlockSpec((tm, n), lambda i, j: (i, j)) for _ in dzs]
    if pair:
        in_specs += [pl.BlockSpec((None, D, n), lambda i, j: (j, layer, 0)),
                     pl.BlockSpec((None, D, n), lambda i, j: (j + nj, layer, 0))]
    else:
        in_specs += [pl.BlockSpec((None, D, n), lambda i, j: (j, layer, 0))]
    in_specs += [pl.BlockSpec((tm, D), lambda i, j: (i, 0)),
                 pl.BlockSpec((1, D), lambda i, j: (0, 0)),
                 pl.BlockSpec((tm, D), lambda i, j: (i, 0))]
    return pl.pallas_call(
        body, name=name,
        grid=(T // tm, nj),
        in_specs=in_specs,
        out_specs=[pl.BlockSpec((tm, D), lambda i, j: (i, 0)), pl.BlockSpec((1, D), lambda i, j: (0, 0))],
        out_shape=[jax.ShapeDtypeStruct((T, D), F32), jax.ShapeDtypeStruct((1, D), F32)],
        scratch_shapes=[pltpu.VMEM((tm, D), F32)],
        compiler_params=_cp("arbitrary", "arbitrary"),
    )(*dzs, *([wg] * len(dzs)), h, gain, dh_in)


def _grad_matmul(a, b, out_shape3, tme, tne, out_index, prev, *, name):
    T, M = a.shape
    N = b.shape[1]
    tk = min(ROW_BLOCK, T)
    nk = T // tk

    def body(a_ref, b_ref, *rest):
        o_ref, acc = rest[-2:]
        k = pl.program_id(2)
        part = jnp.dot(a_ref[...].astype(BF).T, b_ref[...].astype(BF), preferred_element_type=F32)

        @pl.when(k == 0)
        def _():
            acc[...] = part

        @pl.when(k > 0)
        def _():
            acc[...] += part

        @pl.when(k == nk - 1)
        def _():
            o_ref[...] = acc[...].astype(BF)

    in_specs = [pl.BlockSpec((tk, tme), lambda i, j, k: (k, i)),
                pl.BlockSpec((tk, tne), lambda i, j, k: (k, j))]
    args = [a, b]
    aliases = {}
    if prev is not None:
        in_specs.append(ANY)
        args.append(prev)
        aliases = {2: 0}
    return pl.pallas_call(
        body, name=name,
        grid=(M // tme, N // tne, nk),
        in_specs=in_specs,
        out_specs=pl.BlockSpec((None, tme, tne), lambda i, j, k: out_index(i, j)),
        out_shape=jax.ShapeDtypeStruct(out_shape3, BF),
        scratch_shapes=[pltpu.VMEM((tme, tne), F32)],
        input_output_aliases=aliases,
        compiler_params=_cp("parallel", "parallel", "arbitrary"),
    )(*args)


def _row_tile(R):
    for t in (512, 256, 128, 64, 32, 16, 8):
        if R % t == 0:
            return t
    return R


def _pair_sum(a, b, *, name):
    G, R, C = a.shape
    tr = _row_tile(R)

    def body(a_ref, b_ref, o_ref):
        o_ref[...] = (a_ref[...].astype(F32) + b_ref[...].astype(F32)).astype(BF)

    spec = pl.BlockSpec((None, tr, C), lambda g, i: (g, i, 0))
    return pl.pallas_call(
        body, name=name, grid=(G, R // tr), in_specs=[spec, spec], out_specs=spec,
        out_shape=jax.ShapeDtypeStruct((G, R, C), BF), compiler_params=_cp("parallel", "parallel"),
    )(a, b)


def _chip_sum(parts, *, name):
    G, R, C = parts.shape
    tr = _row_tile(R)

    def body(p_ref, o_ref):
        acc = p_ref[0].astype(F32)
        for gidx in range(1, G):
            acc = acc + p_ref[gidx].astype(F32)
        o_ref[...] = acc

    return pl.pallas_call(
        body, name=name, grid=(R // tr,),
        in_specs=[pl.BlockSpec((G, tr, C), lambda i: (0, i, 0))],
        out_specs=pl.BlockSpec((tr, C), lambda i: (i, 0)),
        out_shape=jax.ShapeDtypeStruct((R, C), F32), compiler_params=_cp("parallel"),
    )(parts)


def _adamw(w, g, m, v, *, name):
    R, C = w.shape
    tr = _row_tile(R)

    def body(w_ref, g_ref, m_ref, v_ref, d_ref, mo_ref, vo_ref):
        gv = g_ref[...]
        mn = ADAM_B1 * m_ref[...] + (1.0 - ADAM_B1) * gv
        vn = ADAM_B2 * v_ref[...] + (1.0 - ADAM_B2) * jnp.square(gv)
        m_hat = mn / (1.0 - ADAM_B1 ** ADAM_STEP)
        v_hat = vn / (1.0 - ADAM_B2 ** ADAM_STEP)
        d_ref[...] = -ADAM_LR * (m_hat / (jnp.sqrt(v_hat) + ADAM_EPS) + ADAM_WD * w_ref[...])
        mo_ref[...] = mn
        vo_ref[...] = vn

    spec = pl.BlockSpec((tr, C), lambda i: (i, 0))
    shp = jax.ShapeDtypeStruct((R, C), F32)
    return pl.pallas_call(
        body, name=name, grid=(R // tr,), in_specs=[spec] * 4, out_specs=[spec] * 3,
        out_shape=[shp, shp, shp], compiler_params=_cp("parallel"),
    )(w, g, m, v)


def _place():
    x = lax.axis_index("x")
    y = lax.axis_index("y")
    c = lax.axis_index("c")
    chips = [(1 - x, y), (x, 1 - y), (1 - x, 1 - y)]
    return x, y, c, chips


def _half_of(ref, j, h, rh, half_major):
    return ref.at[h, j] if half_major else ref.at[j, pl.ds(h * rh, rh)]


def _all_gather_weights(shards, half_major, small):
    nm = len(shards)
    out_shapes = []
    for s, hmaj in zip(shards, half_major):
        R, C = s.shape
        shp = (2, N_CHIPS, R // 2, C) if hmaj else (N_CHIPS, R, C)
        out_shapes.append(jax.ShapeDtypeStruct(shp, s.dtype))
    out_shapes.append(jax.ShapeDtypeStruct((N_CHIPS,) + small.shape, small.dtype))

    def body(*refs):
        ins = refs[:nm]
        sm_in = refs[nm]
        outs = refs[nm + 1:2 * nm + 1]
        sm_out = refs[2 * nm + 1]
        ssem, rsem, fsem, gsem, lsem, sm_s, sm_r, sm_l = refs[2 * nm + 2:]
        x, y, c, chips = _place()
        myj = 2 * x + y
        sib = (x, y, 1 - c)

        def spot(m, j, h):
            return _half_of(outs[m], j, h, shards[m].shape[0] // 2, half_major[m])

        local = []
        first = []
        for m in range(nm):
            rh = shards[m].shape[0] // 2
            for h in range(2):
                cp = pltpu.make_async_copy(ins[m].at[pl.ds(h * rh, rh)], spot(m, myj, h), lsem.at[2 * m + h])
                cp.start()
                local.append(cp)
            for k, (px, py) in enumerate(chips):
                cp = pltpu.make_async_remote_copy(
                    src_ref=ins[m].at[pl.ds(c * rh, rh)], dst_ref=spot(m, myj, c),
                    send_sem=ssem.at[3 * m + k], recv_sem=rsem.at[3 * m + k],
                    device_id=(px, py, c), device_id_type=MESH)
                cp.start()
                first.append(cp)
        cp = pltpu.make_async_copy(sm_in, sm_out.at[myj], sm_l)
        cp.start()
        local.append(cp)
        for k, (px, py) in enumerate(chips):
            cp = pltpu.make_async_remote_copy(
                src_ref=sm_in, dst_ref=sm_out.at[myj], send_sem=sm_s.at[k], recv_sem=sm_r.at[k],
                device_id=(px, py, c), device_id_type=MESH)
            cp.start()
            first.append(cp)

        passed = []
        for m in range(nm):
            for k, (px, py) in enumerate(chips):
                pj = 2 * px + py
                landed = spot(m, pj, c)
                pltpu.make_async_remote_copy(
                    src_ref=landed, dst_ref=landed, send_sem=ssem.at[3 * m + k], recv_sem=rsem.at[3 * m + k],
                    device_id=(px, py, c), device_id_type=MESH).wait_recv()
                cp = pltpu.make_async_remote_copy(
                    src_ref=landed, dst_ref=landed, send_sem=fsem.at[3 * m + k], recv_sem=gsem.at[3 * m + k],
                    device_id=sib, device_id_type=MESH)
                cp.start()
                passed.append(cp)
        for m in range(nm):
            for k, (px, py) in enumerate(chips):
                theirs = spot(m, 2 * px + py, 1 - c)
                pltpu.make_async_remote_copy(
                    src_ref=theirs, dst_ref=theirs, send_sem=fsem.at[3 * m + k], recv_sem=gsem.at[3 * m + k],
                    device_id=sib, device_id_type=MESH).wait_recv()
        for k, (px, py) in enumerate(chips):
            got = sm_out.at[2 * px + py]
            pltpu.make_async_remote_copy(
                src_ref=got, dst_ref=got, send_sem=sm_s.at[k], recv_sem=sm_r.at[k],
                device_id=(px, py, c), device_id_type=MESH).wait_recv()
        for cp in first + passed:
            cp.wait_send()
        for cp in local:
            cp.wait()

    return pl.pallas_call(
        body, name="ag_weights",
        in_specs=[ANY] * (nm + 1),
        out_specs=[ANY] * (nm + 1),
        out_shape=out_shapes,
        scratch_shapes=[pltpu.SemaphoreType.DMA((3 * nm,)), pltpu.SemaphoreType.DMA((3 * nm,)),
                        pltpu.SemaphoreType.DMA((3 * nm,)), pltpu.SemaphoreType.DMA((3 * nm,)),
                        pltpu.SemaphoreType.DMA((2 * nm,)),
                        pltpu.SemaphoreType.DMA((3,)), pltpu.SemaphoreType.DMA((3,)), pltpu.SemaphoreType.DMA],
    )(*shards, small)


def _sibling_swap(dws, half_major):
    nm = len(dws)
    shapes = []
    for dw, hmaj in zip(dws, half_major):
        if hmaj:
            _, G, rh, C = dw.shape
        else:
            G, R, C = dw.shape
            rh = R // 2
        shapes.append(jax.ShapeDtypeStruct((G, rh, C), dw.dtype))

    def body(*refs):
        ins = refs[:nm]
        mine = refs[nm:2 * nm]
        theirs = refs[2 * nm:3 * nm]
        ssem, rsem, lsem = refs[3 * nm:]
        x, y, c, _ = _place()
        sib = (x, y, 1 - c)

        def half(m, h):
            rh = shapes[m].shape[1]
            return ins[m].at[h] if half_major[m] else ins[m].at[:, pl.ds(h * rh, rh), :]

        cps = []
        for m in range(nm):
            lc = pltpu.make_async_copy(half(m, c), mine[m], lsem.at[m])
            lc.start()
            rc = pltpu.make_async_remote_copy(
                src_ref=half(m, 1 - c), dst_ref=theirs[m], send_sem=ssem.at[m], recv_sem=rsem.at[m],
                device_id=sib, device_id_type=MESH)
            rc.start()
            cps.append((lc, rc))
        for lc, rc in cps:
            rc.wait()
            lc.wait()

    outs = pl.pallas_call(
        body, name="rs_sibling_swap",
        in_specs=[ANY] * nm, out_specs=[ANY] * (2 * nm), out_shape=shapes + shapes,
        scratch_shapes=[pltpu.SemaphoreType.DMA((nm,)), pltpu.SemaphoreType.DMA((nm,)),
                        pltpu.SemaphoreType.DMA((nm,))],
    )(*dws)
    return outs[:nm], outs[nm:]


def _chip_exchange(parts):
    nm = len(parts)
    shapes = [jax.ShapeDtypeStruct(p.shape, p.dtype) for p in parts]

    def body(*refs):
        ins = refs[:nm]
        outs = refs[nm:2 * nm]
        ssem, rsem, lsem = refs[2 * nm:]
        x, y, c, chips = _place()
        myj = 2 * x + y
        local, sent = [], []
        for m in range(nm):
            lc = pltpu.make_async_copy(ins[m].at[myj], outs[m].at[myj], lsem.at[m])
            lc.start()
            local.append(lc)
            for k, (px, py) in enumerate(chips):
                rc = pltpu.make_async_remote_copy(
                    src_ref=ins[m].at[2 * px + py], dst_ref=outs[m].at[myj],
                    send_sem=ssem.at[3 * m + k], recv_sem=rsem.at[3 * m + k],
                    device_id=(px, py, c), device_id_type=MESH)
                rc.start()
                sent.append(rc)
        for m in range(nm):
            for k, (px, py) in enumerate(chips):
                got = outs[m].at[2 * px + py]
                pltpu.make_async_remote_copy(
                    src_ref=got, dst_ref=got, send_sem=ssem.at[3 * m + k], recv_sem=rsem.at[3 * m + k],
                    device_id=(px, py, c), device_id_type=MESH).wait_recv()
        for cp in sent:
            cp.wait_send()
        for cp in local:
            cp.wait()

    return pl.pallas_call(
        body, name="rs_chip_exchange",
        in_specs=[ANY] * nm, out_specs=[ANY] * nm, out_shape=shapes,
        scratch_shapes=[pltpu.SemaphoreType.DMA((3 * nm,)), pltpu.SemaphoreType.DMA((3 * nm,)),
                        pltpu.SemaphoreType.DMA((nm,))],
    )(*parts)


def _sibling_join(halves):
    nm = len(halves)
    shapes = [jax.ShapeDtypeStruct((2 * h.shape[0], h.shape[1]), h.dtype) for h in halves]

    def body(*refs):
        ins = refs[:nm]
        outs = refs[nm:2 * nm]
        ssem, rsem, lsem = refs[2 * nm:]
        x, y, c, _ = _place()
        sib = (x, y, 1 - c)
        cps = []
        for m in range(nm):
            rh = halves[m].shape[0]
            lc = pltpu.make_async_copy(ins[m], outs[m].at[pl.ds(c * rh, rh)], lsem.at[m])
            lc.start()
            rc = pltpu.make_async_remote_copy(
                src_ref=ins[m], dst_ref=outs[m].at[pl.ds(c * rh, rh)],
                send_sem=ssem.at[m], recv_sem=rsem.at[m], device_id=sib, device_id_type=MESH)
            rc.start()
            cps.append((lc, rc))
        for m, (lc, rc) in enumerate(cps):
            rh = halves[m].shape[0]
            rc.wait_send()
            got = outs[m].at[pl.ds((1 - c) * rh, rh)]
            pltpu.make_async_remote_copy(
                src_ref=got, dst_ref=got, send_sem=ssem.at[m], recv_sem=rsem.at[m],
                device_id=sib, device_id_type=MESH).wait_recv()
            lc.wait()

    return pl.pallas_call(
        body, name="rs_sibling_join",
        in_specs=[ANY] * nm, out_specs=[ANY] * nm, out_shape=shapes,
        scratch_shapes=[pltpu.SemaphoreType.DMA((nm,)), pltpu.SemaphoreType.DMA((nm,)),
                        pltpu.SemaphoreType.DMA((nm,))],
    )(*halves)


def _all_reduce_small(pack):
    R, C = pack.shape

    def body(in_ref, out_ref, slots, ssem, rsem):
        x, y, c, _ = _place()
        me = 4 * x + 2 * y + c
        slots[me] = in_ref[...]
        cps = []
        for k in range(1, N_DEV):
            dx, dy, dc = (k >> 2) & 1, (k >> 1) & 1, k & 1
            peer = (x ^ dx, y ^ dy, c ^ dc)
            cp = pltpu.make_async_remote_copy(
                src_ref=in_ref, dst_ref=slots.at[me], send_sem=ssem.at[k], recv_sem=rsem.at[k],
                device_id=peer, device_id_type=MESH)
            cp.start()
            cps.append(cp)
        for k in range(1, N_DEV):
            dx, dy, dc = (k >> 2) & 1, (k >> 1) & 1, k & 1
            got = slots.at[4 * (x ^ dx) + 2 * (y ^ dy) + (c ^ dc)]
            pltpu.make_async_remote_copy(
                src_ref=got, dst_ref=got, send_sem=ssem.at[k], recv_sem=rsem.at[k],
                device_id=(x ^ dx, y ^ dy, c ^ dc), device_id_type=MESH).wait_recv()
        for cp in cps:
            cp.wait_send()
        acc = slots[0]
        for s in range(1, N_DEV):
            acc = acc + slots[s]
        out_ref[...] = acc

    return pl.pallas_call(
        body, name="ar_small",
        in_specs=[pl.BlockSpec(memory_space=pltpu.VMEM)],
        out_specs=pl.BlockSpec(memory_space=pltpu.VMEM),
        out_shape=jax.ShapeDtypeStruct((R, C), F32),
        scratch_shapes=[pltpu.VMEM((N_DEV, R, C), F32),
                        pltpu.SemaphoreType.DMA((N_DEV,)), pltpu.SemaphoreType.DMA((N_DEV,))],
    )(pack)


def kernel(x, positions, mix_norm_pre, mix_norm_post, ffn_norm_pre, ffn_norm_post, ffn_w_gate_up, ffn_w_down, conv_w_in, conv_w, conv_w_out, kv_norm, w_kv, w_q, w_o, loss_target, m_mix_norm_pre, m_mix_norm_post, m_ffn_norm_pre, m_ffn_norm_post, m_ffn_w_gate_up, m_ffn_w_down, m_conv_w_in, m_conv_w, m_conv_w_out, m_kv_norm, m_w_kv, m_w_q, m_w_o, v_mix_norm_pre, v_mix_norm_post, v_ffn_norm_pre, v_ffn_norm_post, v_ffn_w_gate_up, v_ffn_w_down, v_conv_w_in, v_conv_w, v_conv_w_out, v_kv_norm, v_w_kv, v_w_q, v_w_o):
    T, D = x.shape[1], x.shape[2]
    L = ffn_w_gate_up.shape[0]
    n_gu = ffn_w_gate_up.shape[2]
    f_sh = ffn_w_down.shape[1]
    F = N_CHIPS * f_sh
    x0 = x[0]
    tgt = loss_target[0]

    half = HEAD_DIM // 2
    inv_freq = ROPE_THETA ** (-jnp.arange(half, dtype=F32) / half)
    ang = positions[0].astype(F32)[:, None] * inv_freq
    cosv, sinv = jnp.cos(ang), jnp.sin(ang)
    cos2 = jnp.tile(cosv, (1, LANES // half))
    ss2 = jnp.tile(jnp.concatenate([-sinv, sinv], axis=1), (1, LANES // HEAD_DIM))

    def as2d(a):
        return a.reshape(-1, a.shape[-1])

    big = [ffn_w_gate_up, ffn_w_down, conv_w_in, conv_w_out, w_kv, w_q, w_o]
    big_m = [m_ffn_w_gate_up, m_ffn_w_down, m_conv_w_in, m_conv_w_out, m_w_kv, m_w_q, m_w_o]
    big_v = [v_ffn_w_gate_up, v_ffn_w_down, v_conv_w_in, v_conv_w_out, v_w_kv, v_w_q, v_w_o]
    half_major = [False, True, False, False, False, False, False]
    shards = [as2d(w).astype(BF) for w in big]
    tc = conv_w.shape[2]
    cw_pad = jnp.concatenate([conv_w[0], jnp.zeros((8 - conv_w.shape[1], tc), F32)], axis=0)

    gathered = _all_gather_weights(shards, half_major, cw_pad)
    wgu, wd4, wci, wco4, wkv, wq, wo4, cw = gathered
    wd = wd4.reshape(L, F, D)
    wco = wco4.reshape(1, D, D)
    wo = wo4.reshape(1, D, D)

    def row(a, i):
        return a[i:i + 1]

    z, hn_m0 = _norm_matmul(x0, row(mix_norm_pre, 0), wci, cos2, ss2, name="f0_conv_in",
                            rope_shards=0, scale=1.0, out_dtype=BF)
    vmix = _conv_fwd(z, cw, name="f0_conv")
    y0, h1 = _matmul_postnorm(vmix, wco, 0, row(mix_norm_post, 0), x0, name="f0_conv_out")
    g0, u0, a0, hn_f0 = _norm_swiglu(h1, row(ffn_norm_pre, 0), wgu, 0, name="f0_gate_up")
    f0, h2 = _matmul_postnorm(a0, wd, 0, row(ffn_norm_post, 0), h1, name="f0_down")

    kv_all, hn_kv = _norm_matmul(h2, kv_norm.reshape(1, D), wkv, cos2, ss2, name="f1_kv",
                                 rope_shards=N_CHIPS // 2, scale=1.0, out_dtype=F32)
    q_all, hn_m1 = _norm_matmul(h2, row(mix_norm_pre, 1), wq, cos2, ss2, name="f1_q",
                                rope_shards=N_CHIPS, scale=HEAD_DIM ** -0.5, out_dtype=F32)
    os_, ls_ = [], []
    for gi, (window, dil) in enumerate(BRANCHES):
        o_g, l_g = _attn_fwd(q_all, kv_all, gi, dil, name=f"f1_attn{gi}")
        os_.append(o_g)
        ls_.append(l_g)
    o_att, lse = _attn_merge(os_, ls_, name="f1_merge")
    y1, h3 = _matmul_postnorm(o_att, wo, 0, row(mix_norm_post, 1), h2, name="f1_attn_out")
    g1, u1, a1, hn_f1 = _norm_swiglu(h3, row(ffn_norm_pre, 1), wgu, 1, name="f1_gate_up")
    f1, h4 = _matmul_postnorm(a1, wd, 1, row(ffn_norm_post, 1), h3, name="f1_down")

    dh4, sq = _loss_head(h4, tgt, name="loss_head")
    loss_part = 0.5 * sq[0, 0] / D

    gu_shape = (N_CHIPS, L * D, n_gu)
    dyf1, dg1, du1, d_ffn_post1 = _postnorm_bwd_swiglu(dh4, f1, row(ffn_norm_post, 1), wd, 1, g1, u1,
                                                       name="b1_down")
    dwd = _grad_matmul(a1, dyf1, (L, F, D), F // 2, D, lambda i, j: (1, i, 0), None, name="b1_dw_down")
    dwgu = _grad_matmul(hn_f1, dg1, gu_shape, D, n_gu, lambda i, j: (j, 1, 0), None, name="b1_dw_gate")
    dwgu = _grad_matmul(hn_f1, du1, gu_shape, D, n_gu, lambda i, j: (j + 2, 1, 0), dwgu, name="b1_dw_up")
    dh3, d_ffn_pre1 = _matmul_prenorm_bwd((dg1, du1), wgu, 1, h3, row(ffn_norm_pre, 1), dh4, name="b1_gate_up")

    dy1, do, d_mix_post1 = _postnorm_bwd_matmul(dh3, y1, row(mix_norm_post, 1), wo, 0, name="b1_attn_out",
                                                da_dtype=F32)
    dwo = _grad_matmul(o_att, dy1, (1, D, D), D, D, lambda i, j: (0, 0, 0), None, name="b1_dw_o")
    prev = None
    for gi, (window, dil) in enumerate(BRANCHES):
        prev = _attn_bwd(q_all, kv_all, do, o_att, lse, cos2, ss2, gi, dil, prev, name=f"b1_attn{gi}")
    dq_all, dk_all, dv_all = prev
    n_q = wq.shape[2]
    n_kv = wkv.shape[2]
    dwq = _grad_matmul(hn_m1, dq_all, (N_CHIPS, D, n_q), D, n_q, lambda i, j: (j, 0, 0), None, name="b1_dw_q")
    dwkv = _grad_matmul(hn_kv, dk_all, (N_CHIPS, D, n_kv), D, n_kv, lambda i, j: (j, 0, 0), None, name="b1_dw_k")
    dwkv = _grad_matmul(hn_kv, dv_all, (N_CHIPS, D, n_kv), D, n_kv, lambda i, j: (j + 2, 0, 0), dwkv, name="b1_dw_v")
    dh2, d_mix_pre1 = _matmul_prenorm_bwd((dq_all,), wq, 0, h2, row(mix_norm_pre, 1), dh3, name="b1_q")
    dh2, d_kv_norm = _matmul_prenorm_bwd((dk_all, dv_all), wkv, 0, h2, kv_norm.reshape(1, D), dh2, name="b1_kv")

    dyf0, dg0, du0, d_ffn_post0 = _postnorm_bwd_swiglu(dh2, f0, row(ffn_norm_post, 0), wd, 0, g0, u0,
                                                       name="b0_down")
    dwd = _grad_matmul(a0, dyf0, (L, F, D), F // 2, D, lambda i, j: (0, i, 0), dwd, name="b0_dw_down")
    dwgu = _grad_matmul(hn_f0, dg0, gu_shape, D, n_gu, lambda i, j: (j, 0, 0), dwgu, name="b0_dw_gate")
    dwgu = _grad_matmul(hn_f0, du0, gu_shape, D, n_gu, lambda i, j: (j + 2, 0, 0), dwgu, name="b0_dw_up")
    dh1, d_ffn_pre0 = _matmul_prenorm_bwd((dg0, du0), wgu, 0, h1, row(ffn_norm_pre, 0), dh2, name="b0_gate_up")

    dy0, dvmix, d_mix_post0 = _postnorm_bwd_matmul(dh1, y0, row(mix_norm_post, 0), wco, 0, name="b0_conv_out",
                                                   da_dtype=BF)
    dwco = _grad_matmul(vmix, dy0, (1, D, D), D, D, lambda i, j: (0, 0, 0), None, name="b0_dw_conv_out")
    dz, dcw = _conv_bwd(z, cw, dvmix, name="b0_conv")
    n_ci = wci.shape[2]
    dwci = _grad_matmul(hn_m0, dz, (N_CHIPS, D, n_ci), D, n_ci, lambda i, j: (j, 0, 0), None, name="b0_dw_conv_in")
    dx, d_mix_pre0 = _matmul_prenorm_bwd((dz,), wci, 0, x0, row(mix_norm_pre, 0), dh1, name="b0_conv_in")

    pack = jnp.concatenate([
        d_mix_pre0, d_mix_pre1, d_mix_post0, d_mix_post1, d_ffn_pre0, d_ffn_pre1, d_ffn_post0, d_ffn_post1,
        d_kv_norm, dcw[0:3], jnp.full((1, D), loss_part, F32),
        jnp.zeros((SMALL_ROWS - 13, D), F32)], axis=0)
    red = _all_reduce_small(pack)
    loss = red[12, 0]
    myj = 2 * lax.axis_index("x") + lax.axis_index("y")
    g_conv_w = lax.dynamic_slice(red, (9, myj * tc), (3, tc))

    zeros7 = jnp.zeros((SMALL_ROWS - 9, D), F32)
    w_small = jnp.concatenate([mix_norm_pre, mix_norm_post, ffn_norm_pre, ffn_norm_post, kv_norm.reshape(1, D), zeros7], axis=0)
    m_small = jnp.concatenate([m_mix_norm_pre, m_mix_norm_post, m_ffn_norm_pre, m_ffn_norm_post, m_kv_norm.reshape(1, D), zeros7], axis=0)
    v_small = jnp.concatenate([v_mix_norm_pre, v_mix_norm_post, v_ffn_norm_pre, v_ffn_norm_post, v_kv_norm.reshape(1, D), zeros7], axis=0)
    d_small, nm_small, nv_small = _adamw(w_small, red, m_small, v_small, name="adamw_small")

    pad5 = jnp.zeros((5, tc), F32)
    d_cw, nm_cw, nv_cw = _adamw(cw_pad, jnp.concatenate([g_conv_w, pad5], axis=0),
                                jnp.concatenate([m_conv_w[0], pad5], axis=0),
                                jnp.concatenate([v_conv_w[0], pad5], axis=0), name="adamw_conv_w")

    dws = [dwgu, dwd.reshape(2, N_CHIPS, f_sh, D), dwci, dwco.reshape(N_CHIPS, D // N_CHIPS, D), dwkv, dwq,
           dwo.reshape(N_CHIPS, D // N_CHIPS, D)]
    mine, theirs = _sibling_swap(dws, half_major)
    parts = [_pair_sum(a, b, name=f"rs_pair_sum{i}") for i, (a, b) in enumerate(zip(mine, theirs))]
    landed = _chip_exchange(parts)
    halves = [_chip_sum(p, name=f"rs_chip_sum{i}") for i, p in enumerate(landed)]
    grads2d = _sibling_join(halves)

    big_out = []
    for i, (w, gr, m, v) in enumerate(zip(big, grads2d, big_m, big_v)):
        d_, m_, v_ = _adamw(as2d(w), gr, as2d(m), as2d(v), name=f"adamw{i}")
        big_out.append((gr.reshape(w.shape), d_.reshape(w.shape), m_.reshape(w.shape), v_.reshape(w.shape)))

    def small(a):
        return (a[0:2], a[2:4], a[4:6], a[6:8])

    def assemble(sm, cwv, kind):
        pre, post, fpre, fpost = small(sm)
        b = [t[kind] for t in big_out]
        return [pre, post, fpre, fpost, b[0], b[1], b[2], cwv[0:3].reshape(conv_w.shape), b[3],
                sm[8], b[4], b[5].reshape(w_q.shape), b[6].reshape(w_o.shape)]

    grads = assemble(red, jnp.concatenate([g_conv_w, pad5], axis=0), 0)
    deltas = assemble(d_small, d_cw, 1)
    new_m = assemble(nm_small, nm_cw, 2)
    new_v = assemble(nv_small, nv_cw, 3)
    return (loss, dx.reshape(x.shape), *grads, *deltas, *new_m, *new_v)
```

```python
import functools

import jax
import jax.numpy as jnp
from jax import lax
from jax.experimental import pallas as pl
from jax.experimental.pallas import tpu as pltpu

HEAD_DIM = 64
BAND = 128
BRANCHES = ((128, 1), (512, 4), (2048, 16))
ROPE_THETA = 10000.0
RMS_EPS = 1e-6
NEG_INF = -1e30
ADAM_LR = 0.001
ADAM_B1 = 0.9
ADAM_B2 = 0.999
ADAM_EPS = 1e-08
ADAM_WD = 0.01
ADAM_STEP = 10

N_CHIPS = 4
N_DEV = 8
LANES = 128
ROW_BLOCK = 512
VMEM_LIMIT = 56 * 1024 * 1024
SMALL_ROWS = 16
DMA_CHUNK_BYTES = 512 * 1024

BF = jnp.bfloat16
F32 = jnp.float32
MESH = pl.DeviceIdType.MESH
ANY = pl.BlockSpec(memory_space=pl.ANY)


def _cp(*sem):
    return pltpu.CompilerParams(dimension_semantics=sem, vmem_limit_bytes=VMEM_LIMIT)


def _rot_half(t, first):
    return jnp.where(first, pltpu.roll(t, 96, 1), pltpu.roll(t, 32, 1))


def _first_half_mask(rows):
    lane = lax.broadcasted_iota(jnp.int32, (rows, LANES), 1)
    return (lane % HEAD_DIM) < (HEAD_DIM // 2)


def _norm_matmul(x, gain, wg, cos2, ss2, *, name, rope_shards, scale, out_dtype):
    T, D = x.shape
    n = wg.shape[2]
    tm = min(ROW_BLOCK, T)

    def body(x_ref, g_ref, w_ref, cos_ref, ss_ref, y_ref, xn_ref, xs):
        j = pl.program_id(1)

        @pl.when(j == 0)
        def _():
            xv = x_ref[...]
            r = lax.rsqrt(jnp.mean(xv * xv, axis=-1, keepdims=True) + RMS_EPS)
            xn = (xv * r * g_ref[...]).astype(BF)
            xs[...] = xn
            xn_ref[...] = xn

        acc = jnp.dot(xs[...], w_ref[...], preferred_element_type=F32)

        def plain():
            y_ref[...] = acc.astype(out_dtype)

        def rope():
            cosv = cos_ref[...]
            ssv = ss_ref[...]
            first = _first_half_mask(tm)
            for ci in range(n // LANES):
                t = acc[:, ci * LANES:(ci + 1) * LANES]
                y = (t * cosv + _rot_half(t, first) * ssv) * scale
                y_ref[:, ci * LANES:(ci + 1) * LANES] = y.astype(out_dtype)

        if rope_shards == 0:
            plain()
        elif rope_shards == N_CHIPS:
            rope()
        else:
            pl.when(j < rope_shards)(rope)
            pl.when(j >= rope_shards)(plain)

    return pl.pallas_call(
        body, name=name,
        grid=(T // tm, N_CHIPS),
        in_specs=[
            pl.BlockSpec((tm, D), lambda i, j: (i, 0)),
            pl.BlockSpec((1, D), lambda i, j: (0, 0)),
            pl.BlockSpec((None, D, n), lambda i, j: (j, 0, 0)),
            pl.BlockSpec((tm, LANES), lambda i, j: (i, 0)),
            pl.BlockSpec((tm, LANES), lambda i, j: (i, 0)),
        ],
        out_specs=[
            pl.BlockSpec((tm, n), lambda i, j: (i, j)),
            pl.BlockSpec((tm, D), lambda i, j: (i, 0)),
        ],
        out_shape=[jax.ShapeDtypeStruct((T, N_CHIPS * n), out_dtype),
                   jax.ShapeDtypeStruct((T, D), BF)],
        scratch_shapes=[pltpu.VMEM((tm, D), BF)],
        compiler_params=_cp("parallel", "arbitrary"),
    )(x, gain, wg, cos2, ss2)


def _norm_swiglu(x, gain, wg, layer, *, name):
    T, D = x.shape
    n = wg.shape[2]
    tm = min(ROW_BLOCK, T)

    def body(x_ref, g_ref, wg_ref, wu_ref, go_ref, uo_ref, ao_ref, xn_ref, xs):
        j = pl.program_id(1)

        @pl.when(j == 0)
        def _():
            xv = x_ref[...]
            r = lax.rsqrt(jnp.mean(xv * xv, axis=-1, keepdims=True) + RMS_EPS)
            xn = (xv * r * g_ref[...]).astype(BF)
            xs[...] = xn
            xn_ref[...] = xn

        g = jnp.dot(xs[...], wg_ref[...], preferred_element_type=F32)
        u = jnp.dot(xs[...], wu_ref[...], preferred_element_type=F32)
        go_ref[...] = g.astype(BF)
        uo_ref[...] = u.astype(BF)
        ao_ref[...] = (g * jax.nn.sigmoid(g) * u).astype(BF)

    half = N_CHIPS // 2
    act = jax.ShapeDtypeStruct((T, half * n), BF)
    return pl.pallas_call(
        body, name=name,
        grid=(T // tm, half),
        in_specs=[
            pl.BlockSpec((tm, D), lambda i, j: (i, 0)),
            pl.BlockSpec((1, D), lambda i, j: (0, 0)),
            pl.BlockSpec((None, D, n), lambda i, j: (j, layer, 0)),
            pl.BlockSpec((None, D, n), lambda i, j: (j + half, layer, 0)),
        ],
        out_specs=[
            pl.BlockSpec((tm, n), lambda i, j: (i, j)),
            pl.BlockSpec((tm, n), lambda i, j: (i, j)),
            pl.BlockSpec((tm, n), lambda i, j: (i, j)),
            pl.BlockSpec((tm, D), lambda i, j: (i, 0)),
        ],
        out_shape=[act, act, act, jax.ShapeDtypeStruct((T, D), BF)],
        scratch_shapes=[pltpu.VMEM((tm, D), BF)],
        compiler_params=_cp("parallel", "arbitrary"),
    )(x, gain, wg, wg)


def _matmul_postnorm(a, w3, widx, gain, h_old, *, name):
    T, K = a.shape
    D = w3.shape[2]
    tm = min(ROW_BLOCK, T)

    def body(a_ref, w_ref, g_ref, h_ref, y_ref, hn_ref):
        y = jnp.dot(a_ref[...].astype(BF), w_ref[...], preferred_element_type=F32)
        y_ref[...] = y.astype(BF)
        r = lax.rsqrt(jnp.mean(y * y, axis=-1, keepdims=True) + RMS_EPS)
        hn_ref[...] = h_ref[...] + y * r * g_ref[...]

    return pl.pallas_call(
        body, name=name,
        grid=(T // tm,),
        in_specs=[
            pl.BlockSpec((tm, K), lambda i: (i, 0)),
            pl.BlockSpec((None, K, D), lambda i: (widx, 0, 0)),
            pl.BlockSpec((1, D), lambda i: (0, 0)),
            pl.BlockSpec((tm, D), lambda i: (i, 0)),
        ],
        out_specs=[pl.BlockSpec((tm, D), lambda i: (i, 0)),
                   pl.BlockSpec((tm, D), lambda i: (i, 0))],
        out_shape=[jax.ShapeDtypeStruct((T, D), BF), jax.ShapeDtypeStruct((T, D), F32)],
        compiler_params=_cp("parallel"),
    )(a, w3, gain, h_old)


def _loss_head(h, target, *, name):
    T, D = h.shape
    tm = min(ROW_BLOCK, T)

    def body(h_ref, t_ref, dh_ref, s_ref):
        i = pl.program_id(0)

        @pl.when(i == 0)
        def _():
            s_ref[...] = jnp.zeros_like(s_ref)

        e = h_ref[...] - t_ref[...]
        dh_ref[...] = e * (1.0 / D)
        s_ref[...] += jnp.sum(e * e)

    return pl.pallas_call(
        body, name=name,
        grid=(T // tm,),
        in_specs=[pl.BlockSpec((tm, D), lambda i: (i, 0)), pl.BlockSpec((tm, D), lambda i: (i, 0))],
        out_specs=[pl.BlockSpec((tm, D), lambda i: (i, 0)), pl.BlockSpec((8, LANES), lambda i: (0, 0))],
        out_shape=[jax.ShapeDtypeStruct((T, D), F32), jax.ShapeDtypeStruct((8, LANES), F32)],
        compiler_params=_cp("arbitrary"),
    )(h, target)


def _shift_down(u, k):
    row = lax.broadcasted_iota(jnp.int32, u.shape, 0)
    return jnp.where(row >= k, pltpu.roll(u, k, 0), 0.0)


def _shift_up(u, k):
    T = u.shape[0]
    row = lax.broadcasted_iota(jnp.int32, u.shape, 0)
    return jnp.where(row < T - k, pltpu.roll(u, T - k, 0), 0.0)


def _conv_fwd(z, cw, *, name):
    T = z.shape[0]
    D = z.shape[1] // 3
    tc = cw.shape[2]
    nb = D // tc

    def body(b_ref, c_ref, h_ref, w_ref, o_ref):
        u = c_ref[...].astype(F32) * h_ref[...].astype(F32)
        w = w_ref[...]
        conv = w[2:3] * u + w[1:2] * _shift_down(u, 1) + w[0:1] * _shift_down(u, 2)
        o_ref[...] = (b_ref[...].astype(F32) * conv).astype(BF)

    return pl.pallas_call(
        body, name=name,
        grid=(nb,),
        in_specs=[
            pl.BlockSpec((T, tc), lambda j: (0, j)),
            pl.BlockSpec((T, tc), lambda j: (0, nb + j)),
            pl.BlockSpec((T, tc), lambda j: (0, 2 * nb + j)),
            pl.BlockSpec((None, 8, tc), lambda j: (j, 0, 0)),
        ],
        out_specs=pl.BlockSpec((T, tc), lambda j: (0, j)),
        out_shape=jax.ShapeDtypeStruct((T, D), BF),
        compiler_params=_cp("parallel"),
    )(z, z, z, cw)


def _conv_bwd(z, cw, dv, *, name):
    T = z.shape[0]
    D = z.shape[1] // 3
    tc = cw.shape[2]
    nb = D // tc

    def body(b_ref, c_ref, h_ref, w_ref, dv_ref, dz_ref, dw_ref):
        p = pl.program_id(0)
        c = c_ref[...].astype(F32)
        h = h_ref[...].astype(F32)
        u = c * h
        u1 = _shift_down(u, 1)
        u2 = _shift_down(u, 2)
        w = w_ref[...]
        dvv = dv_ref[...].astype(F32)
        dconv = dvv * b_ref[...].astype(F32)
        du = w[2:3] * dconv + w[1:2] * _shift_up(dconv, 1) + w[0:1] * _shift_up(dconv, 2)
        rows = lax.broadcasted_iota(jnp.int32, (8, tc), 0)
        dw = jnp.where(rows == 0, jnp.sum(dconv * u2, axis=0, keepdims=True),
                       jnp.where(rows == 1, jnp.sum(dconv * u1, axis=0, keepdims=True),
                                 jnp.where(rows == 2, jnp.sum(dconv * u, axis=0, keepdims=True), 0.0)))
        dw_ref[...] = dw

        @pl.when(p == 0)
        def _():
            conv = w[2:3] * u + w[1:2] * u1 + w[0:1] * u2
            dz_ref[...] = (dvv * conv).astype(BF)

        @pl.when(p == 1)
        def _():
            dz_ref[...] = (du * h).astype(BF)

        @pl.when(p == 2)
        def _():
            dz_ref[...] = (du * c).astype(BF)

    return pl.pallas_call(
        body, name=name,
        grid=(3, nb),
        in_specs=[
            pl.BlockSpec((T, tc), lambda p, j: (0, j)),
            pl.BlockSpec((T, tc), lambda p, j: (0, nb + j)),
            pl.BlockSpec((T, tc), lambda p, j: (0, 2 * nb + j)),
            pl.BlockSpec((None, 8, tc), lambda p, j: (j, 0, 0)),
            pl.BlockSpec((T, tc), lambda p, j: (0, j)),
        ],
        out_specs=[pl.BlockSpec((T, tc), lambda p, j: (0, p * nb + j)),
                   pl.BlockSpec((8, tc), lambda p, j: (p, j))],
        out_shape=[jax.ShapeDtypeStruct((T, 3 * D), BF), jax.ShapeDtypeStruct((3 * 8, D), F32)],
        compiler_params=_cp("arbitrary", "arbitrary"),
    )(z, z, z, cw, dv)


def _rows(r, d):
    return pl.ds(r, BAND, stride=d) if d > 1 else pl.ds(0, BAND)


def _band_mask(n):
    qi = lax.broadcasted_iota(jnp.int32, (2 * BAND, 2 * BAND), 0) % BAND
    kj = lax.broadcasted_iota(jnp.int32, (2 * BAND, 2 * BAND), 1)
    dist = qi + BAND - kj
    return (dist >= 0) & (dist <= BAND) & ((kj >= BAND) | (n > 0))


def _head_mask():
    lane = lax.broadcasted_iota(jnp.int32, (2 * BAND, LANES), 1)
    row = lax.broadcasted_iota(jnp.int32, (2 * BAND, LANES), 0)
    return (lane < HEAD_DIM) == (row < BAND)


def _attn_fwd(q_all, kv_all, g, d, *, name):
    T = q_all.shape[0]
    Dm = q_all.shape[1] // len(BRANCHES)
    HP = Dm // LANES
    rows = BAND * d
    nc = T // rows
    NB = len(BRANCHES)

    def body(q_ref, kp_ref, kc_ref, vp_ref, vc_ref, o_ref, l_ref):
        n = pl.program_id(0)
        allowed = _band_mask(n)
        hm = _head_mask()
        low = lax.broadcasted_iota(jnp.int32, (BAND, LANES), 1) < HEAD_DIM

        def unit(r, carry):
            sl = _rows(r, d)
            q = q_ref[sl, :]
            q2 = jnp.where(hm, jnp.concatenate([q, q], axis=0), 0.0).astype(BF)
            k2 = jnp.concatenate([kp_ref[sl, :], kc_ref[sl, :]], axis=0).astype(BF)
            v2 = jnp.concatenate([vp_ref[sl, :], vc_ref[sl, :]], axis=0).astype(BF)
            s = lax.dot_general(q2, k2, (((1,), (1,)), ((), ())), preferred_element_type=F32)
            s = jnp.where(allowed, s, NEG_INF)
            m = jnp.max(s, axis=-1, keepdims=True)
            p = jnp.exp(s - m)
            l = jnp.sum(p, axis=-1, keepdims=True)
            pv = jnp.dot(p.astype(BF), v2, preferred_element_type=F32) / l
            lse = m + jnp.log(l)
            o_ref[sl, :] = jnp.where(low, pv[:BAND], pv[BAND:])
            l_ref[sl, :] = jnp.where(low, lse[:BAND], lse[BAND:])
            return carry

        lax.fori_loop(0, d, unit, 0)

    blk = (rows, LANES)
    return pl.pallas_call(
        body, name=name,
        grid=(nc, HP),
        in_specs=[
            pl.BlockSpec(blk, lambda n, hp: (n, g * HP + hp)),
            pl.BlockSpec(blk, lambda n, hp: (jnp.maximum(n - 1, 0), g * HP + hp)),
            pl.BlockSpec(blk, lambda n, hp: (n, g * HP + hp)),
            pl.BlockSpec(blk, lambda n, hp: (jnp.maximum(n - 1, 0), (NB + g) * HP + hp)),
            pl.BlockSpec(blk, lambda n, hp: (n, (NB + g) * HP + hp)),
        ],
        out_specs=[pl.BlockSpec(blk, lambda n, hp: (n, hp)), pl.BlockSpec(blk, lambda n, hp: (n, hp))],
        out_shape=[jax.ShapeDtypeStruct((T, Dm), F32), jax.ShapeDtypeStruct((T, Dm), F32)],
        compiler_params=_cp("parallel", "parallel"),
    )(q_all, kv_all, kv_all, kv_all, kv_all)


def _attn_merge(os, ls, *, name):
    T, Dm = os[0].shape
    tm = min(ROW_BLOCK, T)
    nbr = len(os)

    def body(*refs):
        o_refs = refs[:nbr]
        l_refs = refs[nbr:2 * nbr]
        o_out, l_out = refs[2 * nbr:]
        lv = [r[...] for r in l_refs]
        m = functools.reduce(jnp.maximum, lv)
        e = [jnp.exp(v - m) for v in lv]
        tot = functools.reduce(jnp.add, e)
        o = functools.reduce(jnp.add, [(ev / tot) * orf[...] for ev, orf in zip(e, o_refs)])
        o_out[...] = o
        l_out[...] = m + jnp.log(tot)

    spec = pl.BlockSpec((tm, Dm), lambda i: (i, 0))
    return pl.pallas_call(
        body, name=name,
        grid=(T // tm,),
        in_specs=[spec] * (2 * nbr),
        out_specs=[spec, spec],
        out_shape=[jax.ShapeDtypeStruct((T, Dm), F32), jax.ShapeDtypeStruct((T, Dm), F32)],
        compiler_params=_cp("parallel"),
    )(*os, *ls)


def _attn_bwd(q_all, kv_all, do, o, lse, cos2, ss2, g, d, prev, *, name):
    T = q_all.shape[0]
    NB = len(BRANCHES)
    Dm = q_all.shape[1] // NB
    HP = Dm // LANES
    rows = BAND * d
    nc = T // rows
    scale = HEAD_DIM ** -0.5

    def rope_bwd(t, cosv, ssv, first):
        return t * cosv - _rot_half(t, first) * ssv

    def body(q_ref, kp_ref, kc_ref, vp_ref, vc_ref, do_ref, o_ref, l_ref,
             cq_ref, sq_ref, ck_ref, sk_ref, *rest):
        dq_ref, dk_ref, dv_ref, ck_car, cv_car = rest[-5:]
        n = pl.program_id(1)
        first = _first_half_mask(BAND)

        @pl.when(n < nc)
        def _():
            allowed = _band_mask(n)
            hm = _head_mask()
            low = lax.broadcasted_iota(jnp.int32, (BAND, LANES), 1) < HEAD_DIM

            def unit(r, carry):
                sl = _rows(r, d)
                q = q_ref[sl, :]
                dov = do_ref[sl, :]
                ov = o_ref[sl, :]
                lv = l_ref[sl, :]
                q2 = jnp.where(hm, jnp.concatenate([q, q], axis=0), 0.0).astype(BF)
                do2 = jnp.where(hm, jnp.concatenate([dov, dov], axis=0), 0.0)
                oo = dov * ov
                delta = jnp.sum(jnp.where(hm, jnp.concatenate([oo, oo], axis=0), 0.0),
                                axis=-1, keepdims=True)
                lse2 = jnp.concatenate([lv[:, 0:1], lv[:, HEAD_DIM:HEAD_DIM + 1]], axis=0)
                do2 = do2.astype(BF)
                k2 = jnp.concatenate([kp_ref[sl, :], kc_ref[sl, :]], axis=0).astype(BF)
                v2 = jnp.concatenate([vp_ref[sl, :], vc_ref[sl, :]], axis=0).astype(BF)
                s = lax.dot_general(q2, k2, (((1,), (1,)), ((), ())), preferred_element_type=F32)
                p = jnp.where(allowed, jnp.exp(s - lse2), 0.0)
                dp = lax.dot_general(do2, v2, (((1,), (1,)), ((), ())), preferred_element_type=F32)
                ds = (p * (dp - delta)).astype(BF)
                dq2 = jnp.dot(ds, k2, preferred_element_type=F32)
                dq = jnp.where(low, dq2[:BAND], dq2[BAND:])
                dq_ref[sl, :] = rope_bwd(dq, cq_ref[sl, :], sq_ref[sl, :], first) * scale
                dk2 = jnp.dot(ds.T, q2, preferred_element_type=F32)
                dv2 = jnp.dot(p.astype(BF).T, do2, preferred_element_type=F32)

                @pl.when(n > 0)
                def _():
                    dk_ref[sl, :] = rope_bwd(ck_car[sl, :] + dk2[:BAND], ck_ref[sl, :], sk_ref[sl, :], first)
                    dv_ref[sl, :] = cv_car[sl, :] + dv2[:BAND]

                ck_car[sl, :] = dk2[BAND:]
                cv_car[sl, :] = dv2[BAND:]
                return carry

            lax.fori_loop(0, d, unit, 0)

        @pl.when(n == nc)
        def _():
            dk_ref[...] = rope_bwd(ck_car[...], ck_ref[...], sk_ref[...], _first_half_mask(rows))
            dv_ref[...] = cv_car[...]

    blk = (rows, LANES)
    cur = lambda hp, n: jnp.minimum(n, nc - 1)
    prv = lambda hp, n: jnp.clip(n - 1, 0, nc - 1)
    in_specs = [
        pl.BlockSpec(blk, lambda hp, n: (cur(hp, n), g * HP + hp)),
        pl.BlockSpec(blk, lambda hp, n: (prv(hp, n), g * HP + hp)),
        pl.BlockSpec(blk, lambda hp, n: (cur(hp, n), g * HP + hp)),
        pl.BlockSpec(blk, lambda hp, n: (prv(hp, n), (NB + g) * HP + hp)),
        pl.BlockSpec(blk, lambda hp, n: (cur(hp, n), (NB + g) * HP + hp)),
        pl.BlockSpec(blk, lambda hp, n: (cur(hp, n), hp)),
        pl.BlockSpec(blk, lambda hp, n: (cur(hp, n), hp)),
        pl.BlockSpec(blk, lambda hp, n: (cur(hp, n), hp)),
        pl.BlockSpec(blk, lambda hp, n: (cur(hp, n), 0)),
        pl.BlockSpec(blk, lambda hp, n: (cur(hp, n), 0)),
        pl.BlockSpec(blk, lambda hp, n: (prv(hp, n), 0)),
        pl.BlockSpec(blk, lambda hp, n: (prv(hp, n), 0)),
    ]
    args = [q_all, kv_all, kv_all, kv_all, kv_all, do, o, lse, cos2, ss2, cos2, ss2]
    aliases = {}
    if prev is not None:
        in_specs += [ANY, ANY, ANY]
        aliases = {len(args): 0, len(args) + 1: 1, len(args) + 2: 2}
        args += list(prev)
    wide = jax.ShapeDtypeStruct((T, NB * Dm), F32)
    return pl.pallas_call(
        body, name=name,
        grid=(HP, nc + 1),
        in_specs=in_specs,
        out_specs=[
            pl.BlockSpec(blk, lambda hp, n: (cur(hp, n), g * HP + hp)),
            pl.BlockSpec(blk, lambda hp, n: (prv(hp, n), g * HP + hp)),
            pl.BlockSpec(blk, lambda hp, n: (prv(hp, n), g * HP + hp)),
        ],
        out_shape=[wide, wide, wide],
        scratch_shapes=[pltpu.VMEM(blk, F32), pltpu.VMEM(blk, F32)],
        input_output_aliases=aliases,
        compiler_params=_cp("arbitrary", "arbitrary"),
    )(*args)


def _postnorm_bwd(dh, y, g_ref_val):
    r = lax.rsqrt(jnp.mean(y * y, axis=-1, keepdims=True) + RMS_EPS)
    yn = y * r
    dyn = dh * g_ref_val
    dy = r * (dyn - yn * jnp.mean(dyn * yn, axis=-1, keepdims=True))
    return dy, yn


def _postnorm_bwd_matmul(dh, y, gain, w3, widx, *, name, da_dtype):
    T, D = dh.shape
    K = w3.shape[1]
    tm = min(ROW_BLOCK, T)

    def body(dh_ref, y_ref, g_ref, w_ref, dy_ref, da_ref, dg_ref):
        i = pl.program_id(0)

        @pl.when(i == 0)
        def _():
            dg_ref[...] = jnp.zeros_like(dg_ref)

        dhv = dh_ref[...]
        dy, yn = _postnorm_bwd(dhv, y_ref[...].astype(F32), g_ref[...])
        dg_ref[...] += jnp.sum(dhv * yn, axis=0, keepdims=True)
        dyb = dy.astype(BF)
        dy_ref[...] = dyb
        da = lax.dot_general(dyb, w_ref[...], (((1,), (1,)), ((), ())), preferred_element_type=F32)
        da_ref[...] = da.astype(da_dtype)

    return pl.pallas_call(
        body, name=name,
        grid=(T // tm,),
        in_specs=[
            pl.BlockSpec((tm, D), lambda i: (i, 0)),
            pl.BlockSpec((tm, D), lambda i: (i, 0)),
            pl.BlockSpec((1, D), lambda i: (0, 0)),
            pl.BlockSpec((None, K, D), lambda i: (widx, 0, 0)),
        ],
        out_specs=[pl.BlockSpec((tm, D), lambda i: (i, 0)),
                   pl.BlockSpec((tm, K), lambda i: (i, 0)),
                   pl.BlockSpec((1, D), lambda i: (0, 0))],
        out_shape=[jax.ShapeDtypeStruct((T, D), BF), jax.ShapeDtypeStruct((T, K), da_dtype),
                   jax.ShapeDtypeStruct((1, D), F32)],
        compiler_params=_cp("arbitrary"),
    )(dh, y, gain, w3)


def _postnorm_bwd_swiglu(dh, y, gain, wd3, layer, g, u, *, name):
    T, D = dh.shape
    F = wd3.shape[1]
    nf = F // 2
    tm = min(ROW_BLOCK, T)

    def body(dh_ref, y_ref, g_ref, w_ref, gg_ref, uu_ref, dy_ref, dgo_ref, duo_ref, dgain_ref, dys):
        i = pl.program_id(0)
        j = pl.program_id(1)

        @pl.when((i == 0) & (j == 0))
        def _():
            dgain_ref[...] = jnp.zeros_like(dgain_ref)

        @pl.when(j == 0)
        def _():
            dhv = dh_ref[...]
            dy, yn = _postnorm_bwd(dhv, y_ref[...].astype(F32), g_ref[...])
            dgain_ref[...] += jnp.sum(dhv * yn, axis=0, keepdims=True)
            dyb = dy.astype(BF)
            dys[...] = dyb
            dy_ref[...] = dyb

        da = lax.dot_general(dys[...], w_ref[...], (((1,), (1,)), ((), ())), preferred_element_type=F32)
        gv = gg_ref[...].astype(F32)
        uv = uu_ref[...].astype(F32)
        sg = jax.nn.sigmoid(gv)
        silu = gv * sg
        dgo_ref[...] = (da * uv * (sg + silu * (1.0 - sg))).astype(BF)
        duo_ref[...] = (da * silu).astype(BF)

    act = jax.ShapeDtypeStruct((T, F), BF)
    return pl.pallas_call(
        body, name=name,
        grid=(T // tm, 2),
        in_specs=[
            pl.BlockSpec((tm, D), lambda i, j: (i, 0)),
            pl.BlockSpec((tm, D), lambda i, j: (i, 0)),
            pl.BlockSpec((1, D), lambda i, j: (0, 0)),
            pl.BlockSpec((None, nf, D), lambda i, j: (layer, j, 0)),
            pl.BlockSpec((tm, nf), lambda i, j: (i, j)),
            pl.BlockSpec((tm, nf), lambda i, j: (i, j)),
        ],
        out_specs=[pl.BlockSpec((tm, D), lambda i, j: (i, 0)),
                   pl.BlockSpec((tm, nf), lambda i, j: (i, j)),
                   pl.BlockSpec((tm, nf), lambda i, j: (i, j)),
                   pl.BlockSpec((1, D), lambda i, j: (0, 0))],
        out_shape=[jax.ShapeDtypeStruct((T, D), BF), act, act, jax.ShapeDtypeStruct((1, D), F32)],
        scratch_shapes=[pltpu.VMEM((tm, D), BF)],
        compiler_params=_cp("arbitrary", "arbitrary"),
    )(dh, y, gain, wd3, g, u)


def _matmul_prenorm_bwd(dzs, wg, layer, h, gain, dh_in, *, name):
    T, D = h.shape
    n = wg.shape[2]
    tm = min(ROW_BLOCK, T)
    pair = len(dzs) == 2
    nj = N_CHIPS // 2 if pair else N_CHIPS

    def body(*refs):
        dz_refs = refs[:len(dzs)]
        w_refs = refs[len(dzs):2 * len(dzs)]
        h_ref, g_ref, dhi_ref, dh_ref, dg_ref, acc = refs[2 * len(dzs):]
        i = pl.program_id(0)
        j = pl.program_id(1)

        @pl.when((i == 0) & (j == 0))
        def _():
            dg_ref[...] = jnp.zeros_like(dg_ref)

        part = None
        for dz_ref, w_ref in zip(dz_refs, w_refs):
            t = lax.dot_general(dz_ref[...].astype(BF), w_ref[...], (((1,), (1,)), ((), ())),
                                preferred_element_type=F32)
            part = t if part is None else part + t

        @pl.when(j == 0)
        def _():
            acc[...] = part

        @pl.when(j > 0)
        def _():
            acc[...] += part

        @pl.when(j == nj - 1)
        def _():
            dhn = acc[...]
            hv = h_ref[...]
            r = lax.rsqrt(jnp.mean(hv * hv, axis=-1, keepdims=True) + RMS_EPS)
            xh = hv * r
            dg_ref[...] += jnp.sum(dhn * xh, axis=0, keepdims=True)
            dxn = dhn * g_ref[...]
            dh_ref[...] = dhi_ref[...] + r * (dxn - xh * jnp.mean(dxn * xh, axis=-1, keepdims=True))

    in_specs = [pl.BlockSpec((tm, n), lambda i, j: (i, j)) for _ in dzs]
    if pair:
        in_specs += [pl.BlockSpec((None, D, n), lambda i, j: (j, layer, 0)),
                     pl.BlockSpec((None, D, n), lambda i, j: (j + nj, layer, 0))]
    else:
        in_specs += [pl.BlockSpec((None, D, n), lambda i, j: (j, layer, 0))]
    in_specs += [pl.BlockSpec((tm, D), lambda i, j: (i, 0)),
                 pl.BlockSpec((1, D), lambda i, j: (0, 0)),
                 pl.BlockSpec((tm, D), lambda i, j: (i, 0))]
    return pl.pallas_call(
        body, name=name,
        grid=(T // tm, nj),
        in_specs=in_specs,
        out_specs=[pl.BlockSpec((tm, D), lambda i, j: (i, 0)), pl.BlockSpec((1, D), lambda i, j: (0, 0))],
        out_shape=[jax.ShapeDtypeStruct((T, D), F32), jax.ShapeDtypeStruct((1, D), F32)],
        scratch_shapes=[pltpu.VMEM((tm, D), F32)],
        compiler_params=_cp("arbitrary", "arbitrary"),
    )(*dzs, *([wg] * len(dzs)), h, gain, dh_in)


def _grad_matmul(a, b, out_shape3, tme, tne, out_index, prev, *, name):
    T, M = a.shape
    N = b.shape[1]
    tk = min(ROW_BLOCK, T)
    nk = T // tk

    def body(a_ref, b_ref, *rest):
        o_ref, acc = rest[-2:]
        k = pl.program_id(2)
        part = jnp.dot(a_ref[...].astype(BF).T, b_ref[...].astype(BF), preferred_element_type=F32)

        @pl.when(k == 0)
        def _():
            acc[...] = part

        @pl.when(k > 0)
        def _():
            acc[...] += part

        @pl.when(k == nk - 1)
        def _():
            o_ref[...] = acc[...].astype(BF)

    in_specs = [pl.BlockSpec((tk, tme), lambda i, j, k: (k, i)),
                pl.BlockSpec((tk, tne), lambda i, j, k: (k, j))]
    args = [a, b]
    aliases = {}
    if prev is not None:
        in_specs.append(ANY)
        args.append(prev)
        aliases = {2: 0}
    return pl.pallas_call(
        body, name=name,
        grid=(M // tme, N // tne, nk),
        in_specs=in_specs,
        out_specs=pl.BlockSpec((None, tme, tne), lambda i, j, k: out_index(i, j)),
        out_shape=jax.ShapeDtypeStruct(out_shape3, BF),
        scratch_shapes=[pltpu.VMEM((tme, tne), F32)],
        input_output_aliases=aliases,
        compiler_params=_cp("parallel", "parallel", "arbitrary"),
    )(*args)


def _row_tile(R):
    for t in (512, 256, 128, 64, 32, 16, 8):
        if R % t == 0:
            return t
    return R


def _pair_sum(a, b, *, name):
    G, R, C = a.shape
    tr = _row_tile(R)

    def body(a_ref, b_ref, o_ref):
        o_ref[...] = (a_ref[...].astype(F32) + b_ref[...].astype(F32)).astype(BF)

    spec = pl.BlockSpec((None, tr, C), lambda g, i: (g, i, 0))
    return pl.pallas_call(
        body, name=name, grid=(G, R // tr), in_specs=[spec, spec], out_specs=spec,
        out_shape=jax.ShapeDtypeStruct((G, R, C), BF), compiler_params=_cp("parallel", "parallel"),
    )(a, b)


def _chip_sum(parts, *, name):
    G, R, C = parts.shape
    tr = _row_tile(R)

    def body(p_ref, o_ref):
        acc = p_ref[0].astype(F32)
        for gidx in range(1, G):
            acc = acc + p_ref[gidx].astype(F32)
        o_ref[...] = acc

    return pl.pallas_call(
        body, name=name, grid=(R // tr,),
        in_specs=[pl.BlockSpec((G, tr, C), lambda i: (0, i, 0))],
        out_specs=pl.BlockSpec((tr, C), lambda i: (i, 0)),
        out_shape=jax.ShapeDtypeStruct((R, C), F32), compiler_params=_cp("parallel"),
    )(parts)


def _adamw(w, g, m, v, *, name):
    R, C = w.shape
    tr = _row_tile(R)

    def body(w_ref, g_ref, m_ref, v_ref, d_ref, mo_ref, vo_ref):
        gv = g_ref[...]
        mn = ADAM_B1 * m_ref[...] + (1.0 - ADAM_B1) * gv
        vn = ADAM_B2 * v_ref[...] + (1.0 - ADAM_B2) * jnp.square(gv)
        m_hat = mn / (1.0 - ADAM_B1 ** ADAM_STEP)
        v_hat = vn / (1.0 - ADAM_B2 ** ADAM_STEP)
        d_ref[...] = -ADAM_LR * (m_hat / (jnp.sqrt(v_hat) + ADAM_EPS) + ADAM_WD * w_ref[...])
        mo_ref[...] = mn
        vo_ref[...] = vn

    spec = pl.BlockSpec((tr, C), lambda i: (i, 0))
    shp = jax.ShapeDtypeStruct((R, C), F32)
    return pl.pallas_call(
        body, name=name, grid=(R // tr,), in_specs=[spec] * 4, out_specs=[spec] * 3,
        out_shape=[shp, shp, shp], compiler_params=_cp("parallel"),
    )(w, g, m, v)


def _place():
    x = lax.axis_index("x")
    y = lax.axis_index("y")
    c = lax.axis_index("c")
    chips = [(1 - x, y), (x, 1 - y), (1 - x, 1 - y)]
    return x, y, c, chips


def _half_of(ref, j, h, rh, half_major, r0=0, rc=None):
    rc = rh if rc is None else rc
    return ref.at[h, j, pl.ds(r0, rc)] if half_major else ref.at[j, pl.ds(h * rh + r0, rc)]


def _chunk_rows(rows, row_bytes, align):
    if rows <= align:
        return rows
    cands = [r for r in range(align, rows + 1, align) if rows % r == 0]
    fit = [r for r in cands if r * row_bytes <= DMA_CHUNK_BYTES]
    return max(fit) if fit else min(cands)


def _row_align(dtype):
    return 8 * (4 // jnp.dtype(dtype).itemsize)


def _start_chunks(make, rows, rc):
    for r0 in range(0, rows, rc):
        make(r0, rc).start()


def _all_gather_weights(shards, half_major, small):
    nm = len(shards)
    out_shapes = []
    for s, hmaj in zip(shards, half_major):
        R, C = s.shape
        shp = (2, N_CHIPS, R // 2, C) if hmaj else (N_CHIPS, R, C)
        out_shapes.append(jax.ShapeDtypeStruct(shp, s.dtype))
    out_shapes.append(jax.ShapeDtypeStruct((N_CHIPS,) + small.shape, small.dtype))

    def body(*refs):
        ins = refs[:nm]
        sm_in = refs[nm]
        outs = refs[nm + 1:2 * nm + 1]
        sm_out = refs[2 * nm + 1]
        ssem, rsem, fsem, gsem, lsem, sm_s, sm_r, sm_l = refs[2 * nm + 2:]
        x, y, c, chips = _place()
        myj = 2 * x + y
        sib = (x, y, 1 - c)

        def spot(m, j, h, r0=0, rc=None):
            return _half_of(outs[m], j, h, shards[m].shape[0] // 2, half_major[m], r0, rc)

        def chunk(m):
            rh, cols = shards[m].shape[0] // 2, shards[m].shape[1]
            return rh, _chunk_rows(rh, cols * shards[m].dtype.itemsize, _row_align(shards[m].dtype))

        local = []
        first = []
        for m in range(nm):
            rh, rc = chunk(m)
            for h in range(2):
                _start_chunks(lambda r0, n, m=m, h=h: pltpu.make_async_copy(
                    ins[m].at[pl.ds(h * rh + r0, n)], spot(m, myj, h, r0, n), lsem.at[2 * m + h]), rh, rc)
                local.append(pltpu.make_async_copy(ins[m].at[pl.ds(h * rh, rh)], spot(m, myj, h), lsem.at[2 * m + h]))
            for k, (px, py) in enumerate(chips):
                def send(r0, n, m=m, k=k, px=px, py=py, rh=rh):
                    return pltpu.make_async_remote_copy(
                        src_ref=ins[m].at[pl.ds(c * rh + r0, n)], dst_ref=spot(m, myj, c, r0, n),
                        send_sem=ssem.at[3 * m + k], recv_sem=rsem.at[3 * m + k],
                        device_id=(px, py, c), device_id_type=MESH)
                _start_chunks(send, rh, rc)
                first.append(send(0, rh))
        cp = pltpu.make_async_copy(sm_in, sm_out.at[myj], sm_l)
        cp.start()
        local.append(cp)
        for k, (px, py) in enumerate(chips):
            cp = pltpu.make_async_remote_copy(
                src_ref=sm_in, dst_ref=sm_out.at[myj], send_sem=sm_s.at[k], recv_sem=sm_r.at[k],
                device_id=(px, py, c), device_id_type=MESH)
            cp.start()
            first.append(cp)

        passed = []
        for m in range(nm):
            rh, rc = chunk(m)
            for k, (px, py) in enumerate(chips):
                pj = 2 * px + py
                landed = spot(m, pj, c)
                pltpu.make_async_remote_copy(
                    src_ref=landed, dst_ref=landed, send_sem=ssem.at[3 * m + k], recv_sem=rsem.at[3 * m + k],
                    device_id=(px, py, c), device_id_type=MESH).wait_recv()

                def fwd(r0, n, m=m, k=k, pj=pj):
                    part = spot(m, pj, c, r0, n)
                    return pltpu.make_async_remote_copy(
                        src_ref=part, dst_ref=part, send_sem=fsem.at[3 * m + k], recv_sem=gsem.at[3 * m + k],
                        device_id=sib, device_id_type=MESH)
                _start_chunks(fwd, rh, rc)
                passed.append(fwd(0, rh))
        for m in range(nm):
            for k, (px, py) in enumerate(chips):
                theirs = spot(m, 2 * px + py, 1 - c)
                pltpu.make_async_remote_copy(
                    src_ref=theirs, dst_ref=theirs, send_sem=fsem.at[3 * m + k], recv_sem=gsem.at[3 * m + k],
                    device_id=sib, device_id_type=MESH).wait_recv()
        for k, (px, py) in enumerate(chips):
            got = sm_out.at[2 * px + py]
            pltpu.make_async_remote_copy(
                src_ref=got, dst_ref=got, send_sem=sm_s.at[k], recv_sem=sm_r.at[k],
                device_id=(px, py, c), device_id_type=MESH).wait_recv()
        for cp in first + passed:
            cp.wait_send()
        for cp in local:
            cp.wait()

    return pl.pallas_call(
        body, name="ag_weights",
        in_specs=[ANY] * (nm + 1),
        out_specs=[ANY] * (nm + 1),
        out_shape=out_shapes,
        scratch_shapes=[pltpu.SemaphoreType.DMA((3 * nm,)), pltpu.SemaphoreType.DMA((3 * nm,)),
                        pltpu.SemaphoreType.DMA((3 * nm,)), pltpu.SemaphoreType.DMA((3 * nm,)),
                        pltpu.SemaphoreType.DMA((2 * nm,)),
                        pltpu.SemaphoreType.DMA((3,)), pltpu.SemaphoreType.DMA((3,)), pltpu.SemaphoreType.DMA],
    )(*shards, small)


def _sibling_swap(dws, half_major):
    nm = len(dws)
    shapes = []
    for dw, hmaj in zip(dws, half_major):
        if hmaj:
            _, G, rh, C = dw.shape
        else:
            G, R, C = dw.shape
            rh = R // 2
        shapes.append(jax.ShapeDtypeStruct((G, rh, C), dw.dtype))

    def body(*refs):
        ins = refs[:nm]
        mine = refs[nm:2 * nm]
        theirs = refs[2 * nm:3 * nm]
        ssem, rsem, lsem = refs[3 * nm:]
        x, y, c, _ = _place()
        sib = (x, y, 1 - c)

        def half(m, h):
            rh = shapes[m].shape[1]
            return ins[m].at[h] if half_major[m] else ins[m].at[:, pl.ds(h * rh, rh), :]

        cps = []
        for m in range(nm):
            G, rh, cols = shapes[m].shape
            rc = _chunk_rows(rh, cols * shapes[m].dtype.itemsize, _row_align(shapes[m].dtype))
            for j in range(G):
                _start_chunks(lambda r0, n, m=m, j=j, rh=rh: pltpu.make_async_copy(
                    _half_of(ins[m], j, c, rh, half_major[m], r0, n), mine[m].at[j, pl.ds(r0, n)],
                    lsem.at[m]), rh, rc)
                _start_chunks(lambda r0, n, m=m, j=j, rh=rh: pltpu.make_async_remote_copy(
                    src_ref=_half_of(ins[m], j, 1 - c, rh, half_major[m], r0, n),
                    dst_ref=theirs[m].at[j, pl.ds(r0, n)], send_sem=ssem.at[m], recv_sem=rsem.at[m],
                    device_id=sib, device_id_type=MESH), rh, rc)
            lc = pltpu.make_async_copy(half(m, c), mine[m], lsem.at[m])
            rcp = pltpu.make_async_remote_copy(
                src_ref=half(m, 1 - c), dst_ref=theirs[m], send_sem=ssem.at[m], recv_sem=rsem.at[m],
                device_id=sib, device_id_type=MESH)
            cps.append((lc, rcp))
        for lc, rcp in cps:
            rcp.wait()
            lc.wait()

    outs = pl.pallas_call(
        body, name="rs_sibling_swap",
        in_specs=[ANY] * nm, out_specs=[ANY] * (2 * nm), out_shape=shapes + shapes,
        scratch_shapes=[pltpu.SemaphoreType.DMA((nm,)), pltpu.SemaphoreType.DMA((nm,)),
                        pltpu.SemaphoreType.DMA((nm,))],
    )(*dws)
    return outs[:nm], outs[nm:]


def _chip_exchange(parts):
    nm = len(parts)
    shapes = [jax.ShapeDtypeStruct(p.shape, p.dtype) for p in parts]

    def body(*refs):
        ins = refs[:nm]
        outs = refs[nm:2 * nm]
        ssem, rsem, lsem = refs[2 * nm:]
        x, y, c, chips = _place()
        myj = 2 * x + y
        local, sent = [], []
        for m in range(nm):
            _, rh, cols = shapes[m].shape
            rc = _chunk_rows(rh, cols * shapes[m].dtype.itemsize, _row_align(shapes[m].dtype))
            _start_chunks(lambda r0, n, m=m: pltpu.make_async_copy(
                ins[m].at[myj, pl.ds(r0, n)], outs[m].at[myj, pl.ds(r0, n)], lsem.at[m]), rh, rc)
            local.append(pltpu.make_async_copy(ins[m].at[myj], outs[m].at[myj], lsem.at[m]))
            for k, (px, py) in enumerate(chips):
                def send(r0, n, m=m, k=k, px=px, py=py):
                    return pltpu.make_async_remote_copy(
                        src_ref=ins[m].at[2 * px + py, pl.ds(r0, n)], dst_ref=outs[m].at[myj, pl.ds(r0, n)],
                        send_sem=ssem.at[3 * m + k], recv_sem=rsem.at[3 * m + k],
                        device_id=(px, py, c), device_id_type=MESH)
                _start_chunks(send, rh, rc)
                sent.append(send(0, rh))
        for m in range(nm):
            for k, (px, py) in enumerate(chips):
                got = outs[m].at[2 * px + py]
                pltpu.make_async_remote_copy(
                    src_ref=got, dst_ref=got, send_sem=ssem.at[3 * m + k], recv_sem=rsem.at[3 * m + k],
                    device_id=(px, py, c), device_id_type=MESH).wait_recv()
        for cp in sent:
            cp.wait_send()
        for cp in local:
            cp.wait()

    return pl.pallas_call(
        body, name="rs_chip_exchange",
        in_specs=[ANY] * nm, out_specs=[ANY] * nm, out_shape=shapes,
        scratch_shapes=[pltpu.SemaphoreType.DMA((3 * nm,)), pltpu.SemaphoreType.DMA((3 * nm,)),
                        pltpu.SemaphoreType.DMA((nm,))],
    )(*parts)


def _sibling_join(halves):
    nm = len(halves)
    shapes = [jax.ShapeDtypeStruct((2 * h.shape[0], h.shape[1]), h.dtype) for h in halves]

    def body(*refs):
        ins = refs[:nm]
        outs = refs[nm:2 * nm]
        ssem, rsem, lsem = refs[2 * nm:]
        x, y, c, _ = _place()
        sib = (x, y, 1 - c)
        cps = []
        for m in range(nm):
            rh, cols = halves[m].shape
            rc = _chunk_rows(rh, cols * halves[m].dtype.itemsize, _row_align(halves[m].dtype))
            _start_chunks(lambda r0, n, m=m, rh=rh: pltpu.make_async_copy(
                ins[m].at[pl.ds(r0, n)], outs[m].at[pl.ds(c * rh + r0, n)], lsem.at[m]), rh, rc)
            _start_chunks(lambda r0, n, m=m, rh=rh: pltpu.make_async_remote_copy(
                src_ref=ins[m].at[pl.ds(r0, n)], dst_ref=outs[m].at[pl.ds(c * rh + r0, n)],
                send_sem=ssem.at[m], recv_sem=rsem.at[m], device_id=sib, device_id_type=MESH), rh, rc)
            lc = pltpu.make_async_copy(ins[m], outs[m].at[pl.ds(c * rh, rh)], lsem.at[m])
            whole = pltpu.make_async_remote_copy(
                src_ref=ins[m], dst_ref=outs[m].at[pl.ds(c * rh, rh)],
                send_sem=ssem.at[m], recv_sem=rsem.at[m], device_id=sib, device_id_type=MESH)
            cps.append((lc, whole))
        for m, (lc, rc) in enumerate(cps):
            rh = halves[m].shape[0]
            rc.wait_send()
            got = outs[m].at[pl.ds((1 - c) * rh, rh)]
            pltpu.make_async_remote_copy(
                src_ref=got, dst_ref=got, send_sem=ssem.at[m], recv_sem=rsem.at[m],
                device_id=sib, device_id_type=MESH).wait_recv()
            lc.wait()

    return pl.pallas_call(
        body, name="rs_sibling_join",
        in_specs=[ANY] * nm, out_specs=[ANY] * nm, out_shape=shapes,
        scratch_shapes=[pltpu.SemaphoreType.DMA((nm,)), pltpu.SemaphoreType.DMA((nm,)),
                        pltpu.SemaphoreType.DMA((nm,))],
    )(*halves)


def _all_reduce_small(pack):
    R, C = pack.shape

    def body(in_ref, out_ref, slots, ssem, rsem):
        x, y, c, _ = _place()
        me = 4 * x + 2 * y + c
        slots[me] = in_ref[...]
        cps = []
        for k in range(1, N_DEV):
            dx, dy, dc = (k >> 2) & 1, (k >> 1) & 1, k & 1
            peer = (x ^ dx, y ^ dy, c ^ dc)
            cp = pltpu.make_async_remote_copy(
                src_ref=in_ref, dst_ref=slots.at[me], send_sem=ssem.at[k], recv_sem=rsem.at[k],
                device_id=peer, device_id_type=MESH)
            cp.start()
            cps.append(cp)
        for k in range(1, N_DEV):
            dx, dy, dc = (k >> 2) & 1, (k >> 1) & 1, k & 1
            got = slots.at[4 * (x ^ dx) + 2 * (y ^ dy) + (c ^ dc)]
            pltpu.make_async_remote_copy(
                src_ref=got, dst_ref=got, send_sem=ssem.at[k], recv_sem=rsem.at[k],
                device_id=(x ^ dx, y ^ dy, c ^ dc), device_id_type=MESH).wait_recv()
        for cp in cps:
            cp.wait_send()
        acc = slots[0]
        for s in range(1, N_DEV):
            acc = acc + slots[s]
        out_ref[...] = acc

    return pl.pallas_call(
        body, name="ar_small",
        in_specs=[pl.BlockSpec(memory_space=pltpu.VMEM)],
        out_specs=pl.BlockSpec(memory_space=pltpu.VMEM),
        out_shape=jax.ShapeDtypeStruct((R, C), F32),
        scratch_shapes=[pltpu.VMEM((N_DEV, R, C), F32),
                        pltpu.SemaphoreType.DMA((N_DEV,)), pltpu.SemaphoreType.DMA((N_DEV,))],
    )(pack)


def kernel(x, positions, mix_norm_pre, mix_norm_post, ffn_norm_pre, ffn_norm_post, ffn_w_gate_up, ffn_w_down, conv_w_in, conv_w, conv_w_out, kv_norm, w_kv, w_q, w_o, loss_target, m_mix_norm_pre, m_mix_norm_post, m_ffn_norm_pre, m_ffn_norm_post, m_ffn_w_gate_up, m_ffn_w_down, m_conv_w_in, m_conv_w, m_conv_w_out, m_kv_norm, m_w_kv, m_w_q, m_w_o, v_mix_norm_pre, v_mix_norm_post, v_ffn_norm_pre, v_ffn_norm_post, v_ffn_w_gate_up, v_ffn_w_down, v_conv_w_in, v_conv_w, v_conv_w_out, v_kv_norm, v_w_kv, v_w_q, v_w_o):
    T, D = x.shape[1], x.shape[2]
    L = ffn_w_gate_up.shape[0]
    n_gu = ffn_w_gate_up.shape[2]
    f_sh = ffn_w_down.shape[1]
    F = N_CHIPS * f_sh
    x0 = x[0]
    tgt = loss_target[0]

    half = HEAD_DIM // 2
    inv_freq = ROPE_THETA ** (-jnp.arange(half, dtype=F32) / half)
    ang = positions[0].astype(F32)[:, None] * inv_freq
    cosv, sinv = jnp.cos(ang), jnp.sin(ang)
    cos2 = jnp.tile(cosv, (1, LANES // half))
    ss2 = jnp.tile(jnp.concatenate([-sinv, sinv], axis=1), (1, LANES // HEAD_DIM))

    def as2d(a):
        return a.reshape(-1, a.shape[-1])

    big = [ffn_w_gate_up, ffn_w_down, conv_w_in, conv_w_out, w_kv, w_q, w_o]
    big_m = [m_ffn_w_gate_up, m_ffn_w_down, m_conv_w_in, m_conv_w_out, m_w_kv, m_w_q, m_w_o]
    big_v = [v_ffn_w_gate_up, v_ffn_w_down, v_conv_w_in, v_conv_w_out, v_w_kv, v_w_q, v_w_o]
    half_major = [False, True, False, False, False, False, False]
    shards = [as2d(w).astype(BF) for w in big]
    tc = conv_w.shape[2]
    cw_pad = jnp.concatenate([conv_w[0], jnp.zeros((8 - conv_w.shape[1], tc), F32)], axis=0)

    gathered = _all_gather_weights(shards, half_major, cw_pad)
    wgu, wd4, wci, wco4, wkv, wq, wo4, cw = gathered
    wd = wd4.reshape(L, F, D)
    wco = wco4.reshape(1, D, D)
    wo = wo4.reshape(1, D, D)

    def row(a, i):
        return a[i:i + 1]

    z, hn_m0 = _norm_matmul(x0, row(mix_norm_pre, 0), wci, cos2, ss2, name="f0_conv_in",
                            rope_shards=0, scale=1.0, out_dtype=BF)
    vmix = _conv_fwd(z, cw, name="f0_conv")
    y0, h1 = _matmul_postnorm(vmix, wco, 0, row(mix_norm_post, 0), x0, name="f0_conv_out")
    g0, u0, a0, hn_f0 = _norm_swiglu(h1, row(ffn_norm_pre, 0), wgu, 0, name="f0_gate_up")
    f0, h2 = _matmul_postnorm(a0, wd, 0, row(ffn_norm_post, 0), h1, name="f0_down")

    kv_all, hn_kv = _norm_matmul(h2, kv_norm.reshape(1, D), wkv, cos2, ss2, name="f1_kv",
                                 rope_shards=N_CHIPS // 2, scale=1.0, out_dtype=F32)
    q_all, hn_m1 = _norm_matmul(h2, row(mix_norm_pre, 1), wq, cos2, ss2, name="f1_q",
                                rope_shards=N_CHIPS, scale=HEAD_DIM ** -0.5, out_dtype=F32)
    os_, ls_ = [], []
    for gi, (window, dil) in enumerate(BRANCHES):
        o_g, l_g = _attn_fwd(q_all, kv_all, gi, dil, name=f"f1_attn{gi}")
        os_.append(o_g)
        ls_.append(l_g)
    o_att, lse = _attn_merge(os_, ls_, name="f1_merge")
    y1, h3 = _matmul_postnorm(o_att, wo, 0, row(mix_norm_post, 1), h2, name="f1_attn_out")
    g1, u1, a1, hn_f1 = _norm_swiglu(h3, row(ffn_norm_pre, 1), wgu, 1, name="f1_gate_up")
    f1, h4 = _matmul_postnorm(a1, wd, 1, row(ffn_norm_post, 1), h3, name="f1_down")

    dh4, sq = _loss_head(h4, tgt, name="loss_head")
    loss_part = 0.5 * sq[0, 0] / D

    gu_shape = (N_CHIPS, L * D, n_gu)
    dyf1, dg1, du1, d_ffn_post1 = _postnorm_bwd_swiglu(dh4, f1, row(ffn_norm_post, 1), wd, 1, g1, u1,
                                                       name="b1_down")
    dwd = _grad_matmul(a1, dyf1, (L, F, D), F // 2, D, lambda i, j: (1, i, 0), None, name="b1_dw_down")
    dwgu = _grad_matmul(hn_f1, dg1, gu_shape, D, n_gu, lambda i, j: (j, 1, 0), None, name="b1_dw_gate")
    dwgu = _grad_matmul(hn_f1, du1, gu_shape, D, n_gu, lambda i, j: (j + 2, 1, 0), dwgu, name="b1_dw_up")
    dh3, d_ffn_pre1 = _matmul_prenorm_bwd((dg1, du1), wgu, 1, h3, row(ffn_norm_pre, 1), dh4, name="b1_gate_up")

    dy1, do, d_mix_post1 = _postnorm_bwd_matmul(dh3, y1, row(mix_norm_post, 1), wo, 0, name="b1_attn_out",
                                                da_dtype=F32)
    dwo = _grad_matmul(o_att, dy1, (1, D, D), D, D, lambda i, j: (0, 0, 0), None, name="b1_dw_o")
    prev = None
    for gi, (window, dil) in enumerate(BRANCHES):
        prev = _attn_bwd(q_all, kv_all, do, o_att, lse, cos2, ss2, gi, dil, prev, name=f"b1_attn{gi}")
    dq_all, dk_all, dv_all = prev
    n_q = wq.shape[2]
    n_kv = wkv.shape[2]
    dwq = _grad_matmul(hn_m1, dq_all, (N_CHIPS, D, n_q), D, n_q, lambda i, j: (j, 0, 0), None, name="b1_dw_q")
    dwkv = _grad_matmul(hn_kv, dk_all, (N_CHIPS, D, n_kv), D, n_kv, lambda i, j: (j, 0, 0), None, name="b1_dw_k")
    dwkv = _grad_matmul(hn_kv, dv_all, (N_CHIPS, D, n_kv), D, n_kv, lambda i, j: (j + 2, 0, 0), dwkv, name="b1_dw_v")
    dh2, d_mix_pre1 = _matmul_prenorm_bwd((dq_all,), wq, 0, h2, row(mix_norm_pre, 1), dh3, name="b1_q")
    dh2, d_kv_norm = _matmul_prenorm_bwd((dk_all, dv_all), wkv, 0, h2, kv_norm.reshape(1, D), dh2, name="b1_kv")

    dyf0, dg0, du0, d_ffn_post0 = _postnorm_bwd_swiglu(dh2, f0, row(ffn_norm_post, 0), wd, 0, g0, u0,
                                                       name="b0_down")
    dwd = _grad_matmul(a0, dyf0, (L, F, D), F // 2, D, lambda i, j: (0, i, 0), dwd, name="b0_dw_down")
    dwgu = _grad_matmul(hn_f0, dg0, gu_shape, D, n_gu, lambda i, j: (j, 0, 0), dwgu, name="b0_dw_gate")
    dwgu = _grad_matmul(hn_f0, du0, gu_shape, D, n_gu, lambda i, j: (j + 2, 0, 0), dwgu, name="b0_dw_up")
    dh1, d_ffn_pre0 = _matmul_prenorm_bwd((dg0, du0), wgu, 0, h1, row(ffn_norm_pre, 0), dh2, name="b0_gate_up")

    dy0, dvmix, d_mix_post0 = _postnorm_bwd_matmul(dh1, y0, row(mix_norm_post, 0), wco, 0, name="b0_conv_out",
                                                   da_dtype=BF)
    dwco = _grad_matmul(vmix, dy0, (1, D, D), D, D, lambda i, j: (0, 0, 0), None, name="b0_dw_conv_out")
    dz, dcw = _conv_bwd(z, cw, dvmix, name="b0_conv")
    n_ci = wci.shape[2]
    dwci = _grad_matmul(hn_m0, dz, (N_CHIPS, D, n_ci), D, n_ci, lambda i, j: (j, 0, 0), None, name="b0_dw_conv_in")
    dx, d_mix_pre0 = _matmul_prenorm_bwd((dz,), wci, 0, x0, row(mix_norm_pre, 0), dh1, name="b0_conv_in")

    pack = jnp.concatenate([
        d_mix_pre0, d_mix_pre1, d_mix_post0, d_mix_post1, d_ffn_pre0, d_ffn_pre1, d_ffn_post0, d_ffn_post1,
        d_kv_norm, dcw[0:3], jnp.full((1, D), loss_part, F32),
        jnp.zeros((SMALL_ROWS - 13, D), F32)], axis=0)
    red = _all_reduce_small(pack)
    loss = red[12, 0]
    myj = 2 * lax.axis_index("x") + lax.axis_index("y")
    g_conv_w = lax.dynamic_slice(red, (9, myj * tc), (3, tc))

    zeros7 = jnp.zeros((SMALL_ROWS - 9, D), F32)
    w_small = jnp.concatenate([mix_norm_pre, mix_norm_post, ffn_norm_pre, ffn_norm_post, kv_norm.reshape(1, D), zeros7], axis=0)
    m_small = jnp.concatenate([m_mix_norm_pre, m_mix_norm_post, m_ffn_norm_pre, m_ffn_norm_post, m_kv_norm.reshape(1, D), zeros7], axis=0)
    v_small = jnp.concatenate([v_mix_norm_pre, v_mix_norm_post, v_ffn_norm_pre, v_ffn_norm_post, v_kv_norm.reshape(1, D), zeros7], axis=0)
    d_small, nm_small, nv_small = _adamw(w_small, red, m_small, v_small, name="adamw_small")

    pad5 = jnp.zeros((5, tc), F32)
    d_cw, nm_cw, nv_cw = _adamw(cw_pad, jnp.concatenate([g_conv_w, pad5], axis=0),
                                jnp.concatenate([m_conv_w[0], pad5], axis=0),
                                jnp.concatenate([v_conv_w[0], pad5], axis=0), name="adamw_conv_w")

    dws = [dwgu, dwd.reshape(2, N_CHIPS, f_sh, D), dwci, dwco.reshape(N_CHIPS, D // N_CHIPS, D), dwkv, dwq,
           dwo.reshape(N_CHIPS, D // N_CHIPS, D)]
    mine, theirs = _sibling_swap(dws, half_major)
    parts = [_pair_sum(a, b, name=f"rs_pair_sum{i}") for i, (a, b) in enumerate(zip(mine, theirs))]
    landed = _chip_exchange(parts)
    halves = [_chip_sum(p, name=f"rs_chip_sum{i}") for i, p in enumerate(landed)]
    grads2d = _sibling_join(halves)

    big_out = []
    for i, (w, gr, m, v) in enumerate(zip(big, grads2d, big_m, big_v)):
        d_, m_, v_ = _adamw(as2d(w), gr, as2d(m), as2d(v), name=f"adamw{i}")
        big_out.append((gr.reshape(w.shape), d_.reshape(w.shape), m_.reshape(w.shape), v_.reshape(w.shape)))

    def small(a):
        return (a[0:2], a[2:4], a[4:6], a[6:8])

    def assemble(sm, cwv, kind):
        pre, post, fpre, fpost = small(sm)
        b = [t[kind] for t in big_out]
        return [pre, post, fpre, fpost, b[0], b[1], b[2], cwv[0:3].reshape(conv_w.shape), b[3],
                sm[8], b[4], b[5].reshape(w_q.shape), b[6].reshape(w_o.shape)]

    grads = assemble(red, jnp.concatenate([g_conv_w, pad5], axis=0), 0)
    deltas = assemble(d_small, d_cw, 1)
    new_m = assemble(nm_small, nm_cw, 2)
    new_v = assemble(nv_small, nv_cw, 3)
    return (loss, dx.reshape(x.shape), *grads, *deltas, *new_m, *new_v)
```

```python
import functools

import jax
import jax.numpy as jnp
from jax import lax
from jax.experimental import pallas as pl
from jax.experimental.pallas import tpu as pltpu

HEAD_DIM = 64
BAND = 128
BRANCHES = ((128, 1), (512, 4), (2048, 16))
ROPE_THETA = 10000.0
RMS_EPS = 1e-6
NEG_INF = -1e30
ADAM_LR = 0.001
ADAM_B1 = 0.9
ADAM_B2 = 0.999
ADAM_EPS = 1e-08
ADAM_WD = 0.01
ADAM_STEP = 10

N_CHIPS = 4
N_DEV = 8
LANES = 128
ROW_BLOCK = 512
VMEM_LIMIT = 56 * 1024 * 1024
SMALL_ROWS = 16
DMA_CHUNK_BYTES = 512 * 1024

BF = jnp.bfloat16
F32 = jnp.float32
MESH = pl.DeviceIdType.MESH
ANY = pl.BlockSpec(memory_space=pl.ANY)


def _cp(*sem):
    return pltpu.CompilerParams(dimension_semantics=sem, vmem_limit_bytes=VMEM_LIMIT)


def _rot_half(t, first):
    return jnp.where(first, pltpu.roll(t, 96, 1), pltpu.roll(t, 32, 1))


def _first_half_mask(rows):
    lane = lax.broadcasted_iota(jnp.int32, (rows, LANES), 1)
    return (lane % HEAD_DIM) < (HEAD_DIM // 2)


def _norm_matmul(x, gain, wg, cos2, ss2, *, name, rope_shards, scale, out_dtype):
    T, D = x.shape
    n = wg.shape[2]
    tm = min(ROW_BLOCK, T)

    def body(x_ref, g_ref, w_ref, cos_ref, ss_ref, y_ref, xn_ref, xs):
        j = pl.program_id(1)

        @pl.when(j == 0)
        def _():
            xv = x_ref[...]
            r = lax.rsqrt(jnp.mean(xv * xv, axis=-1, keepdims=True) + RMS_EPS)
            xn = (xv * r * g_ref[...]).astype(BF)
            xs[...] = xn
            xn_ref[...] = xn

        acc = jnp.dot(xs[...], w_ref[...], preferred_element_type=F32)

        def plain():
            y_ref[...] = acc.astype(out_dtype)

        def rope():
            cosv = cos_ref[...]
            ssv = ss_ref[...]
            first = _first_half_mask(tm)
            for ci in range(n // LANES):
                t = acc[:, ci * LANES:(ci + 1) * LANES]
                y = (t * cosv + _rot_half(t, first) * ssv) * scale
                y_ref[:, ci * LANES:(ci + 1) * LANES] = y.astype(out_dtype)

        if rope_shards == 0:
            plain()
        elif rope_shards == N_CHIPS:
            rope()
        else:
            pl.when(j < rope_shards)(rope)
            pl.when(j >= rope_shards)(plain)

    return pl.pallas_call(
        body, name=name,
        grid=(T // tm, N_CHIPS),
        in_specs=[
            pl.BlockSpec((tm, D), lambda i, j: (i, 0)),
            pl.BlockSpec((1, D), lambda i, j: (0, 0)),
            pl.BlockSpec((None, D, n), lambda i, j: (j, 0, 0)),
            pl.BlockSpec((tm, LANES), lambda i, j: (i, 0)),
            pl.BlockSpec((tm, LANES), lambda i, j: (i, 0)),
        ],
        out_specs=[
            pl.BlockSpec((tm, n), lambda i, j: (i, j)),
            pl.BlockSpec((tm, D), lambda i, j: (i, 0)),
        ],
        out_shape=[jax.ShapeDtypeStruct((T, N_CHIPS * n), out_dtype),
                   jax.ShapeDtypeStruct((T, D), BF)],
        scratch_shapes=[pltpu.VMEM((tm, D), BF)],
        compiler_params=_cp("parallel", "arbitrary"),
    )(x, gain, wg, cos2, ss2)


def _norm_swiglu(x, gain, wg, layer, *, name):
    T, D = x.shape
    n = wg.shape[2]
    tm = min(ROW_BLOCK, T)

    def body(x_ref, g_ref, wg_ref, wu_ref, go_ref, uo_ref, ao_ref, xn_ref, xs):
        j = pl.program_id(1)

        @pl.when(j == 0)
        def _():
            xv = x_ref[...]
            r = lax.rsqrt(jnp.mean(xv * xv, axis=-1, keepdims=True) + RMS_EPS)
            xn = (xv * r * g_ref[...]).astype(BF)
            xs[...] = xn
            xn_ref[...] = xn

        g = jnp.dot(xs[...], wg_ref[...], preferred_element_type=F32)
        u = jnp.dot(xs[...], wu_ref[...], preferred_element_type=F32)
        go_ref[...] = g.astype(BF)
        uo_ref[...] = u.astype(BF)
        ao_ref[...] = (g * jax.nn.sigmoid(g) * u).astype(BF)

    half = N_CHIPS // 2
    act = jax.ShapeDtypeStruct((T, half * n), BF)
    return pl.pallas_call(
        body, name=name,
        grid=(T // tm, half),
        in_specs=[
            pl.BlockSpec((tm, D), lambda i, j: (i, 0)),
            pl.BlockSpec((1, D), lambda i, j: (0, 0)),
            pl.BlockSpec((None, D, n), lambda i, j: (j, layer, 0)),
            pl.BlockSpec((None, D, n), lambda i, j: (j + half, layer, 0)),
        ],
        out_specs=[
            pl.BlockSpec((tm, n), lambda i, j: (i, j)),
            pl.BlockSpec((tm, n), lambda i, j: (i, j)),
            pl.BlockSpec((tm, n), lambda i, j: (i, j)),
            pl.BlockSpec((tm, D), lambda i, j: (i, 0)),
        ],
        out_shape=[act, act, act, jax.ShapeDtypeStruct((T, D), BF)],
        scratch_shapes=[pltpu.VMEM((tm, D), BF)],
        compiler_params=_cp("parallel", "arbitrary"),
    )(x, gain, wg, wg)


def _matmul_postnorm(a, w3, widx, gain, h_old, *, name):
    T, K = a.shape
    D = w3.shape[2]
    tm = min(ROW_BLOCK, T)

    def body(a_ref, w_ref, g_ref, h_ref, y_ref, hn_ref):
        y = jnp.dot(a_ref[...].astype(BF), w_ref[...], preferred_element_type=F32)
        y_ref[...] = y.astype(BF)
        r = lax.rsqrt(jnp.mean(y * y, axis=-1, keepdims=True) + RMS_EPS)
        hn_ref[...] = h_ref[...] + y * r * g_ref[...]

    return pl.pallas_call(
        body, name=name,
        grid=(T // tm,),
        in_specs=[
            pl.BlockSpec((tm, K), lambda i: (i, 0)),
            pl.BlockSpec((None, K, D), lambda i: (widx, 0, 0)),
            pl.BlockSpec((1, D), lambda i: (0, 0)),
            pl.BlockSpec((tm, D), lambda i: (i, 0)),
        ],
        out_specs=[pl.BlockSpec((tm, D), lambda i: (i, 0)),
                   pl.BlockSpec((tm, D), lambda i: (i, 0))],
        out_shape=[jax.ShapeDtypeStruct((T, D), BF), jax.ShapeDtypeStruct((T, D), F32)],
        compiler_params=_cp("parallel"),
    )(a, w3, gain, h_old)


def _loss_head(h, target, *, name):
    T, D = h.shape
    tm = min(ROW_BLOCK, T)

    def body(h_ref, t_ref, dh_ref, s_ref):
        i = pl.program_id(0)

        @pl.when(i == 0)
        def _():
            s_ref[...] = jnp.zeros_like(s_ref)

        e = h_ref[...] - t_ref[...]
        dh_ref[...] = e * (1.0 / D)
        s_ref[...] += jnp.sum(e * e)

    return pl.pallas_call(
        body, name=name,
        grid=(T // tm,),
        in_specs=[pl.BlockSpec((tm, D), lambda i: (i, 0)), pl.BlockSpec((tm, D), lambda i: (i, 0))],
        out_specs=[pl.BlockSpec((tm, D), lambda i: (i, 0)), pl.BlockSpec((8, LANES), lambda i: (0, 0))],
        out_shape=[jax.ShapeDtypeStruct((T, D), F32), jax.ShapeDtypeStruct((8, LANES), F32)],
        compiler_params=_cp("arbitrary"),
    )(h, target)


def _shift_down(u, k):
    row = lax.broadcasted_iota(jnp.int32, u.shape, 0)
    return jnp.where(row >= k, pltpu.roll(u, k, 0), 0.0)


def _shift_up(u, k):
    T = u.shape[0]
    row = lax.broadcasted_iota(jnp.int32, u.shape, 0)
    return jnp.where(row < T - k, pltpu.roll(u, T - k, 0), 0.0)


def _conv_fwd(z, cw, *, name):
    T = z.shape[0]
    D = z.shape[1] // 3
    tc = cw.shape[2]
    nb = D // tc

    def body(b_ref, c_ref, h_ref, w_ref, o_ref):
        u = c_ref[...].astype(F32) * h_ref[...].astype(F32)
        w = w_ref[...]
        conv = w[2:3] * u + w[1:2] * _shift_down(u, 1) + w[0:1] * _shift_down(u, 2)
        o_ref[...] = (b_ref[...].astype(F32) * conv).astype(BF)

    return pl.pallas_call(
        body, name=name,
        grid=(nb,),
        in_specs=[
            pl.BlockSpec((T, tc), lambda j: (0, j)),
            pl.BlockSpec((T, tc), lambda j: (0, nb + j)),
            pl.BlockSpec((T, tc), lambda j: (0, 2 * nb + j)),
            pl.BlockSpec((None, 8, tc), lambda j: (j, 0, 0)),
        ],
        out_specs=pl.BlockSpec((T, tc), lambda j: (0, j)),
        out_shape=jax.ShapeDtypeStruct((T, D), BF),
        compiler_params=_cp("parallel"),
    )(z, z, z, cw)


def _conv_bwd(z, cw, dv, *, name):
    T = z.shape[0]
    D = z.shape[1] // 3
    tc = cw.shape[2]
    nb = D // tc

    def body(b_ref, c_ref, h_ref, w_ref, dv_ref, dz_ref, dw_ref):
        p = pl.program_id(0)
        c = c_ref[...].astype(F32)
        h = h_ref[...].astype(F32)
        u = c * h
        u1 = _shift_down(u, 1)
        u2 = _shift_down(u, 2)
        w = w_ref[...]
        dvv = dv_ref[...].astype(F32)
        dconv = dvv * b_ref[...].astype(F32)
        du = w[2:3] * dconv + w[1:2] * _shift_up(dconv, 1) + w[0:1] * _shift_up(dconv, 2)
        rows = lax.broadcasted_iota(jnp.int32, (8, tc), 0)
        dw = jnp.where(rows == 0, jnp.sum(dconv * u2, axis=0, keepdims=True),
                       jnp.where(rows == 1, jnp.sum(dconv * u1, axis=0, keepdims=True),
                                 jnp.where(rows == 2, jnp.sum(dconv * u, axis=0, keepdims=True), 0.0)))
        dw_ref[...] = dw

        @pl.when(p == 0)
        def _():
            conv = w[2:3] * u + w[1:2] * u1 + w[0:1] * u2
            dz_ref[...] = (dvv * conv).astype(BF)

        @pl.when(p == 1)
        def _():
            dz_ref[...] = (du * h).astype(BF)

        @pl.when(p == 2)
        def _():
            dz_ref[...] = (du * c).astype(BF)

    return pl.pallas_call(
        body, name=name,
        grid=(3, nb),
        in_specs=[
            pl.BlockSpec((T, tc), lambda p, j: (0, j)),
            pl.BlockSpec((T, tc), lambda p, j: (0, nb + j)),
            pl.BlockSpec((T, tc), lambda p, j: (0, 2 * nb + j)),
            pl.BlockSpec((None, 8, tc), lambda p, j: (j, 0, 0)),
            pl.BlockSpec((T, tc), lambda p, j: (0, j)),
        ],
        out_specs=[pl.BlockSpec((T, tc), lambda p, j: (0, p * nb + j)),
                   pl.BlockSpec((8, tc), lambda p, j: (p, j))],
        out_shape=[jax.ShapeDtypeStruct((T, 3 * D), BF), jax.ShapeDtypeStruct((3 * 8, D), F32)],
        compiler_params=_cp("arbitrary", "arbitrary"),
    )(z, z, z, cw, dv)


def _rows(r, d):
    return pl.ds(r, BAND, stride=d) if d > 1 else pl.ds(0, BAND)


def _band_mask(n):
    qi = lax.broadcasted_iota(jnp.int32, (2 * BAND, 2 * BAND), 0) % BAND
    kj = lax.broadcasted_iota(jnp.int32, (2 * BAND, 2 * BAND), 1)
    dist = qi + BAND - kj
    return (dist >= 0) & (dist <= BAND) & ((kj >= BAND) | (n > 0))


def _head_mask():
    lane = lax.broadcasted_iota(jnp.int32, (2 * BAND, LANES), 1)
    row = lax.broadcasted_iota(jnp.int32, (2 * BAND, LANES), 0)
    return (lane < HEAD_DIM) == (row < BAND)


def _attn_fwd(q_all, kv_all, g, d, *, name):
    T = q_all.shape[0]
    Dm = q_all.shape[1] // len(BRANCHES)
    HP = Dm // LANES
    rows = BAND * d
    nc = T // rows
    NB = len(BRANCHES)

    def body(q_ref, kp_ref, kc_ref, vp_ref, vc_ref, o_ref, l_ref):
        n = pl.program_id(0)
        allowed = _band_mask(n)
        hm = _head_mask()
        low = lax.broadcasted_iota(jnp.int32, (BAND, LANES), 1) < HEAD_DIM

        def unit(r, carry):
            sl = _rows(r, d)
            q = q_ref[sl, :]
            q2 = jnp.where(hm, jnp.concatenate([q, q], axis=0), 0.0).astype(BF)
            k2 = jnp.concatenate([kp_ref[sl, :], kc_ref[sl, :]], axis=0).astype(BF)
            v2 = jnp.concatenate([vp_ref[sl, :], vc_ref[sl, :]], axis=0).astype(BF)
            s = lax.dot_general(q2, k2, (((1,), (1,)), ((), ())), preferred_element_type=F32)
            s = jnp.where(allowed, s, NEG_INF)
            m = jnp.max(s, axis=-1, keepdims=True)
            p = jnp.exp(s - m)
            l = jnp.sum(p, axis=-1, keepdims=True)
            pv = jnp.dot(p.astype(BF), v2, preferred_element_type=F32) / l
            lse = m + jnp.log(l)
            o_ref[sl, :] = jnp.where(low, pv[:BAND], pv[BAND:])
            l_ref[sl, :] = jnp.where(low, lse[:BAND], lse[BAND:])
            return carry

        lax.fori_loop(0, d, unit, 0)

    blk = (rows, LANES)
    return pl.pallas_call(
        body, name=name,
        grid=(nc, HP),
        in_specs=[
            pl.BlockSpec(blk, lambda n, hp: (n, g * HP + hp)),
            pl.BlockSpec(blk, lambda n, hp: (jnp.maximum(n - 1, 0), g * HP + hp)),
            pl.BlockSpec(blk, lambda n, hp: (n, g * HP + hp)),
            pl.BlockSpec(blk, lambda n, hp: (jnp.maximum(n - 1, 0), (NB + g) * HP + hp)),
            pl.BlockSpec(blk, lambda n, hp: (n, (NB + g) * HP + hp)),
        ],
        out_specs=[pl.BlockSpec(blk, lambda n, hp: (n, hp)), pl.BlockSpec(blk, lambda n, hp: (n, hp))],
        out_shape=[jax.ShapeDtypeStruct((T, Dm), F32), jax.ShapeDtypeStruct((T, Dm), F32)],
        compiler_params=_cp("parallel", "parallel"),
    )(q_all, kv_all, kv_all, kv_all, kv_all)


def _attn_merge(os, ls, *, name):
    T, Dm = os[0].shape
    tm = min(ROW_BLOCK, T)
    nbr = len(os)

    def body(*refs):
        o_refs = refs[:nbr]
        l_refs = refs[nbr:2 * nbr]
        o_out, l_out = refs[2 * nbr:]
        lv = [r[...] for r in l_refs]
        m = functools.reduce(jnp.maximum, lv)
        e = [jnp.exp(v - m) for v in lv]
        tot = functools.reduce(jnp.add, e)
        o = functools.reduce(jnp.add, [(ev / tot) * orf[...] for ev, orf in zip(e, o_refs)])
        o_out[...] = o
        l_out[...] = m + jnp.log(tot)

    spec = pl.BlockSpec((tm, Dm), lambda i: (i, 0))
    return pl.pallas_call(
        body, name=name,
        grid=(T // tm,),
        in_specs=[spec] * (2 * nbr),
        out_specs=[spec, spec],
        out_shape=[jax.ShapeDtypeStruct((T, Dm), F32), jax.ShapeDtypeStruct((T, Dm), F32)],
        compiler_params=_cp("parallel"),
    )(*os, *ls)


def _attn_bwd(q_all, kv_all, do, o, lse, cos2, ss2, g, d, prev, *, name):
    T = q_all.shape[0]
    NB = len(BRANCHES)
    Dm = q_all.shape[1] // NB
    HP = Dm // LANES
    rows = BAND * d
    nc = T // rows
    scale = HEAD_DIM ** -0.5

    def rope_bwd(t, cosv, ssv, first):
        return t * cosv - _rot_half(t, first) * ssv

    def body(q_ref, kp_ref, kc_ref, vp_ref, vc_ref, do_ref, o_ref, l_ref,
             cq_ref, sq_ref, ck_ref, sk_ref, *rest):
        dq_ref, dk_ref, dv_ref, ck_car, cv_car = rest[-5:]
        n = pl.program_id(1)
        first = _first_half_mask(BAND)

        @pl.when(n < nc)
        def _():
            allowed = _band_mask(n)
            hm = _head_mask()
            low = lax.broadcasted_iota(jnp.int32, (BAND, LANES), 1) < HEAD_DIM

            def unit(r, carry):
                sl = _rows(r, d)
                q = q_ref[sl, :]
                dov = do_ref[sl, :]
                ov = o_ref[sl, :]
                lv = l_ref[sl, :]
                q2 = jnp.where(hm, jnp.concatenate([q, q], axis=0), 0.0).astype(BF)
                do2 = jnp.where(hm, jnp.concatenate([dov, dov], axis=0), 0.0)
                oo = dov * ov
                delta = jnp.sum(jnp.where(hm, jnp.concatenate([oo, oo], axis=0), 0.0),
                                axis=-1, keepdims=True)
                lse2 = jnp.concatenate([lv[:, 0:1], lv[:, HEAD_DIM:HEAD_DIM + 1]], axis=0)
                do2 = do2.astype(BF)
                k2 = jnp.concatenate([kp_ref[sl, :], kc_ref[sl, :]], axis=0).astype(BF)
                v2 = jnp.concatenate([vp_ref[sl, :], vc_ref[sl, :]], axis=0).astype(BF)
                s = lax.dot_general(q2, k2, (((1,), (1,)), ((), ())), preferred_element_type=F32)
                p = jnp.where(allowed, jnp.exp(s - lse2), 0.0)
                dp = lax.dot_general(do2, v2, (((1,), (1,)), ((), ())), preferred_element_type=F32)
                ds = (p * (dp - delta)).astype(BF)
                dq2 = jnp.dot(ds, k2, preferred_element_type=F32)
                dq = jnp.where(low, dq2[:BAND], dq2[BAND:])
                dq_ref[sl, :] = rope_bwd(dq, cq_ref[sl, :], sq_ref[sl, :], first) * scale
                dk2 = jnp.dot(ds.T, q2, preferred_element_type=F32)
                dv2 = jnp.dot(p.astype(BF).T, do2, preferred_element_type=F32)

                @pl.when(n > 0)
                def _():
                    dk_ref[sl, :] = rope_bwd(ck_car[sl, :] + dk2[:BAND], ck_ref[sl, :], sk_ref[sl, :], first)
                    dv_ref[sl, :] = cv_car[sl, :] + dv2[:BAND]

                ck_car[sl, :] = dk2[BAND:]
                cv_car[sl, :] = dv2[BAND:]
                return carry

            lax.fori_loop(0, d, unit, 0)

        @pl.when(n == nc)
        def _():
            dk_ref[...] = rope_bwd(ck_car[...], ck_ref[...], sk_ref[...], _first_half_mask(rows))
            dv_ref[...] = cv_car[...]

    blk = (rows, LANES)
    cur = lambda hp, n: jnp.minimum(n, nc - 1)
    prv = lambda hp, n: jnp.clip(n - 1, 0, nc - 1)
    in_specs = [
        pl.BlockSpec(blk, lambda hp, n: (cur(hp, n), g * HP + hp)),
        pl.BlockSpec(blk, lambda hp, n: (prv(hp, n), g * HP + hp)),
        pl.BlockSpec(blk, lambda hp, n: (cur(hp, n), g * HP + hp)),
        pl.BlockSpec(blk, lambda hp, n: (prv(hp, n), (NB + g) * HP + hp)),
        pl.BlockSpec(blk, lambda hp, n: (cur(hp, n), (NB + g) * HP + hp)),
        pl.BlockSpec(blk, lambda hp, n: (cur(hp, n), hp)),
        pl.BlockSpec(blk, lambda hp, n: (cur(hp, n), hp)),
        pl.BlockSpec(blk, lambda hp, n: (cur(hp, n), hp)),
        pl.BlockSpec(blk, lambda hp, n: (cur(hp, n), 0)),
        pl.BlockSpec(blk, lambda hp, n: (cur(hp, n), 0)),
        pl.BlockSpec(blk, lambda hp, n: (prv(hp, n), 0)),
        pl.BlockSpec(blk, lambda hp, n: (prv(hp, n), 0)),
    ]
    args = [q_all, kv_all, kv_all, kv_all, kv_all, do, o, lse, cos2, ss2, cos2, ss2]
    aliases = {}
    if prev is not None:
        in_specs += [ANY, ANY, ANY]
        aliases = {len(args): 0, len(args) + 1: 1, len(args) + 2: 2}
        args += list(prev)
    wide = jax.ShapeDtypeStruct((T, NB * Dm), F32)
    return pl.pallas_call(
        body, name=name,
        grid=(HP, nc + 1),
        in_specs=in_specs,
        out_specs=[
            pl.BlockSpec(blk, lambda hp, n: (cur(hp, n), g * HP + hp)),
            pl.BlockSpec(blk, lambda hp, n: (prv(hp, n), g * HP + hp)),
            pl.BlockSpec(blk, lambda hp, n: (prv(hp, n), g * HP + hp)),
        ],
        out_shape=[wide, wide, wide],
        scratch_shapes=[pltpu.VMEM(blk, F32), pltpu.VMEM(blk, F32)],
        input_output_aliases=aliases,
        compiler_params=_cp("arbitrary", "arbitrary"),
    )(*args)


def _postnorm_bwd(dh, y, g_ref_val):
    r = lax.rsqrt(jnp.mean(y * y, axis=-1, keepdims=True) + RMS_EPS)
    yn = y * r
    dyn = dh * g_ref_val
    dy = r * (dyn - yn * jnp.mean(dyn * yn, axis=-1, keepdims=True))
    return dy, yn


def _postnorm_bwd_matmul(dh, y, gain, w3, widx, *, name, da_dtype):
    T, D = dh.shape
    K = w3.shape[1]
    tm = min(ROW_BLOCK, T)

    def body(dh_ref, y_ref, g_ref, w_ref, dy_ref, da_ref, dg_ref):
        i = pl.program_id(0)

        @pl.when(i == 0)
        def _():
            dg_ref[...] = jnp.zeros_like(dg_ref)

        dhv = dh_ref[...]
        dy, yn = _postnorm_bwd(dhv, y_ref[...].astype(F32), g_ref[...])
        dg_ref[...] += jnp.sum(dhv * yn, axis=0, keepdims=True)
        dyb = dy.astype(BF)
        dy_ref[...] = dyb
        da = lax.dot_general(dyb, w_ref[...], (((1,), (1,)), ((), ())), preferred_element_type=F32)
        da_ref[...] = da.astype(da_dtype)

    return pl.pallas_call(
        body, name=name,
        grid=(T // tm,),
        in_specs=[
            pl.BlockSpec((tm, D), lambda i: (i, 0)),
            pl.BlockSpec((tm, D), lambda i: (i, 0)),
            pl.BlockSpec((1, D), lambda i: (0, 0)),
            pl.BlockSpec((None, K, D), lambda i: (widx, 0, 0)),
        ],
        out_specs=[pl.BlockSpec((tm, D), lambda i: (i, 0)),
                   pl.BlockSpec((tm, K), lambda i: (i, 0)),
                   pl.BlockSpec((1, D), lambda i: (0, 0))],
        out_shape=[jax.ShapeDtypeStruct((T, D), BF), jax.ShapeDtypeStruct((T, K), da_dtype),
                   jax.ShapeDtypeStruct((1, D), F32)],
        compiler_params=_cp("arbitrary"),
    )(dh, y, gain, w3)


def _postnorm_bwd_swiglu(dh, y, gain, wd3, layer, g, u, *, name):
    T, D = dh.shape
    F = wd3.shape[1]
    nf = F // 2
    tm = min(ROW_BLOCK, T)

    def body(dh_ref, y_ref, g_ref, w_ref, gg_ref, uu_ref, dy_ref, dgo_ref, duo_ref, dgain_ref, dys):
        i = pl.program_id(0)
        j = pl.program_id(1)

        @pl.when((i == 0) & (j == 0))
        def _():
            dgain_ref[...] = jnp.zeros_like(dgain_ref)

        @pl.when(j == 0)
        def _():
            dhv = dh_ref[...]
            dy, yn = _postnorm_bwd(dhv, y_ref[...].astype(F32), g_ref[...])
            dgain_ref[...] += jnp.sum(dhv * yn, axis=0, keepdims=True)
            dyb = dy.astype(BF)
            dys[...] = dyb
            dy_ref[...] = dyb

        da = lax.dot_general(dys[...], w_ref[...], (((1,), (1,)), ((), ())), preferred_element_type=F32)
        gv = gg_ref[...].astype(F32)
        uv = uu_ref[...].astype(F32)
        sg = jax.nn.sigmoid(gv)
        silu = gv * sg
        dgo_ref[...] = (da * uv * (sg + silu * (1.0 - sg))).astype(BF)
        duo_ref[...] = (da * silu).astype(BF)

    act = jax.ShapeDtypeStruct((T, F), BF)
    return pl.pallas_call(
        body, name=name,
        grid=(T // tm, 2),
        in_specs=[
            pl.BlockSpec((tm, D), lambda i, j: (i, 0)),
            pl.BlockSpec((tm, D), lambda i, j: (i, 0)),
            pl.BlockSpec((1, D), lambda i, j: (0, 0)),
            pl.BlockSpec((None, nf, D), lambda i, j: (layer, j, 0)),
            pl.BlockSpec((tm, nf), lambda i, j: (i, j)),
            pl.BlockSpec((tm, nf), lambda i, j: (i, j)),
        ],
        out_specs=[pl.BlockSpec((tm, D), lambda i, j: (i, 0)),
                   pl.BlockSpec((tm, nf), lambda i, j: (i, j)),
                   pl.BlockSpec((tm, nf), lambda i, j: (i, j)),
                   pl.BlockSpec((1, D), lambda i, j: (0, 0))],
        out_shape=[jax.ShapeDtypeStruct((T, D), BF), act, act, jax.ShapeDtypeStruct((1, D), F32)],
        scratch_shapes=[pltpu.VMEM((tm, D), BF)],
        compiler_params=_cp("arbitrary", "arbitrary"),
    )(dh, y, gain, wd3, g, u)


def _matmul_prenorm_bwd(dzs, wg, layer, h, gain, dh_in, *, name):
    T, D = h.shape
    n = wg.shape[2]
    tm = min(ROW_BLOCK, T)
    pair = len(dzs) == 2
    nj = N_CHIPS // 2 if pair else N_CHIPS

    def body(*refs):
        dz_refs = refs[:len(dzs)]
        w_refs = refs[len(dzs):2 * len(dzs)]
        h_ref, g_ref, dhi_ref, dh_ref, dg_ref, acc = refs[2 * len(dzs):]
        i = pl.program_id(0)
        j = pl.program_id(1)

        @pl.when((i == 0) & (j == 0))
        def _():
            dg_ref[...] = jnp.zeros_like(dg_ref)

        part = None
        for dz_ref, w_ref in zip(dz_refs, w_refs):
            t = lax.dot_general(dz_ref[...].astype(BF), w_ref[...], (((1,), (1,)), ((), ())),
                                preferred_element_type=F32)
            part = t if part is None else part + t

        @pl.when(j == 0)
        def _():
            acc[...] = part

        @pl.when(j > 0)
        def _():
            acc[...] += part

        @pl.when(j == nj - 1)
        def _():
            dhn = acc[...]
            hv = h_ref[...]
            r = lax.rsqrt(jnp.mean(hv * hv, axis=-1, keepdims=True) + RMS_EPS)
            xh = hv * r
            dg_ref[...] += jnp.sum(dhn * xh, axis=0, keepdims=True)
            dxn = dhn * g_ref[...]
            dh_ref[...] = dhi_ref[...] + r * (dxn - xh * jnp.mean(dxn * xh, axis=-1, keepdims=True))

    in_specs = [pl.BlockSpec((tm, n), lambda i, j: (i, j)) for _ in dzs]
    if pair:
        in_specs += [pl.BlockSpec((None, D, n), lambda i, j: (j, layer, 0)),
                     pl.BlockSpec((None, D, n), lambda i, j: (j + nj, layer, 0))]
    else:
        in_specs += [pl.BlockSpec((None, D, n), lambda i, j: (j, layer, 0))]
    in_specs += [pl.BlockSpec((tm, D), lambda i, j: (i, 0)),
                 pl.BlockSpec((1, D), lambda i, j: (0, 0)),
                 pl.BlockSpec((tm, D), lambda i, j: (i, 0))]
    return pl.pallas_call(
        body, name=name,
        grid=(T // tm, nj),
        in_specs=in_specs,
        out_specs=[pl.BlockSpec((tm, D), lambda i, j: (i, 0)), pl.BlockSpec((1, D), lambda i, j: (0, 0))],
        out_shape=[jax.ShapeDtypeStruct((T, D), F32), jax.ShapeDtypeStruct((1, D), F32)],
        scratch_shapes=[pltpu.VMEM((tm, D), F32)],
        compiler_params=_cp("arbitrary", "arbitrary"),
    )(*dzs, *([wg] * len(dzs)), h, gain, dh_in)


def _grad_matmul(a, b, out_shape3, tme, tne, out_index, prev, *, name):
    T, M = a.shape
    N = b.shape[1]
    tk = min(ROW_BLOCK, T)
    nk = T // tk

    def body(a_ref, b_ref, *rest):
        o_ref, acc = rest[-2:]
        k = pl.program_id(2)
        part = jnp.dot(a_ref[...].astype(BF).T, b_ref[...].astype(BF), preferred_element_type=F32)

        @pl.when(k == 0)
        def _():
            acc[...] = part

        @pl.when(k > 0)
        def _():
            acc[...] += part

        @pl.when(k == nk - 1)
        def _():
            o_ref[...] = acc[...].astype(BF)

    in_specs = [pl.BlockSpec((tk, tme), lambda i, j, k: (k, i)),
                pl.BlockSpec((tk, tne), lambda i, j, k: (k, j))]
    args = [a, b]
    aliases = {}
    if prev is not None:
        in_specs.append(ANY)
        args.append(prev)
        aliases = {2: 0}
    return pl.pallas_call(
        body, name=name,
        grid=(M // tme, N // tne, nk),
        in_specs=in_specs,
        out_specs=pl.BlockSpec((None, tme, tne), lambda i, j, k: out_index(i, j)),
        out_shape=jax.ShapeDtypeStruct(out_shape3, BF),
        scratch_shapes=[pltpu.VMEM((tme, tne), F32)],
        input_output_aliases=aliases,
        compiler_params=_cp("parallel", "parallel", "arbitrary"),
    )(*args)


def _row_tile(R):
    for t in (512, 256, 128, 64, 32, 16, 8):
        if R % t == 0:
            return t
    return R


def _cast_place(w2d, half_major, where, dtype, *, name):
    R, C = w2d.shape
    rh = R // 2 if half_major else R
    tr = _row_tile(rh)
    nr = rh // tr

    def body(s_ref, w_ref, o_ref):
        o_ref[...] = w_ref[...].astype(o_ref.dtype)

    if half_major:
        out_spec = pl.BlockSpec((None, None, tr, C), lambda h, i, s: (h, s[0], i, 0))
        shape = (2, N_CHIPS, rh, C)
    else:
        out_spec = pl.BlockSpec((None, tr, C), lambda h, i, s: (s[0], i, 0))
        shape = (N_CHIPS, R, C)
    return pl.pallas_call(
        body, name=name,
        grid_spec=pltpu.PrefetchScalarGridSpec(
            num_scalar_prefetch=1, grid=(R // rh, nr),
            in_specs=[pl.BlockSpec((tr, C), lambda h, i, s: (h * nr + i, 0))],
            out_specs=out_spec),
        out_shape=jax.ShapeDtypeStruct(shape, dtype),
        compiler_params=_cp("arbitrary", "arbitrary"),
    )(where, w2d)


def _pair_sum(dw, theirs, half_major, where, *, name):
    G, rh, C = theirs.shape
    tr = _row_tile(rh)
    nr = rh // tr

    def body(s_ref, a_ref, b_ref, o_ref):
        o_ref[...] = (a_ref[...].astype(F32) + b_ref[...].astype(F32)).astype(BF)

    if half_major:
        mine = pl.BlockSpec((None, None, tr, C), lambda g, i, s: (s[1], g, i, 0))
    else:
        mine = pl.BlockSpec((None, tr, C), lambda g, i, s: (g, s[1] * nr + i, 0))
    spec = pl.BlockSpec((None, tr, C), lambda g, i, s: (g, i, 0))
    return pl.pallas_call(
        body, name=name,
        grid_spec=pltpu.PrefetchScalarGridSpec(
            num_scalar_prefetch=1, grid=(G, nr), in_specs=[mine, spec], out_specs=spec),
        out_shape=jax.ShapeDtypeStruct((G, rh, C), BF),
        compiler_params=_cp("arbitrary", "arbitrary"),
    )(where, dw, theirs)


def _chip_sum(landed, parts, where, *, name):
    G, rh, C = landed.shape
    tr = _row_tile(rh)
    nr = rh // tr

    def body(s_ref, l_ref, p_ref, o_ref):
        for j in range(G):
            def own(j=j):
                v = p_ref[...].astype(F32)
                o_ref[...] = v if j == 0 else o_ref[...] + v

            def other(j=j):
                v = l_ref[j].astype(F32)
                o_ref[...] = v if j == 0 else o_ref[...] + v

            pl.when(s_ref[0] == j)(own)
            pl.when(s_ref[0] != j)(other)

    return pl.pallas_call(
        body, name=name,
        grid_spec=pltpu.PrefetchScalarGridSpec(
            num_scalar_prefetch=1, grid=(nr,),
            in_specs=[pl.BlockSpec((G, tr, C), lambda i, s: (0, i, 0)),
                      pl.BlockSpec((None, tr, C), lambda i, s: (s[0], i, 0))],
            out_specs=pl.BlockSpec((tr, C), lambda i, s: (s[1] * nr + i, 0))),
        out_shape=jax.ShapeDtypeStruct((2 * rh, C), F32),
        compiler_params=_cp("arbitrary"),
    )(where, landed, parts)


def _adamw(w, g, m, v, *, name):
    R, C = w.shape
    tr = _row_tile(R)

    def body(w_ref, g_ref, m_ref, v_ref, d_ref, mo_ref, vo_ref):
        gv = g_ref[...]
        mn = ADAM_B1 * m_ref[...] + (1.0 - ADAM_B1) * gv
        vn = ADAM_B2 * v_ref[...] + (1.0 - ADAM_B2) * jnp.square(gv)
        m_hat = mn / (1.0 - ADAM_B1 ** ADAM_STEP)
        v_hat = vn / (1.0 - ADAM_B2 ** ADAM_STEP)
        d_ref[...] = -ADAM_LR * (m_hat / (jnp.sqrt(v_hat) + ADAM_EPS) + ADAM_WD * w_ref[...])
        mo_ref[...] = mn
        vo_ref[...] = vn

    spec = pl.BlockSpec((tr, C), lambda i: (i, 0))
    shp = jax.ShapeDtypeStruct((R, C), F32)
    return pl.pallas_call(
        body, name=name, grid=(R // tr,), in_specs=[spec] * 4, out_specs=[spec] * 3,
        out_shape=[shp, shp, shp], compiler_params=_cp("parallel"),
    )(w, g, m, v)


def _place():
    x = lax.axis_index("x")
    y = lax.axis_index("y")
    c = lax.axis_index("c")
    chips = [(1 - x, y), (x, 1 - y), (1 - x, 1 - y)]
    return x, y, c, chips


def _half_of(ref, j, h, rh, half_major, r0=0, rc=None):
    rc = rh if rc is None else rc
    return ref.at[h, j, pl.ds(r0, rc)] if half_major else ref.at[j, pl.ds(h * rh + r0, rc)]


def _chunk_rows(rows, row_bytes, align):
    if rows <= align:
        return rows
    cands = [r for r in range(align, rows + 1, align) if rows % r == 0]
    fit = [r for r in cands if r * row_bytes <= DMA_CHUNK_BYTES]
    return max(fit) if fit else min(cands)


def _row_align(dtype):
    return 8 * (4 // jnp.dtype(dtype).itemsize)


def _start_chunks(make, rows, rc):
    for r0 in range(0, rows, rc):
        make(r0, rc).start()


def _all_gather_weights(placed, half_major, small):
    nm = len(placed)
    halves = []
    for p, hmaj in zip(placed, half_major):
        halves.append((p.shape[2], p.shape[3]) if hmaj else (p.shape[1] // 2, p.shape[2]))
    out_shapes = [jax.ShapeDtypeStruct(p.shape, p.dtype) for p in placed]
    out_shapes.append(jax.ShapeDtypeStruct(small.shape, small.dtype))

    def body(*refs):
        outs = refs[nm + 1:2 * nm + 1]
        sm_out = refs[2 * nm + 1]
        ssem, rsem, fsem, gsem, sm_s, sm_r = refs[2 * nm + 2:]
        x, y, c, chips = _place()
        myj = 2 * x + y
        sib = (x, y, 1 - c)

        def spot(m, j, h, r0=0, rc=None):
            return _half_of(outs[m], j, h, halves[m][0], half_major[m], r0, rc)

        def chunk(m):
            rh, cols = halves[m]
            return rh, _chunk_rows(rh, cols * placed[m].dtype.itemsize, _row_align(placed[m].dtype))

        first = []
        for m in range(nm):
            rh, rc = chunk(m)
            for k, (px, py) in enumerate(chips):
                def send(r0, n, m=m, k=k, px=px, py=py):
                    part = spot(m, myj, c, r0, n)
                    return pltpu.make_async_remote_copy(
                        src_ref=part, dst_ref=part,
                        send_sem=ssem.at[3 * m + k], recv_sem=rsem.at[3 * m + k],
                        device_id=(px, py, c), device_id_type=MESH)
                _start_chunks(send, rh, rc)
                first.append(send(0, rh))
        for k, (px, py) in enumerate(chips):
            cp = pltpu.make_async_remote_copy(
                src_ref=sm_out.at[myj], dst_ref=sm_out.at[myj], send_sem=sm_s.at[k], recv_sem=sm_r.at[k],
                device_id=(px, py, c), device_id_type=MESH)
            cp.start()
            first.append(cp)

        passed = []
        for m in range(nm):
            rh, rc = chunk(m)
            for k, (px, py) in enumerate(chips):
                pj = 2 * px + py
                landed = spot(m, pj, c)
                pltpu.make_async_remote_copy(
                    src_ref=landed, dst_ref=landed, send_sem=ssem.at[3 * m + k], recv_sem=rsem.at[3 * m + k],
                    device_id=(px, py, c), device_id_type=MESH).wait_recv()

                def fwd(r0, n, m=m, k=k, pj=pj):
                    part = spot(m, pj, c, r0, n)
                    return pltpu.make_async_remote_copy(
                        src_ref=part, dst_ref=part, send_sem=fsem.at[3 * m + k], recv_sem=gsem.at[3 * m + k],
                        device_id=sib, device_id_type=MESH)
                _start_chunks(fwd, rh, rc)
                passed.append(fwd(0, rh))
        for m in range(nm):
            for k, (px, py) in enumerate(chips):
                theirs = spot(m, 2 * px + py, 1 - c)
                pltpu.make_async_remote_copy(
                    src_ref=theirs, dst_ref=theirs, send_sem=fsem.at[3 * m + k], recv_sem=gsem.at[3 * m + k],
                    device_id=sib, device_id_type=MESH).wait_recv()
        for k, (px, py) in enumerate(chips):
            got = sm_out.at[2 * px + py]
            pltpu.make_async_remote_copy(
                src_ref=got, dst_ref=got, send_sem=sm_s.at[k], recv_sem=sm_r.at[k],
                device_id=(px, py, c), device_id_type=MESH).wait_recv()
        for cp in first + passed:
            cp.wait_send()

    return pl.pallas_call(
        body, name="ag_weights",
        in_specs=[ANY] * (nm + 1),
        out_specs=[ANY] * (nm + 1),
        out_shape=out_shapes,
        input_output_aliases={i: i for i in range(nm + 1)},
        scratch_shapes=[pltpu.SemaphoreType.DMA((3 * nm,)), pltpu.SemaphoreType.DMA((3 * nm,)),
                        pltpu.SemaphoreType.DMA((3 * nm,)), pltpu.SemaphoreType.DMA((3 * nm,)),
                        pltpu.SemaphoreType.DMA((3,)), pltpu.SemaphoreType.DMA((3,))],
    )(*placed, small)


def _sibling_swap(dws, half_major):
    nm = len(dws)
    shapes = []
    for dw, hmaj in zip(dws, half_major):
        if hmaj:
            _, G, rh, C = dw.shape
        else:
            G, R, C = dw.shape
            rh = R // 2
        shapes.append(jax.ShapeDtypeStruct((G, rh, C), dw.dtype))

    def body(*refs):
        ins = refs[:nm]
        theirs = refs[nm:2 * nm]
        ssem, rsem = refs[2 * nm:]
        x, y, c, _ = _place()
        sib = (x, y, 1 - c)

        def half(m, h):
            rh = shapes[m].shape[1]
            return ins[m].at[h] if half_major[m] else ins[m].at[:, pl.ds(h * rh, rh), :]

        cps = []
        for m in range(nm):
            G, rh, cols = shapes[m].shape
            rc = _chunk_rows(rh, cols * shapes[m].dtype.itemsize, _row_align(shapes[m].dtype))
            for j in range(G):
                _start_chunks(lambda r0, n, m=m, j=j, rh=rh: pltpu.make_async_remote_copy(
                    src_ref=_half_of(ins[m], j, 1 - c, rh, half_major[m], r0, n),
                    dst_ref=theirs[m].at[j, pl.ds(r0, n)], send_sem=ssem.at[m], recv_sem=rsem.at[m],
                    device_id=sib, device_id_type=MESH), rh, rc)
            cps.append(pltpu.make_async_remote_copy(
                src_ref=half(m, 1 - c), dst_ref=theirs[m], send_sem=ssem.at[m], recv_sem=rsem.at[m],
                device_id=sib, device_id_type=MESH))
        for cp in cps:
            cp.wait()

    return pl.pallas_call(
        body, name="rs_sibling_swap",
        in_specs=[ANY] * nm, out_specs=[ANY] * nm, out_shape=shapes,
        scratch_shapes=[pltpu.SemaphoreType.DMA((nm,)), pltpu.SemaphoreType.DMA((nm,))],
    )(*dws)


def _chip_exchange(parts):
    nm = len(parts)
    shapes = [jax.ShapeDtypeStruct(p.shape, p.dtype) for p in parts]

    def body(*refs):
        ins = refs[:nm]
        outs = refs[nm:2 * nm]
        ssem, rsem = refs[2 * nm:]
        x, y, c, chips = _place()
        myj = 2 * x + y
        sent = []
        for m in range(nm):
            _, rh, cols = shapes[m].shape
            rc = _chunk_rows(rh, cols * shapes[m].dtype.itemsize, _row_align(shapes[m].dtype))
            for k, (px, py) in enumerate(chips):
                def send(r0, n, m=m, k=k, px=px, py=py):
                    return pltpu.make_async_remote_copy(
                        src_ref=ins[m].at[2 * px + py, pl.ds(r0, n)], dst_ref=outs[m].at[myj, pl.ds(r0, n)],
                        send_sem=ssem.at[3 * m + k], recv_sem=rsem.at[3 * m + k],
                        device_id=(px, py, c), device_id_type=MESH)
                _start_chunks(send, rh, rc)
                sent.append(send(0, rh))
        for m in range(nm):
            for k, (px, py) in enumerate(chips):
                got = outs[m].at[2 * px + py]
                pltpu.make_async_remote_copy(
                    src_ref=got, dst_ref=got, send_sem=ssem.at[3 * m + k], recv_sem=rsem.at[3 * m + k],
                    device_id=(px, py, c), device_id_type=MESH).wait_recv()
        for cp in sent:
            cp.wait_send()

    return pl.pallas_call(
        body, name="rs_chip_exchange",
        in_specs=[ANY] * nm, out_specs=[ANY] * nm, out_shape=shapes,
        scratch_shapes=[pltpu.SemaphoreType.DMA((3 * nm,)), pltpu.SemaphoreType.DMA((3 * nm,))],
    )(*parts)


def _sibling_join(grads):
    nm = len(grads)
    shapes = [jax.ShapeDtypeStruct(g.shape, g.dtype) for g in grads]

    def body(*refs):
        outs = refs[nm:2 * nm]
        ssem, rsem = refs[2 * nm:]
        x, y, c, _ = _place()
        sib = (x, y, 1 - c)
        cps = []
        for m in range(nm):
            rh, cols = grads[m].shape[0] // 2, grads[m].shape[1]
            rc = _chunk_rows(rh, cols * grads[m].dtype.itemsize, _row_align(grads[m].dtype))

            def send(r0, n, m=m, rh=rh):
                part = outs[m].at[pl.ds(c * rh + r0, n)]
                return pltpu.make_async_remote_copy(
                    src_ref=part, dst_ref=part, send_sem=ssem.at[m], recv_sem=rsem.at[m],
                    device_id=sib, device_id_type=MESH)
            _start_chunks(send, rh, rc)
            cps.append(send(0, rh))
        for m, cp in enumerate(cps):
            rh = grads[m].shape[0] // 2
            cp.wait_send()
            got = outs[m].at[pl.ds((1 - c) * rh, rh)]
            pltpu.make_async_remote_copy(
                src_ref=got, dst_ref=got, send_sem=ssem.at[m], recv_sem=rsem.at[m],
                device_id=sib, device_id_type=MESH).wait_recv()

    return pl.pallas_call(
        body, name="rs_sibling_join",
        in_specs=[ANY] * nm, out_specs=[ANY] * nm, out_shape=shapes,
        input_output_aliases={i: i for i in range(nm)},
        scratch_shapes=[pltpu.SemaphoreType.DMA((nm,)), pltpu.SemaphoreType.DMA((nm,))],
    )(*grads)


def _all_reduce_small(pack):
    R, C = pack.shape

    def body(in_ref, out_ref, slots, ssem, rsem):
        x, y, c, _ = _place()
        me = 4 * x + 2 * y + c
        slots[me] = in_ref[...]
        cps = []
        for k in range(1, N_DEV):
            dx, dy, dc = (k >> 2) & 1, (k >> 1) & 1, k & 1
            peer = (x ^ dx, y ^ dy, c ^ dc)
            cp = pltpu.make_async_remote_copy(
                src_ref=in_ref, dst_ref=slots.at[me], send_sem=ssem.at[k], recv_sem=rsem.at[k],
                device_id=peer, device_id_type=MESH)
            cp.start()
            cps.append(cp)
        for k in range(1, N_DEV):
            dx, dy, dc = (k >> 2) & 1, (k >> 1) & 1, k & 1
            got = slots.at[4 * (x ^ dx) + 2 * (y ^ dy) + (c ^ dc)]
            pltpu.make_async_remote_copy(
                src_ref=got, dst_ref=got, send_sem=ssem.at[k], recv_sem=rsem.at[k],
                device_id=(x ^ dx, y ^ dy, c ^ dc), device_id_type=MESH).wait_recv()
        for cp in cps:
            cp.wait_send()
        acc = slots[0]
        for s in range(1, N_DEV):
            acc = acc + slots[s]
        out_ref[...] = acc

    return pl.pallas_call(
        body, name="ar_small",
        in_specs=[pl.BlockSpec(memory_space=pltpu.VMEM)],
        out_specs=pl.BlockSpec(memory_space=pltpu.VMEM),
        out_shape=jax.ShapeDtypeStruct((R, C), F32),
        scratch_shapes=[pltpu.VMEM((N_DEV, R, C), F32),
                        pltpu.SemaphoreType.DMA((N_DEV,)), pltpu.SemaphoreType.DMA((N_DEV,))],
    )(pack)


def kernel(x, positions, mix_norm_pre, mix_norm_post, ffn_norm_pre, ffn_norm_post, ffn_w_gate_up, ffn_w_down, conv_w_in, conv_w, conv_w_out, kv_norm, w_kv, w_q, w_o, loss_target, m_mix_norm_pre, m_mix_norm_post, m_ffn_norm_pre, m_ffn_norm_post, m_ffn_w_gate_up, m_ffn_w_down, m_conv_w_in, m_conv_w, m_conv_w_out, m_kv_norm, m_w_kv, m_w_q, m_w_o, v_mix_norm_pre, v_mix_norm_post, v_ffn_norm_pre, v_ffn_norm_post, v_ffn_w_gate_up, v_ffn_w_down, v_conv_w_in, v_conv_w, v_conv_w_out, v_kv_norm, v_w_kv, v_w_q, v_w_o):
    T, D = x.shape[1], x.shape[2]
    L = ffn_w_gate_up.shape[0]
    n_gu = ffn_w_gate_up.shape[2]
    f_sh = ffn_w_down.shape[1]
    F = N_CHIPS * f_sh
    x0 = x[0]
    tgt = loss_target[0]

    half = HEAD_DIM // 2
    inv_freq = ROPE_THETA ** (-jnp.arange(half, dtype=F32) / half)
    ang = positions[0].astype(F32)[:, None] * inv_freq
    cosv, sinv = jnp.cos(ang), jnp.sin(ang)
    cos2 = jnp.tile(cosv, (1, LANES // half))
    ss2 = jnp.tile(jnp.concatenate([-sinv, sinv], axis=1), (1, LANES // HEAD_DIM))

    def as2d(a):
        return a.reshape(-1, a.shape[-1])

    big = [ffn_w_gate_up, ffn_w_down, conv_w_in, conv_w_out, w_kv, w_q, w_o]
    big_m = [m_ffn_w_gate_up, m_ffn_w_down, m_conv_w_in, m_conv_w_out, m_w_kv, m_w_q, m_w_o]
    big_v = [v_ffn_w_gate_up, v_ffn_w_down, v_conv_w_in, v_conv_w_out, v_w_kv, v_w_q, v_w_o]
    half_major = [False, True, False, False, False, False, False]
    chip = 2 * lax.axis_index("x") + lax.axis_index("y")
    where = jnp.stack([chip, lax.axis_index("c")]).astype(jnp.int32)
    tc = conv_w.shape[2]
    cw_pad = jnp.concatenate([conv_w[0], jnp.zeros((8 - conv_w.shape[1], tc), F32)], axis=0)

    placed = [_cast_place(as2d(w), hm, where, BF, name=f"place{i}")
              for i, (w, hm) in enumerate(zip(big, half_major))]
    gathered = _all_gather_weights(placed, half_major, _cast_place(cw_pad, False, where, F32, name="place_taps"))
    wgu, wd4, wci, wco4, wkv, wq, wo4, cw = gathered
    wd = wd4.reshape(L, F, D)
    wco = wco4.reshape(1, D, D)
    wo = wo4.reshape(1, D, D)

    def row(a, i):
        return a[i:i + 1]

    z, hn_m0 = _norm_matmul(x0, row(mix_norm_pre, 0), wci, cos2, ss2, name="f0_conv_in",
                            rope_shards=0, scale=1.0, out_dtype=BF)
    vmix = _conv_fwd(z, cw, name="f0_conv")
    y0, h1 = _matmul_postnorm(vmix, wco, 0, row(mix_norm_post, 0), x0, name="f0_conv_out")
    g0, u0, a0, hn_f0 = _norm_swiglu(h1, row(ffn_norm_pre, 0), wgu, 0, name="f0_gate_up")
    f0, h2 = _matmul_postnorm(a0, wd, 0, row(ffn_norm_post, 0), h1, name="f0_down")

    kv_all, hn_kv = _norm_matmul(h2, kv_norm.reshape(1, D), wkv, cos2, ss2, name="f1_kv",
                                 rope_shards=N_CHIPS // 2, scale=1.0, out_dtype=F32)
    q_all, hn_m1 = _norm_matmul(h2, row(mix_norm_pre, 1), wq, cos2, ss2, name="f1_q",
                                rope_shards=N_CHIPS, scale=HEAD_DIM ** -0.5, out_dtype=F32)
    os_, ls_ = [], []
    for gi, (window, dil) in enumerate(BRANCHES):
        o_g, l_g = _attn_fwd(q_all, kv_all, gi, dil, name=f"f1_attn{gi}")
        os_.append(o_g)
        ls_.append(l_g)
    o_att, lse = _attn_merge(os_, ls_, name="f1_merge")
    y1, h3 = _matmul_postnorm(o_att, wo, 0, row(mix_norm_post, 1), h2, name="f1_attn_out")
    g1, u1, a1, hn_f1 = _norm_swiglu(h3, row(ffn_norm_pre, 1), wgu, 1, name="f1_gate_up")
    f1, h4 = _matmul_postnorm(a1, wd, 1, row(ffn_norm_post, 1), h3, name="f1_down")

    dh4, sq = _loss_head(h4, tgt, name="loss_head")
    loss_part = 0.5 * sq[0, 0] / D

    gu_shape = (N_CHIPS, L * D, n_gu)
    dyf1, dg1, du1, d_ffn_post1 = _postnorm_bwd_swiglu(dh4, f1, row(ffn_norm_post, 1), wd, 1, g1, u1,
                                                       name="b1_down")
    dwd = _grad_matmul(a1, dyf1, (L, F, D), F // 2, D, lambda i, j: (1, i, 0), None, name="b1_dw_down")
    dwgu = _grad_matmul(hn_f1, dg1, gu_shape, D, n_gu, lambda i, j: (j, 1, 0), None, name="b1_dw_gate")
    dwgu = _grad_matmul(hn_f1, du1, gu_shape, D, n_gu, lambda i, j: (j + 2, 1, 0), dwgu, name="b1_dw_up")
    dh3, d_ffn_pre1 = _matmul_prenorm_bwd((dg1, du1), wgu, 1, h3, row(ffn_norm_pre, 1), dh4, name="b1_gate_up")

    dy1, do, d_mix_post1 = _postnorm_bwd_matmul(dh3, y1, row(mix_norm_post, 1), wo, 0, name="b1_attn_out",
                                                da_dtype=F32)
    dwo = _grad_matmul(o_att, dy1, (1, D, D), D, D, lambda i, j: (0, 0, 0), None, name="b1_dw_o")
    prev = None
    for gi, (window, dil) in enumerate(BRANCHES):
        prev = _attn_bwd(q_all, kv_all, do, o_att, lse, cos2, ss2, gi, dil, prev, name=f"b1_attn{gi}")
    dq_all, dk_all, dv_all = prev
    n_q = wq.shape[2]
    n_kv = wkv.shape[2]
    dwq = _grad_matmul(hn_m1, dq_all, (N_CHIPS, D, n_q), D, n_q, lambda i, j: (j, 0, 0), None, name="b1_dw_q")
    dwkv = _grad_matmul(hn_kv, dk_all, (N_CHIPS, D, n_kv), D, n_kv, lambda i, j: (j, 0, 0), None, name="b1_dw_k")
    dwkv = _grad_matmul(hn_kv, dv_all, (N_CHIPS, D, n_kv), D, n_kv, lambda i, j: (j + 2, 0, 0), dwkv, name="b1_dw_v")
    dh2, d_mix_pre1 = _matmul_prenorm_bwd((dq_all,), wq, 0, h2, row(mix_norm_pre, 1), dh3, name="b1_q")
    dh2, d_kv_norm = _matmul_prenorm_bwd((dk_all, dv_all), wkv, 0, h2, kv_norm.reshape(1, D), dh2, name="b1_kv")

    dyf0, dg0, du0, d_ffn_post0 = _postnorm_bwd_swiglu(dh2, f0, row(ffn_norm_post, 0), wd, 0, g0, u0,
                                                       name="b0_down")
    dwd = _grad_matmul(a0, dyf0, (L, F, D), F // 2, D, lambda i, j: (0, i, 0), dwd, name="b0_dw_down")
    dwgu = _grad_matmul(hn_f0, dg0, gu_shape, D, n_gu, lambda i, j: (j, 0, 0), dwgu, name="b0_dw_gate")
    dwgu = _grad_matmul(hn_f0, du0, gu_shape, D, n_gu, lambda i, j: (j + 2, 0, 0), dwgu, name="b0_dw_up")
    dh1, d_ffn_pre0 = _matmul_prenorm_bwd((dg0, du0), wgu, 0, h1, row(ffn_norm_pre, 0), dh2, name="b0_gate_up")

    dy0, dvmix, d_mix_post0 = _postnorm_bwd_matmul(dh1, y0, row(mix_norm_post, 0), wco, 0, name="b0_conv_out",
                                                   da_dtype=BF)
    dwco = _grad_matmul(vmix, dy0, (1, D, D), D, D, lambda i, j: (0, 0, 0), None, name="b0_dw_conv_out")
    dz, dcw = _conv_bwd(z, cw, dvmix, name="b0_conv")
    n_ci = wci.shape[2]
    dwci = _grad_matmul(hn_m0, dz, (N_CHIPS, D, n_ci), D, n_ci, lambda i, j: (j, 0, 0), None, name="b0_dw_conv_in")
    dx, d_mix_pre0 = _matmul_prenorm_bwd((dz,), wci, 0, x0, row(mix_norm_pre, 0), dh1, name="b0_conv_in")

    pack = jnp.concatenate([
        d_mix_pre0, d_mix_pre1, d_mix_post0, d_mix_post1, d_ffn_pre0, d_ffn_pre1, d_ffn_post0, d_ffn_post1,
        d_kv_norm, dcw[0:3], jnp.full((1, D), loss_part, F32),
        jnp.zeros((SMALL_ROWS - 13, D), F32)], axis=0)
    red = _all_reduce_small(pack)
    loss = red[12, 0]
    myj = 2 * lax.axis_index("x") + lax.axis_index("y")
    g_conv_w = lax.dynamic_slice(red, (9, myj * tc), (3, tc))

    zeros7 = jnp.zeros((SMALL_ROWS - 9, D), F32)
    w_small = jnp.concatenate([mix_norm_pre, mix_norm_post, ffn_norm_pre, ffn_norm_post, kv_norm.reshape(1, D), zeros7], axis=0)
    m_small = jnp.concatenate([m_mix_norm_pre, m_mix_norm_post, m_ffn_norm_pre, m_ffn_norm_post, m_kv_norm.reshape(1, D), zeros7], axis=0)
    v_small = jnp.concatenate([v_mix_norm_pre, v_mix_norm_post, v_ffn_norm_pre, v_ffn_norm_post, v_kv_norm.reshape(1, D), zeros7], axis=0)
    d_small, nm_small, nv_small = _adamw(w_small, red, m_small, v_small, name="adamw_small")

    pad5 = jnp.zeros((5, tc), F32)
    d_cw, nm_cw, nv_cw = _adamw(cw_pad, jnp.concatenate([g_conv_w, pad5], axis=0),
                                jnp.concatenate([m_conv_w[0], pad5], axis=0),
                                jnp.concatenate([v_conv_w[0], pad5], axis=0), name="adamw_conv_w")

    dws = [dwgu, dwd.reshape(2, N_CHIPS, f_sh, D), dwci, dwco.reshape(N_CHIPS, D // N_CHIPS, D), dwkv, dwq,
           dwo.reshape(N_CHIPS, D // N_CHIPS, D)]
    theirs = _sibling_swap(dws, half_major)
    parts = [_pair_sum(dw, t, hm, where, name=f"rs_pair_sum{i}")
             for i, (dw, t, hm) in enumerate(zip(dws, theirs, half_major))]
    landed = _chip_exchange(parts)
    halves = [_chip_sum(l, p, where, name=f"rs_chip_sum{i}") for i, (l, p) in enumerate(zip(landed, parts))]
    grads2d = _sibling_join(halves)

    big_out = []
    for i, (w, gr, m, v) in enumerate(zip(big, grads2d, big_m, big_v)):
        d_, m_, v_ = _adamw(as2d(w), gr, as2d(m), as2d(v), name=f"adamw{i}")
        big_out.append((gr.reshape(w.shape), d_.reshape(w.shape), m_.reshape(w.shape), v_.reshape(w.shape)))

    def small(a):
        return (a[0:2], a[2:4], a[4:6], a[6:8])

    def assemble(sm, cwv, kind):
        pre, post, fpre, fpost = small(sm)
        b = [t[kind] for t in big_out]
        return [pre, post, fpre, fpost, b[0], b[1], b[2], cwv[0:3].reshape(conv_w.shape), b[3],
                sm[8], b[4], b[5].reshape(w_q.shape), b[6].reshape(w_o.shape)]

    grads = assemble(red, jnp.concatenate([g_conv_w, pad5], axis=0), 0)
    deltas = assemble(d_small, d_cw, 1)
    new_m = assemble(nm_small, nm_cw, 2)
    new_v = assemble(nv_small, nv_cw, 3)
    return (loss, dx.reshape(x.shape), *grads, *deltas, *new_m, *new_v)
```

```python
import functools

import jax
import jax.numpy as jnp
from jax import lax
from jax.experimental import pallas as pl
from jax.experimental.pallas import tpu as pltpu

HEAD_DIM = 64
BAND = 128
BRANCHES = ((128, 1), (512, 4), (2048, 16))
ROPE_THETA = 10000.0
RMS_EPS = 1e-6
NEG_INF = -1e30
ADAM_LR = 0.001
ADAM_B1 = 0.9
ADAM_B2 = 0.999
ADAM_EPS = 1e-08
ADAM_WD = 0.01
ADAM_STEP = 10

N_CHIPS = 4
N_DEV = 8
LANES = 128
ROW_BLOCK = 512
ATTN_BLOCK_ROWS = 2048
VMEM_LIMIT = 56 * 1024 * 1024
SMALL_ROWS = 16
DMA_CHUNK_BYTES = 512 * 1024

BF = jnp.bfloat16
F32 = jnp.float32
MESH = pl.DeviceIdType.MESH
ANY = pl.BlockSpec(memory_space=pl.ANY)


def _cp(*sem):
    return pltpu.CompilerParams(dimension_semantics=sem, vmem_limit_bytes=VMEM_LIMIT)


def _rot_half(t, first):
    return jnp.where(first, pltpu.roll(t, 96, 1), pltpu.roll(t, 32, 1))


def _first_half_mask(rows):
    lane = lax.broadcasted_iota(jnp.int32, (rows, LANES), 1)
    return (lane % HEAD_DIM) < (HEAD_DIM // 2)


def _normed_rows(j, rows, x_ref, g_ref, xn_ref, xs, last_start, tm):
    @pl.when(j == 0)
    def _():
        xv = x_ref[...]
        r = lax.rsqrt(jnp.mean(xv * xv, axis=-1, keepdims=True) + RMS_EPS)
        xn = (xv * r * g_ref[...]).astype(BF)
        xs[rows, :] = xn
        xn_ref[...] = xn

    @pl.when(j > 0)
    def _():
        xn_ref[...] = xs[pl.ds(last_start, tm), :]


def _norm_matmul(x, gain, wg, cos2, ss2, *, name, rope_shards, scale, out_dtype):
    T, D = x.shape
    n = wg.shape[2]
    tm = min(ROW_BLOCK, T)

    ni = T // tm

    def body(x_ref, g_ref, w_ref, cos_ref, ss_ref, y_ref, xn_ref, xs):
        j = pl.program_id(0)
        rows = pl.ds(pl.multiple_of(pl.program_id(1) * tm, tm), tm)
        _normed_rows(j, rows, x_ref, g_ref, xn_ref, xs, (ni - 1) * tm, tm)
        acc = jnp.dot(xs[rows, :], w_ref[...], preferred_element_type=F32)

        def plain():
            y_ref[...] = acc.astype(out_dtype)

        def rope():
            cosv = cos_ref[...]
            ssv = ss_ref[...]
            first = _first_half_mask(tm)
            for ci in range(n // LANES):
                t = acc[:, ci * LANES:(ci + 1) * LANES]
                y = (t * cosv + _rot_half(t, first) * ssv) * scale
                y_ref[:, ci * LANES:(ci + 1) * LANES] = y.astype(out_dtype)

        if rope_shards == 0:
            plain()
        elif rope_shards == N_CHIPS:
            rope()
        else:
            pl.when(j < rope_shards)(rope)
            pl.when(j >= rope_shards)(plain)

    first_pass = lambda j, i: (jnp.where(j == 0, i, ni - 1), 0)
    return pl.pallas_call(
        body, name=name,
        grid=(N_CHIPS, ni),
        in_specs=[
            pl.BlockSpec((tm, D), first_pass),
            pl.BlockSpec((1, D), lambda j, i: (0, 0)),
            pl.BlockSpec((None, D, n), lambda j, i: (j, 0, 0)),
            pl.BlockSpec((tm, LANES), lambda j, i: (i, 0)),
            pl.BlockSpec((tm, LANES), lambda j, i: (i, 0)),
        ],
        out_specs=[
            pl.BlockSpec((tm, n), lambda j, i: (i, j)),
            pl.BlockSpec((tm, D), first_pass),
        ],
        out_shape=[jax.ShapeDtypeStruct((T, N_CHIPS * n), out_dtype),
                   jax.ShapeDtypeStruct((T, D), BF)],
        scratch_shapes=[pltpu.VMEM((T, D), BF)],
        compiler_params=_cp("arbitrary", "arbitrary"),
    )(x, gain, wg, cos2, ss2)


def _norm_swiglu(x, gain, wg, layer, *, name):
    T, D = x.shape
    n = wg.shape[2]
    tm = min(ROW_BLOCK, T)

    ni = T // tm

    def body(x_ref, g_ref, wg_ref, wu_ref, go_ref, uo_ref, ao_ref, xn_ref, xs):
        j = pl.program_id(0)
        rows = pl.ds(pl.multiple_of(pl.program_id(1) * tm, tm), tm)
        _normed_rows(j, rows, x_ref, g_ref, xn_ref, xs, (ni - 1) * tm, tm)
        g = jnp.dot(xs[rows, :], wg_ref[...], preferred_element_type=F32)
        u = jnp.dot(xs[rows, :], wu_ref[...], preferred_element_type=F32)
        go_ref[...] = g.astype(BF)
        uo_ref[...] = u.astype(BF)
        ao_ref[...] = (g * jax.nn.sigmoid(g) * u).astype(BF)

    half = N_CHIPS // 2
    first_pass = lambda j, i: (jnp.where(j == 0, i, ni - 1), 0)
    act = jax.ShapeDtypeStruct((T, half * n), BF)
    return pl.pallas_call(
        body, name=name,
        grid=(half, ni),
        in_specs=[
            pl.BlockSpec((tm, D), first_pass),
            pl.BlockSpec((1, D), lambda j, i: (0, 0)),
            pl.BlockSpec((None, D, n), lambda j, i: (j, layer, 0)),
            pl.BlockSpec((None, D, n), lambda j, i: (j + half, layer, 0)),
        ],
        out_specs=[
            pl.BlockSpec((tm, n), lambda j, i: (i, j)),
            pl.BlockSpec((tm, n), lambda j, i: (i, j)),
            pl.BlockSpec((tm, n), lambda j, i: (i, j)),
            pl.BlockSpec((tm, D), first_pass),
        ],
        out_shape=[act, act, act, jax.ShapeDtypeStruct((T, D), BF)],
        scratch_shapes=[pltpu.VMEM((T, D), BF)],
        compiler_params=_cp("arbitrary", "arbitrary"),
    )(x, gain, wg, wg)


def _matmul_postnorm(a, w3, widx, gain, h_old, *, name):
    T, K = a.shape
    D = w3.shape[2]
    tm = min(ROW_BLOCK, T)

    def body(a_ref, w_ref, g_ref, h_ref, y_ref, hn_ref):
        y = jnp.dot(a_ref[...].astype(BF), w_ref[...], preferred_element_type=F32)
        y_ref[...] = y.astype(BF)
        r = lax.rsqrt(jnp.mean(y * y, axis=-1, keepdims=True) + RMS_EPS)
        hn_ref[...] = h_ref[...] + y * r * g_ref[...]

    return pl.pallas_call(
        body, name=name,
        grid=(T // tm,),
        in_specs=[
            pl.BlockSpec((tm, K), lambda i: (i, 0)),
            pl.BlockSpec((None, K, D), lambda i: (widx, 0, 0)),
            pl.BlockSpec((1, D), lambda i: (0, 0)),
            pl.BlockSpec((tm, D), lambda i: (i, 0)),
        ],
        out_specs=[pl.BlockSpec((tm, D), lambda i: (i, 0)),
                   pl.BlockSpec((tm, D), lambda i: (i, 0))],
        out_shape=[jax.ShapeDtypeStruct((T, D), BF), jax.ShapeDtypeStruct((T, D), F32)],
        compiler_params=_cp("parallel"),
    )(a, w3, gain, h_old)


def _loss_head(h, target, *, name):
    T, D = h.shape
    tm = min(ROW_BLOCK, T)

    def body(h_ref, t_ref, dh_ref, s_ref):
        i = pl.program_id(0)

        @pl.when(i == 0)
        def _():
            s_ref[...] = jnp.zeros_like(s_ref)

        e = h_ref[...] - t_ref[...]
        dh_ref[...] = e * (1.0 / D)
        s_ref[...] += jnp.sum(e * e)

    return pl.pallas_call(
        body, name=name,
        grid=(T // tm,),
        in_specs=[pl.BlockSpec((tm, D), lambda i: (i, 0)), pl.BlockSpec((tm, D), lambda i: (i, 0))],
        out_specs=[pl.BlockSpec((tm, D), lambda i: (i, 0)), pl.BlockSpec((8, LANES), lambda i: (0, 0))],
        out_shape=[jax.ShapeDtypeStruct((T, D), F32), jax.ShapeDtypeStruct((8, LANES), F32)],
        compiler_params=_cp("arbitrary"),
    )(h, target)


def _shift_down(u, k):
    row = lax.broadcasted_iota(jnp.int32, u.shape, 0)
    return jnp.where(row >= k, pltpu.roll(u, k, 0), 0.0)


def _shift_up(u, k):
    T = u.shape[0]
    row = lax.broadcasted_iota(jnp.int32, u.shape, 0)
    return jnp.where(row < T - k, pltpu.roll(u, T - k, 0), 0.0)


def _conv_fwd(z, cw, *, name):
    T = z.shape[0]
    D = z.shape[1] // 3
    tc = cw.shape[2]
    nb = D // tc

    def body(b_ref, c_ref, h_ref, w_ref, o_ref):
        u = c_ref[...].astype(F32) * h_ref[...].astype(F32)
        w = w_ref[...]
        conv = w[2:3] * u + w[1:2] * _shift_down(u, 1) + w[0:1] * _shift_down(u, 2)
        o_ref[...] = (b_ref[...].astype(F32) * conv).astype(BF)

    return pl.pallas_call(
        body, name=name,
        grid=(nb,),
        in_specs=[
            pl.BlockSpec((T, tc), lambda j: (0, j)),
            pl.BlockSpec((T, tc), lambda j: (0, nb + j)),
            pl.BlockSpec((T, tc), lambda j: (0, 2 * nb + j)),
            pl.BlockSpec((None, 8, tc), lambda j: (j, 0, 0)),
        ],
        out_specs=pl.BlockSpec((T, tc), lambda j: (0, j)),
        out_shape=jax.ShapeDtypeStruct((T, D), BF),
        compiler_params=_cp("parallel"),
    )(z, z, z, cw)


def _conv_bwd(z, cw, dv, *, name):
    T = z.shape[0]
    D = z.shape[1] // 3
    tc = cw.shape[2]
    nb = D // tc

    def body(b_ref, c_ref, h_ref, w_ref, dv_ref, dz_ref, dw_ref):
        p = pl.program_id(0)
        c = c_ref[...].astype(F32)
        h = h_ref[...].astype(F32)
        u = c * h
        u1 = _shift_down(u, 1)
        u2 = _shift_down(u, 2)
        w = w_ref[...]
        dvv = dv_ref[...].astype(F32)
        dconv = dvv * b_ref[...].astype(F32)
        du = w[2:3] * dconv + w[1:2] * _shift_up(dconv, 1) + w[0:1] * _shift_up(dconv, 2)
        rows = lax.broadcasted_iota(jnp.int32, (8, tc), 0)
        dw = jnp.where(rows == 0, jnp.sum(dconv * u2, axis=0, keepdims=True),
                       jnp.where(rows == 1, jnp.sum(dconv * u1, axis=0, keepdims=True),
                                 jnp.where(rows == 2, jnp.sum(dconv * u, axis=0, keepdims=True), 0.0)))
        dw_ref[...] = dw

        @pl.when(p == 0)
        def _():
            conv = w[2:3] * u + w[1:2] * u1 + w[0:1] * u2
            dz_ref[...] = (dvv * conv).astype(BF)

        @pl.when(p == 1)
        def _():
            dz_ref[...] = (du * h).astype(BF)

        @pl.when(p == 2)
        def _():
            dz_ref[...] = (du * c).astype(BF)

    return pl.pallas_call(
        body, name=name,
        grid=(3, nb),
        in_specs=[
            pl.BlockSpec((T, tc), lambda p, j: (0, j)),
            pl.BlockSpec((T, tc), lambda p, j: (0, nb + j)),
            pl.BlockSpec((T, tc), lambda p, j: (0, 2 * nb + j)),
            pl.BlockSpec((None, 8, tc), lambda p, j: (j, 0, 0)),
            pl.BlockSpec((T, tc), lambda p, j: (0, j)),
        ],
        out_specs=[pl.BlockSpec((T, tc), lambda p, j: (0, p * nb + j)),
                   pl.BlockSpec((8, tc), lambda p, j: (p, j))],
        out_shape=[jax.ShapeDtypeStruct((T, 3 * D), BF), jax.ShapeDtypeStruct((3 * 8, D), F32)],
        compiler_params=_cp("arbitrary", "arbitrary"),
    )(z, z, z, cw, dv)


def _strided(base, count, d):
    return pl.ds(base, count, stride=d) if d > 1 else pl.ds(pl.multiple_of(base, BAND), count)


def _band_mask(has_prev):
    qi = lax.broadcasted_iota(jnp.int32, (2 * BAND, 2 * BAND), 0) % BAND
    kj = lax.broadcasted_iota(jnp.int32, (2 * BAND, 2 * BAND), 1)
    dist = qi + BAND - kj
    return (dist >= 0) & (dist <= BAND) & ((kj >= BAND) | has_prev)


def _attn_block_rows(T):
    return min(ATTN_BLOCK_ROWS, T)


def _head_mask():
    lane = lax.broadcasted_iota(jnp.int32, (2 * BAND, LANES), 1)
    row = lax.broadcasted_iota(jnp.int32, (2 * BAND, LANES), 0)
    return (lane < HEAD_DIM) == (row < BAND)


def _attn_fwd(q_all, kv_all, *, name):
    T = q_all.shape[0]
    NB = len(BRANCHES)
    Dm = q_all.shape[1] // NB
    HP = Dm // LANES
    R = _attn_block_rows(T)
    units = R // BAND
    dmax = max(d for _, d in BRANCHES)

    def body(*refs):
        ins = refs[:5 * NB]
        o_ref, l_ref, kbuf, vbuf, o_s, l_s = refs[5 * NB:]
        n = pl.program_id(0)
        hm = _head_mask()
        low = lax.broadcasted_iota(jnp.int32, (BAND, LANES), 1) < HEAD_DIM

        for g, (_, d) in enumerate(BRANCHES):
            q_ref, kp_ref, kc_ref, vp_ref, vc_ref = ins[5 * g:5 * g + 5]
            pr = BAND * d
            kbuf[0:pr, :] = kp_ref[...]
            kbuf[pr:pr + R, :] = kc_ref[...]
            vbuf[0:pr, :] = vp_ref[...]
            vbuf[pr:pr + R, :] = vc_ref[...]

            def unit(u, carry, g=g, d=d, pr=pr, q_ref=q_ref):
                sub = u // d
                base = sub * pr + (u - sub * d)
                q = q_ref[_strided(base, BAND, d), :]
                q2 = jnp.where(hm, jnp.concatenate([q, q], axis=0), 0.0).astype(BF)
                k2 = kbuf[_strided(base, 2 * BAND, d), :].astype(BF)
                v2 = vbuf[_strided(base, 2 * BAND, d), :].astype(BF)
                s = lax.dot_general(q2, k2, (((1,), (1,)), ((), ())), preferred_element_type=F32)
                s = jnp.where(_band_mask((n > 0) | (sub > 0)), s, NEG_INF)
                m = jnp.max(s, axis=-1, keepdims=True)
                p = jnp.exp(s - m)
                l = jnp.sum(p, axis=-1, keepdims=True)
                pv = jnp.dot(p.astype(BF), v2, preferred_element_type=F32) / l
                lse = m + jnp.log(l)
                o_s[g, _strided(base, BAND, d), :] = jnp.where(low, pv[:BAND], pv[BAND:])
                l_s[g, _strided(base, BAND, d), :] = jnp.where(low, lse[:BAND], lse[BAND:])
                return carry

            lax.fori_loop(0, units, unit, 0, unroll=2)

        def merge(i, carry):
            sl = pl.ds(pl.multiple_of(i * BAND, BAND), BAND)
            lv = [l_s[g, sl, :] for g in range(NB)]
            m = functools.reduce(jnp.maximum, lv)
            e = [jnp.exp(v - m) for v in lv]
            tot = functools.reduce(jnp.add, e)
            o_ref[sl, :] = functools.reduce(jnp.add, [(e[g] / tot) * o_s[g, sl, :] for g in range(NB)])
            l_ref[sl, :] = m + jnp.log(tot)
            return carry

        lax.fori_loop(0, units, merge, 0)

    in_specs, args = [], []
    for g, (_, d) in enumerate(BRANCHES):
        per = R // (BAND * d)
        for col, rows, idx in (
                (g * HP, R, lambda n, hp: n),
                (g * HP, BAND * d, lambda n, hp, per=per: jnp.maximum(n * per - 1, 0)),
                (g * HP, R, lambda n, hp: n),
                ((NB + g) * HP, BAND * d, lambda n, hp, per=per: jnp.maximum(n * per - 1, 0)),
                ((NB + g) * HP, R, lambda n, hp: n)):
            in_specs.append(pl.BlockSpec((rows, LANES), lambda n, hp, col=col, idx=idx: (idx(n, hp), col + hp)))
        args += [q_all, kv_all, kv_all, kv_all, kv_all]
    out = pl.BlockSpec((R, LANES), lambda n, hp: (n, hp))
    return pl.pallas_call(
        body, name=name,
        grid=(T // R, HP),
        in_specs=in_specs,
        out_specs=[out, out],
        out_shape=[jax.ShapeDtypeStruct((T, Dm), F32), jax.ShapeDtypeStruct((T, Dm), F32)],
        scratch_shapes=[pltpu.VMEM((BAND * dmax + R, LANES), F32), pltpu.VMEM((BAND * dmax + R, LANES), F32),
                        pltpu.VMEM((NB, R, LANES), F32), pltpu.VMEM((NB, R, LANES), F32)],
        compiler_params=_cp("parallel", "parallel"),
    )(*args)


def _attn_bwd(q_all, kv_all, do, o, lse, cos2, ss2, g, d, prev, *, name):
    T = q_all.shape[0]
    NB = len(BRANCHES)
    Dm = q_all.shape[1] // NB
    HP = Dm // LANES
    R = _attn_block_rows(T)
    nblk = T // R
    units = R // BAND
    pr = BAND * d
    per = R // pr
    scale = HEAD_DIM ** -0.5

    def rope_bwd(t, cosv, ssv, first):
        return t * cosv - _rot_half(t, first) * ssv

    def body(q_ref, kp_ref, kc_ref, vp_ref, vc_ref, do_ref, o_ref, l_ref, cos_ref, ss_ref, *rest):
        dq_ref, dk_ref, dv_ref, kbuf, vbuf, dkbuf, dvbuf, ck_car, cv_car = rest[-9:]
        i = pl.program_id(1)
        n = nblk - 1 - i
        first = _first_half_mask(BAND)
        hm = _head_mask()
        low = lax.broadcasted_iota(jnp.int32, (BAND, LANES), 1) < HEAD_DIM

        kbuf[0:pr, :] = kp_ref[...]
        kbuf[pr:pr + R, :] = kc_ref[...]
        vbuf[0:pr, :] = vp_ref[...]
        vbuf[pr:pr + R, :] = vc_ref[...]
        dkbuf[...] = jnp.zeros_like(dkbuf)
        dvbuf[...] = jnp.zeros_like(dvbuf)

        @pl.when(i > 0)
        def _():
            dkbuf[R:R + pr, :] = ck_car[...]
            dvbuf[R:R + pr, :] = cv_car[...]

        def unit(u, carry):
            sub = u // d
            base = sub * pr + (u - sub * d)
            sl = _strided(base, BAND, d)
            sl2 = _strided(base, 2 * BAND, d)
            q = q_ref[sl, :]
            dov = do_ref[sl, :]
            ov = o_ref[sl, :]
            lv = l_ref[sl, :]
            q2 = jnp.where(hm, jnp.concatenate([q, q], axis=0), 0.0).astype(BF)
            do2 = jnp.where(hm, jnp.concatenate([dov, dov], axis=0), 0.0)
            oo = dov * ov
            delta = jnp.sum(jnp.where(hm, jnp.concatenate([oo, oo], axis=0), 0.0), axis=-1, keepdims=True)
            lse2 = jnp.concatenate([lv[:, 0:1], lv[:, HEAD_DIM:HEAD_DIM + 1]], axis=0)
            do2 = do2.astype(BF)
            k2 = kbuf[sl2, :].astype(BF)
            v2 = vbuf[sl2, :].astype(BF)
            s = lax.dot_general(q2, k2, (((1,), (1,)), ((), ())), preferred_element_type=F32)
            p = jnp.where(_band_mask((n > 0) | (sub > 0)), jnp.exp(s - lse2), 0.0)
            dp = lax.dot_general(do2, v2, (((1,), (1,)), ((), ())), preferred_element_type=F32)
            ds = (p * (dp - delta)).astype(BF)
            dq2 = jnp.dot(ds, k2, preferred_element_type=F32)
            dq = jnp.where(low, dq2[:BAND], dq2[BAND:])
            dq_ref[sl, :] = rope_bwd(dq, cos_ref[sl, :], ss_ref[sl, :], first) * scale
            dkbuf[sl2, :] += jnp.dot(ds.T, q2, preferred_element_type=F32)
            dvbuf[sl2, :] += jnp.dot(p.astype(BF).T, do2, preferred_element_type=F32)
            return carry

        lax.fori_loop(0, units, unit, 0)

        dk_ref[...] = rope_bwd(dkbuf[pr:pr + R, :], cos_ref[...], ss_ref[...], _first_half_mask(R))
        dv_ref[...] = dvbuf[pr:pr + R, :]
        ck_car[...] = dkbuf[0:pr, :]
        cv_car[...] = dvbuf[0:pr, :]

    blk = (R, LANES)
    pblk = (pr, LANES)
    cur = lambda hp, i: nblk - 1 - i
    prv = lambda hp, i: jnp.maximum((nblk - 1 - i) * per - 1, 0)
    in_specs = [
        pl.BlockSpec(blk, lambda hp, i: (cur(hp, i), g * HP + hp)),
        pl.BlockSpec(pblk, lambda hp, i: (prv(hp, i), g * HP + hp)),
        pl.BlockSpec(blk, lambda hp, i: (cur(hp, i), g * HP + hp)),
        pl.BlockSpec(pblk, lambda hp, i: (prv(hp, i), (NB + g) * HP + hp)),
        pl.BlockSpec(blk, lambda hp, i: (cur(hp, i), (NB + g) * HP + hp)),
        pl.BlockSpec(blk, lambda hp, i: (cur(hp, i), hp)),
        pl.BlockSpec(blk, lambda hp, i: (cur(hp, i), hp)),
        pl.BlockSpec(blk, lambda hp, i: (cur(hp, i), hp)),
        pl.BlockSpec(blk, lambda hp, i: (cur(hp, i), 0)),
        pl.BlockSpec(blk, lambda hp, i: (cur(hp, i), 0)),
    ]
    args = [q_all, kv_all, kv_all, kv_all, kv_all, do, o, lse, cos2, ss2]
    aliases = {}
    if prev is not None:
        in_specs += [ANY, ANY, ANY]
        aliases = {len(args): 0, len(args) + 1: 1, len(args) + 2: 2}
        args += list(prev)
    wide = jax.ShapeDtypeStruct((T, NB * Dm), F32)
    out = pl.BlockSpec(blk, lambda hp, i: (cur(hp, i), g * HP + hp))
    return pl.pallas_call(
        body, name=name,
        grid=(HP, nblk),
        in_specs=in_specs,
        out_specs=[out, out, out],
        out_shape=[wide, wide, wide],
        scratch_shapes=[pltpu.VMEM((pr + R, LANES), F32), pltpu.VMEM((pr + R, LANES), F32),
                        pltpu.VMEM((pr + R, LANES), F32), pltpu.VMEM((pr + R, LANES), F32),
                        pltpu.VMEM(pblk, F32), pltpu.VMEM(pblk, F32)],
        input_output_aliases=aliases,
        compiler_params=_cp("arbitrary", "arbitrary"),
    )(*args)


def _postnorm_bwd(dh, y, g_ref_val):
    r = lax.rsqrt(jnp.mean(y * y, axis=-1, keepdims=True) + RMS_EPS)
    yn = y * r
    dyn = dh * g_ref_val
    dy = r * (dyn - yn * jnp.mean(dyn * yn, axis=-1, keepdims=True))
    return dy, yn


def _postnorm_bwd_matmul(dh, y, gain, w3, widx, *, name, da_dtype):
    T, D = dh.shape
    K = w3.shape[1]
    tm = min(ROW_BLOCK, T)

    def body(dh_ref, y_ref, g_ref, w_ref, dy_ref, da_ref, dg_ref):
        i = pl.program_id(0)

        @pl.when(i == 0)
        def _():
            dg_ref[...] = jnp.zeros_like(dg_ref)

        dhv = dh_ref[...]
        dy, yn = _postnorm_bwd(dhv, y_ref[...].astype(F32), g_ref[...])
        dg_ref[...] += jnp.sum(dhv * yn, axis=0, keepdims=True)
        dyb = dy.astype(BF)
        dy_ref[...] = dyb
        da = lax.dot_general(dyb, w_ref[...], (((1,), (1,)), ((), ())), preferred_element_type=F32)
        da_ref[...] = da.astype(da_dtype)

    return pl.pallas_call(
        body, name=name,
        grid=(T // tm,),
        in_specs=[
            pl.BlockSpec((tm, D), lambda i: (i, 0)),
            pl.BlockSpec((tm, D), lambda i: (i, 0)),
            pl.BlockSpec((1, D), lambda i: (0, 0)),
            pl.BlockSpec((None, K, D), lambda i: (widx, 0, 0)),
        ],
        out_specs=[pl.BlockSpec((tm, D), lambda i: (i, 0)),
                   pl.BlockSpec((tm, K), lambda i: (i, 0)),
                   pl.BlockSpec((1, D), lambda i: (0, 0))],
        out_shape=[jax.ShapeDtypeStruct((T, D), BF), jax.ShapeDtypeStruct((T, K), da_dtype),
                   jax.ShapeDtypeStruct((1, D), F32)],
        compiler_params=_cp("arbitrary"),
    )(dh, y, gain, w3)


def _postnorm_bwd_swiglu(dh, y, gain, wd3, layer, g, u, *, name):
    T, D = dh.shape
    F = wd3.shape[1]
    nf = F // 2
    tm = min(ROW_BLOCK, T)

    def body(dh_ref, y_ref, g_ref, w_ref, gg_ref, uu_ref, dy_ref, dgo_ref, duo_ref, dgain_ref, dys):
        i = pl.program_id(0)
        j = pl.program_id(1)

        @pl.when((i == 0) & (j == 0))
        def _():
            dgain_ref[...] = jnp.zeros_like(dgain_ref)

        @pl.when(j == 0)
        def _():
            dhv = dh_ref[...]
            dy, yn = _postnorm_bwd(dhv, y_ref[...].astype(F32), g_ref[...])
            dgain_ref[...] += jnp.sum(dhv * yn, axis=0, keepdims=True)
            dyb = dy.astype(BF)
            dys[...] = dyb
            dy_ref[...] = dyb

        da = lax.dot_general(dys[...], w_ref[...], (((1,), (1,)), ((), ())), preferred_element_type=F32)
        gv = gg_ref[...].astype(F32)
        uv = uu_ref[...].astype(F32)
        sg = jax.nn.sigmoid(gv)
        silu = gv * sg
        dgo_ref[...] = (da * uv * (sg + silu * (1.0 - sg))).astype(BF)
        duo_ref[...] = (da * silu).astype(BF)

    act = jax.ShapeDtypeStruct((T, F), BF)
    return pl.pallas_call(
        body, name=name,
        grid=(T // tm, 2),
        in_specs=[
            pl.BlockSpec((tm, D), lambda i, j: (i, 0)),
            pl.BlockSpec((tm, D), lambda i, j: (i, 0)),
            pl.BlockSpec((1, D), lambda i, j: (0, 0)),
            pl.BlockSpec((None, nf, D), lambda i, j: (layer, j, 0)),
            pl.BlockSpec((tm, nf), lambda i, j: (i, j)),
            pl.BlockSpec((tm, nf), lambda i, j: (i, j)),
        ],
        out_specs=[pl.BlockSpec((tm, D), lambda i, j: (i, 0)),
                   pl.BlockSpec((tm, nf), lambda i, j: (i, j)),
                   pl.BlockSpec((tm, nf), lambda i, j: (i, j)),
                   pl.BlockSpec((1, D), lambda i, j: (0, 0))],
        out_shape=[jax.ShapeDtypeStruct((T, D), BF), act, act, jax.ShapeDtypeStruct((1, D), F32)],
        scratch_shapes=[pltpu.VMEM((tm, D), BF)],
        compiler_params=_cp("arbitrary", "arbitrary"),
    )(dh, y, gain, wd3, g, u)


def _matmul_prenorm_bwd(dzs, wg, layer, h, gain, dh_in, *, name):
    T, D = h.shape
    n = wg.shape[2]
    tm = min(ROW_BLOCK, T)
    pair = len(dzs) == 2
    nj = N_CHIPS // 2 if pair else N_CHIPS

    def body(*refs):
        dz_refs = refs[:len(dzs)]
        w_refs = refs[len(dzs):2 * len(dzs)]
        h_ref, g_ref, dhi_ref, dh_ref, dg_ref, acc = refs[2 * len(dzs):]
        i = pl.program_id(0)
        j = pl.program_id(1)

        @pl.when((i == 0) & (j == 0))
        def _():
            dg_ref[...] = jnp.zeros_like(dg_ref)

        part = None
        for dz_ref, w_ref in zip(dz_refs, w_refs):
            t = lax.dot_general(dz_ref[...].astype(BF), w_ref[...], (((1,), (1,)), ((), ())),
                                preferred_element_type=F32)
            part = t if part is None else part + t

        @pl.when(j == 0)
        def _():
            acc[...] = part

        @pl.when(j > 0)
        def _():
            acc[...] += part

        @pl.when(j == nj - 1)
        def _():
            dhn = acc[...]
            hv = h_ref[...]
            r = lax.rsqrt(jnp.mean(hv * hv, axis=-1, keepdims=True) + RMS_EPS)
            xh = hv * r
            dg_ref[...] += jnp.sum(dhn * xh, axis=0, keepdims=True)
            dxn = dhn * g_ref[...]
            dh_ref[...] = dhi_ref[...] + r * (dxn - xh * jnp.mean(dxn * xh, axis=-1, keepdims=True))

    in_specs = [pl.BlockSpec((tm, n), lambda i, j: (i, j)) for _ in dzs]
    if pair:
        in_specs += [pl.BlockSpec((None, D, n), lambda i, j: (j, layer, 0)),
                     pl.BlockSpec((None, D, n), lambda i, j: (j + nj, layer, 0))]
    else:
        in_specs += [pl.BlockSpec((None, D, n), lambda i, j: (j, layer, 0))]
    in_specs += [pl.BlockSpec((tm, D), lambda i, j: (i, 0)),
                 pl.BlockSpec((1, D), lambda i, j: (0, 0)),
                 pl.BlockSpec((tm, D), lambda i, j: (i, 0))]
    return pl.pallas_call(
        body, name=name,
        grid=(T // tm, nj),
        in_specs=in_specs,
        out_specs=[pl.BlockSpec((tm, D), lambda i, j: (i, 0)), pl.BlockSpec((1, D), lambda i, j: (0, 0))],
        out_shape=[jax.ShapeDtypeStruct((T, D), F32), jax.ShapeDtypeStruct((1, D), F32)],
        scratch_shapes=[pltpu.VMEM((tm, D), F32)],
        compiler_params=_cp("arbitrary", "arbitrary"),
    )(*dzs, *([wg] * len(dzs)), h, gain, dh_in)


def _grad_matmul(a, b, out_shape3, tme, tne, out_index, prev, *, name):
    T, M = a.shape
    N = b.shape[1]
    tk = min(ROW_BLOCK, T)
    nk = T // tk

    def body(a_ref, b_ref, *rest):
        o_ref, acc = rest[-2:]
        k = pl.program_id(2)
        part = jnp.dot(a_ref[...].astype(BF).T, b_ref[...].astype(BF), preferred_element_type=F32)

        @pl.when(k == 0)
        def _():
            acc[...] = part

        @pl.when(k > 0)
        def _():
            acc[...] += part

        @pl.when(k == nk - 1)
        def _():
            o_ref[...] = acc[...].astype(BF)

    in_specs = [pl.BlockSpec((tk, tme), lambda i, j, k: (k, i)),
                pl.BlockSpec((tk, tne), lambda i, j, k: (k, j))]
    args = [a, b]
    aliases = {}
    if prev is not None:
        in_specs.append(ANY)
        args.append(prev)
        aliases = {2: 0}
    return pl.pallas_call(
        body, name=name,
        grid=(M // tme, N // tne, nk),
        in_specs=in_specs,
        out_specs=pl.BlockSpec((None, tme, tne), lambda i, j, k: out_index(i, j)),
        out_shape=jax.ShapeDtypeStruct(out_shape3, BF),
        scratch_shapes=[pltpu.VMEM((tme, tne), F32)],
        input_output_aliases=aliases,
        compiler_params=_cp("parallel", "parallel", "arbitrary"),
    )(*args)


def _row_tile(R):
    for t in (512, 256, 128, 64, 32, 16, 8):
        if R % t == 0:
            return t
    return R


def _cast_place(w2d, half_major, where, dtype, *, name):
    R, C = w2d.shape
    rh = R // 2 if half_major else R
    tr = _row_tile(rh)
    nr = rh // tr

    def body(s_ref, w_ref, o_ref):
        o_ref[...] = w_ref[...].astype(o_ref.dtype)

    if half_major:
        out_spec = pl.BlockSpec((None, None, tr, C), lambda h, i, s: (h, s[0], i, 0))
        shape = (2, N_CHIPS, rh, C)
    else:
        out_spec = pl.BlockSpec((None, tr, C), lambda h, i, s: (s[0], i, 0))
        shape = (N_CHIPS, R, C)
    return pl.pallas_call(
        body, name=name,
        grid_spec=pltpu.PrefetchScalarGridSpec(
            num_scalar_prefetch=1, grid=(R // rh, nr),
            in_specs=[pl.BlockSpec((tr, C), lambda h, i, s: (h * nr + i, 0))],
            out_specs=out_spec),
        out_shape=jax.ShapeDtypeStruct(shape, dtype),
        compiler_params=_cp("arbitrary", "arbitrary"),
    )(where, w2d)


def _pair_sum(dw, theirs, half_major, where, *, name):
    G, rh, C = theirs.shape
    tr = _row_tile(rh)
    nr = rh // tr

    def body(s_ref, a_ref, b_ref, o_ref):
        o_ref[...] = (a_ref[...].astype(F32) + b_ref[...].astype(F32)).astype(BF)

    if half_major:
        mine = pl.BlockSpec((None, None, tr, C), lambda g, i, s: (s[1], g, i, 0))
    else:
        mine = pl.BlockSpec((None, tr, C), lambda g, i, s: (g, s[1] * nr + i, 0))
    spec = pl.BlockSpec((None, tr, C), lambda g, i, s: (g, i, 0))
    return pl.pallas_call(
        body, name=name,
        grid_spec=pltpu.PrefetchScalarGridSpec(
            num_scalar_prefetch=1, grid=(G, nr), in_specs=[mine, spec], out_specs=spec),
        out_shape=jax.ShapeDtypeStruct((G, rh, C), BF),
        compiler_params=_cp("arbitrary", "arbitrary"),
    )(where, dw, theirs)


def _chip_sum(landed, parts, where, *, name):
    G, rh, C = landed.shape
    tr = _row_tile(rh)
    nr = rh // tr

    def body(s_ref, l_ref, p_ref, o_ref):
        for j in range(G):
            def own(j=j):
                v = p_ref[...].astype(F32)
                o_ref[...] = v if j == 0 else o_ref[...] + v

            def other(j=j):
                v = l_ref[j].astype(F32)
                o_ref[...] = v if j == 0 else o_ref[...] + v

            pl.when(s_ref[0] == j)(own)
            pl.when(s_ref[0] != j)(other)

    return pl.pallas_call(
        body, name=name,
        grid_spec=pltpu.PrefetchScalarGridSpec(
            num_scalar_prefetch=1, grid=(nr,),
            in_specs=[pl.BlockSpec((G, tr, C), lambda i, s: (0, i, 0)),
                      pl.BlockSpec((None, tr, C), lambda i, s: (s[0], i, 0))],
            out_specs=pl.BlockSpec((tr, C), lambda i, s: (s[1] * nr + i, 0))),
        out_shape=jax.ShapeDtypeStruct((2 * rh, C), F32),
        compiler_params=_cp("arbitrary"),
    )(where, landed, parts)


def _adamw(w, g, m, v, *, name):
    R, C = w.shape
    tr = _row_tile(R)

    def body(w_ref, g_ref, m_ref, v_ref, d_ref, mo_ref, vo_ref):
        gv = g_ref[...]
        mn = ADAM_B1 * m_ref[...] + (1.0 - ADAM_B1) * gv
        vn = ADAM_B2 * v_ref[...] + (1.0 - ADAM_B2) * jnp.square(gv)
        m_hat = mn / (1.0 - ADAM_B1 ** ADAM_STEP)
        v_hat = vn / (1.0 - ADAM_B2 ** ADAM_STEP)
        d_ref[...] = -ADAM_LR * (m_hat / (jnp.sqrt(v_hat) + ADAM_EPS) + ADAM_WD * w_ref[...])
        mo_ref[...] = mn
        vo_ref[...] = vn

    spec = pl.BlockSpec((tr, C), lambda i: (i, 0))
    shp = jax.ShapeDtypeStruct((R, C), F32)
    return pl.pallas_call(
        body, name=name, grid=(R // tr,), in_specs=[spec] * 4, out_specs=[spec] * 3,
        out_shape=[shp, shp, shp], compiler_params=_cp("parallel"),
    )(w, g, m, v)


def _place():
    x = lax.axis_index("x")
    y = lax.axis_index("y")
    c = lax.axis_index("c")
    chips = [(1 - x, y), (x, 1 - y), (1 - x, 1 - y)]
    return x, y, c, chips


def _half_of(ref, j, h, rh, half_major, r0=0, rc=None):
    rc = rh if rc is None else rc
    return ref.at[h, j, pl.ds(r0, rc)] if half_major else ref.at[j, pl.ds(h * rh + r0, rc)]


def _chunk_rows(rows, row_bytes, align):
    if rows <= align:
        return rows
    cands = [r for r in range(align, rows + 1, align) if rows % r == 0]
    fit = [r for r in cands if r * row_bytes <= DMA_CHUNK_BYTES]
    return max(fit) if fit else min(cands)


def _row_align(dtype):
    return 8 * (4 // jnp.dtype(dtype).itemsize)


def _start_chunks(make, rows, rc):
    for r0 in range(0, rows, rc):
        make(r0, rc).start()


def _all_gather_weights(placed, half_major, small):
    nm = len(placed)
    halves = []
    for p, hmaj in zip(placed, half_major):
        halves.append((p.shape[2], p.shape[3]) if hmaj else (p.shape[1] // 2, p.shape[2]))
    out_shapes = [jax.ShapeDtypeStruct(p.shape, p.dtype) for p in placed]
    out_shapes.append(jax.ShapeDtypeStruct(small.shape, small.dtype))

    def body(*refs):
        outs = refs[nm + 1:2 * nm + 1]
        sm_out = refs[2 * nm + 1]
        ssem, rsem, fsem, gsem, sm_s, sm_r = refs[2 * nm + 2:]
        x, y, c, chips = _place()
        myj = 2 * x + y
        sib = (x, y, 1 - c)

        def spot(m, j, h, r0=0, rc=None):
            return _half_of(outs[m], j, h, halves[m][0], half_major[m], r0, rc)

        def chunk(m):
            rh, cols = halves[m]
            return rh, _chunk_rows(rh, cols * placed[m].dtype.itemsize, _row_align(placed[m].dtype))

        first = []
        for m in range(nm):
            rh, rc = chunk(m)
            for k, (px, py) in enumerate(chips):
                def send(r0, n, m=m, k=k, px=px, py=py):
                    part = spot(m, myj, c, r0, n)
                    return pltpu.make_async_remote_copy(
                        src_ref=part, dst_ref=part,
                        send_sem=ssem.at[3 * m + k], recv_sem=rsem.at[3 * m + k],
                        device_id=(px, py, c), device_id_type=MESH)
                _start_chunks(send, rh, rc)
                first.append(send(0, rh))
        for k, (px, py) in enumerate(chips):
            cp = pltpu.make_async_remote_copy(
                src_ref=sm_out.at[myj], dst_ref=sm_out.at[myj], send_sem=sm_s.at[k], recv_sem=sm_r.at[k],
                device_id=(px, py, c), device_id_type=MESH)
            cp.start()
            first.append(cp)

        passed = []
        for m in range(nm):
            rh, rc = chunk(m)
            for k, (px, py) in enumerate(chips):
                pj = 2 * px + py
                landed = spot(m, pj, c)
                pltpu.make_async_remote_copy(
                    src_ref=landed, dst_ref=landed, send_sem=ssem.at[3 * m + k], recv_sem=rsem.at[3 * m + k],
                    device_id=(px, py, c), device_id_type=MESH).wait_recv()

                def fwd(r0, n, m=m, k=k, pj=pj):
                    part = spot(m, pj, c, r0, n)
                    return pltpu.make_async_remote_copy(
                        src_ref=part, dst_ref=part, send_sem=fsem.at[3 * m + k], recv_sem=gsem.at[3 * m + k],
                        device_id=sib, device_id_type=MESH)
                _start_chunks(fwd, rh, rc)
                passed.append(fwd(0, rh))
        for m in range(nm):
            for k, (px, py) in enumerate(chips):
                theirs = spot(m, 2 * px + py, 1 - c)
                pltpu.make_async_remote_copy(
                    src_ref=theirs, dst_ref=theirs, send_sem=fsem.at[3 * m + k], recv_sem=gsem.at[3 * m + k],
                    device_id=sib, device_id_type=MESH).wait_recv()
        for k, (px, py) in enumerate(chips):
            got = sm_out.at[2 * px + py]
            pltpu.make_async_remote_copy(
                src_ref=got, dst_ref=got, send_sem=sm_s.at[k], recv_sem=sm_r.at[k],
                device_id=(px, py, c), device_id_type=MESH).wait_recv()
        for cp in first + passed:
            cp.wait_send()

    return pl.pallas_call(
        body, name="ag_weights",
        in_specs=[ANY] * (nm + 1),
        out_specs=[ANY] * (nm + 1),
        out_shape=out_shapes,
        input_output_aliases={i: i for i in range(nm + 1)},
        scratch_shapes=[pltpu.SemaphoreType.DMA((3 * nm,)), pltpu.SemaphoreType.DMA((3 * nm,)),
                        pltpu.SemaphoreType.DMA((3 * nm,)), pltpu.SemaphoreType.DMA((3 * nm,)),
                        pltpu.SemaphoreType.DMA((3,)), pltpu.SemaphoreType.DMA((3,))],
    )(*placed, small)


def _sibling_swap(dws, half_major):
    nm = len(dws)
    shapes = []
    for dw, hmaj in zip(dws, half_major):
        if hmaj:
            _, G, rh, C = dw.shape
        else:
            G, R, C = dw.shape
            rh = R // 2
        shapes.append(jax.ShapeDtypeStruct((G, rh, C), dw.dtype))

    def body(*refs):
        ins = refs[:nm]
        theirs = refs[nm:2 * nm]
        ssem, rsem = refs[2 * nm:]
        x, y, c, _ = _place()
        sib = (x, y, 1 - c)

        def half(m, h):
            rh = shapes[m].shape[1]
            return ins[m].at[h] if half_major[m] else ins[m].at[:, pl.ds(h * rh, rh), :]

        cps = []
        for m in range(nm):
            G, rh, cols = shapes[m].shape
            rc = _chunk_rows(rh, cols * shapes[m].dtype.itemsize, _row_align(shapes[m].dtype))
            for j in range(G):
                _start_chunks(lambda r0, n, m=m, j=j, rh=rh: pltpu.make_async_remote_copy(
                    src_ref=_half_of(ins[m], j, 1 - c, rh, half_major[m], r0, n),
                    dst_ref=theirs[m].at[j, pl.ds(r0, n)], send_sem=ssem.at[m], recv_sem=rsem.at[m],
                    device_id=sib, device_id_type=MESH), rh, rc)
            cps.append(pltpu.make_async_remote_copy(
                src_ref=half(m, 1 - c), dst_ref=theirs[m], send_sem=ssem.at[m], recv_sem=rsem.at[m],
                device_id=sib, device_id_type=MESH))
        for cp in cps:
            cp.wait()

    return pl.pallas_call(
        body, name="rs_sibling_swap",
        in_specs=[ANY] * nm, out_specs=[ANY] * nm, out_shape=shapes,
        scratch_shapes=[pltpu.SemaphoreType.DMA((nm,)), pltpu.SemaphoreType.DMA((nm,))],
    )(*dws)


def _chip_exchange(parts):
    nm = len(parts)
    shapes = [jax.ShapeDtypeStruct(p.shape, p.dtype) for p in parts]

    def body(*refs):
        ins = refs[:nm]
        outs = refs[nm:2 * nm]
        ssem, rsem = refs[2 * nm:]
        x, y, c, chips = _place()
        myj = 2 * x + y
        sent = []
        for m in range(nm):
            _, rh, cols = shapes[m].shape
            rc = _chunk_rows(rh, cols * shapes[m].dtype.itemsize, _row_align(shapes[m].dtype))
            for k, (px, py) in enumerate(chips):
                def send(r0, n, m=m, k=k, px=px, py=py):
                    return pltpu.make_async_remote_copy(
                        src_ref=ins[m].at[2 * px + py, pl.ds(r0, n)], dst_ref=outs[m].at[myj, pl.ds(r0, n)],
                        send_sem=ssem.at[3 * m + k], recv_sem=rsem.at[3 * m + k],
                        device_id=(px, py, c), device_id_type=MESH)
                _start_chunks(send, rh, rc)
                sent.append(send(0, rh))
        for m in range(nm):
            for k, (px, py) in enumerate(chips):
                got = outs[m].at[2 * px + py]
                pltpu.make_async_remote_copy(
                    src_ref=got, dst_ref=got, send_sem=ssem.at[3 * m + k], recv_sem=rsem.at[3 * m + k],
                    device_id=(px, py, c), device_id_type=MESH).wait_recv()
        for cp in sent:
            cp.wait_send()

    return pl.pallas_call(
        body, name="rs_chip_exchange",
        in_specs=[ANY] * nm, out_specs=[ANY] * nm, out_shape=shapes,
        scratch_shapes=[pltpu.SemaphoreType.DMA((3 * nm,)), pltpu.SemaphoreType.DMA((3 * nm,))],
    )(*parts)


def _sibling_join(grads):
    nm = len(grads)
    shapes = [jax.ShapeDtypeStruct(g.shape, g.dtype) for g in grads]

    def body(*refs):
        outs = refs[nm:2 * nm]
        ssem, rsem = refs[2 * nm:]
        x, y, c, _ = _place()
        sib = (x, y, 1 - c)
        cps = []
        for m in range(nm):
            rh, cols = grads[m].shape[0] // 2, grads[m].shape[1]
            rc = _chunk_rows(rh, cols * grads[m].dtype.itemsize, _row_align(grads[m].dtype))

            def send(r0, n, m=m, rh=rh):
                part = outs[m].at[pl.ds(c * rh + r0, n)]
                return pltpu.make_async_remote_copy(
                    src_ref=part, dst_ref=part, send_sem=ssem.at[m], recv_sem=rsem.at[m],
                    device_id=sib, device_id_type=MESH)
            _start_chunks(send, rh, rc)
            cps.append(send(0, rh))
        for m, cp in enumerate(cps):
            rh = grads[m].shape[0] // 2
            cp.wait_send()
            got = outs[m].at[pl.ds((1 - c) * rh, rh)]
            pltpu.make_async_remote_copy(
                src_ref=got, dst_ref=got, send_sem=ssem.at[m], recv_sem=rsem.at[m],
                device_id=sib, device_id_type=MESH).wait_recv()

    return pl.pallas_call(
        body, name="rs_sibling_join",
        in_specs=[ANY] * nm, out_specs=[ANY] * nm, out_shape=shapes,
        input_output_aliases={i: i for i in range(nm)},
        scratch_shapes=[pltpu.SemaphoreType.DMA((nm,)), pltpu.SemaphoreType.DMA((nm,))],
    )(*grads)


def _all_reduce_small(pack):
    R, C = pack.shape

    def body(in_ref, out_ref, slots, ssem, rsem):
        x, y, c, _ = _place()
        me = 4 * x + 2 * y + c
        slots[me] = in_ref[...]
        cps = []
        for k in range(1, N_DEV):
            dx, dy, dc = (k >> 2) & 1, (k >> 1) & 1, k & 1
            peer = (x ^ dx, y ^ dy, c ^ dc)
            cp = pltpu.make_async_remote_copy(
                src_ref=in_ref, dst_ref=slots.at[me], send_sem=ssem.at[k], recv_sem=rsem.at[k],
                device_id=peer, device_id_type=MESH)
            cp.start()
            cps.append(cp)
        for k in range(1, N_DEV):
            dx, dy, dc = (k >> 2) & 1, (k >> 1) & 1, k & 1
            got = slots.at[4 * (x ^ dx) + 2 * (y ^ dy) + (c ^ dc)]
            pltpu.make_async_remote_copy(
                src_ref=got, dst_ref=got, send_sem=ssem.at[k], recv_sem=rsem.at[k],
                device_id=(x ^ dx, y ^ dy, c ^ dc), device_id_type=MESH).wait_recv()
        for cp in cps:
            cp.wait_send()
        acc = slots[0]
        for s in range(1, N_DEV):
            acc = acc + slots[s]
        out_ref[...] = acc

    return pl.pallas_call(
        body, name="ar_small",
        in_specs=[pl.BlockSpec(memory_space=pltpu.VMEM)],
        out_specs=pl.BlockSpec(memory_space=pltpu.VMEM),
        out_shape=jax.ShapeDtypeStruct((R, C), F32),
        scratch_shapes=[pltpu.VMEM((N_DEV, R, C), F32),
                        pltpu.SemaphoreType.DMA((N_DEV,)), pltpu.SemaphoreType.DMA((N_DEV,))],
    )(pack)


def kernel(x, positions, mix_norm_pre, mix_norm_post, ffn_norm_pre, ffn_norm_post, ffn_w_gate_up, ffn_w_down, conv_w_in, conv_w, conv_w_out, kv_norm, w_kv, w_q, w_o, loss_target, m_mix_norm_pre, m_mix_norm_post, m_ffn_norm_pre, m_ffn_norm_post, m_ffn_w_gate_up, m_ffn_w_down, m_conv_w_in, m_conv_w, m_conv_w_out, m_kv_norm, m_w_kv, m_w_q, m_w_o, v_mix_norm_pre, v_mix_norm_post, v_ffn_norm_pre, v_ffn_norm_post, v_ffn_w_gate_up, v_ffn_w_down, v_conv_w_in, v_conv_w, v_conv_w_out, v_kv_norm, v_w_kv, v_w_q, v_w_o):
    T, D = x.shape[1], x.shape[2]
    L = ffn_w_gate_up.shape[0]
    n_gu = ffn_w_gate_up.shape[2]
    f_sh = ffn_w_down.shape[1]
    F = N_CHIPS * f_sh
    x0 = x[0]
    tgt = loss_target[0]

    half = HEAD_DIM // 2
    inv_freq = ROPE_THETA ** (-jnp.arange(half, dtype=F32) / half)
    ang = positions[0].astype(F32)[:, None] * inv_freq
    cosv, sinv = jnp.cos(ang), jnp.sin(ang)
    cos2 = jnp.tile(cosv, (1, LANES // half))
    ss2 = jnp.tile(jnp.concatenate([-sinv, sinv], axis=1), (1, LANES // HEAD_DIM))

    def as2d(a):
        return a.reshape(-1, a.shape[-1])

    big = [ffn_w_gate_up, ffn_w_down, conv_w_in, conv_w_out, w_kv, w_q, w_o]
    big_m = [m_ffn_w_gate_up, m_ffn_w_down, m_conv_w_in, m_conv_w_out, m_w_kv, m_w_q, m_w_o]
    big_v = [v_ffn_w_gate_up, v_ffn_w_down, v_conv_w_in, v_conv_w_out, v_w_kv, v_w_q, v_w_o]
    half_major = [False, True, False, False, False, False, False]
    chip = 2 * lax.axis_index("x") + lax.axis_index("y")
    where = jnp.stack([chip, lax.axis_index("c")]).astype(jnp.int32)
    tc = conv_w.shape[2]
    cw_pad = jnp.concatenate([conv_w[0], jnp.zeros((8 - conv_w.shape[1], tc), F32)], axis=0)

    placed = [_cast_place(as2d(w), hm, where, BF, name=f"place{i}")
              for i, (w, hm) in enumerate(zip(big, half_major))]
    gathered = _all_gather_weights(placed, half_major, _cast_place(cw_pad, False, where, F32, name="place_taps"))
    wgu, wd4, wci, wco4, wkv, wq, wo4, cw = gathered
    wd = wd4.reshape(L, F, D)
    wco = wco4.reshape(1, D, D)
    wo = wo4.reshape(1, D, D)

    def row(a, i):
        return a[i:i + 1]

    z, hn_m0 = _norm_matmul(x0, row(mix_norm_pre, 0), wci, cos2, ss2, name="f0_conv_in",
                            rope_shards=0, scale=1.0, out_dtype=BF)
    vmix = _conv_fwd(z, cw, name="f0_conv")
    y0, h1 = _matmul_postnorm(vmix, wco, 0, row(mix_norm_post, 0), x0, name="f0_conv_out")
    g0, u0, a0, hn_f0 = _norm_swiglu(h1, row(ffn_norm_pre, 0), wgu, 0, name="f0_gate_up")
    f0, h2 = _matmul_postnorm(a0, wd, 0, row(ffn_norm_post, 0), h1, name="f0_down")

    kv_all, hn_kv = _norm_matmul(h2, kv_norm.reshape(1, D), wkv, cos2, ss2, name="f1_kv",
                                 rope_shards=N_CHIPS // 2, scale=1.0, out_dtype=F32)
    q_all, hn_m1 = _norm_matmul(h2, row(mix_norm_pre, 1), wq, cos2, ss2, name="f1_q",
                                rope_shards=N_CHIPS, scale=HEAD_DIM ** -0.5, out_dtype=F32)
    o_att, lse = _attn_fwd(q_all, kv_all, name="f1_attn")
    y1, h3 = _matmul_postnorm(o_att, wo, 0, row(mix_norm_post, 1), h2, name="f1_attn_out")
    g1, u1, a1, hn_f1 = _norm_swiglu(h3, row(ffn_norm_pre, 1), wgu, 1, name="f1_gate_up")
    f1, h4 = _matmul_postnorm(a1, wd, 1, row(ffn_norm_post, 1), h3, name="f1_down")

    dh4, sq = _loss_head(h4, tgt, name="loss_head")
    loss_part = 0.5 * sq[0, 0] / D

    gu_shape = (N_CHIPS, L * D, n_gu)
    dyf1, dg1, du1, d_ffn_post1 = _postnorm_bwd_swiglu(dh4, f1, row(ffn_norm_post, 1), wd, 1, g1, u1,
                                                       name="b1_down")
    dwd = _grad_matmul(a1, dyf1, (L, F, D), F // 2, D, lambda i, j: (1, i, 0), None, name="b1_dw_down")
    dwgu = _grad_matmul(hn_f1, dg1, gu_shape, D, n_gu, lambda i, j: (j, 1, 0), None, name="b1_dw_gate")
    dwgu = _grad_matmul(hn_f1, du1, gu_shape, D, n_gu, lambda i, j: (j + 2, 1, 0), dwgu, name="b1_dw_up")
    dh3, d_ffn_pre1 = _matmul_prenorm_bwd((dg1, du1), wgu, 1, h3, row(ffn_norm_pre, 1), dh4, name="b1_gate_up")

    dy1, do, d_mix_post1 = _postnorm_bwd_matmul(dh3, y1, row(mix_norm_post, 1), wo, 0, name="b1_attn_out",
                                                da_dtype=F32)
    dwo = _grad_matmul(o_att, dy1, (1, D, D), D, D, lambda i, j: (0, 0, 0), None, name="b1_dw_o")
    prev = None
    for gi, (window, dil) in enumerate(BRANCHES):
        prev = _attn_bwd(q_all, kv_all, do, o_att, lse, cos2, ss2, gi, dil, prev, name=f"b1_attn{gi}")
    dq_all, dk_all, dv_all = prev
    n_q = wq.shape[2]
    n_kv = wkv.shape[2]
    dwq = _grad_matmul(hn_m1, dq_all, (N_CHIPS, D, n_q), D, n_q, lambda i, j: (j, 0, 0), None, name="b1_dw_q")
    dwkv = _grad_matmul(hn_kv, dk_all, (N_CHIPS, D, n_kv), D, n_kv, lambda i, j: (j, 0, 0), None, name="b1_dw_k")
    dwkv = _grad_matmul(hn_kv, dv_all, (N_CHIPS, D, n_kv), D, n_kv, lambda i, j: (j + 2, 0, 0), dwkv, name="b1_dw_v")
    dh2, d_mix_pre1 = _matmul_prenorm_bwd((dq_all,), wq, 0, h2, row(mix_norm_pre, 1), dh3, name="b1_q")
    dh2, d_kv_norm = _matmul_prenorm_bwd((dk_all, dv_all), wkv, 0, h2, kv_norm.reshape(1, D), dh2, name="b1_kv")

    dyf0, dg0, du0, d_ffn_post0 = _postnorm_bwd_swiglu(dh2, f0, row(ffn_norm_post, 0), wd, 0, g0, u0,
                                                       name="b0_down")
    dwd = _grad_matmul(a0, dyf0, (L, F, D), F // 2, D, lambda i, j: (0, i, 0), dwd, name="b0_dw_down")
    dwgu = _grad_matmul(hn_f0, dg0, gu_shape, D, n_gu, lambda i, j: (j, 0, 0), dwgu, name="b0_dw_gate")
    dwgu = _grad_matmul(hn_f0, du0, gu_shape, D, n_gu, lambda i, j: (j + 2, 0, 0), dwgu, name="b0_dw_up")
    dh1, d_ffn_pre0 = _matmul_prenorm_bwd((dg0, du0), wgu, 0, h1, row(ffn_norm_pre, 0), dh2, name="b0_gate_up")

    dy0, dvmix, d_mix_post0 = _postnorm_bwd_matmul(dh1, y0, row(mix_norm_post, 0), wco, 0, name="b0_conv_out",
                                                   da_dtype=BF)
    dwco = _grad_matmul(vmix, dy0, (1, D, D), D, D, lambda i, j: (0, 0, 0), None, name="b0_dw_conv_out")
    dz, dcw = _conv_bwd(z, cw, dvmix, name="b0_conv")
    n_ci = wci.shape[2]
    dwci = _grad_matmul(hn_m0, dz, (N_CHIPS, D, n_ci), D, n_ci, lambda i, j: (j, 0, 0), None, name="b0_dw_conv_in")
    dx, d_mix_pre0 = _matmul_prenorm_bwd((dz,), wci, 0, x0, row(mix_norm_pre, 0), dh1, name="b0_conv_in")

    pack = jnp.concatenate([
        d_mix_pre0, d_mix_pre1, d_mix_post0, d_mix_post1, d_ffn_pre0, d_ffn_pre1, d_ffn_post0, d_ffn_post1,
        d_kv_norm, dcw[0:3], jnp.full((1, D), loss_part, F32),
        jnp.zeros((SMALL_ROWS - 13, D), F32)], axis=0)
    red = _all_reduce_small(pack)
    loss = red[12, 0]
    myj = 2 * lax.axis_index("x") + lax.axis_index("y")
    g_conv_w = lax.dynamic_slice(red, (9, myj * tc), (3, tc))

    zeros7 = jnp.zeros((SMALL_ROWS - 9, D), F32)
    w_small = jnp.concatenate([mix_norm_pre, mix_norm_post, ffn_norm_pre, ffn_norm_post, kv_norm.reshape(1, D), zeros7], axis=0)
    m_small = jnp.concatenate([m_mix_norm_pre, m_mix_norm_post, m_ffn_norm_pre, m_ffn_norm_post, m_kv_norm.reshape(1, D), zeros7], axis=0)
    v_small = jnp.concatenate([v_mix_norm_pre, v_mix_norm_post, v_ffn_norm_pre, v_ffn_norm_post, v_kv_norm.reshape(1, D), zeros7], axis=0)
    d_small, nm_small, nv_small = _adamw(w_small, red, m_small, v_small, name="adamw_small")

    pad5 = jnp.zeros((5, tc), F32)
    d_cw, nm_cw, nv_cw = _adamw(cw_pad, jnp.concatenate([g_conv_w, pad5], axis=0),
                                jnp.concatenate([m_conv_w[0], pad5], axis=0),
                                jnp.concatenate([v_conv_w[0], pad5], axis=0), name="adamw_conv_w")

    dws = [dwgu, dwd.reshape(2, N_CHIPS, f_sh, D), dwci, dwco.reshape(N_CHIPS, D // N_CHIPS, D), dwkv, dwq,
           dwo.reshape(N_CHIPS, D // N_CHIPS, D)]
    theirs = _sibling_swap(dws, half_major)
    parts = [_pair_sum(dw, t, hm, where, name=f"rs_pair_sum{i}")
             for i, (dw, t, hm) in enumerate(zip(dws, theirs, half_major))]
    landed = _chip_exchange(parts)
    halves = [_chip_sum(l, p, where, name=f"rs_chip_sum{i}") for i, (l, p) in enumerate(zip(landed, parts))]
    grads2d = _sibling_join(halves)

    big_out = []
    for i, (w, gr, m, v) in enumerate(zip(big, grads2d, big_m, big_v)):
        d_, m_, v_ = _adamw(as2d(w), gr, as2d(m), as2d(v), name=f"adamw{i}")
        big_out.append((gr.reshape(w.shape), d_.reshape(w.shape), m_.reshape(w.shape), v_.reshape(w.shape)))

    def small(a):
        return (a[0:2], a[2:4], a[4:6], a[6:8])

    def assemble(sm, cwv, kind):
        pre, post, fpre, fpost = small(sm)
        b = [t[kind] for t in big_out]
        return [pre, post, fpre, fpost, b[0], b[1], b[2], cwv[0:3].reshape(conv_w.shape), b[3],
                sm[8], b[4], b[5].reshape(w_q.shape), b[6].reshape(w_o.shape)]

    grads = assemble(red, jnp.concatenate([g_conv_w, pad5], axis=0), 0)
    deltas = assemble(d_small, d_cw, 1)
    new_m = assemble(nm_small, nm_cw, 2)
    new_v = assemble(nv_small, nv_cw, 3)
    return (loss, dx.reshape(x.shape), *grads, *deltas, *new_m, *new_v)
```

```python
import functools

import jax
import jax.numpy as jnp
from jax import lax
from jax.experimental import pallas as pl
from jax.experimental.pallas import tpu as pltpu

HEAD_DIM = 64
BAND = 128
BRANCHES = ((128, 1), (512, 4), (2048, 16))
ROPE_THETA = 10000.0
RMS_EPS = 1e-6
NEG_INF = -1e30
ADAM_LR = 0.001
ADAM_B1 = 0.9
ADAM_B2 = 0.999
ADAM_EPS = 1e-08
ADAM_WD = 0.01
ADAM_STEP = 10

N_CHIPS = 4
N_DEV = 8
LANES = 128
ROW_BLOCK = 512
ATTN_BLOCK_ROWS = 2048
VMEM_LIMIT = 56 * 1024 * 1024
SMALL_ROWS = 16
DMA_CHUNK_BYTES = 512 * 1024

BF = jnp.bfloat16
F32 = jnp.float32
MESH = pl.DeviceIdType.MESH
ANY = pl.BlockSpec(memory_space=pl.ANY)


def _cp(*sem):
    return pltpu.CompilerParams(dimension_semantics=sem, vmem_limit_bytes=VMEM_LIMIT)


def _rot_half(t, first):
    return jnp.where(first, pltpu.roll(t, 96, 1), pltpu.roll(t, 32, 1))


def _first_half_mask(rows):
    lane = lax.broadcasted_iota(jnp.int32, (rows, LANES), 1)
    return (lane % HEAD_DIM) < (HEAD_DIM // 2)


def _normed_rows(j, rows, x_ref, g_ref, xn_ref, xs, last_start, tm):
    @pl.when(j == 0)
    def _():
        xv = x_ref[...]
        r = lax.rsqrt(jnp.mean(xv * xv, axis=-1, keepdims=True) + RMS_EPS)
        xn = (xv * r * g_ref[...]).astype(BF)
        xs[rows, :] = xn
        xn_ref[...] = xn

    @pl.when(j > 0)
    def _():
        xn_ref[...] = xs[pl.ds(last_start, tm), :]


def _norm_matmul(x, gain, wg, cos2, ss2, *, name, rope_shards, scale, out_dtype):
    T, D = x.shape
    n = wg.shape[2]
    tm = min(ROW_BLOCK, T)

    ni = T // tm

    def body(x_ref, g_ref, w_ref, cos_ref, ss_ref, y_ref, xn_ref, xs):
        j = pl.program_id(0)
        rows = pl.ds(pl.multiple_of(pl.program_id(1) * tm, tm), tm)
        _normed_rows(j, rows, x_ref, g_ref, xn_ref, xs, (ni - 1) * tm, tm)
        acc = jnp.dot(xs[rows, :], w_ref[...], preferred_element_type=F32)

        def plain():
            y_ref[...] = acc.astype(out_dtype)

        def rope():
            cosv = cos_ref[...]
            ssv = ss_ref[...]
            first = _first_half_mask(tm)
            for ci in range(n // LANES):
                t = acc[:, ci * LANES:(ci + 1) * LANES]
                y = (t * cosv + _rot_half(t, first) * ssv) * scale
                y_ref[:, ci * LANES:(ci + 1) * LANES] = y.astype(out_dtype)

        if rope_shards == 0:
            plain()
        elif rope_shards == N_CHIPS:
            rope()
        else:
            pl.when(j < rope_shards)(rope)
            pl.when(j >= rope_shards)(plain)

    first_pass = lambda j, i: (jnp.where(j == 0, i, ni - 1), 0)
    return pl.pallas_call(
        body, name=name,
        grid=(N_CHIPS, ni),
        in_specs=[
            pl.BlockSpec((tm, D), first_pass),
            pl.BlockSpec((1, D), lambda j, i: (0, 0)),
            pl.BlockSpec((None, D, n), lambda j, i: (j, 0, 0)),
            pl.BlockSpec((tm, LANES), lambda j, i: (i, 0)),
            pl.BlockSpec((tm, LANES), lambda j, i: (i, 0)),
        ],
        out_specs=[
            pl.BlockSpec((tm, n), lambda j, i: (i, j)),
            pl.BlockSpec((tm, D), first_pass),
        ],
        out_shape=[jax.ShapeDtypeStruct((T, N_CHIPS * n), out_dtype),
                   jax.ShapeDtypeStruct((T, D), BF)],
        scratch_shapes=[pltpu.VMEM((T, D), BF)],
        compiler_params=_cp("arbitrary", "arbitrary"),
    )(x, gain, wg, cos2, ss2)


def _norm_swiglu(x, gain, wg, layer, *, name):
    T, D = x.shape
    n = wg.shape[2]
    tm = min(ROW_BLOCK, T)

    ni = T // tm

    def body(x_ref, g_ref, wg_ref, wu_ref, go_ref, uo_ref, ao_ref, xn_ref, xs):
        j = pl.program_id(0)
        rows = pl.ds(pl.multiple_of(pl.program_id(1) * tm, tm), tm)
        _normed_rows(j, rows, x_ref, g_ref, xn_ref, xs, (ni - 1) * tm, tm)
        g = jnp.dot(xs[rows, :], wg_ref[...], preferred_element_type=F32)
        u = jnp.dot(xs[rows, :], wu_ref[...], preferred_element_type=F32)
        go_ref[...] = g.astype(BF)
        uo_ref[...] = u.astype(BF)
        ao_ref[...] = (g * jax.nn.sigmoid(g) * u).astype(BF)

    half = N_CHIPS // 2
    first_pass = lambda j, i: (jnp.where(j == 0, i, ni - 1), 0)
    act = jax.ShapeDtypeStruct((T, half * n), BF)
    return pl.pallas_call(
        body, name=name,
        grid=(half, ni),
        in_specs=[
            pl.BlockSpec((tm, D), first_pass),
            pl.BlockSpec((1, D), lambda j, i: (0, 0)),
            pl.BlockSpec((None, D, n), lambda j, i: (j, layer, 0)),
            pl.BlockSpec((None, D, n), lambda j, i: (j + half, layer, 0)),
        ],
        out_specs=[
            pl.BlockSpec((tm, n), lambda j, i: (i, j)),
            pl.BlockSpec((tm, n), lambda j, i: (i, j)),
            pl.BlockSpec((tm, n), lambda j, i: (i, j)),
            pl.BlockSpec((tm, D), first_pass),
        ],
        out_shape=[act, act, act, jax.ShapeDtypeStruct((T, D), BF)],
        scratch_shapes=[pltpu.VMEM((T, D), BF)],
        compiler_params=_cp("arbitrary", "arbitrary"),
    )(x, gain, wg, wg)


def _matmul_postnorm(a, w3, widx, gain, h_old, *, name):
    T, K = a.shape
    D = w3.shape[2]
    tm = min(ROW_BLOCK, T)

    def body(a_ref, w_ref, g_ref, h_ref, y_ref, hn_ref):
        y = jnp.dot(a_ref[...].astype(BF), w_ref[...], preferred_element_type=F32)
        y_ref[...] = y.astype(BF)
        r = lax.rsqrt(jnp.mean(y * y, axis=-1, keepdims=True) + RMS_EPS)
        hn_ref[...] = h_ref[...] + y * r * g_ref[...]

    return pl.pallas_call(
        body, name=name,
        grid=(T // tm,),
        in_specs=[
            pl.BlockSpec((tm, K), lambda i: (i, 0)),
            pl.BlockSpec((None, K, D), lambda i: (widx, 0, 0)),
            pl.BlockSpec((1, D), lambda i: (0, 0)),
            pl.BlockSpec((tm, D), lambda i: (i, 0)),
        ],
        out_specs=[pl.BlockSpec((tm, D), lambda i: (i, 0)),
                   pl.BlockSpec((tm, D), lambda i: (i, 0))],
        out_shape=[jax.ShapeDtypeStruct((T, D), BF), jax.ShapeDtypeStruct((T, D), F32)],
        compiler_params=_cp("parallel"),
    )(a, w3, gain, h_old)


def _loss_head(h, target, *, name):
    T, D = h.shape
    tm = min(ROW_BLOCK, T)

    def body(h_ref, t_ref, dh_ref, s_ref):
        i = pl.program_id(0)

        @pl.when(i == 0)
        def _():
            s_ref[...] = jnp.zeros_like(s_ref)

        e = h_ref[...] - t_ref[...]
        dh_ref[...] = e * (1.0 / D)
        s_ref[...] += jnp.sum(e * e)

    return pl.pallas_call(
        body, name=name,
        grid=(T // tm,),
        in_specs=[pl.BlockSpec((tm, D), lambda i: (i, 0)), pl.BlockSpec((tm, D), lambda i: (i, 0))],
        out_specs=[pl.BlockSpec((tm, D), lambda i: (i, 0)), pl.BlockSpec((8, LANES), lambda i: (0, 0))],
        out_shape=[jax.ShapeDtypeStruct((T, D), F32), jax.ShapeDtypeStruct((8, LANES), F32)],
        compiler_params=_cp("arbitrary"),
    )(h, target)


def _shift_down(u, k):
    row = lax.broadcasted_iota(jnp.int32, u.shape, 0)
    return jnp.where(row >= k, pltpu.roll(u, k, 0), 0.0)


def _shift_up(u, k):
    T = u.shape[0]
    row = lax.broadcasted_iota(jnp.int32, u.shape, 0)
    return jnp.where(row < T - k, pltpu.roll(u, T - k, 0), 0.0)


def _conv_fwd(z, cw, *, name):
    T = z.shape[0]
    D = z.shape[1] // 3
    tc = cw.shape[2]
    nb = D // tc

    def body(b_ref, c_ref, h_ref, w_ref, o_ref):
        u = c_ref[...].astype(F32) * h_ref[...].astype(F32)
        w = w_ref[...]
        conv = w[2:3] * u + w[1:2] * _shift_down(u, 1) + w[0:1] * _shift_down(u, 2)
        o_ref[...] = (b_ref[...].astype(F32) * conv).astype(BF)

    return pl.pallas_call(
        body, name=name,
        grid=(nb,),
        in_specs=[
            pl.BlockSpec((T, tc), lambda j: (0, j)),
            pl.BlockSpec((T, tc), lambda j: (0, nb + j)),
            pl.BlockSpec((T, tc), lambda j: (0, 2 * nb + j)),
            pl.BlockSpec((None, 8, tc), lambda j: (j, 0, 0)),
        ],
        out_specs=pl.BlockSpec((T, tc), lambda j: (0, j)),
        out_shape=jax.ShapeDtypeStruct((T, D), BF),
        compiler_params=_cp("parallel"),
    )(z, z, z, cw)


def _conv_bwd(z, cw, dv, *, name):
    T = z.shape[0]
    D = z.shape[1] // 3
    tc = cw.shape[2]
    nb = D // tc

    def body(b_ref, c_ref, h_ref, w_ref, dv_ref, dz_ref, dw_ref):
        p = pl.program_id(0)
        c = c_ref[...].astype(F32)
        h = h_ref[...].astype(F32)
        u = c * h
        u1 = _shift_down(u, 1)
        u2 = _shift_down(u, 2)
        w = w_ref[...]
        dvv = dv_ref[...].astype(F32)
        dconv = dvv * b_ref[...].astype(F32)
        du = w[2:3] * dconv + w[1:2] * _shift_up(dconv, 1) + w[0:1] * _shift_up(dconv, 2)
        rows = lax.broadcasted_iota(jnp.int32, (8, tc), 0)
        dw = jnp.where(rows == 0, jnp.sum(dconv * u2, axis=0, keepdims=True),
                       jnp.where(rows == 1, jnp.sum(dconv * u1, axis=0, keepdims=True),
                                 jnp.where(rows == 2, jnp.sum(dconv * u, axis=0, keepdims=True), 0.0)))
        dw_ref[...] = dw

        @pl.when(p == 0)
        def _():
            conv = w[2:3] * u + w[1:2] * u1 + w[0:1] * u2
            dz_ref[...] = (dvv * conv).astype(BF)

        @pl.when(p == 1)
        def _():
            dz_ref[...] = (du * h).astype(BF)

        @pl.when(p == 2)
        def _():
            dz_ref[...] = (du * c).astype(BF)

    return pl.pallas_call(
        body, name=name,
        grid=(3, nb),
        in_specs=[
            pl.BlockSpec((T, tc), lambda p, j: (0, j)),
            pl.BlockSpec((T, tc), lambda p, j: (0, nb + j)),
            pl.BlockSpec((T, tc), lambda p, j: (0, 2 * nb + j)),
            pl.BlockSpec((None, 8, tc), lambda p, j: (j, 0, 0)),
            pl.BlockSpec((T, tc), lambda p, j: (0, j)),
        ],
        out_specs=[pl.BlockSpec((T, tc), lambda p, j: (0, p * nb + j)),
                   pl.BlockSpec((8, tc), lambda p, j: (p, j))],
        out_shape=[jax.ShapeDtypeStruct((T, 3 * D), BF), jax.ShapeDtypeStruct((3 * 8, D), F32)],
        compiler_params=_cp("arbitrary", "arbitrary"),
    )(z, z, z, cw, dv)


def _strided(base, count, d):
    return pl.ds(base, count, stride=d) if d > 1 else pl.ds(pl.multiple_of(base, BAND), count)


def _fill_band_bias(bias):
    qi = lax.broadcasted_iota(jnp.int32, (2 * BAND, 2 * BAND), 0) % BAND
    kj = lax.broadcasted_iota(jnp.int32, (2 * BAND, 2 * BAND), 1)
    dist = qi + BAND - kj
    band = (dist >= 0) & (dist <= BAND)
    bias[0] = jnp.where(band & (kj >= BAND), 0.0, NEG_INF)
    bias[1] = jnp.where(band, 0.0, NEG_INF)


def _attn_block_rows(T):
    return min(ATTN_BLOCK_ROWS, T)


def _head_mask():
    lane = lax.broadcasted_iota(jnp.int32, (2 * BAND, LANES), 1)
    row = lax.broadcasted_iota(jnp.int32, (2 * BAND, LANES), 0)
    return (lane < HEAD_DIM) == (row < BAND)


def _attn_fwd(q_all, kv_all, *, name):
    T = q_all.shape[0]
    NB = len(BRANCHES)
    Dm = q_all.shape[1] // NB
    HP = Dm // LANES
    R = _attn_block_rows(T)
    units = R // BAND
    dmax = max(d for _, d in BRANCHES)

    def body(*refs):
        ins = refs[:5 * NB]
        o_ref, l_ref, kbuf, vbuf, o_s, l_s, bias = refs[5 * NB:]
        n = pl.program_id(0)
        _fill_band_bias(bias)
        hm = _head_mask()
        low = lax.broadcasted_iota(jnp.int32, (BAND, LANES), 1) < HEAD_DIM

        for g, (_, d) in enumerate(BRANCHES):
            q_ref, kp_ref, kc_ref, vp_ref, vc_ref = ins[5 * g:5 * g + 5]
            pr = BAND * d
            kbuf[0:pr, :] = kp_ref[...]
            kbuf[pr:pr + R, :] = kc_ref[...]
            vbuf[0:pr, :] = vp_ref[...]
            vbuf[pr:pr + R, :] = vc_ref[...]

            def unit(u, carry, g=g, d=d, pr=pr, q_ref=q_ref):
                sub = u // d
                base = sub * pr + (u - sub * d)
                q = q_ref[_strided(base, BAND, d), :]
                q2 = jnp.where(hm, jnp.concatenate([q, q], axis=0), 0.0).astype(BF)
                k2 = kbuf[_strided(base, 2 * BAND, d), :].astype(BF)
                v2 = vbuf[_strided(base, 2 * BAND, d), :].astype(BF)
                s = lax.dot_general(q2, k2, (((1,), (1,)), ((), ())), preferred_element_type=F32)
                s = s + bias[((n > 0) | (sub > 0)).astype(jnp.int32)]
                m = jnp.max(s, axis=-1, keepdims=True)
                p = jnp.exp(s - m)
                l = jnp.sum(p, axis=-1, keepdims=True)
                pv = jnp.dot(p.astype(BF), v2, preferred_element_type=F32) / l
                lse = m + jnp.log(l)
                o_s[g, _strided(base, BAND, d), :] = jnp.where(low, pv[:BAND], pv[BAND:])
                l_s[g, _strided(base, BAND, d), :] = jnp.where(low, lse[:BAND], lse[BAND:])
                return carry

            lax.fori_loop(0, units, unit, 0, unroll=4)

        def merge(i, carry):
            sl = pl.ds(pl.multiple_of(i * BAND, BAND), BAND)
            lv = [l_s[g, sl, :] for g in range(NB)]
            m = functools.reduce(jnp.maximum, lv)
            e = [jnp.exp(v - m) for v in lv]
            tot = functools.reduce(jnp.add, e)
            o_ref[sl, :] = functools.reduce(jnp.add, [(e[g] / tot) * o_s[g, sl, :] for g in range(NB)])
            l_ref[sl, :] = m + jnp.log(tot)
            return carry

        lax.fori_loop(0, units, merge, 0)

    in_specs, args = [], []
    for g, (_, d) in enumerate(BRANCHES):
        per = R // (BAND * d)
        for col, rows, idx in (
                (g * HP, R, lambda n, hp: n),
                (g * HP, BAND * d, lambda n, hp, per=per: jnp.maximum(n * per - 1, 0)),
                (g * HP, R, lambda n, hp: n),
                ((NB + g) * HP, BAND * d, lambda n, hp, per=per: jnp.maximum(n * per - 1, 0)),
                ((NB + g) * HP, R, lambda n, hp: n)):
            in_specs.append(pl.BlockSpec((rows, LANES), lambda n, hp, col=col, idx=idx: (idx(n, hp), col + hp)))
        args += [q_all, kv_all, kv_all, kv_all, kv_all]
    out = pl.BlockSpec((R, LANES), lambda n, hp: (n, hp))
    return pl.pallas_call(
        body, name=name,
        grid=(T // R, HP),
        in_specs=in_specs,
        out_specs=[out, out],
        out_shape=[jax.ShapeDtypeStruct((T, Dm), F32), jax.ShapeDtypeStruct((T, Dm), F32)],
        scratch_shapes=[pltpu.VMEM((BAND * dmax + R, LANES), F32), pltpu.VMEM((BAND * dmax + R, LANES), F32),
                        pltpu.VMEM((NB, R, LANES), F32), pltpu.VMEM((NB, R, LANES), F32),
                        pltpu.VMEM((2, 2 * BAND, 2 * BAND), F32)],
        compiler_params=_cp("parallel", "parallel"),
    )(*args)


def _attn_bwd(q_all, kv_all, do, o, lse, cos2, ss2, g, d, prev, *, name):
    T = q_all.shape[0]
    NB = len(BRANCHES)
    Dm = q_all.shape[1] // NB
    HP = Dm // LANES
    R = _attn_block_rows(T)
    nblk = T // R
    units = R // BAND
    pr = BAND * d
    per = R // pr
    scale = HEAD_DIM ** -0.5

    def rope_bwd(t, cosv, ssv, first):
        return t * cosv - _rot_half(t, first) * ssv

    def body(q_ref, kp_ref, kc_ref, vp_ref, vc_ref, do_ref, o_ref, l_ref, cos_ref, ss_ref, *rest):
        dq_ref, dk_ref, dv_ref, kbuf, vbuf, dkbuf, dvbuf, ck_car, cv_car, bias = rest[-10:]
        i = pl.program_id(1)
        n = nblk - 1 - i
        _fill_band_bias(bias)
        first = _first_half_mask(BAND)
        hm = _head_mask()
        low = lax.broadcasted_iota(jnp.int32, (BAND, LANES), 1) < HEAD_DIM

        kbuf[0:pr, :] = kp_ref[...]
        kbuf[pr:pr + R, :] = kc_ref[...]
        vbuf[0:pr, :] = vp_ref[...]
        vbuf[pr:pr + R, :] = vc_ref[...]
        dkbuf[...] = jnp.zeros_like(dkbuf)
        dvbuf[...] = jnp.zeros_like(dvbuf)

        @pl.when(i > 0)
        def _():
            dkbuf[R:R + pr, :] = ck_car[...]
            dvbuf[R:R + pr, :] = cv_car[...]

        def unit(u, carry):
            sub = u // d
            base = sub * pr + (u - sub * d)
            sl = _strided(base, BAND, d)
            sl2 = _strided(base, 2 * BAND, d)
            q = q_ref[sl, :]
            dov = do_ref[sl, :]
            ov = o_ref[sl, :]
            lv = l_ref[sl, :]
            q2 = jnp.where(hm, jnp.concatenate([q, q], axis=0), 0.0).astype(BF)
            do2 = jnp.where(hm, jnp.concatenate([dov, dov], axis=0), 0.0)
            oo = dov * ov
            delta = jnp.sum(jnp.where(hm, jnp.concatenate([oo, oo], axis=0), 0.0), axis=-1, keepdims=True)
            lse2 = jnp.concatenate([lv[:, 0:1], lv[:, HEAD_DIM:HEAD_DIM + 1]], axis=0)
            do2 = do2.astype(BF)
            k2 = kbuf[sl2, :].astype(BF)
            v2 = vbuf[sl2, :].astype(BF)
            s = lax.dot_general(q2, k2, (((1,), (1,)), ((), ())), preferred_element_type=F32)
            p = jnp.exp(s + bias[((n > 0) | (sub > 0)).astype(jnp.int32)] - lse2)
            dp = lax.dot_general(do2, v2, (((1,), (1,)), ((), ())), preferred_element_type=F32)
            ds = (p * (dp - delta)).astype(BF)
            dq2 = jnp.dot(ds, k2, preferred_element_type=F32)
            dq = jnp.where(low, dq2[:BAND], dq2[BAND:])
            dq_ref[sl, :] = rope_bwd(dq, cos_ref[sl, :], ss_ref[sl, :], first) * scale
            dkbuf[sl2, :] += jnp.dot(ds.T, q2, preferred_element_type=F32)
            dvbuf[sl2, :] += jnp.dot(p.astype(BF).T, do2, preferred_element_type=F32)
            return carry

        lax.fori_loop(0, units, unit, 0, unroll=2)

        dk_ref[...] = rope_bwd(dkbuf[pr:pr + R, :], cos_ref[...], ss_ref[...], _first_half_mask(R))
        dv_ref[...] = dvbuf[pr:pr + R, :]
        ck_car[...] = dkbuf[0:pr, :]
        cv_car[...] = dvbuf[0:pr, :]

    blk = (R, LANES)
    pblk = (pr, LANES)
    cur = lambda hp, i: nblk - 1 - i
    prv = lambda hp, i: jnp.maximum((nblk - 1 - i) * per - 1, 0)
    in_specs = [
        pl.BlockSpec(blk, lambda hp, i: (cur(hp, i), g * HP + hp)),
        pl.BlockSpec(pblk, lambda hp, i: (prv(hp, i), g * HP + hp)),
        pl.BlockSpec(blk, lambda hp, i: (cur(hp, i), g * HP + hp)),
        pl.BlockSpec(pblk, lambda hp, i: (prv(hp, i), (NB + g) * HP + hp)),
        pl.BlockSpec(blk, lambda hp, i: (cur(hp, i), (NB + g) * HP + hp)),
        pl.BlockSpec(blk, lambda hp, i: (cur(hp, i), hp)),
        pl.BlockSpec(blk, lambda hp, i: (cur(hp, i), hp)),
        pl.BlockSpec(blk, lambda hp, i: (cur(hp, i), hp)),
        pl.BlockSpec(blk, lambda hp, i: (cur(hp, i), 0)),
        pl.BlockSpec(blk, lambda hp, i: (cur(hp, i), 0)),
    ]
    args = [q_all, kv_all, kv_all, kv_all, kv_all, do, o, lse, cos2, ss2]
    aliases = {}
    if prev is not None:
        in_specs += [ANY, ANY, ANY]
        aliases = {len(args): 0, len(args) + 1: 1, len(args) + 2: 2}
        args += list(prev)
    wide = jax.ShapeDtypeStruct((T, NB * Dm), F32)
    out = pl.BlockSpec(blk, lambda hp, i: (cur(hp, i), g * HP + hp))
    return pl.pallas_call(
        body, name=name,
        grid=(HP, nblk),
        in_specs=in_specs,
        out_specs=[out, out, out],
        out_shape=[wide, wide, wide],
        scratch_shapes=[pltpu.VMEM((pr + R, LANES), F32), pltpu.VMEM((pr + R, LANES), F32),
                        pltpu.VMEM((pr + R, LANES), F32), pltpu.VMEM((pr + R, LANES), F32),
                        pltpu.VMEM(pblk, F32), pltpu.VMEM(pblk, F32),
                        pltpu.VMEM((2, 2 * BAND, 2 * BAND), F32)],
        input_output_aliases=aliases,
        compiler_params=_cp("arbitrary", "arbitrary"),
    )(*args)


def _postnorm_bwd(dh, y, g_ref_val):
    r = lax.rsqrt(jnp.mean(y * y, axis=-1, keepdims=True) + RMS_EPS)
    yn = y * r
    dyn = dh * g_ref_val
    dy = r * (dyn - yn * jnp.mean(dyn * yn, axis=-1, keepdims=True))
    return dy, yn


def _postnorm_bwd_matmul(dh, y, gain, w3, widx, *, name, da_dtype):
    T, D = dh.shape
    K = w3.shape[1]
    tm = min(ROW_BLOCK, T)

    def body(dh_ref, y_ref, g_ref, w_ref, dy_ref, da_ref, dg_ref):
        i = pl.program_id(0)

        @pl.when(i == 0)
        def _():
            dg_ref[...] = jnp.zeros_like(dg_ref)

        dhv = dh_ref[...]
        dy, yn = _postnorm_bwd(dhv, y_ref[...].astype(F32), g_ref[...])
        dg_ref[...] += jnp.sum(dhv * yn, axis=0, keepdims=True)
        dyb = dy.astype(BF)
        dy_ref[...] = dyb
        da = lax.dot_general(dyb, w_ref[...], (((1,), (1,)), ((), ())), preferred_element_type=F32)
        da_ref[...] = da.astype(da_dtype)

    return pl.pallas_call(
        body, name=name,
        grid=(T // tm,),
        in_specs=[
            pl.BlockSpec((tm, D), lambda i: (i, 0)),
            pl.BlockSpec((tm, D), lambda i: (i, 0)),
            pl.BlockSpec((1, D), lambda i: (0, 0)),
            pl.BlockSpec((None, K, D), lambda i: (widx, 0, 0)),
        ],
        out_specs=[pl.BlockSpec((tm, D), lambda i: (i, 0)),
                   pl.BlockSpec((tm, K), lambda i: (i, 0)),
                   pl.BlockSpec((1, D), lambda i: (0, 0))],
        out_shape=[jax.ShapeDtypeStruct((T, D), BF), jax.ShapeDtypeStruct((T, K), da_dtype),
                   jax.ShapeDtypeStruct((1, D), F32)],
        compiler_params=_cp("arbitrary"),
    )(dh, y, gain, w3)


def _postnorm_bwd_swiglu(dh, y, gain, wd3, layer, g, u, *, name):
    T, D = dh.shape
    F = wd3.shape[1]
    nf = F // 2
    tm = min(ROW_BLOCK, T)

    def body(dh_ref, y_ref, g_ref, w_ref, gg_ref, uu_ref, dy_ref, dgo_ref, duo_ref, dgain_ref, dys):
        i = pl.program_id(0)
        j = pl.program_id(1)

        @pl.when((i == 0) & (j == 0))
        def _():
            dgain_ref[...] = jnp.zeros_like(dgain_ref)

        @pl.when(j == 0)
        def _():
            dhv = dh_ref[...]
            dy, yn = _postnorm_bwd(dhv, y_ref[...].astype(F32), g_ref[...])
            dgain_ref[...] += jnp.sum(dhv * yn, axis=0, keepdims=True)
            dyb = dy.astype(BF)
            dys[...] = dyb
            dy_ref[...] = dyb

        da = lax.dot_general(dys[...], w_ref[...], (((1,), (1,)), ((), ())), preferred_element_type=F32)
        gv = gg_ref[...].astype(F32)
        uv = uu_ref[...].astype(F32)
        sg = jax.nn.sigmoid(gv)
        silu = gv * sg
        dgo_ref[...] = (da * uv * (sg + silu * (1.0 - sg))).astype(BF)
        duo_ref[...] = (da * silu).astype(BF)

    act = jax.ShapeDtypeStruct((T, F), BF)
    return pl.pallas_call(
        body, name=name,
        grid=(T // tm, 2),
        in_specs=[
            pl.BlockSpec((tm, D), lambda i, j: (i, 0)),
            pl.BlockSpec((tm, D), lambda i, j: (i, 0)),
            pl.BlockSpec((1, D), lambda i, j: (0, 0)),
            pl.BlockSpec((None, nf, D), lambda i, j: (layer, j, 0)),
            pl.BlockSpec((tm, nf), lambda i, j: (i, j)),
            pl.BlockSpec((tm, nf), lambda i, j: (i, j)),
        ],
        out_specs=[pl.BlockSpec((tm, D), lambda i, j: (i, 0)),
                   pl.BlockSpec((tm, nf), lambda i, j: (i, j)),
                   pl.BlockSpec((tm, nf), lambda i, j: (i, j)),
                   pl.BlockSpec((1, D), lambda i, j: (0, 0))],
        out_shape=[jax.ShapeDtypeStruct((T, D), BF), act, act, jax.ShapeDtypeStruct((1, D), F32)],
        scratch_shapes=[pltpu.VMEM((tm, D), BF)],
        compiler_params=_cp("arbitrary", "arbitrary"),
    )(dh, y, gain, wd3, g, u)


def _matmul_prenorm_bwd(dzs, wg, layer, h, gain, dh_in, *, name):
    T, D = h.shape
    n = wg.shape[2]
    tm = min(ROW_BLOCK, T)
    pair = len(dzs) == 2
    nj = N_CHIPS // 2 if pair else N_CHIPS

    def body(*refs):
        dz_refs = refs[:len(dzs)]
        w_refs = refs[len(dzs):2 * len(dzs)]
        h_ref, g_ref, dhi_ref, dh_ref, dg_ref, acc = refs[2 * len(dzs):]
        i = pl.program_id(0)
        j = pl.program_id(1)

        @pl.when((i == 0) & (j == 0))
        def _():
            dg_ref[...] = jnp.zeros_like(dg_ref)

        part = None
        for dz_ref, w_ref in zip(dz_refs, w_refs):
            t = lax.dot_general(dz_ref[...].astype(BF), w_ref[...], (((1,), (1,)), ((), ())),
                                preferred_element_type=F32)
            part = t if part is None else part + t

        @pl.when(j == 0)
        def _():
            acc[...] = part

        @pl.when(j > 0)
        def _():
            acc[...] += part

        @pl.when(j == nj - 1)
        def _():
            dhn = acc[...]
            hv = h_ref[...]
            r = lax.rsqrt(jnp.mean(hv * hv, axis=-1, keepdims=True) + RMS_EPS)
            xh = hv * r
            dg_ref[...] += jnp.sum(dhn * xh, axis=0, keepdims=True)
            dxn = dhn * g_ref[...]
            dh_ref[...] = dhi_ref[...] + r * (dxn - xh * jnp.mean(dxn * xh, axis=-1, keepdims=True))

    in_specs = [pl.BlockSpec((tm, n), lambda i, j: (i, j)) for _ in dzs]
    if pair:
        in_specs += [pl.BlockSpec((None, D, n), lambda i, j: (j, layer, 0)),
                     pl.BlockSpec((None, D, n), lambda i, j: (j + nj, layer, 0))]
    else:
        in_specs += [pl.BlockSpec((None, D, n), lambda i, j: (j, layer, 0))]
    in_specs += [pl.BlockSpec((tm, D), lambda i, j: (i, 0)),
                 pl.BlockSpec((1, D), lambda i, j: (0, 0)),
                 pl.BlockSpec((tm, D), lambda i, j: (i, 0))]
    return pl.pallas_call(
        body, name=name,
        grid=(T // tm, nj),
        in_specs=in_specs,
        out_specs=[pl.BlockSpec((tm, D), lambda i, j: (i, 0)), pl.BlockSpec((1, D), lambda i, j: (0, 0))],
        out_shape=[jax.ShapeDtypeStruct((T, D), F32), jax.ShapeDtypeStruct((1, D), F32)],
        scratch_shapes=[pltpu.VMEM((tm, D), F32)],
        compiler_params=_cp("arbitrary", "arbitrary"),
    )(*dzs, *([wg] * len(dzs)), h, gain, dh_in)


def _grad_matmul(a, b, out_shape3, tme, tne, out_index, prev, *, name):
    T, M = a.shape
    N = b.shape[1]
    tk = min(ROW_BLOCK, T)
    nk = T // tk

    def body(a_ref, b_ref, *rest):
        o_ref, acc = rest[-2:]
        k = pl.program_id(2)
        part = jnp.dot(a_ref[...].astype(BF).T, b_ref[...].astype(BF), preferred_element_type=F32)

        @pl.when(k == 0)
        def _():
            acc[...] = part

        @pl.when(k > 0)
        def _():
            acc[...] += part

        @pl.when(k == nk - 1)
        def _():
            o_ref[...] = acc[...].astype(BF)

    in_specs = [pl.BlockSpec((tk, tme), lambda i, j, k: (k, i)),
                pl.BlockSpec((tk, tne), lambda i, j, k: (k, j))]
    args = [a, b]
    aliases = {}
    if prev is not None:
        in_specs.append(ANY)
        args.append(prev)
        aliases = {2: 0}
    return pl.pallas_call(
        body, name=name,
        grid=(M // tme, N // tne, nk),
        in_specs=in_specs,
        out_specs=pl.BlockSpec((None, tme, tne), lambda i, j, k: out_index(i, j)),
        out_shape=jax.ShapeDtypeStruct(out_shape3, BF),
        scratch_shapes=[pltpu.VMEM((tme, tne), F32)],
        input_output_aliases=aliases,
        compiler_params=_cp("parallel", "parallel", "arbitrary"),
    )(*args)


def _row_tile(R):
    for t in (512, 256, 128, 64, 32, 16, 8):
        if R % t == 0:
            return t
    return R


def _cast_place(w2d, half_major, where, dtype, *, name):
    R, C = w2d.shape
    rh = R // 2 if half_major else R
    tr = _row_tile(rh)
    nr = rh // tr

    def body(s_ref, w_ref, o_ref):
        o_ref[...] = w_ref[...].astype(o_ref.dtype)

    if half_major:
        out_spec = pl.BlockSpec((None, None, tr, C), lambda h, i, s: (h, s[0], i, 0))
        shape = (2, N_CHIPS, rh, C)
    else:
        out_spec = pl.BlockSpec((None, tr, C), lambda h, i, s: (s[0], i, 0))
        shape = (N_CHIPS, R, C)
    return pl.pallas_call(
        body, name=name,
        grid_spec=pltpu.PrefetchScalarGridSpec(
            num_scalar_prefetch=1, grid=(R // rh, nr),
            in_specs=[pl.BlockSpec((tr, C), lambda h, i, s: (h * nr + i, 0))],
            out_specs=out_spec),
        out_shape=jax.ShapeDtypeStruct(shape, dtype),
        compiler_params=_cp("arbitrary", "arbitrary"),
    )(where, w2d)


def _pair_sum(dw, theirs, half_major, where, *, name):
    G, rh, C = theirs.shape
    tr = _row_tile(rh)
    nr = rh // tr

    def body(s_ref, a_ref, b_ref, o_ref):
        o_ref[...] = (a_ref[...].astype(F32) + b_ref[...].astype(F32)).astype(BF)

    if half_major:
        mine = pl.BlockSpec((None, None, tr, C), lambda g, i, s: (s[1], g, i, 0))
    else:
        mine = pl.BlockSpec((None, tr, C), lambda g, i, s: (g, s[1] * nr + i, 0))
    spec = pl.BlockSpec((None, tr, C), lambda g, i, s: (g, i, 0))
    return pl.pallas_call(
        body, name=name,
        grid_spec=pltpu.PrefetchScalarGridSpec(
            num_scalar_prefetch=1, grid=(G, nr), in_specs=[mine, spec], out_specs=spec),
        out_shape=jax.ShapeDtypeStruct((G, rh, C), BF),
        compiler_params=_cp("arbitrary", "arbitrary"),
    )(where, dw, theirs)


def _chip_sum(landed, parts, where, *, name):
    G, rh, C = landed.shape
    tr = _row_tile(rh)
    nr = rh // tr

    def body(s_ref, l_ref, p_ref, o_ref):
        for j in range(G):
            def own(j=j):
                v = p_ref[...].astype(F32)
                o_ref[...] = v if j == 0 else o_ref[...] + v

            def other(j=j):
                v = l_ref[j].astype(F32)
                o_ref[...] = v if j == 0 else o_ref[...] + v

            pl.when(s_ref[0] == j)(own)
            pl.when(s_ref[0] != j)(other)

    return pl.pallas_call(
        body, name=name,
        grid_spec=pltpu.PrefetchScalarGridSpec(
            num_scalar_prefetch=1, grid=(nr,),
            in_specs=[pl.BlockSpec((G, tr, C), lambda i, s: (0, i, 0)),
                      pl.BlockSpec((None, tr, C), lambda i, s: (s[0], i, 0))],
            out_specs=pl.BlockSpec((tr, C), lambda i, s: (s[1] * nr + i, 0))),
        out_shape=jax.ShapeDtypeStruct((2 * rh, C), F32),
        compiler_params=_cp("arbitrary"),
    )(where, landed, parts)


def _adamw(w, g, m, v, *, name):
    R, C = w.shape
    tr = _row_tile(R)

    def body(w_ref, g_ref, m_ref, v_ref, d_ref, mo_ref, vo_ref):
        gv = g_ref[...]
        mn = ADAM_B1 * m_ref[...] + (1.0 - ADAM_B1) * gv
        vn = ADAM_B2 * v_ref[...] + (1.0 - ADAM_B2) * jnp.square(gv)
        m_hat = mn / (1.0 - ADAM_B1 ** ADAM_STEP)
        v_hat = vn / (1.0 - ADAM_B2 ** ADAM_STEP)
        d_ref[...] = -ADAM_LR * (m_hat / (jnp.sqrt(v_hat) + ADAM_EPS) + ADAM_WD * w_ref[...])
        mo_ref[...] = mn
        vo_ref[...] = vn

    spec = pl.BlockSpec((tr, C), lambda i: (i, 0))
    shp = jax.ShapeDtypeStruct((R, C), F32)
    return pl.pallas_call(
        body, name=name, grid=(R // tr,), in_specs=[spec] * 4, out_specs=[spec] * 3,
        out_shape=[shp, shp, shp], compiler_params=_cp("parallel"),
    )(w, g, m, v)


def _place():
    x = lax.axis_index("x")
    y = lax.axis_index("y")
    c = lax.axis_index("c")
    chips = [(1 - x, y), (x, 1 - y), (1 - x, 1 - y)]
    return x, y, c, chips


def _half_of(ref, j, h, rh, half_major, r0=0, rc=None):
    rc = rh if rc is None else rc
    return ref.at[h, j, pl.ds(r0, rc)] if half_major else ref.at[j, pl.ds(h * rh + r0, rc)]


def _chunk_rows(rows, row_bytes, align):
    if rows <= align:
        return rows
    cands = [r for r in range(align, rows + 1, align) if rows % r == 0]
    fit = [r for r in cands if r * row_bytes <= DMA_CHUNK_BYTES]
    return max(fit) if fit else min(cands)


def _row_align(dtype):
    return 8 * (4 // jnp.dtype(dtype).itemsize)


def _start_chunks(make, rows, rc):
    for r0 in range(0, rows, rc):
        make(r0, rc).start()


def _all_gather_weights(placed, half_major, small):
    nm = len(placed)
    halves = []
    for p, hmaj in zip(placed, half_major):
        halves.append((p.shape[2], p.shape[3]) if hmaj else (p.shape[1] // 2, p.shape[2]))
    out_shapes = [jax.ShapeDtypeStruct(p.shape, p.dtype) for p in placed]
    out_shapes.append(jax.ShapeDtypeStruct(small.shape, small.dtype))

    def body(*refs):
        outs = refs[nm + 1:2 * nm + 1]
        sm_out = refs[2 * nm + 1]
        ssem, rsem, fsem, gsem, sm_s, sm_r = refs[2 * nm + 2:]
        x, y, c, chips = _place()
        myj = 2 * x + y
        sib = (x, y, 1 - c)

        def spot(m, j, h, r0=0, rc=None):
            return _half_of(outs[m], j, h, halves[m][0], half_major[m], r0, rc)

        def chunk(m):
            rh, cols = halves[m]
            return rh, _chunk_rows(rh, cols * placed[m].dtype.itemsize, _row_align(placed[m].dtype))

        first = []
        for m in range(nm):
            rh, rc = chunk(m)
            for k, (px, py) in enumerate(chips):
                def send(r0, n, m=m, k=k, px=px, py=py):
                    part = spot(m, myj, c, r0, n)
                    return pltpu.make_async_remote_copy(
                        src_ref=part, dst_ref=part,
                        send_sem=ssem.at[3 * m + k], recv_sem=rsem.at[3 * m + k],
                        device_id=(px, py, c), device_id_type=MESH)
                _start_chunks(send, rh, rc)
                first.append(send(0, rh))
        for k, (px, py) in enumerate(chips):
            cp = pltpu.make_async_remote_copy(
                src_ref=sm_out.at[myj], dst_ref=sm_out.at[myj], send_sem=sm_s.at[k], recv_sem=sm_r.at[k],
                device_id=(px, py, c), device_id_type=MESH)
            cp.start()
            first.append(cp)

        passed = []
        for m in range(nm):
            rh, rc = chunk(m)
            for k, (px, py) in enumerate(chips):
                pj = 2 * px + py
                landed = spot(m, pj, c)
                pltpu.make_async_remote_copy(
                    src_ref=landed, dst_ref=landed, send_sem=ssem.at[3 * m + k], recv_sem=rsem.at[3 * m + k],
                    device_id=(px, py, c), device_id_type=MESH).wait_recv()

                def fwd(r0, n, m=m, k=k, pj=pj):
                    part = spot(m, pj, c, r0, n)
                    return pltpu.make_async_remote_copy(
                        src_ref=part, dst_ref=part, send_sem=fsem.at[3 * m + k], recv_sem=gsem.at[3 * m + k],
                        device_id=sib, device_id_type=MESH)
                _start_chunks(fwd, rh, rc)
                passed.append(fwd(0, rh))
        for m in range(nm):
            for k, (px, py) in enumerate(chips):
                theirs = spot(m, 2 * px + py, 1 - c)
                pltpu.make_async_remote_copy(
                    src_ref=theirs, dst_ref=theirs, send_sem=fsem.at[3 * m + k], recv_sem=gsem.at[3 * m + k],
                    device_id=sib, device_id_type=MESH).wait_recv()
        for k, (px, py) in enumerate(chips):
            got = sm_out.at[2 * px + py]
            pltpu.make_async_remote_copy(
                src_ref=got, dst_ref=got, send_sem=sm_s.at[k], recv_sem=sm_r.at[k],
                device_id=(px, py, c), device_id_type=MESH).wait_recv()
        for cp in first + passed:
            cp.wait_send()

    return pl.pallas_call(
        body, name="ag_weights",
        in_specs=[ANY] * (nm + 1),
        out_specs=[ANY] * (nm + 1),
        out_shape=out_shapes,
        input_output_aliases={i: i for i in range(nm + 1)},
        scratch_shapes=[pltpu.SemaphoreType.DMA((3 * nm,)), pltpu.SemaphoreType.DMA((3 * nm,)),
                        pltpu.SemaphoreType.DMA((3 * nm,)), pltpu.SemaphoreType.DMA((3 * nm,)),
                        pltpu.SemaphoreType.DMA((3,)), pltpu.SemaphoreType.DMA((3,))],
    )(*placed, small)


def _sibling_swap(dws, half_major):
    nm = len(dws)
    shapes = []
    for dw, hmaj in zip(dws, half_major):
        if hmaj:
            _, G, rh, C = dw.shape
        else:
            G, R, C = dw.shape
            rh = R // 2
        shapes.append(jax.ShapeDtypeStruct((G, rh, C), dw.dtype))

    def body(*refs):
        ins = refs[:nm]
        theirs = refs[nm:2 * nm]
        ssem, rsem = refs[2 * nm:]
        x, y, c, _ = _place()
        sib = (x, y, 1 - c)

        def half(m, h):
            rh = shapes[m].shape[1]
            return ins[m].at[h] if half_major[m] else ins[m].at[:, pl.ds(h * rh, rh), :]

        cps = []
        for m in range(nm):
            G, rh, cols = shapes[m].shape
            rc = _chunk_rows(rh, cols * shapes[m].dtype.itemsize, _row_align(shapes[m].dtype))
            for j in range(G):
                _start_chunks(lambda r0, n, m=m, j=j, rh=rh: pltpu.make_async_remote_copy(
                    src_ref=_half_of(ins[m], j, 1 - c, rh, half_major[m], r0, n),
                    dst_ref=theirs[m].at[j, pl.ds(r0, n)], send_sem=ssem.at[m], recv_sem=rsem.at[m],
                    device_id=sib, device_id_type=MESH), rh, rc)
            cps.append(pltpu.make_async_remote_copy(
                src_ref=half(m, 1 - c), dst_ref=theirs[m], send_sem=ssem.at[m], recv_sem=rsem.at[m],
                device_id=sib, device_id_type=MESH))
        for cp in cps:
            cp.wait()

    return pl.pallas_call(
        body, name="rs_sibling_swap",
        in_specs=[ANY] * nm, out_specs=[ANY] * nm, out_shape=shapes,
        scratch_shapes=[pltpu.SemaphoreType.DMA((nm,)), pltpu.SemaphoreType.DMA((nm,))],
    )(*dws)


def _chip_exchange(parts):
    nm = len(parts)
    shapes = [jax.ShapeDtypeStruct(p.shape, p.dtype) for p in parts]

    def body(*refs):
        ins = refs[:nm]
        outs = refs[nm:2 * nm]
        ssem, rsem = refs[2 * nm:]
        x, y, c, chips = _place()
        myj = 2 * x + y
        sent = []
        for m in range(nm):
            _, rh, cols = shapes[m].shape
            rc = _chunk_rows(rh, cols * shapes[m].dtype.itemsize, _row_align(shapes[m].dtype))
            for k, (px, py) in enumerate(chips):
                def send(r0, n, m=m, k=k, px=px, py=py):
                    return pltpu.make_async_remote_copy(
                        src_ref=ins[m].at[2 * px + py, pl.ds(r0, n)], dst_ref=outs[m].at[myj, pl.ds(r0, n)],
                        send_sem=ssem.at[3 * m + k], recv_sem=rsem.at[3 * m + k],
                        device_id=(px, py, c), device_id_type=MESH)
                _start_chunks(send, rh, rc)
                sent.append(send(0, rh))
        for m in range(nm):
            for k, (px, py) in enumerate(chips):
                got = outs[m].at[2 * px + py]
                pltpu.make_async_remote_copy(
                    src_ref=got, dst_ref=got, send_sem=ssem.at[3 * m + k], recv_sem=rsem.at[3 * m + k],
                    device_id=(px, py, c), device_id_type=MESH).wait_recv()
        for cp in sent:
            cp.wait_send()

    return pl.pallas_call(
        body, name="rs_chip_exchange",
        in_specs=[ANY] * nm, out_specs=[ANY] * nm, out_shape=shapes,
        scratch_shapes=[pltpu.SemaphoreType.DMA((3 * nm,)), pltpu.SemaphoreType.DMA((3 * nm,))],
    )(*parts)


def _sibling_join(grads):
    nm = len(grads)
    shapes = [jax.ShapeDtypeStruct(g.shape, g.dtype) for g in grads]

    def body(*refs):
        outs = refs[nm:2 * nm]
        ssem, rsem = refs[2 * nm:]
        x, y, c, _ = _place()
        sib = (x, y, 1 - c)
        cps = []
        for m in range(nm):
            rh, cols = grads[m].shape[0] // 2, grads[m].shape[1]
            rc = _chunk_rows(rh, cols * grads[m].dtype.itemsize, _row_align(grads[m].dtype))

            def send(r0, n, m=m, rh=rh):
                part = outs[m].at[pl.ds(c * rh + r0, n)]
                return pltpu.make_async_remote_copy(
                    src_ref=part, dst_ref=part, send_sem=ssem.at[m], recv_sem=rsem.at[m],
                    device_id=sib, device_id_type=MESH)
            _start_chunks(send, rh, rc)
            cps.append(send(0, rh))
        for m, cp in enumerate(cps):
            rh = grads[m].shape[0] // 2
            cp.wait_send()
            got = outs[m].at[pl.ds((1 - c) * rh, rh)]
            pltpu.make_async_remote_copy(
                src_ref=got, dst_ref=got, send_sem=ssem.at[m], recv_sem=rsem.at[m],
                device_id=sib, device_id_type=MESH).wait_recv()

    return pl.pallas_call(
        body, name="rs_sibling_join",
        in_specs=[ANY] * nm, out_specs=[ANY] * nm, out_shape=shapes,
        input_output_aliases={i: i for i in range(nm)},
        scratch_shapes=[pltpu.SemaphoreType.DMA((nm,)), pltpu.SemaphoreType.DMA((nm,))],
    )(*grads)


def _all_reduce_small(pack):
    R, C = pack.shape

    def body(in_ref, out_ref, slots, ssem, rsem):
        x, y, c, _ = _place()
        me = 4 * x + 2 * y + c
        slots[me] = in_ref[...]
        cps = []
        for k in range(1, N_DEV):
            dx, dy, dc = (k >> 2) & 1, (k >> 1) & 1, k & 1
            peer = (x ^ dx, y ^ dy, c ^ dc)
            cp = pltpu.make_async_remote_copy(
                src_ref=in_ref, dst_ref=slots.at[me], send_sem=ssem.at[k], recv_sem=rsem.at[k],
                device_id=peer, device_id_type=MESH)
            cp.start()
            cps.append(cp)
        for k in range(1, N_DEV):
            dx, dy, dc = (k >> 2) & 1, (k >> 1) & 1, k & 1
            got = slots.at[4 * (x ^ dx) + 2 * (y ^ dy) + (c ^ dc)]
            pltpu.make_async_remote_copy(
                src_ref=got, dst_ref=got, send_sem=ssem.at[k], recv_sem=rsem.at[k],
                device_id=(x ^ dx, y ^ dy, c ^ dc), device_id_type=MESH).wait_recv()
        for cp in cps:
            cp.wait_send()
        acc = slots[0]
        for s in range(1, N_DEV):
            acc = acc + slots[s]
        out_ref[...] = acc

    return pl.pallas_call(
        body, name="ar_small",
        in_specs=[pl.BlockSpec(memory_space=pltpu.VMEM)],
        out_specs=pl.BlockSpec(memory_space=pltpu.VMEM),
        out_shape=jax.ShapeDtypeStruct((R, C), F32),
        scratch_shapes=[pltpu.VMEM((N_DEV, R, C), F32),
                        pltpu.SemaphoreType.DMA((N_DEV,)), pltpu.SemaphoreType.DMA((N_DEV,))],
    )(pack)


def kernel(x, positions, mix_norm_pre, mix_norm_post, ffn_norm_pre, ffn_norm_post, ffn_w_gate_up, ffn_w_down, conv_w_in, conv_w, conv_w_out, kv_norm, w_kv, w_q, w_o, loss_target, m_mix_norm_pre, m_mix_norm_post, m_ffn_norm_pre, m_ffn_norm_post, m_ffn_w_gate_up, m_ffn_w_down, m_conv_w_in, m_conv_w, m_conv_w_out, m_kv_norm, m_w_kv, m_w_q, m_w_o, v_mix_norm_pre, v_mix_norm_post, v_ffn_norm_pre, v_ffn_norm_post, v_ffn_w_gate_up, v_ffn_w_down, v_conv_w_in, v_conv_w, v_conv_w_out, v_kv_norm, v_w_kv, v_w_q, v_w_o):
    T, D = x.shape[1], x.shape[2]
    L = ffn_w_gate_up.shape[0]
    n_gu = ffn_w_gate_up.shape[2]
    f_sh = ffn_w_down.shape[1]
    F = N_CHIPS * f_sh
    x0 = x[0]
    tgt = loss_target[0]

    half = HEAD_DIM // 2
    inv_freq = ROPE_THETA ** (-jnp.arange(half, dtype=F32) / half)
    ang = positions[0].astype(F32)[:, None] * inv_freq
    cosv, sinv = jnp.cos(ang), jnp.sin(ang)
    cos2 = jnp.tile(cosv, (1, LANES // half))
    ss2 = jnp.tile(jnp.concatenate([-sinv, sinv], axis=1), (1, LANES // HEAD_DIM))

    def as2d(a):
        return a.reshape(-1, a.shape[-1])

    big = [ffn_w_gate_up, ffn_w_down, conv_w_in, conv_w_out, w_kv, w_q, w_o]
    big_m = [m_ffn_w_gate_up, m_ffn_w_down, m_conv_w_in, m_conv_w_out, m_w_kv, m_w_q, m_w_o]
    big_v = [v_ffn_w_gate_up, v_ffn_w_down, v_conv_w_in, v_conv_w_out, v_w_kv, v_w_q, v_w_o]
    half_major = [False, True, False, False, False, False, False]
    chip = 2 * lax.axis_index("x") + lax.axis_index("y")
    where = jnp.stack([chip, lax.axis_index("c")]).astype(jnp.int32)
    tc = conv_w.shape[2]
    cw_pad = jnp.concatenate([conv_w[0], jnp.zeros((8 - conv_w.shape[1], tc), F32)], axis=0)

    placed = [_cast_place(as2d(w), hm, where, BF, name=f"place{i}")
              for i, (w, hm) in enumerate(zip(big, half_major))]
    gathered = _all_gather_weights(placed, half_major, _cast_place(cw_pad, False, where, F32, name="place_taps"))
    wgu, wd4, wci, wco4, wkv, wq, wo4, cw = gathered
    wd = wd4.reshape(L, F, D)
    wco = wco4.reshape(1, D, D)
    wo = wo4.reshape(1, D, D)

    def row(a, i):
        return a[i:i + 1]

    z, hn_m0 = _norm_matmul(x0, row(mix_norm_pre, 0), wci, cos2, ss2, name="f0_conv_in",
                            rope_shards=0, scale=1.0, out_dtype=BF)
    vmix = _conv_fwd(z, cw, name="f0_conv")
    y0, h1 = _matmul_postnorm(vmix, wco, 0, row(mix_norm_post, 0), x0, name="f0_conv_out")
    g0, u0, a0, hn_f0 = _norm_swiglu(h1, row(ffn_norm_pre, 0), wgu, 0, name="f0_gate_up")
    f0, h2 = _matmul_postnorm(a0, wd, 0, row(ffn_norm_post, 0), h1, name="f0_down")

    kv_all, hn_kv = _norm_matmul(h2, kv_norm.reshape(1, D), wkv, cos2, ss2, name="f1_kv",
                                 rope_shards=N_CHIPS // 2, scale=1.0, out_dtype=F32)
    q_all, hn_m1 = _norm_matmul(h2, row(mix_norm_pre, 1), wq, cos2, ss2, name="f1_q",
                                rope_shards=N_CHIPS, scale=HEAD_DIM ** -0.5, out_dtype=F32)
    o_att, lse = _attn_fwd(q_all, kv_all, name="f1_attn")
    y1, h3 = _matmul_postnorm(o_att, wo, 0, row(mix_norm_post, 1), h2, name="f1_attn_out")
    g1, u1, a1, hn_f1 = _norm_swiglu(h3, row(ffn_norm_pre, 1), wgu, 1, name="f1_gate_up")
    f1, h4 = _matmul_postnorm(a1, wd, 1, row(ffn_norm_post, 1), h3, name="f1_down")

    dh4, sq = _loss_head(h4, tgt, name="loss_head")
    loss_part = 0.5 * sq[0, 0] / D

    gu_shape = (N_CHIPS, L * D, n_gu)
    dyf1, dg1, du1, d_ffn_post1 = _postnorm_bwd_swiglu(dh4, f1, row(ffn_norm_post, 1), wd, 1, g1, u1,
                                                       name="b1_down")
    dwd = _grad_matmul(a1, dyf1, (L, F, D), F // 2, D, lambda i, j: (1, i, 0), None, name="b1_dw_down")
    dwgu = _grad_matmul(hn_f1, dg1, gu_shape, D, n_gu, lambda i, j: (j, 1, 0), None, name="b1_dw_gate")
    dwgu = _grad_matmul(hn_f1, du1, gu_shape, D, n_gu, lambda i, j: (j + 2, 1, 0), dwgu, name="b1_dw_up")
    dh3, d_ffn_pre1 = _matmul_prenorm_bwd((dg1, du1), wgu, 1, h3, row(ffn_norm_pre, 1), dh4, name="b1_gate_up")

    dy1, do, d_mix_post1 = _postnorm_bwd_matmul(dh3, y1, row(mix_norm_post, 1), wo, 0, name="b1_attn_out",
                                                da_dtype=F32)
    dwo = _grad_matmul(o_att, dy1, (1, D, D), D, D, lambda i, j: (0, 0, 0), None, name="b1_dw_o")
    prev = None
    for gi, (window, dil) in enumerate(BRANCHES):
        prev = _attn_bwd(q_all, kv_all, do, o_att, lse, cos2, ss2, gi, dil, prev, name=f"b1_attn{gi}")
    dq_all, dk_all, dv_all = prev
    n_q = wq.shape[2]
    n_kv = wkv.shape[2]
    dwq = _grad_matmul(hn_m1, dq_all, (N_CHIPS, D, n_q), D, n_q, lambda i, j: (j, 0, 0), None, name="b1_dw_q")
    dwkv = _grad_matmul(hn_kv, dk_all, (N_CHIPS, D, n_kv), D, n_kv, lambda i, j: (j, 0, 0), None, name="b1_dw_k")
    dwkv = _grad_matmul(hn_kv, dv_all, (N_CHIPS, D, n_kv), D, n_kv, lambda i, j: (j + 2, 0, 0), dwkv, name="b1_dw_v")
    dh2, d_mix_pre1 = _matmul_prenorm_bwd((dq_all,), wq, 0, h2, row(mix_norm_pre, 1), dh3, name="b1_q")
    dh2, d_kv_norm = _matmul_prenorm_bwd((dk_all, dv_all), wkv, 0, h2, kv_norm.reshape(1, D), dh2, name="b1_kv")

    dyf0, dg0, du0, d_ffn_post0 = _postnorm_bwd_swiglu(dh2, f0, row(ffn_norm_post, 0), wd, 0, g0, u0,
                                                       name="b0_down")
    dwd = _grad_matmul(a0, dyf0, (L, F, D), F // 2, D, lambda i, j: (0, i, 0), dwd, name="b0_dw_down")
    dwgu = _grad_matmul(hn_f0, dg0, gu_shape, D, n_gu, lambda i, j: (j, 0, 0), dwgu, name="b0_dw_gate")
    dwgu = _grad_matmul(hn_f0, du0, gu_shape, D, n_gu, lambda i, j: (j + 2, 0, 0), dwgu, name="b0_dw_up")
    dh1, d_ffn_pre0 = _matmul_prenorm_bwd((dg0, du0), wgu, 0, h1, row(ffn_norm_pre, 0), dh2, name="b0_gate_up")

    dy0, dvmix, d_mix_post0 = _postnorm_bwd_matmul(dh1, y0, row(mix_norm_post, 0), wco, 0, name="b0_conv_out",
                                                   da_dtype=BF)
    dwco = _grad_matmul(vmix, dy0, (1, D, D), D, D, lambda i, j: (0, 0, 0), None, name="b0_dw_conv_out")
    dz, dcw = _conv_bwd(z, cw, dvmix, name="b0_conv")
    n_ci = wci.shape[2]
    dwci = _grad_matmul(hn_m0, dz, (N_CHIPS, D, n_ci), D, n_ci, lambda i, j: (j, 0, 0), None, name="b0_dw_conv_in")
    dx, d_mix_pre0 = _matmul_prenorm_bwd((dz,), wci, 0, x0, row(mix_norm_pre, 0), dh1, name="b0_conv_in")

    pack = jnp.concatenate([
        d_mix_pre0, d_mix_pre1, d_mix_post0, d_mix_post1, d_ffn_pre0, d_ffn_pre1, d_ffn_post0, d_ffn_post1,
        d_kv_norm, dcw[0:3], jnp.full((1, D), loss_part, F32),
        jnp.zeros((SMALL_ROWS - 13, D), F32)], axis=0)
    red = _all_reduce_small(pack)
    loss = red[12, 0]
    myj = 2 * lax.axis_index("x") + lax.axis_index("y")
    g_conv_w = lax.dynamic_slice(red, (9, myj * tc), (3, tc))

    zeros7 = jnp.zeros((SMALL_ROWS - 9, D), F32)
    w_small = jnp.concatenate([mix_norm_pre, mix_norm_post, ffn_norm_pre, ffn_norm_post, kv_norm.reshape(1, D), zeros7], axis=0)
    m_small = jnp.concatenate([m_mix_norm_pre, m_mix_norm_post, m_ffn_norm_pre, m_ffn_norm_post, m_kv_norm.reshape(1, D), zeros7], axis=0)
    v_small = jnp.concatenate([v_mix_norm_pre, v_mix_norm_post, v_ffn_norm_pre, v_ffn_norm_post, v_kv_norm.reshape(1, D), zeros7], axis=0)
    d_small, nm_small, nv_small = _adamw(w_small, red, m_small, v_small, name="adamw_small")

    pad5 = jnp.zeros((5, tc), F32)
    d_cw, nm_cw, nv_cw = _adamw(cw_pad, jnp.concatenate([g_conv_w, pad5], axis=0),
                                jnp.concatenate([m_conv_w[0], pad5], axis=0),
                                jnp.concatenate([v_conv_w[0], pad5], axis=0), name="adamw_conv_w")

    dws = [dwgu, dwd.reshape(2, N_CHIPS, f_sh, D), dwci, dwco.reshape(N_CHIPS, D // N_CHIPS, D), dwkv, dwq,
           dwo.reshape(N_CHIPS, D // N_CHIPS, D)]
    theirs = _sibling_swap(dws, half_major)
    parts = [_pair_sum(dw, t, hm, where, name=f"rs_pair_sum{i}")
             for i, (dw, t, hm) in enumerate(zip(dws, theirs, half_major))]
    landed = _chip_exchange(parts)
    halves = [_chip_sum(l, p, where, name=f"rs_chip_sum{i}") for i, (l, p) in enumerate(zip(landed, parts))]
    grads2d = _sibling_join(halves)

    big_out = []
    for i, (w, gr, m, v) in enumerate(zip(big, grads2d, big_m, big_v)):
        d_, m_, v_ = _adamw(as2d(w), gr, as2d(m), as2d(v), name=f"adamw{i}")
        big_out.append((gr.reshape(w.shape), d_.reshape(w.shape), m_.reshape(w.shape), v_.reshape(w.shape)))

    def small(a):
        return (a[0:2], a[2:4], a[4:6], a[6:8])

    def assemble(sm, cwv, kind):
        pre, post, fpre, fpost = small(sm)
        b = [t[kind] for t in big_out]
        return [pre, post, fpre, fpost, b[0], b[1], b[2], cwv[0:3].reshape(conv_w.shape), b[3],
                sm[8], b[4], b[5].reshape(w_q.shape), b[6].reshape(w_o.shape)]

    grads = assemble(red, jnp.concatenate([g_conv_w, pad5], axis=0), 0)
    deltas = assemble(d_small, d_cw, 1)
    new_m = assemble(nm_small, nm_cw, 2)
    new_v = assemble(nv_small, nv_cw, 3)
    return (loss, dx.reshape(x.shape), *grads, *deltas, *new_m, *new_v)
```

```python
import functools

import jax
import jax.numpy as jnp
from jax import lax
from jax.experimental import pallas as pl
from jax.experimental.pallas import tpu as pltpu

HEAD_DIM = 64
BAND = 128
BRANCHES = ((128, 1), (512, 4), (2048, 16))
ROPE_THETA = 10000.0
RMS_EPS = 1e-6
NEG_INF = -1e30
ADAM_LR = 0.001
ADAM_B1 = 0.9
ADAM_B2 = 0.999
ADAM_EPS = 1e-08
ADAM_WD = 0.01
ADAM_STEP = 10

N_CHIPS = 4
N_DEV = 8
LANES = 128
ROW_BLOCK = 512
ATTN_BLOCK_ROWS = 2048
VMEM_LIMIT = 56 * 1024 * 1024
SMALL_ROWS = 16
DMA_CHUNK_BYTES = 512 * 1024

BF = jnp.bfloat16
F32 = jnp.float32
MESH = pl.DeviceIdType.MESH
ANY = pl.BlockSpec(memory_space=pl.ANY)


def _cp(*sem):
    return pltpu.CompilerParams(dimension_semantics=sem, vmem_limit_bytes=VMEM_LIMIT)


def _rot_half(t, first):
    return jnp.where(first, pltpu.roll(t, 96, 1), pltpu.roll(t, 32, 1))


def _first_half_mask(rows):
    lane = lax.broadcasted_iota(jnp.int32, (rows, LANES), 1)
    return (lane % HEAD_DIM) < (HEAD_DIM // 2)


def _normed_rows(j, rows, x_ref, g_ref, xn_ref, xs, last_start, tm):
    @pl.when(j == 0)
    def _():
        xv = x_ref[...]
        r = lax.rsqrt(jnp.mean(xv * xv, axis=-1, keepdims=True) + RMS_EPS)
        xn = (xv * r * g_ref[...]).astype(BF)
        xs[rows, :] = xn
        xn_ref[...] = xn

    @pl.when(j > 0)
    def _():
        xn_ref[...] = xs[pl.ds(last_start, tm), :]


def _norm_matmul(x, gain, wg, cos2, ss2, *, name, rope_shards, scale, out_dtype):
    T, D = x.shape
    n = wg.shape[2]
    tm = min(ROW_BLOCK, T)

    ni = T // tm

    def body(x_ref, g_ref, w_ref, cos_ref, ss_ref, y_ref, xn_ref, xs):
        j = pl.program_id(0)
        rows = pl.ds(pl.multiple_of(pl.program_id(1) * tm, tm), tm)
        _normed_rows(j, rows, x_ref, g_ref, xn_ref, xs, (ni - 1) * tm, tm)
        acc = jnp.dot(xs[rows, :], w_ref[...], preferred_element_type=F32)

        def plain():
            y_ref[...] = acc.astype(out_dtype)

        def rope():
            cosv = cos_ref[...]
            ssv = ss_ref[...]
            first = _first_half_mask(tm)
            for ci in range(n // LANES):
                t = acc[:, ci * LANES:(ci + 1) * LANES]
                y = (t * cosv + _rot_half(t, first) * ssv) * scale
                y_ref[:, ci * LANES:(ci + 1) * LANES] = y.astype(out_dtype)

        if rope_shards == 0:
            plain()
        elif rope_shards == N_CHIPS:
            rope()
        else:
            pl.when(j < rope_shards)(rope)
            pl.when(j >= rope_shards)(plain)

    first_pass = lambda j, i: (jnp.where(j == 0, i, ni - 1), 0)
    return pl.pallas_call(
        body, name=name,
        grid=(N_CHIPS, ni),
        in_specs=[
            pl.BlockSpec((tm, D), first_pass),
            pl.BlockSpec((1, D), lambda j, i: (0, 0)),
            pl.BlockSpec((None, D, n), lambda j, i: (j, 0, 0)),
            pl.BlockSpec((tm, LANES), lambda j, i: (i, 0)),
            pl.BlockSpec((tm, LANES), lambda j, i: (i, 0)),
        ],
        out_specs=[
            pl.BlockSpec((tm, n), lambda j, i: (i, j)),
            pl.BlockSpec((tm, D), first_pass),
        ],
        out_shape=[jax.ShapeDtypeStruct((T, N_CHIPS * n), out_dtype),
                   jax.ShapeDtypeStruct((T, D), BF)],
        scratch_shapes=[pltpu.VMEM((T, D), BF)],
        compiler_params=_cp("arbitrary", "arbitrary"),
    )(x, gain, wg, cos2, ss2)


def _norm_swiglu(x, gain, wg, layer, *, name):
    T, D = x.shape
    n = wg.shape[2]
    tm = min(ROW_BLOCK, T)

    ni = T // tm

    def body(x_ref, g_ref, wg_ref, wu_ref, go_ref, uo_ref, ao_ref, xn_ref, xs):
        j = pl.program_id(0)
        rows = pl.ds(pl.multiple_of(pl.program_id(1) * tm, tm), tm)
        _normed_rows(j, rows, x_ref, g_ref, xn_ref, xs, (ni - 1) * tm, tm)
        g = jnp.dot(xs[rows, :], wg_ref[...], preferred_element_type=F32)
        u = jnp.dot(xs[rows, :], wu_ref[...], preferred_element_type=F32)
        go_ref[...] = g.astype(BF)
        uo_ref[...] = u.astype(BF)
        ao_ref[...] = (g * jax.nn.sigmoid(g) * u).astype(BF)

    half = N_CHIPS // 2
    first_pass = lambda j, i: (jnp.where(j == 0, i, ni - 1), 0)
    act = jax.ShapeDtypeStruct((T, half * n), BF)
    return pl.pallas_call(
        body, name=name,
        grid=(half, ni),
        in_specs=[
            pl.BlockSpec((tm, D), first_pass),
            pl.BlockSpec((1, D), lambda j, i: (0, 0)),
            pl.BlockSpec((None, D, n), lambda j, i: (j, layer, 0)),
            pl.BlockSpec((None, D, n), lambda j, i: (j + half, layer, 0)),
        ],
        out_specs=[
            pl.BlockSpec((tm, n), lambda j, i: (i, j)),
            pl.BlockSpec((tm, n), lambda j, i: (i, j)),
            pl.BlockSpec((tm, n), lambda j, i: (i, j)),
            pl.BlockSpec((tm, D), first_pass),
        ],
        out_shape=[act, act, act, jax.ShapeDtypeStruct((T, D), BF)],
        scratch_shapes=[pltpu.VMEM((T, D), BF)],
        compiler_params=_cp("arbitrary", "arbitrary"),
    )(x, gain, wg, wg)


def _matmul_postnorm(a, w3, widx, gain, h_old, *, name):
    T, K = a.shape
    D = w3.shape[2]
    tm = min(ROW_BLOCK, T)

    def body(a_ref, w_ref, g_ref, h_ref, y_ref, hn_ref):
        y = jnp.dot(a_ref[...].astype(BF), w_ref[...], preferred_element_type=F32)
        y_ref[...] = y.astype(BF)
        r = lax.rsqrt(jnp.mean(y * y, axis=-1, keepdims=True) + RMS_EPS)
        hn_ref[...] = h_ref[...] + y * r * g_ref[...]

    return pl.pallas_call(
        body, name=name,
        grid=(T // tm,),
        in_specs=[
            pl.BlockSpec((tm, K), lambda i: (i, 0)),
            pl.BlockSpec((None, K, D), lambda i: (widx, 0, 0)),
            pl.BlockSpec((1, D), lambda i: (0, 0)),
            pl.BlockSpec((tm, D), lambda i: (i, 0)),
        ],
        out_specs=[pl.BlockSpec((tm, D), lambda i: (i, 0)),
                   pl.BlockSpec((tm, D), lambda i: (i, 0))],
        out_shape=[jax.ShapeDtypeStruct((T, D), BF), jax.ShapeDtypeStruct((T, D), F32)],
        compiler_params=_cp("parallel"),
    )(a, w3, gain, h_old)


def _loss_head(h, target, *, name):
    T, D = h.shape
    tm = min(ROW_BLOCK, T)

    def body(h_ref, t_ref, dh_ref, s_ref):
        i = pl.program_id(0)

        @pl.when(i == 0)
        def _():
            s_ref[...] = jnp.zeros_like(s_ref)

        e = h_ref[...] - t_ref[...]
        dh_ref[...] = e * (1.0 / D)
        s_ref[...] += jnp.sum(e * e)

    return pl.pallas_call(
        body, name=name,
        grid=(T // tm,),
        in_specs=[pl.BlockSpec((tm, D), lambda i: (i, 0)), pl.BlockSpec((tm, D), lambda i: (i, 0))],
        out_specs=[pl.BlockSpec((tm, D), lambda i: (i, 0)), pl.BlockSpec((8, LANES), lambda i: (0, 0))],
        out_shape=[jax.ShapeDtypeStruct((T, D), F32), jax.ShapeDtypeStruct((8, LANES), F32)],
        compiler_params=_cp("arbitrary"),
    )(h, target)


def _shift_down(u, k):
    row = lax.broadcasted_iota(jnp.int32, u.shape, 0)
    return jnp.where(row >= k, pltpu.roll(u, k, 0), 0.0)


def _shift_up(u, k):
    T = u.shape[0]
    row = lax.broadcasted_iota(jnp.int32, u.shape, 0)
    return jnp.where(row < T - k, pltpu.roll(u, T - k, 0), 0.0)


def _conv_fwd(z, cw, *, name):
    T = z.shape[0]
    D = z.shape[1] // 3
    tc = cw.shape[2]
    nb = D // tc

    def body(b_ref, c_ref, h_ref, w_ref, o_ref):
        u = c_ref[...].astype(F32) * h_ref[...].astype(F32)
        w = w_ref[...]
        conv = w[2:3] * u + w[1:2] * _shift_down(u, 1) + w[0:1] * _shift_down(u, 2)
        o_ref[...] = (b_ref[...].astype(F32) * conv).astype(BF)

    return pl.pallas_call(
        body, name=name,
        grid=(nb,),
        in_specs=[
            pl.BlockSpec((T, tc), lambda j: (0, j)),
            pl.BlockSpec((T, tc), lambda j: (0, nb + j)),
            pl.BlockSpec((T, tc), lambda j: (0, 2 * nb + j)),
            pl.BlockSpec((None, 8, tc), lambda j: (j, 0, 0)),
        ],
        out_specs=pl.BlockSpec((T, tc), lambda j: (0, j)),
        out_shape=jax.ShapeDtypeStruct((T, D), BF),
        compiler_params=_cp("parallel"),
    )(z, z, z, cw)


def _conv_bwd(z, cw, dv, *, name):
    T = z.shape[0]
    D = z.shape[1] // 3
    tc = cw.shape[2]
    nb = D // tc

    def body(b_ref, c_ref, h_ref, w_ref, dv_ref, dz_ref, dw_ref):
        p = pl.program_id(0)
        c = c_ref[...].astype(F32)
        h = h_ref[...].astype(F32)
        u = c * h
        u1 = _shift_down(u, 1)
        u2 = _shift_down(u, 2)
        w = w_ref[...]
        dvv = dv_ref[...].astype(F32)
        dconv = dvv * b_ref[...].astype(F32)
        du = w[2:3] * dconv + w[1:2] * _shift_up(dconv, 1) + w[0:1] * _shift_up(dconv, 2)
        rows = lax.broadcasted_iota(jnp.int32, (8, tc), 0)
        dw = jnp.where(rows == 0, jnp.sum(dconv * u2, axis=0, keepdims=True),
                       jnp.where(rows == 1, jnp.sum(dconv * u1, axis=0, keepdims=True),
                                 jnp.where(rows == 2, jnp.sum(dconv * u, axis=0, keepdims=True), 0.0)))
        dw_ref[...] = dw

        @pl.when(p == 0)
        def _():
            conv = w[2:3] * u + w[1:2] * u1 + w[0:1] * u2
            dz_ref[...] = (dvv * conv).astype(BF)

        @pl.when(p == 1)
        def _():
            dz_ref[...] = (du * h).astype(BF)

        @pl.when(p == 2)
        def _():
            dz_ref[...] = (du * c).astype(BF)

    return pl.pallas_call(
        body, name=name,
        grid=(3, nb),
        in_specs=[
            pl.BlockSpec((T, tc), lambda p, j: (0, j)),
            pl.BlockSpec((T, tc), lambda p, j: (0, nb + j)),
            pl.BlockSpec((T, tc), lambda p, j: (0, 2 * nb + j)),
            pl.BlockSpec((None, 8, tc), lambda p, j: (j, 0, 0)),
            pl.BlockSpec((T, tc), lambda p, j: (0, j)),
        ],
        out_specs=[pl.BlockSpec((T, tc), lambda p, j: (0, p * nb + j)),
                   pl.BlockSpec((8, tc), lambda p, j: (p, j))],
        out_shape=[jax.ShapeDtypeStruct((T, 3 * D), BF), jax.ShapeDtypeStruct((3 * 8, D), F32)],
        compiler_params=_cp("arbitrary", "arbitrary"),
    )(z, z, z, cw, dv)


def _strided(base, count, d):
    return pl.ds(base, count, stride=d) if d > 1 else pl.ds(pl.multiple_of(base, BAND), count)


def _fill_band_bias(bias):
    qi = lax.broadcasted_iota(jnp.int32, (2 * BAND, 2 * BAND), 0) % BAND
    kj = lax.broadcasted_iota(jnp.int32, (2 * BAND, 2 * BAND), 1)
    dist = qi + BAND - kj
    band = (dist >= 0) & (dist <= BAND)
    bias[0] = jnp.where(band & (kj >= BAND), 0.0, NEG_INF)
    bias[1] = jnp.where(band, 0.0, NEG_INF)


def _attn_block_rows(T):
    return min(ATTN_BLOCK_ROWS, T)


def _head_mask():
    lane = lax.broadcasted_iota(jnp.int32, (2 * BAND, LANES), 1)
    row = lax.broadcasted_iota(jnp.int32, (2 * BAND, LANES), 0)
    return (lane < HEAD_DIM) == (row < BAND)


def _attn_fwd(q_all, kv_all, *, name):
    T = q_all.shape[0]
    NB = len(BRANCHES)
    Dm = q_all.shape[1] // NB
    HP = Dm // LANES
    R = _attn_block_rows(T)
    units = R // BAND
    dmax = max(d for _, d in BRANCHES)

    def body(*refs):
        ins = refs[:5 * NB]
        o_ref, l_ref, kbuf, vbuf, o_s, l_s, bias = refs[5 * NB:]
        n = pl.program_id(0)
        _fill_band_bias(bias)
        hm = _head_mask()
        low = lax.broadcasted_iota(jnp.int32, (BAND, LANES), 1) < HEAD_DIM

        for g, (_, d) in enumerate(BRANCHES):
            q_ref, kp_ref, kc_ref, vp_ref, vc_ref = ins[5 * g:5 * g + 5]
            pr = BAND * d
            kbuf[0:pr, :] = kp_ref[...]
            kbuf[pr:pr + R, :] = kc_ref[...]
            vbuf[0:pr, :] = vp_ref[...]
            vbuf[pr:pr + R, :] = vc_ref[...]

            def unit(u, carry, g=g, d=d, pr=pr, q_ref=q_ref):
                sub = u // d
                base = sub * pr + (u - sub * d)
                q = q_ref[_strided(base, BAND, d), :]
                q2 = jnp.where(hm, jnp.concatenate([q, q], axis=0), 0.0).astype(BF)
                k2 = kbuf[_strided(base, 2 * BAND, d), :].astype(BF)
                v2 = vbuf[_strided(base, 2 * BAND, d), :].astype(BF)
                s = lax.dot_general(q2, k2, (((1,), (1,)), ((), ())), preferred_element_type=F32)
                s = s + bias[((n > 0) | (sub > 0)).astype(jnp.int32)]
                m = jnp.max(s, axis=-1, keepdims=True)
                p = jnp.exp(s - m)
                l = jnp.sum(p, axis=-1, keepdims=True)
                pv = jnp.dot(p.astype(BF), v2, preferred_element_type=F32) / l
                lse = m + jnp.log(l)
                o_s[g, _strided(base, BAND, d), :] = jnp.where(low, pv[:BAND], pv[BAND:])
                l_s[g, _strided(base, BAND, d), :] = jnp.where(low, lse[:BAND], lse[BAND:])
                return carry

            lax.fori_loop(0, units, unit, 0, unroll=4)

        def merge(i, carry):
            sl = pl.ds(pl.multiple_of(i * BAND, BAND), BAND)
            lv = [l_s[g, sl, :] for g in range(NB)]
            m = functools.reduce(jnp.maximum, lv)
            e = [jnp.exp(v - m) for v in lv]
            tot = functools.reduce(jnp.add, e)
            o_ref[sl, :] = functools.reduce(jnp.add, [(e[g] / tot) * o_s[g, sl, :] for g in range(NB)])
            l_ref[sl, :] = m + jnp.log(tot)
            return carry

        lax.fori_loop(0, units, merge, 0)

    in_specs, args = [], []
    for g, (_, d) in enumerate(BRANCHES):
        per = R // (BAND * d)
        for col, rows, idx in (
                (g * HP, R, lambda n, hp: n),
                (g * HP, BAND * d, lambda n, hp, per=per: jnp.maximum(n * per - 1, 0)),
                (g * HP, R, lambda n, hp: n),
                ((NB + g) * HP, BAND * d, lambda n, hp, per=per: jnp.maximum(n * per - 1, 0)),
                ((NB + g) * HP, R, lambda n, hp: n)):
            in_specs.append(pl.BlockSpec((rows, LANES), lambda n, hp, col=col, idx=idx: (idx(n, hp), col + hp)))
        args += [q_all, kv_all, kv_all, kv_all, kv_all]
    out = pl.BlockSpec((R, LANES), lambda n, hp: (n, hp))
    return pl.pallas_call(
        body, name=name,
        grid=(T // R, HP),
        in_specs=in_specs,
        out_specs=[out, out],
        out_shape=[jax.ShapeDtypeStruct((T, Dm), F32), jax.ShapeDtypeStruct((T, Dm), F32)],
        scratch_shapes=[pltpu.VMEM((BAND * dmax + R, LANES), F32), pltpu.VMEM((BAND * dmax + R, LANES), F32),
                        pltpu.VMEM((NB, R, LANES), F32), pltpu.VMEM((NB, R, LANES), F32),
                        pltpu.VMEM((2, 2 * BAND, 2 * BAND), F32)],
        compiler_params=_cp("parallel", "parallel"),
    )(*args)


def _attn_bwd(q_all, kv_all, do, o, lse, cos2, ss2, g, d, prev, *, name, dep=None):
    T = q_all.shape[0]
    NB = len(BRANCHES)
    Dm = q_all.shape[1] // NB
    HP = Dm // LANES
    R = _attn_block_rows(T)
    nblk = T // R
    units = R // BAND
    pr = BAND * d
    per = R // pr
    scale = HEAD_DIM ** -0.5

    def rope_bwd(t, cosv, ssv, first):
        return t * cosv - _rot_half(t, first) * ssv

    def body(q_ref, kp_ref, kc_ref, vp_ref, vc_ref, do_ref, o_ref, l_ref, cos_ref, ss_ref, *rest):
        dq_ref, dk_ref, dv_ref, kbuf, vbuf, dkbuf, dvbuf, ck_car, cv_car, bias = rest[-10:]
        i = pl.program_id(1)
        n = nblk - 1 - i
        _fill_band_bias(bias)
        first = _first_half_mask(BAND)
        hm = _head_mask()
        low = lax.broadcasted_iota(jnp.int32, (BAND, LANES), 1) < HEAD_DIM

        kbuf[0:pr, :] = kp_ref[...]
        kbuf[pr:pr + R, :] = kc_ref[...]
        vbuf[0:pr, :] = vp_ref[...]
        vbuf[pr:pr + R, :] = vc_ref[...]
        dkbuf[...] = jnp.zeros_like(dkbuf)
        dvbuf[...] = jnp.zeros_like(dvbuf)

        @pl.when(i > 0)
        def _():
            dkbuf[R:R + pr, :] = ck_car[...]
            dvbuf[R:R + pr, :] = cv_car[...]

        def unit(u, carry):
            sub = u // d
            base = sub * pr + (u - sub * d)
            sl = _strided(base, BAND, d)
            sl2 = _strided(base, 2 * BAND, d)
            q = q_ref[sl, :]
            dov = do_ref[sl, :]
            ov = o_ref[sl, :]
            lv = l_ref[sl, :]
            q2 = jnp.where(hm, jnp.concatenate([q, q], axis=0), 0.0).astype(BF)
            do2 = jnp.where(hm, jnp.concatenate([dov, dov], axis=0), 0.0)
            oo = dov * ov
            delta = jnp.sum(jnp.where(hm, jnp.concatenate([oo, oo], axis=0), 0.0), axis=-1, keepdims=True)
            lse2 = jnp.concatenate([lv[:, 0:1], lv[:, HEAD_DIM:HEAD_DIM + 1]], axis=0)
            do2 = do2.astype(BF)
            k2 = kbuf[sl2, :].astype(BF)
            v2 = vbuf[sl2, :].astype(BF)
            s = lax.dot_general(q2, k2, (((1,), (1,)), ((), ())), preferred_element_type=F32)
            p = jnp.exp(s + bias[((n > 0) | (sub > 0)).astype(jnp.int32)] - lse2)
            dp = lax.dot_general(do2, v2, (((1,), (1,)), ((), ())), preferred_element_type=F32)
            ds = (p * (dp - delta)).astype(BF)
            dq2 = jnp.dot(ds, k2, preferred_element_type=F32)
            dq = jnp.where(low, dq2[:BAND], dq2[BAND:])
            dq_ref[sl, :] = rope_bwd(dq, cos_ref[sl, :], ss_ref[sl, :], first) * scale
            dkbuf[sl2, :] += jnp.dot(ds.T, q2, preferred_element_type=F32)
            dvbuf[sl2, :] += jnp.dot(p.astype(BF).T, do2, preferred_element_type=F32)
            return carry

        lax.fori_loop(0, units, unit, 0, unroll=2)

        dk_ref[...] = rope_bwd(dkbuf[pr:pr + R, :], cos_ref[...], ss_ref[...], _first_half_mask(R))
        dv_ref[...] = dvbuf[pr:pr + R, :]
        ck_car[...] = dkbuf[0:pr, :]
        cv_car[...] = dvbuf[0:pr, :]

    blk = (R, LANES)
    pblk = (pr, LANES)
    cur = lambda hp, i: nblk - 1 - i
    prv = lambda hp, i: jnp.maximum((nblk - 1 - i) * per - 1, 0)
    in_specs = [
        pl.BlockSpec(blk, lambda hp, i: (cur(hp, i), g * HP + hp)),
        pl.BlockSpec(pblk, lambda hp, i: (prv(hp, i), g * HP + hp)),
        pl.BlockSpec(blk, lambda hp, i: (cur(hp, i), g * HP + hp)),
        pl.BlockSpec(pblk, lambda hp, i: (prv(hp, i), (NB + g) * HP + hp)),
        pl.BlockSpec(blk, lambda hp, i: (cur(hp, i), (NB + g) * HP + hp)),
        pl.BlockSpec(blk, lambda hp, i: (cur(hp, i), hp)),
        pl.BlockSpec(blk, lambda hp, i: (cur(hp, i), hp)),
        pl.BlockSpec(blk, lambda hp, i: (cur(hp, i), hp)),
        pl.BlockSpec(blk, lambda hp, i: (cur(hp, i), 0)),
        pl.BlockSpec(blk, lambda hp, i: (cur(hp, i), 0)),
    ]
    args = [q_all, kv_all, kv_all, kv_all, kv_all, do, o, lse, cos2, ss2]
    if dep is not None:
        in_specs.append(ANY)
        args.append(dep)
    aliases = {}
    if prev is not None:
        in_specs += [ANY, ANY, ANY]
        aliases = {len(args): 0, len(args) + 1: 1, len(args) + 2: 2}
        args += list(prev)
    wide = jax.ShapeDtypeStruct((T, NB * Dm), F32)
    out = pl.BlockSpec(blk, lambda hp, i: (cur(hp, i), g * HP + hp))
    return pl.pallas_call(
        body, name=name,
        grid=(HP, nblk),
        in_specs=in_specs,
        out_specs=[out, out, out],
        out_shape=[wide, wide, wide],
        scratch_shapes=[pltpu.VMEM((pr + R, LANES), F32), pltpu.VMEM((pr + R, LANES), F32),
                        pltpu.VMEM((pr + R, LANES), F32), pltpu.VMEM((pr + R, LANES), F32),
                        pltpu.VMEM(pblk, F32), pltpu.VMEM(pblk, F32),
                        pltpu.VMEM((2, 2 * BAND, 2 * BAND), F32)],
        input_output_aliases=aliases,
        compiler_params=_cp("arbitrary", "arbitrary"),
    )(*args)


def _postnorm_bwd(dh, y, g_ref_val):
    r = lax.rsqrt(jnp.mean(y * y, axis=-1, keepdims=True) + RMS_EPS)
    yn = y * r
    dyn = dh * g_ref_val
    dy = r * (dyn - yn * jnp.mean(dyn * yn, axis=-1, keepdims=True))
    return dy, yn


def _after(body, n_in, dep):
    if dep is None:
        return body
    return lambda *refs: body(*refs[:n_in], *refs[n_in + 1:])


def _dep_spec(dep):
    return [] if dep is None else [ANY]


def _dep_arg(dep):
    return [] if dep is None else [dep]


def _postnorm_bwd_matmul(dh, y, gain, w3, widx, *, name, da_dtype, dep=None):
    T, D = dh.shape
    K = w3.shape[1]
    tm = min(ROW_BLOCK, T)

    def body(dh_ref, y_ref, g_ref, w_ref, dy_ref, da_ref, dg_ref):
        i = pl.program_id(0)

        @pl.when(i == 0)
        def _():
            dg_ref[...] = jnp.zeros_like(dg_ref)

        dhv = dh_ref[...]
        dy, yn = _postnorm_bwd(dhv, y_ref[...].astype(F32), g_ref[...])
        dg_ref[...] += jnp.sum(dhv * yn, axis=0, keepdims=True)
        dyb = dy.astype(BF)
        dy_ref[...] = dyb
        da = lax.dot_general(dyb, w_ref[...], (((1,), (1,)), ((), ())), preferred_element_type=F32)
        da_ref[...] = da.astype(da_dtype)

    return pl.pallas_call(
        _after(body, 4, dep), name=name,
        grid=(T // tm,),
        in_specs=[
            pl.BlockSpec((tm, D), lambda i: (i, 0)),
            pl.BlockSpec((tm, D), lambda i: (i, 0)),
            pl.BlockSpec((1, D), lambda i: (0, 0)),
            pl.BlockSpec((None, K, D), lambda i: (widx, 0, 0)),
        ] + _dep_spec(dep),
        out_specs=[pl.BlockSpec((tm, D), lambda i: (i, 0)),
                   pl.BlockSpec((tm, K), lambda i: (i, 0)),
                   pl.BlockSpec((1, D), lambda i: (0, 0))],
        out_shape=[jax.ShapeDtypeStruct((T, D), BF), jax.ShapeDtypeStruct((T, K), da_dtype),
                   jax.ShapeDtypeStruct((1, D), F32)],
        compiler_params=_cp("arbitrary"),
    )(dh, y, gain, w3, *_dep_arg(dep))


def _postnorm_bwd_swiglu(dh, y, gain, wd3, layer, g, u, *, name, dep=None):
    T, D = dh.shape
    F = wd3.shape[1]
    nf = F // 2
    tm = min(ROW_BLOCK, T)

    def body(dh_ref, y_ref, g_ref, w_ref, gg_ref, uu_ref, dy_ref, dgo_ref, duo_ref, dgain_ref, dys):
        i = pl.program_id(0)
        j = pl.program_id(1)

        @pl.when((i == 0) & (j == 0))
        def _():
            dgain_ref[...] = jnp.zeros_like(dgain_ref)

        @pl.when(j == 0)
        def _():
            dhv = dh_ref[...]
            dy, yn = _postnorm_bwd(dhv, y_ref[...].astype(F32), g_ref[...])
            dgain_ref[...] += jnp.sum(dhv * yn, axis=0, keepdims=True)
            dyb = dy.astype(BF)
            dys[...] = dyb
            dy_ref[...] = dyb

        da = lax.dot_general(dys[...], w_ref[...], (((1,), (1,)), ((), ())), preferred_element_type=F32)
        gv = gg_ref[...].astype(F32)
        uv = uu_ref[...].astype(F32)
        sg = jax.nn.sigmoid(gv)
        silu = gv * sg
        dgo_ref[...] = (da * uv * (sg + silu * (1.0 - sg))).astype(BF)
        duo_ref[...] = (da * silu).astype(BF)

    act = jax.ShapeDtypeStruct((T, F), BF)
    return pl.pallas_call(
        _after(body, 6, dep), name=name,
        grid=(T // tm, 2),
        in_specs=[
            pl.BlockSpec((tm, D), lambda i, j: (i, 0)),
            pl.BlockSpec((tm, D), lambda i, j: (i, 0)),
            pl.BlockSpec((1, D), lambda i, j: (0, 0)),
            pl.BlockSpec((None, nf, D), lambda i, j: (layer, j, 0)),
            pl.BlockSpec((tm, nf), lambda i, j: (i, j)),
            pl.BlockSpec((tm, nf), lambda i, j: (i, j)),
        ] + _dep_spec(dep),
        out_specs=[pl.BlockSpec((tm, D), lambda i, j: (i, 0)),
                   pl.BlockSpec((tm, nf), lambda i, j: (i, j)),
                   pl.BlockSpec((tm, nf), lambda i, j: (i, j)),
                   pl.BlockSpec((1, D), lambda i, j: (0, 0))],
        out_shape=[jax.ShapeDtypeStruct((T, D), BF), act, act, jax.ShapeDtypeStruct((1, D), F32)],
        scratch_shapes=[pltpu.VMEM((tm, D), BF)],
        compiler_params=_cp("arbitrary", "arbitrary"),
    )(dh, y, gain, wd3, g, u, *_dep_arg(dep))


def _matmul_prenorm_bwd(dzs, wg, layer, h, gain, dh_in, *, name):
    T, D = h.shape
    n = wg.shape[2]
    tm = min(ROW_BLOCK, T)
    pair = len(dzs) == 2
    nj = N_CHIPS // 2 if pair else N_CHIPS

    def body(*refs):
        dz_refs = refs[:len(dzs)]
        w_refs = refs[len(dzs):2 * len(dzs)]
        h_ref, g_ref, dhi_ref, dh_ref, dg_ref, acc = refs[2 * len(dzs):]
        i = pl.program_id(0)
        j = pl.program_id(1)

        @pl.when((i == 0) & (j == 0))
        def _():
            dg_ref[...] = jnp.zeros_like(dg_ref)

        part = None
        for dz_ref, w_ref in zip(dz_refs, w_refs):
            t = lax.dot_general(dz_ref[...].astype(BF), w_ref[...], (((1,), (1,)), ((), ())),
                                preferred_element_type=F32)
            part = t if part is None else part + t

        @pl.when(j == 0)
        def _():
            acc[...] = part

        @pl.when(j > 0)
        def _():
            acc[...] += part

        @pl.when(j == nj - 1)
        def _():
            dhn = acc[...]
            hv = h_ref[...]
            r = lax.rsqrt(jnp.mean(hv * hv, axis=-1, keepdims=True) + RMS_EPS)
            xh = hv * r
            dg_ref[...] += jnp.sum(dhn * xh, axis=0, keepdims=True)
            dxn = dhn * g_ref[...]
            dh_ref[...] = dhi_ref[...] + r * (dxn - xh * jnp.mean(dxn * xh, axis=-1, keepdims=True))

    in_specs = [pl.BlockSpec((tm, n), lambda i, j: (i, j)) for _ in dzs]
    if pair:
        in_specs += [pl.BlockSpec((None, D, n), lambda i, j: (j, layer, 0)),
                     pl.BlockSpec((None, D, n), lambda i, j: (j + nj, layer, 0))]
    else:
        in_specs += [pl.BlockSpec((None, D, n), lambda i, j: (j, layer, 0))]
    in_specs += [pl.BlockSpec((tm, D), lambda i, j: (i, 0)),
                 pl.BlockSpec((1, D), lambda i, j: (0, 0)),
                 pl.BlockSpec((tm, D), lambda i, j: (i, 0))]
    return pl.pallas_call(
        body, name=name,
        grid=(T // tm, nj),
        in_specs=in_specs,
        out_specs=[pl.BlockSpec((tm, D), lambda i, j: (i, 0)), pl.BlockSpec((1, D), lambda i, j: (0, 0))],
        out_shape=[jax.ShapeDtypeStruct((T, D), F32), jax.ShapeDtypeStruct((1, D), F32)],
        scratch_shapes=[pltpu.VMEM((tm, D), F32)],
        compiler_params=_cp("arbitrary", "arbitrary"),
    )(*dzs, *([wg] * len(dzs)), h, gain, dh_in)


def _grad_matmul(a, b, out_shape3, tme, tne, out_index, prev, *, name):
    T, M = a.shape
    N = b.shape[1]
    tk = min(ROW_BLOCK, T)
    nk = T // tk

    def body(a_ref, b_ref, *rest):
        o_ref, acc = rest[-2:]
        k = pl.program_id(2)
        part = jnp.dot(a_ref[...].astype(BF).T, b_ref[...].astype(BF), preferred_element_type=F32)

        @pl.when(k == 0)
        def _():
            acc[...] = part

        @pl.when(k > 0)
        def _():
            acc[...] += part

        @pl.when(k == nk - 1)
        def _():
            o_ref[...] = acc[...].astype(BF)

    in_specs = [pl.BlockSpec((tk, tme), lambda i, j, k: (k, i)),
                pl.BlockSpec((tk, tne), lambda i, j, k: (k, j))]
    args = [a, b]
    aliases = {}
    if prev is not None:
        in_specs.append(ANY)
        args.append(prev)
        aliases = {2: 0}
    return pl.pallas_call(
        body, name=name,
        grid=(M // tme, N // tne, nk),
        in_specs=in_specs,
        out_specs=pl.BlockSpec((None, tme, tne), lambda i, j, k: out_index(i, j)),
        out_shape=jax.ShapeDtypeStruct(out_shape3, BF),
        scratch_shapes=[pltpu.VMEM((tme, tne), F32)],
        input_output_aliases=aliases,
        compiler_params=_cp("parallel", "parallel", "arbitrary"),
    )(*args)


def _row_tile(R):
    for t in (512, 256, 128, 64, 32, 16, 8):
        if R % t == 0:
            return t
    return R


def _cast_place(w2d, half_major, where, dtype, *, name):
    R, C = w2d.shape
    rh = R // 2 if half_major else R
    tr = _row_tile(rh)
    nr = rh // tr

    def body(s_ref, w_ref, o_ref):
        o_ref[...] = w_ref[...].astype(o_ref.dtype)

    if half_major:
        out_spec = pl.BlockSpec((None, None, tr, C), lambda h, i, s: (h, s[0], i, 0))
        shape = (2, N_CHIPS, rh, C)
    else:
        out_spec = pl.BlockSpec((None, tr, C), lambda h, i, s: (s[0], i, 0))
        shape = (N_CHIPS, R, C)
    return pl.pallas_call(
        body, name=name,
        grid_spec=pltpu.PrefetchScalarGridSpec(
            num_scalar_prefetch=1, grid=(R // rh, nr),
            in_specs=[pl.BlockSpec((tr, C), lambda h, i, s: (h * nr + i, 0))],
            out_specs=out_spec),
        out_shape=jax.ShapeDtypeStruct(shape, dtype),
        compiler_params=_cp("arbitrary", "arbitrary"),
    )(where, w2d)


def _pair_sum(dw, theirs, where, *, name):
    G, rh, C = theirs.shape
    tr = _row_tile(rh)
    nr = rh // tr

    def body(s_ref, a_ref, b_ref, o_ref):
        o_ref[...] = (a_ref[...].astype(F32) + b_ref[...].astype(F32)).astype(BF)

    mine = pl.BlockSpec((None, tr, C), lambda g, i, s: (g, s[1] * nr + i, 0))
    spec = pl.BlockSpec((None, tr, C), lambda g, i, s: (g, i, 0))
    return pl.pallas_call(
        body, name=name,
        grid_spec=pltpu.PrefetchScalarGridSpec(
            num_scalar_prefetch=1, grid=(G, nr), in_specs=[mine, spec], out_specs=spec),
        out_shape=jax.ShapeDtypeStruct((G, rh, C), BF),
        compiler_params=_cp("arbitrary", "arbitrary"),
    )(where, dw, theirs)


def _chip_sum(landed, parts, where, total_rows, row_off, prev, *, name):
    G, rh, C = landed.shape
    tr = _row_tile(rh)
    nr = rh // tr
    base = row_off // tr

    def body(s_ref, l_ref, p_ref, *rest):
        o_ref = rest[-1]
        for j in range(G):
            def own(j=j):
                v = p_ref[...].astype(F32)
                o_ref[...] = v if j == 0 else o_ref[...] + v

            def other(j=j):
                v = l_ref[j].astype(F32)
                o_ref[...] = v if j == 0 else o_ref[...] + v

            pl.when(s_ref[0] == j)(own)
            pl.when(s_ref[0] != j)(other)

    in_specs = [pl.BlockSpec((G, tr, C), lambda i, s: (0, i, 0)),
                pl.BlockSpec((None, tr, C), lambda i, s: (s[0], i, 0))]
    args = [where, landed, parts]
    aliases = {}
    if prev is not None:
        in_specs.append(ANY)
        args.append(prev)
        aliases = {3: 0}
    return pl.pallas_call(
        body, name=name,
        grid_spec=pltpu.PrefetchScalarGridSpec(
            num_scalar_prefetch=1, grid=(nr,),
            in_specs=in_specs,
            out_specs=pl.BlockSpec((tr, C), lambda i, s: (base + s[1] * nr + i, 0))),
        out_shape=jax.ShapeDtypeStruct((total_rows, C), F32),
        input_output_aliases=aliases,
        compiler_params=_cp("arbitrary"),
    )(*args)


def _adamw(w, g, m, v, *, name):
    R, C = w.shape
    tr = _row_tile(R)

    def body(w_ref, g_ref, m_ref, v_ref, d_ref, mo_ref, vo_ref):
        gv = g_ref[...]
        mn = ADAM_B1 * m_ref[...] + (1.0 - ADAM_B1) * gv
        vn = ADAM_B2 * v_ref[...] + (1.0 - ADAM_B2) * jnp.square(gv)
        m_hat = mn / (1.0 - ADAM_B1 ** ADAM_STEP)
        v_hat = vn / (1.0 - ADAM_B2 ** ADAM_STEP)
        d_ref[...] = -ADAM_LR * (m_hat / (jnp.sqrt(v_hat) + ADAM_EPS) + ADAM_WD * w_ref[...])
        mo_ref[...] = mn
        vo_ref[...] = vn

    spec = pl.BlockSpec((tr, C), lambda i: (i, 0))
    shp = jax.ShapeDtypeStruct((R, C), F32)
    return pl.pallas_call(
        body, name=name, grid=(R // tr,), in_specs=[spec] * 4, out_specs=[spec] * 3,
        out_shape=[shp, shp, shp], compiler_params=_cp("parallel"),
    )(w, g, m, v)


def _place():
    x = lax.axis_index("x")
    y = lax.axis_index("y")
    c = lax.axis_index("c")
    chips = [(1 - x, y), (x, 1 - y), (1 - x, 1 - y)]
    return x, y, c, chips


def _half_of(ref, j, h, rh, half_major, r0=0, rc=None):
    rc = rh if rc is None else rc
    return ref.at[h, j, pl.ds(r0, rc)] if half_major else ref.at[j, pl.ds(h * rh + r0, rc)]


def _chunk_rows(rows, row_bytes, align):
    if rows <= align:
        return rows
    cands = [r for r in range(align, rows + 1, align) if rows % r == 0]
    fit = [r for r in cands if r * row_bytes <= DMA_CHUNK_BYTES]
    return max(fit) if fit else min(cands)


def _row_align(dtype):
    return 8 * (4 // jnp.dtype(dtype).itemsize)


def _start_chunks(make, rows, rc):
    for r0 in range(0, rows, rc):
        make(r0, rc).start()


def _all_gather_weights(placed, half_major, small):
    nm = len(placed)
    halves = []
    for p, hmaj in zip(placed, half_major):
        halves.append((p.shape[2], p.shape[3]) if hmaj else (p.shape[1] // 2, p.shape[2]))
    out_shapes = [jax.ShapeDtypeStruct(p.shape, p.dtype) for p in placed]
    out_shapes.append(jax.ShapeDtypeStruct(small.shape, small.dtype))

    def body(*refs):
        outs = refs[nm + 1:2 * nm + 1]
        sm_out = refs[2 * nm + 1]
        ssem, rsem, fsem, gsem, sm_s, sm_r = refs[2 * nm + 2:]
        x, y, c, chips = _place()
        myj = 2 * x + y
        sib = (x, y, 1 - c)

        def spot(m, j, h, r0=0, rc=None):
            return _half_of(outs[m], j, h, halves[m][0], half_major[m], r0, rc)

        def chunk(m):
            rh, cols = halves[m]
            return rh, _chunk_rows(rh, cols * placed[m].dtype.itemsize, _row_align(placed[m].dtype))

        first = []
        for m in range(nm):
            rh, rc = chunk(m)
            for k, (px, py) in enumerate(chips):
                def send(r0, n, m=m, k=k, px=px, py=py):
                    part = spot(m, myj, c, r0, n)
                    return pltpu.make_async_remote_copy(
                        src_ref=part, dst_ref=part,
                        send_sem=ssem.at[3 * m + k], recv_sem=rsem.at[3 * m + k],
                        device_id=(px, py, c), device_id_type=MESH)
                _start_chunks(send, rh, rc)
                first.append(send(0, rh))
        for k, (px, py) in enumerate(chips):
            cp = pltpu.make_async_remote_copy(
                src_ref=sm_out.at[myj], dst_ref=sm_out.at[myj], send_sem=sm_s.at[k], recv_sem=sm_r.at[k],
                device_id=(px, py, c), device_id_type=MESH)
            cp.start()
            first.append(cp)

        passed = []
        for m in range(nm):
            rh, rc = chunk(m)
            for k, (px, py) in enumerate(chips):
                pj = 2 * px + py
                landed = spot(m, pj, c)
                pltpu.make_async_remote_copy(
                    src_ref=landed, dst_ref=landed, send_sem=ssem.at[3 * m + k], recv_sem=rsem.at[3 * m + k],
                    device_id=(px, py, c), device_id_type=MESH).wait_recv()

                def fwd(r0, n, m=m, k=k, pj=pj):
                    part = spot(m, pj, c, r0, n)
                    return pltpu.make_async_remote_copy(
                        src_ref=part, dst_ref=part, send_sem=fsem.at[3 * m + k], recv_sem=gsem.at[3 * m + k],
                        device_id=sib, device_id_type=MESH)
                _start_chunks(fwd, rh, rc)
                passed.append(fwd(0, rh))
        for m in range(nm):
            for k, (px, py) in enumerate(chips):
                theirs = spot(m, 2 * px + py, 1 - c)
                pltpu.make_async_remote_copy(
                    src_ref=theirs, dst_ref=theirs, send_sem=fsem.at[3 * m + k], recv_sem=gsem.at[3 * m + k],
                    device_id=sib, device_id_type=MESH).wait_recv()
        for k, (px, py) in enumerate(chips):
            got = sm_out.at[2 * px + py]
            pltpu.make_async_remote_copy(
                src_ref=got, dst_ref=got, send_sem=sm_s.at[k], recv_sem=sm_r.at[k],
                device_id=(px, py, c), device_id_type=MESH).wait_recv()
        for cp in first + passed:
            cp.wait_send()

    return pl.pallas_call(
        body, name="ag_weights",
        in_specs=[ANY] * (nm + 1),
        out_specs=[ANY] * (nm + 1),
        out_shape=out_shapes,
        input_output_aliases={i: i for i in range(nm + 1)},
        scratch_shapes=[pltpu.SemaphoreType.DMA((3 * nm,)), pltpu.SemaphoreType.DMA((3 * nm,)),
                        pltpu.SemaphoreType.DMA((3 * nm,)), pltpu.SemaphoreType.DMA((3 * nm,)),
                        pltpu.SemaphoreType.DMA((3,)), pltpu.SemaphoreType.DMA((3,))],
    )(*placed, small)


def _sibling_swap(dws, *, name):
    nm = len(dws)
    shapes = [jax.ShapeDtypeStruct((dw.shape[0], dw.shape[1] // 2, dw.shape[2]), dw.dtype) for dw in dws]

    def body(*refs):
        ins = refs[:nm]
        theirs = refs[nm:2 * nm]
        ssem, rsem = refs[2 * nm:]
        x, y, c, _ = _place()
        sib = (x, y, 1 - c)
        cps = []
        for m in range(nm):
            G, rh, cols = shapes[m].shape
            rc = _chunk_rows(rh, cols * shapes[m].dtype.itemsize, _row_align(shapes[m].dtype))
            for j in range(G):
                _start_chunks(lambda r0, n, m=m, j=j, rh=rh: pltpu.make_async_remote_copy(
                    src_ref=ins[m].at[j, pl.ds((1 - c) * rh + r0, n)],
                    dst_ref=theirs[m].at[j, pl.ds(r0, n)], send_sem=ssem.at[m], recv_sem=rsem.at[m],
                    device_id=sib, device_id_type=MESH), rh, rc)
            cps.append(pltpu.make_async_remote_copy(
                src_ref=ins[m].at[:, pl.ds((1 - c) * rh, rh), :], dst_ref=theirs[m],
                send_sem=ssem.at[m], recv_sem=rsem.at[m], device_id=sib, device_id_type=MESH))
        for cp in cps:
            cp.wait()

    return pl.pallas_call(
        body, name=name,
        in_specs=[ANY] * nm, out_specs=[ANY] * nm, out_shape=shapes,
        scratch_shapes=[pltpu.SemaphoreType.DMA((nm,)), pltpu.SemaphoreType.DMA((nm,))],
    )(*dws)


HBM = pl.BlockSpec(memory_space=pltpu.HBM)
SEM = pl.BlockSpec(memory_space=pltpu.SEMAPHORE)
EFFECT = pltpu.SideEffectType.DATAFLOW_SIDE_EFFECTING


def _in_hbm(a):
    return pltpu.with_memory_space_constraint(a, pltpu.HBM)


def _exchange_start(parts, *, name):
    nm = len(parts)

    def body(*refs):
        ins = refs[:nm]
        lands = refs[nm:2 * nm]
        ssem, rsem = refs[2 * nm:2 * nm + 2]
        token = refs[-1]
        x, y, c, chips = _place()
        myj = 2 * x + y
        for m in range(nm):
            _, rh, cols = parts[m].shape
            rc = _chunk_rows(rh, cols * parts[m].dtype.itemsize, _row_align(parts[m].dtype))
            for k, (px, py) in enumerate(chips):
                _start_chunks(lambda r0, n, m=m, k=k, px=px, py=py: pltpu.make_async_remote_copy(
                    src_ref=ins[m].at[2 * px + py, pl.ds(r0, n)], dst_ref=lands[m].at[myj, pl.ds(r0, n)],
                    send_sem=ssem.at[3 * m + k], recv_sem=rsem.at[3 * m + k],
                    device_id=(px, py, c), device_id_type=MESH), rh, rc)
        token[...] = jnp.zeros_like(token)

    bufs = [pltpu.HBM(p.shape, p.dtype) for p in parts]
    outs = pl.pallas_call(
        body, name=name,
        out_shape=(pltpu.SemaphoreType.DMA((3 * nm,)), pltpu.SemaphoreType.DMA((3 * nm,)), *bufs, *bufs,
                   jax.ShapeDtypeStruct((8, LANES), F32)),
        in_specs=[HBM] * (2 * nm),
        out_specs=(SEM, SEM, *([HBM] * (2 * nm)), pl.BlockSpec(memory_space=pltpu.VMEM)),
        input_output_aliases={i: 2 + i for i in range(2 * nm)},
        compiler_params=pltpu.CompilerParams(has_side_effects=EFFECT),
    )(*[_in_hbm(p) for p in parts], *[_in_hbm(lax.empty(p.shape, p.dtype)) for p in parts])
    return (outs[0], outs[1], list(outs[2:2 + nm]), list(outs[2 + nm:2 + 2 * nm])), outs[-1]


def _exchange_wait(handle, after, *, name):
    ssem_in, rsem_in, parts, lands = handle
    nm = len(parts)

    def body(*refs):
        ins = refs[:nm]
        lnd = refs[nm:2 * nm]
        ssem, rsem = refs[2 * nm:2 * nm + 2]
        x, y, c, chips = _place()
        for m in range(nm):
            for k, (px, py) in enumerate(chips):
                pj = 2 * px + py
                cp = pltpu.make_async_remote_copy(
                    src_ref=ins[m].at[pj], dst_ref=lnd[m].at[pj],
                    send_sem=ssem.at[3 * m + k], recv_sem=rsem.at[3 * m + k],
                    device_id=(px, py, c), device_id_type=MESH)
                cp.wait_send()
                cp.wait_recv()

    bufs = [pltpu.HBM(p.shape, p.dtype) for p in parts]
    outs = pl.pallas_call(
        body, name=name,
        out_shape=(*bufs, *bufs),
        in_specs=[HBM] * (2 * nm) + [SEM, SEM, ANY],
        out_specs=[HBM] * (2 * nm),
        input_output_aliases={i: i for i in range(2 * nm)},
        compiler_params=pltpu.CompilerParams(has_side_effects=EFFECT),
    )(*parts, *lands, ssem_in, rsem_in, after)
    return list(outs[nm:])


def _sibling_join(grads, regions):
    nm = len(grads)
    nr = len(regions)
    shapes = [jax.ShapeDtypeStruct(g.shape, g.dtype) for g in grads]

    def body(*refs):
        outs = refs[nm:2 * nm]
        ssem, rsem = refs[2 * nm:]
        x, y, c, _ = _place()
        sib = (x, y, 1 - c)
        cps = []
        for i, (m, off, rows) in enumerate(regions):
            rh, cols = rows // 2, grads[m].shape[1]
            rc = _chunk_rows(rh, cols * grads[m].dtype.itemsize, _row_align(grads[m].dtype))

            def send(r0, n, i=i, m=m, off=off, rh=rh):
                part = outs[m].at[pl.ds(off + c * rh + r0, n)]
                return pltpu.make_async_remote_copy(
                    src_ref=part, dst_ref=part, send_sem=ssem.at[i], recv_sem=rsem.at[i],
                    device_id=sib, device_id_type=MESH)
            _start_chunks(send, rh, rc)
            cps.append(send(0, rh))
        for i, (m, off, rows) in enumerate(regions):
            rh = rows // 2
            cps[i].wait_send()
            got = outs[m].at[pl.ds(off + (1 - c) * rh, rh)]
            pltpu.make_async_remote_copy(
                src_ref=got, dst_ref=got, send_sem=ssem.at[i], recv_sem=rsem.at[i],
                device_id=sib, device_id_type=MESH).wait_recv()

    return pl.pallas_call(
        body, name="rs_sibling_join",
        in_specs=[ANY] * nm, out_specs=[ANY] * nm, out_shape=shapes,
        input_output_aliases={i: i for i in range(nm)},
        scratch_shapes=[pltpu.SemaphoreType.DMA((nr,)), pltpu.SemaphoreType.DMA((nr,))],
    )(*grads)


def _all_reduce_small(pack):
    R, C = pack.shape

    def body(in_ref, out_ref, slots, ssem, rsem):
        x, y, c, _ = _place()
        me = 4 * x + 2 * y + c
        slots[me] = in_ref[...]
        cps = []
        for k in range(1, N_DEV):
            dx, dy, dc = (k >> 2) & 1, (k >> 1) & 1, k & 1
            peer = (x ^ dx, y ^ dy, c ^ dc)
            cp = pltpu.make_async_remote_copy(
                src_ref=in_ref, dst_ref=slots.at[me], send_sem=ssem.at[k], recv_sem=rsem.at[k],
                device_id=peer, device_id_type=MESH)
            cp.start()
            cps.append(cp)
        for k in range(1, N_DEV):
            dx, dy, dc = (k >> 2) & 1, (k >> 1) & 1, k & 1
            got = slots.at[4 * (x ^ dx) + 2 * (y ^ dy) + (c ^ dc)]
            pltpu.make_async_remote_copy(
                src_ref=got, dst_ref=got, send_sem=ssem.at[k], recv_sem=rsem.at[k],
                device_id=(x ^ dx, y ^ dy, c ^ dc), device_id_type=MESH).wait_recv()
        for cp in cps:
            cp.wait_send()
        acc = slots[0]
        for s in range(1, N_DEV):
            acc = acc + slots[s]
        out_ref[...] = acc

    return pl.pallas_call(
        body, name="ar_small",
        in_specs=[pl.BlockSpec(memory_space=pltpu.VMEM)],
        out_specs=pl.BlockSpec(memory_space=pltpu.VMEM),
        out_shape=jax.ShapeDtypeStruct((R, C), F32),
        scratch_shapes=[pltpu.VMEM((N_DEV, R, C), F32),
                        pltpu.SemaphoreType.DMA((N_DEV,)), pltpu.SemaphoreType.DMA((N_DEV,))],
    )(pack)


def kernel(x, positions, mix_norm_pre, mix_norm_post, ffn_norm_pre, ffn_norm_post, ffn_w_gate_up, ffn_w_down, conv_w_in, conv_w, conv_w_out, kv_norm, w_kv, w_q, w_o, loss_target, m_mix_norm_pre, m_mix_norm_post, m_ffn_norm_pre, m_ffn_norm_post, m_ffn_w_gate_up, m_ffn_w_down, m_conv_w_in, m_conv_w, m_conv_w_out, m_kv_norm, m_w_kv, m_w_q, m_w_o, v_mix_norm_pre, v_mix_norm_post, v_ffn_norm_pre, v_ffn_norm_post, v_ffn_w_gate_up, v_ffn_w_down, v_conv_w_in, v_conv_w, v_conv_w_out, v_kv_norm, v_w_kv, v_w_q, v_w_o):
    T, D = x.shape[1], x.shape[2]
    L = ffn_w_gate_up.shape[0]
    n_gu = ffn_w_gate_up.shape[2]
    f_sh = ffn_w_down.shape[1]
    F = N_CHIPS * f_sh
    x0 = x[0]
    tgt = loss_target[0]

    half = HEAD_DIM // 2
    inv_freq = ROPE_THETA ** (-jnp.arange(half, dtype=F32) / half)
    ang = positions[0].astype(F32)[:, None] * inv_freq
    cosv, sinv = jnp.cos(ang), jnp.sin(ang)
    cos2 = jnp.tile(cosv, (1, LANES // half))
    ss2 = jnp.tile(jnp.concatenate([-sinv, sinv], axis=1), (1, LANES // HEAD_DIM))

    def as2d(a):
        return a.reshape(-1, a.shape[-1])

    big = [ffn_w_gate_up, ffn_w_down, conv_w_in, conv_w_out, w_kv, w_q, w_o]
    big_m = [m_ffn_w_gate_up, m_ffn_w_down, m_conv_w_in, m_conv_w_out, m_w_kv, m_w_q, m_w_o]
    big_v = [v_ffn_w_gate_up, v_ffn_w_down, v_conv_w_in, v_conv_w_out, v_w_kv, v_w_q, v_w_o]
    half_major = [False, True, False, False, False, False, False]
    chip = 2 * lax.axis_index("x") + lax.axis_index("y")
    where = jnp.stack([chip, lax.axis_index("c")]).astype(jnp.int32)
    tc = conv_w.shape[2]
    cw_pad = jnp.concatenate([conv_w[0], jnp.zeros((8 - conv_w.shape[1], tc), F32)], axis=0)

    placed = [_cast_place(as2d(w), hm, where, BF, name=f"place{i}")
              for i, (w, hm) in enumerate(zip(big, half_major))]
    gathered = _all_gather_weights(placed, half_major, _cast_place(cw_pad, False, where, F32, name="place_taps"))
    wgu, wd4, wci, wco4, wkv, wq, wo4, cw = gathered
    wd = wd4.reshape(L, F, D)
    wco = wco4.reshape(1, D, D)
    wo = wo4.reshape(1, D, D)

    def row(a, i):
        return a[i:i + 1]

    z, hn_m0 = _norm_matmul(x0, row(mix_norm_pre, 0), wci, cos2, ss2, name="f0_conv_in",
                            rope_shards=0, scale=1.0, out_dtype=BF)
    vmix = _conv_fwd(z, cw, name="f0_conv")
    y0, h1 = _matmul_postnorm(vmix, wco, 0, row(mix_norm_post, 0), x0, name="f0_conv_out")
    g0, u0, a0, hn_f0 = _norm_swiglu(h1, row(ffn_norm_pre, 0), wgu, 0, name="f0_gate_up")
    f0, h2 = _matmul_postnorm(a0, wd, 0, row(ffn_norm_post, 0), h1, name="f0_down")

    kv_all, hn_kv = _norm_matmul(h2, kv_norm.reshape(1, D), wkv, cos2, ss2, name="f1_kv",
                                 rope_shards=N_CHIPS // 2, scale=1.0, out_dtype=F32)
    q_all, hn_m1 = _norm_matmul(h2, row(mix_norm_pre, 1), wq, cos2, ss2, name="f1_q",
                                rope_shards=N_CHIPS, scale=HEAD_DIM ** -0.5, out_dtype=F32)
    o_att, lse = _attn_fwd(q_all, kv_all, name="f1_attn")
    y1, h3 = _matmul_postnorm(o_att, wo, 0, row(mix_norm_post, 1), h2, name="f1_attn_out")
    g1, u1, a1, hn_f1 = _norm_swiglu(h3, row(ffn_norm_pre, 1), wgu, 1, name="f1_gate_up")
    f1, h4 = _matmul_postnorm(a1, wd, 1, row(ffn_norm_post, 1), h3, name="f1_down")

    dh4, sq = _loss_head(h4, tgt, name="loss_head")
    loss_part = 0.5 * sq[0, 0] / D

    gu_shape = (N_CHIPS, D, n_gu)
    in_chips = lambda a: a.reshape(N_CHIPS, -1, a.shape[-1])

    def scatter_start(dws, tag):
        theirs = _sibling_swap(dws, name=f"rs_swap_{tag}")
        parts = [_pair_sum(dw, t, where, name=f"rs_pair_sum_{tag}{i}") for i, (dw, t) in enumerate(zip(dws, theirs))]
        handle, token = _exchange_start(parts, name=f"rs_exchange_start_{tag}")
        return (handle, parts), token

    dyf1, dg1, du1, d_ffn_post1 = _postnorm_bwd_swiglu(dh4, f1, row(ffn_norm_post, 1), wd, 1, g1, u1,
                                                       name="b1_down")
    dwd1 = _grad_matmul(a1, dyf1, (2, F // 2, D), F // 2, D, lambda i, j: (i, 0, 0), None, name="b1_dw_down")
    dwgu1 = _grad_matmul(hn_f1, dg1, gu_shape, D, n_gu, lambda i, j: (j, 0, 0), None, name="b1_dw_gate")
    dwgu1 = _grad_matmul(hn_f1, du1, gu_shape, D, n_gu, lambda i, j: (j + 2, 0, 0), dwgu1, name="b1_dw_up")
    dh3, d_ffn_pre1 = _matmul_prenorm_bwd((dg1, du1), wgu, 1, h3, row(ffn_norm_pre, 1), dh4, name="b1_gate_up")

    dy1, do, d_mix_post1 = _postnorm_bwd_matmul(dh3, y1, row(mix_norm_post, 1), wo, 0, name="b1_attn_out",
                                                da_dtype=F32)
    dwo = _grad_matmul(o_att, dy1, (1, D, D), D, D, lambda i, j: (0, 0, 0), None, name="b1_dw_o")
    rs_a, token = scatter_start([dwgu1, in_chips(dwd1), in_chips(dwo)], "a")
    prev = None
    for gi, (window, dil) in enumerate(BRANCHES):
        prev = _attn_bwd(q_all, kv_all, do, o_att, lse, cos2, ss2, gi, dil, prev, name=f"b1_attn{gi}",
                         dep=token if gi == 0 else None)
    dq_all, dk_all, dv_all = prev
    n_q = wq.shape[2]
    n_kv = wkv.shape[2]
    dwq = _grad_matmul(hn_m1, dq_all, (N_CHIPS, D, n_q), D, n_q, lambda i, j: (j, 0, 0), None, name="b1_dw_q")
    dwkv = _grad_matmul(hn_kv, dk_all, (N_CHIPS, D, n_kv), D, n_kv, lambda i, j: (j, 0, 0), None, name="b1_dw_k")
    dwkv = _grad_matmul(hn_kv, dv_all, (N_CHIPS, D, n_kv), D, n_kv, lambda i, j: (j + 2, 0, 0), dwkv, name="b1_dw_v")
    dh2, d_mix_pre1 = _matmul_prenorm_bwd((dq_all,), wq, 0, h2, row(mix_norm_pre, 1), dh3, name="b1_q")
    dh2, d_kv_norm = _matmul_prenorm_bwd((dk_all, dv_all), wkv, 0, h2, kv_norm.reshape(1, D), dh2, name="b1_kv")
    rs_b, token = scatter_start([dwkv, dwq], "b")

    dyf0, dg0, du0, d_ffn_post0 = _postnorm_bwd_swiglu(dh2, f0, row(ffn_norm_post, 0), wd, 0, g0, u0,
                                                       name="b0_down", dep=token)
    dwd0 = _grad_matmul(a0, dyf0, (2, F // 2, D), F // 2, D, lambda i, j: (i, 0, 0), None, name="b0_dw_down")
    dwgu0 = _grad_matmul(hn_f0, dg0, gu_shape, D, n_gu, lambda i, j: (j, 0, 0), None, name="b0_dw_gate")
    dwgu0 = _grad_matmul(hn_f0, du0, gu_shape, D, n_gu, lambda i, j: (j + 2, 0, 0), dwgu0, name="b0_dw_up")
    dh1, d_ffn_pre0 = _matmul_prenorm_bwd((dg0, du0), wgu, 0, h1, row(ffn_norm_pre, 0), dh2, name="b0_gate_up")
    rs_c, token = scatter_start([dwgu0, in_chips(dwd0)], "c")

    dy0, dvmix, d_mix_post0 = _postnorm_bwd_matmul(dh1, y0, row(mix_norm_post, 0), wco, 0, name="b0_conv_out",
                                                   da_dtype=BF, dep=token)
    dwco = _grad_matmul(vmix, dy0, (1, D, D), D, D, lambda i, j: (0, 0, 0), None, name="b0_dw_conv_out")
    dz, dcw = _conv_bwd(z, cw, dvmix, name="b0_conv")
    n_ci = wci.shape[2]
    dwci = _grad_matmul(hn_m0, dz, (N_CHIPS, D, n_ci), D, n_ci, lambda i, j: (j, 0, 0), None, name="b0_dw_conv_in")
    dx, d_mix_pre0 = _matmul_prenorm_bwd((dz,), wci, 0, x0, row(mix_norm_pre, 0), dh1, name="b0_conv_in")

    pack = jnp.concatenate([
        d_mix_pre0, d_mix_pre1, d_mix_post0, d_mix_post1, d_ffn_pre0, d_ffn_pre1, d_ffn_post0, d_ffn_post1,
        d_kv_norm, dcw[0:3], jnp.full((1, D), loss_part, F32),
        jnp.zeros((SMALL_ROWS - 13, D), F32)], axis=0)
    red = _all_reduce_small(pack)
    loss = red[12, 0]
    myj = 2 * lax.axis_index("x") + lax.axis_index("y")
    g_conv_w = lax.dynamic_slice(red, (9, myj * tc), (3, tc))

    zeros7 = jnp.zeros((SMALL_ROWS - 9, D), F32)
    w_small = jnp.concatenate([mix_norm_pre, mix_norm_post, ffn_norm_pre, ffn_norm_post, kv_norm.reshape(1, D), zeros7], axis=0)
    m_small = jnp.concatenate([m_mix_norm_pre, m_mix_norm_post, m_ffn_norm_pre, m_ffn_norm_post, m_kv_norm.reshape(1, D), zeros7], axis=0)
    v_small = jnp.concatenate([v_mix_norm_pre, v_mix_norm_post, v_ffn_norm_pre, v_ffn_norm_post, v_kv_norm.reshape(1, D), zeros7], axis=0)
    d_small, nm_small, nv_small = _adamw(w_small, red, m_small, v_small, name="adamw_small")

    pad5 = jnp.zeros((5, tc), F32)
    d_cw, nm_cw, nv_cw = _adamw(cw_pad, jnp.concatenate([g_conv_w, pad5], axis=0),
                                jnp.concatenate([m_conv_w[0], pad5], axis=0),
                                jnp.concatenate([v_conv_w[0], pad5], axis=0), name="adamw_conv_w")

    rs_d, _ = scatter_start([dwci, in_chips(dwco)], "d")

    pieces = {"a": [(0, D), (1, f_sh), (6, 0)], "b": [(4, 0), (5, 0)], "c": [(0, 0), (1, 0)], "d": [(2, 0), (3, 0)]}
    grads2d = [None] * len(big)
    regions = []
    for tag, (handle, parts) in (("a", rs_a), ("b", rs_b), ("c", rs_c), ("d", rs_d)):
        landed = _exchange_wait(handle, dx, name=f"rs_exchange_wait_{tag}")
        for i, (l, p, (wi, off)) in enumerate(zip(landed, parts, pieces[tag])):
            total = as2d(big[wi]).shape[0]
            grads2d[wi] = _chip_sum(l, p, where, total, off, grads2d[wi], name=f"rs_chip_sum_{tag}{i}")
            regions.append((wi, off, 2 * l.shape[1]))
    grads2d = _sibling_join(grads2d, regions)

    big_out = []
    for i, (w, gr, m, v) in enumerate(zip(big, grads2d, big_m, big_v)):
        d_, m_, v_ = _adamw(as2d(w), gr, as2d(m), as2d(v), name=f"adamw{i}")
        big_out.append((gr.reshape(w.shape), d_.reshape(w.shape), m_.reshape(w.shape), v_.reshape(w.shape)))

    def small(a):
        return (a[0:2], a[2:4], a[4:6], a[6:8])

    def assemble(sm, cwv, kind):
        pre, post, fpre, fpost = small(sm)
        b = [t[kind] for t in big_out]
        return [pre, post, fpre, fpost, b[0], b[1], b[2], cwv[0:3].reshape(conv_w.shape), b[3],
                sm[8], b[4], b[5].reshape(w_q.shape), b[6].reshape(w_o.shape)]

    grads = assemble(red, jnp.concatenate([g_conv_w, pad5], axis=0), 0)
    deltas = assemble(d_small, d_cw, 1)
    new_m = assemble(nm_small, nm_cw, 2)
    new_v = assemble(nv_small, nv_cw, 3)
    return (loss, dx.reshape(x.shape), *grads, *deltas, *new_m, *new_v)
```

```python
import functools

import jax
import jax.numpy as jnp
from jax import lax
from jax.experimental import pallas as pl
from jax.experimental.pallas import tpu as pltpu

HEAD_DIM = 64
BAND = 128
BRANCHES = ((128, 1), (512, 4), (2048, 16))
ROPE_THETA = 10000.0
RMS_EPS = 1e-6
NEG_INF = -1e30
ADAM_LR = 0.001
ADAM_B1 = 0.9
ADAM_B2 = 0.999
ADAM_EPS = 1e-08
ADAM_WD = 0.01
ADAM_STEP = 10

N_CHIPS = 4
N_DEV = 8
LANES = 128
ROW_BLOCK = 512
ATTN_BLOCK_ROWS = 2048
VMEM_LIMIT = 56 * 1024 * 1024
SMALL_ROWS = 16
DMA_CHUNK_BYTES = 512 * 1024

BF = jnp.bfloat16
F32 = jnp.float32
MESH = pl.DeviceIdType.MESH
ANY = pl.BlockSpec(memory_space=pl.ANY)


def _cp(*sem):
    return pltpu.CompilerParams(dimension_semantics=sem, vmem_limit_bytes=VMEM_LIMIT)


def _rot_half(t, first):
    return jnp.where(first, pltpu.roll(t, 96, 1), pltpu.roll(t, 32, 1))


def _first_half_mask(rows):
    lane = lax.broadcasted_iota(jnp.int32, (rows, LANES), 1)
    return (lane % HEAD_DIM) < (HEAD_DIM // 2)


def _normed_rows(j, rows, x_ref, g_ref, xn_ref, xs, last_start, tm):
    @pl.when(j == 0)
    def _():
        xv = x_ref[...]
        r = lax.rsqrt(jnp.mean(xv * xv, axis=-1, keepdims=True) + RMS_EPS)
        xn = (xv * r * g_ref[...]).astype(BF)
        xs[rows, :] = xn
        xn_ref[...] = xn

    @pl.when(j > 0)
    def _():
        xn_ref[...] = xs[pl.ds(last_start, tm), :]


def _norm_matmul(x, gain, wg, cos2, ss2, *, name, rope_shards, scale, out_dtype):
    T, D = x.shape
    n = wg.shape[2]
    tm = min(ROW_BLOCK, T)

    ni = T // tm

    def body(x_ref, g_ref, w_ref, cos_ref, ss_ref, y_ref, xn_ref, xs):
        j = pl.program_id(0)
        rows = pl.ds(pl.multiple_of(pl.program_id(1) * tm, tm), tm)
        _normed_rows(j, rows, x_ref, g_ref, xn_ref, xs, (ni - 1) * tm, tm)
        acc = jnp.dot(xs[rows, :], w_ref[...], preferred_element_type=F32)

        def plain():
            y_ref[...] = acc.astype(out_dtype)

        def rope():
            cosv = cos_ref[...]
            ssv = ss_ref[...]
            first = _first_half_mask(tm)
            for ci in range(n // LANES):
                t = acc[:, ci * LANES:(ci + 1) * LANES]
                y = (t * cosv + _rot_half(t, first) * ssv) * scale
                y_ref[:, ci * LANES:(ci + 1) * LANES] = y.astype(out_dtype)

        if rope_shards == 0:
            plain()
        elif rope_shards == N_CHIPS:
            rope()
        else:
            pl.when(j < rope_shards)(rope)
            pl.when(j >= rope_shards)(plain)

    first_pass = lambda j, i: (jnp.where(j == 0, i, ni - 1), 0)
    return pl.pallas_call(
        body, name=name,
        grid=(N_CHIPS, ni),
        in_specs=[
            pl.BlockSpec((tm, D), first_pass),
            pl.BlockSpec((1, D), lambda j, i: (0, 0)),
            pl.BlockSpec((None, D, n), lambda j, i: (j, 0, 0)),
            pl.BlockSpec((tm, LANES), lambda j, i: (i, 0)),
            pl.BlockSpec((tm, LANES), lambda j, i: (i, 0)),
        ],
        out_specs=[
            pl.BlockSpec((tm, n), lambda j, i: (i, j)),
            pl.BlockSpec((tm, D), first_pass),
        ],
        out_shape=[jax.ShapeDtypeStruct((T, N_CHIPS * n), out_dtype),
                   jax.ShapeDtypeStruct((T, D), BF)],
        scratch_shapes=[pltpu.VMEM((T, D), BF)],
        compiler_params=_cp("arbitrary", "arbitrary"),
    )(x, gain, wg, cos2, ss2)


def _norm_swiglu(x, gain, wg, layer, *, name):
    T, D = x.shape
    n = wg.shape[2]
    tm = min(ROW_BLOCK, T)

    ni = T // tm

    def body(x_ref, g_ref, wg_ref, wu_ref, go_ref, uo_ref, ao_ref, xn_ref, xs):
        j = pl.program_id(0)
        rows = pl.ds(pl.multiple_of(pl.program_id(1) * tm, tm), tm)
        _normed_rows(j, rows, x_ref, g_ref, xn_ref, xs, (ni - 1) * tm, tm)
        g = jnp.dot(xs[rows, :], wg_ref[...], preferred_element_type=F32)
        u = jnp.dot(xs[rows, :], wu_ref[...], preferred_element_type=F32)
        go_ref[...] = g.astype(BF)
        uo_ref[...] = u.astype(BF)
        ao_ref[...] = (g * jax.nn.sigmoid(g) * u).astype(BF)

    half = N_CHIPS // 2
    first_pass = lambda j, i: (jnp.where(j == 0, i, ni - 1), 0)
    act = jax.ShapeDtypeStruct((T, half * n), BF)
    return pl.pallas_call(
        body, name=name,
        grid=(half, ni),
        in_specs=[
            pl.BlockSpec((tm, D), first_pass),
            pl.BlockSpec((1, D), lambda j, i: (0, 0)),
            pl.BlockSpec((None, D, n), lambda j, i: (j, layer, 0)),
            pl.BlockSpec((None, D, n), lambda j, i: (j + half, layer, 0)),
        ],
        out_specs=[
            pl.BlockSpec((tm, n), lambda j, i: (i, j)),
            pl.BlockSpec((tm, n), lambda j, i: (i, j)),
            pl.BlockSpec((tm, n), lambda j, i: (i, j)),
            pl.BlockSpec((tm, D), first_pass),
        ],
        out_shape=[act, act, act, jax.ShapeDtypeStruct((T, D), BF)],
        scratch_shapes=[pltpu.VMEM((T, D), BF)],
        compiler_params=_cp("arbitrary", "arbitrary"),
    )(x, gain, wg, wg)


def _matmul_postnorm(a, w3, widx, gain, h_old, *, name):
    T, K = a.shape
    D = w3.shape[2]
    tm = min(ROW_BLOCK, T)

    def body(a_ref, w_ref, g_ref, h_ref, y_ref, hn_ref):
        y = jnp.dot(a_ref[...].astype(BF), w_ref[...], preferred_element_type=F32)
        y_ref[...] = y.astype(BF)
        r = lax.rsqrt(jnp.mean(y * y, axis=-1, keepdims=True) + RMS_EPS)
        hn_ref[...] = h_ref[...] + y * r * g_ref[...]

    return pl.pallas_call(
        body, name=name,
        grid=(T // tm,),
        in_specs=[
            pl.BlockSpec((tm, K), lambda i: (i, 0)),
            pl.BlockSpec((None, K, D), lambda i: (widx, 0, 0)),
            pl.BlockSpec((1, D), lambda i: (0, 0)),
            pl.BlockSpec((tm, D), lambda i: (i, 0)),
        ],
        out_specs=[pl.BlockSpec((tm, D), lambda i: (i, 0)),
                   pl.BlockSpec((tm, D), lambda i: (i, 0))],
        out_shape=[jax.ShapeDtypeStruct((T, D), BF), jax.ShapeDtypeStruct((T, D), F32)],
        compiler_params=_cp("parallel"),
    )(a, w3, gain, h_old)


def _loss_head(h, target, *, name):
    T, D = h.shape
    tm = min(ROW_BLOCK, T)

    def body(h_ref, t_ref, dh_ref, s_ref):
        i = pl.program_id(0)

        @pl.when(i == 0)
        def _():
            s_ref[...] = jnp.zeros_like(s_ref)

        e = h_ref[...] - t_ref[...]
        dh_ref[...] = e * (1.0 / D)
        s_ref[...] += jnp.sum(e * e)

    return pl.pallas_call(
        body, name=name,
        grid=(T // tm,),
        in_specs=[pl.BlockSpec((tm, D), lambda i: (i, 0)), pl.BlockSpec((tm, D), lambda i: (i, 0))],
        out_specs=[pl.BlockSpec((tm, D), lambda i: (i, 0)), pl.BlockSpec((8, LANES), lambda i: (0, 0))],
        out_shape=[jax.ShapeDtypeStruct((T, D), F32), jax.ShapeDtypeStruct((8, LANES), F32)],
        compiler_params=_cp("arbitrary"),
    )(h, target)


def _shift_down(u, k):
    row = lax.broadcasted_iota(jnp.int32, u.shape, 0)
    return jnp.where(row >= k, pltpu.roll(u, k, 0), 0.0)


def _shift_up(u, k):
    T = u.shape[0]
    row = lax.broadcasted_iota(jnp.int32, u.shape, 0)
    return jnp.where(row < T - k, pltpu.roll(u, T - k, 0), 0.0)


def _conv_fwd(z, cw, *, name):
    T = z.shape[0]
    D = z.shape[1] // 3
    tc = cw.shape[2]
    nb = D // tc

    def body(b_ref, c_ref, h_ref, w_ref, o_ref):
        u = c_ref[...].astype(F32) * h_ref[...].astype(F32)
        w = w_ref[...]
        conv = w[2:3] * u + w[1:2] * _shift_down(u, 1) + w[0:1] * _shift_down(u, 2)
        o_ref[...] = (b_ref[...].astype(F32) * conv).astype(BF)

    return pl.pallas_call(
        body, name=name,
        grid=(nb,),
        in_specs=[
            pl.BlockSpec((T, tc), lambda j: (0, j)),
            pl.BlockSpec((T, tc), lambda j: (0, nb + j)),
            pl.BlockSpec((T, tc), lambda j: (0, 2 * nb + j)),
            pl.BlockSpec((None, 8, tc), lambda j: (j, 0, 0)),
        ],
        out_specs=pl.BlockSpec((T, tc), lambda j: (0, j)),
        out_shape=jax.ShapeDtypeStruct((T, D), BF),
        compiler_params=_cp("parallel"),
    )(z, z, z, cw)


def _conv_bwd(z, cw, dv, *, name):
    T = z.shape[0]
    D = z.shape[1] // 3
    tc = cw.shape[2]
    nb = D // tc

    def body(b_ref, c_ref, h_ref, w_ref, dv_ref, dz_ref, dw_ref):
        p = pl.program_id(0)
        c = c_ref[...].astype(F32)
        h = h_ref[...].astype(F32)
        u = c * h
        u1 = _shift_down(u, 1)
        u2 = _shift_down(u, 2)
        w = w_ref[...]
        dvv = dv_ref[...].astype(F32)
        dconv = dvv * b_ref[...].astype(F32)
        du = w[2:3] * dconv + w[1:2] * _shift_up(dconv, 1) + w[0:1] * _shift_up(dconv, 2)
        rows = lax.broadcasted_iota(jnp.int32, (8, tc), 0)
        dw = jnp.where(rows == 0, jnp.sum(dconv * u2, axis=0, keepdims=True),
                       jnp.where(rows == 1, jnp.sum(dconv * u1, axis=0, keepdims=True),
                                 jnp.where(rows == 2, jnp.sum(dconv * u, axis=0, keepdims=True), 0.0)))
        dw_ref[...] = dw

        @pl.when(p == 0)
        def _():
            conv = w[2:3] * u + w[1:2] * u1 + w[0:1] * u2
            dz_ref[...] = (dvv * conv).astype(BF)

        @pl.when(p == 1)
        def _():
            dz_ref[...] = (du * h).astype(BF)

        @pl.when(p == 2)
        def _():
            dz_ref[...] = (du * c).astype(BF)

    return pl.pallas_call(
        body, name=name,
        grid=(3, nb),
        in_specs=[
            pl.BlockSpec((T, tc), lambda p, j: (0, j)),
            pl.BlockSpec((T, tc), lambda p, j: (0, nb + j)),
            pl.BlockSpec((T, tc), lambda p, j: (0, 2 * nb + j)),
            pl.BlockSpec((None, 8, tc), lambda p, j: (j, 0, 0)),
            pl.BlockSpec((T, tc), lambda p, j: (0, j)),
        ],
        out_specs=[pl.BlockSpec((T, tc), lambda p, j: (0, p * nb + j)),
                   pl.BlockSpec((8, tc), lambda p, j: (p, j))],
        out_shape=[jax.ShapeDtypeStruct((T, 3 * D), BF), jax.ShapeDtypeStruct((3 * 8, D), F32)],
        compiler_params=_cp("arbitrary", "arbitrary"),
    )(z, z, z, cw, dv)


def _strided(base, count, d):
    return pl.ds(base, count, stride=d) if d > 1 else pl.ds(pl.multiple_of(base, BAND), count)


def _fill_band_bias(bias):
    qi = lax.broadcasted_iota(jnp.int32, (2 * BAND, 2 * BAND), 0) % BAND
    kj = lax.broadcasted_iota(jnp.int32, (2 * BAND, 2 * BAND), 1)
    dist = qi + BAND - kj
    band = (dist >= 0) & (dist <= BAND)
    bias[0] = jnp.where(band & (kj >= BAND), 0.0, NEG_INF)
    bias[1] = jnp.where(band, 0.0, NEG_INF)


def _attn_block_rows(T):
    return min(ATTN_BLOCK_ROWS, T)


def _head_mask():
    lane = lax.broadcasted_iota(jnp.int32, (2 * BAND, LANES), 1)
    row = lax.broadcasted_iota(jnp.int32, (2 * BAND, LANES), 0)
    return (lane < HEAD_DIM) == (row < BAND)


def _attn_fwd(q_all, kv_all, *, name):
    T = q_all.shape[0]
    NB = len(BRANCHES)
    Dm = q_all.shape[1] // NB
    HP = Dm // LANES
    R = _attn_block_rows(T)
    units = R // BAND
    dmax = max(d for _, d in BRANCHES)

    def body(*refs):
        ins = refs[:5 * NB]
        o_ref, l_ref, kbuf, vbuf, o_s, l_s, bias = refs[5 * NB:]
        n = pl.program_id(0)
        _fill_band_bias(bias)
        hm = _head_mask()
        low = lax.broadcasted_iota(jnp.int32, (BAND, LANES), 1) < HEAD_DIM

        for g, (_, d) in enumerate(BRANCHES):
            q_ref, kp_ref, kc_ref, vp_ref, vc_ref = ins[5 * g:5 * g + 5]
            pr = BAND * d
            kbuf[0:pr, :] = kp_ref[...]
            kbuf[pr:pr + R, :] = kc_ref[...]
            vbuf[0:pr, :] = vp_ref[...]
            vbuf[pr:pr + R, :] = vc_ref[...]

            def unit(u, carry, g=g, d=d, pr=pr, q_ref=q_ref):
                sub = u // d
                base = sub * pr + (u - sub * d)
                q = q_ref[_strided(base, BAND, d), :]
                q2 = jnp.where(hm, jnp.concatenate([q, q], axis=0), 0.0).astype(BF)
                k2 = kbuf[_strided(base, 2 * BAND, d), :].astype(BF)
                v2 = vbuf[_strided(base, 2 * BAND, d), :].astype(BF)
                s = lax.dot_general(q2, k2, (((1,), (1,)), ((), ())), preferred_element_type=F32)
                s = s + bias[((n > 0) | (sub > 0)).astype(jnp.int32)]
                m = jnp.max(s, axis=-1, keepdims=True)
                p = jnp.exp(s - m)
                l = jnp.sum(p, axis=-1, keepdims=True)
                pv = jnp.dot(p.astype(BF), v2, preferred_element_type=F32) / l
                lse = m + jnp.log(l)
                o_s[g, _strided(base, BAND, d), :] = jnp.where(low, pv[:BAND], pv[BAND:])
                l_s[g, _strided(base, BAND, d), :] = jnp.where(low, lse[:BAND], lse[BAND:])
                return carry

            lax.fori_loop(0, units, unit, 0, unroll=4)

        def merge(i, carry):
            sl = pl.ds(pl.multiple_of(i * BAND, BAND), BAND)
            lv = [l_s[g, sl, :] for g in range(NB)]
            m = functools.reduce(jnp.maximum, lv)
            e = [jnp.exp(v - m) for v in lv]
            tot = functools.reduce(jnp.add, e)
            o_ref[sl, :] = functools.reduce(jnp.add, [(e[g] / tot) * o_s[g, sl, :] for g in range(NB)])
            l_ref[sl, :] = m + jnp.log(tot)
            return carry

        lax.fori_loop(0, units, merge, 0)

    in_specs, args = [], []
    for g, (_, d) in enumerate(BRANCHES):
        per = R // (BAND * d)
        for col, rows, idx in (
                (g * HP, R, lambda n, hp: n),
                (g * HP, BAND * d, lambda n, hp, per=per: jnp.maximum(n * per - 1, 0)),
                (g * HP, R, lambda n, hp: n),
                ((NB + g) * HP, BAND * d, lambda n, hp, per=per: jnp.maximum(n * per - 1, 0)),
                ((NB + g) * HP, R, lambda n, hp: n)):
            in_specs.append(pl.BlockSpec((rows, LANES), lambda n, hp, col=col, idx=idx: (idx(n, hp), col + hp)))
        args += [q_all, kv_all, kv_all, kv_all, kv_all]
    out = pl.BlockSpec((R, LANES), lambda n, hp: (n, hp))
    return pl.pallas_call(
        body, name=name,
        grid=(T // R, HP),
        in_specs=in_specs,
        out_specs=[out, out],
        out_shape=[jax.ShapeDtypeStruct((T, Dm), F32), jax.ShapeDtypeStruct((T, Dm), F32)],
        scratch_shapes=[pltpu.VMEM((BAND * dmax + R, LANES), F32), pltpu.VMEM((BAND * dmax + R, LANES), F32),
                        pltpu.VMEM((NB, R, LANES), F32), pltpu.VMEM((NB, R, LANES), F32),
                        pltpu.VMEM((2, 2 * BAND, 2 * BAND), F32)],
        compiler_params=_cp("parallel", "parallel"),
    )(*args)


def _attn_bwd(q_all, kv_all, do, o, lse, cos2, ss2, g, d, prev, *, name, dep=None):
    T = q_all.shape[0]
    NB = len(BRANCHES)
    Dm = q_all.shape[1] // NB
    HP = Dm // LANES
    R = _attn_block_rows(T)
    nblk = T // R
    units = R // BAND
    pr = BAND * d
    per = R // pr
    scale = HEAD_DIM ** -0.5

    def rope_bwd(t, cosv, ssv, first):
        return t * cosv - _rot_half(t, first) * ssv

    def body(q_ref, kp_ref, kc_ref, vp_ref, vc_ref, do_ref, o_ref, l_ref, cos_ref, ss_ref, *rest):
        dq_ref, dk_ref, dv_ref, kbuf, vbuf, dkbuf, dvbuf, ck_car, cv_car, bias = rest[-10:]
        i = pl.program_id(1)
        n = nblk - 1 - i
        _fill_band_bias(bias)
        first = _first_half_mask(BAND)
        hm = _head_mask()
        low = lax.broadcasted_iota(jnp.int32, (BAND, LANES), 1) < HEAD_DIM

        kbuf[0:pr, :] = kp_ref[...]
        kbuf[pr:pr + R, :] = kc_ref[...]
        vbuf[0:pr, :] = vp_ref[...]
        vbuf[pr:pr + R, :] = vc_ref[...]
        dkbuf[...] = jnp.zeros_like(dkbuf)
        dvbuf[...] = jnp.zeros_like(dvbuf)

        @pl.when(i > 0)
        def _():
            dkbuf[R:R + pr, :] = ck_car[...]
            dvbuf[R:R + pr, :] = cv_car[...]

        def unit(u, carry):
            sub = u // d
            base = sub * pr + (u - sub * d)
            sl = _strided(base, BAND, d)
            sl2 = _strided(base, 2 * BAND, d)
            q = q_ref[sl, :]
            dov = do_ref[sl, :]
            ov = o_ref[sl, :]
            lv = l_ref[sl, :]
            q2 = jnp.where(hm, jnp.concatenate([q, q], axis=0), 0.0).astype(BF)
            do2 = jnp.where(hm, jnp.concatenate([dov, dov], axis=0), 0.0)
            oo = dov * ov
            delta = jnp.sum(jnp.where(hm, jnp.concatenate([oo, oo], axis=0), 0.0), axis=-1, keepdims=True)
            lse2 = jnp.concatenate([lv[:, 0:1], lv[:, HEAD_DIM:HEAD_DIM + 1]], axis=0)
            do2 = do2.astype(BF)
            k2 = kbuf[sl2, :].astype(BF)
            v2 = vbuf[sl2, :].astype(BF)
            s = lax.dot_general(q2, k2, (((1,), (1,)), ((), ())), preferred_element_type=F32)
            p = jnp.exp(s + bias[((n > 0) | (sub > 0)).astype(jnp.int32)] - lse2)
            dp = lax.dot_general(do2, v2, (((1,), (1,)), ((), ())), preferred_element_type=F32)
            ds = (p * (dp - delta)).astype(BF)
            dq2 = jnp.dot(ds, k2, preferred_element_type=F32)
            dq = jnp.where(low, dq2[:BAND], dq2[BAND:])
            dq_ref[sl, :] = rope_bwd(dq, cos_ref[sl, :], ss_ref[sl, :], first) * scale
            dkbuf[sl2, :] += jnp.dot(ds.T, q2, preferred_element_type=F32)
            dvbuf[sl2, :] += jnp.dot(p.astype(BF).T, do2, preferred_element_type=F32)
            return carry

        lax.fori_loop(0, units, unit, 0, unroll=2)

        dk_ref[...] = rope_bwd(dkbuf[pr:pr + R, :], cos_ref[...], ss_ref[...], _first_half_mask(R))
        dv_ref[...] = dvbuf[pr:pr + R, :]
        ck_car[...] = dkbuf[0:pr, :]
        cv_car[...] = dvbuf[0:pr, :]

    blk = (R, LANES)
    pblk = (pr, LANES)
    cur = lambda hp, i: nblk - 1 - i
    prv = lambda hp, i: jnp.maximum((nblk - 1 - i) * per - 1, 0)
    in_specs = [
        pl.BlockSpec(blk, lambda hp, i: (cur(hp, i), g * HP + hp)),
        pl.BlockSpec(pblk, lambda hp, i: (prv(hp, i), g * HP + hp)),
        pl.BlockSpec(blk, lambda hp, i: (cur(hp, i), g * HP + hp)),
        pl.BlockSpec(pblk, lambda hp, i: (prv(hp, i), (NB + g) * HP + hp)),
        pl.BlockSpec(blk, lambda hp, i: (cur(hp, i), (NB + g) * HP + hp)),
        pl.BlockSpec(blk, lambda hp, i: (cur(hp, i), hp)),
        pl.BlockSpec(blk, lambda hp, i: (cur(hp, i), hp)),
        pl.BlockSpec(blk, lambda hp, i: (cur(hp, i), hp)),
        pl.BlockSpec(blk, lambda hp, i: (cur(hp, i), 0)),
        pl.BlockSpec(blk, lambda hp, i: (cur(hp, i), 0)),
    ]
    args = [q_all, kv_all, kv_all, kv_all, kv_all, do, o, lse, cos2, ss2]
    if dep is not None:
        in_specs.append(ANY)
        args.append(dep)
    aliases = {}
    if prev is not None:
        in_specs += [ANY, ANY, ANY]
        aliases = {len(args): 0, len(args) + 1: 1, len(args) + 2: 2}
        args += list(prev)
    wide = jax.ShapeDtypeStruct((T, NB * Dm), F32)
    out = pl.BlockSpec(blk, lambda hp, i: (cur(hp, i), g * HP + hp))
    return pl.pallas_call(
        body, name=name,
        grid=(HP, nblk),
        in_specs=in_specs,
        out_specs=[out, out, out],
        out_shape=[wide, wide, wide],
        scratch_shapes=[pltpu.VMEM((pr + R, LANES), F32), pltpu.VMEM((pr + R, LANES), F32),
                        pltpu.VMEM((pr + R, LANES), F32), pltpu.VMEM((pr + R, LANES), F32),
                        pltpu.VMEM(pblk, F32), pltpu.VMEM(pblk, F32),
                        pltpu.VMEM((2, 2 * BAND, 2 * BAND), F32)],
        input_output_aliases=aliases,
        compiler_params=_cp("arbitrary", "arbitrary"),
    )(*args)


def _postnorm_bwd(dh, y, g_ref_val):
    r = lax.rsqrt(jnp.mean(y * y, axis=-1, keepdims=True) + RMS_EPS)
    yn = y * r
    dyn = dh * g_ref_val
    dy = r * (dyn - yn * jnp.mean(dyn * yn, axis=-1, keepdims=True))
    return dy, yn


def _after(body, n_in, dep):
    if dep is None:
        return body
    return lambda *refs: body(*refs[:n_in], *refs[n_in + 1:])


def _dep_spec(dep):
    return [] if dep is None else [ANY]


def _dep_arg(dep):
    return [] if dep is None else [dep]


def _postnorm_bwd_matmul(dh, y, gain, w3, widx, *, name, da_dtype, dep=None):
    T, D = dh.shape
    K = w3.shape[1]
    tm = min(ROW_BLOCK, T)

    def body(dh_ref, y_ref, g_ref, w_ref, dy_ref, da_ref, dg_ref):
        i = pl.program_id(0)

        @pl.when(i == 0)
        def _():
            dg_ref[...] = jnp.zeros_like(dg_ref)

        dhv = dh_ref[...]
        dy, yn = _postnorm_bwd(dhv, y_ref[...].astype(F32), g_ref[...])
        dg_ref[...] += jnp.sum(dhv * yn, axis=0, keepdims=True)
        dyb = dy.astype(BF)
        dy_ref[...] = dyb
        da = lax.dot_general(dyb, w_ref[...], (((1,), (1,)), ((), ())), preferred_element_type=F32)
        da_ref[...] = da.astype(da_dtype)

    return pl.pallas_call(
        _after(body, 4, dep), name=name,
        grid=(T // tm,),
        in_specs=[
            pl.BlockSpec((tm, D), lambda i: (i, 0)),
            pl.BlockSpec((tm, D), lambda i: (i, 0)),
            pl.BlockSpec((1, D), lambda i: (0, 0)),
            pl.BlockSpec((None, K, D), lambda i: (widx, 0, 0)),
        ] + _dep_spec(dep),
        out_specs=[pl.BlockSpec((tm, D), lambda i: (i, 0)),
                   pl.BlockSpec((tm, K), lambda i: (i, 0)),
                   pl.BlockSpec((1, D), lambda i: (0, 0))],
        out_shape=[jax.ShapeDtypeStruct((T, D), BF), jax.ShapeDtypeStruct((T, K), da_dtype),
                   jax.ShapeDtypeStruct((1, D), F32)],
        compiler_params=_cp("arbitrary"),
    )(dh, y, gain, w3, *_dep_arg(dep))


def _postnorm_bwd_swiglu(dh, y, gain, wd3, layer, g, u, *, name, dep=None):
    T, D = dh.shape
    F = wd3.shape[1]
    nf = F // 2
    tm = min(ROW_BLOCK, T)

    def body(dh_ref, y_ref, g_ref, w_ref, gg_ref, uu_ref, dy_ref, dgo_ref, duo_ref, dgain_ref, dys):
        i = pl.program_id(0)
        j = pl.program_id(1)

        @pl.when((i == 0) & (j == 0))
        def _():
            dgain_ref[...] = jnp.zeros_like(dgain_ref)

        @pl.when(j == 0)
        def _():
            dhv = dh_ref[...]
            dy, yn = _postnorm_bwd(dhv, y_ref[...].astype(F32), g_ref[...])
            dgain_ref[...] += jnp.sum(dhv * yn, axis=0, keepdims=True)
            dyb = dy.astype(BF)
            dys[...] = dyb
            dy_ref[...] = dyb

        da = lax.dot_general(dys[...], w_ref[...], (((1,), (1,)), ((), ())), preferred_element_type=F32)
        gv = gg_ref[...].astype(F32)
        uv = uu_ref[...].astype(F32)
        sg = jax.nn.sigmoid(gv)
        silu = gv * sg
        dgo_ref[...] = (da * uv * (sg + silu * (1.0 - sg))).astype(BF)
        duo_ref[...] = (da * silu).astype(BF)

    act = jax.ShapeDtypeStruct((T, F), BF)
    return pl.pallas_call(
        _after(body, 6, dep), name=name,
        grid=(T // tm, 2),
        in_specs=[
            pl.BlockSpec((tm, D), lambda i, j: (i, 0)),
            pl.BlockSpec((tm, D), lambda i, j: (i, 0)),
            pl.BlockSpec((1, D), lambda i, j: (0, 0)),
            pl.BlockSpec((None, nf, D), lambda i, j: (layer, j, 0)),
            pl.BlockSpec((tm, nf), lambda i, j: (i, j)),
            pl.BlockSpec((tm, nf), lambda i, j: (i, j)),
        ] + _dep_spec(dep),
        out_specs=[pl.BlockSpec((tm, D), lambda i, j: (i, 0)),
                   pl.BlockSpec((tm, nf), lambda i, j: (i, j)),
                   pl.BlockSpec((tm, nf), lambda i, j: (i, j)),
                   pl.BlockSpec((1, D), lambda i, j: (0, 0))],
        out_shape=[jax.ShapeDtypeStruct((T, D), BF), act, act, jax.ShapeDtypeStruct((1, D), F32)],
        scratch_shapes=[pltpu.VMEM((tm, D), BF)],
        compiler_params=_cp("arbitrary", "arbitrary"),
    )(dh, y, gain, wd3, g, u, *_dep_arg(dep))


def _matmul_prenorm_bwd(dzs, wg, layer, h, gain, dh_in, *, name):
    T, D = h.shape
    n = wg.shape[2]
    tm = min(ROW_BLOCK, T)
    pair = len(dzs) == 2
    nj = N_CHIPS // 2 if pair else N_CHIPS

    def body(*refs):
        dz_refs = refs[:len(dzs)]
        w_refs = refs[len(dzs):2 * len(dzs)]
        h_ref, g_ref, dhi_ref, dh_ref, dg_ref, acc = refs[2 * len(dzs):]
        i = pl.program_id(0)
        j = pl.program_id(1)

        @pl.when((i == 0) & (j == 0))
        def _():
            dg_ref[...] = jnp.zeros_like(dg_ref)

        part = None
        for dz_ref, w_ref in zip(dz_refs, w_refs):
            t = lax.dot_general(dz_ref[...].astype(BF), w_ref[...], (((1,), (1,)), ((), ())),
                                preferred_element_type=F32)
            part = t if part is None else part + t

        @pl.when(j == 0)
        def _():
            acc[...] = part

        @pl.when(j > 0)
        def _():
            acc[...] += part

        @pl.when(j == nj - 1)
        def _():
            dhn = acc[...]
            hv = h_ref[...]
            r = lax.rsqrt(jnp.mean(hv * hv, axis=-1, keepdims=True) + RMS_EPS)
            xh = hv * r
            dg_ref[...] += jnp.sum(dhn * xh, axis=0, keepdims=True)
            dxn = dhn * g_ref[...]
            dh_ref[...] = dhi_ref[...] + r * (dxn - xh * jnp.mean(dxn * xh, axis=-1, keepdims=True))

    in_specs = [pl.BlockSpec((tm, n), lambda i, j: (i, j)) for _ in dzs]
    if pair:
        in_specs += [pl.BlockSpec((None, D, n), lambda i, j: (j, layer, 0)),
                     pl.BlockSpec((None, D, n), lambda i, j: (j + nj, layer, 0))]
    else:
        in_specs += [pl.BlockSpec((None, D, n), lambda i, j: (j, layer, 0))]
    in_specs += [pl.BlockSpec((tm, D), lambda i, j: (i, 0)),
                 pl.BlockSpec((1, D), lambda i, j: (0, 0)),
                 pl.BlockSpec((tm, D), lambda i, j: (i, 0))]
    return pl.pallas_call(
        body, name=name,
        grid=(T // tm, nj),
        in_specs=in_specs,
        out_specs=[pl.BlockSpec((tm, D), lambda i, j: (i, 0)), pl.BlockSpec((1, D), lambda i, j: (0, 0))],
        out_shape=[jax.ShapeDtypeStruct((T, D), F32), jax.ShapeDtypeStruct((1, D), F32)],
        scratch_shapes=[pltpu.VMEM((tm, D), F32)],
        compiler_params=_cp("arbitrary", "arbitrary"),
    )(*dzs, *([wg] * len(dzs)), h, gain, dh_in)


def _grad_matmul(a, b, out_shape3, tme, tne, out_index, prev, *, name):
    T, M = a.shape
    N = b.shape[1]
    tk = min(ROW_BLOCK, T)
    nk = T // tk

    def body(a_ref, b_ref, *rest):
        o_ref, acc = rest[-2:]
        k = pl.program_id(2)
        part = jnp.dot(a_ref[...].astype(BF).T, b_ref[...].astype(BF), preferred_element_type=F32)

        @pl.when(k == 0)
        def _():
            acc[...] = part

        @pl.when(k > 0)
        def _():
            acc[...] += part

        @pl.when(k == nk - 1)
        def _():
            o_ref[...] = acc[...].astype(BF)

    in_specs = [pl.BlockSpec((tk, tme), lambda i, j, k: (k, i)),
                pl.BlockSpec((tk, tne), lambda i, j, k: (k, j))]
    args = [a, b]
    aliases = {}
    if prev is not None:
        in_specs.append(ANY)
        args.append(prev)
        aliases = {2: 0}
    return pl.pallas_call(
        body, name=name,
        grid=(M // tme, N // tne, nk),
        in_specs=in_specs,
        out_specs=pl.BlockSpec((None, tme, tne), lambda i, j, k: out_index(i, j)),
        out_shape=jax.ShapeDtypeStruct(out_shape3, BF),
        scratch_shapes=[pltpu.VMEM((tme, tne), F32)],
        input_output_aliases=aliases,
        compiler_params=_cp("parallel", "parallel", "arbitrary"),
    )(*args)


def _row_tile(R):
    fit = [t for t in range(16, min(R, 512) + 1, 16) if R % t == 0]
    return max(fit) if fit else R


def _cast_place(w2d, where, dtype, *, name):
    R, C = w2d.shape
    tr = _row_tile(R)

    def body(s_ref, w_ref, o_ref):
        o_ref[...] = w_ref[...].astype(o_ref.dtype)

    return pl.pallas_call(
        body, name=name,
        grid_spec=pltpu.PrefetchScalarGridSpec(
            num_scalar_prefetch=1, grid=(R // tr,),
            in_specs=[pl.BlockSpec((tr, C), lambda i, s: (i, 0))],
            out_specs=pl.BlockSpec((None, tr, C), lambda i, s: (s[0], i, 0))),
        out_shape=jax.ShapeDtypeStruct((N_CHIPS, R, C), dtype),
        compiler_params=_cp("arbitrary"),
    )(where, w2d)


def _pair_sum(dw, theirs, where, *, name):
    G, rh, C = theirs.shape
    tr = _row_tile(rh)
    nr = rh // tr

    def body(s_ref, a_ref, b_ref, o_ref):
        o_ref[...] = (a_ref[...].astype(F32) + b_ref[...].astype(F32)).astype(BF)

    mine = pl.BlockSpec((None, tr, C), lambda g, i, s: (g, s[1] * nr + i, 0))
    spec = pl.BlockSpec((None, tr, C), lambda g, i, s: (g, i, 0))
    return pl.pallas_call(
        body, name=name,
        grid_spec=pltpu.PrefetchScalarGridSpec(
            num_scalar_prefetch=1, grid=(G, nr), in_specs=[mine, spec], out_specs=spec),
        out_shape=jax.ShapeDtypeStruct((G, rh, C), BF),
        compiler_params=_cp("arbitrary", "arbitrary"),
    )(where, dw, theirs)


def _chip_sum(landed, parts, where, total_rows, row_off, prev, *, name):
    G, rh, C = landed.shape
    tr = _row_tile(rh)
    nr = rh // tr
    base = row_off // tr

    def body(s_ref, l_ref, p_ref, *rest):
        o_ref = rest[-1]
        for j in range(G):
            def own(j=j):
                v = p_ref[...].astype(F32)
                o_ref[...] = v if j == 0 else o_ref[...] + v

            def other(j=j):
                v = l_ref[j].astype(F32)
                o_ref[...] = v if j == 0 else o_ref[...] + v

            pl.when(s_ref[0] == j)(own)
            pl.when(s_ref[0] != j)(other)

    in_specs = [pl.BlockSpec((G, tr, C), lambda i, s: (0, i, 0)),
                pl.BlockSpec((None, tr, C), lambda i, s: (s[0], i, 0))]
    args = [where, landed, parts]
    aliases = {}
    if prev is not None:
        in_specs.append(ANY)
        args.append(prev)
        aliases = {3: 0}
    return pl.pallas_call(
        body, name=name,
        grid_spec=pltpu.PrefetchScalarGridSpec(
            num_scalar_prefetch=1, grid=(nr,),
            in_specs=in_specs,
            out_specs=pl.BlockSpec((tr, C), lambda i, s: (base + s[1] * nr + i, 0))),
        out_shape=jax.ShapeDtypeStruct((total_rows, C), F32),
        input_output_aliases=aliases,
        compiler_params=_cp("arbitrary"),
    )(*args)


def _adamw(w, g, m, v, *, name):
    R, C = w.shape
    tr = _row_tile(R)

    def body(w_ref, g_ref, m_ref, v_ref, d_ref, mo_ref, vo_ref):
        gv = g_ref[...]
        mn = ADAM_B1 * m_ref[...] + (1.0 - ADAM_B1) * gv
        vn = ADAM_B2 * v_ref[...] + (1.0 - ADAM_B2) * jnp.square(gv)
        m_hat = mn / (1.0 - ADAM_B1 ** ADAM_STEP)
        v_hat = vn / (1.0 - ADAM_B2 ** ADAM_STEP)
        d_ref[...] = -ADAM_LR * (m_hat / (jnp.sqrt(v_hat) + ADAM_EPS) + ADAM_WD * w_ref[...])
        mo_ref[...] = mn
        vo_ref[...] = vn

    spec = pl.BlockSpec((tr, C), lambda i: (i, 0))
    shp = jax.ShapeDtypeStruct((R, C), F32)
    return pl.pallas_call(
        body, name=name, grid=(R // tr,), in_specs=[spec] * 4, out_specs=[spec] * 3,
        out_shape=[shp, shp, shp], compiler_params=_cp("parallel"),
    )(w, g, m, v)


def _place():
    x = lax.axis_index("x")
    y = lax.axis_index("y")
    c = lax.axis_index("c")
    chips = [(1 - x, y), (x, 1 - y), (1 - x, 1 - y)]
    return x, y, c, chips


def _chunk_rows(rows, row_bytes, align):
    if rows <= align:
        return rows
    cands = [r for r in range(align, rows + 1, align) if rows % r == 0]
    fit = [r for r in cands if r * row_bytes <= DMA_CHUNK_BYTES]
    return max(fit) if fit else min(cands)


def _row_align(dtype):
    return 8 * (4 // jnp.dtype(dtype).itemsize)


def _start_chunks(make, rows, rc):
    for r0 in range(0, rows, rc):
        make(r0, rc).start()


def _piece_rows(ref, piece, j, h, r0=0, n=None):
    _, lead, off, rows = piece
    rh = rows // 2
    n = rh if n is None else n
    if lead is not None:
        return ref.at[lead, j, pl.ds(h * rh + r0, n)]
    return ref.at[j, pl.ds(off + h * rh + r0, n)]


def _piece_chunk(arr, piece):
    rh = piece[3] // 2
    return rh, _chunk_rows(rh, arr.shape[-1] * arr.dtype.itemsize, _row_align(arr.dtype))


def _ag_start(arrays, taps, groups, *, name):
    na = len(arrays)
    ng = len(groups)
    n_sem = [3 * len(grp) + (3 if g == 0 else 0) for g, grp in enumerate(groups)]

    def body(*refs):
        ins = refs[:na]
        taps_ref = refs[na]
        sems = refs[na + 1:na + 1 + 2 * ng]
        x, y, c, chips = _place()
        myj = 2 * x + y
        for g, grp in enumerate(groups):
            ssem, rsem = sems[2 * g], sems[2 * g + 1]
            for idx, piece in enumerate(grp):
                ref = ins[piece[0]]
                rh, rc = _piece_chunk(arrays[piece[0]], piece)
                for k, (px, py) in enumerate(chips):
                    def send(r0, n, ref=ref, piece=piece, idx=idx, k=k, px=px, py=py, ssem=ssem, rsem=rsem):
                        part = _piece_rows(ref, piece, myj, c, r0, n)
                        return pltpu.make_async_remote_copy(
                            src_ref=part, dst_ref=part, send_sem=ssem.at[3 * idx + k], recv_sem=rsem.at[3 * idx + k],
                            device_id=(px, py, c), device_id_type=MESH)
                    _start_chunks(send, rh, rc)
            if g == 0:
                for k, (px, py) in enumerate(chips):
                    pltpu.make_async_remote_copy(
                        src_ref=taps_ref.at[myj], dst_ref=taps_ref.at[myj],
                        send_sem=ssem.at[3 * len(grp) + k], recv_sem=rsem.at[3 * len(grp) + k],
                        device_id=(px, py, c), device_id_type=MESH).start()

    sem_shapes = []
    for n in n_sem:
        sem_shapes += [pltpu.SemaphoreType.DMA((n,)), pltpu.SemaphoreType.DMA((n,))]
    bufs = [pltpu.HBM(a.shape, a.dtype) for a in arrays] + [pltpu.HBM(taps.shape, taps.dtype)]
    outs = pl.pallas_call(
        body, name=name,
        out_shape=(*sem_shapes, *bufs),
        in_specs=[HBM] * (na + 1),
        out_specs=(*([SEM] * (2 * ng)), *([HBM] * (na + 1))),
        input_output_aliases={i: 2 * ng + i for i in range(na + 1)},
        compiler_params=pltpu.CompilerParams(has_side_effects=EFFECT),
    )(*[_in_hbm(a) for a in arrays], _in_hbm(taps))
    sems = [(outs[2 * g], outs[2 * g + 1]) for g in range(ng)]
    return sems, list(outs[2 * ng:2 * ng + na]), outs[2 * ng + na]


def _ag_wait(sems, vals, taps, group, after, *, name):
    nv = len(vals)
    extra = ([taps] if taps is not None else [])
    nb = nv + len(extra)

    def body(*refs):
        bufs = refs[:nb]
        ssem, rsem = refs[nb], refs[nb + 1]
        x, y, c, chips = _place()
        for idx, piece in enumerate(group):
            for k, (px, py) in enumerate(chips):
                got = _piece_rows(bufs[piece[0]], piece, 2 * px + py, c)
                cp = pltpu.make_async_remote_copy(
                    src_ref=got, dst_ref=got, send_sem=ssem.at[3 * idx + k], recv_sem=rsem.at[3 * idx + k],
                    device_id=(px, py, c), device_id_type=MESH)
                cp.wait_send()
                cp.wait_recv()
        if taps is not None:
            for k, (px, py) in enumerate(chips):
                got = bufs[nv].at[2 * px + py]
                cp = pltpu.make_async_remote_copy(
                    src_ref=got, dst_ref=got, send_sem=ssem.at[3 * len(group) + k],
                    recv_sem=rsem.at[3 * len(group) + k], device_id=(px, py, c), device_id_type=MESH)
                cp.wait_send()
                cp.wait_recv()

    ops = list(vals) + extra
    shapes = [pltpu.HBM(a.shape, a.dtype) for a in ops]
    outs = pl.pallas_call(
        body, name=name,
        out_shape=tuple(shapes),
        in_specs=[HBM] * nb + [SEM, SEM] + _dep_spec(after),
        out_specs=[HBM] * nb,
        input_output_aliases={i: i for i in range(nb)},
        compiler_params=pltpu.CompilerParams(has_side_effects=EFFECT),
    )(*ops, sems[0], sems[1], *_dep_arg(after))
    return list(outs[:nv]), (outs[nv] if taps is not None else None)


def _ag_forward(vals, group, *, name):
    nv = len(vals)
    npc = len(group)

    def body(*refs):
        bufs = refs[nv:2 * nv]
        fsem, gsem = refs[2 * nv:]
        x, y, c, chips = _place()
        sib = (x, y, 1 - c)
        sent = []
        for idx, piece in enumerate(group):
            rh, rc = _piece_chunk(vals[piece[0]], piece)
            for k, (px, py) in enumerate(chips):
                def fwd(r0, n, piece=piece, idx=idx, k=k, pj=2 * px + py):
                    part = _piece_rows(bufs[piece[0]], piece, pj, c, r0, n)
                    return pltpu.make_async_remote_copy(
                        src_ref=part, dst_ref=part, send_sem=fsem.at[3 * idx + k], recv_sem=gsem.at[3 * idx + k],
                        device_id=sib, device_id_type=MESH)
                _start_chunks(fwd, rh, rc)
                sent.append(fwd(0, rh))
        for idx, piece in enumerate(group):
            for k, (px, py) in enumerate(chips):
                theirs = _piece_rows(bufs[piece[0]], piece, 2 * px + py, 1 - c)
                pltpu.make_async_remote_copy(
                    src_ref=theirs, dst_ref=theirs, send_sem=fsem.at[3 * idx + k], recv_sem=gsem.at[3 * idx + k],
                    device_id=sib, device_id_type=MESH).wait_recv()
        for cp in sent:
            cp.wait_send()

    return pl.pallas_call(
        body, name=name,
        in_specs=[ANY] * nv, out_specs=[ANY] * nv,
        out_shape=[jax.ShapeDtypeStruct(v.shape, v.dtype) for v in vals],
        input_output_aliases={i: i for i in range(nv)},
        scratch_shapes=[pltpu.SemaphoreType.DMA((3 * npc,)), pltpu.SemaphoreType.DMA((3 * npc,))],
    )(*vals)


def _sibling_swap(dws, *, name):
    nm = len(dws)
    shapes = [jax.ShapeDtypeStruct((dw.shape[0], dw.shape[1] // 2, dw.shape[2]), dw.dtype) for dw in dws]

    def body(*refs):
        ins = refs[:nm]
        theirs = refs[nm:2 * nm]
        ssem, rsem = refs[2 * nm:]
        x, y, c, _ = _place()
        sib = (x, y, 1 - c)
        cps = []
        for m in range(nm):
            G, rh, cols = shapes[m].shape
            rc = _chunk_rows(rh, cols * shapes[m].dtype.itemsize, _row_align(shapes[m].dtype))
            for j in range(G):
                _start_chunks(lambda r0, n, m=m, j=j, rh=rh: pltpu.make_async_remote_copy(
                    src_ref=ins[m].at[j, pl.ds((1 - c) * rh + r0, n)],
                    dst_ref=theirs[m].at[j, pl.ds(r0, n)], send_sem=ssem.at[m], recv_sem=rsem.at[m],
                    device_id=sib, device_id_type=MESH), rh, rc)
            cps.append(pltpu.make_async_remote_copy(
                src_ref=ins[m].at[:, pl.ds((1 - c) * rh, rh), :], dst_ref=theirs[m],
                send_sem=ssem.at[m], recv_sem=rsem.at[m], device_id=sib, device_id_type=MESH))
        for cp in cps:
            cp.wait()

    return pl.pallas_call(
        body, name=name,
        in_specs=[ANY] * nm, out_specs=[ANY] * nm, out_shape=shapes,
        scratch_shapes=[pltpu.SemaphoreType.DMA((nm,)), pltpu.SemaphoreType.DMA((nm,))],
    )(*dws)


HBM = pl.BlockSpec(memory_space=pltpu.HBM)
SEM = pl.BlockSpec(memory_space=pltpu.SEMAPHORE)
EFFECT = pltpu.SideEffectType.DATAFLOW_SIDE_EFFECTING


def _in_hbm(a):
    return pltpu.with_memory_space_constraint(a, pltpu.HBM)


def _exchange_start(parts, *, name):
    nm = len(parts)

    def body(*refs):
        ins = refs[:nm]
        lands = refs[nm:2 * nm]
        ssem, rsem = refs[2 * nm:2 * nm + 2]
        token = refs[-1]
        x, y, c, chips = _place()
        myj = 2 * x + y
        for m in range(nm):
            _, rh, cols = parts[m].shape
            rc = _chunk_rows(rh, cols * parts[m].dtype.itemsize, _row_align(parts[m].dtype))
            for k, (px, py) in enumerate(chips):
                _start_chunks(lambda r0, n, m=m, k=k, px=px, py=py: pltpu.make_async_remote_copy(
                    src_ref=ins[m].at[2 * px + py, pl.ds(r0, n)], dst_ref=lands[m].at[myj, pl.ds(r0, n)],
                    send_sem=ssem.at[3 * m + k], recv_sem=rsem.at[3 * m + k],
                    device_id=(px, py, c), device_id_type=MESH), rh, rc)
        token[...] = jnp.zeros_like(token)

    bufs = [pltpu.HBM(p.shape, p.dtype) for p in parts]
    outs = pl.pallas_call(
        body, name=name,
        out_shape=(pltpu.SemaphoreType.DMA((3 * nm,)), pltpu.SemaphoreType.DMA((3 * nm,)), *bufs, *bufs,
                   jax.ShapeDtypeStruct((8, LANES), F32)),
        in_specs=[HBM] * (2 * nm),
        out_specs=(SEM, SEM, *([HBM] * (2 * nm)), pl.BlockSpec(memory_space=pltpu.VMEM)),
        input_output_aliases={i: 2 + i for i in range(2 * nm)},
        compiler_params=pltpu.CompilerParams(has_side_effects=EFFECT),
    )(*[_in_hbm(p) for p in parts], *[_in_hbm(lax.empty(p.shape, p.dtype)) for p in parts])
    return (outs[0], outs[1], list(outs[2:2 + nm]), list(outs[2 + nm:2 + 2 * nm])), outs[-1]


def _exchange_wait(handle, after, *, name):
    ssem_in, rsem_in, parts, lands = handle
    nm = len(parts)

    def body(*refs):
        ins = refs[:nm]
        lnd = refs[nm:2 * nm]
        ssem, rsem = refs[2 * nm:2 * nm + 2]
        x, y, c, chips = _place()
        for m in range(nm):
            for k, (px, py) in enumerate(chips):
                pj = 2 * px + py
                cp = pltpu.make_async_remote_copy(
                    src_ref=ins[m].at[pj], dst_ref=lnd[m].at[pj],
                    send_sem=ssem.at[3 * m + k], recv_sem=rsem.at[3 * m + k],
                    device_id=(px, py, c), device_id_type=MESH)
                cp.wait_send()
                cp.wait_recv()

    bufs = [pltpu.HBM(p.shape, p.dtype) for p in parts]
    outs = pl.pallas_call(
        body, name=name,
        out_shape=(*bufs, *bufs),
        in_specs=[HBM] * (2 * nm) + [SEM, SEM, ANY],
        out_specs=[HBM] * (2 * nm),
        input_output_aliases={i: i for i in range(2 * nm)},
        compiler_params=pltpu.CompilerParams(has_side_effects=EFFECT),
    )(*parts, *lands, ssem_in, rsem_in, after)
    return list(outs[nm:])


def _sibling_join(grads, regions):
    nm = len(grads)
    nr = len(regions)
    shapes = [jax.ShapeDtypeStruct(g.shape, g.dtype) for g in grads]

    def body(*refs):
        outs = refs[nm:2 * nm]
        ssem, rsem = refs[2 * nm:]
        x, y, c, _ = _place()
        sib = (x, y, 1 - c)
        cps = []
        for i, (m, off, rows) in enumerate(regions):
            rh, cols = rows // 2, grads[m].shape[1]
            rc = _chunk_rows(rh, cols * grads[m].dtype.itemsize, _row_align(grads[m].dtype))

            def send(r0, n, i=i, m=m, off=off, rh=rh):
                part = outs[m].at[pl.ds(off + c * rh + r0, n)]
                return pltpu.make_async_remote_copy(
                    src_ref=part, dst_ref=part, send_sem=ssem.at[i], recv_sem=rsem.at[i],
                    device_id=sib, device_id_type=MESH)
            _start_chunks(send, rh, rc)
            cps.append(send(0, rh))
        for i, (m, off, rows) in enumerate(regions):
            rh = rows // 2
            cps[i].wait_send()
            got = outs[m].at[pl.ds(off + (1 - c) * rh, rh)]
            pltpu.make_async_remote_copy(
                src_ref=got, dst_ref=got, send_sem=ssem.at[i], recv_sem=rsem.at[i],
                device_id=sib, device_id_type=MESH).wait_recv()

    return pl.pallas_call(
        body, name="rs_sibling_join",
        in_specs=[ANY] * nm, out_specs=[ANY] * nm, out_shape=shapes,
        input_output_aliases={i: i for i in range(nm)},
        scratch_shapes=[pltpu.SemaphoreType.DMA((nr,)), pltpu.SemaphoreType.DMA((nr,))],
    )(*grads)


def _all_reduce_small(pack):
    R, C = pack.shape

    def body(in_ref, out_ref, slots, ssem, rsem):
        x, y, c, _ = _place()
        me = 4 * x + 2 * y + c
        slots[me] = in_ref[...]
        cps = []
        for k in range(1, N_DEV):
            dx, dy, dc = (k >> 2) & 1, (k >> 1) & 1, k & 1
            peer = (x ^ dx, y ^ dy, c ^ dc)
            cp = pltpu.make_async_remote_copy(
                src_ref=in_ref, dst_ref=slots.at[me], send_sem=ssem.at[k], recv_sem=rsem.at[k],
                device_id=peer, device_id_type=MESH)
            cp.start()
            cps.append(cp)
        for k in range(1, N_DEV):
            dx, dy, dc = (k >> 2) & 1, (k >> 1) & 1, k & 1
            got = slots.at[4 * (x ^ dx) + 2 * (y ^ dy) + (c ^ dc)]
            pltpu.make_async_remote_copy(
                src_ref=got, dst_ref=got, send_sem=ssem.at[k], recv_sem=rsem.at[k],
                device_id=(x ^ dx, y ^ dy, c ^ dc), device_id_type=MESH).wait_recv()
        for cp in cps:
            cp.wait_send()
        acc = slots[0]
        for s in range(1, N_DEV):
            acc = acc + slots[s]
        out_ref[...] = acc

    return pl.pallas_call(
        body, name="ar_small",
        in_specs=[pl.BlockSpec(memory_space=pltpu.VMEM)],
        out_specs=pl.BlockSpec(memory_space=pltpu.VMEM),
        out_shape=jax.ShapeDtypeStruct((R, C), F32),
        scratch_shapes=[pltpu.VMEM((N_DEV, R, C), F32),
                        pltpu.SemaphoreType.DMA((N_DEV,)), pltpu.SemaphoreType.DMA((N_DEV,))],
    )(pack)


def kernel(x, positions, mix_norm_pre, mix_norm_post, ffn_norm_pre, ffn_norm_post, ffn_w_gate_up, ffn_w_down, conv_w_in, conv_w, conv_w_out, kv_norm, w_kv, w_q, w_o, loss_target, m_mix_norm_pre, m_mix_norm_post, m_ffn_norm_pre, m_ffn_norm_post, m_ffn_w_gate_up, m_ffn_w_down, m_conv_w_in, m_conv_w, m_conv_w_out, m_kv_norm, m_w_kv, m_w_q, m_w_o, v_mix_norm_pre, v_mix_norm_post, v_ffn_norm_pre, v_ffn_norm_post, v_ffn_w_gate_up, v_ffn_w_down, v_conv_w_in, v_conv_w, v_conv_w_out, v_kv_norm, v_w_kv, v_w_q, v_w_o):
    T, D = x.shape[1], x.shape[2]
    L = ffn_w_gate_up.shape[0]
    n_gu = ffn_w_gate_up.shape[2]
    f_sh = ffn_w_down.shape[1]
    F = N_CHIPS * f_sh
    x0 = x[0]
    tgt = loss_target[0]

    half = HEAD_DIM // 2
    inv_freq = ROPE_THETA ** (-jnp.arange(half, dtype=F32) / half)
    ang = positions[0].astype(F32)[:, None] * inv_freq
    cosv, sinv = jnp.cos(ang), jnp.sin(ang)
    cos2 = jnp.tile(cosv, (1, LANES // half))
    ss2 = jnp.tile(jnp.concatenate([-sinv, sinv], axis=1), (1, LANES // HEAD_DIM))

    def as2d(a):
        return a.reshape(-1, a.shape[-1])

    big = [ffn_w_gate_up, ffn_w_down, conv_w_in, conv_w_out, w_kv, w_q, w_o]
    big_m = [m_ffn_w_gate_up, m_ffn_w_down, m_conv_w_in, m_conv_w_out, m_w_kv, m_w_q, m_w_o]
    big_v = [v_ffn_w_gate_up, v_ffn_w_down, v_conv_w_in, v_conv_w_out, v_w_kv, v_w_q, v_w_o]
    chip = 2 * lax.axis_index("x") + lax.axis_index("y")
    where = jnp.stack([chip, lax.axis_index("c")]).astype(jnp.int32)
    tc = conv_w.shape[2]
    cw_pad = jnp.concatenate([conv_w[0], jnp.zeros((8 - conv_w.shape[1], tc), F32)], axis=0)

    GU0, GU1, WD0, WD1, WCI, WCO, WKV, WQ, WO = range(9)
    shards = [ffn_w_gate_up[0], ffn_w_gate_up[1], ffn_w_down[0], ffn_w_down[1], conv_w_in[0], conv_w_out[0],
              w_kv, w_q[0], w_o[0]]
    placed = [_cast_place(w, where, BF, name=f"place{i}") for i, w in enumerate(shards)]
    ag_groups = [
        [(WCI, None, 0, D), (WCO, None, 0, D // N_CHIPS)],
        [(GU0, None, 0, D), (WD0, None, 0, f_sh)],
        [(WKV, None, 0, D), (WQ, None, 0, D)],
        [(WO, None, 0, D // N_CHIPS), (GU1, None, 0, D), (WD1, None, 0, f_sh)],
    ]
    ag_sems, cur, taps = _ag_start(placed, _cast_place(cw_pad, where, F32, name="place_taps"),
                                   ag_groups, name="ag_start")

    def gather_group(g, after):
        nonlocal taps
        idxs = sorted({p[0] for p in ag_groups[g]})
        local = [(idxs.index(p[0]),) + p[1:] for p in ag_groups[g]]
        vals, landed_taps = _ag_wait(ag_sems[g], [cur[i] for i in idxs], taps if g == 0 else None, local, after,
                                     name=f"ag_wait{g}")
        if g == 0:
            taps = landed_taps
        vals = _ag_forward(vals, local, name=f"ag_forward{g}")
        for i, v in zip(idxs, vals):
            cur[i] = v

    def row(a, i):
        return a[i:i + 1]

    gather_group(0, None)
    wci, wco, cw = cur[WCI], cur[WCO].reshape(1, D, D), taps
    z, hn_m0 = _norm_matmul(x0, row(mix_norm_pre, 0), wci, cos2, ss2, name="f0_conv_in",
                            rope_shards=0, scale=1.0, out_dtype=BF)
    vmix = _conv_fwd(z, cw, name="f0_conv")
    y0, h1 = _matmul_postnorm(vmix, wco, 0, row(mix_norm_post, 0), x0, name="f0_conv_out")
    gather_group(1, h1)
    wgu0, wd0 = cur[GU0], cur[WD0].reshape(1, F, D)
    g0, u0, a0, hn_f0 = _norm_swiglu(h1, row(ffn_norm_pre, 0), wgu0, 0, name="f0_gate_up")
    f0, h2 = _matmul_postnorm(a0, wd0, 0, row(ffn_norm_post, 0), h1, name="f0_down")

    gather_group(2, h2)
    wkv, wq = cur[WKV], cur[WQ]
    kv_all, hn_kv = _norm_matmul(h2, kv_norm.reshape(1, D), wkv, cos2, ss2, name="f1_kv",
                                 rope_shards=N_CHIPS // 2, scale=1.0, out_dtype=F32)
    q_all, hn_m1 = _norm_matmul(h2, row(mix_norm_pre, 1), wq, cos2, ss2, name="f1_q",
                                rope_shards=N_CHIPS, scale=HEAD_DIM ** -0.5, out_dtype=F32)
    o_att, lse = _attn_fwd(q_all, kv_all, name="f1_attn")
    gather_group(3, o_att)
    wgu1, wd1, wo = cur[GU1], cur[WD1].reshape(1, F, D), cur[WO].reshape(1, D, D)
    y1, h3 = _matmul_postnorm(o_att, wo, 0, row(mix_norm_post, 1), h2, name="f1_attn_out")
    g1, u1, a1, hn_f1 = _norm_swiglu(h3, row(ffn_norm_pre, 1), wgu1, 0, name="f1_gate_up")
    f1, h4 = _matmul_postnorm(a1, wd1, 0, row(ffn_norm_post, 1), h3, name="f1_down")

    dh4, sq = _loss_head(h4, tgt, name="loss_head")
    loss_part = 0.5 * sq[0, 0] / D

    gu_shape = (N_CHIPS, D, n_gu)
    in_chips = lambda a: a.reshape(N_CHIPS, -1, a.shape[-1])

    def scatter_start(dws, tag):
        theirs = _sibling_swap(dws, name=f"rs_swap_{tag}")
        parts = [_pair_sum(dw, t, where, name=f"rs_pair_sum_{tag}{i}") for i, (dw, t) in enumerate(zip(dws, theirs))]
        handle, token = _exchange_start(parts, name=f"rs_exchange_start_{tag}")
        return (handle, parts), token

    dyf1, dg1, du1, d_ffn_post1 = _postnorm_bwd_swiglu(dh4, f1, row(ffn_norm_post, 1), wd1, 0, g1, u1,
                                                       name="b1_down")
    dwd1 = _grad_matmul(a1, dyf1, (2, F // 2, D), F // 2, D, lambda i, j: (i, 0, 0), None, name="b1_dw_down")
    dwgu1 = _grad_matmul(hn_f1, dg1, gu_shape, D, n_gu, lambda i, j: (j, 0, 0), None, name="b1_dw_gate")
    dwgu1 = _grad_matmul(hn_f1, du1, gu_shape, D, n_gu, lambda i, j: (j + 2, 0, 0), dwgu1, name="b1_dw_up")
    dh3, d_ffn_pre1 = _matmul_prenorm_bwd((dg1, du1), wgu1, 0, h3, row(ffn_norm_pre, 1), dh4, name="b1_gate_up")

    dy1, do, d_mix_post1 = _postnorm_bwd_matmul(dh3, y1, row(mix_norm_post, 1), wo, 0, name="b1_attn_out",
                                                da_dtype=F32)
    dwo = _grad_matmul(o_att, dy1, (1, D, D), D, D, lambda i, j: (0, 0, 0), None, name="b1_dw_o")
    rs_a, token = scatter_start([dwgu1, in_chips(dwd1), in_chips(dwo)], "a")
    prev = None
    for gi, (window, dil) in enumerate(BRANCHES):
        prev = _attn_bwd(q_all, kv_all, do, o_att, lse, cos2, ss2, gi, dil, prev, name=f"b1_attn{gi}",
                         dep=token if gi == 0 else None)
    dq_all, dk_all, dv_all = prev
    n_q = wq.shape[2]
    n_kv = wkv.shape[2]
    dwq = _grad_matmul(hn_m1, dq_all, (N_CHIPS, D, n_q), D, n_q, lambda i, j: (j, 0, 0), None, name="b1_dw_q")
    dwkv = _grad_matmul(hn_kv, dk_all, (N_CHIPS, D, n_kv), D, n_kv, lambda i, j: (j, 0, 0), None, name="b1_dw_k")
    dwkv = _grad_matmul(hn_kv, dv_all, (N_CHIPS, D, n_kv), D, n_kv, lambda i, j: (j + 2, 0, 0), dwkv, name="b1_dw_v")
    dh2, d_mix_pre1 = _matmul_prenorm_bwd((dq_all,), wq, 0, h2, row(mix_norm_pre, 1), dh3, name="b1_q")
    dh2, d_kv_norm = _matmul_prenorm_bwd((dk_all, dv_all), wkv, 0, h2, kv_norm.reshape(1, D), dh2, name="b1_kv")
    rs_b, token = scatter_start([dwkv, dwq], "b")

    dyf0, dg0, du0, d_ffn_post0 = _postnorm_bwd_swiglu(dh2, f0, row(ffn_norm_post, 0), wd0, 0, g0, u0,
                                                       name="b0_down", dep=token)
    dwd0 = _grad_matmul(a0, dyf0, (2, F // 2, D), F // 2, D, lambda i, j: (i, 0, 0), None, name="b0_dw_down")
    dwgu0 = _grad_matmul(hn_f0, dg0, gu_shape, D, n_gu, lambda i, j: (j, 0, 0), None, name="b0_dw_gate")
    dwgu0 = _grad_matmul(hn_f0, du0, gu_shape, D, n_gu, lambda i, j: (j + 2, 0, 0), dwgu0, name="b0_dw_up")
    dh1, d_ffn_pre0 = _matmul_prenorm_bwd((dg0, du0), wgu0, 0, h1, row(ffn_norm_pre, 0), dh2, name="b0_gate_up")
    rs_c, token = scatter_start([dwgu0, in_chips(dwd0)], "c")

    dy0, dvmix, d_mix_post0 = _postnorm_bwd_matmul(dh1, y0, row(mix_norm_post, 0), wco, 0, name="b0_conv_out",
                                                   da_dtype=BF, dep=token)
    dwco = _grad_matmul(vmix, dy0, (1, D, D), D, D, lambda i, j: (0, 0, 0), None, name="b0_dw_conv_out")
    dz, dcw = _conv_bwd(z, cw, dvmix, name="b0_conv")
    n_ci = wci.shape[2]
    dwci = _grad_matmul(hn_m0, dz, (N_CHIPS, D, n_ci), D, n_ci, lambda i, j: (j, 0, 0), None, name="b0_dw_conv_in")
    dx, d_mix_pre0 = _matmul_prenorm_bwd((dz,), wci, 0, x0, row(mix_norm_pre, 0), dh1, name="b0_conv_in")

    pack = jnp.concatenate([
        d_mix_pre0, d_mix_pre1, d_mix_post0, d_mix_post1, d_ffn_pre0, d_ffn_pre1, d_ffn_post0, d_ffn_post1,
        d_kv_norm, dcw[0:3], jnp.full((1, D), loss_part, F32),
        jnp.zeros((SMALL_ROWS - 13, D), F32)], axis=0)
    red = _all_reduce_small(pack)
    loss = red[12, 0]
    myj = 2 * lax.axis_index("x") + lax.axis_index("y")
    g_conv_w = lax.dynamic_slice(red, (9, myj * tc), (3, tc))

    zeros7 = jnp.zeros((SMALL_ROWS - 9, D), F32)
    w_small = jnp.concatenate([mix_norm_pre, mix_norm_post, ffn_norm_pre, ffn_norm_post, kv_norm.reshape(1, D), zeros7], axis=0)
    m_small = jnp.concatenate([m_mix_norm_pre, m_mix_norm_post, m_ffn_norm_pre, m_ffn_norm_post, m_kv_norm.reshape(1, D), zeros7], axis=0)
    v_small = jnp.concatenate([v_mix_norm_pre, v_mix_norm_post, v_ffn_norm_pre, v_ffn_norm_post, v_kv_norm.reshape(1, D), zeros7], axis=0)
    d_small, nm_small, nv_small = _adamw(w_small, red, m_small, v_small, name="adamw_small")

    pad5 = jnp.zeros((5, tc), F32)
    d_cw, nm_cw, nv_cw = _adamw(cw_pad, jnp.concatenate([g_conv_w, pad5], axis=0),
                                jnp.concatenate([m_conv_w[0], pad5], axis=0),
                                jnp.concatenate([v_conv_w[0], pad5], axis=0), name="adamw_conv_w")

    rs_d, _ = scatter_start([dwci, in_chips(dwco)], "d")

    pieces = {"a": [(0, D), (1, f_sh), (6, 0)], "b": [(4, 0), (5, 0)], "c": [(0, 0), (1, 0)], "d": [(2, 0), (3, 0)]}
    grads2d = [None] * len(big)
    regions = []
    for tag, (handle, parts) in (("a", rs_a), ("b", rs_b), ("c", rs_c), ("d", rs_d)):
        landed = _exchange_wait(handle, dx, name=f"rs_exchange_wait_{tag}")
        for i, (l, p, (wi, off)) in enumerate(zip(landed, parts, pieces[tag])):
            total = as2d(big[wi]).shape[0]
            grads2d[wi] = _chip_sum(l, p, where, total, off, grads2d[wi], name=f"rs_chip_sum_{tag}{i}")
            regions.append((wi, off, 2 * l.shape[1]))
    grads2d = _sibling_join(grads2d, regions)

    big_out = []
    for i, (w, gr, m, v) in enumerate(zip(big, grads2d, big_m, big_v)):
        d_, m_, v_ = _adamw(as2d(w), gr, as2d(m), as2d(v), name=f"adamw{i}")
        big_out.append((gr.reshape(w.shape), d_.reshape(w.shape), m_.reshape(w.shape), v_.reshape(w.shape)))

    def small(a):
        return (a[0:2], a[2:4], a[4:6], a[6:8])

    def assemble(sm, cwv, kind):
        pre, post, fpre, fpost = small(sm)
        b = [t[kind] for t in big_out]
        return [pre, post, fpre, fpost, b[0], b[1], b[2], cwv[0:3].reshape(conv_w.shape), b[3],
                sm[8], b[4], b[5].reshape(w_q.shape), b[6].reshape(w_o.shape)]

    grads = assemble(red, jnp.concatenate([g_conv_w, pad5], axis=0), 0)
    deltas = assemble(d_small, d_cw, 1)
    new_m = assemble(nm_small, nm_cw, 2)
    new_v = assemble(nv_small, nv_cw, 3)
    return (loss, dx.reshape(x.shape), *grads, *deltas, *new_m, *new_v)
```

```python
import functools

import jax
import jax.numpy as jnp
from jax import lax
from jax.experimental import pallas as pl
from jax.experimental.pallas import tpu as pltpu

HEAD_DIM = 64
BAND = 128
BRANCHES = ((128, 1), (512, 4), (2048, 16))
ROPE_THETA = 10000.0
RMS_EPS = 1e-6
NEG_INF = -1e30
ADAM_LR = 0.001
ADAM_B1 = 0.9
ADAM_B2 = 0.999
ADAM_EPS = 1e-08
ADAM_WD = 0.01
ADAM_STEP = 10

N_CHIPS = 4
N_DEV = 8
LANES = 128
ROW_BLOCK = 512
ROW_BLOCK_WIDE = 1024
ATTN_BLOCK_ROWS = 2048
VMEM_LIMIT = 56 * 1024 * 1024
SMALL_ROWS = 16
DMA_CHUNK_BYTES = 512 * 1024

BF = jnp.bfloat16
F32 = jnp.float32
MESH = pl.DeviceIdType.MESH
ANY = pl.BlockSpec(memory_space=pl.ANY)


def _cp(*sem):
    return pltpu.CompilerParams(dimension_semantics=sem, vmem_limit_bytes=VMEM_LIMIT)


def _rot_half(t, first):
    return jnp.where(first, pltpu.roll(t, 96, 1), pltpu.roll(t, 32, 1))


def _first_half_mask(rows):
    lane = lax.broadcasted_iota(jnp.int32, (rows, LANES), 1)
    return (lane % HEAD_DIM) < (HEAD_DIM // 2)


def _normed_rows(j, rows, x_ref, g_ref, xn_ref, xs, last_start, tm):
    @pl.when(j == 0)
    def _():
        xv = x_ref[...]
        r = lax.rsqrt(jnp.mean(xv * xv, axis=-1, keepdims=True) + RMS_EPS)
        xn = (xv * r * g_ref[...]).astype(BF)
        xs[rows, :] = xn
        xn_ref[...] = xn

    @pl.when(j > 0)
    def _():
        xn_ref[...] = xs[pl.ds(last_start, tm), :]


def _norm_matmul(x, gain, wg, cos2, ss2, *, name, rope_shards, scale, out_dtype):
    T, D = x.shape
    n = wg.shape[2]
    tm = min(ROW_BLOCK_WIDE, T)
    ni = T // tm

    def body(x_ref, g_ref, w_ref, cos_ref, ss_ref, y_ref, xn_ref, xs):
        j = pl.program_id(0)
        rows = pl.ds(pl.multiple_of(pl.program_id(1) * tm, tm), tm)
        _normed_rows(j, rows, x_ref, g_ref, xn_ref, xs, (ni - 1) * tm, tm)
        acc = jnp.dot(xs[rows, :], w_ref[...], preferred_element_type=F32)

        def plain():
            y_ref[...] = acc.astype(out_dtype)

        def rope():
            cosv = cos_ref[...]
            ssv = ss_ref[...]
            first = _first_half_mask(tm)
            for ci in range(n // LANES):
                t = acc[:, ci * LANES:(ci + 1) * LANES]
                y = (t * cosv + _rot_half(t, first) * ssv) * scale
                y_ref[:, ci * LANES:(ci + 1) * LANES] = y.astype(out_dtype)

        if rope_shards == 0:
            plain()
        elif rope_shards == N_CHIPS:
            rope()
        else:
            pl.when(j < rope_shards)(rope)
            pl.when(j >= rope_shards)(plain)

    first_pass = lambda j, i: (jnp.where(j == 0, i, ni - 1), 0)
    return pl.pallas_call(
        body, name=name,
        grid=(N_CHIPS, ni),
        in_specs=[
            pl.BlockSpec((tm, D), first_pass),
            pl.BlockSpec((1, D), lambda j, i: (0, 0)),
            pl.BlockSpec((None, D, n), lambda j, i: (j, 0, 0)),
            pl.BlockSpec((tm, LANES), lambda j, i: (i, 0)),
            pl.BlockSpec((tm, LANES), lambda j, i: (i, 0)),
        ],
        out_specs=[
            pl.BlockSpec((tm, n), lambda j, i: (i, j)),
            pl.BlockSpec((tm, D), first_pass),
        ],
        out_shape=[jax.ShapeDtypeStruct((T, N_CHIPS * n), out_dtype),
                   jax.ShapeDtypeStruct((T, D), BF)],
        scratch_shapes=[pltpu.VMEM((T, D), BF)],
        compiler_params=_cp("arbitrary", "arbitrary"),
    )(x, gain, wg, cos2, ss2)


def _norm_swiglu(x, gain, wg, layer, *, name):
    T, D = x.shape
    n = wg.shape[2]
    tm = min(ROW_BLOCK, T)

    ni = T // tm

    def body(x_ref, g_ref, wg_ref, wu_ref, go_ref, uo_ref, ao_ref, xn_ref, xs):
        j = pl.program_id(0)
        rows = pl.ds(pl.multiple_of(pl.program_id(1) * tm, tm), tm)
        _normed_rows(j, rows, x_ref, g_ref, xn_ref, xs, (ni - 1) * tm, tm)
        g = jnp.dot(xs[rows, :], wg_ref[...], preferred_element_type=F32)
        u = jnp.dot(xs[rows, :], wu_ref[...], preferred_element_type=F32)
        go_ref[...] = g.astype(BF)
        uo_ref[...] = u.astype(BF)
        ao_ref[...] = (g * jax.nn.sigmoid(g) * u).astype(BF)

    half = N_CHIPS // 2
    first_pass = lambda j, i: (jnp.where(j == 0, i, ni - 1), 0)
    act = jax.ShapeDtypeStruct((T, half * n), BF)
    return pl.pallas_call(
        body, name=name,
        grid=(half, ni),
        in_specs=[
            pl.BlockSpec((tm, D), first_pass),
            pl.BlockSpec((1, D), lambda j, i: (0, 0)),
            pl.BlockSpec((None, D, n), lambda j, i: (j, layer, 0)),
            pl.BlockSpec((None, D, n), lambda j, i: (j + half, layer, 0)),
        ],
        out_specs=[
            pl.BlockSpec((tm, n), lambda j, i: (i, j)),
            pl.BlockSpec((tm, n), lambda j, i: (i, j)),
            pl.BlockSpec((tm, n), lambda j, i: (i, j)),
            pl.BlockSpec((tm, D), first_pass),
        ],
        out_shape=[act, act, act, jax.ShapeDtypeStruct((T, D), BF)],
        scratch_shapes=[pltpu.VMEM((T, D), BF)],
        compiler_params=_cp("arbitrary", "arbitrary"),
    )(x, gain, wg, wg)


def _matmul_postnorm(a, w3, widx, gain, h_old, *, name):
    T, K = a.shape
    D = w3.shape[2]
    tm = min(ROW_BLOCK_WIDE, T)

    def body(a_ref, w_ref, g_ref, h_ref, y_ref, hn_ref):
        y = jnp.dot(a_ref[...].astype(BF), w_ref[...], preferred_element_type=F32)
        y_ref[...] = y.astype(BF)
        r = lax.rsqrt(jnp.mean(y * y, axis=-1, keepdims=True) + RMS_EPS)
        hn_ref[...] = h_ref[...] + y * r * g_ref[...]

    return pl.pallas_call(
        body, name=name,
        grid=(T // tm,),
        in_specs=[
            pl.BlockSpec((tm, K), lambda i: (i, 0)),
            pl.BlockSpec((None, K, D), lambda i: (widx, 0, 0)),
            pl.BlockSpec((1, D), lambda i: (0, 0)),
            pl.BlockSpec((tm, D), lambda i: (i, 0)),
        ],
        out_specs=[pl.BlockSpec((tm, D), lambda i: (i, 0)),
                   pl.BlockSpec((tm, D), lambda i: (i, 0))],
        out_shape=[jax.ShapeDtypeStruct((T, D), BF), jax.ShapeDtypeStruct((T, D), F32)],
        compiler_params=_cp("parallel"),
    )(a, w3, gain, h_old)


def _loss_head(h, target, *, name):
    T, D = h.shape
    tm = min(ROW_BLOCK, T)

    def body(h_ref, t_ref, dh_ref, s_ref):
        i = pl.program_id(0)

        @pl.when(i == 0)
        def _():
            s_ref[...] = jnp.zeros_like(s_ref)

        e = h_ref[...] - t_ref[...]
        dh_ref[...] = e * (1.0 / D)
        s_ref[...] += jnp.sum(e * e)

    return pl.pallas_call(
        body, name=name,
        grid=(T // tm,),
        in_specs=[pl.BlockSpec((tm, D), lambda i: (i, 0)), pl.BlockSpec((tm, D), lambda i: (i, 0))],
        out_specs=[pl.BlockSpec((tm, D), lambda i: (i, 0)), pl.BlockSpec((8, LANES), lambda i: (0, 0))],
        out_shape=[jax.ShapeDtypeStruct((T, D), F32), jax.ShapeDtypeStruct((8, LANES), F32)],
        compiler_params=_cp("arbitrary"),
    )(h, target)


def _shift_down(u, k):
    row = lax.broadcasted_iota(jnp.int32, u.shape, 0)
    return jnp.where(row >= k, pltpu.roll(u, k, 0), 0.0)


def _shift_up(u, k):
    T = u.shape[0]
    row = lax.broadcasted_iota(jnp.int32, u.shape, 0)
    return jnp.where(row < T - k, pltpu.roll(u, T - k, 0), 0.0)


def _conv_fwd(z, cw, *, name):
    T = z.shape[0]
    D = z.shape[1] // 3
    tc = cw.shape[2]
    nb = D // tc

    def body(b_ref, c_ref, h_ref, w_ref, o_ref):
        u = c_ref[...].astype(F32) * h_ref[...].astype(F32)
        w = w_ref[...]
        conv = w[2:3] * u + w[1:2] * _shift_down(u, 1) + w[0:1] * _shift_down(u, 2)
        o_ref[...] = (b_ref[...].astype(F32) * conv).astype(BF)

    return pl.pallas_call(
        body, name=name,
        grid=(nb,),
        in_specs=[
            pl.BlockSpec((T, tc), lambda j: (0, j)),
            pl.BlockSpec((T, tc), lambda j: (0, nb + j)),
            pl.BlockSpec((T, tc), lambda j: (0, 2 * nb + j)),
            pl.BlockSpec((None, 8, tc), lambda j: (j, 0, 0)),
        ],
        out_specs=pl.BlockSpec((T, tc), lambda j: (0, j)),
        out_shape=jax.ShapeDtypeStruct((T, D), BF),
        compiler_params=_cp("parallel"),
    )(z, z, z, cw)


def _conv_bwd(z, cw, dv, *, name):
    T = z.shape[0]
    D = z.shape[1] // 3
    tc = LANES
    nb = D // tc
    per = cw.shape[2] // tc

    def body(b_ref, c_ref, h_ref, w_ref, dv_ref, dz_ref, dw_ref, stage, sems):
        j = pl.program_id(0)
        slot = j % 2

        def slab(p, jj, s):
            col = pl.multiple_of((p * nb + jj) * tc, tc)
            return pltpu.make_async_copy(stage.at[s, p], dz_ref.at[:, pl.ds(col, tc)], sems.at[s, p])

        @pl.when(j >= 2)
        def _():
            for p in range(3):
                slab(p, j - 2, slot).wait()

        c = c_ref[...].astype(F32)
        h = h_ref[...].astype(F32)
        u = c * h
        u1 = _shift_down(u, 1)
        u2 = _shift_down(u, 2)
        w = w_ref[...]
        dvv = dv_ref[...].astype(F32)
        dconv = dvv * b_ref[...].astype(F32)
        du = w[2:3] * dconv + w[1:2] * _shift_up(dconv, 1) + w[0:1] * _shift_up(dconv, 2)
        rows = lax.broadcasted_iota(jnp.int32, (8, tc), 0)
        dw_ref[...] = jnp.where(rows == 0, jnp.sum(dconv * u2, axis=0, keepdims=True),
                                jnp.where(rows == 1, jnp.sum(dconv * u1, axis=0, keepdims=True),
                                          jnp.where(rows == 2, jnp.sum(dconv * u, axis=0, keepdims=True), 0.0)))
        stage[slot, 0] = (dvv * (w[2:3] * u + w[1:2] * u1 + w[0:1] * u2)).astype(BF)
        stage[slot, 1] = (du * h).astype(BF)
        stage[slot, 2] = (du * c).astype(BF)
        for p in range(3):
            slab(p, j, slot).start()

        @pl.when(j == nb - 1)
        def _():
            for p in range(3):
                slab(p, j, slot).wait()
            if nb > 1:
                for p in range(3):
                    slab(p, j - 1, 1 - slot).wait()

    return pl.pallas_call(
        body, name=name,
        grid=(nb,),
        in_specs=[
            pl.BlockSpec((T, tc), lambda j: (0, j)),
            pl.BlockSpec((T, tc), lambda j: (0, nb + j)),
            pl.BlockSpec((T, tc), lambda j: (0, 2 * nb + j)),
            pl.BlockSpec((None, 8, tc), lambda j: (j // per, 0, j % per)),
            pl.BlockSpec((T, tc), lambda j: (0, j)),
        ],
        out_specs=[ANY, pl.BlockSpec((8, tc), lambda j: (0, j))],
        out_shape=[jax.ShapeDtypeStruct((T, 3 * D), BF), jax.ShapeDtypeStruct((8, D), F32)],
        scratch_shapes=[pltpu.VMEM((2, 3, T, tc), BF), pltpu.SemaphoreType.DMA((2, 3))],
        compiler_params=_cp("arbitrary"),
    )(z, z, z, cw, dv)


def _strided(base, count, d):
    return pl.ds(base, count, stride=d) if d > 1 else pl.ds(pl.multiple_of(base, BAND), count)


def _fill_band_bias(bias):
    qi = lax.broadcasted_iota(jnp.int32, (2 * BAND, 2 * BAND), 0) % BAND
    kj = lax.broadcasted_iota(jnp.int32, (2 * BAND, 2 * BAND), 1)
    dist = qi + BAND - kj
    band = (dist >= 0) & (dist <= BAND)
    bias[0] = jnp.where(band & (kj >= BAND), 0.0, NEG_INF)
    bias[1] = jnp.where(band, 0.0, NEG_INF)


def _attn_block_rows(T):
    return min(ATTN_BLOCK_ROWS, T)


def _head_mask():
    lane = lax.broadcasted_iota(jnp.int32, (2 * BAND, LANES), 1)
    row = lax.broadcasted_iota(jnp.int32, (2 * BAND, LANES), 0)
    return (lane < HEAD_DIM) == (row < BAND)


def _attn_fwd(q_all, kv_all, *, name):
    T = q_all.shape[0]
    NB = len(BRANCHES)
    Dm = q_all.shape[1] // NB
    HP = Dm // LANES
    R = _attn_block_rows(T)
    units = R // BAND
    dmax = max(d for _, d in BRANCHES)

    def body(*refs):
        ins = refs[:5 * NB]
        o_ref, l_ref, kbuf, vbuf, o_s, l_s, bias = refs[5 * NB:]
        n = pl.program_id(0)
        _fill_band_bias(bias)
        hm = _head_mask()
        low = lax.broadcasted_iota(jnp.int32, (BAND, LANES), 1) < HEAD_DIM

        for g, (_, d) in enumerate(BRANCHES):
            q_ref, kp_ref, kc_ref, vp_ref, vc_ref = ins[5 * g:5 * g + 5]
            pr = BAND * d
            kbuf[0:pr, :] = kp_ref[...]
            kbuf[pr:pr + R, :] = kc_ref[...]
            vbuf[0:pr, :] = vp_ref[...]
            vbuf[pr:pr + R, :] = vc_ref[...]

            def unit(u, carry, g=g, d=d, pr=pr, q_ref=q_ref):
                sub = u // d
                base = sub * pr + (u - sub * d)
                q = q_ref[_strided(base, BAND, d), :]
                q2 = jnp.where(hm, jnp.concatenate([q, q], axis=0), 0.0).astype(BF)
                k2 = kbuf[_strided(base, 2 * BAND, d), :].astype(BF)
                v2 = vbuf[_strided(base, 2 * BAND, d), :].astype(BF)
                s = lax.dot_general(q2, k2, (((1,), (1,)), ((), ())), preferred_element_type=F32)
                s = s + bias[((n > 0) | (sub > 0)).astype(jnp.int32)]
                m = jnp.max(s, axis=-1, keepdims=True)
                p = jnp.exp(s - m)
                l = jnp.sum(p, axis=-1, keepdims=True)
                pv = jnp.dot(p.astype(BF), v2, preferred_element_type=F32) / l
                lse = m + jnp.log(l)
                o_s[g, _strided(base, BAND, d), :] = jnp.where(low, pv[:BAND], pv[BAND:])
                l_s[g, _strided(base, BAND, d), :] = jnp.where(low, lse[:BAND], lse[BAND:])
                return carry

            lax.fori_loop(0, units, unit, 0, unroll=4)

        def merge(i, carry):
            sl = pl.ds(pl.multiple_of(i * BAND, BAND), BAND)
            lv = [l_s[g, sl, :] for g in range(NB)]
            m = functools.reduce(jnp.maximum, lv)
            e = [jnp.exp(v - m) for v in lv]
            tot = functools.reduce(jnp.add, e)
            o_ref[sl, :] = functools.reduce(jnp.add, [(e[g] / tot) * o_s[g, sl, :] for g in range(NB)])
            l_ref[sl, :] = m + jnp.log(tot)
            return carry

        lax.fori_loop(0, units, merge, 0)

    in_specs, args = [], []
    for g, (_, d) in enumerate(BRANCHES):
        per = R // (BAND * d)
        for col, rows, idx in (
                (g * HP, R, lambda n, hp: n),
                (g * HP, BAND * d, lambda n, hp, per=per: jnp.maximum(n * per - 1, 0)),
                (g * HP, R, lambda n, hp: n),
                ((NB + g) * HP, BAND * d, lambda n, hp, per=per: jnp.maximum(n * per - 1, 0)),
                ((NB + g) * HP, R, lambda n, hp: n)):
            in_specs.append(pl.BlockSpec((rows, LANES), lambda n, hp, col=col, idx=idx: (idx(n, hp), col + hp)))
        args += [q_all, kv_all, kv_all, kv_all, kv_all]
    out = pl.BlockSpec((R, LANES), lambda n, hp: (n, hp))
    return pl.pallas_call(
        body, name=name,
        grid=(T // R, HP),
        in_specs=in_specs,
        out_specs=[out, out],
        out_shape=[jax.ShapeDtypeStruct((T, Dm), F32), jax.ShapeDtypeStruct((T, Dm), F32)],
        scratch_shapes=[pltpu.VMEM((BAND * dmax + R, LANES), F32), pltpu.VMEM((BAND * dmax + R, LANES), F32),
                        pltpu.VMEM((NB, R, LANES), F32), pltpu.VMEM((NB, R, LANES), F32),
                        pltpu.VMEM((2, 2 * BAND, 2 * BAND), F32)],
        compiler_params=_cp("parallel", "parallel"),
    )(*args)


def _attn_bwd(q_all, kv_all, do, o, lse, cos2, ss2, g, d, prev, *, name, dep=None):
    T = q_all.shape[0]
    NB = len(BRANCHES)
    Dm = q_all.shape[1] // NB
    HP = Dm // LANES
    R = _attn_block_rows(T)
    nblk = T // R
    units = R // BAND
    pr = BAND * d
    per = R // pr
    scale = HEAD_DIM ** -0.5

    def rope_bwd(t, cosv, ssv, first):
        return t * cosv - _rot_half(t, first) * ssv

    def body(q_ref, kp_ref, kc_ref, vp_ref, vc_ref, do_ref, o_ref, l_ref, cos_ref, ss_ref, *rest):
        dq_ref, dk_ref, dv_ref, kbuf, vbuf, dkbuf, dvbuf, ck_car, cv_car, bias = rest[-10:]
        i = pl.program_id(1)
        n = nblk - 1 - i
        _fill_band_bias(bias)
        first = _first_half_mask(BAND)
        hm = _head_mask()
        low = lax.broadcasted_iota(jnp.int32, (BAND, LANES), 1) < HEAD_DIM

        kbuf[0:pr, :] = kp_ref[...]
        kbuf[pr:pr + R, :] = kc_ref[...]
        vbuf[0:pr, :] = vp_ref[...]
        vbuf[pr:pr + R, :] = vc_ref[...]
        dkbuf[...] = jnp.zeros_like(dkbuf)
        dvbuf[...] = jnp.zeros_like(dvbuf)

        @pl.when(i > 0)
        def _():
            dkbuf[R:R + pr, :] = ck_car[...]
            dvbuf[R:R + pr, :] = cv_car[...]

        def unit(u, carry):
            sub = u // d
            base = sub * pr + (u - sub * d)
            sl = _strided(base, BAND, d)
            sl2 = _strided(base, 2 * BAND, d)
            q = q_ref[sl, :]
            dov = do_ref[sl, :]
            ov = o_ref[sl, :]
            lv = l_ref[sl, :]
            q2 = jnp.where(hm, jnp.concatenate([q, q], axis=0), 0.0).astype(BF)
            do2 = jnp.where(hm, jnp.concatenate([dov, dov], axis=0), 0.0)
            oo = dov * ov
            delta = jnp.sum(jnp.where(hm, jnp.concatenate([oo, oo], axis=0), 0.0), axis=-1, keepdims=True)
            lse2 = jnp.concatenate([lv[:, 0:1], lv[:, HEAD_DIM:HEAD_DIM + 1]], axis=0)
            do2 = do2.astype(BF)
            k2 = kbuf[sl2, :].astype(BF)
            v2 = vbuf[sl2, :].astype(BF)
            s = lax.dot_general(q2, k2, (((1,), (1,)), ((), ())), preferred_element_type=F32)
            p = jnp.exp(s + bias[((n > 0) | (sub > 0)).astype(jnp.int32)] - lse2)
            dp = lax.dot_general(do2, v2, (((1,), (1,)), ((), ())), preferred_element_type=F32)
            ds = (p * (dp - delta)).astype(BF)
            dq2 = jnp.dot(ds, k2, preferred_element_type=F32)
            dq = jnp.where(low, dq2[:BAND], dq2[BAND:])
            dq_ref[sl, :] = rope_bwd(dq, cos_ref[sl, :], ss_ref[sl, :], first) * scale
            dkbuf[sl2, :] += jnp.dot(ds.T, q2, preferred_element_type=F32)
            dvbuf[sl2, :] += jnp.dot(p.astype(BF).T, do2, preferred_element_type=F32)
            return carry

        lax.fori_loop(0, units, unit, 0, unroll=2)

        dk_ref[...] = rope_bwd(dkbuf[pr:pr + R, :], cos_ref[...], ss_ref[...], _first_half_mask(R))
        dv_ref[...] = dvbuf[pr:pr + R, :]
        ck_car[...] = dkbuf[0:pr, :]
        cv_car[...] = dvbuf[0:pr, :]

    blk = (R, LANES)
    pblk = (pr, LANES)
    cur = lambda hp, i: nblk - 1 - i
    prv = lambda hp, i: jnp.maximum((nblk - 1 - i) * per - 1, 0)
    in_specs = [
        pl.BlockSpec(blk, lambda hp, i: (cur(hp, i), g * HP + hp)),
        pl.BlockSpec(pblk, lambda hp, i: (prv(hp, i), g * HP + hp)),
        pl.BlockSpec(blk, lambda hp, i: (cur(hp, i), g * HP + hp)),
        pl.BlockSpec(pblk, lambda hp, i: (prv(hp, i), (NB + g) * HP + hp)),
        pl.BlockSpec(blk, lambda hp, i: (cur(hp, i), (NB + g) * HP + hp)),
        pl.BlockSpec(blk, lambda hp, i: (cur(hp, i), hp)),
        pl.BlockSpec(blk, lambda hp, i: (cur(hp, i), hp)),
        pl.BlockSpec(blk, lambda hp, i: (cur(hp, i), hp)),
        pl.BlockSpec(blk, lambda hp, i: (cur(hp, i), 0)),
        pl.BlockSpec(blk, lambda hp, i: (cur(hp, i), 0)),
    ]
    args = [q_all, kv_all, kv_all, kv_all, kv_all, do, o, lse, cos2, ss2]
    if dep is not None:
        in_specs.append(ANY)
        args.append(dep)
    aliases = {}
    if prev is not None:
        in_specs += [ANY, ANY, ANY]
        aliases = {len(args): 0, len(args) + 1: 1, len(args) + 2: 2}
        args += list(prev)
    wide = jax.ShapeDtypeStruct((T, NB * Dm), F32)
    out = pl.BlockSpec(blk, lambda hp, i: (cur(hp, i), g * HP + hp))
    return pl.pallas_call(
        body, name=name,
        grid=(HP, nblk),
        in_specs=in_specs,
        out_specs=[out, out, out],
        out_shape=[wide, wide, wide],
        scratch_shapes=[pltpu.VMEM((pr + R, LANES), F32), pltpu.VMEM((pr + R, LANES), F32),
                        pltpu.VMEM((pr + R, LANES), F32), pltpu.VMEM((pr + R, LANES), F32),
                        pltpu.VMEM(pblk, F32), pltpu.VMEM(pblk, F32),
                        pltpu.VMEM((2, 2 * BAND, 2 * BAND), F32)],
        input_output_aliases=aliases,
        compiler_params=_cp("arbitrary", "arbitrary"),
    )(*args)


def _postnorm_bwd(dh, y, g_ref_val):
    r = lax.rsqrt(jnp.mean(y * y, axis=-1, keepdims=True) + RMS_EPS)
    yn = y * r
    dyn = dh * g_ref_val
    dy = r * (dyn - yn * jnp.mean(dyn * yn, axis=-1, keepdims=True))
    return dy, yn


def _after(body, n_in, dep):
    if dep is None:
        return body
    return lambda *refs: body(*refs[:n_in], *refs[n_in + 1:])


def _dep_spec(dep):
    return [] if dep is None else [ANY]


def _dep_arg(dep):
    return [] if dep is None else [dep]


def _postnorm_bwd_matmul(dh, y, gain, w3, widx, *, name, da_dtype, dep=None):
    T, D = dh.shape
    K = w3.shape[1]
    tm = min(ROW_BLOCK_WIDE, T)

    def body(dh_ref, y_ref, g_ref, w_ref, dy_ref, da_ref, dg_ref):
        i = pl.program_id(0)

        @pl.when(i == 0)
        def _():
            dg_ref[...] = jnp.zeros_like(dg_ref)

        dhv = dh_ref[...]
        dy, yn = _postnorm_bwd(dhv, y_ref[...].astype(F32), g_ref[...])
        dg_ref[...] += jnp.sum(dhv * yn, axis=0, keepdims=True)
        dyb = dy.astype(BF)
        dy_ref[...] = dyb
        da = lax.dot_general(dyb, w_ref[...], (((1,), (1,)), ((), ())), preferred_element_type=F32)
        da_ref[...] = da.astype(da_dtype)

    return pl.pallas_call(
        _after(body, 4, dep), name=name,
        grid=(T // tm,),
        in_specs=[
            pl.BlockSpec((tm, D), lambda i: (i, 0)),
            pl.BlockSpec((tm, D), lambda i: (i, 0)),
            pl.BlockSpec((1, D), lambda i: (0, 0)),
            pl.BlockSpec((None, K, D), lambda i: (widx, 0, 0)),
        ] + _dep_spec(dep),
        out_specs=[pl.BlockSpec((tm, D), lambda i: (i, 0)),
                   pl.BlockSpec((tm, K), lambda i: (i, 0)),
                   pl.BlockSpec((1, D), lambda i: (0, 0))],
        out_shape=[jax.ShapeDtypeStruct((T, D), BF), jax.ShapeDtypeStruct((T, K), da_dtype),
                   jax.ShapeDtypeStruct((1, D), F32)],
        compiler_params=_cp("arbitrary"),
    )(dh, y, gain, w3, *_dep_arg(dep))


def _postnorm_bwd_swiglu(dh, y, gain, wd3, layer, g, u, *, name, dep=None):
    T, D = dh.shape
    F = wd3.shape[1]
    nf = F // 2
    tm = min(ROW_BLOCK, T)

    def body(dh_ref, y_ref, g_ref, w_ref, gg_ref, uu_ref, dy_ref, dgo_ref, duo_ref, dgain_ref, dys):
        i = pl.program_id(0)
        j = pl.program_id(1)

        @pl.when((i == 0) & (j == 0))
        def _():
            dgain_ref[...] = jnp.zeros_like(dgain_ref)

        @pl.when(j == 0)
        def _():
            dhv = dh_ref[...]
            dy, yn = _postnorm_bwd(dhv, y_ref[...].astype(F32), g_ref[...])
            dgain_ref[...] += jnp.sum(dhv * yn, axis=0, keepdims=True)
            dyb = dy.astype(BF)
            dys[...] = dyb
            dy_ref[...] = dyb

        da = lax.dot_general(dys[...], w_ref[...], (((1,), (1,)), ((), ())), preferred_element_type=F32)
        gv = gg_ref[...].astype(F32)
        uv = uu_ref[...].astype(F32)
        sg = jax.nn.sigmoid(gv)
        silu = gv * sg
        dgo_ref[...] = (da * uv * (sg + silu * (1.0 - sg))).astype(BF)
        duo_ref[...] = (da * silu).astype(BF)

    act = jax.ShapeDtypeStruct((T, F), BF)
    return pl.pallas_call(
        _after(body, 6, dep), name=name,
        grid=(T // tm, 2),
        in_specs=[
            pl.BlockSpec((tm, D), lambda i, j: (i, 0)),
            pl.BlockSpec((tm, D), lambda i, j: (i, 0)),
            pl.BlockSpec((1, D), lambda i, j: (0, 0)),
            pl.BlockSpec((None, nf, D), lambda i, j: (layer, j, 0)),
            pl.BlockSpec((tm, nf), lambda i, j: (i, j)),
            pl.BlockSpec((tm, nf), lambda i, j: (i, j)),
        ] + _dep_spec(dep),
        out_specs=[pl.BlockSpec((tm, D), lambda i, j: (i, 0)),
                   pl.BlockSpec((tm, nf), lambda i, j: (i, j)),
                   pl.BlockSpec((tm, nf), lambda i, j: (i, j)),
                   pl.BlockSpec((1, D), lambda i, j: (0, 0))],
        out_shape=[jax.ShapeDtypeStruct((T, D), BF), act, act, jax.ShapeDtypeStruct((1, D), F32)],
        scratch_shapes=[pltpu.VMEM((tm, D), BF)],
        compiler_params=_cp("arbitrary", "arbitrary"),
    )(dh, y, gain, wd3, g, u, *_dep_arg(dep))


def _matmul_prenorm_bwd(dzs, wg, layer, h, gain, dh_in, *, name):
    T, D = h.shape
    n = wg.shape[2]
    pair = len(dzs) == 2
    tm = min(ROW_BLOCK if pair else ROW_BLOCK_WIDE, T)
    nj = N_CHIPS // 2 if pair else N_CHIPS

    def body(*refs):
        dz_refs = refs[:len(dzs)]
        w_refs = refs[len(dzs):2 * len(dzs)]
        h_ref, g_ref, dhi_ref, dh_ref, dg_ref, acc = refs[2 * len(dzs):]
        i = pl.program_id(0)
        j = pl.program_id(1)

        @pl.when((i == 0) & (j == 0))
        def _():
            dg_ref[...] = jnp.zeros_like(dg_ref)

        part = None
        for dz_ref, w_ref in zip(dz_refs, w_refs):
            t = lax.dot_general(dz_ref[...].astype(BF), w_ref[...], (((1,), (1,)), ((), ())),
                                preferred_element_type=F32)
            part = t if part is None else part + t

        @pl.when(j == 0)
        def _():
            acc[...] = part

        @pl.when(j > 0)
        def _():
            acc[...] += part

        @pl.when(j == nj - 1)
        def _():
            dhn = acc[...]
            hv = h_ref[...]
            r = lax.rsqrt(jnp.mean(hv * hv, axis=-1, keepdims=True) + RMS_EPS)
            xh = hv * r
            dg_ref[...] += jnp.sum(dhn * xh, axis=0, keepdims=True)
            dxn = dhn * g_ref[...]
            dh_ref[...] = dhi_ref[...] + r * (dxn - xh * jnp.mean(dxn * xh, axis=-1, keepdims=True))

    in_specs = [pl.BlockSpec((tm, n), lambda i, j: (i, j)) for _ in dzs]
    if pair:
        in_specs += [pl.BlockSpec((None, D, n), lambda i, j: (j, layer, 0)),
                     pl.BlockSpec((None, D, n), lambda i, j: (j + nj, layer, 0))]
    else:
        in_specs += [pl.BlockSpec((None, D, n), lambda i, j: (j, layer, 0))]
    in_specs += [pl.BlockSpec((tm, D), lambda i, j: (i, 0)),
                 pl.BlockSpec((1, D), lambda i, j: (0, 0)),
                 pl.BlockSpec((tm, D), lambda i, j: (i, 0))]
    return pl.pallas_call(
        body, name=name,
        grid=(T // tm, nj),
        in_specs=in_specs,
        out_specs=[pl.BlockSpec((tm, D), lambda i, j: (i, 0)), pl.BlockSpec((1, D), lambda i, j: (0, 0))],
        out_shape=[jax.ShapeDtypeStruct((T, D), F32), jax.ShapeDtypeStruct((1, D), F32)],
        scratch_shapes=[pltpu.VMEM((tm, D), F32)],
        compiler_params=_cp("arbitrary", "arbitrary"),
    )(*dzs, *([wg] * len(dzs)), h, gain, dh_in)


def _grad_matmul(a, b, out_shape3, tme, tne, out_index, prev, *, name):
    T, M = a.shape
    N = b.shape[1]
    tk = min(ROW_BLOCK_WIDE, T)
    nk = T // tk

    def body(a_ref, b_ref, *rest):
        o_ref, acc = rest[-2:]
        k = pl.program_id(2)
        part = jnp.dot(a_ref[...].astype(BF).T, b_ref[...].astype(BF), preferred_element_type=F32)

        @pl.when(k == 0)
        def _():
            acc[...] = part

        @pl.when(k > 0)
        def _():
            acc[...] += part

        @pl.when(k == nk - 1)
        def _():
            o_ref[...] = acc[...].astype(BF)

    in_specs = [pl.BlockSpec((tk, tme), lambda i, j, k: (k, i)),
                pl.BlockSpec((tk, tne), lambda i, j, k: (k, j))]
    args = [a, b]
    aliases = {}
    if prev is not None:
        in_specs.append(ANY)
        args.append(prev)
        aliases = {2: 0}
    return pl.pallas_call(
        body, name=name,
        grid=(M // tme, N // tne, nk),
        in_specs=in_specs,
        out_specs=pl.BlockSpec((None, tme, tne), lambda i, j, k: out_index(i, j)),
        out_shape=jax.ShapeDtypeStruct(out_shape3, BF),
        scratch_shapes=[pltpu.VMEM((tme, tne), F32)],
        input_output_aliases=aliases,
        compiler_params=_cp("parallel", "parallel", "arbitrary"),
    )(*args)


def _row_tile(R):
    fit = [t for t in range(16, min(R, 512) + 1, 16) if R % t == 0]
    return max(fit) if fit else R


def _cast_place(w2d, where, dtype, *, name):
    R, C = w2d.shape
    tr = _row_tile(R)

    def body(s_ref, w_ref, o_ref):
        o_ref[...] = w_ref[...].astype(o_ref.dtype)

    return pl.pallas_call(
        body, name=name,
        grid_spec=pltpu.PrefetchScalarGridSpec(
            num_scalar_prefetch=1, grid=(R // tr,),
            in_specs=[pl.BlockSpec((tr, C), lambda i, s: (i, 0))],
            out_specs=pl.BlockSpec((None, tr, C), lambda i, s: (s[0], i, 0))),
        out_shape=jax.ShapeDtypeStruct((N_CHIPS, R, C), dtype),
        compiler_params=_cp("arbitrary"),
    )(where, w2d)


def _pair_sum(dw, theirs, where, *, name):
    G, rh, C = theirs.shape
    tr = _row_tile(rh)
    nr = rh // tr

    def body(s_ref, a_ref, b_ref, o_ref):
        o_ref[...] = (a_ref[...].astype(F32) + b_ref[...].astype(F32)).astype(BF)

    mine = pl.BlockSpec((None, tr, C), lambda g, i, s: (g, s[1] * nr + i, 0))
    spec = pl.BlockSpec((None, tr, C), lambda g, i, s: (g, i, 0))
    return pl.pallas_call(
        body, name=name,
        grid_spec=pltpu.PrefetchScalarGridSpec(
            num_scalar_prefetch=1, grid=(G, nr), in_specs=[mine, spec], out_specs=spec),
        out_shape=jax.ShapeDtypeStruct((G, rh, C), BF),
        compiler_params=_cp("arbitrary", "arbitrary"),
    )(where, dw, theirs)


def _chip_sum(landed, parts, where, total_rows, row_off, prev, *, name):
    G, rh, C = landed.shape
    tr = _row_tile(rh)
    nr = rh // tr
    base = row_off // tr

    def body(s_ref, l_ref, p_ref, *rest):
        o_ref = rest[-1]
        for j in range(G):
            def own(j=j):
                v = p_ref[...].astype(F32)
                o_ref[...] = v if j == 0 else o_ref[...] + v

            def other(j=j):
                v = l_ref[j].astype(F32)
                o_ref[...] = v if j == 0 else o_ref[...] + v

            pl.when(s_ref[0] == j)(own)
            pl.when(s_ref[0] != j)(other)

    in_specs = [pl.BlockSpec((G, tr, C), lambda i, s: (0, i, 0)),
                pl.BlockSpec((None, tr, C), lambda i, s: (s[0], i, 0))]
    args = [where, landed, parts]
    aliases = {}
    if prev is not None:
        in_specs.append(ANY)
        args.append(prev)
        aliases = {3: 0}
    return pl.pallas_call(
        body, name=name,
        grid_spec=pltpu.PrefetchScalarGridSpec(
            num_scalar_prefetch=1, grid=(nr,),
            in_specs=in_specs,
            out_specs=pl.BlockSpec((tr, C), lambda i, s: (base + s[1] * nr + i, 0))),
        out_shape=jax.ShapeDtypeStruct((total_rows, C), F32),
        input_output_aliases=aliases,
        compiler_params=_cp("arbitrary"),
    )(*args)


def _adamw(w, g, m, v, *, name):
    R, C = w.shape
    tr = _row_tile(R)

    def body(w_ref, g_ref, m_ref, v_ref, d_ref, mo_ref, vo_ref):
        gv = g_ref[...]
        mn = ADAM_B1 * m_ref[...] + (1.0 - ADAM_B1) * gv
        vn = ADAM_B2 * v_ref[...] + (1.0 - ADAM_B2) * jnp.square(gv)
        m_hat = mn / (1.0 - ADAM_B1 ** ADAM_STEP)
        v_hat = vn / (1.0 - ADAM_B2 ** ADAM_STEP)
        d_ref[...] = -ADAM_LR * (m_hat / (jnp.sqrt(v_hat) + ADAM_EPS) + ADAM_WD * w_ref[...])
        mo_ref[...] = mn
        vo_ref[...] = vn

    spec = pl.BlockSpec((tr, C), lambda i: (i, 0))
    shp = jax.ShapeDtypeStruct((R, C), F32)
    return pl.pallas_call(
        body, name=name, grid=(R // tr,), in_specs=[spec] * 4, out_specs=[spec] * 3,
        out_shape=[shp, shp, shp], compiler_params=_cp("parallel"),
    )(w, g, m, v)


def _place():
    x = lax.axis_index("x")
    y = lax.axis_index("y")
    c = lax.axis_index("c")
    chips = [(1 - x, y), (x, 1 - y), (1 - x, 1 - y)]
    return x, y, c, chips


def _chunk_rows(rows, row_bytes, align):
    if rows <= align:
        return rows
    cands = [r for r in range(align, rows + 1, align) if rows % r == 0]
    fit = [r for r in cands if r * row_bytes <= DMA_CHUNK_BYTES]
    return max(fit) if fit else min(cands)


def _row_align(dtype):
    return 8 * (4 // jnp.dtype(dtype).itemsize)


def _start_chunks(make, rows, rc):
    for r0 in range(0, rows, rc):
        make(r0, rc).start()


def _piece_rows(ref, piece, j, h, r0=0, n=None):
    _, lead, off, rows = piece
    rh = rows // 2
    n = rh if n is None else n
    if lead is not None:
        return ref.at[lead, j, pl.ds(h * rh + r0, n)]
    return ref.at[j, pl.ds(off + h * rh + r0, n)]


def _piece_chunk(arr, piece):
    rh = piece[3] // 2
    return rh, _chunk_rows(rh, arr.shape[-1] * arr.dtype.itemsize, _row_align(arr.dtype))


def _ag_start(arrays, taps, groups, *, name):
    na = len(arrays)
    ng = len(groups)
    n_sem = [3 * len(grp) + (3 if g == 0 else 0) for g, grp in enumerate(groups)]

    def body(*refs):
        ins = refs[:na]
        taps_ref = refs[na]
        sems = refs[na + 1:na + 1 + 2 * ng]
        x, y, c, chips = _place()
        myj = 2 * x + y
        for g, grp in enumerate(groups):
            ssem, rsem = sems[2 * g], sems[2 * g + 1]
            for idx, piece in enumerate(grp):
                ref = ins[piece[0]]
                rh, rc = _piece_chunk(arrays[piece[0]], piece)
                for k, (px, py) in enumerate(chips):
                    def send(r0, n, ref=ref, piece=piece, idx=idx, k=k, px=px, py=py, ssem=ssem, rsem=rsem):
                        part = _piece_rows(ref, piece, myj, c, r0, n)
                        return pltpu.make_async_remote_copy(
                            src_ref=part, dst_ref=part, send_sem=ssem.at[3 * idx + k], recv_sem=rsem.at[3 * idx + k],
                            device_id=(px, py, c), device_id_type=MESH)
                    _start_chunks(send, rh, rc)
            if g == 0:
                for k, (px, py) in enumerate(chips):
                    pltpu.make_async_remote_copy(
                        src_ref=taps_ref.at[myj], dst_ref=taps_ref.at[myj],
                        send_sem=ssem.at[3 * len(grp) + k], recv_sem=rsem.at[3 * len(grp) + k],
                        device_id=(px, py, c), device_id_type=MESH).start()

    sem_shapes = []
    for n in n_sem:
        sem_shapes += [pltpu.SemaphoreType.DMA((n,)), pltpu.SemaphoreType.DMA((n,))]
    bufs = [pltpu.HBM(a.shape, a.dtype) for a in arrays] + [pltpu.HBM(taps.shape, taps.dtype)]
    outs = pl.pallas_call(
        body, name=name,
        out_shape=(*sem_shapes, *bufs),
        in_specs=[HBM] * (na + 1),
        out_specs=(*([SEM] * (2 * ng)), *([HBM] * (na + 1))),
        input_output_aliases={i: 2 * ng + i for i in range(na + 1)},
        compiler_params=pltpu.CompilerParams(has_side_effects=EFFECT),
    )(*[_in_hbm(a) for a in arrays], _in_hbm(taps))
    sems = [(outs[2 * g], outs[2 * g + 1]) for g in range(ng)]
    return sems, list(outs[2 * ng:2 * ng + na]), outs[2 * ng + na]


def _ag_wait(sems, vals, taps, group, after, *, name):
    nv = len(vals)
    extra = ([taps] if taps is not None else [])
    nb = nv + len(extra)

    def body(*refs):
        bufs = refs[:nb]
        ssem, rsem = refs[nb], refs[nb + 1]
        x, y, c, chips = _place()
        for idx, piece in enumerate(group):
            for k, (px, py) in enumerate(chips):
                got = _piece_rows(bufs[piece[0]], piece, 2 * px + py, c)
                cp = pltpu.make_async_remote_copy(
                    src_ref=got, dst_ref=got, send_sem=ssem.at[3 * idx + k], recv_sem=rsem.at[3 * idx + k],
                    device_id=(px, py, c), device_id_type=MESH)
                cp.wait_send()
                cp.wait_recv()
        if taps is not None:
            for k, (px, py) in enumerate(chips):
                got = bufs[nv].at[2 * px + py]
                cp = pltpu.make_async_remote_copy(
                    src_ref=got, dst_ref=got, send_sem=ssem.at[3 * len(group) + k],
                    recv_sem=rsem.at[3 * len(group) + k], device_id=(px, py, c), device_id_type=MESH)
                cp.wait_send()
                cp.wait_recv()

    ops = list(vals) + extra
    shapes = [pltpu.HBM(a.shape, a.dtype) for a in ops]
    outs = pl.pallas_call(
        body, name=name,
        out_shape=tuple(shapes),
        in_specs=[HBM] * nb + [SEM, SEM] + _dep_spec(after),
        out_specs=[HBM] * nb,
        input_output_aliases={i: i for i in range(nb)},
        compiler_params=pltpu.CompilerParams(has_side_effects=EFFECT),
    )(*ops, sems[0], sems[1], *_dep_arg(after))
    return list(outs[:nv]), (outs[nv] if taps is not None else None)


def _ag_forward(vals, group, *, name):
    nv = len(vals)
    npc = len(group)

    def body(*refs):
        bufs = refs[nv:2 * nv]
        fsem, gsem = refs[2 * nv:]
        x, y, c, chips = _place()
        sib = (x, y, 1 - c)
        sent = []
        for idx, piece in enumerate(group):
            rh, rc = _piece_chunk(vals[piece[0]], piece)
            for k, (px, py) in enumerate(chips):
                def fwd(r0, n, piece=piece, idx=idx, k=k, pj=2 * px + py):
                    part = _piece_rows(bufs[piece[0]], piece, pj, c, r0, n)
                    return pltpu.make_async_remote_copy(
                        src_ref=part, dst_ref=part, send_sem=fsem.at[3 * idx + k], recv_sem=gsem.at[3 * idx + k],
                        device_id=sib, device_id_type=MESH)
                _start_chunks(fwd, rh, rc)
                sent.append(fwd(0, rh))
        for idx, piece in enumerate(group):
            for k, (px, py) in enumerate(chips):
                theirs = _piece_rows(bufs[piece[0]], piece, 2 * px + py, 1 - c)
                pltpu.make_async_remote_copy(
                    src_ref=theirs, dst_ref=theirs, send_sem=fsem.at[3 * idx + k], recv_sem=gsem.at[3 * idx + k],
                    device_id=sib, device_id_type=MESH).wait_recv()
        for cp in sent:
            cp.wait_send()

    return pl.pallas_call(
        body, name=name,
        in_specs=[ANY] * nv, out_specs=[ANY] * nv,
        out_shape=[jax.ShapeDtypeStruct(v.shape, v.dtype) for v in vals],
        input_output_aliases={i: i for i in range(nv)},
        scratch_shapes=[pltpu.SemaphoreType.DMA((3 * npc,)), pltpu.SemaphoreType.DMA((3 * npc,))],
    )(*vals)


def _sibling_swap(dws, *, name):
    nm = len(dws)
    shapes = [jax.ShapeDtypeStruct((dw.shape[0], dw.shape[1] // 2, dw.shape[2]), dw.dtype) for dw in dws]

    def body(*refs):
        ins = refs[:nm]
        theirs = refs[nm:2 * nm]
        ssem, rsem = refs[2 * nm:]
        x, y, c, _ = _place()
        sib = (x, y, 1 - c)
        cps = []
        for m in range(nm):
            G, rh, cols = shapes[m].shape
            rc = _chunk_rows(rh, cols * shapes[m].dtype.itemsize, _row_align(shapes[m].dtype))
            for j in range(G):
                _start_chunks(lambda r0, n, m=m, j=j, rh=rh: pltpu.make_async_remote_copy(
                    src_ref=ins[m].at[j, pl.ds((1 - c) * rh + r0, n)],
                    dst_ref=theirs[m].at[j, pl.ds(r0, n)], send_sem=ssem.at[m], recv_sem=rsem.at[m],
                    device_id=sib, device_id_type=MESH), rh, rc)
            cps.append(pltpu.make_async_remote_copy(
                src_ref=ins[m].at[:, pl.ds((1 - c) * rh, rh), :], dst_ref=theirs[m],
                send_sem=ssem.at[m], recv_sem=rsem.at[m], device_id=sib, device_id_type=MESH))
        for cp in cps:
            cp.wait()

    return pl.pallas_call(
        body, name=name,
        in_specs=[ANY] * nm, out_specs=[ANY] * nm, out_shape=shapes,
        scratch_shapes=[pltpu.SemaphoreType.DMA((nm,)), pltpu.SemaphoreType.DMA((nm,))],
    )(*dws)


HBM = pl.BlockSpec(memory_space=pltpu.HBM)
SEM = pl.BlockSpec(memory_space=pltpu.SEMAPHORE)
EFFECT = pltpu.SideEffectType.DATAFLOW_SIDE_EFFECTING


def _in_hbm(a):
    return pltpu.with_memory_space_constraint(a, pltpu.HBM)


def _exchange_start(parts, *, name):
    nm = len(parts)

    def body(*refs):
        ins = refs[:nm]
        lands = refs[nm:2 * nm]
        ssem, rsem = refs[2 * nm:2 * nm + 2]
        token = refs[-1]
        x, y, c, chips = _place()
        myj = 2 * x + y
        for m in range(nm):
            _, rh, cols = parts[m].shape
            rc = _chunk_rows(rh, cols * parts[m].dtype.itemsize, _row_align(parts[m].dtype))
            for k, (px, py) in enumerate(chips):
                _start_chunks(lambda r0, n, m=m, k=k, px=px, py=py: pltpu.make_async_remote_copy(
                    src_ref=ins[m].at[2 * px + py, pl.ds(r0, n)], dst_ref=lands[m].at[myj, pl.ds(r0, n)],
                    send_sem=ssem.at[3 * m + k], recv_sem=rsem.at[3 * m + k],
                    device_id=(px, py, c), device_id_type=MESH), rh, rc)
        token[...] = jnp.zeros_like(token)

    bufs = [pltpu.HBM(p.shape, p.dtype) for p in parts]
    outs = pl.pallas_call(
        body, name=name,
        out_shape=(pltpu.SemaphoreType.DMA((3 * nm,)), pltpu.SemaphoreType.DMA((3 * nm,)), *bufs, *bufs,
                   jax.ShapeDtypeStruct((8, LANES), F32)),
        in_specs=[HBM] * (2 * nm),
        out_specs=(SEM, SEM, *([HBM] * (2 * nm)), pl.BlockSpec(memory_space=pltpu.VMEM)),
        input_output_aliases={i: 2 + i for i in range(2 * nm)},
        compiler_params=pltpu.CompilerParams(has_side_effects=EFFECT),
    )(*[_in_hbm(p) for p in parts], *[_in_hbm(lax.empty(p.shape, p.dtype)) for p in parts])
    return (outs[0], outs[1], list(outs[2:2 + nm]), list(outs[2 + nm:2 + 2 * nm])), outs[-1]


def _exchange_wait(handle, after, *, name):
    ssem_in, rsem_in, parts, lands = handle
    nm = len(parts)

    def body(*refs):
        ins = refs[:nm]
        lnd = refs[nm:2 * nm]
        ssem, rsem = refs[2 * nm:2 * nm + 2]
        x, y, c, chips = _place()
        for m in range(nm):
            for k, (px, py) in enumerate(chips):
                pj = 2 * px + py
                cp = pltpu.make_async_remote_copy(
                    src_ref=ins[m].at[pj], dst_ref=lnd[m].at[pj],
                    send_sem=ssem.at[3 * m + k], recv_sem=rsem.at[3 * m + k],
                    device_id=(px, py, c), device_id_type=MESH)
                cp.wait_send()
                cp.wait_recv()

    bufs = [pltpu.HBM(p.shape, p.dtype) for p in parts]
    outs = pl.pallas_call(
        body, name=name,
        out_shape=(*bufs, *bufs),
        in_specs=[HBM] * (2 * nm) + [SEM, SEM, ANY],
        out_specs=[HBM] * (2 * nm),
        input_output_aliases={i: i for i in range(2 * nm)},
        compiler_params=pltpu.CompilerParams(has_side_effects=EFFECT),
    )(*parts, *lands, ssem_in, rsem_in, after)
    return list(outs[nm:]), list(outs[:nm])


def _sibling_join(grads, regions):
    nm = len(grads)
    nr = len(regions)
    shapes = [jax.ShapeDtypeStruct(g.shape, g.dtype) for g in grads]

    def body(*refs):
        outs = refs[nm:2 * nm]
        ssem, rsem = refs[2 * nm:]
        x, y, c, _ = _place()
        sib = (x, y, 1 - c)
        cps = []
        for i, (m, off, rows) in enumerate(regions):
            rh, cols = rows // 2, grads[m].shape[1]
            rc = _chunk_rows(rh, cols * grads[m].dtype.itemsize, _row_align(grads[m].dtype))

            def send(r0, n, i=i, m=m, off=off, rh=rh):
                part = outs[m].at[pl.ds(off + c * rh + r0, n)]
                return pltpu.make_async_remote_copy(
                    src_ref=part, dst_ref=part, send_sem=ssem.at[i], recv_sem=rsem.at[i],
                    device_id=sib, device_id_type=MESH)
            _start_chunks(send, rh, rc)
            cps.append(send(0, rh))
        for i, (m, off, rows) in enumerate(regions):
            rh = rows // 2
            cps[i].wait_send()
            got = outs[m].at[pl.ds(off + (1 - c) * rh, rh)]
            pltpu.make_async_remote_copy(
                src_ref=got, dst_ref=got, send_sem=ssem.at[i], recv_sem=rsem.at[i],
                device_id=sib, device_id_type=MESH).wait_recv()

    return pl.pallas_call(
        body, name="rs_sibling_join",
        in_specs=[ANY] * nm, out_specs=[ANY] * nm, out_shape=shapes,
        input_output_aliases={i: i for i in range(nm)},
        scratch_shapes=[pltpu.SemaphoreType.DMA((nr,)), pltpu.SemaphoreType.DMA((nr,))],
    )(*grads)


def _all_reduce_small(pack):
    R, C = pack.shape

    def body(in_ref, out_ref, slots, ssem, rsem):
        x, y, c, _ = _place()
        me = 4 * x + 2 * y + c
        slots[me] = in_ref[...]
        cps = []
        for k in range(1, N_DEV):
            dx, dy, dc = (k >> 2) & 1, (k >> 1) & 1, k & 1
            peer = (x ^ dx, y ^ dy, c ^ dc)
            cp = pltpu.make_async_remote_copy(
                src_ref=in_ref, dst_ref=slots.at[me], send_sem=ssem.at[k], recv_sem=rsem.at[k],
                device_id=peer, device_id_type=MESH)
            cp.start()
            cps.append(cp)
        for k in range(1, N_DEV):
            dx, dy, dc = (k >> 2) & 1, (k >> 1) & 1, k & 1
            got = slots.at[4 * (x ^ dx) + 2 * (y ^ dy) + (c ^ dc)]
            pltpu.make_async_remote_copy(
                src_ref=got, dst_ref=got, send_sem=ssem.at[k], recv_sem=rsem.at[k],
                device_id=(x ^ dx, y ^ dy, c ^ dc), device_id_type=MESH).wait_recv()
        for cp in cps:
            cp.wait_send()
        acc = slots[0]
        for s in range(1, N_DEV):
            acc = acc + slots[s]
        out_ref[...] = acc

    return pl.pallas_call(
        body, name="ar_small",
        in_specs=[pl.BlockSpec(memory_space=pltpu.VMEM)],
        out_specs=pl.BlockSpec(memory_space=pltpu.VMEM),
        out_shape=jax.ShapeDtypeStruct((R, C), F32),
        scratch_shapes=[pltpu.VMEM((N_DEV, R, C), F32),
                        pltpu.SemaphoreType.DMA((N_DEV,)), pltpu.SemaphoreType.DMA((N_DEV,))],
    )(pack)


def kernel(x, positions, mix_norm_pre, mix_norm_post, ffn_norm_pre, ffn_norm_post, ffn_w_gate_up, ffn_w_down, conv_w_in, conv_w, conv_w_out, kv_norm, w_kv, w_q, w_o, loss_target, m_mix_norm_pre, m_mix_norm_post, m_ffn_norm_pre, m_ffn_norm_post, m_ffn_w_gate_up, m_ffn_w_down, m_conv_w_in, m_conv_w, m_conv_w_out, m_kv_norm, m_w_kv, m_w_q, m_w_o, v_mix_norm_pre, v_mix_norm_post, v_ffn_norm_pre, v_ffn_norm_post, v_ffn_w_gate_up, v_ffn_w_down, v_conv_w_in, v_conv_w, v_conv_w_out, v_kv_norm, v_w_kv, v_w_q, v_w_o):
    T, D = x.shape[1], x.shape[2]
    L = ffn_w_gate_up.shape[0]
    n_gu = ffn_w_gate_up.shape[2]
    f_sh = ffn_w_down.shape[1]
    F = N_CHIPS * f_sh
    x0 = x[0]
    tgt = loss_target[0]

    half = HEAD_DIM // 2
    inv_freq = ROPE_THETA ** (-jnp.arange(half, dtype=F32) / half)
    ang = positions[0].astype(F32)[:, None] * inv_freq
    cosv, sinv = jnp.cos(ang), jnp.sin(ang)
    cos2 = jnp.tile(cosv, (1, LANES // half))
    ss2 = jnp.tile(jnp.concatenate([-sinv, sinv], axis=1), (1, LANES // HEAD_DIM))

    def as2d(a):
        return a.reshape(-1, a.shape[-1])

    big = [ffn_w_gate_up, ffn_w_down, conv_w_in, conv_w_out, w_kv, w_q, w_o]
    big_m = [m_ffn_w_gate_up, m_ffn_w_down, m_conv_w_in, m_conv_w_out, m_w_kv, m_w_q, m_w_o]
    big_v = [v_ffn_w_gate_up, v_ffn_w_down, v_conv_w_in, v_conv_w_out, v_w_kv, v_w_q, v_w_o]
    chip = 2 * lax.axis_index("x") + lax.axis_index("y")
    where = jnp.stack([chip, lax.axis_index("c")]).astype(jnp.int32)
    tc = conv_w.shape[2]
    cw_pad = jnp.concatenate([conv_w[0], jnp.zeros((8 - conv_w.shape[1], tc), F32)], axis=0)

    GU0, GU1, WD0, WD1, WCI, WCO, WKV, WQ, WO = range(9)
    shards = [ffn_w_gate_up[0], ffn_w_gate_up[1], ffn_w_down[0], ffn_w_down[1], conv_w_in[0], conv_w_out[0],
              w_kv, w_q[0], w_o[0]]
    placed = [_cast_place(w, where, BF, name=f"place{i}") for i, w in enumerate(shards)]
    ag_groups = [
        [(WCI, None, 0, D), (WCO, None, 0, D // N_CHIPS)],
        [(GU0, None, 0, D), (WD0, None, 0, f_sh)],
        [(WKV, None, 0, D), (WQ, None, 0, D)],
        [(WO, None, 0, D // N_CHIPS), (GU1, None, 0, D), (WD1, None, 0, f_sh)],
    ]
    ag_sems, cur, taps = _ag_start(placed, _cast_place(cw_pad, where, F32, name="place_taps"),
                                   ag_groups, name="ag_start")

    def gather_group(g, after):
        nonlocal taps
        idxs = sorted({p[0] for p in ag_groups[g]})
        local = [(idxs.index(p[0]),) + p[1:] for p in ag_groups[g]]
        vals, landed_taps = _ag_wait(ag_sems[g], [cur[i] for i in idxs], taps if g == 0 else None, local, after,
                                     name=f"ag_wait{g}")
        if g == 0:
            taps = landed_taps
        vals = _ag_forward(vals, local, name=f"ag_forward{g}")
        for i, v in zip(idxs, vals):
            cur[i] = v

    def row(a, i):
        return a[i:i + 1]

    gather_group(0, None)
    wci, wco, cw = cur[WCI], cur[WCO].reshape(1, D, D), taps
    z, hn_m0 = _norm_matmul(x0, row(mix_norm_pre, 0), wci, cos2, ss2, name="f0_conv_in",
                            rope_shards=0, scale=1.0, out_dtype=BF)
    vmix = _conv_fwd(z, cw, name="f0_conv")
    y0, h1 = _matmul_postnorm(vmix, wco, 0, row(mix_norm_post, 0), x0, name="f0_conv_out")
    gather_group(1, h1)
    wgu0, wd0 = cur[GU0], cur[WD0].reshape(1, F, D)
    g0, u0, a0, hn_f0 = _norm_swiglu(h1, row(ffn_norm_pre, 0), wgu0, 0, name="f0_gate_up")
    f0, h2 = _matmul_postnorm(a0, wd0, 0, row(ffn_norm_post, 0), h1, name="f0_down")

    gather_group(2, h2)
    wkv, wq = cur[WKV], cur[WQ]
    kv_all, hn_kv = _norm_matmul(h2, kv_norm.reshape(1, D), wkv, cos2, ss2, name="f1_kv",
                                 rope_shards=N_CHIPS // 2, scale=1.0, out_dtype=F32)
    q_all, hn_m1 = _norm_matmul(h2, row(mix_norm_pre, 1), wq, cos2, ss2, name="f1_q",
                                rope_shards=N_CHIPS, scale=HEAD_DIM ** -0.5, out_dtype=F32)
    o_att, lse = _attn_fwd(q_all, kv_all, name="f1_attn")
    gather_group(3, o_att)
    wgu1, wd1, wo = cur[GU1], cur[WD1].reshape(1, F, D), cur[WO].reshape(1, D, D)
    y1, h3 = _matmul_postnorm(o_att, wo, 0, row(mix_norm_post, 1), h2, name="f1_attn_out")
    g1, u1, a1, hn_f1 = _norm_swiglu(h3, row(ffn_norm_pre, 1), wgu1, 0, name="f1_gate_up")
    f1, h4 = _matmul_postnorm(a1, wd1, 0, row(ffn_norm_post, 1), h3, name="f1_down")

    dh4, sq = _loss_head(h4, tgt, name="loss_head")
    loss_part = 0.5 * sq[0, 0] / D

    gu_shape = (N_CHIPS, D, n_gu)
    in_chips = lambda a: a.reshape(N_CHIPS, -1, a.shape[-1])

    def scatter_start(dws, tag):
        theirs = _sibling_swap(dws, name=f"rs_swap_{tag}")
        parts = [_pair_sum(dw, t, where, name=f"rs_pair_sum_{tag}{i}") for i, (dw, t) in enumerate(zip(dws, theirs))]
        return _exchange_start(parts, name=f"rs_exchange_start_{tag}")

    dyf1, dg1, du1, d_ffn_post1 = _postnorm_bwd_swiglu(dh4, f1, row(ffn_norm_post, 1), wd1, 0, g1, u1,
                                                       name="b1_down")
    dwd1 = _grad_matmul(a1, dyf1, (2, F // 2, D), F // 2, D, lambda i, j: (i, 0, 0), None, name="b1_dw_down")
    dwgu1 = _grad_matmul(hn_f1, dg1, gu_shape, D, n_gu, lambda i, j: (j, 0, 0), None, name="b1_dw_gate")
    dwgu1 = _grad_matmul(hn_f1, du1, gu_shape, D, n_gu, lambda i, j: (j + 2, 0, 0), dwgu1, name="b1_dw_up")
    dh3, d_ffn_pre1 = _matmul_prenorm_bwd((dg1, du1), wgu1, 0, h3, row(ffn_norm_pre, 1), dh4, name="b1_gate_up")

    dy1, do, d_mix_post1 = _postnorm_bwd_matmul(dh3, y1, row(mix_norm_post, 1), wo, 0, name="b1_attn_out",
                                                da_dtype=F32)
    dwo = _grad_matmul(o_att, dy1, (1, D, D), D, D, lambda i, j: (0, 0, 0), None, name="b1_dw_o")
    rs_a, token = scatter_start([dwgu1, in_chips(dwd1), in_chips(dwo)], "a")
    prev = None
    for gi, (window, dil) in enumerate(BRANCHES):
        prev = _attn_bwd(q_all, kv_all, do, o_att, lse, cos2, ss2, gi, dil, prev, name=f"b1_attn{gi}",
                         dep=token if gi == 0 else None)
    dq_all, dk_all, dv_all = prev
    n_q = wq.shape[2]
    n_kv = wkv.shape[2]
    dwq = _grad_matmul(hn_m1, dq_all, (N_CHIPS, D, n_q), D, n_q, lambda i, j: (j, 0, 0), None, name="b1_dw_q")
    dwkv = _grad_matmul(hn_kv, dk_all, (N_CHIPS, D, n_kv), D, n_kv, lambda i, j: (j, 0, 0), None, name="b1_dw_k")
    dwkv = _grad_matmul(hn_kv, dv_all, (N_CHIPS, D, n_kv), D, n_kv, lambda i, j: (j + 2, 0, 0), dwkv, name="b1_dw_v")
    dh2, d_mix_pre1 = _matmul_prenorm_bwd((dq_all,), wq, 0, h2, row(mix_norm_pre, 1), dh3, name="b1_q")
    dh2, d_kv_norm = _matmul_prenorm_bwd((dk_all, dv_all), wkv, 0, h2, kv_norm.reshape(1, D), dh2, name="b1_kv")
    rs_b, token = scatter_start([dwkv, dwq], "b")

    dyf0, dg0, du0, d_ffn_post0 = _postnorm_bwd_swiglu(dh2, f0, row(ffn_norm_post, 0), wd0, 0, g0, u0,
                                                       name="b0_down", dep=token)
    dwd0 = _grad_matmul(a0, dyf0, (2, F // 2, D), F // 2, D, lambda i, j: (i, 0, 0), None, name="b0_dw_down")
    dwgu0 = _grad_matmul(hn_f0, dg0, gu_shape, D, n_gu, lambda i, j: (j, 0, 0), None, name="b0_dw_gate")
    dwgu0 = _grad_matmul(hn_f0, du0, gu_shape, D, n_gu, lambda i, j: (j + 2, 0, 0), dwgu0, name="b0_dw_up")
    dh1, d_ffn_pre0 = _matmul_prenorm_bwd((dg0, du0), wgu0, 0, h1, row(ffn_norm_pre, 0), dh2, name="b0_gate_up")
    rs_c, token = scatter_start([dwgu0, in_chips(dwd0)], "c")

    dy0, dvmix, d_mix_post0 = _postnorm_bwd_matmul(dh1, y0, row(mix_norm_post, 0), wco, 0, name="b0_conv_out",
                                                   da_dtype=BF, dep=token)
    dwco = _grad_matmul(vmix, dy0, (1, D, D), D, D, lambda i, j: (0, 0, 0), None, name="b0_dw_conv_out")
    dz, dcw = _conv_bwd(z, cw, dvmix, name="b0_conv")
    n_ci = wci.shape[2]
    dwci = _grad_matmul(hn_m0, dz, (N_CHIPS, D, n_ci), D, n_ci, lambda i, j: (j, 0, 0), None, name="b0_dw_conv_in")
    dx, d_mix_pre0 = _matmul_prenorm_bwd((dz,), wci, 0, x0, row(mix_norm_pre, 0), dh1, name="b0_conv_in")

    pack = jnp.concatenate([
        d_mix_pre0, d_mix_pre1, d_mix_post0, d_mix_post1, d_ffn_pre0, d_ffn_pre1, d_ffn_post0, d_ffn_post1,
        d_kv_norm, dcw[0:3], jnp.full((1, D), loss_part, F32),
        jnp.zeros((SMALL_ROWS - 13, D), F32)], axis=0)
    red = _all_reduce_small(pack)
    loss = red[12, 0]
    myj = 2 * lax.axis_index("x") + lax.axis_index("y")
    g_conv_w = lax.dynamic_slice(red, (9, myj * tc), (3, tc))

    zeros7 = jnp.zeros((SMALL_ROWS - 9, D), F32)
    w_small = jnp.concatenate([mix_norm_pre, mix_norm_post, ffn_norm_pre, ffn_norm_post, kv_norm.reshape(1, D), zeros7], axis=0)
    m_small = jnp.concatenate([m_mix_norm_pre, m_mix_norm_post, m_ffn_norm_pre, m_ffn_norm_post, m_kv_norm.reshape(1, D), zeros7], axis=0)
    v_small = jnp.concatenate([v_mix_norm_pre, v_mix_norm_post, v_ffn_norm_pre, v_ffn_norm_post, v_kv_norm.reshape(1, D), zeros7], axis=0)
    d_small, nm_small, nv_small = _adamw(w_small, red, m_small, v_small, name="adamw_small")

    pad5 = jnp.zeros((5, tc), F32)
    d_cw, nm_cw, nv_cw = _adamw(cw_pad, jnp.concatenate([g_conv_w, pad5], axis=0),
                                jnp.concatenate([m_conv_w[0], pad5], axis=0),
                                jnp.concatenate([v_conv_w[0], pad5], axis=0), name="adamw_conv_w")

    rs_d, _ = scatter_start([dwci, in_chips(dwco)], "d")

    pieces = {"a": [(0, D), (1, f_sh), (6, 0)], "b": [(4, 0), (5, 0)], "c": [(0, 0), (1, 0)], "d": [(2, 0), (3, 0)]}
    grads2d = [None] * len(big)
    regions = []
    for tag, handle in (("a", rs_a), ("b", rs_b), ("c", rs_c), ("d", rs_d)):
        landed, parts = _exchange_wait(handle, dx, name=f"rs_exchange_wait_{tag}")
        for i, (l, p, (wi, off)) in enumerate(zip(landed, parts, pieces[tag])):
            total = as2d(big[wi]).shape[0]
            grads2d[wi] = _chip_sum(l, p, where, total, off, grads2d[wi], name=f"rs_chip_sum_{tag}{i}")
            regions.append((wi, off, 2 * l.shape[1]))
    grads2d = _sibling_join(grads2d, regions)

    big_out = []
    for i, (w, gr, m, v) in enumerate(zip(big, grads2d, big_m, big_v)):
        d_, m_, v_ = _adamw(as2d(w), gr, as2d(m), as2d(v), name=f"adamw{i}")
        big_out.append((gr.reshape(w.shape), d_.reshape(w.shape), m_.reshape(w.shape), v_.reshape(w.shape)))

    def small(a):
        return (a[0:2], a[2:4], a[4:6], a[6:8])

    def assemble(sm, cwv, kind):
        pre, post, fpre, fpost = small(sm)
        b = [t[kind] for t in big_out]
        return [pre, post, fpre, fpost, b[0], b[1], b[2], cwv[0:3].reshape(conv_w.shape), b[3],
                sm[8], b[4], b[5].reshape(w_q.shape), b[6].reshape(w_o.shape)]

    grads = assemble(red, jnp.concatenate([g_conv_w, pad5], axis=0), 0)
    deltas = assemble(d_small, d_cw, 1)
    new_m = assemble(nm_small, nm_cw, 2)
    new_v = assemble(nv_small, nv_cw, 3)
    return (loss, dx.reshape(x.shape), *grads, *deltas, *new_m, *new_v)
```

```python
import functools

import jax
import jax.numpy as jnp
from jax import lax
from jax.experimental import pallas as pl
from jax.experimental.pallas import tpu as pltpu

HEAD_DIM = 64
BAND = 128
BRANCHES = ((128, 1), (512, 4), (2048, 16))
ROPE_THETA = 10000.0
RMS_EPS = 1e-6
NEG_INF = -1e30
ADAM_LR = 0.001
ADAM_B1 = 0.9
ADAM_B2 = 0.999
ADAM_EPS = 1e-08
ADAM_WD = 0.01
ADAM_STEP = 10

N_CHIPS = 4
N_DEV = 8
LANES = 128
MXU_COLS = 256
ROW_BLOCK = 512
ROW_BLOCK_WIDE = 1024
ATTN_BLOCK_ROWS = 2048
VMEM_LIMIT = 56 * 1024 * 1024
SMALL_ROWS = 16
DMA_CHUNK_BYTES = 512 * 1024

BF = jnp.bfloat16
F32 = jnp.float32
MESH = pl.DeviceIdType.MESH
ANY = pl.BlockSpec(memory_space=pl.ANY)


def _cp(*sem):
    return pltpu.CompilerParams(dimension_semantics=sem, vmem_limit_bytes=VMEM_LIMIT)


def _rot_half(t, first):
    return jnp.where(first, pltpu.roll(t, 96, 1), pltpu.roll(t, 32, 1))


def _sigmoid(x):
    return pl.reciprocal(1.0 + jnp.exp(-x), approx=True)


def _col_chunks(n):
    return [(c0, min(MXU_COLS, n - c0)) for c0 in range(0, n, MXU_COLS)]


def _first_half_mask(rows):
    lane = lax.broadcasted_iota(jnp.int32, (rows, LANES), 1)
    return (lane % HEAD_DIM) < (HEAD_DIM // 2)


def _normed_rows(j, rows, x_ref, g_ref, xn_ref, xs, last_start, tm):
    @pl.when(j == 0)
    def _():
        xv = x_ref[...]
        r = lax.rsqrt(jnp.mean(xv * xv, axis=-1, keepdims=True) + RMS_EPS)
        xn = (xv * r * g_ref[...]).astype(BF)
        xs[rows, :] = xn
        xn_ref[...] = xn

    @pl.when(j > 0)
    def _():
        xn_ref[...] = xs[pl.ds(last_start, tm), :]


def _norm_matmul(x, gain, wg, cos2, ss2, *, name, rope_shards, scale, out_dtype):
    T, D = x.shape
    n = wg.shape[2]
    tm = min(ROW_BLOCK_WIDE, T)
    ni = T // tm

    def body(x_ref, g_ref, w_ref, cos_ref, ss_ref, y_ref, xn_ref, xs):
        j = pl.program_id(0)
        rows = pl.ds(pl.multiple_of(pl.program_id(1) * tm, tm), tm)
        _normed_rows(j, rows, x_ref, g_ref, xn_ref, xs, (ni - 1) * tm, tm)
        acc = jnp.dot(xs[rows, :], w_ref[...], preferred_element_type=F32)

        def plain():
            y_ref[...] = acc.astype(out_dtype)

        def rope():
            cosv = cos_ref[...]
            ssv = ss_ref[...]
            first = _first_half_mask(tm)
            for ci in range(n // LANES):
                t = acc[:, ci * LANES:(ci + 1) * LANES]
                y = (t * cosv + _rot_half(t, first) * ssv) * scale
                y_ref[:, ci * LANES:(ci + 1) * LANES] = y.astype(out_dtype)

        if rope_shards == 0:
            plain()
        elif rope_shards == N_CHIPS:
            rope()
        else:
            pl.when(j < rope_shards)(rope)
            pl.when(j >= rope_shards)(plain)

    first_pass = lambda j, i: (jnp.where(j == 0, i, ni - 1), 0)
    return pl.pallas_call(
        body, name=name,
        grid=(N_CHIPS, ni),
        in_specs=[
            pl.BlockSpec((tm, D), first_pass),
            pl.BlockSpec((1, D), lambda j, i: (0, 0)),
            pl.BlockSpec((None, D, n), lambda j, i: (j, 0, 0)),
            pl.BlockSpec((tm, LANES), lambda j, i: (i, 0)),
            pl.BlockSpec((tm, LANES), lambda j, i: (i, 0)),
        ],
        out_specs=[
            pl.BlockSpec((tm, n), lambda j, i: (i, j)),
            pl.BlockSpec((tm, D), first_pass),
        ],
        out_shape=[jax.ShapeDtypeStruct((T, N_CHIPS * n), out_dtype),
                   jax.ShapeDtypeStruct((T, D), BF)],
        scratch_shapes=[pltpu.VMEM((T, D), BF)],
        compiler_params=_cp("arbitrary", "arbitrary"),
    )(x, gain, wg, cos2, ss2)


def _norm_swiglu(x, gain, wg, layer, *, name):
    T, D = x.shape
    n = wg.shape[2]
    tm = min(ROW_BLOCK, T)

    ni = T // tm

    def body(x_ref, g_ref, wg_ref, wu_ref, go_ref, uo_ref, ao_ref, xn_ref, xs):
        j = pl.program_id(0)
        rows = pl.ds(pl.multiple_of(pl.program_id(1) * tm, tm), tm)
        _normed_rows(j, rows, x_ref, g_ref, xn_ref, xs, (ni - 1) * tm, tm)
        g = jnp.dot(xs[rows, :], wg_ref[...], preferred_element_type=F32)
        u = jnp.dot(xs[rows, :], wu_ref[...], preferred_element_type=F32)
        go_ref[...] = g.astype(BF)
        uo_ref[...] = u.astype(BF)
        ao_ref[...] = (g * _sigmoid(g) * u).astype(BF)

    half = N_CHIPS // 2
    first_pass = lambda j, i: (jnp.where(j == 0, i, ni - 1), 0)
    act = jax.ShapeDtypeStruct((T, half * n), BF)
    return pl.pallas_call(
        body, name=name,
        grid=(half, ni),
        in_specs=[
            pl.BlockSpec((tm, D), first_pass),
            pl.BlockSpec((1, D), lambda j, i: (0, 0)),
            pl.BlockSpec((None, D, n), lambda j, i: (j, layer, 0)),
            pl.BlockSpec((None, D, n), lambda j, i: (j + half, layer, 0)),
        ],
        out_specs=[
            pl.BlockSpec((tm, n), lambda j, i: (i, j)),
            pl.BlockSpec((tm, n), lambda j, i: (i, j)),
            pl.BlockSpec((tm, n), lambda j, i: (i, j)),
            pl.BlockSpec((tm, D), first_pass),
        ],
        out_shape=[act, act, act, jax.ShapeDtypeStruct((T, D), BF)],
        scratch_shapes=[pltpu.VMEM((T, D), BF)],
        compiler_params=_cp("arbitrary", "arbitrary"),
    )(x, gain, wg, wg)


def _matmul_postnorm(a, w3, widx, gain, h_old, *, name):
    T, K = a.shape
    D = w3.shape[2]
    tm = min(ROW_BLOCK_WIDE, T)

    def body(a_ref, w_ref, g_ref, h_ref, y_ref, hn_ref):
        y = jnp.dot(a_ref[...].astype(BF), w_ref[...], preferred_element_type=F32)
        y_ref[...] = y.astype(BF)
        r = lax.rsqrt(jnp.mean(y * y, axis=-1, keepdims=True) + RMS_EPS)
        hn_ref[...] = h_ref[...] + y * r * g_ref[...]

    return pl.pallas_call(
        body, name=name,
        grid=(T // tm,),
        in_specs=[
            pl.BlockSpec((tm, K), lambda i: (i, 0)),
            pl.BlockSpec((None, K, D), lambda i: (widx, 0, 0)),
            pl.BlockSpec((1, D), lambda i: (0, 0)),
            pl.BlockSpec((tm, D), lambda i: (i, 0)),
        ],
        out_specs=[pl.BlockSpec((tm, D), lambda i: (i, 0)),
                   pl.BlockSpec((tm, D), lambda i: (i, 0))],
        out_shape=[jax.ShapeDtypeStruct((T, D), BF), jax.ShapeDtypeStruct((T, D), F32)],
        compiler_params=_cp("parallel"),
    )(a, w3, gain, h_old)


def _loss_head(h, target, *, name):
    T, D = h.shape
    tm = min(ROW_BLOCK, T)

    def body(h_ref, t_ref, dh_ref, s_ref):
        i = pl.program_id(0)

        @pl.when(i == 0)
        def _():
            s_ref[...] = jnp.zeros_like(s_ref)

        e = h_ref[...] - t_ref[...]
        dh_ref[...] = e * (1.0 / D)
        s_ref[...] += jnp.sum(e * e)

    return pl.pallas_call(
        body, name=name,
        grid=(T // tm,),
        in_specs=[pl.BlockSpec((tm, D), lambda i: (i, 0)), pl.BlockSpec((tm, D), lambda i: (i, 0))],
        out_specs=[pl.BlockSpec((tm, D), lambda i: (i, 0)), pl.BlockSpec((8, LANES), lambda i: (0, 0))],
        out_shape=[jax.ShapeDtypeStruct((T, D), F32), jax.ShapeDtypeStruct((8, LANES), F32)],
        compiler_params=_cp("arbitrary"),
    )(h, target)


def _shift_down(u, k):
    row = lax.broadcasted_iota(jnp.int32, u.shape, 0)
    return jnp.where(row >= k, pltpu.roll(u, k, 0), 0.0)


def _shift_up(u, k):
    T = u.shape[0]
    row = lax.broadcasted_iota(jnp.int32, u.shape, 0)
    return jnp.where(row < T - k, pltpu.roll(u, T - k, 0), 0.0)


def _conv_fwd(z, cw, *, name):
    T = z.shape[0]
    D = z.shape[1] // 3
    tc = cw.shape[2]
    nb = D // tc

    def body(b_ref, c_ref, h_ref, w_ref, o_ref):
        u = c_ref[...].astype(F32) * h_ref[...].astype(F32)
        w = w_ref[...]
        conv = w[2:3] * u + w[1:2] * _shift_down(u, 1) + w[0:1] * _shift_down(u, 2)
        o_ref[...] = (b_ref[...].astype(F32) * conv).astype(BF)

    return pl.pallas_call(
        body, name=name,
        grid=(nb,),
        in_specs=[
            pl.BlockSpec((T, tc), lambda j: (0, j)),
            pl.BlockSpec((T, tc), lambda j: (0, nb + j)),
            pl.BlockSpec((T, tc), lambda j: (0, 2 * nb + j)),
            pl.BlockSpec((None, 8, tc), lambda j: (j, 0, 0)),
        ],
        out_specs=pl.BlockSpec((T, tc), lambda j: (0, j)),
        out_shape=jax.ShapeDtypeStruct((T, D), BF),
        compiler_params=_cp("parallel"),
    )(z, z, z, cw)


def _conv_bwd(z, cw, dv, *, name):
    T = z.shape[0]
    D = z.shape[1] // 3
    tc = LANES
    nb = D // tc
    per = cw.shape[2] // tc

    def body(b_ref, c_ref, h_ref, w_ref, dv_ref, dz_ref, dw_ref, stage, sems):
        j = pl.program_id(0)
        slot = j % 2

        def slab(p, jj, s):
            col = pl.multiple_of((p * nb + jj) * tc, tc)
            return pltpu.make_async_copy(stage.at[s, p], dz_ref.at[:, pl.ds(col, tc)], sems.at[s, p])

        @pl.when(j >= 2)
        def _():
            for p in range(3):
                slab(p, j - 2, slot).wait()

        c = c_ref[...].astype(F32)
        h = h_ref[...].astype(F32)
        u = c * h
        u1 = _shift_down(u, 1)
        u2 = _shift_down(u, 2)
        w = w_ref[...]
        dvv = dv_ref[...].astype(F32)
        dconv = dvv * b_ref[...].astype(F32)
        du = w[2:3] * dconv + w[1:2] * _shift_up(dconv, 1) + w[0:1] * _shift_up(dconv, 2)
        rows = lax.broadcasted_iota(jnp.int32, (8, tc), 0)
        dw_ref[...] = jnp.where(rows == 0, jnp.sum(dconv * u2, axis=0, keepdims=True),
                                jnp.where(rows == 1, jnp.sum(dconv * u1, axis=0, keepdims=True),
                                          jnp.where(rows == 2, jnp.sum(dconv * u, axis=0, keepdims=True), 0.0)))
        stage[slot, 0] = (dvv * (w[2:3] * u + w[1:2] * u1 + w[0:1] * u2)).astype(BF)
        stage[slot, 1] = (du * h).astype(BF)
        stage[slot, 2] = (du * c).astype(BF)
        for p in range(3):
            slab(p, j, slot).start()

        @pl.when(j == nb - 1)
        def _():
            for p in range(3):
                slab(p, j, slot).wait()
            if nb > 1:
                for p in range(3):
                    slab(p, j - 1, 1 - slot).wait()

    return pl.pallas_call(
        body, name=name,
        grid=(nb,),
        in_specs=[
            pl.BlockSpec((T, tc), lambda j: (0, j)),
            pl.BlockSpec((T, tc), lambda j: (0, nb + j)),
            pl.BlockSpec((T, tc), lambda j: (0, 2 * nb + j)),
            pl.BlockSpec((None, 8, tc), lambda j: (j // per, 0, j % per)),
            pl.BlockSpec((T, tc), lambda j: (0, j)),
        ],
        out_specs=[ANY, pl.BlockSpec((8, tc), lambda j: (0, j))],
        out_shape=[jax.ShapeDtypeStruct((T, 3 * D), BF), jax.ShapeDtypeStruct((8, D), F32)],
        scratch_shapes=[pltpu.VMEM((2, 3, T, tc), BF), pltpu.SemaphoreType.DMA((2, 3))],
        compiler_params=_cp("arbitrary"),
    )(z, z, z, cw, dv)


def _strided(base, count, d):
    return pl.ds(base, count, stride=d) if d > 1 else pl.ds(pl.multiple_of(base, BAND), count)


def _fill_band_bias(bias):
    qi = lax.broadcasted_iota(jnp.int32, (2 * BAND, 2 * BAND), 0) % BAND
    kj = lax.broadcasted_iota(jnp.int32, (2 * BAND, 2 * BAND), 1)
    dist = qi + BAND - kj
    band = (dist >= 0) & (dist <= BAND)
    bias[0] = jnp.where(band & (kj >= BAND), 0.0, NEG_INF)
    bias[1] = jnp.where(band, 0.0, NEG_INF)


def _attn_block_rows(T):
    return min(ATTN_BLOCK_ROWS, T)


def _head_mask():
    lane = lax.broadcasted_iota(jnp.int32, (2 * BAND, LANES), 1)
    row = lax.broadcasted_iota(jnp.int32, (2 * BAND, LANES), 0)
    return (lane < HEAD_DIM) == (row < BAND)


def _attn_fwd(q_all, kv_all, *, name):
    T = q_all.shape[0]
    NB = len(BRANCHES)
    Dm = q_all.shape[1] // NB
    HP = Dm // LANES
    R = _attn_block_rows(T)
    units = R // BAND
    dmax = max(d for _, d in BRANCHES)

    def body(*refs):
        ins = refs[:5 * NB]
        o_ref, l_ref, kbuf, vbuf, o_s, l_s, bias = refs[5 * NB:]
        n = pl.program_id(0)
        _fill_band_bias(bias)
        hm = _head_mask()
        low = lax.broadcasted_iota(jnp.int32, (BAND, LANES), 1) < HEAD_DIM

        for g, (_, d) in enumerate(BRANCHES):
            q_ref, kp_ref, kc_ref, vp_ref, vc_ref = ins[5 * g:5 * g + 5]
            pr = BAND * d
            kbuf[0:pr, :] = kp_ref[...]
            kbuf[pr:pr + R, :] = kc_ref[...]
            vbuf[0:pr, :] = vp_ref[...]
            vbuf[pr:pr + R, :] = vc_ref[...]

            def unit(u, carry, g=g, d=d, pr=pr, q_ref=q_ref):
                sub = u // d
                base = sub * pr + (u - sub * d)
                q = q_ref[_strided(base, BAND, d), :]
                q2 = jnp.where(hm, jnp.concatenate([q, q], axis=0), 0.0).astype(BF)
                k2 = kbuf[_strided(base, 2 * BAND, d), :].astype(BF)
                v2 = vbuf[_strided(base, 2 * BAND, d), :].astype(BF)
                s = lax.dot_general(q2, k2, (((1,), (1,)), ((), ())), preferred_element_type=F32)
                s = s + bias[((n > 0) | (sub > 0)).astype(jnp.int32)]
                m = jnp.max(s, axis=-1, keepdims=True)
                p = jnp.exp(s - m)
                l = jnp.sum(p, axis=-1, keepdims=True)
                pv = jnp.dot(p.astype(BF), v2, preferred_element_type=F32) / l
                lse = m + jnp.log(l)
                o_s[g, _strided(base, BAND, d), :] = jnp.where(low, pv[:BAND], pv[BAND:])
                l_s[g, _strided(base, BAND, d), :] = jnp.where(low, lse[:BAND], lse[BAND:])
                return carry

            lax.fori_loop(0, units, unit, 0, unroll=4)

        def merge(i, carry):
            sl = pl.ds(pl.multiple_of(i * BAND, BAND), BAND)
            lv = [l_s[g, sl, :] for g in range(NB)]
            m = functools.reduce(jnp.maximum, lv)
            e = [jnp.exp(v - m) for v in lv]
            tot = functools.reduce(jnp.add, e)
            o_ref[sl, :] = functools.reduce(jnp.add, [(e[g] / tot) * o_s[g, sl, :] for g in range(NB)])
            l_ref[sl, :] = m + jnp.log(tot)
            return carry

        lax.fori_loop(0, units, merge, 0)

    in_specs, args = [], []
    for g, (_, d) in enumerate(BRANCHES):
        per = R // (BAND * d)
        for col, rows, idx in (
                (g * HP, R, lambda n, hp: n),
                (g * HP, BAND * d, lambda n, hp, per=per: jnp.maximum(n * per - 1, 0)),
                (g * HP, R, lambda n, hp: n),
                ((NB + g) * HP, BAND * d, lambda n, hp, per=per: jnp.maximum(n * per - 1, 0)),
                ((NB + g) * HP, R, lambda n, hp: n)):
            in_specs.append(pl.BlockSpec((rows, LANES), lambda n, hp, col=col, idx=idx: (idx(n, hp), col + hp)))
        args += [q_all, kv_all, kv_all, kv_all, kv_all]
    out = pl.BlockSpec((R, LANES), lambda n, hp: (n, hp))
    return pl.pallas_call(
        body, name=name,
        grid=(T // R, HP),
        in_specs=in_specs,
        out_specs=[out, out],
        out_shape=[jax.ShapeDtypeStruct((T, Dm), F32), jax.ShapeDtypeStruct((T, Dm), F32)],
        scratch_shapes=[pltpu.VMEM((BAND * dmax + R, LANES), F32), pltpu.VMEM((BAND * dmax + R, LANES), F32),
                        pltpu.VMEM((NB, R, LANES), F32), pltpu.VMEM((NB, R, LANES), F32),
                        pltpu.VMEM((2, 2 * BAND, 2 * BAND), F32)],
        compiler_params=_cp("parallel", "parallel"),
    )(*args)


def _attn_bwd(q_all, kv_all, do, o, lse, cos2, ss2, g, d, prev, *, name, dep=None):
    T = q_all.shape[0]
    NB = len(BRANCHES)
    Dm = q_all.shape[1] // NB
    HP = Dm // LANES
    R = _attn_block_rows(T)
    nblk = T // R
    units = R // BAND
    pr = BAND * d
    per = R // pr
    scale = HEAD_DIM ** -0.5

    def rope_bwd(t, cosv, ssv, first):
        return t * cosv - _rot_half(t, first) * ssv

    def body(q_ref, kp_ref, kc_ref, vp_ref, vc_ref, do_ref, o_ref, l_ref, cos_ref, ss_ref, *rest):
        dq_ref, dk_ref, dv_ref, kbuf, vbuf, dkbuf, dvbuf, ck_car, cv_car, bias = rest[-10:]
        i = pl.program_id(1)
        n = nblk - 1 - i
        _fill_band_bias(bias)
        first = _first_half_mask(BAND)
        hm = _head_mask()
        low = lax.broadcasted_iota(jnp.int32, (BAND, LANES), 1) < HEAD_DIM

        kbuf[0:pr, :] = kp_ref[...]
        kbuf[pr:pr + R, :] = kc_ref[...]
        vbuf[0:pr, :] = vp_ref[...]
        vbuf[pr:pr + R, :] = vc_ref[...]
        dkbuf[...] = jnp.zeros_like(dkbuf)
        dvbuf[...] = jnp.zeros_like(dvbuf)

        @pl.when(i > 0)
        def _():
            dkbuf[R:R + pr, :] = ck_car[...]
            dvbuf[R:R + pr, :] = cv_car[...]

        def unit(u, carry):
            sub = u // d
            base = sub * pr + (u - sub * d)
            sl = _strided(base, BAND, d)
            sl2 = _strided(base, 2 * BAND, d)
            q = q_ref[sl, :]
            dov = do_ref[sl, :]
            ov = o_ref[sl, :]
            lv = l_ref[sl, :]
            q2 = jnp.where(hm, jnp.concatenate([q, q], axis=0), 0.0).astype(BF)
            do2 = jnp.where(hm, jnp.concatenate([dov, dov], axis=0), 0.0)
            oo = dov * ov
            delta = jnp.sum(jnp.where(hm, jnp.concatenate([oo, oo], axis=0), 0.0), axis=-1, keepdims=True)
            lse2 = jnp.concatenate([lv[:, 0:1], lv[:, HEAD_DIM:HEAD_DIM + 1]], axis=0)
            do2 = do2.astype(BF)
            k2 = kbuf[sl2, :].astype(BF)
            v2 = vbuf[sl2, :].astype(BF)
            s = lax.dot_general(q2, k2, (((1,), (1,)), ((), ())), preferred_element_type=F32)
            p = jnp.exp(s + bias[((n > 0) | (sub > 0)).astype(jnp.int32)] - lse2)
            dp = lax.dot_general(do2, v2, (((1,), (1,)), ((), ())), preferred_element_type=F32)
            ds = (p * (dp - delta)).astype(BF)
            dq2 = jnp.dot(ds, k2, preferred_element_type=F32)
            dq = jnp.where(low, dq2[:BAND], dq2[BAND:])
            dq_ref[sl, :] = rope_bwd(dq, cos_ref[sl, :], ss_ref[sl, :], first) * scale
            dkbuf[sl2, :] += jnp.dot(ds.T, q2, preferred_element_type=F32)
            dvbuf[sl2, :] += jnp.dot(p.astype(BF).T, do2, preferred_element_type=F32)
            return carry

        lax.fori_loop(0, units, unit, 0, unroll=4)

        dk_ref[...] = rope_bwd(dkbuf[pr:pr + R, :], cos_ref[...], ss_ref[...], _first_half_mask(R))
        dv_ref[...] = dvbuf[pr:pr + R, :]
        ck_car[...] = dkbuf[0:pr, :]
        cv_car[...] = dvbuf[0:pr, :]

    blk = (R, LANES)
    pblk = (pr, LANES)
    cur = lambda hp, i: nblk - 1 - i
    prv = lambda hp, i: jnp.maximum((nblk - 1 - i) * per - 1, 0)
    in_specs = [
        pl.BlockSpec(blk, lambda hp, i: (cur(hp, i), g * HP + hp)),
        pl.BlockSpec(pblk, lambda hp, i: (prv(hp, i), g * HP + hp)),
        pl.BlockSpec(blk, lambda hp, i: (cur(hp, i), g * HP + hp)),
        pl.BlockSpec(pblk, lambda hp, i: (prv(hp, i), (NB + g) * HP + hp)),
        pl.BlockSpec(blk, lambda hp, i: (cur(hp, i), (NB + g) * HP + hp)),
        pl.BlockSpec(blk, lambda hp, i: (cur(hp, i), hp)),
        pl.BlockSpec(blk, lambda hp, i: (cur(hp, i), hp)),
        pl.BlockSpec(blk, lambda hp, i: (cur(hp, i), hp)),
        pl.BlockSpec(blk, lambda hp, i: (cur(hp, i), 0)),
        pl.BlockSpec(blk, lambda hp, i: (cur(hp, i), 0)),
    ]
    args = [q_all, kv_all, kv_all, kv_all, kv_all, do, o, lse, cos2, ss2]
    if dep is not None:
        in_specs.append(ANY)
        args.append(dep)
    aliases = {}
    if prev is not None:
        in_specs += [ANY, ANY, ANY]
        aliases = {len(args): 0, len(args) + 1: 1, len(args) + 2: 2}
        args += list(prev)
    wide = jax.ShapeDtypeStruct((T, NB * Dm), F32)
    out = pl.BlockSpec(blk, lambda hp, i: (cur(hp, i), g * HP + hp))
    return pl.pallas_call(
        body, name=name,
        grid=(HP, nblk),
        in_specs=in_specs,
        out_specs=[out, out, out],
        out_shape=[wide, wide, wide],
        scratch_shapes=[pltpu.VMEM((pr + R, LANES), F32), pltpu.VMEM((pr + R, LANES), F32),
                        pltpu.VMEM((pr + R, LANES), F32), pltpu.VMEM((pr + R, LANES), F32),
                        pltpu.VMEM(pblk, F32), pltpu.VMEM(pblk, F32),
                        pltpu.VMEM((2, 2 * BAND, 2 * BAND), F32)],
        input_output_aliases=aliases,
        compiler_params=_cp("arbitrary", "arbitrary"),
    )(*args)


def _postnorm_bwd(dh, y, g_ref_val):
    r = lax.rsqrt(jnp.mean(y * y, axis=-1, keepdims=True) + RMS_EPS)
    yn = y * r
    dyn = dh * g_ref_val
    dy = r * (dyn - yn * jnp.mean(dyn * yn, axis=-1, keepdims=True))
    return dy, yn


def _after(body, n_in, dep):
    if dep is None:
        return body
    return lambda *refs: body(*refs[:n_in], *refs[n_in + 1:])


def _dep_spec(dep):
    return [] if dep is None else [ANY]


def _dep_arg(dep):
    return [] if dep is None else [dep]


def _postnorm_bwd_matmul(dh, y, gain, w3, widx, *, name, da_dtype, dep=None):
    T, D = dh.shape
    K = w3.shape[1]
    tm = min(ROW_BLOCK_WIDE, T)

    def body(dh_ref, y_ref, g_ref, w_ref, dy_ref, da_ref, dg_ref):
        i = pl.program_id(0)

        @pl.when(i == 0)
        def _():
            dg_ref[...] = jnp.zeros_like(dg_ref)

        dhv = dh_ref[...]
        dy, yn = _postnorm_bwd(dhv, y_ref[...].astype(F32), g_ref[...])
        dg_ref[...] += jnp.sum(dhv * yn, axis=0, keepdims=True)
        dyb = dy.astype(BF)
        dy_ref[...] = dyb
        da = lax.dot_general(dyb, w_ref[...], (((1,), (1,)), ((), ())), preferred_element_type=F32)
        da_ref[...] = da.astype(da_dtype)

    return pl.pallas_call(
        _after(body, 4, dep), name=name,
        grid=(T // tm,),
        in_specs=[
            pl.BlockSpec((tm, D), lambda i: (i, 0)),
            pl.BlockSpec((tm, D), lambda i: (i, 0)),
            pl.BlockSpec((1, D), lambda i: (0, 0)),
            pl.BlockSpec((None, K, D), lambda i: (widx, 0, 0)),
        ] + _dep_spec(dep),
        out_specs=[pl.BlockSpec((tm, D), lambda i: (i, 0)),
                   pl.BlockSpec((tm, K), lambda i: (i, 0)),
                   pl.BlockSpec((1, D), lambda i: (0, 0))],
        out_shape=[jax.ShapeDtypeStruct((T, D), BF), jax.ShapeDtypeStruct((T, K), da_dtype),
                   jax.ShapeDtypeStruct((1, D), F32)],
        compiler_params=_cp("arbitrary"),
    )(dh, y, gain, w3, *_dep_arg(dep))


def _postnorm_bwd_swiglu(dh, y, gain, wd3, layer, g, u, *, name, dep=None):
    T, D = dh.shape
    F = wd3.shape[1]
    nf = F // 2
    tm = min(ROW_BLOCK, T)

    def body(dh_ref, y_ref, g_ref, w_ref, gg_ref, uu_ref, dy_ref, dgo_ref, duo_ref, dgain_ref, dys):
        i = pl.program_id(0)
        j = pl.program_id(1)

        @pl.when((i == 0) & (j == 0))
        def _():
            dgain_ref[...] = jnp.zeros_like(dgain_ref)

        @pl.when(j == 0)
        def _():
            dhv = dh_ref[...]
            dy, yn = _postnorm_bwd(dhv, y_ref[...].astype(F32), g_ref[...])
            dgain_ref[...] += jnp.sum(dhv * yn, axis=0, keepdims=True)
            dyb = dy.astype(BF)
            dys[...] = dyb
            dy_ref[...] = dyb

        for c0, cw in _col_chunks(nf):
            cols = slice(c0, c0 + cw)
            da = lax.dot_general(dys[...], w_ref[cols, :], (((1,), (1,)), ((), ())), preferred_element_type=F32)
            gv = gg_ref[:, cols].astype(F32)
            uv = uu_ref[:, cols].astype(F32)
            sg = _sigmoid(gv)
            silu = gv * sg
            dgo_ref[:, cols] = (da * uv * (sg + silu * (1.0 - sg))).astype(BF)
            duo_ref[:, cols] = (da * silu).astype(BF)

    act = jax.ShapeDtypeStruct((T, F), BF)
    return pl.pallas_call(
        _after(body, 6, dep), name=name,
        grid=(T // tm, 2),
        in_specs=[
            pl.BlockSpec((tm, D), lambda i, j: (i, 0)),
            pl.BlockSpec((tm, D), lambda i, j: (i, 0)),
            pl.BlockSpec((1, D), lambda i, j: (0, 0)),
            pl.BlockSpec((None, nf, D), lambda i, j: (layer, j, 0)),
            pl.BlockSpec((tm, nf), lambda i, j: (i, j)),
            pl.BlockSpec((tm, nf), lambda i, j: (i, j)),
        ] + _dep_spec(dep),
        out_specs=[pl.BlockSpec((tm, D), lambda i, j: (i, 0)),
                   pl.BlockSpec((tm, nf), lambda i, j: (i, j)),
                   pl.BlockSpec((tm, nf), lambda i, j: (i, j)),
                   pl.BlockSpec((1, D), lambda i, j: (0, 0))],
        out_shape=[jax.ShapeDtypeStruct((T, D), BF), act, act, jax.ShapeDtypeStruct((1, D), F32)],
        scratch_shapes=[pltpu.VMEM((tm, D), BF)],
        compiler_params=_cp("arbitrary", "arbitrary"),
    )(dh, y, gain, wd3, g, u, *_dep_arg(dep))


def _matmul_prenorm_bwd(dzs, wg, layer, h, gain, dh_in, *, name):
    T, D = h.shape
    n = wg.shape[2]
    pair = len(dzs) == 2
    tm = min(ROW_BLOCK if pair else ROW_BLOCK_WIDE, T)
    nj = N_CHIPS // 2 if pair else N_CHIPS

    def body(*refs):
        dz_refs = refs[:len(dzs)]
        w_refs = refs[len(dzs):2 * len(dzs)]
        h_ref, g_ref, dhi_ref, dh_ref, dg_ref, acc = refs[2 * len(dzs):]
        i = pl.program_id(0)
        j = pl.program_id(1)

        @pl.when((i == 0) & (j == 0))
        def _():
            dg_ref[...] = jnp.zeros_like(dg_ref)

        part = None
        for dz_ref, w_ref in zip(dz_refs, w_refs):
            t = lax.dot_general(dz_ref[...].astype(BF), w_ref[...], (((1,), (1,)), ((), ())),
                                preferred_element_type=F32)
            part = t if part is None else part + t

        @pl.when(j == 0)
        def _():
            acc[...] = part

        @pl.when(j > 0)
        def _():
            acc[...] += part

        @pl.when(j == nj - 1)
        def _():
            dhn = acc[...]
            hv = h_ref[...]
            r = lax.rsqrt(jnp.mean(hv * hv, axis=-1, keepdims=True) + RMS_EPS)
            xh = hv * r
            dg_ref[...] += jnp.sum(dhn * xh, axis=0, keepdims=True)
            dxn = dhn * g_ref[...]
            dh_ref[...] = dhi_ref[...] + r * (dxn - xh * jnp.mean(dxn * xh, axis=-1, keepdims=True))

    in_specs = [pl.BlockSpec((tm, n), lambda i, j: (i, j)) for _ in dzs]
    if pair:
        in_specs += [pl.BlockSpec((None, D, n), lambda i, j: (j, layer, 0)),
                     pl.BlockSpec((None, D, n), lambda i, j: (j + nj, layer, 0))]
    else:
        in_specs += [pl.BlockSpec((None, D, n), lambda i, j: (j, layer, 0))]
    in_specs += [pl.BlockSpec((tm, D), lambda i, j: (i, 0)),
                 pl.BlockSpec((1, D), lambda i, j: (0, 0)),
                 pl.BlockSpec((tm, D), lambda i, j: (i, 0))]
    return pl.pallas_call(
        body, name=name,
        grid=(T // tm, nj),
        in_specs=in_specs,
        out_specs=[pl.BlockSpec((tm, D), lambda i, j: (i, 0)), pl.BlockSpec((1, D), lambda i, j: (0, 0))],
        out_shape=[jax.ShapeDtypeStruct((T, D), F32), jax.ShapeDtypeStruct((1, D), F32)],
        scratch_shapes=[pltpu.VMEM((tm, D), F32)],
        compiler_params=_cp("arbitrary", "arbitrary"),
    )(*dzs, *([wg] * len(dzs)), h, gain, dh_in)


def _grad_matmul(a, b, out_shape3, tme, tne, out_index, prev, *, name):
    T, M = a.shape
    N = b.shape[1]
    tk = min(ROW_BLOCK_WIDE, T)
    nk = T // tk

    def body(a_ref, b_ref, *rest):
        o_ref, acc = rest[-2:]
        k = pl.program_id(2)
        part = jnp.dot(a_ref[...].astype(BF).T, b_ref[...].astype(BF), preferred_element_type=F32)

        @pl.when(k == 0)
        def _():
            acc[...] = part

        @pl.when(k > 0)
        def _():
            acc[...] += part

        @pl.when(k == nk - 1)
        def _():
            o_ref[...] = acc[...].astype(BF)

    in_specs = [pl.BlockSpec((tk, tme), lambda i, j, k: (k, i)),
                pl.BlockSpec((tk, tne), lambda i, j, k: (k, j))]
    args = [a, b]
    aliases = {}
    if prev is not None:
        in_specs.append(ANY)
        args.append(prev)
        aliases = {2: 0}
    return pl.pallas_call(
        body, name=name,
        grid=(M // tme, N // tne, nk),
        in_specs=in_specs,
        out_specs=pl.BlockSpec((None, tme, tne), lambda i, j, k: out_index(i, j)),
        out_shape=jax.ShapeDtypeStruct(out_shape3, BF),
        scratch_shapes=[pltpu.VMEM((tme, tne), F32)],
        input_output_aliases=aliases,
        compiler_params=_cp("parallel", "parallel", "arbitrary"),
    )(*args)


def _row_tile(R):
    fit = [t for t in range(16, min(R, 512) + 1, 16) if R % t == 0]
    return max(fit) if fit else R


def _cast_place(w2d, where, dtype, *, name, dep=None):
    R, C = w2d.shape
    tr = _row_tile(R)

    def body(s_ref, w_ref, o_ref):
        o_ref[...] = w_ref[...].astype(o_ref.dtype)

    return pl.pallas_call(
        _after(body, 2, dep), name=name,
        grid_spec=pltpu.PrefetchScalarGridSpec(
            num_scalar_prefetch=1, grid=(R // tr,),
            in_specs=[pl.BlockSpec((tr, C), lambda i, s: (i, 0))] + _dep_spec(dep),
            out_specs=pl.BlockSpec((None, tr, C), lambda i, s: (s[0], i, 0))),
        out_shape=jax.ShapeDtypeStruct((N_CHIPS, R, C), dtype),
        compiler_params=_cp("arbitrary"),
    )(where, w2d, *_dep_arg(dep))


def _pair_sum(dw, theirs, where, *, name):
    G, rh, C = theirs.shape
    tr = _row_tile(rh)
    nr = rh // tr

    def body(s_ref, a_ref, b_ref, o_ref):
        o_ref[...] = (a_ref[...].astype(F32) + b_ref[...].astype(F32)).astype(BF)

    mine = pl.BlockSpec((None, tr, C), lambda g, i, s: (g, s[1] * nr + i, 0))
    spec = pl.BlockSpec((None, tr, C), lambda g, i, s: (g, i, 0))
    return pl.pallas_call(
        body, name=name,
        grid_spec=pltpu.PrefetchScalarGridSpec(
            num_scalar_prefetch=1, grid=(G, nr), in_specs=[mine, spec], out_specs=spec),
        out_shape=jax.ShapeDtypeStruct((G, rh, C), BF),
        compiler_params=_cp("arbitrary", "arbitrary"),
    )(where, dw, theirs)


def _chip_sum(landed, parts, where, total_rows, row_off, prev, *, name):
    G, rh, C = landed.shape
    tr = _row_tile(rh)
    nr = rh // tr
    base = row_off // tr

    def body(s_ref, l_ref, p_ref, *rest):
        o_ref = rest[-1]
        for j in range(G):
            def own(j=j):
                v = p_ref[...].astype(F32)
                o_ref[...] = v if j == 0 else o_ref[...] + v

            def other(j=j):
                v = l_ref[j].astype(F32)
                o_ref[...] = v if j == 0 else o_ref[...] + v

            pl.when(s_ref[0] == j)(own)
            pl.when(s_ref[0] != j)(other)

    in_specs = [pl.BlockSpec((G, tr, C), lambda i, s: (0, i, 0)),
                pl.BlockSpec((None, tr, C), lambda i, s: (s[0], i, 0))]
    args = [where, landed, parts]
    aliases = {}
    if prev is not None:
        in_specs.append(ANY)
        args.append(prev)
        aliases = {3: 0}
    return pl.pallas_call(
        body, name=name,
        grid_spec=pltpu.PrefetchScalarGridSpec(
            num_scalar_prefetch=1, grid=(nr,),
            in_specs=in_specs,
            out_specs=pl.BlockSpec((tr, C), lambda i, s: (base + s[1] * nr + i, 0))),
        out_shape=jax.ShapeDtypeStruct((total_rows, C), F32),
        input_output_aliases=aliases,
        compiler_params=_cp("arbitrary"),
    )(*args)


def _adamw(w, g, m, v, *, name):
    R, C = w.shape
    tr = _row_tile(R)

    def body(w_ref, g_ref, m_ref, v_ref, d_ref, mo_ref, vo_ref):
        gv = g_ref[...]
        mn = ADAM_B1 * m_ref[...] + (1.0 - ADAM_B1) * gv
        vn = ADAM_B2 * v_ref[...] + (1.0 - ADAM_B2) * jnp.square(gv)
        m_hat = mn / (1.0 - ADAM_B1 ** ADAM_STEP)
        v_hat = vn / (1.0 - ADAM_B2 ** ADAM_STEP)
        d_ref[...] = -ADAM_LR * (m_hat / (jnp.sqrt(v_hat) + ADAM_EPS) + ADAM_WD * w_ref[...])
        mo_ref[...] = mn
        vo_ref[...] = vn

    spec = pl.BlockSpec((tr, C), lambda i: (i, 0))
    shp = jax.ShapeDtypeStruct((R, C), F32)
    return pl.pallas_call(
        body, name=name, grid=(R // tr,), in_specs=[spec] * 4, out_specs=[spec] * 3,
        out_shape=[shp, shp, shp], compiler_params=_cp("parallel"),
    )(w, g, m, v)


def _place():
    x = lax.axis_index("x")
    y = lax.axis_index("y")
    c = lax.axis_index("c")
    chips = [(1 - x, y), (x, 1 - y), (1 - x, 1 - y)]
    return x, y, c, chips


def _chunk_rows(rows, row_bytes, align):
    if rows <= align:
        return rows
    cands = [r for r in range(align, rows + 1, align) if rows % r == 0]
    fit = [r for r in cands if r * row_bytes <= DMA_CHUNK_BYTES]
    return max(fit) if fit else min(cands)


def _row_align(dtype):
    return 8 * (4 // jnp.dtype(dtype).itemsize)


def _start_chunks(make, rows, rc):
    for r0 in range(0, rows, rc):
        make(r0, rc).start()


def _piece_rows(ref, piece, j, h, r0=0, n=None):
    _, lead, off, rows = piece
    rh = rows // 2
    n = rh if n is None else n
    if lead is not None:
        return ref.at[lead, j, pl.ds(h * rh + r0, n)]
    return ref.at[j, pl.ds(off + h * rh + r0, n)]


def _piece_chunk(arr, piece):
    rh = piece[3] // 2
    return rh, _chunk_rows(rh, arr.shape[-1] * arr.dtype.itemsize, _row_align(arr.dtype))


def _ag_start(arrays, taps, groups, *, name):
    na = len(arrays)
    ng = len(groups)
    nt = 0 if taps is None else 1
    n_sem = [3 * len(grp) + (3 if nt and g == 0 else 0) for g, grp in enumerate(groups)]

    def body(*refs):
        ins = refs[:na]
        taps_ref = refs[na] if nt else None
        sems = refs[na + nt:na + nt + 2 * ng]
        token = refs[-1]
        token[...] = jnp.zeros_like(token)
        x, y, c, chips = _place()
        myj = 2 * x + y
        for g, grp in enumerate(groups):
            ssem, rsem = sems[2 * g], sems[2 * g + 1]
            for idx, piece in enumerate(grp):
                ref = ins[piece[0]]
                rh, rc = _piece_chunk(arrays[piece[0]], piece)
                for k, (px, py) in enumerate(chips):
                    def send(r0, n, ref=ref, piece=piece, idx=idx, k=k, px=px, py=py, ssem=ssem, rsem=rsem):
                        part = _piece_rows(ref, piece, myj, c, r0, n)
                        return pltpu.make_async_remote_copy(
                            src_ref=part, dst_ref=part, send_sem=ssem.at[3 * idx + k], recv_sem=rsem.at[3 * idx + k],
                            device_id=(px, py, c), device_id_type=MESH)
                    _start_chunks(send, rh, rc)
            if nt and g == 0:
                for k, (px, py) in enumerate(chips):
                    pltpu.make_async_remote_copy(
                        src_ref=taps_ref.at[myj], dst_ref=taps_ref.at[myj],
                        send_sem=ssem.at[3 * len(grp) + k], recv_sem=rsem.at[3 * len(grp) + k],
                        device_id=(px, py, c), device_id_type=MESH).start()

    sem_shapes = []
    for n in n_sem:
        sem_shapes += [pltpu.SemaphoreType.DMA((n,)), pltpu.SemaphoreType.DMA((n,))]
    ops = list(arrays) + ([taps] if nt else [])
    bufs = [pltpu.HBM(a.shape, a.dtype) for a in ops]
    outs = pl.pallas_call(
        body, name=name,
        out_shape=(*sem_shapes, *bufs, jax.ShapeDtypeStruct((8, LANES), F32)),
        in_specs=[HBM] * (na + nt),
        out_specs=(*([SEM] * (2 * ng)), *([HBM] * (na + nt)), pl.BlockSpec(memory_space=pltpu.VMEM)),
        input_output_aliases={i: 2 * ng + i for i in range(na + nt)},
        compiler_params=pltpu.CompilerParams(has_side_effects=EFFECT),
    )(*[_in_hbm(a) for a in ops])
    sems = [(outs[2 * g], outs[2 * g + 1]) for g in range(ng)]
    return sems, list(outs[2 * ng:2 * ng + na]), (outs[2 * ng + na] if nt else None), outs[-1]


def _ag_wait(sems, vals, taps, group, after, *, name):
    nv = len(vals)
    extra = ([taps] if taps is not None else [])
    nb = nv + len(extra)

    def body(*refs):
        bufs = refs[:nb]
        ssem, rsem = refs[nb], refs[nb + 1]
        x, y, c, chips = _place()
        for idx, piece in enumerate(group):
            for k, (px, py) in enumerate(chips):
                got = _piece_rows(bufs[piece[0]], piece, 2 * px + py, c)
                cp = pltpu.make_async_remote_copy(
                    src_ref=got, dst_ref=got, send_sem=ssem.at[3 * idx + k], recv_sem=rsem.at[3 * idx + k],
                    device_id=(px, py, c), device_id_type=MESH)
                cp.wait_send()
                cp.wait_recv()
        if taps is not None:
            for k, (px, py) in enumerate(chips):
                got = bufs[nv].at[2 * px + py]
                cp = pltpu.make_async_remote_copy(
                    src_ref=got, dst_ref=got, send_sem=ssem.at[3 * len(group) + k],
                    recv_sem=rsem.at[3 * len(group) + k], device_id=(px, py, c), device_id_type=MESH)
                cp.wait_send()
                cp.wait_recv()

    ops = list(vals) + extra
    shapes = [pltpu.HBM(a.shape, a.dtype) for a in ops]
    outs = pl.pallas_call(
        body, name=name,
        out_shape=tuple(shapes),
        in_specs=[HBM] * nb + [SEM, SEM] + _dep_spec(after),
        out_specs=[HBM] * nb,
        input_output_aliases={i: i for i in range(nb)},
        compiler_params=pltpu.CompilerParams(has_side_effects=EFFECT),
    )(*ops, sems[0], sems[1], *_dep_arg(after))
    return list(outs[:nv]), (outs[nv] if taps is not None else None)


def _ag_forward(vals, group, *, name):
    nv = len(vals)
    npc = len(group)

    def body(*refs):
        bufs = refs[nv:2 * nv]
        fsem, gsem = refs[2 * nv:]
        x, y, c, chips = _place()
        sib = (x, y, 1 - c)
        sent = []
        for idx, piece in enumerate(group):
            rh, rc = _piece_chunk(vals[piece[0]], piece)
            for k, (px, py) in enumerate(chips):
                def fwd(r0, n, piece=piece, idx=idx, k=k, pj=2 * px + py):
                    part = _piece_rows(bufs[piece[0]], piece, pj, c, r0, n)
                    return pltpu.make_async_remote_copy(
                        src_ref=part, dst_ref=part, send_sem=fsem.at[3 * idx + k], recv_sem=gsem.at[3 * idx + k],
                        device_id=sib, device_id_type=MESH)
                _start_chunks(fwd, rh, rc)
                sent.append(fwd(0, rh))
        for idx, piece in enumerate(group):
            for k, (px, py) in enumerate(chips):
                theirs = _piece_rows(bufs[piece[0]], piece, 2 * px + py, 1 - c)
                pltpu.make_async_remote_copy(
                    src_ref=theirs, dst_ref=theirs, send_sem=fsem.at[3 * idx + k], recv_sem=gsem.at[3 * idx + k],
                    device_id=sib, device_id_type=MESH).wait_recv()
        for cp in sent:
            cp.wait_send()

    return pl.pallas_call(
        body, name=name,
        in_specs=[ANY] * nv, out_specs=[ANY] * nv,
        out_shape=[jax.ShapeDtypeStruct(v.shape, v.dtype) for v in vals],
        input_output_aliases={i: i for i in range(nv)},
        scratch_shapes=[pltpu.SemaphoreType.DMA((3 * npc,)), pltpu.SemaphoreType.DMA((3 * npc,))],
    )(*vals)


def _sibling_swap(dws, *, name):
    nm = len(dws)
    shapes = [jax.ShapeDtypeStruct((dw.shape[0], dw.shape[1] // 2, dw.shape[2]), dw.dtype) for dw in dws]

    def body(*refs):
        ins = refs[:nm]
        theirs = refs[nm:2 * nm]
        ssem, rsem = refs[2 * nm:]
        x, y, c, _ = _place()
        sib = (x, y, 1 - c)
        cps = []
        for m in range(nm):
            G, rh, cols = shapes[m].shape
            rc = _chunk_rows(rh, cols * shapes[m].dtype.itemsize, _row_align(shapes[m].dtype))
            for j in range(G):
                _start_chunks(lambda r0, n, m=m, j=j, rh=rh: pltpu.make_async_remote_copy(
                    src_ref=ins[m].at[j, pl.ds((1 - c) * rh + r0, n)],
                    dst_ref=theirs[m].at[j, pl.ds(r0, n)], send_sem=ssem.at[m], recv_sem=rsem.at[m],
                    device_id=sib, device_id_type=MESH), rh, rc)
            cps.append(pltpu.make_async_remote_copy(
                src_ref=ins[m].at[:, pl.ds((1 - c) * rh, rh), :], dst_ref=theirs[m],
                send_sem=ssem.at[m], recv_sem=rsem.at[m], device_id=sib, device_id_type=MESH))
        for cp in cps:
            cp.wait()

    return pl.pallas_call(
        body, name=name,
        in_specs=[ANY] * nm, out_specs=[ANY] * nm, out_shape=shapes,
        scratch_shapes=[pltpu.SemaphoreType.DMA((nm,)), pltpu.SemaphoreType.DMA((nm,))],
    )(*dws)


HBM = pl.BlockSpec(memory_space=pltpu.HBM)
SEM = pl.BlockSpec(memory_space=pltpu.SEMAPHORE)
EFFECT = pltpu.SideEffectType.DATAFLOW_SIDE_EFFECTING


def _in_hbm(a):
    return pltpu.with_memory_space_constraint(a, pltpu.HBM)


def _exchange_start(parts, *, name):
    nm = len(parts)

    def body(*refs):
        ins = refs[:nm]
        lands = refs[nm:2 * nm]
        ssem, rsem = refs[2 * nm:2 * nm + 2]
        token = refs[-1]
        x, y, c, chips = _place()
        myj = 2 * x + y
        for m in range(nm):
            _, rh, cols = parts[m].shape
            rc = _chunk_rows(rh, cols * parts[m].dtype.itemsize, _row_align(parts[m].dtype))
            for k, (px, py) in enumerate(chips):
                _start_chunks(lambda r0, n, m=m, k=k, px=px, py=py: pltpu.make_async_remote_copy(
                    src_ref=ins[m].at[2 * px + py, pl.ds(r0, n)], dst_ref=lands[m].at[myj, pl.ds(r0, n)],
                    send_sem=ssem.at[3 * m + k], recv_sem=rsem.at[3 * m + k],
                    device_id=(px, py, c), device_id_type=MESH), rh, rc)
        token[...] = jnp.zeros_like(token)

    bufs = [pltpu.HBM(p.shape, p.dtype) for p in parts]
    outs = pl.pallas_call(
        body, name=name,
        out_shape=(pltpu.SemaphoreType.DMA((3 * nm,)), pltpu.SemaphoreType.DMA((3 * nm,)), *bufs, *bufs,
                   jax.ShapeDtypeStruct((8, LANES), F32)),
        in_specs=[HBM] * (2 * nm),
        out_specs=(SEM, SEM, *([HBM] * (2 * nm)), pl.BlockSpec(memory_space=pltpu.VMEM)),
        input_output_aliases={i: 2 + i for i in range(2 * nm)},
        compiler_params=pltpu.CompilerParams(has_side_effects=EFFECT),
    )(*[_in_hbm(p) for p in parts], *[_in_hbm(lax.empty(p.shape, p.dtype)) for p in parts])
    return (outs[0], outs[1], list(outs[2:2 + nm]), list(outs[2 + nm:2 + 2 * nm])), outs[-1]


def _exchange_wait(handle, after, *, name):
    ssem_in, rsem_in, parts, lands = handle
    nm = len(parts)

    def body(*refs):
        ins = refs[:nm]
        lnd = refs[nm:2 * nm]
        ssem, rsem = refs[2 * nm:2 * nm + 2]
        x, y, c, chips = _place()
        for m in range(nm):
            for k, (px, py) in enumerate(chips):
                pj = 2 * px + py
                cp = pltpu.make_async_remote_copy(
                    src_ref=ins[m].at[pj], dst_ref=lnd[m].at[pj],
                    send_sem=ssem.at[3 * m + k], recv_sem=rsem.at[3 * m + k],
                    device_id=(px, py, c), device_id_type=MESH)
                cp.wait_send()
                cp.wait_recv()

    bufs = [pltpu.HBM(p.shape, p.dtype) for p in parts]
    outs = pl.pallas_call(
        body, name=name,
        out_shape=(*bufs, *bufs),
        in_specs=[HBM] * (2 * nm) + [SEM, SEM, ANY],
        out_specs=[HBM] * (2 * nm),
        input_output_aliases={i: i for i in range(2 * nm)},
        compiler_params=pltpu.CompilerParams(has_side_effects=EFFECT),
    )(*parts, *lands, ssem_in, rsem_in, after)
    return list(outs[nm:]), list(outs[:nm])


def _sibling_join(grads, regions, *, name):
    nm = len(grads)
    nr = len(regions)
    shapes = [jax.ShapeDtypeStruct(g.shape, g.dtype) for g in grads]

    def body(*refs):
        outs = refs[nm:2 * nm]
        ssem, rsem = refs[2 * nm:]
        x, y, c, _ = _place()
        sib = (x, y, 1 - c)
        cps = []
        for i, (m, off, rows) in enumerate(regions):
            rh, cols = rows // 2, grads[m].shape[1]
            rc = _chunk_rows(rh, cols * grads[m].dtype.itemsize, _row_align(grads[m].dtype))

            def send(r0, n, i=i, m=m, off=off, rh=rh):
                part = outs[m].at[pl.ds(off + c * rh + r0, n)]
                return pltpu.make_async_remote_copy(
                    src_ref=part, dst_ref=part, send_sem=ssem.at[i], recv_sem=rsem.at[i],
                    device_id=sib, device_id_type=MESH)
            _start_chunks(send, rh, rc)
            cps.append(send(0, rh))
        for i, (m, off, rows) in enumerate(regions):
            rh = rows // 2
            cps[i].wait_send()
            got = outs[m].at[pl.ds(off + (1 - c) * rh, rh)]
            pltpu.make_async_remote_copy(
                src_ref=got, dst_ref=got, send_sem=ssem.at[i], recv_sem=rsem.at[i],
                device_id=sib, device_id_type=MESH).wait_recv()

    return pl.pallas_call(
        body, name=name,
        in_specs=[ANY] * nm, out_specs=[ANY] * nm, out_shape=shapes,
        input_output_aliases={i: i for i in range(nm)},
        scratch_shapes=[pltpu.SemaphoreType.DMA((nr,)), pltpu.SemaphoreType.DMA((nr,))],
    )(*grads)


def _all_reduce_small(pack):
    R, C = pack.shape

    def body(in_ref, out_ref, slots, ssem, rsem):
        x, y, c, _ = _place()
        me = 4 * x + 2 * y + c
        slots[me] = in_ref[...]
        cps = []
        for k in range(1, N_DEV):
            dx, dy, dc = (k >> 2) & 1, (k >> 1) & 1, k & 1
            peer = (x ^ dx, y ^ dy, c ^ dc)
            cp = pltpu.make_async_remote_copy(
                src_ref=in_ref, dst_ref=slots.at[me], send_sem=ssem.at[k], recv_sem=rsem.at[k],
                device_id=peer, device_id_type=MESH)
            cp.start()
            cps.append(cp)
        for k in range(1, N_DEV):
            dx, dy, dc = (k >> 2) & 1, (k >> 1) & 1, k & 1
            got = slots.at[4 * (x ^ dx) + 2 * (y ^ dy) + (c ^ dc)]
            pltpu.make_async_remote_copy(
                src_ref=got, dst_ref=got, send_sem=ssem.at[k], recv_sem=rsem.at[k],
                device_id=(x ^ dx, y ^ dy, c ^ dc), device_id_type=MESH).wait_recv()
        for cp in cps:
            cp.wait_send()
        acc = slots[0]
        for s in range(1, N_DEV):
            acc = acc + slots[s]
        out_ref[...] = acc

    return pl.pallas_call(
        body, name="ar_small",
        in_specs=[pl.BlockSpec(memory_space=pltpu.VMEM)],
        out_specs=pl.BlockSpec(memory_space=pltpu.VMEM),
        out_shape=jax.ShapeDtypeStruct((R, C), F32),
        scratch_shapes=[pltpu.VMEM((N_DEV, R, C), F32),
                        pltpu.SemaphoreType.DMA((N_DEV,)), pltpu.SemaphoreType.DMA((N_DEV,))],
    )(pack)


def kernel(x, positions, mix_norm_pre, mix_norm_post, ffn_norm_pre, ffn_norm_post, ffn_w_gate_up, ffn_w_down, conv_w_in, conv_w, conv_w_out, kv_norm, w_kv, w_q, w_o, loss_target, m_mix_norm_pre, m_mix_norm_post, m_ffn_norm_pre, m_ffn_norm_post, m_ffn_w_gate_up, m_ffn_w_down, m_conv_w_in, m_conv_w, m_conv_w_out, m_kv_norm, m_w_kv, m_w_q, m_w_o, v_mix_norm_pre, v_mix_norm_post, v_ffn_norm_pre, v_ffn_norm_post, v_ffn_w_gate_up, v_ffn_w_down, v_conv_w_in, v_conv_w, v_conv_w_out, v_kv_norm, v_w_kv, v_w_q, v_w_o):
    T, D = x.shape[1], x.shape[2]
    L = ffn_w_gate_up.shape[0]
    n_gu = ffn_w_gate_up.shape[2]
    f_sh = ffn_w_down.shape[1]
    F = N_CHIPS * f_sh
    x0 = x[0]
    tgt = loss_target[0]

    half = HEAD_DIM // 2
    inv_freq = ROPE_THETA ** (-jnp.arange(half, dtype=F32) / half)
    ang = positions[0].astype(F32)[:, None] * inv_freq
    cosv, sinv = jnp.cos(ang), jnp.sin(ang)
    cos2 = jnp.tile(cosv, (1, LANES // half))
    ss2 = jnp.tile(jnp.concatenate([-sinv, sinv], axis=1), (1, LANES // HEAD_DIM))

    def as2d(a):
        return a.reshape(-1, a.shape[-1])

    big = [ffn_w_gate_up, ffn_w_down, conv_w_in, conv_w_out, w_kv, w_q, w_o]
    big_m = [m_ffn_w_gate_up, m_ffn_w_down, m_conv_w_in, m_conv_w_out, m_w_kv, m_w_q, m_w_o]
    big_v = [v_ffn_w_gate_up, v_ffn_w_down, v_conv_w_in, v_conv_w_out, v_w_kv, v_w_q, v_w_o]
    chip = 2 * lax.axis_index("x") + lax.axis_index("y")
    where = jnp.stack([chip, lax.axis_index("c")]).astype(jnp.int32)
    tc = conv_w.shape[2]
    cw_pad = jnp.concatenate([conv_w[0], jnp.zeros((8 - conv_w.shape[1], tc), F32)], axis=0)

    GU0, GU1, WD0, WD1, WCI, WCO, WKV, WQ, WO = range(9)
    shards = [ffn_w_gate_up[0], ffn_w_gate_up[1], ffn_w_down[0], ffn_w_down[1], conv_w_in[0], conv_w_out[0],
              w_kv, w_q[0], w_o[0]]
    ag_groups = [
        [(WCI, None, 0, D), (WCO, None, 0, D // N_CHIPS)],
        [(GU0, None, 0, D), (WD0, None, 0, f_sh)],
        [(WKV, None, 0, D), (WQ, None, 0, D)],
        [(WO, None, 0, D // N_CHIPS), (GU1, None, 0, D), (WD1, None, 0, f_sh)],
    ]

    def localised(group, idxs):
        return [(idxs.index(p[0]),) + p[1:] for p in group]

    cur = [None] * len(shards)
    first = [WCI, WCO]
    sems0, vals, taps, token = _ag_start(
        [_cast_place(shards[i], where, BF, name=f"place{i}") for i in first],
        _cast_place(cw_pad, where, F32, name="place_taps"), [localised(ag_groups[0], first)], name="ag_start0")
    for i, v in zip(first, vals):
        cur[i] = v
    rest = [i for i in range(len(shards)) if i not in first]
    sems1, vals, _, _ = _ag_start(
        [_cast_place(shards[i], where, BF, name=f"place{i}", dep=token) for i in rest], None,
        [localised(g, rest) for g in ag_groups[1:]], name="ag_start1")
    for i, v in zip(rest, vals):
        cur[i] = v
    ag_sems = sems0 + sems1

    def gather_group(g, after):
        nonlocal taps
        idxs = sorted({p[0] for p in ag_groups[g]})
        local = localised(ag_groups[g], idxs)
        vals, landed_taps = _ag_wait(ag_sems[g], [cur[i] for i in idxs], taps if g == 0 else None, local, after,
                                     name=f"ag_wait{g}")
        if g == 0:
            taps = landed_taps
        vals = _ag_forward(vals, local, name=f"ag_forward{g}")
        for i, v in zip(idxs, vals):
            cur[i] = v

    def row(a, i):
        return a[i:i + 1]

    gather_group(0, None)
    wci, wco, cw = cur[WCI], cur[WCO].reshape(1, D, D), taps
    z, hn_m0 = _norm_matmul(x0, row(mix_norm_pre, 0), wci, cos2, ss2, name="f0_conv_in",
                            rope_shards=0, scale=1.0, out_dtype=BF)
    vmix = _conv_fwd(z, cw, name="f0_conv")
    y0, h1 = _matmul_postnorm(vmix, wco, 0, row(mix_norm_post, 0), x0, name="f0_conv_out")
    gather_group(1, h1)
    wgu0, wd0 = cur[GU0], cur[WD0].reshape(1, F, D)
    g0, u0, a0, hn_f0 = _norm_swiglu(h1, row(ffn_norm_pre, 0), wgu0, 0, name="f0_gate_up")
    f0, h2 = _matmul_postnorm(a0, wd0, 0, row(ffn_norm_post, 0), h1, name="f0_down")

    gather_group(2, h2)
    wkv, wq = cur[WKV], cur[WQ]
    kv_all, hn_kv = _norm_matmul(h2, kv_norm.reshape(1, D), wkv, cos2, ss2, name="f1_kv",
                                 rope_shards=N_CHIPS // 2, scale=1.0, out_dtype=F32)
    q_all, hn_m1 = _norm_matmul(h2, row(mix_norm_pre, 1), wq, cos2, ss2, name="f1_q",
                                rope_shards=N_CHIPS, scale=HEAD_DIM ** -0.5, out_dtype=F32)
    o_att, lse = _attn_fwd(q_all, kv_all, name="f1_attn")
    gather_group(3, o_att)
    wgu1, wd1, wo = cur[GU1], cur[WD1].reshape(1, F, D), cur[WO].reshape(1, D, D)
    y1, h3 = _matmul_postnorm(o_att, wo, 0, row(mix_norm_post, 1), h2, name="f1_attn_out")
    g1, u1, a1, hn_f1 = _norm_swiglu(h3, row(ffn_norm_pre, 1), wgu1, 0, name="f1_gate_up")
    f1, h4 = _matmul_postnorm(a1, wd1, 0, row(ffn_norm_post, 1), h3, name="f1_down")

    dh4, sq = _loss_head(h4, tgt, name="loss_head")
    loss_part = 0.5 * sq[0, 0] / D

    gu_shape = (N_CHIPS, D, n_gu)
    in_chips = lambda a: a.reshape(N_CHIPS, -1, a.shape[-1])

    def scatter_start(dws, tag):
        theirs = _sibling_swap(dws, name=f"rs_swap_{tag}")
        parts = [_pair_sum(dw, t, where, name=f"rs_pair_sum_{tag}{i}") for i, (dw, t) in enumerate(zip(dws, theirs))]
        return _exchange_start(parts, name=f"rs_exchange_start_{tag}")

    dyf1, dg1, du1, d_ffn_post1 = _postnorm_bwd_swiglu(dh4, f1, row(ffn_norm_post, 1), wd1, 0, g1, u1,
                                                       name="b1_down")
    dwd1 = _grad_matmul(a1, dyf1, (2, F // 2, D), F // 2, D, lambda i, j: (i, 0, 0), None, name="b1_dw_down")
    dwgu1 = _grad_matmul(hn_f1, dg1, gu_shape, D, n_gu, lambda i, j: (j, 0, 0), None, name="b1_dw_gate")
    dwgu1 = _grad_matmul(hn_f1, du1, gu_shape, D, n_gu, lambda i, j: (j + 2, 0, 0), dwgu1, name="b1_dw_up")
    dh3, d_ffn_pre1 = _matmul_prenorm_bwd((dg1, du1), wgu1, 0, h3, row(ffn_norm_pre, 1), dh4, name="b1_gate_up")

    dy1, do, d_mix_post1 = _postnorm_bwd_matmul(dh3, y1, row(mix_norm_post, 1), wo, 0, name="b1_attn_out",
                                                da_dtype=F32)
    dwo = _grad_matmul(o_att, dy1, (1, D, D), D, D, lambda i, j: (0, 0, 0), None, name="b1_dw_o")
    rs_a, token = scatter_start([dwgu1, in_chips(dwd1), in_chips(dwo)], "a")
    prev = None
    for gi, (window, dil) in enumerate(BRANCHES):
        prev = _attn_bwd(q_all, kv_all, do, o_att, lse, cos2, ss2, gi, dil, prev, name=f"b1_attn{gi}",
                         dep=token if gi == 0 else None)
    dq_all, dk_all, dv_all = prev
    n_q = wq.shape[2]
    n_kv = wkv.shape[2]
    dwq = _grad_matmul(hn_m1, dq_all, (N_CHIPS, D, n_q), D, n_q, lambda i, j: (j, 0, 0), None, name="b1_dw_q")
    dwkv = _grad_matmul(hn_kv, dk_all, (N_CHIPS, D, n_kv), D, n_kv, lambda i, j: (j, 0, 0), None, name="b1_dw_k")
    dwkv = _grad_matmul(hn_kv, dv_all, (N_CHIPS, D, n_kv), D, n_kv, lambda i, j: (j + 2, 0, 0), dwkv, name="b1_dw_v")
    dh2, d_mix_pre1 = _matmul_prenorm_bwd((dq_all,), wq, 0, h2, row(mix_norm_pre, 1), dh3, name="b1_q")
    dh2, d_kv_norm = _matmul_prenorm_bwd((dk_all, dv_all), wkv, 0, h2, kv_norm.reshape(1, D), dh2, name="b1_kv")
    rs_b, token = scatter_start([dwkv, dwq], "b")

    dyf0, dg0, du0, d_ffn_post0 = _postnorm_bwd_swiglu(dh2, f0, row(ffn_norm_post, 0), wd0, 0, g0, u0,
                                                       name="b0_down", dep=token)
    dwd0 = _grad_matmul(a0, dyf0, (2, F // 2, D), F // 2, D, lambda i, j: (i, 0, 0), None, name="b0_dw_down")
    dwgu0 = _grad_matmul(hn_f0, dg0, gu_shape, D, n_gu, lambda i, j: (j, 0, 0), None, name="b0_dw_gate")
    dwgu0 = _grad_matmul(hn_f0, du0, gu_shape, D, n_gu, lambda i, j: (j + 2, 0, 0), dwgu0, name="b0_dw_up")
    dh1, d_ffn_pre0 = _matmul_prenorm_bwd((dg0, du0), wgu0, 0, h1, row(ffn_norm_pre, 0), dh2, name="b0_gate_up")
    rs_c, token = scatter_start([dwgu0, in_chips(dwd0)], "c")

    dy0, dvmix, d_mix_post0 = _postnorm_bwd_matmul(dh1, y0, row(mix_norm_post, 0), wco, 0, name="b0_conv_out",
                                                   da_dtype=BF, dep=token)
    dwco = _grad_matmul(vmix, dy0, (1, D, D), D, D, lambda i, j: (0, 0, 0), None, name="b0_dw_conv_out")
    dz, dcw = _conv_bwd(z, cw, dvmix, name="b0_conv")
    n_ci = wci.shape[2]
    dwci = _grad_matmul(hn_m0, dz, (N_CHIPS, D, n_ci), D, n_ci, lambda i, j: (j, 0, 0), None, name="b0_dw_conv_in")
    dx, d_mix_pre0 = _matmul_prenorm_bwd((dz,), wci, 0, x0, row(mix_norm_pre, 0), dh1, name="b0_conv_in")

    pack = jnp.concatenate([
        d_mix_pre0, d_mix_pre1, d_mix_post0, d_mix_post1, d_ffn_pre0, d_ffn_pre1, d_ffn_post0, d_ffn_post1,
        d_kv_norm, dcw[0:3], jnp.full((1, D), loss_part, F32),
        jnp.zeros((SMALL_ROWS - 13, D), F32)], axis=0)
    red = _all_reduce_small(pack)
    loss = red[12, 0]
    myj = 2 * lax.axis_index("x") + lax.axis_index("y")
    g_conv_w = lax.dynamic_slice(red, (9, myj * tc), (3, tc))

    zeros7 = jnp.zeros((SMALL_ROWS - 9, D), F32)
    w_small = jnp.concatenate([mix_norm_pre, mix_norm_post, ffn_norm_pre, ffn_norm_post, kv_norm.reshape(1, D), zeros7], axis=0)
    m_small = jnp.concatenate([m_mix_norm_pre, m_mix_norm_post, m_ffn_norm_pre, m_ffn_norm_post, m_kv_norm.reshape(1, D), zeros7], axis=0)
    v_small = jnp.concatenate([v_mix_norm_pre, v_mix_norm_post, v_ffn_norm_pre, v_ffn_norm_post, v_kv_norm.reshape(1, D), zeros7], axis=0)
    d_small, nm_small, nv_small = _adamw(w_small, red, m_small, v_small, name="adamw_small")

    pad5 = jnp.zeros((5, tc), F32)
    d_cw, nm_cw, nv_cw = _adamw(cw_pad, jnp.concatenate([g_conv_w, pad5], axis=0),
                                jnp.concatenate([m_conv_w[0], pad5], axis=0),
                                jnp.concatenate([v_conv_w[0], pad5], axis=0), name="adamw_conv_w")

    rs_d, _ = scatter_start([dwci, in_chips(dwco)], "d")

    pieces = {"a": [(0, D), (1, f_sh), (6, 0)], "b": [(4, 0), (5, 0)], "c": [(0, 0), (1, 0)], "d": [(2, 0), (3, 0)]}
    grads2d = [None] * len(big)
    big_out = [None] * len(big)

    def finish(groups, after, tag):
        regions, idxs = [], []
        for gtag, handle in groups:
            landed, parts = _exchange_wait(handle, after, name=f"rs_exchange_wait_{gtag}")
            for i, (l, p, (wi, off)) in enumerate(zip(landed, parts, pieces[gtag])):
                total = as2d(big[wi]).shape[0]
                grads2d[wi] = _chip_sum(l, p, where, total, off, grads2d[wi], name=f"rs_chip_sum_{gtag}{i}")
                if wi not in idxs:
                    idxs.append(wi)
                regions.append((idxs.index(wi), off, 2 * l.shape[1]))
        joined = _sibling_join([grads2d[wi] for wi in idxs], regions, name=f"rs_sibling_join_{tag}")
        for wi, gr in zip(idxs, joined):
            w = big[wi]
            d_, m_, v_ = _adamw(as2d(w), gr, as2d(big_m[wi]), as2d(big_v[wi]), name=f"adamw{wi}")
            big_out[wi] = (gr.reshape(w.shape), d_.reshape(w.shape), m_.reshape(w.shape), v_.reshape(w.shape))

    finish([("a", rs_a), ("b", rs_b), ("c", rs_c)], dx, "abc")
    finish([("d", rs_d)], big_out[0][1], "d")

    def small(a):
        return (a[0:2], a[2:4], a[4:6], a[6:8])

    def assemble(sm, cwv, kind):
        pre, post, fpre, fpost = small(sm)
        b = [t[kind] for t in big_out]
        return [pre, post, fpre, fpost, b[0], b[1], b[2], cwv[0:3].reshape(conv_w.shape), b[3],
                sm[8], b[4], b[5].reshape(w_q.shape), b[6].reshape(w_o.shape)]

    grads = assemble(red, jnp.concatenate([g_conv_w, pad5], axis=0), 0)
    deltas = assemble(d_small, d_cw, 1)
    new_m = assemble(nm_small, nm_cw, 2)
    new_v = assemble(nv_small, nv_cw, 3)
    return (loss, dx.reshape(x.shape), *grads, *deltas, *new_m, *new_v)
```

```python
import functools

import jax
import jax.numpy as jnp
from jax import lax
from jax.experimental import pallas as pl
from jax.experimental.pallas import tpu as pltpu

HEAD_DIM = 64
BAND = 128
BRANCHES = ((128, 1), (512, 4), (2048, 16))
ROPE_THETA = 10000.0
RMS_EPS = 1e-6
NEG_INF = -1e30
ADAM_LR = 0.001
ADAM_B1 = 0.9
ADAM_B2 = 0.999
ADAM_EPS = 1e-08
ADAM_WD = 0.01
ADAM_STEP = 10

N_CHIPS = 4
N_DEV = 8
LANES = 128
MXU_COLS = 256
ROW_BLOCK = 512
ROW_BLOCK_WIDE = 1024
ATTN_BLOCK_ROWS = 2048
VMEM_LIMIT = 56 * 1024 * 1024
SMALL_ROWS = 16
ADAMW_BLOCK_BYTES = 1024 * 1024
DMA_CHUNK_BYTES = 512 * 1024

BF = jnp.bfloat16
F32 = jnp.float32
MESH = pl.DeviceIdType.MESH
ANY = pl.BlockSpec(memory_space=pl.ANY)


def _cp(*sem):
    return pltpu.CompilerParams(dimension_semantics=sem, vmem_limit_bytes=VMEM_LIMIT)


def _rot_half(t, first):
    return jnp.where(first, pltpu.roll(t, 96, 1), pltpu.roll(t, 32, 1))


def _sigmoid(x):
    return pl.reciprocal(1.0 + jnp.exp(-x), approx=True)


def _col_chunks(n):
    return [(c0, min(MXU_COLS, n - c0)) for c0 in range(0, n, MXU_COLS)]


def _first_half_mask(rows):
    lane = lax.broadcasted_iota(jnp.int32, (rows, LANES), 1)
    return (lane % HEAD_DIM) < (HEAD_DIM // 2)


def _normed_rows(j, rows, x_ref, g_ref, xn_ref, xs, last_start, tm):
    @pl.when(j == 0)
    def _():
        xv = x_ref[...]
        r = lax.rsqrt(jnp.mean(xv * xv, axis=-1, keepdims=True) + RMS_EPS)
        xn = (xv * r * g_ref[...]).astype(BF)
        xs[rows, :] = xn
        xn_ref[...] = xn

    @pl.when(j > 0)
    def _():
        xn_ref[...] = xs[pl.ds(last_start, tm), :]


def _norm_matmul(x, gain, wg, cos2, ss2, *, name, rope_shards, scale, out_dtype):
    T, D = x.shape
    n = wg.shape[2]
    tm = min(ROW_BLOCK_WIDE, T)
    ni = T // tm

    def body(x_ref, g_ref, w_ref, cos_ref, ss_ref, y_ref, xn_ref, xs):
        j = pl.program_id(0)
        rows = pl.ds(pl.multiple_of(pl.program_id(1) * tm, tm), tm)
        _normed_rows(j, rows, x_ref, g_ref, xn_ref, xs, (ni - 1) * tm, tm)
        acc = jnp.dot(xs[rows, :], w_ref[...], preferred_element_type=F32)

        def plain():
            y_ref[...] = acc.astype(out_dtype)

        def rope():
            cosv = cos_ref[...]
            ssv = ss_ref[...]
            first = _first_half_mask(tm)
            for ci in range(n // LANES):
                t = acc[:, ci * LANES:(ci + 1) * LANES]
                y = (t * cosv + _rot_half(t, first) * ssv) * scale
                y_ref[:, ci * LANES:(ci + 1) * LANES] = y.astype(out_dtype)

        if rope_shards == 0:
            plain()
        elif rope_shards == N_CHIPS:
            rope()
        else:
            pl.when(j < rope_shards)(rope)
            pl.when(j >= rope_shards)(plain)

    first_pass = lambda j, i: (jnp.where(j == 0, i, ni - 1), 0)
    return pl.pallas_call(
        body, name=name,
        grid=(N_CHIPS, ni),
        in_specs=[
            pl.BlockSpec((tm, D), first_pass),
            pl.BlockSpec((1, D), lambda j, i: (0, 0)),
            pl.BlockSpec((None, D, n), lambda j, i: (j, 0, 0)),
            pl.BlockSpec((tm, LANES), lambda j, i: (i, 0)),
            pl.BlockSpec((tm, LANES), lambda j, i: (i, 0)),
        ],
        out_specs=[
            pl.BlockSpec((tm, n), lambda j, i: (i, j)),
            pl.BlockSpec((tm, D), first_pass),
        ],
        out_shape=[jax.ShapeDtypeStruct((T, N_CHIPS * n), out_dtype),
                   jax.ShapeDtypeStruct((T, D), BF)],
        scratch_shapes=[pltpu.VMEM((T, D), BF)],
        compiler_params=_cp("arbitrary", "arbitrary"),
    )(x, gain, wg, cos2, ss2)


def _norm_swiglu(x, gain, wg, layer, *, name):
    T, D = x.shape
    n = wg.shape[2]
    tm = min(ROW_BLOCK, T)

    ni = T // tm

    def body(x_ref, g_ref, wg_ref, wu_ref, go_ref, uo_ref, ao_ref, xn_ref, xs):
        j = pl.program_id(0)
        rows = pl.ds(pl.multiple_of(pl.program_id(1) * tm, tm), tm)
        _normed_rows(j, rows, x_ref, g_ref, xn_ref, xs, (ni - 1) * tm, tm)
        g = jnp.dot(xs[rows, :], wg_ref[...], preferred_element_type=F32)
        u = jnp.dot(xs[rows, :], wu_ref[...], preferred_element_type=F32)
        go_ref[...] = g.astype(BF)
        uo_ref[...] = u.astype(BF)
        ao_ref[...] = (g * _sigmoid(g) * u).astype(BF)

    half = N_CHIPS // 2
    first_pass = lambda j, i: (jnp.where(j == 0, i, ni - 1), 0)
    act = jax.ShapeDtypeStruct((T, half * n), BF)
    return pl.pallas_call(
        body, name=name,
        grid=(half, ni),
        in_specs=[
            pl.BlockSpec((tm, D), first_pass),
            pl.BlockSpec((1, D), lambda j, i: (0, 0)),
            pl.BlockSpec((None, D, n), lambda j, i: (j, layer, 0)),
            pl.BlockSpec((None, D, n), lambda j, i: (j + half, layer, 0)),
        ],
        out_specs=[
            pl.BlockSpec((tm, n), lambda j, i: (i, j)),
            pl.BlockSpec((tm, n), lambda j, i: (i, j)),
            pl.BlockSpec((tm, n), lambda j, i: (i, j)),
            pl.BlockSpec((tm, D), first_pass),
        ],
        out_shape=[act, act, act, jax.ShapeDtypeStruct((T, D), BF)],
        scratch_shapes=[pltpu.VMEM((T, D), BF)],
        compiler_params=_cp("arbitrary", "arbitrary"),
    )(x, gain, wg, wg)


def _matmul_postnorm(a, w3, widx, gain, h_old, *, name):
    T, K = a.shape
    D = w3.shape[2]
    tm = min(ROW_BLOCK_WIDE, T)

    def body(a_ref, w_ref, g_ref, h_ref, y_ref, hn_ref):
        y = jnp.dot(a_ref[...].astype(BF), w_ref[...], preferred_element_type=F32)
        y_ref[...] = y.astype(BF)
        r = lax.rsqrt(jnp.mean(y * y, axis=-1, keepdims=True) + RMS_EPS)
        hn_ref[...] = h_ref[...] + y * r * g_ref[...]

    return pl.pallas_call(
        body, name=name,
        grid=(T // tm,),
        in_specs=[
            pl.BlockSpec((tm, K), lambda i: (i, 0)),
            pl.BlockSpec((None, K, D), lambda i: (widx, 0, 0)),
            pl.BlockSpec((1, D), lambda i: (0, 0)),
            pl.BlockSpec((tm, D), lambda i: (i, 0)),
        ],
        out_specs=[pl.BlockSpec((tm, D), lambda i: (i, 0)),
                   pl.BlockSpec((tm, D), lambda i: (i, 0))],
        out_shape=[jax.ShapeDtypeStruct((T, D), BF), jax.ShapeDtypeStruct((T, D), F32)],
        compiler_params=_cp("parallel"),
    )(a, w3, gain, h_old)


def _loss_head(h, target, *, name):
    T, D = h.shape
    tm = min(ROW_BLOCK, T)

    def body(h_ref, t_ref, dh_ref, s_ref):
        i = pl.program_id(0)

        @pl.when(i == 0)
        def _():
            s_ref[...] = jnp.zeros_like(s_ref)

        e = h_ref[...] - t_ref[...]
        dh_ref[...] = e * (1.0 / D)
        s_ref[...] += jnp.sum(e * e)

    return pl.pallas_call(
        body, name=name,
        grid=(T // tm,),
        in_specs=[pl.BlockSpec((tm, D), lambda i: (i, 0)), pl.BlockSpec((tm, D), lambda i: (i, 0))],
        out_specs=[pl.BlockSpec((tm, D), lambda i: (i, 0)), pl.BlockSpec((8, LANES), lambda i: (0, 0))],
        out_shape=[jax.ShapeDtypeStruct((T, D), F32), jax.ShapeDtypeStruct((8, LANES), F32)],
        compiler_params=_cp("arbitrary"),
    )(h, target)


def _shift_down(u, k):
    row = lax.broadcasted_iota(jnp.int32, u.shape, 0)
    return jnp.where(row >= k, pltpu.roll(u, k, 0), 0.0)


def _shift_up(u, k):
    T = u.shape[0]
    row = lax.broadcasted_iota(jnp.int32, u.shape, 0)
    return jnp.where(row < T - k, pltpu.roll(u, T - k, 0), 0.0)


def _conv_fwd(z, cw, *, name):
    T = z.shape[0]
    D = z.shape[1] // 3
    tc = cw.shape[2]
    nb = D // tc

    def body(b_ref, c_ref, h_ref, w_ref, o_ref):
        u = c_ref[...].astype(F32) * h_ref[...].astype(F32)
        w = w_ref[...]
        conv = w[2:3] * u + w[1:2] * _shift_down(u, 1) + w[0:1] * _shift_down(u, 2)
        o_ref[...] = (b_ref[...].astype(F32) * conv).astype(BF)

    return pl.pallas_call(
        body, name=name,
        grid=(nb,),
        in_specs=[
            pl.BlockSpec((T, tc), lambda j: (0, j)),
            pl.BlockSpec((T, tc), lambda j: (0, nb + j)),
            pl.BlockSpec((T, tc), lambda j: (0, 2 * nb + j)),
            pl.BlockSpec((None, 8, tc), lambda j: (j, 0, 0)),
        ],
        out_specs=pl.BlockSpec((T, tc), lambda j: (0, j)),
        out_shape=jax.ShapeDtypeStruct((T, D), BF),
        compiler_params=_cp("parallel"),
    )(z, z, z, cw)


def _conv_bwd(z, cw, dv, *, name):
    T = z.shape[0]
    D = z.shape[1] // 3
    tc = LANES
    nb = D // tc
    per = cw.shape[2] // tc

    def body(b_ref, c_ref, h_ref, w_ref, dv_ref, dz_ref, dw_ref, stage, sems):
        j = pl.program_id(0)
        slot = j % 2

        def slab(p, jj, s):
            col = pl.multiple_of((p * nb + jj) * tc, tc)
            return pltpu.make_async_copy(stage.at[s, p], dz_ref.at[:, pl.ds(col, tc)], sems.at[s, p])

        @pl.when(j >= 2)
        def _():
            for p in range(3):
                slab(p, j - 2, slot).wait()

        c = c_ref[...].astype(F32)
        h = h_ref[...].astype(F32)
        u = c * h
        u1 = _shift_down(u, 1)
        u2 = _shift_down(u, 2)
        w = w_ref[...]
        dvv = dv_ref[...].astype(F32)
        dconv = dvv * b_ref[...].astype(F32)
        du = w[2:3] * dconv + w[1:2] * _shift_up(dconv, 1) + w[0:1] * _shift_up(dconv, 2)
        rows = lax.broadcasted_iota(jnp.int32, (8, tc), 0)
        dw_ref[...] = jnp.where(rows == 0, jnp.sum(dconv * u2, axis=0, keepdims=True),
                                jnp.where(rows == 1, jnp.sum(dconv * u1, axis=0, keepdims=True),
                                          jnp.where(rows == 2, jnp.sum(dconv * u, axis=0, keepdims=True), 0.0)))
        stage[slot, 0] = (dvv * (w[2:3] * u + w[1:2] * u1 + w[0:1] * u2)).astype(BF)
        stage[slot, 1] = (du * h).astype(BF)
        stage[slot, 2] = (du * c).astype(BF)
        for p in range(3):
            slab(p, j, slot).start()

        @pl.when(j == nb - 1)
        def _():
            for p in range(3):
                slab(p, j, slot).wait()
            if nb > 1:
                for p in range(3):
                    slab(p, j - 1, 1 - slot).wait()

    return pl.pallas_call(
        body, name=name,
        grid=(nb,),
        in_specs=[
            pl.BlockSpec((T, tc), lambda j: (0, j)),
            pl.BlockSpec((T, tc), lambda j: (0, nb + j)),
            pl.BlockSpec((T, tc), lambda j: (0, 2 * nb + j)),
            pl.BlockSpec((None, 8, tc), lambda j: (j // per, 0, j % per)),
            pl.BlockSpec((T, tc), lambda j: (0, j)),
        ],
        out_specs=[ANY, pl.BlockSpec((8, tc), lambda j: (0, j))],
        out_shape=[jax.ShapeDtypeStruct((T, 3 * D), BF), jax.ShapeDtypeStruct((8, D), F32)],
        scratch_shapes=[pltpu.VMEM((2, 3, T, tc), BF), pltpu.SemaphoreType.DMA((2, 3))],
        compiler_params=_cp("arbitrary"),
    )(z, z, z, cw, dv)


def _strided(base, count, d):
    return pl.ds(base, count, stride=d) if d > 1 else pl.ds(pl.multiple_of(base, BAND), count)


def _fill_band_bias(bias):
    qi = lax.broadcasted_iota(jnp.int32, (2 * BAND, 2 * BAND), 0) % BAND
    kj = lax.broadcasted_iota(jnp.int32, (2 * BAND, 2 * BAND), 1)
    dist = qi + BAND - kj
    band = (dist >= 0) & (dist <= BAND)
    bias[0] = jnp.where(band & (kj >= BAND), 0.0, NEG_INF)
    bias[1] = jnp.where(band, 0.0, NEG_INF)


def _attn_block_rows(T):
    return min(ATTN_BLOCK_ROWS, T)


def _head_mask():
    lane = lax.broadcasted_iota(jnp.int32, (2 * BAND, LANES), 1)
    row = lax.broadcasted_iota(jnp.int32, (2 * BAND, LANES), 0)
    return (lane < HEAD_DIM) == (row < BAND)


def _attn_fwd(q_all, kv_all, *, name):
    T = q_all.shape[0]
    NB = len(BRANCHES)
    Dm = q_all.shape[1] // NB
    HP = Dm // LANES
    R = _attn_block_rows(T)
    units = R // BAND
    dmax = max(d for _, d in BRANCHES)

    def body(*refs):
        ins = refs[:5 * NB]
        o_ref, l_ref, kbuf, vbuf, o_s, l_s, bias = refs[5 * NB:]
        n = pl.program_id(0)
        _fill_band_bias(bias)
        hm = _head_mask()
        low = lax.broadcasted_iota(jnp.int32, (BAND, LANES), 1) < HEAD_DIM

        for g, (_, d) in enumerate(BRANCHES):
            q_ref, kp_ref, kc_ref, vp_ref, vc_ref = ins[5 * g:5 * g + 5]
            pr = BAND * d
            kbuf[0:pr, :] = kp_ref[...]
            kbuf[pr:pr + R, :] = kc_ref[...]
            vbuf[0:pr, :] = vp_ref[...]
            vbuf[pr:pr + R, :] = vc_ref[...]

            def unit(u, carry, g=g, d=d, pr=pr, q_ref=q_ref):
                sub = u // d
                base = sub * pr + (u - sub * d)
                q = q_ref[_strided(base, BAND, d), :]
                q2 = jnp.where(hm, jnp.concatenate([q, q], axis=0), 0.0).astype(BF)
                k2 = kbuf[_strided(base, 2 * BAND, d), :].astype(BF)
                v2 = vbuf[_strided(base, 2 * BAND, d), :].astype(BF)
                s = lax.dot_general(q2, k2, (((1,), (1,)), ((), ())), preferred_element_type=F32)
                s = s + bias[((n > 0) | (sub > 0)).astype(jnp.int32)]
                m = jnp.max(s, axis=-1, keepdims=True)
                p = jnp.exp(s - m)
                l = jnp.sum(p, axis=-1, keepdims=True)
                pv = jnp.dot(p.astype(BF), v2, preferred_element_type=F32) * (1.0 / l)
                lse = m + jnp.log(l)
                o_s[g, _strided(base, BAND, d), :] = jnp.where(low, pv[:BAND], pv[BAND:])
                l_s[g, _strided(base, BAND, d), :] = jnp.where(low, lse[:BAND], lse[BAND:])
                return carry

            lax.fori_loop(0, units, unit, 0, unroll=4)

        def merge(i, carry):
            sl = pl.ds(pl.multiple_of(i * BAND, BAND), BAND)
            lv = [l_s[g, sl, :] for g in range(NB)]
            m = functools.reduce(jnp.maximum, lv)
            e = [jnp.exp(v - m) for v in lv]
            tot = functools.reduce(jnp.add, e)
            inv = 1.0 / tot
            o_ref[sl, :] = functools.reduce(jnp.add, [(e[g] * inv) * o_s[g, sl, :] for g in range(NB)])
            l_ref[sl, :] = m + jnp.log(tot)
            return carry

        lax.fori_loop(0, units, merge, 0)

    in_specs, args = [], []
    for g, (_, d) in enumerate(BRANCHES):
        per = R // (BAND * d)
        for col, rows, idx in (
                (g * HP, R, lambda n, hp: n),
                (g * HP, BAND * d, lambda n, hp, per=per: jnp.maximum(n * per - 1, 0)),
                (g * HP, R, lambda n, hp: n),
                ((NB + g) * HP, BAND * d, lambda n, hp, per=per: jnp.maximum(n * per - 1, 0)),
                ((NB + g) * HP, R, lambda n, hp: n)):
            in_specs.append(pl.BlockSpec((rows, LANES), lambda n, hp, col=col, idx=idx: (idx(n, hp), col + hp)))
        args += [q_all, kv_all, kv_all, kv_all, kv_all]
    out = pl.BlockSpec((R, LANES), lambda n, hp: (n, hp))
    return pl.pallas_call(
        body, name=name,
        grid=(T // R, HP),
        in_specs=in_specs,
        out_specs=[out, out],
        out_shape=[jax.ShapeDtypeStruct((T, Dm), F32), jax.ShapeDtypeStruct((T, Dm), F32)],
        scratch_shapes=[pltpu.VMEM((BAND * dmax + R, LANES), F32), pltpu.VMEM((BAND * dmax + R, LANES), F32),
                        pltpu.VMEM((NB, R, LANES), F32), pltpu.VMEM((NB, R, LANES), F32),
                        pltpu.VMEM((2, 2 * BAND, 2 * BAND), F32)],
        compiler_params=_cp("parallel", "parallel"),
    )(*args)


def _attn_bwd(q_all, kv_all, do, o, lse, cos2, ss2, g, d, prev, *, name, dep=None):
    T = q_all.shape[0]
    NB = len(BRANCHES)
    Dm = q_all.shape[1] // NB
    HP = Dm // LANES
    R = _attn_block_rows(T)
    nblk = T // R
    units = R // BAND
    pr = BAND * d
    per = R // pr
    scale = HEAD_DIM ** -0.5

    def rope_bwd(t, cosv, ssv, first):
        return t * cosv - _rot_half(t, first) * ssv

    def body(q_ref, kp_ref, kc_ref, vp_ref, vc_ref, do_ref, o_ref, l_ref, cos_ref, ss_ref, *rest):
        dq_ref, dk_ref, dv_ref, kbuf, vbuf, pend_k, pend_v, bias = rest[-8:]
        i = pl.program_id(1)
        n = nblk - 1 - i
        _fill_band_bias(bias)
        first = _first_half_mask(BAND)
        hm = _head_mask()
        low = lax.broadcasted_iota(jnp.int32, (BAND, LANES), 1) < HEAD_DIM

        kbuf[0:pr, :] = kp_ref[...]
        kbuf[pr:pr + R, :] = kc_ref[...]
        vbuf[0:pr, :] = vp_ref[...]
        vbuf[pr:pr + R, :] = vc_ref[...]

        @pl.when(i == 0)
        def _():
            pend_k[...] = jnp.zeros_like(pend_k)
            pend_v[...] = jnp.zeros_like(pend_v)

        def unit(u, carry):
            sub = per - 1 - u // d
            cls = u % d
            base = sub * pr + cls
            sl = _strided(base, BAND, d)
            sl2 = _strided(base, 2 * BAND, d)
            q = q_ref[sl, :]
            dov = do_ref[sl, :]
            ov = o_ref[sl, :]
            lv = l_ref[sl, :]
            q2 = jnp.where(hm, jnp.concatenate([q, q], axis=0), 0.0).astype(BF)
            do2 = jnp.where(hm, jnp.concatenate([dov, dov], axis=0), 0.0)
            oo = dov * ov
            delta = jnp.sum(jnp.where(hm, jnp.concatenate([oo, oo], axis=0), 0.0), axis=-1, keepdims=True)
            lse2 = jnp.concatenate([lv[:, 0:1], lv[:, HEAD_DIM:HEAD_DIM + 1]], axis=0)
            do2 = do2.astype(BF)
            k2 = kbuf[sl2, :].astype(BF)
            v2 = vbuf[sl2, :].astype(BF)
            s = lax.dot_general(q2, k2, (((1,), (1,)), ((), ())), preferred_element_type=F32)
            p = jnp.exp(s + bias[((n > 0) | (sub > 0)).astype(jnp.int32)] - lse2)
            dp = lax.dot_general(do2, v2, (((1,), (1,)), ((), ())), preferred_element_type=F32)
            ds = (p * (dp - delta)).astype(BF)
            dq2 = jnp.dot(ds, k2, preferred_element_type=F32)
            dq = jnp.where(low, dq2[:BAND], dq2[BAND:])
            dq_ref[sl, :] = rope_bwd(dq, cos_ref[sl, :], ss_ref[sl, :], first) * scale
            tn = (((0,), (0,)), ((), ()))
            dk2 = lax.dot_general(ds, q2, tn, preferred_element_type=F32)
            dv2 = lax.dot_general(p.astype(BF), do2, tn, preferred_element_type=F32)
            dk_ref[sl, :] = dk2[BAND:] + pend_k[cls]
            dv_ref[sl, :] = dv2[BAND:] + pend_v[cls]
            pend_k[cls] = dk2[:BAND]
            pend_v[cls] = dv2[:BAND]
            return carry

        lax.fori_loop(0, units, unit, 0, unroll=4)

        dk_ref[...] = rope_bwd(dk_ref[...], cos_ref[...], ss_ref[...], _first_half_mask(R))

    blk = (R, LANES)
    pblk = (pr, LANES)
    cur = lambda hp, i: nblk - 1 - i
    prv = lambda hp, i: jnp.maximum((nblk - 1 - i) * per - 1, 0)
    in_specs = [
        pl.BlockSpec(blk, lambda hp, i: (cur(hp, i), g * HP + hp)),
        pl.BlockSpec(pblk, lambda hp, i: (prv(hp, i), g * HP + hp)),
        pl.BlockSpec(blk, lambda hp, i: (cur(hp, i), g * HP + hp)),
        pl.BlockSpec(pblk, lambda hp, i: (prv(hp, i), (NB + g) * HP + hp)),
        pl.BlockSpec(blk, lambda hp, i: (cur(hp, i), (NB + g) * HP + hp)),
        pl.BlockSpec(blk, lambda hp, i: (cur(hp, i), hp)),
        pl.BlockSpec(blk, lambda hp, i: (cur(hp, i), hp)),
        pl.BlockSpec(blk, lambda hp, i: (cur(hp, i), hp)),
        pl.BlockSpec(blk, lambda hp, i: (cur(hp, i), 0)),
        pl.BlockSpec(blk, lambda hp, i: (cur(hp, i), 0)),
    ]
    args = [q_all, kv_all, kv_all, kv_all, kv_all, do, o, lse, cos2, ss2]
    if dep is not None:
        in_specs.append(ANY)
        args.append(dep)
    aliases = {}
    if prev is not None:
        in_specs += [ANY, ANY, ANY]
        aliases = {len(args): 0, len(args) + 1: 1, len(args) + 2: 2}
        args += list(prev)
    wide = jax.ShapeDtypeStruct((T, NB * Dm), F32)
    out = pl.BlockSpec(blk, lambda hp, i: (cur(hp, i), g * HP + hp))
    return pl.pallas_call(
        body, name=name,
        grid=(HP, nblk),
        in_specs=in_specs,
        out_specs=[out, out, out],
        out_shape=[wide, wide, wide],
        scratch_shapes=[pltpu.VMEM((pr + R, LANES), F32), pltpu.VMEM((pr + R, LANES), F32),
                        pltpu.VMEM((d, BAND, LANES), F32), pltpu.VMEM((d, BAND, LANES), F32),
                        pltpu.VMEM((2, 2 * BAND, 2 * BAND), F32)],
        input_output_aliases=aliases,
        compiler_params=_cp("arbitrary", "arbitrary"),
    )(*args)


def _postnorm_bwd(dh, y, g_ref_val):
    r = lax.rsqrt(jnp.mean(y * y, axis=-1, keepdims=True) + RMS_EPS)
    yn = y * r
    dyn = dh * g_ref_val
    dy = r * (dyn - yn * jnp.mean(dyn * yn, axis=-1, keepdims=True))
    return dy, yn


def _after(body, n_in, dep):
    if dep is None:
        return body
    return lambda *refs: body(*refs[:n_in], *refs[n_in + 1:])


def _dep_spec(dep):
    return [] if dep is None else [ANY]


def _dep_arg(dep):
    return [] if dep is None else [dep]


def _postnorm_bwd_matmul(dh, y, gain, w3, widx, *, name, da_dtype, dep=None):
    T, D = dh.shape
    K = w3.shape[1]
    tm = min(ROW_BLOCK_WIDE, T)

    def body(dh_ref, y_ref, g_ref, w_ref, dy_ref, da_ref, dg_ref):
        i = pl.program_id(0)

        @pl.when(i == 0)
        def _():
            dg_ref[...] = jnp.zeros_like(dg_ref)

        dhv = dh_ref[...]
        dy, yn = _postnorm_bwd(dhv, y_ref[...].astype(F32), g_ref[...])
        dg_ref[...] += jnp.sum(dhv * yn, axis=0, keepdims=True)
        dyb = dy.astype(BF)
        dy_ref[...] = dyb
        da = lax.dot_general(dyb, w_ref[...], (((1,), (1,)), ((), ())), preferred_element_type=F32)
        da_ref[...] = da.astype(da_dtype)

    return pl.pallas_call(
        _after(body, 4, dep), name=name,
        grid=(T // tm,),
        in_specs=[
            pl.BlockSpec((tm, D), lambda i: (i, 0)),
            pl.BlockSpec((tm, D), lambda i: (i, 0)),
            pl.BlockSpec((1, D), lambda i: (0, 0)),
            pl.BlockSpec((None, K, D), lambda i: (widx, 0, 0)),
        ] + _dep_spec(dep),
        out_specs=[pl.BlockSpec((tm, D), lambda i: (i, 0)),
                   pl.BlockSpec((tm, K), lambda i: (i, 0)),
                   pl.BlockSpec((1, D), lambda i: (0, 0))],
        out_shape=[jax.ShapeDtypeStruct((T, D), BF), jax.ShapeDtypeStruct((T, K), da_dtype),
                   jax.ShapeDtypeStruct((1, D), F32)],
        compiler_params=_cp("arbitrary"),
    )(dh, y, gain, w3, *_dep_arg(dep))


def _postnorm_bwd_swiglu(dh, y, gain, wd3, layer, g, u, *, name, dep=None):
    T, D = dh.shape
    F = wd3.shape[1]
    nf = F // 2
    tm = min(ROW_BLOCK, T)

    def body(dh_ref, y_ref, g_ref, w_ref, gg_ref, uu_ref, dy_ref, dgo_ref, duo_ref, dgain_ref, dys):
        i = pl.program_id(0)
        j = pl.program_id(1)

        @pl.when((i == 0) & (j == 0))
        def _():
            dgain_ref[...] = jnp.zeros_like(dgain_ref)

        @pl.when(j == 0)
        def _():
            dhv = dh_ref[...]
            dy, yn = _postnorm_bwd(dhv, y_ref[...].astype(F32), g_ref[...])
            dgain_ref[...] += jnp.sum(dhv * yn, axis=0, keepdims=True)
            dyb = dy.astype(BF)
            dys[...] = dyb
            dy_ref[...] = dyb

        for c0, cw in _col_chunks(nf):
            cols = slice(c0, c0 + cw)
            da = lax.dot_general(dys[...], w_ref[cols, :], (((1,), (1,)), ((), ())), preferred_element_type=F32)
            gv = gg_ref[:, cols].astype(F32)
            uv = uu_ref[:, cols].astype(F32)
            sg = _sigmoid(gv)
            silu = gv * sg
            dgo_ref[:, cols] = (da * uv * (sg + silu * (1.0 - sg))).astype(BF)
            duo_ref[:, cols] = (da * silu).astype(BF)

    act = jax.ShapeDtypeStruct((T, F), BF)
    return pl.pallas_call(
        _after(body, 6, dep), name=name,
        grid=(T // tm, 2),
        in_specs=[
            pl.BlockSpec((tm, D), lambda i, j: (i, 0)),
            pl.BlockSpec((tm, D), lambda i, j: (i, 0)),
            pl.BlockSpec((1, D), lambda i, j: (0, 0)),
            pl.BlockSpec((None, nf, D), lambda i, j: (layer, j, 0)),
            pl.BlockSpec((tm, nf), lambda i, j: (i, j)),
            pl.BlockSpec((tm, nf), lambda i, j: (i, j)),
        ] + _dep_spec(dep),
        out_specs=[pl.BlockSpec((tm, D), lambda i, j: (i, 0)),
                   pl.BlockSpec((tm, nf), lambda i, j: (i, j)),
                   pl.BlockSpec((tm, nf), lambda i, j: (i, j)),
                   pl.BlockSpec((1, D), lambda i, j: (0, 0))],
        out_shape=[jax.ShapeDtypeStruct((T, D), BF), act, act, jax.ShapeDtypeStruct((1, D), F32)],
        scratch_shapes=[pltpu.VMEM((tm, D), BF)],
        compiler_params=_cp("arbitrary", "arbitrary"),
    )(dh, y, gain, wd3, g, u, *_dep_arg(dep))


def _matmul_prenorm_bwd(dzs, wg, layer, h, gain, dh_in, *, name):
    T, D = h.shape
    n = wg.shape[2]
    pair = len(dzs) == 2
    tm = min(ROW_BLOCK if pair else ROW_BLOCK_WIDE, T)
    nj = N_CHIPS // 2 if pair else N_CHIPS

    def body(*refs):
        dz_refs = refs[:len(dzs)]
        w_refs = refs[len(dzs):2 * len(dzs)]
        h_ref, g_ref, dhi_ref, dh_ref, dg_ref, acc = refs[2 * len(dzs):]
        i = pl.program_id(0)
        j = pl.program_id(1)

        @pl.when((i == 0) & (j == 0))
        def _():
            dg_ref[...] = jnp.zeros_like(dg_ref)

        part = None
        for dz_ref, w_ref in zip(dz_refs, w_refs):
            t = lax.dot_general(dz_ref[...].astype(BF), w_ref[...], (((1,), (1,)), ((), ())),
                                preferred_element_type=F32)
            part = t if part is None else part + t

        @pl.when(j == 0)
        def _():
            acc[...] = part

        @pl.when(j > 0)
        def _():
            acc[...] += part

        @pl.when(j == nj - 1)
        def _():
            dhn = acc[...]
            hv = h_ref[...]
            r = lax.rsqrt(jnp.mean(hv * hv, axis=-1, keepdims=True) + RMS_EPS)
            xh = hv * r
            dg_ref[...] += jnp.sum(dhn * xh, axis=0, keepdims=True)
            dxn = dhn * g_ref[...]
            dh_ref[...] = dhi_ref[...] + r * (dxn - xh * jnp.mean(dxn * xh, axis=-1, keepdims=True))

    in_specs = [pl.BlockSpec((tm, n), lambda i, j: (i, j)) for _ in dzs]
    if pair:
        in_specs += [pl.BlockSpec((None, D, n), lambda i, j: (j, layer, 0)),
                     pl.BlockSpec((None, D, n), lambda i, j: (j + nj, layer, 0))]
    else:
        in_specs += [pl.BlockSpec((None, D, n), lambda i, j: (j, layer, 0))]
    in_specs += [pl.BlockSpec((tm, D), lambda i, j: (i, 0)),
                 pl.BlockSpec((1, D), lambda i, j: (0, 0)),
                 pl.BlockSpec((tm, D), lambda i, j: (i, 0))]
    return pl.pallas_call(
        body, name=name,
        grid=(T // tm, nj),
        in_specs=in_specs,
        out_specs=[pl.BlockSpec((tm, D), lambda i, j: (i, 0)), pl.BlockSpec((1, D), lambda i, j: (0, 0))],
        out_shape=[jax.ShapeDtypeStruct((T, D), F32), jax.ShapeDtypeStruct((1, D), F32)],
        scratch_shapes=[pltpu.VMEM((tm, D), F32)],
        compiler_params=_cp("arbitrary", "arbitrary"),
    )(*dzs, *([wg] * len(dzs)), h, gain, dh_in)


def _grad_matmul(a, b, out_shape3, tme, tne, out_index, prev, *, name):
    T, M = a.shape
    N = b.shape[1]
    tk = min(ROW_BLOCK_WIDE, T)
    nk = T // tk

    def body(a_ref, b_ref, *rest):
        o_ref, acc = rest[-2:]
        k = pl.program_id(2)
        part = jnp.dot(a_ref[...].astype(BF).T, b_ref[...].astype(BF), preferred_element_type=F32)

        @pl.when(k == 0)
        def _():
            acc[...] = part

        @pl.when(k > 0)
        def _():
            acc[...] += part

        @pl.when(k == nk - 1)
        def _():
            o_ref[...] = acc[...].astype(BF)

    in_specs = [pl.BlockSpec((tk, tme), lambda i, j, k: (k, i)),
                pl.BlockSpec((tk, tne), lambda i, j, k: (k, j))]
    args = [a, b]
    aliases = {}
    if prev is not None:
        in_specs.append(ANY)
        args.append(prev)
        aliases = {2: 0}
    return pl.pallas_call(
        body, name=name,
        grid=(M // tme, N // tne, nk),
        in_specs=in_specs,
        out_specs=pl.BlockSpec((None, tme, tne), lambda i, j, k: out_index(i, j)),
        out_shape=jax.ShapeDtypeStruct(out_shape3, BF),
        scratch_shapes=[pltpu.VMEM((tme, tne), F32)],
        input_output_aliases=aliases,
        compiler_params=_cp("parallel", "parallel", "arbitrary"),
    )(*args)


def _row_tile(R, cap=512):
    fit = [t for t in range(16, min(R, cap) + 1, 16) if R % t == 0]
    return max(fit) if fit else R


def _cast_place(w3, layer, where, dtype, *, name, dep=None):
    _, R, C = w3.shape
    tr = _row_tile(R)

    def body(s_ref, w_ref, o_ref):
        o_ref[...] = w_ref[...].astype(o_ref.dtype)

    return pl.pallas_call(
        _after(body, 2, dep), name=name,
        grid_spec=pltpu.PrefetchScalarGridSpec(
            num_scalar_prefetch=1, grid=(R // tr,),
            in_specs=[pl.BlockSpec((None, tr, C), lambda i, s: (layer, i, 0))] + _dep_spec(dep),
            out_specs=pl.BlockSpec((None, tr, C), lambda i, s: (s[0], i, 0))),
        out_shape=jax.ShapeDtypeStruct((N_CHIPS, R, C), dtype),
        compiler_params=_cp("arbitrary"),
    )(where, w3, *_dep_arg(dep))


def _pair_sum(dw, theirs, where, *, name):
    G, rh, C = theirs.shape
    tr = _row_tile(rh)
    nr = rh // tr

    def body(s_ref, a_ref, b_ref, o_ref):
        o_ref[...] = (a_ref[...].astype(F32) + b_ref[...].astype(F32)).astype(BF)

    mine = pl.BlockSpec((None, tr, C), lambda g, i, s: (g, s[1] * nr + i, 0))
    spec = pl.BlockSpec((None, tr, C), lambda g, i, s: (g, i, 0))
    return pl.pallas_call(
        body, name=name,
        grid_spec=pltpu.PrefetchScalarGridSpec(
            num_scalar_prefetch=1, grid=(G, nr), in_specs=[mine, spec], out_specs=spec),
        out_shape=jax.ShapeDtypeStruct((G, rh, C), BF),
        compiler_params=_cp("arbitrary", "arbitrary"),
    )(where, dw, theirs)


def _chip_sum(landed, parts, where, total_rows, row_off, prev, *, name):
    G, rh, C = landed.shape
    tr = _row_tile(rh)
    nr = rh // tr
    base = row_off // tr

    def body(s_ref, l_ref, p_ref, *rest):
        o_ref = rest[-1]
        for j in range(G):
            def own(j=j):
                v = p_ref[...].astype(F32)
                o_ref[...] = v if j == 0 else o_ref[...] + v

            def other(j=j):
                v = l_ref[j].astype(F32)
                o_ref[...] = v if j == 0 else o_ref[...] + v

            pl.when(s_ref[0] == j)(own)
            pl.when(s_ref[0] != j)(other)

    in_specs = [pl.BlockSpec((G, tr, C), lambda i, s: (0, i, 0)),
                pl.BlockSpec((None, tr, C), lambda i, s: (s[0], i, 0))]
    args = [where, landed, parts]
    aliases = {}
    if prev is not None:
        in_specs.append(ANY)
        args.append(prev)
        aliases = {3: 0}
    return pl.pallas_call(
        body, name=name,
        grid_spec=pltpu.PrefetchScalarGridSpec(
            num_scalar_prefetch=1, grid=(nr,),
            in_specs=in_specs,
            out_specs=pl.BlockSpec((tr, C), lambda i, s: (base + s[1] * nr + i, 0))),
        out_shape=jax.ShapeDtypeStruct((total_rows, C), F32),
        input_output_aliases=aliases,
        compiler_params=_cp("arbitrary"),
    )(*args)


def _adamw(w, g, m, v, *, name, emit_grad=False):
    R, C = w.shape
    tr = _row_tile(R, cap=max(16, ADAMW_BLOCK_BYTES // (4 * C)))
    n_out = 4 if emit_grad else 3

    def body(w_ref, g_ref, m_ref, v_ref, d_ref, mo_ref, vo_ref, *go_ref):
        gv = g_ref[...]
        if emit_grad:
            go_ref[0][...] = gv
        mn = ADAM_B1 * m_ref[...] + (1.0 - ADAM_B1) * gv
        vn = ADAM_B2 * v_ref[...] + (1.0 - ADAM_B2) * jnp.square(gv)
        m_hat = mn / (1.0 - ADAM_B1 ** ADAM_STEP)
        v_hat = vn / (1.0 - ADAM_B2 ** ADAM_STEP)
        d_ref[...] = -ADAM_LR * (m_hat / (jnp.sqrt(v_hat) + ADAM_EPS) + ADAM_WD * w_ref[...])
        mo_ref[...] = mn
        vo_ref[...] = vn

    spec = pl.BlockSpec((tr, C), lambda i: (i, 0))
    shp = jax.ShapeDtypeStruct((R, C), F32)
    return pl.pallas_call(
        body, name=name, grid=(R // tr,), in_specs=[spec] * 4, out_specs=[spec] * n_out,
        out_shape=[shp] * n_out, compiler_params=_cp("parallel"),
    )(w, g, m, v)


def _place():
    x = lax.axis_index("x")
    y = lax.axis_index("y")
    c = lax.axis_index("c")
    chips = [(1 - x, y), (x, 1 - y), (1 - x, 1 - y)]
    return x, y, c, chips


def _chunk_rows(rows, row_bytes, align):
    if rows <= align:
        return rows
    cands = [r for r in range(align, rows + 1, align) if rows % r == 0]
    fit = [r for r in cands if r * row_bytes <= DMA_CHUNK_BYTES]
    return max(fit) if fit else min(cands)


def _row_align(dtype):
    return 8 * (4 // jnp.dtype(dtype).itemsize)


def _start_chunks(make, rows, rc):
    for r0 in range(0, rows, rc):
        make(r0, rc).start()


def _piece_rows(ref, piece, j, h, r0=0, n=None):
    _, lead, off, rows = piece
    rh = rows // 2
    n = rh if n is None else n
    if lead is not None:
        return ref.at[lead, j, pl.ds(h * rh + r0, n)]
    return ref.at[j, pl.ds(off + h * rh + r0, n)]


def _piece_chunk(arr, piece):
    rh = piece[3] // 2
    return rh, _chunk_rows(rh, arr.shape[-1] * arr.dtype.itemsize, _row_align(arr.dtype))


def _ag_start(arrays, taps, groups, *, name):
    na = len(arrays)
    ng = len(groups)
    nt = 0 if taps is None else 1
    n_sem = [3 * len(grp) + (3 if nt and g == 0 else 0) for g, grp in enumerate(groups)]

    def body(*refs):
        ins = refs[:na]
        taps_ref = refs[na] if nt else None
        sems = refs[na + nt:na + nt + 2 * ng]
        token = refs[-1]
        token[...] = jnp.zeros_like(token)
        x, y, c, chips = _place()
        myj = 2 * x + y
        for g, grp in enumerate(groups):
            ssem, rsem = sems[2 * g], sems[2 * g + 1]
            for idx, piece in enumerate(grp):
                ref = ins[piece[0]]
                rh, rc = _piece_chunk(arrays[piece[0]], piece)
                for k, (px, py) in enumerate(chips):
                    def send(r0, n, ref=ref, piece=piece, idx=idx, k=k, px=px, py=py, ssem=ssem, rsem=rsem):
                        part = _piece_rows(ref, piece, myj, c, r0, n)
                        return pltpu.make_async_remote_copy(
                            src_ref=part, dst_ref=part, send_sem=ssem.at[3 * idx + k], recv_sem=rsem.at[3 * idx + k],
                            device_id=(px, py, c), device_id_type=MESH)
                    _start_chunks(send, rh, rc)
            if nt and g == 0:
                for k, (px, py) in enumerate(chips):
                    pltpu.make_async_remote_copy(
                        src_ref=taps_ref.at[myj], dst_ref=taps_ref.at[myj],
                        send_sem=ssem.at[3 * len(grp) + k], recv_sem=rsem.at[3 * len(grp) + k],
                        device_id=(px, py, c), device_id_type=MESH).start()

    sem_shapes = []
    for n in n_sem:
        sem_shapes += [pltpu.SemaphoreType.DMA((n,)), pltpu.SemaphoreType.DMA((n,))]
    ops = list(arrays) + ([taps] if nt else [])
    bufs = [pltpu.HBM(a.shape, a.dtype) for a in ops]
    outs = pl.pallas_call(
        body, name=name,
        out_shape=(*sem_shapes, *bufs, jax.ShapeDtypeStruct((8, LANES), F32)),
        in_specs=[HBM] * (na + nt),
        out_specs=(*([SEM] * (2 * ng)), *([HBM] * (na + nt)), pl.BlockSpec(memory_space=pltpu.VMEM)),
        input_output_aliases={i: 2 * ng + i for i in range(na + nt)},
        compiler_params=pltpu.CompilerParams(has_side_effects=EFFECT),
    )(*[_in_hbm(a) for a in ops])
    sems = [(outs[2 * g], outs[2 * g + 1]) for g in range(ng)]
    return sems, list(outs[2 * ng:2 * ng + na]), (outs[2 * ng + na] if nt else None), outs[-1]


def _ag_wait(sems, vals, taps, group, after, *, name):
    nv = len(vals)
    extra = ([taps] if taps is not None else [])
    nb = nv + len(extra)

    def body(*refs):
        bufs = refs[:nb]
        ssem, rsem = refs[nb], refs[nb + 1]
        x, y, c, chips = _place()
        for idx, piece in enumerate(group):
            for k, (px, py) in enumerate(chips):
                got = _piece_rows(bufs[piece[0]], piece, 2 * px + py, c)
                cp = pltpu.make_async_remote_copy(
                    src_ref=got, dst_ref=got, send_sem=ssem.at[3 * idx + k], recv_sem=rsem.at[3 * idx + k],
                    device_id=(px, py, c), device_id_type=MESH)
                cp.wait_send()
                cp.wait_recv()
        if taps is not None:
            for k, (px, py) in enumerate(chips):
                got = bufs[nv].at[2 * px + py]
                cp = pltpu.make_async_remote_copy(
                    src_ref=got, dst_ref=got, send_sem=ssem.at[3 * len(group) + k],
                    recv_sem=rsem.at[3 * len(group) + k], device_id=(px, py, c), device_id_type=MESH)
                cp.wait_send()
                cp.wait_recv()

    ops = list(vals) + extra
    shapes = [pltpu.HBM(a.shape, a.dtype) for a in ops]
    outs = pl.pallas_call(
        body, name=name,
        out_shape=tuple(shapes),
        in_specs=[HBM] * nb + [SEM, SEM] + _dep_spec(after),
        out_specs=[HBM] * nb,
        input_output_aliases={i: i for i in range(nb)},
        compiler_params=pltpu.CompilerParams(has_side_effects=EFFECT),
    )(*ops, sems[0], sems[1], *_dep_arg(after))
    return list(outs[:nv]), (outs[nv] if taps is not None else None)


def _ag_forward(vals, group, *, name):
    nv = len(vals)
    npc = len(group)

    def body(*refs):
        bufs = refs[nv:2 * nv]
        fsem, gsem = refs[2 * nv:]
        x, y, c, chips = _place()
        sib = (x, y, 1 - c)
        sent = []
        for idx, piece in enumerate(group):
            rh, rc = _piece_chunk(vals[piece[0]], piece)
            for k, (px, py) in enumerate(chips):
                def fwd(r0, n, piece=piece, idx=idx, k=k, pj=2 * px + py):
                    part = _piece_rows(bufs[piece[0]], piece, pj, c, r0, n)
                    return pltpu.make_async_remote_copy(
                        src_ref=part, dst_ref=part, send_sem=fsem.at[3 * idx + k], recv_sem=gsem.at[3 * idx + k],
                        device_id=sib, device_id_type=MESH)
                _start_chunks(fwd, rh, rc)
                sent.append(fwd(0, rh))
        for idx, piece in enumerate(group):
            for k, (px, py) in enumerate(chips):
                theirs = _piece_rows(bufs[piece[0]], piece, 2 * px + py, 1 - c)
                pltpu.make_async_remote_copy(
                    src_ref=theirs, dst_ref=theirs, send_sem=fsem.at[3 * idx + k], recv_sem=gsem.at[3 * idx + k],
                    device_id=sib, device_id_type=MESH).wait_recv()
        for cp in sent:
            cp.wait_send()

    return pl.pallas_call(
        body, name=name,
        in_specs=[ANY] * nv, out_specs=[ANY] * nv,
        out_shape=[jax.ShapeDtypeStruct(v.shape, v.dtype) for v in vals],
        input_output_aliases={i: i for i in range(nv)},
        scratch_shapes=[pltpu.SemaphoreType.DMA((3 * npc,)), pltpu.SemaphoreType.DMA((3 * npc,))],
    )(*vals)


def _sibling_swap(dws, *, name):
    nm = len(dws)
    shapes = [jax.ShapeDtypeStruct((dw.shape[0], dw.shape[1] // 2, dw.shape[2]), dw.dtype) for dw in dws]

    def body(*refs):
        ins = refs[:nm]
        theirs = refs[nm:2 * nm]
        ssem, rsem = refs[2 * nm:]
        x, y, c, _ = _place()
        sib = (x, y, 1 - c)
        cps = []
        for m in range(nm):
            G, rh, cols = shapes[m].shape
            rc = _chunk_rows(rh, cols * shapes[m].dtype.itemsize, _row_align(shapes[m].dtype))
            for j in range(G):
                _start_chunks(lambda r0, n, m=m, j=j, rh=rh: pltpu.make_async_remote_copy(
                    src_ref=ins[m].at[j, pl.ds((1 - c) * rh + r0, n)],
                    dst_ref=theirs[m].at[j, pl.ds(r0, n)], send_sem=ssem.at[m], recv_sem=rsem.at[m],
                    device_id=sib, device_id_type=MESH), rh, rc)
            cps.append(pltpu.make_async_remote_copy(
                src_ref=ins[m].at[:, pl.ds((1 - c) * rh, rh), :], dst_ref=theirs[m],
                send_sem=ssem.at[m], recv_sem=rsem.at[m], device_id=sib, device_id_type=MESH))
        for cp in cps:
            cp.wait()

    return pl.pallas_call(
        body, name=name,
        in_specs=[ANY] * nm, out_specs=[ANY] * nm, out_shape=shapes,
        scratch_shapes=[pltpu.SemaphoreType.DMA((nm,)), pltpu.SemaphoreType.DMA((nm,))],
    )(*dws)


HBM = pl.BlockSpec(memory_space=pltpu.HBM)
SEM = pl.BlockSpec(memory_space=pltpu.SEMAPHORE)
EFFECT = pltpu.SideEffectType.DATAFLOW_SIDE_EFFECTING


def _in_hbm(a):
    return pltpu.with_memory_space_constraint(a, pltpu.HBM)


def _exchange_start(parts, *, name):
    nm = len(parts)

    def body(*refs):
        ins = refs[:nm]
        lands = refs[nm:2 * nm]
        ssem, rsem = refs[2 * nm:2 * nm + 2]
        token = refs[-1]
        x, y, c, chips = _place()
        myj = 2 * x + y
        for m in range(nm):
            _, rh, cols = parts[m].shape
            rc = _chunk_rows(rh, cols * parts[m].dtype.itemsize, _row_align(parts[m].dtype))
            for k, (px, py) in enumerate(chips):
                _start_chunks(lambda r0, n, m=m, k=k, px=px, py=py: pltpu.make_async_remote_copy(
                    src_ref=ins[m].at[2 * px + py, pl.ds(r0, n)], dst_ref=lands[m].at[myj, pl.ds(r0, n)],
                    send_sem=ssem.at[3 * m + k], recv_sem=rsem.at[3 * m + k],
                    device_id=(px, py, c), device_id_type=MESH), rh, rc)
        token[...] = jnp.zeros_like(token)

    bufs = [pltpu.HBM(p.shape, p.dtype) for p in parts]
    outs = pl.pallas_call(
        body, name=name,
        out_shape=(pltpu.SemaphoreType.DMA((3 * nm,)), pltpu.SemaphoreType.DMA((3 * nm,)), *bufs, *bufs,
                   jax.ShapeDtypeStruct((8, LANES), F32)),
        in_specs=[HBM] * (2 * nm),
        out_specs=(SEM, SEM, *([HBM] * (2 * nm)), pl.BlockSpec(memory_space=pltpu.VMEM)),
        input_output_aliases={i: 2 + i for i in range(2 * nm)},
        compiler_params=pltpu.CompilerParams(has_side_effects=EFFECT),
    )(*[_in_hbm(p) for p in parts], *[_in_hbm(lax.empty(p.shape, p.dtype)) for p in parts])
    return (outs[0], outs[1], list(outs[2:2 + nm]), list(outs[2 + nm:2 + 2 * nm])), outs[-1]


def _exchange_wait(handle, after, *, name):
    ssem_in, rsem_in, parts, lands = handle
    nm = len(parts)

    def body(*refs):
        ins = refs[:nm]
        lnd = refs[nm:2 * nm]
        ssem, rsem = refs[2 * nm:2 * nm + 2]
        x, y, c, chips = _place()
        for m in range(nm):
            for k, (px, py) in enumerate(chips):
                pj = 2 * px + py
                cp = pltpu.make_async_remote_copy(
                    src_ref=ins[m].at[pj], dst_ref=lnd[m].at[pj],
                    send_sem=ssem.at[3 * m + k], recv_sem=rsem.at[3 * m + k],
                    device_id=(px, py, c), device_id_type=MESH)
                cp.wait_send()
                cp.wait_recv()

    bufs = [pltpu.HBM(p.shape, p.dtype) for p in parts]
    outs = pl.pallas_call(
        body, name=name,
        out_shape=(*bufs, *bufs),
        in_specs=[HBM] * (2 * nm) + [SEM, SEM, ANY],
        out_specs=[HBM] * (2 * nm),
        input_output_aliases={i: i for i in range(2 * nm)},
        compiler_params=pltpu.CompilerParams(has_side_effects=EFFECT),
    )(*parts, *lands, ssem_in, rsem_in, after)
    return list(outs[nm:]), list(outs[:nm])


def _sibling_join(grads, regions, *, name):
    nm = len(grads)
    nr = len(regions)
    shapes = [jax.ShapeDtypeStruct(g.shape, g.dtype) for g in grads]

    def body(*refs):
        outs = refs[nm:2 * nm]
        ssem, rsem = refs[2 * nm:]
        x, y, c, _ = _place()
        sib = (x, y, 1 - c)
        cps = []
        for i, (m, off, rows) in enumerate(regions):
            rh, cols = rows // 2, grads[m].shape[1]
            rc = _chunk_rows(rh, cols * grads[m].dtype.itemsize, _row_align(grads[m].dtype))

            def send(r0, n, i=i, m=m, off=off, rh=rh):
                part = outs[m].at[pl.ds(off + c * rh + r0, n)]
                return pltpu.make_async_remote_copy(
                    src_ref=part, dst_ref=part, send_sem=ssem.at[i], recv_sem=rsem.at[i],
                    device_id=sib, device_id_type=MESH)
            _start_chunks(send, rh, rc)
            cps.append(send(0, rh))
        for i, (m, off, rows) in enumerate(regions):
            rh = rows // 2
            cps[i].wait_send()
            got = outs[m].at[pl.ds(off + (1 - c) * rh, rh)]
            pltpu.make_async_remote_copy(
                src_ref=got, dst_ref=got, send_sem=ssem.at[i], recv_sem=rsem.at[i],
                device_id=sib, device_id_type=MESH).wait_recv()

    return pl.pallas_call(
        body, name=name,
        in_specs=[ANY] * nm, out_specs=[ANY] * nm, out_shape=shapes,
        input_output_aliases={i: i for i in range(nm)},
        scratch_shapes=[pltpu.SemaphoreType.DMA((nr,)), pltpu.SemaphoreType.DMA((nr,))],
    )(*grads)


def _all_reduce_small(pack):
    R, C = pack.shape

    def body(in_ref, out_ref, slots, ssem, rsem):
        x, y, c, _ = _place()
        me = 4 * x + 2 * y + c
        slots[me] = in_ref[...]
        cps = []
        for k in range(1, N_DEV):
            dx, dy, dc = (k >> 2) & 1, (k >> 1) & 1, k & 1
            peer = (x ^ dx, y ^ dy, c ^ dc)
            cp = pltpu.make_async_remote_copy(
                src_ref=in_ref, dst_ref=slots.at[me], send_sem=ssem.at[k], recv_sem=rsem.at[k],
                device_id=peer, device_id_type=MESH)
            cp.start()
            cps.append(cp)
        for k in range(1, N_DEV):
            dx, dy, dc = (k >> 2) & 1, (k >> 1) & 1, k & 1
            got = slots.at[4 * (x ^ dx) + 2 * (y ^ dy) + (c ^ dc)]
            pltpu.make_async_remote_copy(
                src_ref=got, dst_ref=got, send_sem=ssem.at[k], recv_sem=rsem.at[k],
                device_id=(x ^ dx, y ^ dy, c ^ dc), device_id_type=MESH).wait_recv()
        for cp in cps:
            cp.wait_send()
        acc = slots[0]
        for s in range(1, N_DEV):
            acc = acc + slots[s]
        out_ref[...] = acc

    return pl.pallas_call(
        body, name="ar_small",
        in_specs=[pl.BlockSpec(memory_space=pltpu.VMEM)],
        out_specs=pl.BlockSpec(memory_space=pltpu.VMEM),
        out_shape=jax.ShapeDtypeStruct((R, C), F32),
        scratch_shapes=[pltpu.VMEM((N_DEV, R, C), F32),
                        pltpu.SemaphoreType.DMA((N_DEV,)), pltpu.SemaphoreType.DMA((N_DEV,))],
    )(pack)


def kernel(x, positions, mix_norm_pre, mix_norm_post, ffn_norm_pre, ffn_norm_post, ffn_w_gate_up, ffn_w_down, conv_w_in, conv_w, conv_w_out, kv_norm, w_kv, w_q, w_o, loss_target, m_mix_norm_pre, m_mix_norm_post, m_ffn_norm_pre, m_ffn_norm_post, m_ffn_w_gate_up, m_ffn_w_down, m_conv_w_in, m_conv_w, m_conv_w_out, m_kv_norm, m_w_kv, m_w_q, m_w_o, v_mix_norm_pre, v_mix_norm_post, v_ffn_norm_pre, v_ffn_norm_post, v_ffn_w_gate_up, v_ffn_w_down, v_conv_w_in, v_conv_w, v_conv_w_out, v_kv_norm, v_w_kv, v_w_q, v_w_o):
    T, D = x.shape[1], x.shape[2]
    L = ffn_w_gate_up.shape[0]
    n_gu = ffn_w_gate_up.shape[2]
    f_sh = ffn_w_down.shape[1]
    F = N_CHIPS * f_sh
    x0 = x[0]
    tgt = loss_target[0]

    half = HEAD_DIM // 2
    inv_freq = ROPE_THETA ** (-jnp.arange(half, dtype=F32) / half)
    ang = positions[0].astype(F32)[:, None] * inv_freq
    cosv, sinv = jnp.cos(ang), jnp.sin(ang)
    cos2 = jnp.tile(cosv, (1, LANES // half))
    ss2 = jnp.tile(jnp.concatenate([-sinv, sinv], axis=1), (1, LANES // HEAD_DIM))

    def as2d(a):
        return a.reshape(-1, a.shape[-1])

    big = [ffn_w_gate_up, ffn_w_down, conv_w_in, conv_w_out, w_kv, w_q, w_o]
    big_m = [m_ffn_w_gate_up, m_ffn_w_down, m_conv_w_in, m_conv_w_out, m_w_kv, m_w_q, m_w_o]
    big_v = [v_ffn_w_gate_up, v_ffn_w_down, v_conv_w_in, v_conv_w_out, v_w_kv, v_w_q, v_w_o]
    chip = 2 * lax.axis_index("x") + lax.axis_index("y")
    where = jnp.stack([chip, lax.axis_index("c")]).astype(jnp.int32)
    tc = conv_w.shape[2]
    cw_pad = jnp.concatenate([conv_w[0], jnp.zeros((8 - conv_w.shape[1], tc), F32)], axis=0)

    GU0, GU1, WD0, WD1, WCI, WCO, WKV, WQ, WO = range(9)
    shards = [(ffn_w_gate_up, 0), (ffn_w_gate_up, 1), (ffn_w_down, 0), (ffn_w_down, 1), (conv_w_in, 0),
              (conv_w_out, 0), (w_kv[None], 0), (w_q, 0), (w_o, 0)]
    ag_groups = [
        [(WCI, None, 0, D), (WCO, None, 0, D // N_CHIPS)],
        [(GU0, None, 0, D), (WD0, None, 0, f_sh)],
        [(WKV, None, 0, D), (WQ, None, 0, D)],
        [(WO, None, 0, D // N_CHIPS), (GU1, None, 0, D), (WD1, None, 0, f_sh)],
    ]

    def localised(group, idxs):
        return [(idxs.index(p[0]),) + p[1:] for p in group]

    cur = [None] * len(shards)
    first = [WCI, WCO]
    sems0, vals, taps, token = _ag_start(
        [_cast_place(*shards[i], where, BF, name=f"place{i}") for i in first],
        _cast_place(cw_pad[None], 0, where, F32, name="place_taps"), [localised(ag_groups[0], first)],
        name="ag_start0")
    for i, v in zip(first, vals):
        cur[i] = v
    rest = [i for i in range(len(shards)) if i not in first]
    sems1, vals, _, _ = _ag_start(
        [_cast_place(*shards[i], where, BF, name=f"place{i}", dep=token) for i in rest], None,
        [localised(g, rest) for g in ag_groups[1:]], name="ag_start1")
    for i, v in zip(rest, vals):
        cur[i] = v
    ag_sems = sems0 + sems1

    def gather_group(g, after):
        nonlocal taps
        idxs = sorted({p[0] for p in ag_groups[g]})
        local = localised(ag_groups[g], idxs)
        vals, landed_taps = _ag_wait(ag_sems[g], [cur[i] for i in idxs], taps if g == 0 else None, local, after,
                                     name=f"ag_wait{g}")
        if g == 0:
            taps = landed_taps
        vals = _ag_forward(vals, local, name=f"ag_forward{g}")
        for i, v in zip(idxs, vals):
            cur[i] = v

    def row(a, i):
        return a[i:i + 1]

    gather_group(0, None)
    wci, wco, cw = cur[WCI], cur[WCO].reshape(1, D, D), taps
    z, hn_m0 = _norm_matmul(x0, row(mix_norm_pre, 0), wci, cos2, ss2, name="f0_conv_in",
                            rope_shards=0, scale=1.0, out_dtype=BF)
    vmix = _conv_fwd(z, cw, name="f0_conv")
    y0, h1 = _matmul_postnorm(vmix, wco, 0, row(mix_norm_post, 0), x0, name="f0_conv_out")
    gather_group(1, h1)
    wgu0, wd0 = cur[GU0], cur[WD0].reshape(1, F, D)
    g0, u0, a0, hn_f0 = _norm_swiglu(h1, row(ffn_norm_pre, 0), wgu0, 0, name="f0_gate_up")
    f0, h2 = _matmul_postnorm(a0, wd0, 0, row(ffn_norm_post, 0), h1, name="f0_down")

    gather_group(2, h2)
    wkv, wq = cur[WKV], cur[WQ]
    kv_all, hn_kv = _norm_matmul(h2, kv_norm.reshape(1, D), wkv, cos2, ss2, name="f1_kv",
                                 rope_shards=N_CHIPS // 2, scale=1.0, out_dtype=F32)
    q_all, hn_m1 = _norm_matmul(h2, row(mix_norm_pre, 1), wq, cos2, ss2, name="f1_q",
                                rope_shards=N_CHIPS, scale=HEAD_DIM ** -0.5, out_dtype=F32)
    o_att, lse = _attn_fwd(q_all, kv_all, name="f1_attn")
    gather_group(3, o_att)
    wgu1, wd1, wo = cur[GU1], cur[WD1].reshape(1, F, D), cur[WO].reshape(1, D, D)
    y1, h3 = _matmul_postnorm(o_att, wo, 0, row(mix_norm_post, 1), h2, name="f1_attn_out")
    g1, u1, a1, hn_f1 = _norm_swiglu(h3, row(ffn_norm_pre, 1), wgu1, 0, name="f1_gate_up")
    f1, h4 = _matmul_postnorm(a1, wd1, 0, row(ffn_norm_post, 1), h3, name="f1_down")

    dh4, sq = _loss_head(h4, tgt, name="loss_head")
    loss_part = 0.5 * sq[0, 0] / D

    gu_shape = (N_CHIPS, D, n_gu)
    in_chips = lambda a: a.reshape(N_CHIPS, -1, a.shape[-1])

    def scatter_start(dws, tag):
        theirs = _sibling_swap(dws, name=f"rs_swap_{tag}")
        parts = [_pair_sum(dw, t, where, name=f"rs_pair_sum_{tag}{i}") for i, (dw, t) in enumerate(zip(dws, theirs))]
        return _exchange_start(parts, name=f"rs_exchange_start_{tag}")

    dyf1, dg1, du1, d_ffn_post1 = _postnorm_bwd_swiglu(dh4, f1, row(ffn_norm_post, 1), wd1, 0, g1, u1,
                                                       name="b1_down")
    dwd1 = _grad_matmul(a1, dyf1, (2, F // 2, D), F // 2, D, lambda i, j: (i, 0, 0), None, name="b1_dw_down")
    dwgu1 = _grad_matmul(hn_f1, dg1, gu_shape, D, n_gu, lambda i, j: (j, 0, 0), None, name="b1_dw_gate")
    dwgu1 = _grad_matmul(hn_f1, du1, gu_shape, D, n_gu, lambda i, j: (j + 2, 0, 0), dwgu1, name="b1_dw_up")
    dh3, d_ffn_pre1 = _matmul_prenorm_bwd((dg1, du1), wgu1, 0, h3, row(ffn_norm_pre, 1), dh4, name="b1_gate_up")

    dy1, do, d_mix_post1 = _postnorm_bwd_matmul(dh3, y1, row(mix_norm_post, 1), wo, 0, name="b1_attn_out",
                                                da_dtype=F32)
    dwo = _grad_matmul(o_att, dy1, (1, D, D), D, D, lambda i, j: (0, 0, 0), None, name="b1_dw_o")
    rs_a, token = scatter_start([dwgu1, in_chips(dwd1), in_chips(dwo)], "a")
    prev = None
    for gi, (window, dil) in enumerate(BRANCHES):
        prev = _attn_bwd(q_all, kv_all, do, o_att, lse, cos2, ss2, gi, dil, prev, name=f"b1_attn{gi}",
                         dep=token if gi == 0 else None)
    dq_all, dk_all, dv_all = prev
    n_q = wq.shape[2]
    n_kv = wkv.shape[2]
    dwq = _grad_matmul(hn_m1, dq_all, (N_CHIPS, D, n_q), D, n_q, lambda i, j: (j, 0, 0), None, name="b1_dw_q")
    dwkv = _grad_matmul(hn_kv, dk_all, (N_CHIPS, D, n_kv), D, n_kv, lambda i, j: (j, 0, 0), None, name="b1_dw_k")
    dwkv = _grad_matmul(hn_kv, dv_all, (N_CHIPS, D, n_kv), D, n_kv, lambda i, j: (j + 2, 0, 0), dwkv, name="b1_dw_v")
    dh2, d_mix_pre1 = _matmul_prenorm_bwd((dq_all,), wq, 0, h2, row(mix_norm_pre, 1), dh3, name="b1_q")
    dh2, d_kv_norm = _matmul_prenorm_bwd((dk_all, dv_all), wkv, 0, h2, kv_norm.reshape(1, D), dh2, name="b1_kv")
    rs_b, token = scatter_start([dwkv, dwq], "b")

    dyf0, dg0, du0, d_ffn_post0 = _postnorm_bwd_swiglu(dh2, f0, row(ffn_norm_post, 0), wd0, 0, g0, u0,
                                                       name="b0_down", dep=token)
    dwd0 = _grad_matmul(a0, dyf0, (2, F // 2, D), F // 2, D, lambda i, j: (i, 0, 0), None, name="b0_dw_down")
    dwgu0 = _grad_matmul(hn_f0, dg0, gu_shape, D, n_gu, lambda i, j: (j, 0, 0), None, name="b0_dw_gate")
    dwgu0 = _grad_matmul(hn_f0, du0, gu_shape, D, n_gu, lambda i, j: (j + 2, 0, 0), dwgu0, name="b0_dw_up")
    dh1, d_ffn_pre0 = _matmul_prenorm_bwd((dg0, du0), wgu0, 0, h1, row(ffn_norm_pre, 0), dh2, name="b0_gate_up")
    rs_c, token = scatter_start([dwgu0, in_chips(dwd0)], "c")

    dy0, dvmix, d_mix_post0 = _postnorm_bwd_matmul(dh1, y0, row(mix_norm_post, 0), wco, 0, name="b0_conv_out",
                                                   da_dtype=BF, dep=token)
    dwco = _grad_matmul(vmix, dy0, (1, D, D), D, D, lambda i, j: (0, 0, 0), None, name="b0_dw_conv_out")
    dz, dcw = _conv_bwd(z, cw, dvmix, name="b0_conv")
    n_ci = wci.shape[2]
    dwci = _grad_matmul(hn_m0, dz, (N_CHIPS, D, n_ci), D, n_ci, lambda i, j: (j, 0, 0), None, name="b0_dw_conv_in")
    dx, d_mix_pre0 = _matmul_prenorm_bwd((dz,), wci, 0, x0, row(mix_norm_pre, 0), dh1, name="b0_conv_in")

    pack = jnp.concatenate([
        d_mix_pre0, d_mix_pre1, d_mix_post0, d_mix_post1, d_ffn_pre0, d_ffn_pre1, d_ffn_post0, d_ffn_post1,
        d_kv_norm, dcw[0:3], jnp.full((1, D), loss_part, F32),
        jnp.zeros((SMALL_ROWS - 13, D), F32)], axis=0)
    red = _all_reduce_small(pack)
    loss = red[12, 0]
    myj = 2 * lax.axis_index("x") + lax.axis_index("y")
    g_conv_w = lax.dynamic_slice(red, (9, myj * tc), (3, tc))

    zeros7 = jnp.zeros((SMALL_ROWS - 9, D), F32)
    w_small = jnp.concatenate([mix_norm_pre, mix_norm_post, ffn_norm_pre, ffn_norm_post, kv_norm.reshape(1, D), zeros7], axis=0)
    m_small = jnp.concatenate([m_mix_norm_pre, m_mix_norm_post, m_ffn_norm_pre, m_ffn_norm_post, m_kv_norm.reshape(1, D), zeros7], axis=0)
    v_small = jnp.concatenate([v_mix_norm_pre, v_mix_norm_post, v_ffn_norm_pre, v_ffn_norm_post, v_kv_norm.reshape(1, D), zeros7], axis=0)
    d_small, nm_small, nv_small = _adamw(w_small, red, m_small, v_small, name="adamw_small")

    pad5 = jnp.zeros((5, tc), F32)
    d_cw, nm_cw, nv_cw = _adamw(cw_pad, jnp.concatenate([g_conv_w, pad5], axis=0),
                                jnp.concatenate([m_conv_w[0], pad5], axis=0),
                                jnp.concatenate([v_conv_w[0], pad5], axis=0), name="adamw_conv_w")

    rs_d, _ = scatter_start([dwci, in_chips(dwco)], "d")

    pieces = {"a": [(0, D), (1, f_sh), (6, 0)], "b": [(4, 0), (5, 0)], "c": [(0, 0), (1, 0)], "d": [(2, 0), (3, 0)]}
    grads2d = [None] * len(big)
    big_out = [None] * len(big)

    def finish(groups, after, tag):
        regions, idxs = [], []
        for gtag, handle in groups:
            landed, parts = _exchange_wait(handle, after, name=f"rs_exchange_wait_{gtag}")
            for i, (l, p, (wi, off)) in enumerate(zip(landed, parts, pieces[gtag])):
                total = as2d(big[wi]).shape[0]
                grads2d[wi] = _chip_sum(l, p, where, total, off, grads2d[wi], name=f"rs_chip_sum_{gtag}{i}")
                if wi not in idxs:
                    idxs.append(wi)
                regions.append((idxs.index(wi), off, 2 * l.shape[1]))
        joined = _sibling_join([grads2d[wi] for wi in idxs], regions, name=f"rs_sibling_join_{tag}")
        for wi, gr in zip(idxs, joined):
            w = big[wi]
            d_, m_, v_, g_ = _adamw(as2d(w), gr, as2d(big_m[wi]), as2d(big_v[wi]), name=f"adamw{wi}",
                                    emit_grad=True)
            big_out[wi] = (g_.reshape(w.shape), d_.reshape(w.shape), m_.reshape(w.shape), v_.reshape(w.shape))

    finish([("a", rs_a), ("b", rs_b), ("c", rs_c)], dx, "abc")
    finish([("d", rs_d)], big_out[0][1], "d")

    def small(a):
        return (a[0:2], a[2:4], a[4:6], a[6:8])

    def assemble(sm, cwv, kind):
        pre, post, fpre, fpost = small(sm)
        b = [t[kind] for t in big_out]
        return [pre, post, fpre, fpost, b[0], b[1], b[2], cwv[0:3].reshape(conv_w.shape), b[3],
                sm[8], b[4], b[5].reshape(w_q.shape), b[6].reshape(w_o.shape)]

    grads = assemble(red, jnp.concatenate([g_conv_w, pad5], axis=0), 0)
    deltas = assemble(d_small, d_cw, 1)
    new_m = assemble(nm_small, nm_cw, 2)
    new_v = assemble(nv_small, nv_cw, 3)
    return (loss, dx.reshape(x.shape), *grads, *deltas, *new_m, *new_v)
```

```python
import functools

import jax
import jax.numpy as jnp
from jax import lax
from jax.experimental import pallas as pl
from jax.experimental.pallas import tpu as pltpu

HEAD_DIM = 64
BAND = 128
BRANCHES = ((128, 1), (512, 4), (2048, 16))
ROPE_THETA = 10000.0
RMS_EPS = 1e-6
NEG_INF = -1e30
ADAM_LR = 0.001
ADAM_B1 = 0.9
ADAM_B2 = 0.999
ADAM_EPS = 1e-08
ADAM_WD = 0.01
ADAM_STEP = 10

N_CHIPS = 4
N_DEV = 8
LANES = 128
MXU_COLS = 256
ROW_BLOCK = 512
ROW_BLOCK_WIDE = 1024
ATTN_BLOCK_ROWS = 2048
VMEM_LIMIT = 56 * 1024 * 1024
SMALL_ROWS = 16
ADAMW_BLOCK_BYTES = 1024 * 1024
DMA_CHUNK_BYTES = 512 * 1024

BF = jnp.bfloat16
F32 = jnp.float32
MESH = pl.DeviceIdType.MESH
ANY = pl.BlockSpec(memory_space=pl.ANY)


def _cp(*sem):
    return pltpu.CompilerParams(dimension_semantics=sem, vmem_limit_bytes=VMEM_LIMIT)


def _rot_half(t, first):
    return jnp.where(first, pltpu.roll(t, 96, 1), pltpu.roll(t, 32, 1))


def _sigmoid(x):
    return pl.reciprocal(1.0 + jnp.exp(-x), approx=True)


def _col_chunks(n):
    return [(c0, min(MXU_COLS, n - c0)) for c0 in range(0, n, MXU_COLS)]


def _first_half_mask(rows):
    lane = lax.broadcasted_iota(jnp.int32, (rows, LANES), 1)
    return (lane % HEAD_DIM) < (HEAD_DIM // 2)


def _normed_rows(j, rows, x_ref, g_ref, xn_ref, xs, last_start, tm):
    @pl.when(j == 0)
    def _():
        xv = x_ref[...]
        r = lax.rsqrt(jnp.mean(xv * xv, axis=-1, keepdims=True) + RMS_EPS)
        xn = (xv * r * g_ref[...]).astype(BF)
        xs[rows, :] = xn
        xn_ref[...] = xn

    @pl.when(j > 0)
    def _():
        xn_ref[...] = xs[pl.ds(last_start, tm), :]


def _norm_matmul(x, gain, wg, cos2, ss2, *, name, rope_shards, scale, out_dtype):
    T, D = x.shape
    n = wg.shape[2]
    tm = min(ROW_BLOCK_WIDE, T)
    ni = T // tm

    def body(x_ref, g_ref, w_ref, cos_ref, ss_ref, y_ref, xn_ref, xs):
        j = pl.program_id(0)
        rows = pl.ds(pl.multiple_of(pl.program_id(1) * tm, tm), tm)
        _normed_rows(j, rows, x_ref, g_ref, xn_ref, xs, (ni - 1) * tm, tm)
        acc = jnp.dot(xs[rows, :], w_ref[...], preferred_element_type=F32)

        def plain():
            y_ref[...] = acc.astype(out_dtype)

        def rope():
            cosv = cos_ref[...]
            ssv = ss_ref[...]
            first = _first_half_mask(tm)
            for ci in range(n // LANES):
                t = acc[:, ci * LANES:(ci + 1) * LANES]
                y = (t * cosv + _rot_half(t, first) * ssv) * scale
                y_ref[:, ci * LANES:(ci + 1) * LANES] = y.astype(out_dtype)

        if rope_shards == 0:
            plain()
        elif rope_shards == N_CHIPS:
            rope()
        else:
            pl.when(j < rope_shards)(rope)
            pl.when(j >= rope_shards)(plain)

    first_pass = lambda j, i: (jnp.where(j == 0, i, ni - 1), 0)
    return pl.pallas_call(
        body, name=name,
        grid=(N_CHIPS, ni),
        in_specs=[
            pl.BlockSpec((tm, D), first_pass),
            pl.BlockSpec((1, D), lambda j, i: (0, 0)),
            pl.BlockSpec((None, D, n), lambda j, i: (j, 0, 0)),
            pl.BlockSpec((tm, LANES), lambda j, i: (i, 0)),
            pl.BlockSpec((tm, LANES), lambda j, i: (i, 0)),
        ],
        out_specs=[
            pl.BlockSpec((tm, n), lambda j, i: (i, j)),
            pl.BlockSpec((tm, D), first_pass),
        ],
        out_shape=[jax.ShapeDtypeStruct((T, N_CHIPS * n), out_dtype),
                   jax.ShapeDtypeStruct((T, D), BF)],
        scratch_shapes=[pltpu.VMEM((T, D), BF)],
        compiler_params=_cp("arbitrary", "arbitrary"),
    )(x, gain, wg, cos2, ss2)


def _norm_swiglu(x, gain, wg, layer, *, name):
    T, D = x.shape
    n = wg.shape[2]
    tm = min(ROW_BLOCK, T)

    ni = T // tm

    def body(x_ref, g_ref, wg_ref, wu_ref, go_ref, uo_ref, ao_ref, xn_ref, xs):
        j = pl.program_id(0)
        rows = pl.ds(pl.multiple_of(pl.program_id(1) * tm, tm), tm)
        _normed_rows(j, rows, x_ref, g_ref, xn_ref, xs, (ni - 1) * tm, tm)
        g = jnp.dot(xs[rows, :], wg_ref[...], preferred_element_type=F32)
        u = jnp.dot(xs[rows, :], wu_ref[...], preferred_element_type=F32)
        go_ref[...] = g.astype(BF)
        uo_ref[...] = u.astype(BF)
        ao_ref[...] = (g * _sigmoid(g) * u).astype(BF)

    half = N_CHIPS // 2
    first_pass = lambda j, i: (jnp.where(j == 0, i, ni - 1), 0)
    act = jax.ShapeDtypeStruct((T, half * n), BF)
    return pl.pallas_call(
        body, name=name,
        grid=(half, ni),
        in_specs=[
            pl.BlockSpec((tm, D), first_pass),
            pl.BlockSpec((1, D), lambda j, i: (0, 0)),
            pl.BlockSpec((None, D, n), lambda j, i: (j, layer, 0)),
            pl.BlockSpec((None, D, n), lambda j, i: (j + half, layer, 0)),
        ],
        out_specs=[
            pl.BlockSpec((tm, n), lambda j, i: (i, j)),
            pl.BlockSpec((tm, n), lambda j, i: (i, j)),
            pl.BlockSpec((tm, n), lambda j, i: (i, j)),
            pl.BlockSpec((tm, D), first_pass),
        ],
        out_shape=[act, act, act, jax.ShapeDtypeStruct((T, D), BF)],
        scratch_shapes=[pltpu.VMEM((T, D), BF)],
        compiler_params=_cp("arbitrary", "arbitrary"),
    )(x, gain, wg, wg)


def _matmul_postnorm(a, w3, widx, gain, h_old, *, name):
    T, K = a.shape
    D = w3.shape[2]
    tm = min(ROW_BLOCK_WIDE, T)

    def body(a_ref, w_ref, g_ref, h_ref, y_ref, hn_ref):
        y = jnp.dot(a_ref[...].astype(BF), w_ref[...], preferred_element_type=F32)
        y_ref[...] = y.astype(BF)
        r = lax.rsqrt(jnp.mean(y * y, axis=-1, keepdims=True) + RMS_EPS)
        hn_ref[...] = h_ref[...] + y * r * g_ref[...]

    return pl.pallas_call(
        body, name=name,
        grid=(T // tm,),
        in_specs=[
            pl.BlockSpec((tm, K), lambda i: (i, 0)),
            pl.BlockSpec((None, K, D), lambda i: (widx, 0, 0)),
            pl.BlockSpec((1, D), lambda i: (0, 0)),
            pl.BlockSpec((tm, D), lambda i: (i, 0)),
        ],
        out_specs=[pl.BlockSpec((tm, D), lambda i: (i, 0)),
                   pl.BlockSpec((tm, D), lambda i: (i, 0))],
        out_shape=[jax.ShapeDtypeStruct((T, D), BF), jax.ShapeDtypeStruct((T, D), F32)],
        compiler_params=_cp("parallel"),
    )(a, w3, gain, h_old)


def _loss_head(h, target, *, name):
    T, D = h.shape
    tm = min(ROW_BLOCK, T)

    def body(h_ref, t_ref, dh_ref, s_ref):
        i = pl.program_id(0)

        @pl.when(i == 0)
        def _():
            s_ref[...] = jnp.zeros_like(s_ref)

        e = h_ref[...] - t_ref[...]
        dh_ref[...] = e * (1.0 / D)
        s_ref[...] += jnp.sum(e * e)

    return pl.pallas_call(
        body, name=name,
        grid=(T // tm,),
        in_specs=[pl.BlockSpec((tm, D), lambda i: (i, 0)), pl.BlockSpec((tm, D), lambda i: (i, 0))],
        out_specs=[pl.BlockSpec((tm, D), lambda i: (i, 0)), pl.BlockSpec((8, LANES), lambda i: (0, 0))],
        out_shape=[jax.ShapeDtypeStruct((T, D), F32), jax.ShapeDtypeStruct((8, LANES), F32)],
        compiler_params=_cp("arbitrary"),
    )(h, target)


def _shift_down(u, k):
    row = lax.broadcasted_iota(jnp.int32, u.shape, 0)
    return jnp.where(row >= k, pltpu.roll(u, k, 0), 0.0)


def _shift_up(u, k):
    T = u.shape[0]
    row = lax.broadcasted_iota(jnp.int32, u.shape, 0)
    return jnp.where(row < T - k, pltpu.roll(u, T - k, 0), 0.0)


def _conv_fwd(z, cw, *, name):
    T = z.shape[0]
    D = z.shape[1] // 3
    tc = cw.shape[2]
    nb = D // tc

    def body(b_ref, c_ref, h_ref, w_ref, o_ref):
        u = c_ref[...].astype(F32) * h_ref[...].astype(F32)
        w = w_ref[...]
        conv = w[2:3] * u + w[1:2] * _shift_down(u, 1) + w[0:1] * _shift_down(u, 2)
        o_ref[...] = (b_ref[...].astype(F32) * conv).astype(BF)

    return pl.pallas_call(
        body, name=name,
        grid=(nb,),
        in_specs=[
            pl.BlockSpec((T, tc), lambda j: (0, j)),
            pl.BlockSpec((T, tc), lambda j: (0, nb + j)),
            pl.BlockSpec((T, tc), lambda j: (0, 2 * nb + j)),
            pl.BlockSpec((None, 8, tc), lambda j: (j, 0, 0)),
        ],
        out_specs=pl.BlockSpec((T, tc), lambda j: (0, j)),
        out_shape=jax.ShapeDtypeStruct((T, D), BF),
        compiler_params=_cp("parallel"),
    )(z, z, z, cw)


def _conv_bwd(z, cw, dv, *, name):
    T = z.shape[0]
    D = z.shape[1] // 3
    tc = LANES
    nb = D // tc
    per = cw.shape[2] // tc

    def body(b_ref, c_ref, h_ref, w_ref, dv_ref, dz_ref, dw_ref, stage, sems):
        j = pl.program_id(0)
        slot = j % 2

        def slab(p, jj, s):
            col = pl.multiple_of((p * nb + jj) * tc, tc)
            return pltpu.make_async_copy(stage.at[s, p], dz_ref.at[:, pl.ds(col, tc)], sems.at[s, p])

        @pl.when(j >= 2)
        def _():
            for p in range(3):
                slab(p, j - 2, slot).wait()

        c = c_ref[...].astype(F32)
        h = h_ref[...].astype(F32)
        u = c * h
        u1 = _shift_down(u, 1)
        u2 = _shift_down(u, 2)
        w = w_ref[...]
        dvv = dv_ref[...].astype(F32)
        dconv = dvv * b_ref[...].astype(F32)
        du = w[2:3] * dconv + w[1:2] * _shift_up(dconv, 1) + w[0:1] * _shift_up(dconv, 2)
        rows = lax.broadcasted_iota(jnp.int32, (8, tc), 0)
        dw_ref[...] = jnp.where(rows == 0, jnp.sum(dconv * u2, axis=0, keepdims=True),
                                jnp.where(rows == 1, jnp.sum(dconv * u1, axis=0, keepdims=True),
                                          jnp.where(rows == 2, jnp.sum(dconv * u, axis=0, keepdims=True), 0.0)))
        stage[slot, 0] = (dvv * (w[2:3] * u + w[1:2] * u1 + w[0:1] * u2)).astype(BF)
        stage[slot, 1] = (du * h).astype(BF)
        stage[slot, 2] = (du * c).astype(BF)
        for p in range(3):
            slab(p, j, slot).start()

        @pl.when(j == nb - 1)
        def _():
            for p in range(3):
                slab(p, j, slot).wait()
            if nb > 1:
                for p in range(3):
                    slab(p, j - 1, 1 - slot).wait()

    return pl.pallas_call(
        body, name=name,
        grid=(nb,),
        in_specs=[
            pl.BlockSpec((T, tc), lambda j: (0, j)),
            pl.BlockSpec((T, tc), lambda j: (0, nb + j)),
            pl.BlockSpec((T, tc), lambda j: (0, 2 * nb + j)),
            pl.BlockSpec((None, 8, tc), lambda j: (j // per, 0, j % per)),
            pl.BlockSpec((T, tc), lambda j: (0, j)),
        ],
        out_specs=[ANY, pl.BlockSpec((8, tc), lambda j: (0, j))],
        out_shape=[jax.ShapeDtypeStruct((T, 3 * D), BF), jax.ShapeDtypeStruct((8, D), F32)],
        scratch_shapes=[pltpu.VMEM((2, 3, T, tc), BF), pltpu.SemaphoreType.DMA((2, 3))],
        compiler_params=_cp("arbitrary"),
    )(z, z, z, cw, dv)


def _strided(base, count, d):
    return pl.ds(base, count, stride=d) if d > 1 else pl.ds(pl.multiple_of(base, BAND), count)


def _fill_band_bias(bias):
    qi = lax.broadcasted_iota(jnp.int32, (2 * BAND, 2 * BAND), 0) % BAND
    kj = lax.broadcasted_iota(jnp.int32, (2 * BAND, 2 * BAND), 1)
    dist = qi + BAND - kj
    band = (dist >= 0) & (dist <= BAND)
    bias[0] = jnp.where(band & (kj >= BAND), 0.0, NEG_INF)
    bias[1] = jnp.where(band, 0.0, NEG_INF)


def _attn_block_rows(T):
    return min(ATTN_BLOCK_ROWS, T)


def _head_mask():
    lane = lax.broadcasted_iota(jnp.int32, (2 * BAND, LANES), 1)
    row = lax.broadcasted_iota(jnp.int32, (2 * BAND, LANES), 0)
    return (lane < HEAD_DIM) == (row < BAND)


def _attn_fwd(q_all, kv_all, *, name):
    T = q_all.shape[0]
    NB = len(BRANCHES)
    Dm = q_all.shape[1] // NB
    HP = Dm // LANES
    R = _attn_block_rows(T)
    units = R // BAND
    dmax = max(d for _, d in BRANCHES)

    def body(*refs):
        ins = refs[:5 * NB]
        o_ref, l_ref, qbuf, kbuf, vbuf, o_s, l_s, bias = refs[5 * NB:]
        n = pl.program_id(0)
        _fill_band_bias(bias)
        hm = _head_mask()
        low = lax.broadcasted_iota(jnp.int32, (BAND, LANES), 1) < HEAD_DIM

        for g, (_, d) in enumerate(BRANCHES):
            q_ref, kp_ref, kc_ref, vp_ref, vc_ref = ins[5 * g:5 * g + 5]
            pr = BAND * d
            qbuf[...] = q_ref[...].astype(F32)
            kbuf[0:pr, :] = kp_ref[...].astype(F32)
            kbuf[pr:pr + R, :] = kc_ref[...].astype(F32)
            vbuf[0:pr, :] = vp_ref[...].astype(F32)
            vbuf[pr:pr + R, :] = vc_ref[...].astype(F32)

            def unit(u, carry, g=g, d=d, pr=pr):
                sub = u // d
                base = sub * pr + (u - sub * d)
                q = qbuf[_strided(base, BAND, d), :]
                q2 = jnp.where(hm, jnp.concatenate([q, q], axis=0), 0.0).astype(BF)
                k2 = kbuf[_strided(base, 2 * BAND, d), :].astype(BF)
                v2 = vbuf[_strided(base, 2 * BAND, d), :].astype(BF)
                s = lax.dot_general(q2, k2, (((1,), (1,)), ((), ())), preferred_element_type=F32)
                s = s + bias[((n > 0) | (sub > 0)).astype(jnp.int32)]
                m = jnp.max(s, axis=-1, keepdims=True)
                p = jnp.exp(s - m)
                l = jnp.sum(p, axis=-1, keepdims=True)
                pv = jnp.dot(p.astype(BF), v2, preferred_element_type=F32) * (1.0 / l)
                lse = m + jnp.log(l)
                o_s[g, _strided(base, BAND, d), :] = jnp.where(low, pv[:BAND], pv[BAND:])
                l_s[g, _strided(base, BAND, d), :] = jnp.where(low, lse[:BAND], lse[BAND:])
                return carry

            lax.fori_loop(0, units, unit, 0, unroll=4)

        def merge(i, carry):
            sl = pl.ds(pl.multiple_of(i * BAND, BAND), BAND)
            lv = [l_s[g, sl, :] for g in range(NB)]
            m = functools.reduce(jnp.maximum, lv)
            e = [jnp.exp(v - m) for v in lv]
            tot = functools.reduce(jnp.add, e)
            inv = 1.0 / tot
            o_ref[sl, :] = functools.reduce(jnp.add, [(e[g] * inv) * o_s[g, sl, :] for g in range(NB)])
            l_ref[sl, :] = m + jnp.log(tot)
            return carry

        lax.fori_loop(0, units, merge, 0)

    in_specs, args = [], []
    for g, (_, d) in enumerate(BRANCHES):
        per = R // (BAND * d)
        for col, rows, idx in (
                (g * HP, R, lambda n, hp: n),
                (g * HP, BAND * d, lambda n, hp, per=per: jnp.maximum(n * per - 1, 0)),
                (g * HP, R, lambda n, hp: n),
                ((NB + g) * HP, BAND * d, lambda n, hp, per=per: jnp.maximum(n * per - 1, 0)),
                ((NB + g) * HP, R, lambda n, hp: n)):
            in_specs.append(pl.BlockSpec((rows, LANES), lambda n, hp, col=col, idx=idx: (idx(n, hp), col + hp)))
        args += [q_all, kv_all, kv_all, kv_all, kv_all]
    out = pl.BlockSpec((R, LANES), lambda n, hp: (n, hp))
    return pl.pallas_call(
        body, name=name,
        grid=(T // R, HP),
        in_specs=in_specs,
        out_specs=[out, out],
        out_shape=[jax.ShapeDtypeStruct((T, Dm), F32), jax.ShapeDtypeStruct((T, Dm), F32)],
        scratch_shapes=[pltpu.VMEM((R, LANES), F32),
                        pltpu.VMEM((BAND * dmax + R, LANES), F32), pltpu.VMEM((BAND * dmax + R, LANES), F32),
                        pltpu.VMEM((NB, R, LANES), F32), pltpu.VMEM((NB, R, LANES), F32),
                        pltpu.VMEM((2, 2 * BAND, 2 * BAND), F32)],
        compiler_params=_cp("parallel", "parallel"),
    )(*args)


def _attn_bwd(q_all, kv_all, do, o, lse, cos2, ss2, g, d, prev, *, name, dep=None):
    T = q_all.shape[0]
    NB = len(BRANCHES)
    Dm = q_all.shape[1] // NB
    HP = Dm // LANES
    R = _attn_block_rows(T)
    nblk = T // R
    units = R // BAND
    pr = BAND * d
    per = R // pr
    scale = HEAD_DIM ** -0.5

    def rope_bwd(t, cosv, ssv, first):
        return t * cosv - _rot_half(t, first) * ssv

    def body(q_ref, kp_ref, kc_ref, vp_ref, vc_ref, do_ref, o_ref, l_ref, cos_ref, ss_ref, *rest):
        dq_ref, dk_ref, dv_ref, qbuf, kbuf, vbuf, dq_s, dk_s, dv_s, pend_k, pend_v, bias = rest[-12:]
        i = pl.program_id(1)
        n = nblk - 1 - i
        _fill_band_bias(bias)
        hm = _head_mask()
        low = lax.broadcasted_iota(jnp.int32, (BAND, LANES), 1) < HEAD_DIM

        qbuf[...] = q_ref[...].astype(F32)
        kbuf[0:pr, :] = kp_ref[...].astype(F32)
        kbuf[pr:pr + R, :] = kc_ref[...].astype(F32)
        vbuf[0:pr, :] = vp_ref[...].astype(F32)
        vbuf[pr:pr + R, :] = vc_ref[...].astype(F32)

        @pl.when(i == 0)
        def _():
            pend_k[...] = jnp.zeros_like(pend_k)
            pend_v[...] = jnp.zeros_like(pend_v)

        def unit(u, carry):
            sub = per - 1 - u // d
            cls = u % d
            base = sub * pr + cls
            sl = _strided(base, BAND, d)
            sl2 = _strided(base, 2 * BAND, d)
            q = qbuf[sl, :]
            dov = do_ref[sl, :]
            ov = o_ref[sl, :]
            lv = l_ref[sl, :]
            q2 = jnp.where(hm, jnp.concatenate([q, q], axis=0), 0.0).astype(BF)
            do2 = jnp.where(hm, jnp.concatenate([dov, dov], axis=0), 0.0)
            oo = dov * ov
            delta = jnp.sum(jnp.where(hm, jnp.concatenate([oo, oo], axis=0), 0.0), axis=-1, keepdims=True)
            lse2 = jnp.concatenate([lv[:, 0:1], lv[:, HEAD_DIM:HEAD_DIM + 1]], axis=0)
            do2 = do2.astype(BF)
            k2 = kbuf[sl2, :].astype(BF)
            v2 = vbuf[sl2, :].astype(BF)
            s = lax.dot_general(q2, k2, (((1,), (1,)), ((), ())), preferred_element_type=F32)
            p = jnp.exp(s + bias[((n > 0) | (sub > 0)).astype(jnp.int32)] - lse2)
            dp = lax.dot_general(do2, v2, (((1,), (1,)), ((), ())), preferred_element_type=F32)
            ds = (p * (dp - delta)).astype(BF)
            dq2 = jnp.dot(ds, k2, preferred_element_type=F32)
            dq = jnp.where(low, dq2[:BAND], dq2[BAND:])
            dq_s[sl, :] = dq
            tn = (((0,), (0,)), ((), ()))
            dk2 = lax.dot_general(ds, q2, tn, preferred_element_type=F32)
            dv2 = lax.dot_general(p.astype(BF), do2, tn, preferred_element_type=F32)
            dk_s[sl, :] = dk2[BAND:] + pend_k[cls]
            dv_s[sl, :] = dv2[BAND:] + pend_v[cls]
            pend_k[cls] = dk2[:BAND]
            pend_v[cls] = dv2[:BAND]
            return carry

        lax.fori_loop(0, units, unit, 0, unroll=4)

        whole = _first_half_mask(R)
        dq_ref[...] = (rope_bwd(dq_s[...], cos_ref[...], ss_ref[...], whole) * scale).astype(BF)
        dk_ref[...] = rope_bwd(dk_s[...], cos_ref[...], ss_ref[...], whole).astype(BF)
        dv_ref[...] = dv_s[...].astype(BF)

    blk = (R, LANES)
    pblk = (pr, LANES)
    cur = lambda hp, i: nblk - 1 - i
    prv = lambda hp, i: jnp.maximum((nblk - 1 - i) * per - 1, 0)
    in_specs = [
        pl.BlockSpec(blk, lambda hp, i: (cur(hp, i), g * HP + hp)),
        pl.BlockSpec(pblk, lambda hp, i: (prv(hp, i), g * HP + hp)),
        pl.BlockSpec(blk, lambda hp, i: (cur(hp, i), g * HP + hp)),
        pl.BlockSpec(pblk, lambda hp, i: (prv(hp, i), (NB + g) * HP + hp)),
        pl.BlockSpec(blk, lambda hp, i: (cur(hp, i), (NB + g) * HP + hp)),
        pl.BlockSpec(blk, lambda hp, i: (cur(hp, i), hp)),
        pl.BlockSpec(blk, lambda hp, i: (cur(hp, i), hp)),
        pl.BlockSpec(blk, lambda hp, i: (cur(hp, i), hp)),
        pl.BlockSpec(blk, lambda hp, i: (cur(hp, i), 0)),
        pl.BlockSpec(blk, lambda hp, i: (cur(hp, i), 0)),
    ]
    args = [q_all, kv_all, kv_all, kv_all, kv_all, do, o, lse, cos2, ss2]
    if dep is not None:
        in_specs.append(ANY)
        args.append(dep)
    aliases = {}
    if prev is not None:
        in_specs += [ANY, ANY, ANY]
        aliases = {len(args): 0, len(args) + 1: 1, len(args) + 2: 2}
        args += list(prev)
    wide = jax.ShapeDtypeStruct((T, NB * Dm), BF)
    out = pl.BlockSpec(blk, lambda hp, i: (cur(hp, i), g * HP + hp))
    return pl.pallas_call(
        body, name=name,
        grid=(HP, nblk),
        in_specs=in_specs,
        out_specs=[out, out, out],
        out_shape=[wide, wide, wide],
        scratch_shapes=[pltpu.VMEM(blk, F32), pltpu.VMEM((pr + R, LANES), F32), pltpu.VMEM((pr + R, LANES), F32),
                        pltpu.VMEM(blk, F32), pltpu.VMEM(blk, F32), pltpu.VMEM(blk, F32),
                        pltpu.VMEM((d, BAND, LANES), F32), pltpu.VMEM((d, BAND, LANES), F32),
                        pltpu.VMEM((2, 2 * BAND, 2 * BAND), F32)],
        input_output_aliases=aliases,
        compiler_params=_cp("arbitrary", "arbitrary"),
    )(*args)


def _postnorm_bwd(dh, y, g_ref_val):
    r = lax.rsqrt(jnp.mean(y * y, axis=-1, keepdims=True) + RMS_EPS)
    yn = y * r
    dyn = dh * g_ref_val
    dy = r * (dyn - yn * jnp.mean(dyn * yn, axis=-1, keepdims=True))
    return dy, yn


def _after(body, n_in, dep):
    if dep is None:
        return body
    return lambda *refs: body(*refs[:n_in], *refs[n_in + 1:])


def _dep_spec(dep):
    return [] if dep is None else [ANY]


def _dep_arg(dep):
    return [] if dep is None else [dep]


def _postnorm_bwd_matmul(dh, y, gain, w3, widx, *, name, da_dtype, dep=None):
    T, D = dh.shape
    K = w3.shape[1]
    tm = min(ROW_BLOCK_WIDE, T)

    def body(dh_ref, y_ref, g_ref, w_ref, dy_ref, da_ref, dg_ref):
        i = pl.program_id(0)

        @pl.when(i == 0)
        def _():
            dg_ref[...] = jnp.zeros_like(dg_ref)

        dhv = dh_ref[...]
        dy, yn = _postnorm_bwd(dhv, y_ref[...].astype(F32), g_ref[...])
        dg_ref[...] += jnp.sum(dhv * yn, axis=0, keepdims=True)
        dyb = dy.astype(BF)
        dy_ref[...] = dyb
        da = lax.dot_general(dyb, w_ref[...], (((1,), (1,)), ((), ())), preferred_element_type=F32)
        da_ref[...] = da.astype(da_dtype)

    return pl.pallas_call(
        _after(body, 4, dep), name=name,
        grid=(T // tm,),
        in_specs=[
            pl.BlockSpec((tm, D), lambda i: (i, 0)),
            pl.BlockSpec((tm, D), lambda i: (i, 0)),
            pl.BlockSpec((1, D), lambda i: (0, 0)),
            pl.BlockSpec((None, K, D), lambda i: (widx, 0, 0)),
        ] + _dep_spec(dep),
        out_specs=[pl.BlockSpec((tm, D), lambda i: (i, 0)),
                   pl.BlockSpec((tm, K), lambda i: (i, 0)),
                   pl.BlockSpec((1, D), lambda i: (0, 0))],
        out_shape=[jax.ShapeDtypeStruct((T, D), BF), jax.ShapeDtypeStruct((T, K), da_dtype),
                   jax.ShapeDtypeStruct((1, D), F32)],
        compiler_params=_cp("arbitrary"),
    )(dh, y, gain, w3, *_dep_arg(dep))


def _postnorm_bwd_swiglu(dh, y, gain, wd3, layer, g, u, *, name, dep=None):
    T, D = dh.shape
    F = wd3.shape[1]
    nf = F // 2
    tm = min(ROW_BLOCK, T)

    def body(dh_ref, y_ref, g_ref, w_ref, gg_ref, uu_ref, dy_ref, dgo_ref, duo_ref, dgain_ref, dys):
        i = pl.program_id(0)
        j = pl.program_id(1)

        @pl.when((i == 0) & (j == 0))
        def _():
            dgain_ref[...] = jnp.zeros_like(dgain_ref)

        @pl.when(j == 0)
        def _():
            dhv = dh_ref[...]
            dy, yn = _postnorm_bwd(dhv, y_ref[...].astype(F32), g_ref[...])
            dgain_ref[...] += jnp.sum(dhv * yn, axis=0, keepdims=True)
            dyb = dy.astype(BF)
            dys[...] = dyb
            dy_ref[...] = dyb

        for c0, cw in _col_chunks(nf):
            cols = slice(c0, c0 + cw)
            da = lax.dot_general(dys[...], w_ref[cols, :], (((1,), (1,)), ((), ())), preferred_element_type=F32)
            gv = gg_ref[:, cols].astype(F32)
            uv = uu_ref[:, cols].astype(F32)
            sg = _sigmoid(gv)
            silu = gv * sg
            dgo_ref[:, cols] = (da * uv * (sg + silu * (1.0 - sg))).astype(BF)
            duo_ref[:, cols] = (da * silu).astype(BF)

    act = jax.ShapeDtypeStruct((T, F), BF)
    return pl.pallas_call(
        _after(body, 6, dep), name=name,
        grid=(T // tm, 2),
        in_specs=[
            pl.BlockSpec((tm, D), lambda i, j: (i, 0)),
            pl.BlockSpec((tm, D), lambda i, j: (i, 0)),
            pl.BlockSpec((1, D), lambda i, j: (0, 0)),
            pl.BlockSpec((None, nf, D), lambda i, j: (layer, j, 0)),
            pl.BlockSpec((tm, nf), lambda i, j: (i, j)),
            pl.BlockSpec((tm, nf), lambda i, j: (i, j)),
        ] + _dep_spec(dep),
        out_specs=[pl.BlockSpec((tm, D), lambda i, j: (i, 0)),
                   pl.BlockSpec((tm, nf), lambda i, j: (i, j)),
                   pl.BlockSpec((tm, nf), lambda i, j: (i, j)),
                   pl.BlockSpec((1, D), lambda i, j: (0, 0))],
        out_shape=[jax.ShapeDtypeStruct((T, D), BF), act, act, jax.ShapeDtypeStruct((1, D), F32)],
        scratch_shapes=[pltpu.VMEM((tm, D), BF)],
        compiler_params=_cp("arbitrary", "arbitrary"),
    )(dh, y, gain, wd3, g, u, *_dep_arg(dep))


def _matmul_prenorm_bwd(dzs, wg, layer, h, gain, dh_in, *, name):
    T, D = h.shape
    n = wg.shape[2]
    pair = len(dzs) == 2
    tm = min(ROW_BLOCK if pair else ROW_BLOCK_WIDE, T)
    nj = N_CHIPS // 2 if pair else N_CHIPS

    def body(*refs):
        dz_refs = refs[:len(dzs)]
        w_refs = refs[len(dzs):2 * len(dzs)]
        h_ref, g_ref, dhi_ref, dh_ref, dg_ref, acc = refs[2 * len(dzs):]
        i = pl.program_id(0)
        j = pl.program_id(1)

        @pl.when((i == 0) & (j == 0))
        def _():
            dg_ref[...] = jnp.zeros_like(dg_ref)

        part = None
        for dz_ref, w_ref in zip(dz_refs, w_refs):
            t = lax.dot_general(dz_ref[...].astype(BF), w_ref[...], (((1,), (1,)), ((), ())),
                                preferred_element_type=F32)
            part = t if part is None else part + t

        @pl.when(j == 0)
        def _():
            acc[...] = part

        @pl.when(j > 0)
        def _():
            acc[...] += part

        @pl.when(j == nj - 1)
        def _():
            dhn = acc[...]
            hv = h_ref[...]
            r = lax.rsqrt(jnp.mean(hv * hv, axis=-1, keepdims=True) + RMS_EPS)
            xh = hv * r
            dg_ref[...] += jnp.sum(dhn * xh, axis=0, keepdims=True)
            dxn = dhn * g_ref[...]
            dh_ref[...] = dhi_ref[...] + r * (dxn - xh * jnp.mean(dxn * xh, axis=-1, keepdims=True))

    in_specs = [pl.BlockSpec((tm, n), lambda i, j: (i, j)) for _ in dzs]
    if pair:
        in_specs += [pl.BlockSpec((None, D, n), lambda i, j: (j, layer, 0)),
                     pl.BlockSpec((None, D, n), lambda i, j: (j + nj, layer, 0))]
    else:
        in_specs += [pl.BlockSpec((None, D, n), lambda i, j: (j, layer, 0))]
    in_specs += [pl.BlockSpec((tm, D), lambda i, j: (i, 0)),
                 pl.BlockSpec((1, D), lambda i, j: (0, 0)),
                 pl.BlockSpec((tm, D), lambda i, j: (i, 0))]
    return pl.pallas_call(
        body, name=name,
        grid=(T // tm, nj),
        in_specs=in_specs,
        out_specs=[pl.BlockSpec((tm, D), lambda i, j: (i, 0)), pl.BlockSpec((1, D), lambda i, j: (0, 0))],
        out_shape=[jax.ShapeDtypeStruct((T, D), F32), jax.ShapeDtypeStruct((1, D), F32)],
        scratch_shapes=[pltpu.VMEM((tm, D), F32)],
        compiler_params=_cp("arbitrary", "arbitrary"),
    )(*dzs, *([wg] * len(dzs)), h, gain, dh_in)


def _grad_matmul(a, b, out_shape3, tme, tne, out_index, prev, *, name):
    T, M = a.shape
    N = b.shape[1]
    tk = min(ROW_BLOCK_WIDE, T)
    nk = T // tk

    def body(a_ref, b_ref, *rest):
        o_ref, acc = rest[-2:]
        k = pl.program_id(2)
        part = jnp.dot(a_ref[...].astype(BF).T, b_ref[...].astype(BF), preferred_element_type=F32)

        @pl.when(k == 0)
        def _():
            acc[...] = part

        @pl.when(k > 0)
        def _():
            acc[...] += part

        @pl.when(k == nk - 1)
        def _():
            o_ref[...] = acc[...].astype(BF)

    in_specs = [pl.BlockSpec((tk, tme), lambda i, j, k: (k, i)),
                pl.BlockSpec((tk, tne), lambda i, j, k: (k, j))]
    args = [a, b]
    aliases = {}
    if prev is not None:
        in_specs.append(ANY)
        args.append(prev)
        aliases = {2: 0}
    return pl.pallas_call(
        body, name=name,
        grid=(M // tme, N // tne, nk),
        in_specs=in_specs,
        out_specs=pl.BlockSpec((None, tme, tne), lambda i, j, k: out_index(i, j)),
        out_shape=jax.ShapeDtypeStruct(out_shape3, BF),
        scratch_shapes=[pltpu.VMEM((tme, tne), F32)],
        input_output_aliases=aliases,
        compiler_params=_cp("parallel", "parallel", "arbitrary"),
    )(*args)


def _row_tile(R, cap=512):
    fit = [t for t in range(16, min(R, cap) + 1, 16) if R % t == 0]
    return max(fit) if fit else R


def _cast_place(w3, layer, where, dtype, *, name, dep=None):
    _, R, C = w3.shape
    tr = _row_tile(R)

    def body(s_ref, w_ref, o_ref):
        o_ref[...] = w_ref[...].astype(o_ref.dtype)

    return pl.pallas_call(
        _after(body, 2, dep), name=name,
        grid_spec=pltpu.PrefetchScalarGridSpec(
            num_scalar_prefetch=1, grid=(R // tr,),
            in_specs=[pl.BlockSpec((None, tr, C), lambda i, s: (layer, i, 0))] + _dep_spec(dep),
            out_specs=pl.BlockSpec((None, tr, C), lambda i, s: (s[0], i, 0))),
        out_shape=jax.ShapeDtypeStruct((N_CHIPS, R, C), dtype),
        compiler_params=_cp("arbitrary"),
    )(where, w3, *_dep_arg(dep))


def _pair_sum(dw, theirs, where, *, name):
    G, rh, C = theirs.shape
    tr = _row_tile(rh)
    nr = rh // tr

    def body(s_ref, a_ref, b_ref, o_ref):
        o_ref[...] = (a_ref[...].astype(F32) + b_ref[...].astype(F32)).astype(BF)

    mine = pl.BlockSpec((None, tr, C), lambda g, i, s: (g, s[1] * nr + i, 0))
    spec = pl.BlockSpec((None, tr, C), lambda g, i, s: (g, i, 0))
    return pl.pallas_call(
        body, name=name,
        grid_spec=pltpu.PrefetchScalarGridSpec(
            num_scalar_prefetch=1, grid=(G, nr), in_specs=[mine, spec], out_specs=spec),
        out_shape=jax.ShapeDtypeStruct((G, rh, C), BF),
        compiler_params=_cp("arbitrary", "arbitrary"),
    )(where, dw, theirs)


def _chip_sum(landed, parts, where, total_rows, row_off, prev, *, name):
    G, rh, C = landed.shape
    tr = _row_tile(rh)
    nr = rh // tr
    base = row_off // tr

    def body(s_ref, l_ref, p_ref, *rest):
        o_ref = rest[-1]
        for j in range(G):
            def own(j=j):
                v = p_ref[...].astype(F32)
                o_ref[...] = v if j == 0 else o_ref[...] + v

            def other(j=j):
                v = l_ref[j].astype(F32)
                o_ref[...] = v if j == 0 else o_ref[...] + v

            pl.when(s_ref[0] == j)(own)
            pl.when(s_ref[0] != j)(other)

    in_specs = [pl.BlockSpec((G, tr, C), lambda i, s: (0, i, 0)),
                pl.BlockSpec((None, tr, C), lambda i, s: (s[0], i, 0))]
    args = [where, landed, parts]
    aliases = {}
    if prev is not None:
        in_specs.append(ANY)
        args.append(prev)
        aliases = {3: 0}
    return pl.pallas_call(
        body, name=name,
        grid_spec=pltpu.PrefetchScalarGridSpec(
            num_scalar_prefetch=1, grid=(nr,),
            in_specs=in_specs,
            out_specs=pl.BlockSpec((tr, C), lambda i, s: (base + s[1] * nr + i, 0))),
        out_shape=jax.ShapeDtypeStruct((total_rows, C), F32),
        input_output_aliases=aliases,
        compiler_params=_cp("arbitrary"),
    )(*args)


def _adamw(w, g, m, v, *, name, emit_grad=False):
    R, C = w.shape
    tr = _row_tile(R, cap=max(16, ADAMW_BLOCK_BYTES // (4 * C)))
    n_out = 4 if emit_grad else 3

    def body(w_ref, g_ref, m_ref, v_ref, d_ref, mo_ref, vo_ref, *go_ref):
        gv = g_ref[...]
        if emit_grad:
            go_ref[0][...] = gv
        mn = ADAM_B1 * m_ref[...] + (1.0 - ADAM_B1) * gv
        vn = ADAM_B2 * v_ref[...] + (1.0 - ADAM_B2) * jnp.square(gv)
        m_hat = mn / (1.0 - ADAM_B1 ** ADAM_STEP)
        v_hat = vn / (1.0 - ADAM_B2 ** ADAM_STEP)
        d_ref[...] = -ADAM_LR * (m_hat / (jnp.sqrt(v_hat) + ADAM_EPS) + ADAM_WD * w_ref[...])
        mo_ref[...] = mn
        vo_ref[...] = vn

    spec = pl.BlockSpec((tr, C), lambda i: (i, 0))
    shp = jax.ShapeDtypeStruct((R, C), F32)
    return pl.pallas_call(
        body, name=name, grid=(R // tr,), in_specs=[spec] * 4, out_specs=[spec] * n_out,
        out_shape=[shp] * n_out, compiler_params=_cp("parallel"),
    )(w, g, m, v)


def _place():
    x = lax.axis_index("x")
    y = lax.axis_index("y")
    c = lax.axis_index("c")
    chips = [(1 - x, y), (x, 1 - y), (1 - x, 1 - y)]
    return x, y, c, chips


def _chunk_rows(rows, row_bytes, align):
    if rows <= align:
        return rows
    cands = [r for r in range(align, rows + 1, align) if rows % r == 0]
    fit = [r for r in cands if r * row_bytes <= DMA_CHUNK_BYTES]
    return max(fit) if fit else min(cands)


def _row_align(dtype):
    return 8 * (4 // jnp.dtype(dtype).itemsize)


def _start_chunks(make, rows, rc):
    for r0 in range(0, rows, rc):
        make(r0, rc).start()


def _piece_rows(ref, piece, j, h, r0=0, n=None):
    _, lead, off, rows = piece
    rh = rows // 2
    n = rh if n is None else n
    if lead is not None:
        return ref.at[lead, j, pl.ds(h * rh + r0, n)]
    return ref.at[j, pl.ds(off + h * rh + r0, n)]


def _piece_chunk(arr, piece):
    rh = piece[3] // 2
    return rh, _chunk_rows(rh, arr.shape[-1] * arr.dtype.itemsize, _row_align(arr.dtype))


def _ag_start(arrays, taps, groups, *, name):
    na = len(arrays)
    ng = len(groups)
    nt = 0 if taps is None else 1
    n_sem = [3 * len(grp) + (3 if nt and g == 0 else 0) for g, grp in enumerate(groups)]

    def body(*refs):
        ins = refs[:na]
        taps_ref = refs[na] if nt else None
        sems = refs[na + nt:na + nt + 2 * ng]
        token = refs[-1]
        token[...] = jnp.zeros_like(token)
        x, y, c, chips = _place()
        myj = 2 * x + y
        for g, grp in enumerate(groups):
            ssem, rsem = sems[2 * g], sems[2 * g + 1]
            for idx, piece in enumerate(grp):
                ref = ins[piece[0]]
                rh, rc = _piece_chunk(arrays[piece[0]], piece)
                for k, (px, py) in enumerate(chips):
                    def send(r0, n, ref=ref, piece=piece, idx=idx, k=k, px=px, py=py, ssem=ssem, rsem=rsem):
                        part = _piece_rows(ref, piece, myj, c, r0, n)
                        return pltpu.make_async_remote_copy(
                            src_ref=part, dst_ref=part, send_sem=ssem.at[3 * idx + k], recv_sem=rsem.at[3 * idx + k],
                            device_id=(px, py, c), device_id_type=MESH)
                    _start_chunks(send, rh, rc)
            if nt and g == 0:
                for k, (px, py) in enumerate(chips):
                    pltpu.make_async_remote_copy(
                        src_ref=taps_ref.at[myj], dst_ref=taps_ref.at[myj],
                        send_sem=ssem.at[3 * len(grp) + k], recv_sem=rsem.at[3 * len(grp) + k],
                        device_id=(px, py, c), device_id_type=MESH).start()

    sem_shapes = []
    for n in n_sem:
        sem_shapes += [pltpu.SemaphoreType.DMA((n,)), pltpu.SemaphoreType.DMA((n,))]
    ops = list(arrays) + ([taps] if nt else [])
    bufs = [pltpu.HBM(a.shape, a.dtype) for a in ops]
    outs = pl.pallas_call(
        body, name=name,
        out_shape=(*sem_shapes, *bufs, jax.ShapeDtypeStruct((8, LANES), F32)),
        in_specs=[HBM] * (na + nt),
        out_specs=(*([SEM] * (2 * ng)), *([HBM] * (na + nt)), pl.BlockSpec(memory_space=pltpu.VMEM)),
        input_output_aliases={i: 2 * ng + i for i in range(na + nt)},
        compiler_params=pltpu.CompilerParams(has_side_effects=EFFECT),
    )(*[_in_hbm(a) for a in ops])
    sems = [(outs[2 * g], outs[2 * g + 1]) for g in range(ng)]
    return sems, list(outs[2 * ng:2 * ng + na]), (outs[2 * ng + na] if nt else None), outs[-1]


def _ag_wait(sems, vals, taps, group, after, *, name):
    nv = len(vals)
    extra = ([taps] if taps is not None else [])
    nb = nv + len(extra)

    def body(*refs):
        bufs = refs[:nb]
        ssem, rsem = refs[nb], refs[nb + 1]
        x, y, c, chips = _place()
        for idx, piece in enumerate(group):
            for k, (px, py) in enumerate(chips):
                got = _piece_rows(bufs[piece[0]], piece, 2 * px + py, c)
                cp = pltpu.make_async_remote_copy(
                    src_ref=got, dst_ref=got, send_sem=ssem.at[3 * idx + k], recv_sem=rsem.at[3 * idx + k],
                    device_id=(px, py, c), device_id_type=MESH)
                cp.wait_send()
                cp.wait_recv()
        if taps is not None:
            for k, (px, py) in enumerate(chips):
                got = bufs[nv].at[2 * px + py]
                cp = pltpu.make_async_remote_copy(
                    src_ref=got, dst_ref=got, send_sem=ssem.at[3 * len(group) + k],
                    recv_sem=rsem.at[3 * len(group) + k], device_id=(px, py, c), device_id_type=MESH)
                cp.wait_send()
                cp.wait_recv()

    ops = list(vals) + extra
    shapes = [pltpu.HBM(a.shape, a.dtype) for a in ops]
    outs = pl.pallas_call(
        body, name=name,
        out_shape=tuple(shapes),
        in_specs=[HBM] * nb + [SEM, SEM] + _dep_spec(after),
        out_specs=[HBM] * nb,
        input_output_aliases={i: i for i in range(nb)},
        compiler_params=pltpu.CompilerParams(has_side_effects=EFFECT),
    )(*ops, sems[0], sems[1], *_dep_arg(after))
    return list(outs[:nv]), (outs[nv] if taps is not None else None)


def _ag_forward(vals, group, *, name):
    nv = len(vals)
    npc = len(group)

    def body(*refs):
        bufs = refs[nv:2 * nv]
        fsem, gsem = refs[2 * nv:]
        x, y, c, chips = _place()
        sib = (x, y, 1 - c)
        sent = []
        for idx, piece in enumerate(group):
            rh, rc = _piece_chunk(vals[piece[0]], piece)
            for k, (px, py) in enumerate(chips):
                def fwd(r0, n, piece=piece, idx=idx, k=k, pj=2 * px + py):
                    part = _piece_rows(bufs[piece[0]], piece, pj, c, r0, n)
                    return pltpu.make_async_remote_copy(
                        src_ref=part, dst_ref=part, send_sem=fsem.at[3 * idx + k], recv_sem=gsem.at[3 * idx + k],
                        device_id=sib, device_id_type=MESH)
                _start_chunks(fwd, rh, rc)
                sent.append(fwd(0, rh))
        for idx, piece in enumerate(group):
            for k, (px, py) in enumerate(chips):
                theirs = _piece_rows(bufs[piece[0]], piece, 2 * px + py, 1 - c)
                pltpu.make_async_remote_copy(
                    src_ref=theirs, dst_ref=theirs, send_sem=fsem.at[3 * idx + k], recv_sem=gsem.at[3 * idx + k],
                    device_id=sib, device_id_type=MESH).wait_recv()
        for cp in sent:
            cp.wait_send()

    return pl.pallas_call(
        body, name=name,
        in_specs=[ANY] * nv, out_specs=[ANY] * nv,
        out_shape=[jax.ShapeDtypeStruct(v.shape, v.dtype) for v in vals],
        input_output_aliases={i: i for i in range(nv)},
        scratch_shapes=[pltpu.SemaphoreType.DMA((3 * npc,)), pltpu.SemaphoreType.DMA((3 * npc,))],
    )(*vals)


def _sibling_swap(dws, *, name):
    nm = len(dws)
    shapes = [jax.ShapeDtypeStruct((dw.shape[0], dw.shape[1] // 2, dw.shape[2]), dw.dtype) for dw in dws]

    def body(*refs):
        ins = refs[:nm]
        theirs = refs[nm:2 * nm]
        ssem, rsem = refs[2 * nm:]
        x, y, c, _ = _place()
        sib = (x, y, 1 - c)
        cps = []
        for m in range(nm):
            G, rh, cols = shapes[m].shape
            rc = _chunk_rows(rh, cols * shapes[m].dtype.itemsize, _row_align(shapes[m].dtype))
            for j in range(G):
                _start_chunks(lambda r0, n, m=m, j=j, rh=rh: pltpu.make_async_remote_copy(
                    src_ref=ins[m].at[j, pl.ds((1 - c) * rh + r0, n)],
                    dst_ref=theirs[m].at[j, pl.ds(r0, n)], send_sem=ssem.at[m], recv_sem=rsem.at[m],
                    device_id=sib, device_id_type=MESH), rh, rc)
            cps.append(pltpu.make_async_remote_copy(
                src_ref=ins[m].at[:, pl.ds((1 - c) * rh, rh), :], dst_ref=theirs[m],
                send_sem=ssem.at[m], recv_sem=rsem.at[m], device_id=sib, device_id_type=MESH))
        for cp in cps:
            cp.wait()

    return pl.pallas_call(
        body, name=name,
        in_specs=[ANY] * nm, out_specs=[ANY] * nm, out_shape=shapes,
        scratch_shapes=[pltpu.SemaphoreType.DMA((nm,)), pltpu.SemaphoreType.DMA((nm,))],
    )(*dws)


HBM = pl.BlockSpec(memory_space=pltpu.HBM)
SEM = pl.BlockSpec(memory_space=pltpu.SEMAPHORE)
EFFECT = pltpu.SideEffectType.DATAFLOW_SIDE_EFFECTING


def _in_hbm(a):
    return pltpu.with_memory_space_constraint(a, pltpu.HBM)


def _exchange_start(parts, *, name):
    nm = len(parts)

    def body(*refs):
        ins = refs[:nm]
        lands = refs[nm:2 * nm]
        ssem, rsem = refs[2 * nm:2 * nm + 2]
        token = refs[-1]
        x, y, c, chips = _place()
        myj = 2 * x + y
        for m in range(nm):
            _, rh, cols = parts[m].shape
            rc = _chunk_rows(rh, cols * parts[m].dtype.itemsize, _row_align(parts[m].dtype))
            for k, (px, py) in enumerate(chips):
                _start_chunks(lambda r0, n, m=m, k=k, px=px, py=py: pltpu.make_async_remote_copy(
                    src_ref=ins[m].at[2 * px + py, pl.ds(r0, n)], dst_ref=lands[m].at[myj, pl.ds(r0, n)],
                    send_sem=ssem.at[3 * m + k], recv_sem=rsem.at[3 * m + k],
                    device_id=(px, py, c), device_id_type=MESH), rh, rc)
        token[...] = jnp.zeros_like(token)

    bufs = [pltpu.HBM(p.shape, p.dtype) for p in parts]
    outs = pl.pallas_call(
        body, name=name,
        out_shape=(pltpu.SemaphoreType.DMA((3 * nm,)), pltpu.SemaphoreType.DMA((3 * nm,)), *bufs, *bufs,
                   jax.ShapeDtypeStruct((8, LANES), F32)),
        in_specs=[HBM] * (2 * nm),
        out_specs=(SEM, SEM, *([HBM] * (2 * nm)), pl.BlockSpec(memory_space=pltpu.VMEM)),
        input_output_aliases={i: 2 + i for i in range(2 * nm)},
        compiler_params=pltpu.CompilerParams(has_side_effects=EFFECT),
    )(*[_in_hbm(p) for p in parts], *[_in_hbm(lax.empty(p.shape, p.dtype)) for p in parts])
    return (outs[0], outs[1], list(outs[2:2 + nm]), list(outs[2 + nm:2 + 2 * nm])), outs[-1]


def _exchange_wait(handle, after, *, name):
    ssem_in, rsem_in, parts, lands = handle
    nm = len(parts)

    def body(*refs):
        ins = refs[:nm]
        lnd = refs[nm:2 * nm]
        ssem, rsem = refs[2 * nm:2 * nm + 2]
        x, y, c, chips = _place()
        for m in range(nm):
            for k, (px, py) in enumerate(chips):
                pj = 2 * px + py
                cp = pltpu.make_async_remote_copy(
                    src_ref=ins[m].at[pj], dst_ref=lnd[m].at[pj],
                    send_sem=ssem.at[3 * m + k], recv_sem=rsem.at[3 * m + k],
                    device_id=(px, py, c), device_id_type=MESH)
                cp.wait_send()
                cp.wait_recv()

    bufs = [pltpu.HBM(p.shape, p.dtype) for p in parts]
    outs = pl.pallas_call(
        body, name=name,
        out_shape=(*bufs, *bufs),
        in_specs=[HBM] * (2 * nm) + [SEM, SEM, ANY],
        out_specs=[HBM] * (2 * nm),
        input_output_aliases={i: i for i in range(2 * nm)},
        compiler_params=pltpu.CompilerParams(has_side_effects=EFFECT),
    )(*parts, *lands, ssem_in, rsem_in, after)
    return list(outs[nm:]), list(outs[:nm])


def _sibling_join(grads, regions, *, name):
    nm = len(grads)
    nr = len(regions)
    shapes = [jax.ShapeDtypeStruct(g.shape, g.dtype) for g in grads]

    def body(*refs):
        outs = refs[nm:2 * nm]
        ssem, rsem = refs[2 * nm:]
        x, y, c, _ = _place()
        sib = (x, y, 1 - c)
        cps = []
        for i, (m, off, rows) in enumerate(regions):
            rh, cols = rows // 2, grads[m].shape[1]
            rc = _chunk_rows(rh, cols * grads[m].dtype.itemsize, _row_align(grads[m].dtype))

            def send(r0, n, i=i, m=m, off=off, rh=rh):
                part = outs[m].at[pl.ds(off + c * rh + r0, n)]
                return pltpu.make_async_remote_copy(
                    src_ref=part, dst_ref=part, send_sem=ssem.at[i], recv_sem=rsem.at[i],
                    device_id=sib, device_id_type=MESH)
            _start_chunks(send, rh, rc)
            cps.append(send(0, rh))
        for i, (m, off, rows) in enumerate(regions):
            rh = rows // 2
            cps[i].wait_send()
            got = outs[m].at[pl.ds(off + (1 - c) * rh, rh)]
            pltpu.make_async_remote_copy(
                src_ref=got, dst_ref=got, send_sem=ssem.at[i], recv_sem=rsem.at[i],
                device_id=sib, device_id_type=MESH).wait_recv()

    return pl.pallas_call(
        body, name=name,
        in_specs=[ANY] * nm, out_specs=[ANY] * nm, out_shape=shapes,
        input_output_aliases={i: i for i in range(nm)},
        scratch_shapes=[pltpu.SemaphoreType.DMA((nr,)), pltpu.SemaphoreType.DMA((nr,))],
    )(*grads)


def _all_reduce_small(pack):
    R, C = pack.shape

    def body(in_ref, out_ref, slots, ssem, rsem):
        x, y, c, _ = _place()
        me = 4 * x + 2 * y + c
        slots[me] = in_ref[...]
        cps = []
        for k in range(1, N_DEV):
            dx, dy, dc = (k >> 2) & 1, (k >> 1) & 1, k & 1
            peer = (x ^ dx, y ^ dy, c ^ dc)
            cp = pltpu.make_async_remote_copy(
                src_ref=in_ref, dst_ref=slots.at[me], send_sem=ssem.at[k], recv_sem=rsem.at[k],
                device_id=peer, device_id_type=MESH)
            cp.start()
            cps.append(cp)
        for k in range(1, N_DEV):
            dx, dy, dc = (k >> 2) & 1, (k >> 1) & 1, k & 1
            got = slots.at[4 * (x ^ dx) + 2 * (y ^ dy) + (c ^ dc)]
            pltpu.make_async_remote_copy(
                src_ref=got, dst_ref=got, send_sem=ssem.at[k], recv_sem=rsem.at[k],
                device_id=(x ^ dx, y ^ dy, c ^ dc), device_id_type=MESH).wait_recv()
        for cp in cps:
            cp.wait_send()
        acc = slots[0]
        for s in range(1, N_DEV):
            acc = acc + slots[s]
        out_ref[...] = acc

    return pl.pallas_call(
        body, name="ar_small",
        in_specs=[pl.BlockSpec(memory_space=pltpu.VMEM)],
        out_specs=pl.BlockSpec(memory_space=pltpu.VMEM),
        out_shape=jax.ShapeDtypeStruct((R, C), F32),
        scratch_shapes=[pltpu.VMEM((N_DEV, R, C), F32),
                        pltpu.SemaphoreType.DMA((N_DEV,)), pltpu.SemaphoreType.DMA((N_DEV,))],
    )(pack)


def kernel(x, positions, mix_norm_pre, mix_norm_post, ffn_norm_pre, ffn_norm_post, ffn_w_gate_up, ffn_w_down, conv_w_in, conv_w, conv_w_out, kv_norm, w_kv, w_q, w_o, loss_target, m_mix_norm_pre, m_mix_norm_post, m_ffn_norm_pre, m_ffn_norm_post, m_ffn_w_gate_up, m_ffn_w_down, m_conv_w_in, m_conv_w, m_conv_w_out, m_kv_norm, m_w_kv, m_w_q, m_w_o, v_mix_norm_pre, v_mix_norm_post, v_ffn_norm_pre, v_ffn_norm_post, v_ffn_w_gate_up, v_ffn_w_down, v_conv_w_in, v_conv_w, v_conv_w_out, v_kv_norm, v_w_kv, v_w_q, v_w_o):
    T, D = x.shape[1], x.shape[2]
    L = ffn_w_gate_up.shape[0]
    n_gu = ffn_w_gate_up.shape[2]
    f_sh = ffn_w_down.shape[1]
    F = N_CHIPS * f_sh
    x0 = x[0]
    tgt = loss_target[0]

    half = HEAD_DIM // 2
    inv_freq = ROPE_THETA ** (-jnp.arange(half, dtype=F32) / half)
    ang = positions[0].astype(F32)[:, None] * inv_freq
    cosv, sinv = jnp.cos(ang), jnp.sin(ang)
    cos2 = jnp.tile(cosv, (1, LANES // half))
    ss2 = jnp.tile(jnp.concatenate([-sinv, sinv], axis=1), (1, LANES // HEAD_DIM))

    def as2d(a):
        return a.reshape(-1, a.shape[-1])

    big = [ffn_w_gate_up, ffn_w_down, conv_w_in, conv_w_out, w_kv, w_q, w_o]
    big_m = [m_ffn_w_gate_up, m_ffn_w_down, m_conv_w_in, m_conv_w_out, m_w_kv, m_w_q, m_w_o]
    big_v = [v_ffn_w_gate_up, v_ffn_w_down, v_conv_w_in, v_conv_w_out, v_w_kv, v_w_q, v_w_o]
    chip = 2 * lax.axis_index("x") + lax.axis_index("y")
    where = jnp.stack([chip, lax.axis_index("c")]).astype(jnp.int32)
    tc = conv_w.shape[2]
    cw_pad = jnp.concatenate([conv_w[0], jnp.zeros((8 - conv_w.shape[1], tc), F32)], axis=0)

    GU0, GU1, WD0, WD1, WCI, WCO, WKV, WQ, WO = range(9)
    shards = [(ffn_w_gate_up, 0), (ffn_w_gate_up, 1), (ffn_w_down, 0), (ffn_w_down, 1), (conv_w_in, 0),
              (conv_w_out, 0), (w_kv[None], 0), (w_q, 0), (w_o, 0)]
    ag_groups = [
        [(WCI, None, 0, D), (WCO, None, 0, D // N_CHIPS)],
        [(GU0, None, 0, D), (WD0, None, 0, f_sh)],
        [(WKV, None, 0, D), (WQ, None, 0, D)],
        [(WO, None, 0, D // N_CHIPS), (GU1, None, 0, D), (WD1, None, 0, f_sh)],
    ]

    def localised(group, idxs):
        return [(idxs.index(p[0]),) + p[1:] for p in group]

    cur = [None] * len(shards)
    first = [WCI, WCO]
    sems0, vals, taps, token = _ag_start(
        [_cast_place(*shards[i], where, BF, name=f"place{i}") for i in first],
        _cast_place(cw_pad[None], 0, where, F32, name="place_taps"), [localised(ag_groups[0], first)],
        name="ag_start0")
    for i, v in zip(first, vals):
        cur[i] = v
    rest = [i for i in range(len(shards)) if i not in first]
    sems1, vals, _, _ = _ag_start(
        [_cast_place(*shards[i], where, BF, name=f"place{i}", dep=token) for i in rest], None,
        [localised(g, rest) for g in ag_groups[1:]], name="ag_start1")
    for i, v in zip(rest, vals):
        cur[i] = v
    ag_sems = sems0 + sems1

    def gather_group(g, after):
        nonlocal taps
        idxs = sorted({p[0] for p in ag_groups[g]})
        local = localised(ag_groups[g], idxs)
        vals, landed_taps = _ag_wait(ag_sems[g], [cur[i] for i in idxs], taps if g == 0 else None, local, after,
                                     name=f"ag_wait{g}")
        if g == 0:
            taps = landed_taps
        vals = _ag_forward(vals, local, name=f"ag_forward{g}")
        for i, v in zip(idxs, vals):
            cur[i] = v

    def row(a, i):
        return a[i:i + 1]

    gather_group(0, None)
    wci, wco, cw = cur[WCI], cur[WCO].reshape(1, D, D), taps
    z, hn_m0 = _norm_matmul(x0, row(mix_norm_pre, 0), wci, cos2, ss2, name="f0_conv_in",
                            rope_shards=0, scale=1.0, out_dtype=BF)
    vmix = _conv_fwd(z, cw, name="f0_conv")
    y0, h1 = _matmul_postnorm(vmix, wco, 0, row(mix_norm_post, 0), x0, name="f0_conv_out")
    gather_group(1, h1)
    wgu0, wd0 = cur[GU0], cur[WD0].reshape(1, F, D)
    g0, u0, a0, hn_f0 = _norm_swiglu(h1, row(ffn_norm_pre, 0), wgu0, 0, name="f0_gate_up")
    f0, h2 = _matmul_postnorm(a0, wd0, 0, row(ffn_norm_post, 0), h1, name="f0_down")

    gather_group(2, h2)
    wkv, wq = cur[WKV], cur[WQ]
    kv_all, hn_kv = _norm_matmul(h2, kv_norm.reshape(1, D), wkv, cos2, ss2, name="f1_kv",
                                 rope_shards=N_CHIPS // 2, scale=1.0, out_dtype=BF)
    q_all, hn_m1 = _norm_matmul(h2, row(mix_norm_pre, 1), wq, cos2, ss2, name="f1_q",
                                rope_shards=N_CHIPS, scale=HEAD_DIM ** -0.5, out_dtype=BF)
    o_att, lse = _attn_fwd(q_all, kv_all, name="f1_attn")
    gather_group(3, o_att)
    wgu1, wd1, wo = cur[GU1], cur[WD1].reshape(1, F, D), cur[WO].reshape(1, D, D)
    y1, h3 = _matmul_postnorm(o_att, wo, 0, row(mix_norm_post, 1), h2, name="f1_attn_out")
    g1, u1, a1, hn_f1 = _norm_swiglu(h3, row(ffn_norm_pre, 1), wgu1, 0, name="f1_gate_up")
    f1, h4 = _matmul_postnorm(a1, wd1, 0, row(ffn_norm_post, 1), h3, name="f1_down")

    dh4, sq = _loss_head(h4, tgt, name="loss_head")
    loss_part = 0.5 * sq[0, 0] / D

    gu_shape = (N_CHIPS, D, n_gu)
    in_chips = lambda a: a.reshape(N_CHIPS, -1, a.shape[-1])

    def scatter_start(dws, tag):
        theirs = _sibling_swap(dws, name=f"rs_swap_{tag}")
        parts = [_pair_sum(dw, t, where, name=f"rs_pair_sum_{tag}{i}") for i, (dw, t) in enumerate(zip(dws, theirs))]
        return _exchange_start(parts, name=f"rs_exchange_start_{tag}")

    dyf1, dg1, du1, d_ffn_post1 = _postnorm_bwd_swiglu(dh4, f1, row(ffn_norm_post, 1), wd1, 0, g1, u1,
                                                       name="b1_down")
    dwd1 = _grad_matmul(a1, dyf1, (2, F // 2, D), F // 2, D, lambda i, j: (i, 0, 0), None, name="b1_dw_down")
    dwgu1 = _grad_matmul(hn_f1, dg1, gu_shape, D, n_gu, lambda i, j: (j, 0, 0), None, name="b1_dw_gate")
    dwgu1 = _grad_matmul(hn_f1, du1, gu_shape, D, n_gu, lambda i, j: (j + 2, 0, 0), dwgu1, name="b1_dw_up")
    dh3, d_ffn_pre1 = _matmul_prenorm_bwd((dg1, du1), wgu1, 0, h3, row(ffn_norm_pre, 1), dh4, name="b1_gate_up")

    dy1, do, d_mix_post1 = _postnorm_bwd_matmul(dh3, y1, row(mix_norm_post, 1), wo, 0, name="b1_attn_out",
                                                da_dtype=F32)
    dwo = _grad_matmul(o_att, dy1, (1, D, D), D, D, lambda i, j: (0, 0, 0), None, name="b1_dw_o")
    rs_a, token = scatter_start([dwgu1, in_chips(dwd1), in_chips(dwo)], "a")
    prev = None
    for gi, (window, dil) in enumerate(BRANCHES):
        prev = _attn_bwd(q_all, kv_all, do, o_att, lse, cos2, ss2, gi, dil, prev, name=f"b1_attn{gi}",
                         dep=token if gi == 0 else None)
    dq_all, dk_all, dv_all = prev
    n_q = wq.shape[2]
    n_kv = wkv.shape[2]
    dwq = _grad_matmul(hn_m1, dq_all, (N_CHIPS, D, n_q), D, n_q, lambda i, j: (j, 0, 0), None, name="b1_dw_q")
    dwkv = _grad_matmul(hn_kv, dk_all, (N_CHIPS, D, n_kv), D, n_kv, lambda i, j: (j, 0, 0), None, name="b1_dw_k")
    dwkv = _grad_matmul(hn_kv, dv_all, (N_CHIPS, D, n_kv), D, n_kv, lambda i, j: (j + 2, 0, 0), dwkv, name="b1_dw_v")
    dh2, d_mix_pre1 = _matmul_prenorm_bwd((dq_all,), wq, 0, h2, row(mix_norm_pre, 1), dh3, name="b1_q")
    dh2, d_kv_norm = _matmul_prenorm_bwd((dk_all, dv_all), wkv, 0, h2, kv_norm.reshape(1, D), dh2, name="b1_kv")
    rs_b, token = scatter_start([dwkv, dwq], "b")

    dyf0, dg0, du0, d_ffn_post0 = _postnorm_bwd_swiglu(dh2, f0, row(ffn_norm_post, 0), wd0, 0, g0, u0,
                                                       name="b0_down", dep=token)
    dwd0 = _grad_matmul(a0, dyf0, (2, F // 2, D), F // 2, D, lambda i, j: (i, 0, 0), None, name="b0_dw_down")
    dwgu0 = _grad_matmul(hn_f0, dg0, gu_shape, D, n_gu, lambda i, j: (j, 0, 0), None, name="b0_dw_gate")
    dwgu0 = _grad_matmul(hn_f0, du0, gu_shape, D, n_gu, lambda i, j: (j + 2, 0, 0), dwgu0, name="b0_dw_up")
    dh1, d_ffn_pre0 = _matmul_prenorm_bwd((dg0, du0), wgu0, 0, h1, row(ffn_norm_pre, 0), dh2, name="b0_gate_up")
    rs_c, token = scatter_start([dwgu0, in_chips(dwd0)], "c")

    dy0, dvmix, d_mix_post0 = _postnorm_bwd_matmul(dh1, y0, row(mix_norm_post, 0), wco, 0, name="b0_conv_out",
                                                   da_dtype=BF, dep=token)
    dwco = _grad_matmul(vmix, dy0, (1, D, D), D, D, lambda i, j: (0, 0, 0), None, name="b0_dw_conv_out")
    dz, dcw = _conv_bwd(z, cw, dvmix, name="b0_conv")
    n_ci = wci.shape[2]
    dwci = _grad_matmul(hn_m0, dz, (N_CHIPS, D, n_ci), D, n_ci, lambda i, j: (j, 0, 0), None, name="b0_dw_conv_in")
    dx, d_mix_pre0 = _matmul_prenorm_bwd((dz,), wci, 0, x0, row(mix_norm_pre, 0), dh1, name="b0_conv_in")

    pack = jnp.concatenate([
        d_mix_pre0, d_mix_pre1, d_mix_post0, d_mix_post1, d_ffn_pre0, d_ffn_pre1, d_ffn_post0, d_ffn_post1,
        d_kv_norm, dcw[0:3], jnp.full((1, D), loss_part, F32),
        jnp.zeros((SMALL_ROWS - 13, D), F32)], axis=0)
    red = _all_reduce_small(pack)
    loss = red[12, 0]
    myj = 2 * lax.axis_index("x") + lax.axis_index("y")
    g_conv_w = lax.dynamic_slice(red, (9, myj * tc), (3, tc))

    zeros7 = jnp.zeros((SMALL_ROWS - 9, D), F32)
    w_small = jnp.concatenate([mix_norm_pre, mix_norm_post, ffn_norm_pre, ffn_norm_post, kv_norm.reshape(1, D), zeros7], axis=0)
    m_small = jnp.concatenate([m_mix_norm_pre, m_mix_norm_post, m_ffn_norm_pre, m_ffn_norm_post, m_kv_norm.reshape(1, D), zeros7], axis=0)
    v_small = jnp.concatenate([v_mix_norm_pre, v_mix_norm_post, v_ffn_norm_pre, v_ffn_norm_post, v_kv_norm.reshape(1, D), zeros7], axis=0)
    d_small, nm_small, nv_small = _adamw(w_small, red, m_small, v_small, name="adamw_small")

    pad5 = jnp.zeros((5, tc), F32)
    d_cw, nm_cw, nv_cw = _adamw(cw_pad, jnp.concatenate([g_conv_w, pad5], axis=0),
                                jnp.concatenate([m_conv_w[0], pad5], axis=0),
                                jnp.concatenate([v_conv_w[0], pad5], axis=0), name="adamw_conv_w")

    rs_d, _ = scatter_start([dwci, in_chips(dwco)], "d")

    pieces = {"a": [(0, D), (1, f_sh), (6, 0)], "b": [(4, 0), (5, 0)], "c": [(0, 0), (1, 0)], "d": [(2, 0), (3, 0)]}
    grads2d = [None] * len(big)
    big_out = [None] * len(big)

    def finish(groups, after, tag):
        regions, idxs = [], []
        for gtag, handle in groups:
            landed, parts = _exchange_wait(handle, after, name=f"rs_exchange_wait_{gtag}")
            for i, (l, p, (wi, off)) in enumerate(zip(landed, parts, pieces[gtag])):
                total = as2d(big[wi]).shape[0]
                grads2d[wi] = _chip_sum(l, p, where, total, off, grads2d[wi], name=f"rs_chip_sum_{gtag}{i}")
                if wi not in idxs:
                    idxs.append(wi)
                regions.append((idxs.index(wi), off, 2 * l.shape[1]))
        joined = _sibling_join([grads2d[wi] for wi in idxs], regions, name=f"rs_sibling_join_{tag}")
        for wi, gr in zip(idxs, joined):
            w = big[wi]
            d_, m_, v_, g_ = _adamw(as2d(w), gr, as2d(big_m[wi]), as2d(big_v[wi]), name=f"adamw{wi}",
                                    emit_grad=True)
            big_out[wi] = (g_.reshape(w.shape), d_.reshape(w.shape), m_.reshape(w.shape), v_.reshape(w.shape))

    finish([("a", rs_a), ("b", rs_b), ("c", rs_c)], dx, "abc")
    finish([("d", rs_d)], big_out[0][1], "d")

    def small(a):
        return (a[0:2], a[2:4], a[4:6], a[6:8])

    def assemble(sm, cwv, kind):
        pre, post, fpre, fpost = small(sm)
        b = [t[kind] for t in big_out]
        return [pre, post, fpre, fpost, b[0], b[1], b[2], cwv[0:3].reshape(conv_w.shape), b[3],
                sm[8], b[4], b[5].reshape(w_q.shape), b[6].reshape(w_o.shape)]

    grads = assemble(red, jnp.concatenate([g_conv_w, pad5], axis=0), 0)
    deltas = assemble(d_small, d_cw, 1)
    new_m = assemble(nm_small, nm_cw, 2)
    new_v = assemble(nv_small, nv_cw, 3)
    return (loss, dx.reshape(x.shape), *grads, *deltas, *new_m, *new_v)
```

```python
import functools

import jax
import jax.numpy as jnp
from jax import lax
from jax.experimental import pallas as pl
from jax.experimental.pallas import tpu as pltpu

HEAD_DIM = 64
BAND = 128
BRANCHES = ((128, 1), (512, 4), (2048, 16))
ROPE_THETA = 10000.0
RMS_EPS = 1e-6
NEG_INF = -1e30
ADAM_LR = 0.001
ADAM_B1 = 0.9
ADAM_B2 = 0.999
ADAM_EPS = 1e-08
ADAM_WD = 0.01
ADAM_STEP = 10

N_CHIPS = 4
N_DEV = 8
LANES = 128
MXU_COLS = 256
ROW_BLOCK = 512
ROW_BLOCK_WIDE = 1024
ATTN_BLOCK_ROWS = 2048
VMEM_LIMIT = 56 * 1024 * 1024
SMALL_ROWS = 16
ADAMW_BLOCK_BYTES = 1024 * 1024
DMA_CHUNK_BYTES = 512 * 1024

BF = jnp.bfloat16
F32 = jnp.float32
MESH = pl.DeviceIdType.MESH
ANY = pl.BlockSpec(memory_space=pl.ANY)


def _cp(*sem):
    return pltpu.CompilerParams(dimension_semantics=sem, vmem_limit_bytes=VMEM_LIMIT)


def _rot_half(t, first):
    return jnp.where(first, pltpu.roll(t, 96, 1), pltpu.roll(t, 32, 1))


def _sigmoid(x):
    return pl.reciprocal(1.0 + jnp.exp(-x), approx=True)


def _col_chunks(n):
    return [(c0, min(MXU_COLS, n - c0)) for c0 in range(0, n, MXU_COLS)]


def _first_half_mask(rows):
    lane = lax.broadcasted_iota(jnp.int32, (rows, LANES), 1)
    return (lane % HEAD_DIM) < (HEAD_DIM // 2)


def _normed_rows(j, rows, x_ref, g_ref, xn_ref, xs, last_start, tm):
    @pl.when(j == 0)
    def _():
        xv = x_ref[...]
        r = lax.rsqrt(jnp.mean(xv * xv, axis=-1, keepdims=True) + RMS_EPS)
        xn = (xv * r * g_ref[...]).astype(BF)
        xs[rows, :] = xn
        xn_ref[...] = xn

    @pl.when(j > 0)
    def _():
        xn_ref[...] = xs[pl.ds(last_start, tm), :]


def _norm_matmul(x, gain, wg, cos2, ss2, *, name, rope_shards, scale, out_dtype):
    T, D = x.shape
    n = wg.shape[2]
    tm = min(ROW_BLOCK_WIDE, T)
    ni = T // tm

    def body(x_ref, g_ref, w_ref, cos_ref, ss_ref, y_ref, xn_ref, xs):
        j = pl.program_id(0)
        rows = pl.ds(pl.multiple_of(pl.program_id(1) * tm, tm), tm)
        _normed_rows(j, rows, x_ref, g_ref, xn_ref, xs, (ni - 1) * tm, tm)
        acc = jnp.dot(xs[rows, :], w_ref[...], preferred_element_type=F32)

        def plain():
            y_ref[...] = acc.astype(out_dtype)

        def rope():
            cosv = cos_ref[...]
            ssv = ss_ref[...]
            first = _first_half_mask(tm)
            for ci in range(n // LANES):
                t = acc[:, ci * LANES:(ci + 1) * LANES]
                y = (t * cosv + _rot_half(t, first) * ssv) * scale
                y_ref[:, ci * LANES:(ci + 1) * LANES] = y.astype(out_dtype)

        if rope_shards == 0:
            plain()
        elif rope_shards == N_CHIPS:
            rope()
        else:
            pl.when(j < rope_shards)(rope)
            pl.when(j >= rope_shards)(plain)

    first_pass = lambda j, i: (jnp.where(j == 0, i, ni - 1), 0)
    return pl.pallas_call(
        body, name=name,
        grid=(N_CHIPS, ni),
        in_specs=[
            pl.BlockSpec((tm, D), first_pass),
            pl.BlockSpec((1, D), lambda j, i: (0, 0)),
            pl.BlockSpec((None, D, n), lambda j, i: (j, 0, 0)),
            pl.BlockSpec((tm, LANES), lambda j, i: (i, 0)),
            pl.BlockSpec((tm, LANES), lambda j, i: (i, 0)),
        ],
        out_specs=[
            pl.BlockSpec((tm, n), lambda j, i: (i, j)),
            pl.BlockSpec((tm, D), first_pass),
        ],
        out_shape=[jax.ShapeDtypeStruct((T, N_CHIPS * n), out_dtype),
                   jax.ShapeDtypeStruct((T, D), BF)],
        scratch_shapes=[pltpu.VMEM((T, D), BF)],
        compiler_params=_cp("arbitrary", "arbitrary"),
    )(x, gain, wg, cos2, ss2)


def _norm_swiglu(x, gain, wg, layer, *, name):
    T, D = x.shape
    n = wg.shape[2]
    tm = min(ROW_BLOCK, T)

    ni = T // tm

    def body(x_ref, g_ref, wg_ref, wu_ref, go_ref, uo_ref, ao_ref, xn_ref, xs):
        j = pl.program_id(0)
        rows = pl.ds(pl.multiple_of(pl.program_id(1) * tm, tm), tm)
        _normed_rows(j, rows, x_ref, g_ref, xn_ref, xs, (ni - 1) * tm, tm)
        g = jnp.dot(xs[rows, :], wg_ref[...], preferred_element_type=F32)
        u = jnp.dot(xs[rows, :], wu_ref[...], preferred_element_type=F32)
        go_ref[...] = g.astype(BF)
        uo_ref[...] = u.astype(BF)
        ao_ref[...] = (g * _sigmoid(g) * u).astype(BF)

    half = N_CHIPS // 2
    first_pass = lambda j, i: (jnp.where(j == 0, i, ni - 1), 0)
    act = jax.ShapeDtypeStruct((T, half * n), BF)
    return pl.pallas_call(
        body, name=name,
        grid=(half, ni),
        in_specs=[
            pl.BlockSpec((tm, D), first_pass),
            pl.BlockSpec((1, D), lambda j, i: (0, 0)),
            pl.BlockSpec((None, D, n), lambda j, i: (j, layer, 0)),
            pl.BlockSpec((None, D, n), lambda j, i: (j + half, layer, 0)),
        ],
        out_specs=[
            pl.BlockSpec((tm, n), lambda j, i: (i, j)),
            pl.BlockSpec((tm, n), lambda j, i: (i, j)),
            pl.BlockSpec((tm, n), lambda j, i: (i, j)),
            pl.BlockSpec((tm, D), first_pass),
        ],
        out_shape=[act, act, act, jax.ShapeDtypeStruct((T, D), BF)],
        scratch_shapes=[pltpu.VMEM((T, D), BF)],
        compiler_params=_cp("arbitrary", "arbitrary"),
    )(x, gain, wg, wg)


def _matmul_postnorm(a, w3, widx, gain, h_old, *, name):
    T, K = a.shape
    D = w3.shape[2]
    tm = min(ROW_BLOCK_WIDE, T)

    def body(a_ref, w_ref, g_ref, h_ref, y_ref, hn_ref):
        y = jnp.dot(a_ref[...].astype(BF), w_ref[...], preferred_element_type=F32)
        y_ref[...] = y.astype(BF)
        r = lax.rsqrt(jnp.mean(y * y, axis=-1, keepdims=True) + RMS_EPS)
        hn_ref[...] = h_ref[...] + y * r * g_ref[...]

    return pl.pallas_call(
        body, name=name,
        grid=(T // tm,),
        in_specs=[
            pl.BlockSpec((tm, K), lambda i: (i, 0)),
            pl.BlockSpec((None, K, D), lambda i: (widx, 0, 0)),
            pl.BlockSpec((1, D), lambda i: (0, 0)),
            pl.BlockSpec((tm, D), lambda i: (i, 0)),
        ],
        out_specs=[pl.BlockSpec((tm, D), lambda i: (i, 0)),
                   pl.BlockSpec((tm, D), lambda i: (i, 0))],
        out_shape=[jax.ShapeDtypeStruct((T, D), BF), jax.ShapeDtypeStruct((T, D), F32)],
        compiler_params=_cp("parallel"),
    )(a, w3, gain, h_old)


def _loss_head(h, target, *, name):
    T, D = h.shape
    tm = min(ROW_BLOCK, T)

    def body(h_ref, t_ref, dh_ref, s_ref):
        i = pl.program_id(0)

        @pl.when(i == 0)
        def _():
            s_ref[...] = jnp.zeros_like(s_ref)

        e = h_ref[...] - t_ref[...]
        dh_ref[...] = e * (1.0 / D)
        s_ref[...] += jnp.sum(e * e)

    return pl.pallas_call(
        body, name=name,
        grid=(T // tm,),
        in_specs=[pl.BlockSpec((tm, D), lambda i: (i, 0)), pl.BlockSpec((tm, D), lambda i: (i, 0))],
        out_specs=[pl.BlockSpec((tm, D), lambda i: (i, 0)), pl.BlockSpec((8, LANES), lambda i: (0, 0))],
        out_shape=[jax.ShapeDtypeStruct((T, D), F32), jax.ShapeDtypeStruct((8, LANES), F32)],
        compiler_params=_cp("arbitrary"),
    )(h, target)


def _shift_down(u, k):
    row = lax.broadcasted_iota(jnp.int32, u.shape, 0)
    return jnp.where(row >= k, pltpu.roll(u, k, 0), 0.0)


def _shift_up(u, k):
    T = u.shape[0]
    row = lax.broadcasted_iota(jnp.int32, u.shape, 0)
    return jnp.where(row < T - k, pltpu.roll(u, T - k, 0), 0.0)


def _conv_fwd(z, cw, *, name):
    T = z.shape[0]
    D = z.shape[1] // 3
    tc = cw.shape[2]
    nb = D // tc

    def body(b_ref, c_ref, h_ref, w_ref, o_ref):
        u = c_ref[...].astype(F32) * h_ref[...].astype(F32)
        w = w_ref[...]
        conv = w[2:3] * u + w[1:2] * _shift_down(u, 1) + w[0:1] * _shift_down(u, 2)
        o_ref[...] = (b_ref[...].astype(F32) * conv).astype(BF)

    return pl.pallas_call(
        body, name=name,
        grid=(nb,),
        in_specs=[
            pl.BlockSpec((T, tc), lambda j: (0, j)),
            pl.BlockSpec((T, tc), lambda j: (0, nb + j)),
            pl.BlockSpec((T, tc), lambda j: (0, 2 * nb + j)),
            pl.BlockSpec((None, 8, tc), lambda j: (j, 0, 0)),
        ],
        out_specs=pl.BlockSpec((T, tc), lambda j: (0, j)),
        out_shape=jax.ShapeDtypeStruct((T, D), BF),
        compiler_params=_cp("parallel"),
    )(z, z, z, cw)


def _conv_bwd(z, cw, dv, *, name):
    T = z.shape[0]
    D = z.shape[1] // 3
    tc = LANES
    nb = D // tc
    per = cw.shape[2] // tc

    def body(b_ref, c_ref, h_ref, w_ref, dv_ref, dz_ref, dw_ref, stage, sems):
        j = pl.program_id(0)
        slot = j % 2

        def slab(p, jj, s):
            col = pl.multiple_of((p * nb + jj) * tc, tc)
            return pltpu.make_async_copy(stage.at[s, p], dz_ref.at[:, pl.ds(col, tc)], sems.at[s, p])

        @pl.when(j >= 2)
        def _():
            for p in range(3):
                slab(p, j - 2, slot).wait()

        c = c_ref[...].astype(F32)
        h = h_ref[...].astype(F32)
        u = c * h
        u1 = _shift_down(u, 1)
        u2 = _shift_down(u, 2)
        w = w_ref[...]
        dvv = dv_ref[...].astype(F32)
        dconv = dvv * b_ref[...].astype(F32)
        du = w[2:3] * dconv + w[1:2] * _shift_up(dconv, 1) + w[0:1] * _shift_up(dconv, 2)
        rows = lax.broadcasted_iota(jnp.int32, (8, tc), 0)
        dw_ref[...] = jnp.where(rows == 0, jnp.sum(dconv * u2, axis=0, keepdims=True),
                                jnp.where(rows == 1, jnp.sum(dconv * u1, axis=0, keepdims=True),
                                          jnp.where(rows == 2, jnp.sum(dconv * u, axis=0, keepdims=True), 0.0)))
        stage[slot, 0] = (dvv * (w[2:3] * u + w[1:2] * u1 + w[0:1] * u2)).astype(BF)
        stage[slot, 1] = (du * h).astype(BF)
        stage[slot, 2] = (du * c).astype(BF)
        for p in range(3):
            slab(p, j, slot).start()

        @pl.when(j == nb - 1)
        def _():
            for p in range(3):
                slab(p, j, slot).wait()
            if nb > 1:
                for p in range(3):
                    slab(p, j - 1, 1 - slot).wait()

    return pl.pallas_call(
        body, name=name,
        grid=(nb,),
        in_specs=[
            pl.BlockSpec((T, tc), lambda j: (0, j)),
            pl.BlockSpec((T, tc), lambda j: (0, nb + j)),
            pl.BlockSpec((T, tc), lambda j: (0, 2 * nb + j)),
            pl.BlockSpec((None, 8, tc), lambda j: (j // per, 0, j % per)),
            pl.BlockSpec((T, tc), lambda j: (0, j)),
        ],
        out_specs=[ANY, pl.BlockSpec((8, tc), lambda j: (0, j))],
        out_shape=[jax.ShapeDtypeStruct((T, 3 * D), BF), jax.ShapeDtypeStruct((8, D), F32)],
        scratch_shapes=[pltpu.VMEM((2, 3, T, tc), BF), pltpu.SemaphoreType.DMA((2, 3))],
        compiler_params=_cp("arbitrary"),
    )(z, z, z, cw, dv)


def _strided(base, count, d):
    return pl.ds(base, count, stride=d) if d > 1 else pl.ds(pl.multiple_of(base, BAND), count)


def _fill_band_bias(bias):
    qi = lax.broadcasted_iota(jnp.int32, (2 * BAND, 2 * BAND), 0) % BAND
    kj = lax.broadcasted_iota(jnp.int32, (2 * BAND, 2 * BAND), 1)
    dist = qi + BAND - kj
    band = (dist >= 0) & (dist <= BAND)
    bias[0] = jnp.where(band & (kj >= BAND), 0.0, NEG_INF)
    bias[1] = jnp.where(band, 0.0, NEG_INF)


def _attn_block_rows(T):
    return min(ATTN_BLOCK_ROWS, T)


def _head_mask():
    lane = lax.broadcasted_iota(jnp.int32, (2 * BAND, LANES), 1)
    row = lax.broadcasted_iota(jnp.int32, (2 * BAND, LANES), 0)
    return (lane < HEAD_DIM) == (row < BAND)


def _attn_fwd(q_all, kv_all, *, name):
    T = q_all.shape[0]
    NB = len(BRANCHES)
    Dm = q_all.shape[1] // NB
    HP = Dm // LANES
    R = _attn_block_rows(T)
    units = R // BAND
    dmax = max(d for _, d in BRANCHES)

    def body(*refs):
        ins = refs[:5 * NB]
        o_ref, l_ref, qbuf, kbuf, vbuf, o_s, l_s, bias = refs[5 * NB:]
        n = pl.program_id(0)
        pl.when((n == 0) & (pl.program_id(1) == 0))(lambda: _fill_band_bias(bias))
        hm = _head_mask()
        low = lax.broadcasted_iota(jnp.int32, (BAND, LANES), 1) < HEAD_DIM

        for g, (_, d) in enumerate(BRANCHES):
            q_ref, kp_ref, kc_ref, vp_ref, vc_ref = ins[5 * g:5 * g + 5]
            pr = BAND * d
            qbuf[...] = q_ref[...].astype(F32)
            kbuf[0:pr, :] = kp_ref[...].astype(F32)
            kbuf[pr:pr + R, :] = kc_ref[...].astype(F32)
            vbuf[0:pr, :] = vp_ref[...].astype(F32)
            vbuf[pr:pr + R, :] = vc_ref[...].astype(F32)

            def unit(u, carry, g=g, d=d, pr=pr):
                sub = u // d
                base = sub * pr + (u - sub * d)
                q = qbuf[_strided(base, BAND, d), :]
                q2 = jnp.where(hm, jnp.concatenate([q, q], axis=0), 0.0).astype(BF)
                k2 = kbuf[_strided(base, 2 * BAND, d), :].astype(BF)
                v2 = vbuf[_strided(base, 2 * BAND, d), :].astype(BF)
                s = lax.dot_general(q2, k2, (((1,), (1,)), ((), ())), preferred_element_type=F32)
                s = s + bias[((n > 0) | (sub > 0)).astype(jnp.int32)]
                m = jnp.max(s, axis=-1, keepdims=True)
                p = jnp.exp(s - m)
                l = jnp.sum(p, axis=-1, keepdims=True)
                pv = jnp.dot(p.astype(BF), v2, preferred_element_type=F32) * (1.0 / l)
                lse = m + jnp.log(l)
                o_s[g, _strided(base, BAND, d), :] = jnp.where(low, pv[:BAND], pv[BAND:])
                l_s[g, _strided(base, BAND, d), :] = jnp.where(low, lse[:BAND], lse[BAND:])
                return carry

            lax.fori_loop(0, units, unit, 0, unroll=8)

        def merge(i, carry):
            sl = pl.ds(pl.multiple_of(i * BAND, BAND), BAND)
            lv = [l_s[g, sl, :] for g in range(NB)]
            m = functools.reduce(jnp.maximum, lv)
            e = [jnp.exp(v - m) for v in lv]
            tot = functools.reduce(jnp.add, e)
            inv = 1.0 / tot
            o_ref[sl, :] = functools.reduce(jnp.add, [(e[g] * inv) * o_s[g, sl, :] for g in range(NB)])
            l_ref[sl, :] = m + jnp.log(tot)
            return carry

        lax.fori_loop(0, units, merge, 0)

    in_specs, args = [], []
    for g, (_, d) in enumerate(BRANCHES):
        per = R // (BAND * d)
        for col, rows, idx in (
                (g * HP, R, lambda n, hp: n),
                (g * HP, BAND * d, lambda n, hp, per=per: jnp.maximum(n * per - 1, 0)),
                (g * HP, R, lambda n, hp: n),
                ((NB + g) * HP, BAND * d, lambda n, hp, per=per: jnp.maximum(n * per - 1, 0)),
                ((NB + g) * HP, R, lambda n, hp: n)):
            in_specs.append(pl.BlockSpec((rows, LANES), lambda n, hp, col=col, idx=idx: (idx(n, hp), col + hp)))
        args += [q_all, kv_all, kv_all, kv_all, kv_all]
    out = pl.BlockSpec((R, LANES), lambda n, hp: (n, hp))
    return pl.pallas_call(
        body, name=name,
        grid=(T // R, HP),
        in_specs=in_specs,
        out_specs=[out, out],
        out_shape=[jax.ShapeDtypeStruct((T, Dm), F32), jax.ShapeDtypeStruct((T, Dm), F32)],
        scratch_shapes=[pltpu.VMEM((R, LANES), F32),
                        pltpu.VMEM((BAND * dmax + R, LANES), F32), pltpu.VMEM((BAND * dmax + R, LANES), F32),
                        pltpu.VMEM((NB, R, LANES), F32), pltpu.VMEM((NB, R, LANES), F32),
                        pltpu.VMEM((2, 2 * BAND, 2 * BAND), F32)],
        compiler_params=_cp("arbitrary", "arbitrary"),
    )(*args)


def _attn_bwd(q_all, kv_all, do, o, lse, cos2, ss2, g, d, prev, *, name, dep=None):
    T = q_all.shape[0]
    NB = len(BRANCHES)
    Dm = q_all.shape[1] // NB
    HP = Dm // LANES
    R = _attn_block_rows(T)
    nblk = T // R
    units = R // BAND
    pr = BAND * d
    per = R // pr
    scale = HEAD_DIM ** -0.5

    def rope_bwd(t, cosv, ssv, first):
        return t * cosv - _rot_half(t, first) * ssv

    def body(q_ref, kp_ref, kc_ref, vp_ref, vc_ref, do_ref, o_ref, l_ref, cos_ref, ss_ref, *rest):
        dq_ref, dk_ref, dv_ref, qbuf, kbuf, vbuf, dq_s, dk_s, dv_s, pend_k, pend_v, bias = rest[-12:]
        i = pl.program_id(1)
        n = nblk - 1 - i
        pl.when((i == 0) & (pl.program_id(0) == 0))(lambda: _fill_band_bias(bias))
        hm = _head_mask()
        low = lax.broadcasted_iota(jnp.int32, (BAND, LANES), 1) < HEAD_DIM

        qbuf[...] = q_ref[...].astype(F32)
        kbuf[0:pr, :] = kp_ref[...].astype(F32)
        kbuf[pr:pr + R, :] = kc_ref[...].astype(F32)
        vbuf[0:pr, :] = vp_ref[...].astype(F32)
        vbuf[pr:pr + R, :] = vc_ref[...].astype(F32)

        @pl.when(i == 0)
        def _():
            pend_k[...] = jnp.zeros_like(pend_k)
            pend_v[...] = jnp.zeros_like(pend_v)

        def unit(u, carry):
            sub = per - 1 - u // d
            cls = u % d
            base = sub * pr + cls
            sl = _strided(base, BAND, d)
            sl2 = _strided(base, 2 * BAND, d)
            q = qbuf[sl, :]
            dov = do_ref[sl, :]
            ov = o_ref[sl, :]
            lv = l_ref[sl, :]
            q2 = jnp.where(hm, jnp.concatenate([q, q], axis=0), 0.0).astype(BF)
            do2 = jnp.where(hm, jnp.concatenate([dov, dov], axis=0), 0.0)
            oo = dov * ov
            delta = jnp.sum(jnp.where(hm, jnp.concatenate([oo, oo], axis=0), 0.0), axis=-1, keepdims=True)
            lse2 = jnp.concatenate([lv[:, 0:1], lv[:, HEAD_DIM:HEAD_DIM + 1]], axis=0)
            do2 = do2.astype(BF)
            k2 = kbuf[sl2, :].astype(BF)
            v2 = vbuf[sl2, :].astype(BF)
            s = lax.dot_general(q2, k2, (((1,), (1,)), ((), ())), preferred_element_type=F32)
            p = jnp.exp(s + bias[((n > 0) | (sub > 0)).astype(jnp.int32)] - lse2)
            dp = lax.dot_general(do2, v2, (((1,), (1,)), ((), ())), preferred_element_type=F32)
            ds = (p * (dp - delta)).astype(BF)
            dq2 = jnp.dot(ds, k2, preferred_element_type=F32)
            dq = jnp.where(low, dq2[:BAND], dq2[BAND:])
            dq_s[sl, :] = dq
            tn = (((0,), (0,)), ((), ()))
            dk2 = lax.dot_general(ds, q2, tn, preferred_element_type=F32)
            dv2 = lax.dot_general(p.astype(BF), do2, tn, preferred_element_type=F32)
            dk_s[sl, :] = dk2[BAND:] + pend_k[cls]
            dv_s[sl, :] = dv2[BAND:] + pend_v[cls]
            pend_k[cls] = dk2[:BAND]
            pend_v[cls] = dv2[:BAND]
            return carry

        lax.fori_loop(0, units, unit, 0, unroll=8)

        whole = _first_half_mask(R)
        dq_ref[...] = (rope_bwd(dq_s[...], cos_ref[...], ss_ref[...], whole) * scale).astype(BF)
        dk_ref[...] = rope_bwd(dk_s[...], cos_ref[...], ss_ref[...], whole).astype(BF)
        dv_ref[...] = dv_s[...].astype(BF)

    blk = (R, LANES)
    pblk = (pr, LANES)
    cur = lambda hp, i: nblk - 1 - i
    prv = lambda hp, i: jnp.maximum((nblk - 1 - i) * per - 1, 0)
    in_specs = [
        pl.BlockSpec(blk, lambda hp, i: (cur(hp, i), g * HP + hp)),
        pl.BlockSpec(pblk, lambda hp, i: (prv(hp, i), g * HP + hp)),
        pl.BlockSpec(blk, lambda hp, i: (cur(hp, i), g * HP + hp)),
        pl.BlockSpec(pblk, lambda hp, i: (prv(hp, i), (NB + g) * HP + hp)),
        pl.BlockSpec(blk, lambda hp, i: (cur(hp, i), (NB + g) * HP + hp)),
        pl.BlockSpec(blk, lambda hp, i: (cur(hp, i), hp)),
        pl.BlockSpec(blk, lambda hp, i: (cur(hp, i), hp)),
        pl.BlockSpec(blk, lambda hp, i: (cur(hp, i), hp)),
        pl.BlockSpec(blk, lambda hp, i: (cur(hp, i), 0)),
        pl.BlockSpec(blk, lambda hp, i: (cur(hp, i), 0)),
    ]
    args = [q_all, kv_all, kv_all, kv_all, kv_all, do, o, lse, cos2, ss2]
    if dep is not None:
        in_specs.append(ANY)
        args.append(dep)
    aliases = {}
    if prev is not None:
        in_specs += [ANY, ANY, ANY]
        aliases = {len(args): 0, len(args) + 1: 1, len(args) + 2: 2}
        args += list(prev)
    wide = jax.ShapeDtypeStruct((T, NB * Dm), BF)
    out = pl.BlockSpec(blk, lambda hp, i: (cur(hp, i), g * HP + hp))
    return pl.pallas_call(
        body, name=name,
        grid=(HP, nblk),
        in_specs=in_specs,
        out_specs=[out, out, out],
        out_shape=[wide, wide, wide],
        scratch_shapes=[pltpu.VMEM(blk, F32), pltpu.VMEM((pr + R, LANES), F32), pltpu.VMEM((pr + R, LANES), F32),
                        pltpu.VMEM(blk, F32), pltpu.VMEM(blk, F32), pltpu.VMEM(blk, F32),
                        pltpu.VMEM((d, BAND, LANES), F32), pltpu.VMEM((d, BAND, LANES), F32),
                        pltpu.VMEM((2, 2 * BAND, 2 * BAND), F32)],
        input_output_aliases=aliases,
        compiler_params=_cp("arbitrary", "arbitrary"),
    )(*args)


def _postnorm_bwd(dh, y, g_ref_val):
    r = lax.rsqrt(jnp.mean(y * y, axis=-1, keepdims=True) + RMS_EPS)
    yn = y * r
    dyn = dh * g_ref_val
    dy = r * (dyn - yn * jnp.mean(dyn * yn, axis=-1, keepdims=True))
    return dy, yn


def _after(body, n_in, dep):
    if dep is None:
        return body
    return lambda *refs: body(*refs[:n_in], *refs[n_in + 1:])


def _dep_spec(dep):
    return [] if dep is None else [ANY]


def _dep_arg(dep):
    return [] if dep is None else [dep]


def _postnorm_bwd_matmul(dh, y, gain, w3, widx, *, name, da_dtype, dep=None):
    T, D = dh.shape
    K = w3.shape[1]
    tm = min(ROW_BLOCK_WIDE, T)

    def body(dh_ref, y_ref, g_ref, w_ref, dy_ref, da_ref, dg_ref):
        i = pl.program_id(0)

        @pl.when(i == 0)
        def _():
            dg_ref[...] = jnp.zeros_like(dg_ref)

        dhv = dh_ref[...]
        dy, yn = _postnorm_bwd(dhv, y_ref[...].astype(F32), g_ref[...])
        dg_ref[...] += jnp.sum(dhv * yn, axis=0, keepdims=True)
        dyb = dy.astype(BF)
        dy_ref[...] = dyb
        da = lax.dot_general(dyb, w_ref[...], (((1,), (1,)), ((), ())), preferred_element_type=F32)
        da_ref[...] = da.astype(da_dtype)

    return pl.pallas_call(
        _after(body, 4, dep), name=name,
        grid=(T // tm,),
        in_specs=[
            pl.BlockSpec((tm, D), lambda i: (i, 0)),
            pl.BlockSpec((tm, D), lambda i: (i, 0)),
            pl.BlockSpec((1, D), lambda i: (0, 0)),
            pl.BlockSpec((None, K, D), lambda i: (widx, 0, 0)),
        ] + _dep_spec(dep),
        out_specs=[pl.BlockSpec((tm, D), lambda i: (i, 0)),
                   pl.BlockSpec((tm, K), lambda i: (i, 0)),
                   pl.BlockSpec((1, D), lambda i: (0, 0))],
        out_shape=[jax.ShapeDtypeStruct((T, D), BF), jax.ShapeDtypeStruct((T, K), da_dtype),
                   jax.ShapeDtypeStruct((1, D), F32)],
        compiler_params=_cp("arbitrary"),
    )(dh, y, gain, w3, *_dep_arg(dep))


def _postnorm_bwd_swiglu(dh, y, gain, wd3, layer, g, u, *, name, dep=None):
    T, D = dh.shape
    F = wd3.shape[1]
    nf = F // 2
    tm = min(ROW_BLOCK, T)

    def body(dh_ref, y_ref, g_ref, w_ref, gg_ref, uu_ref, dy_ref, dgo_ref, duo_ref, dgain_ref, dys):
        i = pl.program_id(0)
        j = pl.program_id(1)

        @pl.when((i == 0) & (j == 0))
        def _():
            dgain_ref[...] = jnp.zeros_like(dgain_ref)

        @pl.when(j == 0)
        def _():
            dhv = dh_ref[...]
            dy, yn = _postnorm_bwd(dhv, y_ref[...].astype(F32), g_ref[...])
            dgain_ref[...] += jnp.sum(dhv * yn, axis=0, keepdims=True)
            dyb = dy.astype(BF)
            dys[...] = dyb
            dy_ref[...] = dyb

        for c0, cw in _col_chunks(nf):
            cols = slice(c0, c0 + cw)
            da = lax.dot_general(dys[...], w_ref[cols, :], (((1,), (1,)), ((), ())), preferred_element_type=F32)
            gv = gg_ref[:, cols].astype(F32)
            uv = uu_ref[:, cols].astype(F32)
            sg = _sigmoid(gv)
            silu = gv * sg
            dgo_ref[:, cols] = (da * uv * (sg + silu * (1.0 - sg))).astype(BF)
            duo_ref[:, cols] = (da * silu).astype(BF)

    act = jax.ShapeDtypeStruct((T, F), BF)
    return pl.pallas_call(
        _after(body, 6, dep), name=name,
        grid=(T // tm, 2),
        in_specs=[
            pl.BlockSpec((tm, D), lambda i, j: (i, 0)),
            pl.BlockSpec((tm, D), lambda i, j: (i, 0)),
            pl.BlockSpec((1, D), lambda i, j: (0, 0)),
            pl.BlockSpec((None, nf, D), lambda i, j: (layer, j, 0)),
            pl.BlockSpec((tm, nf), lambda i, j: (i, j)),
            pl.BlockSpec((tm, nf), lambda i, j: (i, j)),
        ] + _dep_spec(dep),
        out_specs=[pl.BlockSpec((tm, D), lambda i, j: (i, 0)),
                   pl.BlockSpec((tm, nf), lambda i, j: (i, j)),
                   pl.BlockSpec((tm, nf), lambda i, j: (i, j)),
                   pl.BlockSpec((1, D), lambda i, j: (0, 0))],
        out_shape=[jax.ShapeDtypeStruct((T, D), BF), act, act, jax.ShapeDtypeStruct((1, D), F32)],
        scratch_shapes=[pltpu.VMEM((tm, D), BF)],
        compiler_params=_cp("arbitrary", "arbitrary"),
    )(dh, y, gain, wd3, g, u, *_dep_arg(dep))


def _matmul_prenorm_bwd(dzs, wg, layer, h, gain, dh_in, *, name):
    T, D = h.shape
    n = wg.shape[2]
    pair = len(dzs) == 2
    tm = min(ROW_BLOCK if pair else ROW_BLOCK_WIDE, T)
    nj = N_CHIPS // 2 if pair else N_CHIPS

    def body(*refs):
        dz_refs = refs[:len(dzs)]
        w_refs = refs[len(dzs):2 * len(dzs)]
        h_ref, g_ref, dhi_ref, dh_ref, dg_ref, acc = refs[2 * len(dzs):]
        i = pl.program_id(0)
        j = pl.program_id(1)

        @pl.when((i == 0) & (j == 0))
        def _():
            dg_ref[...] = jnp.zeros_like(dg_ref)

        part = None
        for dz_ref, w_ref in zip(dz_refs, w_refs):
            t = lax.dot_general(dz_ref[...].astype(BF), w_ref[...], (((1,), (1,)), ((), ())),
                                preferred_element_type=F32)
            part = t if part is None else part + t

        @pl.when(j == 0)
        def _():
            acc[...] = part

        @pl.when(j > 0)
        def _():
            acc[...] += part

        @pl.when(j == nj - 1)
        def _():
            dhn = acc[...]
            hv = h_ref[...]
            r = lax.rsqrt(jnp.mean(hv * hv, axis=-1, keepdims=True) + RMS_EPS)
            xh = hv * r
            dg_ref[...] += jnp.sum(dhn * xh, axis=0, keepdims=True)
            dxn = dhn * g_ref[...]
            dh_ref[...] = dhi_ref[...] + r * (dxn - xh * jnp.mean(dxn * xh, axis=-1, keepdims=True))

    in_specs = [pl.BlockSpec((tm, n), lambda i, j: (i, j)) for _ in dzs]
    if pair:
        in_specs += [pl.BlockSpec((None, D, n), lambda i, j: (j, layer, 0)),
                     pl.BlockSpec((None, D, n), lambda i, j: (j + nj, layer, 0))]
    else:
        in_specs += [pl.BlockSpec((None, D, n), lambda i, j: (j, layer, 0))]
    in_specs += [pl.BlockSpec((tm, D), lambda i, j: (i, 0)),
                 pl.BlockSpec((1, D), lambda i, j: (0, 0)),
                 pl.BlockSpec((tm, D), lambda i, j: (i, 0))]
    return pl.pallas_call(
        body, name=name,
        grid=(T // tm, nj),
        in_specs=in_specs,
        out_specs=[pl.BlockSpec((tm, D), lambda i, j: (i, 0)), pl.BlockSpec((1, D), lambda i, j: (0, 0))],
        out_shape=[jax.ShapeDtypeStruct((T, D), F32), jax.ShapeDtypeStruct((1, D), F32)],
        scratch_shapes=[pltpu.VMEM((tm, D), F32)],
        compiler_params=_cp("arbitrary", "arbitrary"),
    )(*dzs, *([wg] * len(dzs)), h, gain, dh_in)


def _grad_matmul(a, b, out_shape3, tme, tne, out_index, prev, *, name):
    T, M = a.shape
    N = b.shape[1]
    tk = min(ROW_BLOCK_WIDE, T)
    nk = T // tk

    def body(a_ref, b_ref, *rest):
        o_ref, acc = rest[-2:]
        k = pl.program_id(2)
        part = jnp.dot(a_ref[...].astype(BF).T, b_ref[...].astype(BF), preferred_element_type=F32)

        @pl.when(k == 0)
        def _():
            acc[...] = part

        @pl.when(k > 0)
        def _():
            acc[...] += part

        @pl.when(k == nk - 1)
        def _():
            o_ref[...] = acc[...].astype(BF)

    in_specs = [pl.BlockSpec((tk, tme), lambda i, j, k: (k, i)),
                pl.BlockSpec((tk, tne), lambda i, j, k: (k, j))]
    args = [a, b]
    aliases = {}
    if prev is not None:
        in_specs.append(ANY)
        args.append(prev)
        aliases = {2: 0}
    return pl.pallas_call(
        body, name=name,
        grid=(M // tme, N // tne, nk),
        in_specs=in_specs,
        out_specs=pl.BlockSpec((None, tme, tne), lambda i, j, k: out_index(i, j)),
        out_shape=jax.ShapeDtypeStruct(out_shape3, BF),
        scratch_shapes=[pltpu.VMEM((tme, tne), F32)],
        input_output_aliases=aliases,
        compiler_params=_cp("parallel", "parallel", "arbitrary"),
    )(*args)


def _row_tile(R, cap=512):
    fit = [t for t in range(16, min(R, cap) + 1, 16) if R % t == 0]
    return max(fit) if fit else R


def _cast_place(w3, layer, where, dtype, *, name, dep=None):
    _, R, C = w3.shape
    tr = _row_tile(R)

    def body(s_ref, w_ref, o_ref):
        o_ref[...] = w_ref[...].astype(o_ref.dtype)

    return pl.pallas_call(
        _after(body, 2, dep), name=name,
        grid_spec=pltpu.PrefetchScalarGridSpec(
            num_scalar_prefetch=1, grid=(R // tr,),
            in_specs=[pl.BlockSpec((None, tr, C), lambda i, s: (layer, i, 0))] + _dep_spec(dep),
            out_specs=pl.BlockSpec((None, tr, C), lambda i, s: (s[0], i, 0))),
        out_shape=jax.ShapeDtypeStruct((N_CHIPS, R, C), dtype),
        compiler_params=_cp("arbitrary"),
    )(where, w3, *_dep_arg(dep))


def _pair_sum(dw, theirs, where, *, name):
    G, rh, C = theirs.shape
    tr = _row_tile(rh)
    nr = rh // tr

    def body(s_ref, a_ref, b_ref, o_ref):
        o_ref[...] = (a_ref[...].astype(F32) + b_ref[...].astype(F32)).astype(BF)

    mine = pl.BlockSpec((None, tr, C), lambda g, i, s: (g, s[1] * nr + i, 0))
    spec = pl.BlockSpec((None, tr, C), lambda g, i, s: (g, i, 0))
    return pl.pallas_call(
        body, name=name,
        grid_spec=pltpu.PrefetchScalarGridSpec(
            num_scalar_prefetch=1, grid=(G, nr), in_specs=[mine, spec], out_specs=spec),
        out_shape=jax.ShapeDtypeStruct((G, rh, C), BF),
        compiler_params=_cp("arbitrary", "arbitrary"),
    )(where, dw, theirs)


def _chip_sum(landed, parts, where, total_rows, row_off, prev, *, name):
    G, rh, C = landed.shape
    tr = _row_tile(rh)
    nr = rh // tr
    base = row_off // tr

    def body(s_ref, l_ref, p_ref, *rest):
        o_ref = rest[-1]
        for j in range(G):
            def own(j=j):
                v = p_ref[...].astype(F32)
                o_ref[...] = v if j == 0 else o_ref[...] + v

            def other(j=j):
                v = l_ref[j].astype(F32)
                o_ref[...] = v if j == 0 else o_ref[...] + v

            pl.when(s_ref[0] == j)(own)
            pl.when(s_ref[0] != j)(other)

    in_specs = [pl.BlockSpec((G, tr, C), lambda i, s: (0, i, 0)),
                pl.BlockSpec((None, tr, C), lambda i, s: (s[0], i, 0))]
    args = [where, landed, parts]
    aliases = {}
    if prev is not None:
        in_specs.append(ANY)
        args.append(prev)
        aliases = {3: 0}
    return pl.pallas_call(
        body, name=name,
        grid_spec=pltpu.PrefetchScalarGridSpec(
            num_scalar_prefetch=1, grid=(nr,),
            in_specs=in_specs,
            out_specs=pl.BlockSpec((tr, C), lambda i, s: (base + s[1] * nr + i, 0))),
        out_shape=jax.ShapeDtypeStruct((total_rows, C), F32),
        input_output_aliases=aliases,
        compiler_params=_cp("arbitrary"),
    )(*args)


def _adamw(w, g, m, v, *, name, emit_grad=False):
    R, C = w.shape
    tr = _row_tile(R, cap=max(16, ADAMW_BLOCK_BYTES // (4 * C)))
    n_out = 4 if emit_grad else 3

    def body(w_ref, g_ref, m_ref, v_ref, d_ref, mo_ref, vo_ref, *go_ref):
        gv = g_ref[...]
        if emit_grad:
            go_ref[0][...] = gv
        mn = ADAM_B1 * m_ref[...] + (1.0 - ADAM_B1) * gv
        vn = ADAM_B2 * v_ref[...] + (1.0 - ADAM_B2) * jnp.square(gv)
        m_hat = mn / (1.0 - ADAM_B1 ** ADAM_STEP)
        v_hat = vn / (1.0 - ADAM_B2 ** ADAM_STEP)
        d_ref[...] = -ADAM_LR * (m_hat / (jnp.sqrt(v_hat) + ADAM_EPS) + ADAM_WD * w_ref[...])
        mo_ref[...] = mn
        vo_ref[...] = vn

    spec = pl.BlockSpec((tr, C), lambda i: (i, 0))
    shp = jax.ShapeDtypeStruct((R, C), F32)
    return pl.pallas_call(
        body, name=name, grid=(R // tr,), in_specs=[spec] * 4, out_specs=[spec] * n_out,
        out_shape=[shp] * n_out, compiler_params=_cp("parallel"),
    )(w, g, m, v)


def _place():
    x = lax.axis_index("x")
    y = lax.axis_index("y")
    c = lax.axis_index("c")
    chips = [(1 - x, y), (x, 1 - y), (1 - x, 1 - y)]
    return x, y, c, chips


def _chunk_rows(rows, row_bytes, align):
    if rows <= align:
        return rows
    cands = [r for r in range(align, rows + 1, align) if rows % r == 0]
    fit = [r for r in cands if r * row_bytes <= DMA_CHUNK_BYTES]
    return max(fit) if fit else min(cands)


def _row_align(dtype):
    return 8 * (4 // jnp.dtype(dtype).itemsize)


def _start_chunks(make, rows, rc):
    for r0 in range(0, rows, rc):
        make(r0, rc).start()


def _piece_rows(ref, piece, j, h, r0=0, n=None):
    _, lead, off, rows = piece
    rh = rows // 2
    n = rh if n is None else n
    if lead is not None:
        return ref.at[lead, j, pl.ds(h * rh + r0, n)]
    return ref.at[j, pl.ds(off + h * rh + r0, n)]


def _piece_chunk(arr, piece):
    rh = piece[3] // 2
    return rh, _chunk_rows(rh, arr.shape[-1] * arr.dtype.itemsize, _row_align(arr.dtype))


def _ag_start(arrays, taps, groups, *, name):
    na = len(arrays)
    ng = len(groups)
    nt = 0 if taps is None else 1
    n_sem = [3 * len(grp) + (3 if nt and g == 0 else 0) for g, grp in enumerate(groups)]

    def body(*refs):
        ins = refs[:na]
        taps_ref = refs[na] if nt else None
        sems = refs[na + nt:na + nt + 2 * ng]
        token = refs[-1]
        token[...] = jnp.zeros_like(token)
        x, y, c, chips = _place()
        myj = 2 * x + y
        for g, grp in enumerate(groups):
            ssem, rsem = sems[2 * g], sems[2 * g + 1]
            for idx, piece in enumerate(grp):
                ref = ins[piece[0]]
                rh, rc = _piece_chunk(arrays[piece[0]], piece)
                for k, (px, py) in enumerate(chips):
                    def send(r0, n, ref=ref, piece=piece, idx=idx, k=k, px=px, py=py, ssem=ssem, rsem=rsem):
                        part = _piece_rows(ref, piece, myj, c, r0, n)
                        return pltpu.make_async_remote_copy(
                            src_ref=part, dst_ref=part, send_sem=ssem.at[3 * idx + k], recv_sem=rsem.at[3 * idx + k],
                            device_id=(px, py, c), device_id_type=MESH)
                    _start_chunks(send, rh, rc)
            if nt and g == 0:
                for k, (px, py) in enumerate(chips):
                    pltpu.make_async_remote_copy(
                        src_ref=taps_ref.at[myj], dst_ref=taps_ref.at[myj],
                        send_sem=ssem.at[3 * len(grp) + k], recv_sem=rsem.at[3 * len(grp) + k],
                        device_id=(px, py, c), device_id_type=MESH).start()

    sem_shapes = []
    for n in n_sem:
        sem_shapes += [pltpu.SemaphoreType.DMA((n,)), pltpu.SemaphoreType.DMA((n,))]
    ops = list(arrays) + ([taps] if nt else [])
    bufs = [pltpu.HBM(a.shape, a.dtype) for a in ops]
    outs = pl.pallas_call(
        body, name=name,
        out_shape=(*sem_shapes, *bufs, jax.ShapeDtypeStruct((8, LANES), F32)),
        in_specs=[HBM] * (na + nt),
        out_specs=(*([SEM] * (2 * ng)), *([HBM] * (na + nt)), pl.BlockSpec(memory_space=pltpu.VMEM)),
        input_output_aliases={i: 2 * ng + i for i in range(na + nt)},
        compiler_params=pltpu.CompilerParams(has_side_effects=EFFECT),
    )(*[_in_hbm(a) for a in ops])
    sems = [(outs[2 * g], outs[2 * g + 1]) for g in range(ng)]
    return sems, list(outs[2 * ng:2 * ng + na]), (outs[2 * ng + na] if nt else None), outs[-1]


def _ag_wait(sems, vals, taps, group, after, *, name):
    nv = len(vals)
    extra = ([taps] if taps is not None else [])
    nb = nv + len(extra)

    def body(*refs):
        bufs = refs[:nb]
        ssem, rsem = refs[nb], refs[nb + 1]
        x, y, c, chips = _place()
        for idx, piece in enumerate(group):
            for k, (px, py) in enumerate(chips):
                got = _piece_rows(bufs[piece[0]], piece, 2 * px + py, c)
                cp = pltpu.make_async_remote_copy(
                    src_ref=got, dst_ref=got, send_sem=ssem.at[3 * idx + k], recv_sem=rsem.at[3 * idx + k],
                    device_id=(px, py, c), device_id_type=MESH)
                cp.wait_send()
                cp.wait_recv()
        if taps is not None:
            for k, (px, py) in enumerate(chips):
                got = bufs[nv].at[2 * px + py]
                cp = pltpu.make_async_remote_copy(
                    src_ref=got, dst_ref=got, send_sem=ssem.at[3 * len(group) + k],
                    recv_sem=rsem.at[3 * len(group) + k], device_id=(px, py, c), device_id_type=MESH)
                cp.wait_send()
                cp.wait_recv()

    ops = list(vals) + extra
    shapes = [pltpu.HBM(a.shape, a.dtype) for a in ops]
    outs = pl.pallas_call(
        body, name=name,
        out_shape=tuple(shapes),
        in_specs=[HBM] * nb + [SEM, SEM] + _dep_spec(after),
        out_specs=[HBM] * nb,
        input_output_aliases={i: i for i in range(nb)},
        compiler_params=pltpu.CompilerParams(has_side_effects=EFFECT),
    )(*ops, sems[0], sems[1], *_dep_arg(after))
    return list(outs[:nv]), (outs[nv] if taps is not None else None)


def _ag_forward(vals, group, *, name):
    nv = len(vals)
    npc = len(group)

    def body(*refs):
        bufs = refs[nv:2 * nv]
        fsem, gsem = refs[2 * nv:]
        x, y, c, chips = _place()
        sib = (x, y, 1 - c)
        sent = []
        for idx, piece in enumerate(group):
            rh, rc = _piece_chunk(vals[piece[0]], piece)
            for k, (px, py) in enumerate(chips):
                def fwd(r0, n, piece=piece, idx=idx, k=k, pj=2 * px + py):
                    part = _piece_rows(bufs[piece[0]], piece, pj, c, r0, n)
                    return pltpu.make_async_remote_copy(
                        src_ref=part, dst_ref=part, send_sem=fsem.at[3 * idx + k], recv_sem=gsem.at[3 * idx + k],
                        device_id=sib, device_id_type=MESH)
                _start_chunks(fwd, rh, rc)
                sent.append(fwd(0, rh))
        for idx, piece in enumerate(group):
            for k, (px, py) in enumerate(chips):
                theirs = _piece_rows(bufs[piece[0]], piece, 2 * px + py, 1 - c)
                pltpu.make_async_remote_copy(
                    src_ref=theirs, dst_ref=theirs, send_sem=fsem.at[3 * idx + k], recv_sem=gsem.at[3 * idx + k],
                    device_id=sib, device_id_type=MESH).wait_recv()
        for cp in sent:
            cp.wait_send()

    return pl.pallas_call(
        body, name=name,
        in_specs=[ANY] * nv, out_specs=[ANY] * nv,
        out_shape=[jax.ShapeDtypeStruct(v.shape, v.dtype) for v in vals],
        input_output_aliases={i: i for i in range(nv)},
        scratch_shapes=[pltpu.SemaphoreType.DMA((3 * npc,)), pltpu.SemaphoreType.DMA((3 * npc,))],
    )(*vals)


def _sibling_swap(dws, *, name):
    nm = len(dws)
    shapes = [jax.ShapeDtypeStruct((dw.shape[0], dw.shape[1] // 2, dw.shape[2]), dw.dtype) for dw in dws]

    def body(*refs):
        ins = refs[:nm]
        theirs = refs[nm:2 * nm]
        ssem, rsem = refs[2 * nm:]
        x, y, c, _ = _place()
        sib = (x, y, 1 - c)
        cps = []
        for m in range(nm):
            G, rh, cols = shapes[m].shape
            rc = _chunk_rows(rh, cols * shapes[m].dtype.itemsize, _row_align(shapes[m].dtype))
            for j in range(G):
                _start_chunks(lambda r0, n, m=m, j=j, rh=rh: pltpu.make_async_remote_copy(
                    src_ref=ins[m].at[j, pl.ds((1 - c) * rh + r0, n)],
                    dst_ref=theirs[m].at[j, pl.ds(r0, n)], send_sem=ssem.at[m], recv_sem=rsem.at[m],
                    device_id=sib, device_id_type=MESH), rh, rc)
            cps.append(pltpu.make_async_remote_copy(
                src_ref=ins[m].at[:, pl.ds((1 - c) * rh, rh), :], dst_ref=theirs[m],
                send_sem=ssem.at[m], recv_sem=rsem.at[m], device_id=sib, device_id_type=MESH))
        for cp in cps:
            cp.wait()

    return pl.pallas_call(
        body, name=name,
        in_specs=[ANY] * nm, out_specs=[ANY] * nm, out_shape=shapes,
        scratch_shapes=[pltpu.SemaphoreType.DMA((nm,)), pltpu.SemaphoreType.DMA((nm,))],
    )(*dws)


HBM = pl.BlockSpec(memory_space=pltpu.HBM)
SEM = pl.BlockSpec(memory_space=pltpu.SEMAPHORE)
EFFECT = pltpu.SideEffectType.DATAFLOW_SIDE_EFFECTING


def _in_hbm(a):
    return pltpu.with_memory_space_constraint(a, pltpu.HBM)


def _exchange_start(parts, *, name):
    nm = len(parts)

    def body(*refs):
        ins = refs[:nm]
        lands = refs[nm:2 * nm]
        ssem, rsem = refs[2 * nm:2 * nm + 2]
        token = refs[-1]
        x, y, c, chips = _place()
        myj = 2 * x + y
        for m in range(nm):
            _, rh, cols = parts[m].shape
            rc = _chunk_rows(rh, cols * parts[m].dtype.itemsize, _row_align(parts[m].dtype))
            for k, (px, py) in enumerate(chips):
                _start_chunks(lambda r0, n, m=m, k=k, px=px, py=py: pltpu.make_async_remote_copy(
                    src_ref=ins[m].at[2 * px + py, pl.ds(r0, n)], dst_ref=lands[m].at[myj, pl.ds(r0, n)],
                    send_sem=ssem.at[3 * m + k], recv_sem=rsem.at[3 * m + k],
                    device_id=(px, py, c), device_id_type=MESH), rh, rc)
        token[...] = jnp.zeros_like(token)

    bufs = [pltpu.HBM(p.shape, p.dtype) for p in parts]
    outs = pl.pallas_call(
        body, name=name,
        out_shape=(pltpu.SemaphoreType.DMA((3 * nm,)), pltpu.SemaphoreType.DMA((3 * nm,)), *bufs, *bufs,
                   jax.ShapeDtypeStruct((8, LANES), F32)),
        in_specs=[HBM] * (2 * nm),
        out_specs=(SEM, SEM, *([HBM] * (2 * nm)), pl.BlockSpec(memory_space=pltpu.VMEM)),
        input_output_aliases={i: 2 + i for i in range(2 * nm)},
        compiler_params=pltpu.CompilerParams(has_side_effects=EFFECT),
    )(*[_in_hbm(p) for p in parts], *[_in_hbm(lax.empty(p.shape, p.dtype)) for p in parts])
    return (outs[0], outs[1], list(outs[2:2 + nm]), list(outs[2 + nm:2 + 2 * nm])), outs[-1]


def _exchange_wait(handle, after, *, name):
    ssem_in, rsem_in, parts, lands = handle
    nm = len(parts)

    def body(*refs):
        ins = refs[:nm]
        lnd = refs[nm:2 * nm]
        ssem, rsem = refs[2 * nm:2 * nm + 2]
        x, y, c, chips = _place()
        for m in range(nm):
            for k, (px, py) in enumerate(chips):
                pj = 2 * px + py
                cp = pltpu.make_async_remote_copy(
                    src_ref=ins[m].at[pj], dst_ref=lnd[m].at[pj],
                    send_sem=ssem.at[3 * m + k], recv_sem=rsem.at[3 * m + k],
                    device_id=(px, py, c), device_id_type=MESH)
                cp.wait_send()
                cp.wait_recv()

    bufs = [pltpu.HBM(p.shape, p.dtype) for p in parts]
    outs = pl.pallas_call(
        body, name=name,
        out_shape=(*bufs, *bufs),
        in_specs=[HBM] * (2 * nm) + [SEM, SEM, ANY],
        out_specs=[HBM] * (2 * nm),
        input_output_aliases={i: i for i in range(2 * nm)},
        compiler_params=pltpu.CompilerParams(has_side_effects=EFFECT),
    )(*parts, *lands, ssem_in, rsem_in, after)
    return list(outs[nm:]), list(outs[:nm])


def _sibling_join(grads, regions, *, name):
    nm = len(grads)
    nr = len(regions)
    shapes = [jax.ShapeDtypeStruct(g.shape, g.dtype) for g in grads]

    def body(*refs):
        outs = refs[nm:2 * nm]
        ssem, rsem = refs[2 * nm:]
        x, y, c, _ = _place()
        sib = (x, y, 1 - c)
        cps = []
        for i, (m, off, rows) in enumerate(regions):
            rh, cols = rows // 2, grads[m].shape[1]
            rc = _chunk_rows(rh, cols * grads[m].dtype.itemsize, _row_align(grads[m].dtype))

            def send(r0, n, i=i, m=m, off=off, rh=rh):
                part = outs[m].at[pl.ds(off + c * rh + r0, n)]
                return pltpu.make_async_remote_copy(
                    src_ref=part, dst_ref=part, send_sem=ssem.at[i], recv_sem=rsem.at[i],
                    device_id=sib, device_id_type=MESH)
            _start_chunks(send, rh, rc)
            cps.append(send(0, rh))
        for i, (m, off, rows) in enumerate(regions):
            rh = rows // 2
            cps[i].wait_send()
            got = outs[m].at[pl.ds(off + (1 - c) * rh, rh)]
            pltpu.make_async_remote_copy(
                src_ref=got, dst_ref=got, send_sem=ssem.at[i], recv_sem=rsem.at[i],
                device_id=sib, device_id_type=MESH).wait_recv()

    return pl.pallas_call(
        body, name=name,
        in_specs=[ANY] * nm, out_specs=[ANY] * nm, out_shape=shapes,
        input_output_aliases={i: i for i in range(nm)},
        scratch_shapes=[pltpu.SemaphoreType.DMA((nr,)), pltpu.SemaphoreType.DMA((nr,))],
    )(*grads)


def _all_reduce_small(pack):
    R, C = pack.shape

    def body(in_ref, out_ref, slots, ssem, rsem):
        x, y, c, _ = _place()
        me = 4 * x + 2 * y + c
        slots[me] = in_ref[...]
        cps = []
        for k in range(1, N_DEV):
            dx, dy, dc = (k >> 2) & 1, (k >> 1) & 1, k & 1
            peer = (x ^ dx, y ^ dy, c ^ dc)
            cp = pltpu.make_async_remote_copy(
                src_ref=in_ref, dst_ref=slots.at[me], send_sem=ssem.at[k], recv_sem=rsem.at[k],
                device_id=peer, device_id_type=MESH)
            cp.start()
            cps.append(cp)
        for k in range(1, N_DEV):
            dx, dy, dc = (k >> 2) & 1, (k >> 1) & 1, k & 1
            got = slots.at[4 * (x ^ dx) + 2 * (y ^ dy) + (c ^ dc)]
            pltpu.make_async_remote_copy(
                src_ref=got, dst_ref=got, send_sem=ssem.at[k], recv_sem=rsem.at[k],
                device_id=(x ^ dx, y ^ dy, c ^ dc), device_id_type=MESH).wait_recv()
        for cp in cps:
            cp.wait_send()
        acc = slots[0]
        for s in range(1, N_DEV):
            acc = acc + slots[s]
        out_ref[...] = acc

    return pl.pallas_call(
        body, name="ar_small",
        in_specs=[pl.BlockSpec(memory_space=pltpu.VMEM)],
        out_specs=pl.BlockSpec(memory_space=pltpu.VMEM),
        out_shape=jax.ShapeDtypeStruct((R, C), F32),
        scratch_shapes=[pltpu.VMEM((N_DEV, R, C), F32),
                        pltpu.SemaphoreType.DMA((N_DEV,)), pltpu.SemaphoreType.DMA((N_DEV,))],
    )(pack)


def kernel(x, positions, mix_norm_pre, mix_norm_post, ffn_norm_pre, ffn_norm_post, ffn_w_gate_up, ffn_w_down, conv_w_in, conv_w, conv_w_out, kv_norm, w_kv, w_q, w_o, loss_target, m_mix_norm_pre, m_mix_norm_post, m_ffn_norm_pre, m_ffn_norm_post, m_ffn_w_gate_up, m_ffn_w_down, m_conv_w_in, m_conv_w, m_conv_w_out, m_kv_norm, m_w_kv, m_w_q, m_w_o, v_mix_norm_pre, v_mix_norm_post, v_ffn_norm_pre, v_ffn_norm_post, v_ffn_w_gate_up, v_ffn_w_down, v_conv_w_in, v_conv_w, v_conv_w_out, v_kv_norm, v_w_kv, v_w_q, v_w_o):
    T, D = x.shape[1], x.shape[2]
    L = ffn_w_gate_up.shape[0]
    n_gu = ffn_w_gate_up.shape[2]
    f_sh = ffn_w_down.shape[1]
    F = N_CHIPS * f_sh
    x0 = x[0]
    tgt = loss_target[0]

    half = HEAD_DIM // 2
    inv_freq = ROPE_THETA ** (-jnp.arange(half, dtype=F32) / half)
    ang = positions[0].astype(F32)[:, None] * inv_freq
    cosv, sinv = jnp.cos(ang), jnp.sin(ang)
    cos2 = jnp.tile(cosv, (1, LANES // half))
    ss2 = jnp.tile(jnp.concatenate([-sinv, sinv], axis=1), (1, LANES // HEAD_DIM))

    def as2d(a):
        return a.reshape(-1, a.shape[-1])

    big = [ffn_w_gate_up, ffn_w_down, conv_w_in, conv_w_out, w_kv, w_q, w_o]
    big_m = [m_ffn_w_gate_up, m_ffn_w_down, m_conv_w_in, m_conv_w_out, m_w_kv, m_w_q, m_w_o]
    big_v = [v_ffn_w_gate_up, v_ffn_w_down, v_conv_w_in, v_conv_w_out, v_w_kv, v_w_q, v_w_o]
    chip = 2 * lax.axis_index("x") + lax.axis_index("y")
    where = jnp.stack([chip, lax.axis_index("c")]).astype(jnp.int32)
    tc = conv_w.shape[2]
    cw_pad = jnp.concatenate([conv_w[0], jnp.zeros((8 - conv_w.shape[1], tc), F32)], axis=0)

    GU0, GU1, WD0, WD1, WCI, WCO, WKV, WQ, WO = range(9)
    shards = [(ffn_w_gate_up, 0), (ffn_w_gate_up, 1), (ffn_w_down, 0), (ffn_w_down, 1), (conv_w_in, 0),
              (conv_w_out, 0), (w_kv[None], 0), (w_q, 0), (w_o, 0)]
    ag_groups = [
        [(WCI, None, 0, D), (WCO, None, 0, D // N_CHIPS)],
        [(GU0, None, 0, D), (WD0, None, 0, f_sh)],
        [(WKV, None, 0, D), (WQ, None, 0, D)],
        [(WO, None, 0, D // N_CHIPS), (GU1, None, 0, D), (WD1, None, 0, f_sh)],
    ]

    def localised(group, idxs):
        return [(idxs.index(p[0]),) + p[1:] for p in group]

    cur = [None] * len(shards)
    first = [WCI, WCO]
    sems0, vals, taps, token = _ag_start(
        [_cast_place(*shards[i], where, BF, name=f"place{i}") for i in first],
        _cast_place(cw_pad[None], 0, where, F32, name="place_taps"), [localised(ag_groups[0], first)],
        name="ag_start0")
    for i, v in zip(first, vals):
        cur[i] = v
    rest = [i for i in range(len(shards)) if i not in first]
    sems1, vals, _, _ = _ag_start(
        [_cast_place(*shards[i], where, BF, name=f"place{i}", dep=token) for i in rest], None,
        [localised(g, rest) for g in ag_groups[1:]], name="ag_start1")
    for i, v in zip(rest, vals):
        cur[i] = v
    ag_sems = sems0 + sems1

    def gather_group(g, after):
        nonlocal taps
        idxs = sorted({p[0] for p in ag_groups[g]})
        local = localised(ag_groups[g], idxs)
        vals, landed_taps = _ag_wait(ag_sems[g], [cur[i] for i in idxs], taps if g == 0 else None, local, after,
                                     name=f"ag_wait{g}")
        if g == 0:
            taps = landed_taps
        vals = _ag_forward(vals, local, name=f"ag_forward{g}")
        for i, v in zip(idxs, vals):
            cur[i] = v

    def row(a, i):
        return a[i:i + 1]

    gather_group(0, None)
    wci, wco, cw = cur[WCI], cur[WCO].reshape(1, D, D), taps
    z, hn_m0 = _norm_matmul(x0, row(mix_norm_pre, 0), wci, cos2, ss2, name="f0_conv_in",
                            rope_shards=0, scale=1.0, out_dtype=BF)
    vmix = _conv_fwd(z, cw, name="f0_conv")
    y0, h1 = _matmul_postnorm(vmix, wco, 0, row(mix_norm_post, 0), x0, name="f0_conv_out")
    gather_group(1, h1)
    wgu0, wd0 = cur[GU0], cur[WD0].reshape(1, F, D)
    g0, u0, a0, hn_f0 = _norm_swiglu(h1, row(ffn_norm_pre, 0), wgu0, 0, name="f0_gate_up")
    f0, h2 = _matmul_postnorm(a0, wd0, 0, row(ffn_norm_post, 0), h1, name="f0_down")

    gather_group(2, h2)
    wkv, wq = cur[WKV], cur[WQ]
    kv_all, hn_kv = _norm_matmul(h2, kv_norm.reshape(1, D), wkv, cos2, ss2, name="f1_kv",
                                 rope_shards=N_CHIPS // 2, scale=1.0, out_dtype=BF)
    q_all, hn_m1 = _norm_matmul(h2, row(mix_norm_pre, 1), wq, cos2, ss2, name="f1_q",
                                rope_shards=N_CHIPS, scale=HEAD_DIM ** -0.5, out_dtype=BF)
    o_att, lse = _attn_fwd(q_all, kv_all, name="f1_attn")
    gather_group(3, o_att)
    wgu1, wd1, wo = cur[GU1], cur[WD1].reshape(1, F, D), cur[WO].reshape(1, D, D)
    y1, h3 = _matmul_postnorm(o_att, wo, 0, row(mix_norm_post, 1), h2, name="f1_attn_out")
    g1, u1, a1, hn_f1 = _norm_swiglu(h3, row(ffn_norm_pre, 1), wgu1, 0, name="f1_gate_up")
    f1, h4 = _matmul_postnorm(a1, wd1, 0, row(ffn_norm_post, 1), h3, name="f1_down")

    dh4, sq = _loss_head(h4, tgt, name="loss_head")
    loss_part = 0.5 * sq[0, 0] / D

    gu_shape = (N_CHIPS, D, n_gu)
    in_chips = lambda a: a.reshape(N_CHIPS, -1, a.shape[-1])

    def scatter_start(dws, tag):
        theirs = _sibling_swap(dws, name=f"rs_swap_{tag}")
        parts = [_pair_sum(dw, t, where, name=f"rs_pair_sum_{tag}{i}") for i, (dw, t) in enumerate(zip(dws, theirs))]
        return _exchange_start(parts, name=f"rs_exchange_start_{tag}")

    dyf1, dg1, du1, d_ffn_post1 = _postnorm_bwd_swiglu(dh4, f1, row(ffn_norm_post, 1), wd1, 0, g1, u1,
                                                       name="b1_down")
    dwd1 = _grad_matmul(a1, dyf1, (2, F // 2, D), F // 2, D, lambda i, j: (i, 0, 0), None, name="b1_dw_down")
    dwgu1 = _grad_matmul(hn_f1, dg1, gu_shape, D, n_gu, lambda i, j: (j, 0, 0), None, name="b1_dw_gate")
    dwgu1 = _grad_matmul(hn_f1, du1, gu_shape, D, n_gu, lambda i, j: (j + 2, 0, 0), dwgu1, name="b1_dw_up")
    dh3, d_ffn_pre1 = _matmul_prenorm_bwd((dg1, du1), wgu1, 0, h3, row(ffn_norm_pre, 1), dh4, name="b1_gate_up")

    dy1, do, d_mix_post1 = _postnorm_bwd_matmul(dh3, y1, row(mix_norm_post, 1), wo, 0, name="b1_attn_out",
                                                da_dtype=F32)
    dwo = _grad_matmul(o_att, dy1, (1, D, D), D, D, lambda i, j: (0, 0, 0), None, name="b1_dw_o")
    rs_a, token = scatter_start([dwgu1, in_chips(dwd1), in_chips(dwo)], "a")
    prev = None
    for gi, (window, dil) in enumerate(BRANCHES):
        prev = _attn_bwd(q_all, kv_all, do, o_att, lse, cos2, ss2, gi, dil, prev, name=f"b1_attn{gi}",
                         dep=token if gi == 0 else None)
    dq_all, dk_all, dv_all = prev
    n_q = wq.shape[2]
    n_kv = wkv.shape[2]
    dwq = _grad_matmul(hn_m1, dq_all, (N_CHIPS, D, n_q), D, n_q, lambda i, j: (j, 0, 0), None, name="b1_dw_q")
    dwkv = _grad_matmul(hn_kv, dk_all, (N_CHIPS, D, n_kv), D, n_kv, lambda i, j: (j, 0, 0), None, name="b1_dw_k")
    dwkv = _grad_matmul(hn_kv, dv_all, (N_CHIPS, D, n_kv), D, n_kv, lambda i, j: (j + 2, 0, 0), dwkv, name="b1_dw_v")
    dh2, d_mix_pre1 = _matmul_prenorm_bwd((dq_all,), wq, 0, h2, row(mix_norm_pre, 1), dh3, name="b1_q")
    dh2, d_kv_norm = _matmul_prenorm_bwd((dk_all, dv_all), wkv, 0, h2, kv_norm.reshape(1, D), dh2, name="b1_kv")
    rs_b, token = scatter_start([dwkv, dwq], "b")

    dyf0, dg0, du0, d_ffn_post0 = _postnorm_bwd_swiglu(dh2, f0, row(ffn_norm_post, 0), wd0, 0, g0, u0,
                                                       name="b0_down", dep=token)
    dwd0 = _grad_matmul(a0, dyf0, (2, F // 2, D), F // 2, D, lambda i, j: (i, 0, 0), None, name="b0_dw_down")
    dwgu0 = _grad_matmul(hn_f0, dg0, gu_shape, D, n_gu, lambda i, j: (j, 0, 0), None, name="b0_dw_gate")
    dwgu0 = _grad_matmul(hn_f0, du0, gu_shape, D, n_gu, lambda i, j: (j + 2, 0, 0), dwgu0, name="b0_dw_up")
    dh1, d_ffn_pre0 = _matmul_prenorm_bwd((dg0, du0), wgu0, 0, h1, row(ffn_norm_pre, 0), dh2, name="b0_gate_up")
    rs_c, token = scatter_start([dwgu0, in_chips(dwd0)], "c")

    dy0, dvmix, d_mix_post0 = _postnorm_bwd_matmul(dh1, y0, row(mix_norm_post, 0), wco, 0, name="b0_conv_out",
                                                   da_dtype=BF, dep=token)
    dwco = _grad_matmul(vmix, dy0, (1, D, D), D, D, lambda i, j: (0, 0, 0), None, name="b0_dw_conv_out")
    dz, dcw = _conv_bwd(z, cw, dvmix, name="b0_conv")
    n_ci = wci.shape[2]
    dwci = _grad_matmul(hn_m0, dz, (N_CHIPS, D, n_ci), D, n_ci, lambda i, j: (j, 0, 0), None, name="b0_dw_conv_in")
    dx, d_mix_pre0 = _matmul_prenorm_bwd((dz,), wci, 0, x0, row(mix_norm_pre, 0), dh1, name="b0_conv_in")

    pack = jnp.concatenate([
        d_mix_pre0, d_mix_pre1, d_mix_post0, d_mix_post1, d_ffn_pre0, d_ffn_pre1, d_ffn_post0, d_ffn_post1,
        d_kv_norm, dcw[0:3], jnp.full((1, D), loss_part, F32),
        jnp.zeros((SMALL_ROWS - 13, D), F32)], axis=0)
    red = _all_reduce_small(pack)
    loss = red[12, 0]
    myj = 2 * lax.axis_index("x") + lax.axis_index("y")
    g_conv_w = lax.dynamic_slice(red, (9, myj * tc), (3, tc))

    zeros7 = jnp.zeros((SMALL_ROWS - 9, D), F32)
    w_small = jnp.concatenate([mix_norm_pre, mix_norm_post, ffn_norm_pre, ffn_norm_post, kv_norm.reshape(1, D), zeros7], axis=0)
    m_small = jnp.concatenate([m_mix_norm_pre, m_mix_norm_post, m_ffn_norm_pre, m_ffn_norm_post, m_kv_norm.reshape(1, D), zeros7], axis=0)
    v_small = jnp.concatenate([v_mix_norm_pre, v_mix_norm_post, v_ffn_norm_pre, v_ffn_norm_post, v_kv_norm.reshape(1, D), zeros7], axis=0)
    d_small, nm_small, nv_small = _adamw(w_small, red, m_small, v_small, name="adamw_small")

    pad5 = jnp.zeros((5, tc), F32)
    d_cw, nm_cw, nv_cw = _adamw(cw_pad, jnp.concatenate([g_conv_w, pad5], axis=0),
                                jnp.concatenate([m_conv_w[0], pad5], axis=0),
                                jnp.concatenate([v_conv_w[0], pad5], axis=0), name="adamw_conv_w")

    rs_d, _ = scatter_start([dwci, in_chips(dwco)], "d")

    pieces = {"a": [(0, D), (1, f_sh), (6, 0)], "b": [(4, 0), (5, 0)], "c": [(0, 0), (1, 0)], "d": [(2, 0), (3, 0)]}
    grads2d = [None] * len(big)
    big_out = [None] * len(big)

    def finish(groups, after, tag):
        regions, idxs = [], []
        for gtag, handle in groups:
            landed, parts = _exchange_wait(handle, after, name=f"rs_exchange_wait_{gtag}")
            for i, (l, p, (wi, off)) in enumerate(zip(landed, parts, pieces[gtag])):
                total = as2d(big[wi]).shape[0]
                grads2d[wi] = _chip_sum(l, p, where, total, off, grads2d[wi], name=f"rs_chip_sum_{gtag}{i}")
                if wi not in idxs:
                    idxs.append(wi)
                regions.append((idxs.index(wi), off, 2 * l.shape[1]))
        joined = _sibling_join([grads2d[wi] for wi in idxs], regions, name=f"rs_sibling_join_{tag}")
        for wi, gr in zip(idxs, joined):
            w = big[wi]
            d_, m_, v_, g_ = _adamw(as2d(w), gr, as2d(big_m[wi]), as2d(big_v[wi]), name=f"adamw{wi}",
                                    emit_grad=True)
            big_out[wi] = (g_.reshape(w.shape), d_.reshape(w.shape), m_.reshape(w.shape), v_.reshape(w.shape))

    finish([("a", rs_a), ("b", rs_b), ("c", rs_c)], dx, "abc")
    finish([("d", rs_d)], big_out[0][1], "d")

    def small(a):
        return (a[0:2], a[2:4], a[4:6], a[6:8])

    def assemble(sm, cwv, kind):
        pre, post, fpre, fpost = small(sm)
        b = [t[kind] for t in big_out]
        return [pre, post, fpre, fpost, b[0], b[1], b[2], cwv[0:3].reshape(conv_w.shape), b[3],
                sm[8], b[4], b[5].reshape(w_q.shape), b[6].reshape(w_o.shape)]

    grads = assemble(red, jnp.concatenate([g_conv_w, pad5], axis=0), 0)
    deltas = assemble(d_small, d_cw, 1)
    new_m = assemble(nm_small, nm_cw, 2)
    new_v = assemble(nv_small, nv_cw, 3)
    return (loss, dx.reshape(x.shape), *grads, *deltas, *new_m, *new_v)
```

```python
import functools

import jax
import jax.numpy as jnp
from jax import lax
from jax.experimental import pallas as pl
from jax.experimental.pallas import tpu as pltpu

HEAD_DIM = 64
BAND = 128
BRANCHES = ((128, 1), (512, 4), (2048, 16))
ROPE_THETA = 10000.0
RMS_EPS = 1e-6
NEG_INF = -1e30
ADAM_LR = 0.001
ADAM_B1 = 0.9
ADAM_B2 = 0.999
ADAM_EPS = 1e-08
ADAM_WD = 0.01
ADAM_STEP = 10

N_CHIPS = 4
N_DEV = 8
LANES = 128
MXU_COLS = 256
ROW_BLOCK = 512
ROW_BLOCK_WIDE = 1024
ATTN_BLOCK_ROWS = 2048
VMEM_LIMIT = 56 * 1024 * 1024
SMALL_ROWS = 16
ADAMW_BLOCK_BYTES = 1024 * 1024
DMA_CHUNK_BYTES = 512 * 1024

BF = jnp.bfloat16
F32 = jnp.float32
MESH = pl.DeviceIdType.MESH
ANY = pl.BlockSpec(memory_space=pl.ANY)


def _cp(*sem):
    return pltpu.CompilerParams(dimension_semantics=sem, vmem_limit_bytes=VMEM_LIMIT)


def _rot_half(t, first):
    return jnp.where(first, pltpu.roll(t, 96, 1), pltpu.roll(t, 32, 1))


def _sigmoid(x):
    return pl.reciprocal(1.0 + jnp.exp(-x), approx=True)


def _col_chunks(n):
    return [(c0, min(MXU_COLS, n - c0)) for c0 in range(0, n, MXU_COLS)]


def _first_half_mask(rows):
    lane = lax.broadcasted_iota(jnp.int32, (rows, LANES), 1)
    return (lane % HEAD_DIM) < (HEAD_DIM // 2)


def _normed_rows(j, rows, x_ref, g_ref, xn_ref, xs, last_start, tm):
    @pl.when(j == 0)
    def _():
        xv = x_ref[...]
        r = lax.rsqrt(jnp.mean(xv * xv, axis=-1, keepdims=True) + RMS_EPS)
        xn = (xv * r * g_ref[...]).astype(BF)
        xs[rows, :] = xn
        xn_ref[...] = xn

    @pl.when(j > 0)
    def _():
        xn_ref[...] = xs[pl.ds(last_start, tm), :]


def _norm_matmul(x, gain, wg, cos2, ss2, *, name, rope_shards, scale, out_dtype):
    T, D = x.shape
    n = wg.shape[2]
    tm = min(ROW_BLOCK_WIDE, T)
    ni = T // tm

    def body(x_ref, g_ref, w_ref, cos_ref, ss_ref, y_ref, xn_ref, xs):
        j = pl.program_id(0)
        rows = pl.ds(pl.multiple_of(pl.program_id(1) * tm, tm), tm)
        _normed_rows(j, rows, x_ref, g_ref, xn_ref, xs, (ni - 1) * tm, tm)
        acc = jnp.dot(xs[rows, :], w_ref[...], preferred_element_type=F32)

        def plain():
            y_ref[...] = acc.astype(out_dtype)

        def rope():
            cosv = cos_ref[...]
            ssv = ss_ref[...]
            first = _first_half_mask(tm)
            for ci in range(n // LANES):
                t = acc[:, ci * LANES:(ci + 1) * LANES]
                y = (t * cosv + _rot_half(t, first) * ssv) * scale
                y_ref[:, ci * LANES:(ci + 1) * LANES] = y.astype(out_dtype)

        if rope_shards == 0:
            plain()
        elif rope_shards == N_CHIPS:
            rope()
        else:
            pl.when(j < rope_shards)(rope)
            pl.when(j >= rope_shards)(plain)

    first_pass = lambda j, i: (jnp.where(j == 0, i, ni - 1), 0)
    return pl.pallas_call(
        body, name=name,
        grid=(N_CHIPS, ni),
        in_specs=[
            pl.BlockSpec((tm, D), first_pass),
            pl.BlockSpec((1, D), lambda j, i: (0, 0)),
            pl.BlockSpec((None, D, n), lambda j, i: (j, 0, 0)),
            pl.BlockSpec((tm, LANES), lambda j, i: (i, 0)),
            pl.BlockSpec((tm, LANES), lambda j, i: (i, 0)),
        ],
        out_specs=[
            pl.BlockSpec((tm, n), lambda j, i: (i, j)),
            pl.BlockSpec((tm, D), first_pass),
        ],
        out_shape=[jax.ShapeDtypeStruct((T, N_CHIPS * n), out_dtype),
                   jax.ShapeDtypeStruct((T, D), BF)],
        scratch_shapes=[pltpu.VMEM((T, D), BF)],
        compiler_params=_cp("arbitrary", "arbitrary"),
    )(x, gain, wg, cos2, ss2)


def _norm_swiglu(x, gain, wg, layer, *, name):
    T, D = x.shape
    n = wg.shape[2]
    tm = min(ROW_BLOCK, T)

    ni = T // tm

    def body(x_ref, g_ref, wg_ref, wu_ref, go_ref, uo_ref, ao_ref, xn_ref, xs):
        j = pl.program_id(0)
        rows = pl.ds(pl.multiple_of(pl.program_id(1) * tm, tm), tm)
        _normed_rows(j, rows, x_ref, g_ref, xn_ref, xs, (ni - 1) * tm, tm)
        g = jnp.dot(xs[rows, :], wg_ref[...], preferred_element_type=F32)
        u = jnp.dot(xs[rows, :], wu_ref[...], preferred_element_type=F32)
        go_ref[...] = g.astype(BF)
        uo_ref[...] = u.astype(BF)
        ao_ref[...] = (g * _sigmoid(g) * u).astype(BF)

    half = N_CHIPS // 2
    first_pass = lambda j, i: (jnp.where(j == 0, i, ni - 1), 0)
    act = jax.ShapeDtypeStruct((T, half * n), BF)
    return pl.pallas_call(
        body, name=name,
        grid=(half, ni),
        in_specs=[
            pl.BlockSpec((tm, D), first_pass),
            pl.BlockSpec((1, D), lambda j, i: (0, 0)),
            pl.BlockSpec((None, D, n), lambda j, i: (j, layer, 0)),
            pl.BlockSpec((None, D, n), lambda j, i: (j + half, layer, 0)),
        ],
        out_specs=[
            pl.BlockSpec((tm, n), lambda j, i: (i, j)),
            pl.BlockSpec((tm, n), lambda j, i: (i, j)),
            pl.BlockSpec((tm, n), lambda j, i: (i, j)),
            pl.BlockSpec((tm, D), first_pass),
        ],
        out_shape=[act, act, act, jax.ShapeDtypeStruct((T, D), BF)],
        scratch_shapes=[pltpu.VMEM((T, D), BF)],
        compiler_params=_cp("arbitrary", "arbitrary"),
    )(x, gain, wg, wg)


def _matmul_postnorm(a, w3, widx, gain, h_old, *, name):
    T, K = a.shape
    D = w3.shape[2]
    tm = min(ROW_BLOCK_WIDE, T)

    def body(a_ref, w_ref, g_ref, h_ref, y_ref, hn_ref):
        y = jnp.dot(a_ref[...].astype(BF), w_ref[...], preferred_element_type=F32)
        y_ref[...] = y.astype(BF)
        r = lax.rsqrt(jnp.mean(y * y, axis=-1, keepdims=True) + RMS_EPS)
        hn_ref[...] = h_ref[...] + y * r * g_ref[...]

    return pl.pallas_call(
        body, name=name,
        grid=(T // tm,),
        in_specs=[
            pl.BlockSpec((tm, K), lambda i: (i, 0)),
            pl.BlockSpec((None, K, D), lambda i: (widx, 0, 0)),
            pl.BlockSpec((1, D), lambda i: (0, 0)),
            pl.BlockSpec((tm, D), lambda i: (i, 0)),
        ],
        out_specs=[pl.BlockSpec((tm, D), lambda i: (i, 0)),
                   pl.BlockSpec((tm, D), lambda i: (i, 0))],
        out_shape=[jax.ShapeDtypeStruct((T, D), BF), jax.ShapeDtypeStruct((T, D), F32)],
        compiler_params=_cp("parallel"),
    )(a, w3, gain, h_old)


def _matmul_postnorm_loss(a, w3, widx, gain, h_old, target, *, name):
    T, K = a.shape
    D = w3.shape[2]
    tm = min(ROW_BLOCK, T)

    def body(a_ref, w_ref, g_ref, h_ref, t_ref, y_ref, dh_ref, s_ref):
        @pl.when(pl.program_id(0) == 0)
        def _():
            s_ref[...] = jnp.zeros_like(s_ref)

        y = jnp.dot(a_ref[...].astype(BF), w_ref[...], preferred_element_type=F32)
        y_ref[...] = y.astype(BF)
        r = lax.rsqrt(jnp.mean(y * y, axis=-1, keepdims=True) + RMS_EPS)
        e = (h_ref[...] + y * r * g_ref[...]) - t_ref[...]
        dh_ref[...] = e * (1.0 / D)
        s_ref[...] += jnp.sum(e * e)

    rows = pl.BlockSpec((tm, D), lambda i: (i, 0))
    return pl.pallas_call(
        body, name=name,
        grid=(T // tm,),
        in_specs=[
            pl.BlockSpec((tm, K), lambda i: (i, 0)),
            pl.BlockSpec((None, K, D), lambda i: (widx, 0, 0)),
            pl.BlockSpec((1, D), lambda i: (0, 0)),
            rows, rows,
        ],
        out_specs=[rows, rows, pl.BlockSpec((8, LANES), lambda i: (0, 0))],
        out_shape=[jax.ShapeDtypeStruct((T, D), BF), jax.ShapeDtypeStruct((T, D), F32),
                   jax.ShapeDtypeStruct((8, LANES), F32)],
        compiler_params=_cp("arbitrary"),
    )(a, w3, gain, h_old, target)


def _shift_down(u, k):
    row = lax.broadcasted_iota(jnp.int32, u.shape, 0)
    return jnp.where(row >= k, pltpu.roll(u, k, 0), 0.0)


def _shift_up(u, k):
    T = u.shape[0]
    row = lax.broadcasted_iota(jnp.int32, u.shape, 0)
    return jnp.where(row < T - k, pltpu.roll(u, T - k, 0), 0.0)


def _conv_fwd(z, cw, *, name):
    T = z.shape[0]
    D = z.shape[1] // 3
    tc = cw.shape[2]
    nb = D // tc

    def body(b_ref, c_ref, h_ref, w_ref, o_ref):
        u = c_ref[...].astype(F32) * h_ref[...].astype(F32)
        w = w_ref[...]
        conv = w[2:3] * u + w[1:2] * _shift_down(u, 1) + w[0:1] * _shift_down(u, 2)
        o_ref[...] = (b_ref[...].astype(F32) * conv).astype(BF)

    return pl.pallas_call(
        body, name=name,
        grid=(nb,),
        in_specs=[
            pl.BlockSpec((T, tc), lambda j: (0, j)),
            pl.BlockSpec((T, tc), lambda j: (0, nb + j)),
            pl.BlockSpec((T, tc), lambda j: (0, 2 * nb + j)),
            pl.BlockSpec((None, 8, tc), lambda j: (j, 0, 0)),
        ],
        out_specs=pl.BlockSpec((T, tc), lambda j: (0, j)),
        out_shape=jax.ShapeDtypeStruct((T, D), BF),
        compiler_params=_cp("parallel"),
    )(z, z, z, cw)


def _conv_bwd(z, cw, dv, *, name):
    T = z.shape[0]
    D = z.shape[1] // 3
    tc = LANES
    nb = D // tc
    per = cw.shape[2] // tc

    def body(b_ref, c_ref, h_ref, w_ref, dv_ref, dz_ref, dw_ref, stage, sems):
        j = pl.program_id(0)
        slot = j % 2

        def slab(p, jj, s):
            col = pl.multiple_of((p * nb + jj) * tc, tc)
            return pltpu.make_async_copy(stage.at[s, p], dz_ref.at[:, pl.ds(col, tc)], sems.at[s, p])

        @pl.when(j >= 2)
        def _():
            for p in range(3):
                slab(p, j - 2, slot).wait()

        c = c_ref[...].astype(F32)
        h = h_ref[...].astype(F32)
        u = c * h
        u1 = _shift_down(u, 1)
        u2 = _shift_down(u, 2)
        w = w_ref[...]
        dvv = dv_ref[...].astype(F32)
        dconv = dvv * b_ref[...].astype(F32)
        du = w[2:3] * dconv + w[1:2] * _shift_up(dconv, 1) + w[0:1] * _shift_up(dconv, 2)
        rows = lax.broadcasted_iota(jnp.int32, (8, tc), 0)
        dw_ref[...] = jnp.where(rows == 0, jnp.sum(dconv * u2, axis=0, keepdims=True),
                                jnp.where(rows == 1, jnp.sum(dconv * u1, axis=0, keepdims=True),
                                          jnp.where(rows == 2, jnp.sum(dconv * u, axis=0, keepdims=True), 0.0)))
        stage[slot, 0] = (dvv * (w[2:3] * u + w[1:2] * u1 + w[0:1] * u2)).astype(BF)
        stage[slot, 1] = (du * h).astype(BF)
        stage[slot, 2] = (du * c).astype(BF)
        for p in range(3):
            slab(p, j, slot).start()

        @pl.when(j == nb - 1)
        def _():
            for p in range(3):
                slab(p, j, slot).wait()
            if nb > 1:
                for p in range(3):
                    slab(p, j - 1, 1 - slot).wait()

    return pl.pallas_call(
        body, name=name,
        grid=(nb,),
        in_specs=[
            pl.BlockSpec((T, tc), lambda j: (0, j)),
            pl.BlockSpec((T, tc), lambda j: (0, nb + j)),
            pl.BlockSpec((T, tc), lambda j: (0, 2 * nb + j)),
            pl.BlockSpec((None, 8, tc), lambda j: (j // per, 0, j % per)),
            pl.BlockSpec((T, tc), lambda j: (0, j)),
        ],
        out_specs=[ANY, pl.BlockSpec((8, tc), lambda j: (0, j))],
        out_shape=[jax.ShapeDtypeStruct((T, 3 * D), BF), jax.ShapeDtypeStruct((8, D), F32)],
        scratch_shapes=[pltpu.VMEM((2, 3, T, tc), BF), pltpu.SemaphoreType.DMA((2, 3))],
        compiler_params=_cp("arbitrary"),
    )(z, z, z, cw, dv)


def _strided(base, count, d):
    return pl.ds(base, count, stride=d) if d > 1 else pl.ds(pl.multiple_of(base, BAND), count)


def _fill_band_bias(bias):
    qi = lax.broadcasted_iota(jnp.int32, (2 * BAND, 2 * BAND), 0) % BAND
    kj = lax.broadcasted_iota(jnp.int32, (2 * BAND, 2 * BAND), 1)
    dist = qi + BAND - kj
    band = (dist >= 0) & (dist <= BAND)
    bias[0] = jnp.where(band & (kj >= BAND), 0.0, NEG_INF)
    bias[1] = jnp.where(band, 0.0, NEG_INF)


def _attn_block_rows(T):
    return min(ATTN_BLOCK_ROWS, T)


def _head_mask():
    lane = lax.broadcasted_iota(jnp.int32, (2 * BAND, LANES), 1)
    row = lax.broadcasted_iota(jnp.int32, (2 * BAND, LANES), 0)
    return (lane < HEAD_DIM) == (row < BAND)


def _attn_fwd(q_all, kv_all, *, name):
    T = q_all.shape[0]
    NB = len(BRANCHES)
    Dm = q_all.shape[1] // NB
    HP = Dm // LANES
    R = _attn_block_rows(T)
    units = R // BAND
    dmax = max(d for _, d in BRANCHES)

    def body(*refs):
        ins = refs[:5 * NB]
        o_ref, l_ref, qbuf, kbuf, vbuf, o_s, l_s, bias = refs[5 * NB:]
        n = pl.program_id(0)
        pl.when((n == 0) & (pl.program_id(1) == 0))(lambda: _fill_band_bias(bias))
        hm = _head_mask()
        low = lax.broadcasted_iota(jnp.int32, (BAND, LANES), 1) < HEAD_DIM

        for g, (_, d) in enumerate(BRANCHES):
            q_ref, kp_ref, kc_ref, vp_ref, vc_ref = ins[5 * g:5 * g + 5]
            pr = BAND * d
            qbuf[...] = q_ref[...].astype(F32)
            kbuf[0:pr, :] = kp_ref[...].astype(F32)
            kbuf[pr:pr + R, :] = kc_ref[...].astype(F32)
            vbuf[0:pr, :] = vp_ref[...].astype(F32)
            vbuf[pr:pr + R, :] = vc_ref[...].astype(F32)

            def unit(u, carry, g=g, d=d, pr=pr):
                sub = u // d
                base = sub * pr + (u - sub * d)
                q = qbuf[_strided(base, BAND, d), :]
                q2 = jnp.where(hm, jnp.concatenate([q, q], axis=0), 0.0).astype(BF)
                k2 = kbuf[_strided(base, 2 * BAND, d), :].astype(BF)
                v2 = vbuf[_strided(base, 2 * BAND, d), :].astype(BF)
                s = lax.dot_general(q2, k2, (((1,), (1,)), ((), ())), preferred_element_type=F32)
                s = s + bias[((n > 0) | (sub > 0)).astype(jnp.int32)]
                m = jnp.max(s, axis=-1, keepdims=True)
                p = jnp.exp(s - m)
                l = jnp.sum(p, axis=-1, keepdims=True)
                pv = jnp.dot(p.astype(BF), v2, preferred_element_type=F32) * (1.0 / l)
                lse = m + jnp.log(l)
                o_s[g, _strided(base, BAND, d), :] = jnp.where(low, pv[:BAND], pv[BAND:])
                l_s[g, _strided(base, BAND, d), :] = jnp.where(low, lse[:BAND], lse[BAND:])
                return carry

            lax.fori_loop(0, units, unit, 0, unroll=16)

        def merge(i, carry):
            sl = pl.ds(pl.multiple_of(i * BAND, BAND), BAND)
            lv = [l_s[g, sl, :] for g in range(NB)]
            m = functools.reduce(jnp.maximum, lv)
            e = [jnp.exp(v - m) for v in lv]
            tot = functools.reduce(jnp.add, e)
            inv = 1.0 / tot
            o_ref[sl, :] = functools.reduce(jnp.add, [(e[g] * inv) * o_s[g, sl, :] for g in range(NB)])
            l_ref[sl, :] = m + jnp.log(tot)
            return carry

        lax.fori_loop(0, units, merge, 0)

    in_specs, args = [], []
    for g, (_, d) in enumerate(BRANCHES):
        per = R // (BAND * d)
        for col, rows, idx in (
                (g * HP, R, lambda n, hp: n),
                (g * HP, BAND * d, lambda n, hp, per=per: jnp.maximum(n * per - 1, 0)),
                (g * HP, R, lambda n, hp: n),
                ((NB + g) * HP, BAND * d, lambda n, hp, per=per: jnp.maximum(n * per - 1, 0)),
                ((NB + g) * HP, R, lambda n, hp: n)):
            in_specs.append(pl.BlockSpec((rows, LANES), lambda n, hp, col=col, idx=idx: (idx(n, hp), col + hp)))
        args += [q_all, kv_all, kv_all, kv_all, kv_all]
    out = pl.BlockSpec((R, LANES), lambda n, hp: (n, hp))
    return pl.pallas_call(
        body, name=name,
        grid=(T // R, HP),
        in_specs=in_specs,
        out_specs=[out, out],
        out_shape=[jax.ShapeDtypeStruct((T, Dm), F32), jax.ShapeDtypeStruct((T, Dm), F32)],
        scratch_shapes=[pltpu.VMEM((R, LANES), F32),
                        pltpu.VMEM((BAND * dmax + R, LANES), F32), pltpu.VMEM((BAND * dmax + R, LANES), F32),
                        pltpu.VMEM((NB, R, LANES), F32), pltpu.VMEM((NB, R, LANES), F32),
                        pltpu.VMEM((2, 2 * BAND, 2 * BAND), F32)],
        compiler_params=_cp("arbitrary", "arbitrary"),
    )(*args)


def _attn_bwd(q_all, kv_all, do, o, lse, cos2, ss2, g, d, prev, *, name, dep=None):
    T = q_all.shape[0]
    NB = len(BRANCHES)
    Dm = q_all.shape[1] // NB
    HP = Dm // LANES
    R = _attn_block_rows(T)
    nblk = T // R
    units = R // BAND
    pr = BAND * d
    per = R // pr
    scale = HEAD_DIM ** -0.5

    def rope_bwd(t, cosv, ssv, first):
        return t * cosv - _rot_half(t, first) * ssv

    def body(q_ref, kp_ref, kc_ref, vp_ref, vc_ref, do_ref, o_ref, l_ref, cos_ref, ss_ref, *rest):
        dq_ref, dk_ref, dv_ref, qbuf, kbuf, vbuf, dq_s, dk_s, dv_s, pend_k, pend_v, bias = rest[-12:]
        i = pl.program_id(1)
        n = nblk - 1 - i
        pl.when((i == 0) & (pl.program_id(0) == 0))(lambda: _fill_band_bias(bias))
        hm = _head_mask()
        low = lax.broadcasted_iota(jnp.int32, (BAND, LANES), 1) < HEAD_DIM

        qbuf[...] = q_ref[...].astype(F32)
        kbuf[0:pr, :] = kp_ref[...].astype(F32)
        kbuf[pr:pr + R, :] = kc_ref[...].astype(F32)
        vbuf[0:pr, :] = vp_ref[...].astype(F32)
        vbuf[pr:pr + R, :] = vc_ref[...].astype(F32)

        @pl.when(i == 0)
        def _():
            pend_k[...] = jnp.zeros_like(pend_k)
            pend_v[...] = jnp.zeros_like(pend_v)

        def unit(u, carry):
            sub = per - 1 - u // d
            cls = u % d
            base = sub * pr + cls
            sl = _strided(base, BAND, d)
            sl2 = _strided(base, 2 * BAND, d)
            q = qbuf[sl, :]
            dov = do_ref[sl, :]
            ov = o_ref[sl, :]
            lv = l_ref[sl, :]
            q2 = jnp.where(hm, jnp.concatenate([q, q], axis=0), 0.0).astype(BF)
            do2 = jnp.where(hm, jnp.concatenate([dov, dov], axis=0), 0.0)
            oo = dov * ov
            delta = jnp.sum(jnp.where(hm, jnp.concatenate([oo, oo], axis=0), 0.0), axis=-1, keepdims=True)
            lse2 = jnp.concatenate([lv[:, 0:1], lv[:, HEAD_DIM:HEAD_DIM + 1]], axis=0)
            do2 = do2.astype(BF)
            k2 = kbuf[sl2, :].astype(BF)
            v2 = vbuf[sl2, :].astype(BF)
            s = lax.dot_general(q2, k2, (((1,), (1,)), ((), ())), preferred_element_type=F32)
            p = jnp.exp(s + bias[((n > 0) | (sub > 0)).astype(jnp.int32)] - lse2)
            dp = lax.dot_general(do2, v2, (((1,), (1,)), ((), ())), preferred_element_type=F32)
            ds = (p * (dp - delta)).astype(BF)
            dq2 = jnp.dot(ds, k2, preferred_element_type=F32)
            dq = jnp.where(low, dq2[:BAND], dq2[BAND:])
            dq_s[sl, :] = dq
            tn = (((0,), (0,)), ((), ()))
            dk2 = lax.dot_general(ds, q2, tn, preferred_element_type=F32)
            dv2 = lax.dot_general(p.astype(BF), do2, tn, preferred_element_type=F32)
            dk_s[sl, :] = dk2[BAND:] + pend_k[cls]
            dv_s[sl, :] = dv2[BAND:] + pend_v[cls]
            pend_k[cls] = dk2[:BAND]
            pend_v[cls] = dv2[:BAND]
            return carry

        lax.fori_loop(0, units, unit, 0, unroll=8)

        whole = _first_half_mask(R)
        dq_ref[...] = (rope_bwd(dq_s[...], cos_ref[...], ss_ref[...], whole) * scale).astype(BF)
        dk_ref[...] = rope_bwd(dk_s[...], cos_ref[...], ss_ref[...], whole).astype(BF)
        dv_ref[...] = dv_s[...].astype(BF)

    blk = (R, LANES)
    pblk = (pr, LANES)
    cur = lambda hp, i: nblk - 1 - i
    prv = lambda hp, i: jnp.maximum((nblk - 1 - i) * per - 1, 0)
    in_specs = [
        pl.BlockSpec(blk, lambda hp, i: (cur(hp, i), g * HP + hp)),
        pl.BlockSpec(pblk, lambda hp, i: (prv(hp, i), g * HP + hp)),
        pl.BlockSpec(blk, lambda hp, i: (cur(hp, i), g * HP + hp)),
        pl.BlockSpec(pblk, lambda hp, i: (prv(hp, i), (NB + g) * HP + hp)),
        pl.BlockSpec(blk, lambda hp, i: (cur(hp, i), (NB + g) * HP + hp)),
        pl.BlockSpec(blk, lambda hp, i: (cur(hp, i), hp)),
        pl.BlockSpec(blk, lambda hp, i: (cur(hp, i), hp)),
        pl.BlockSpec(blk, lambda hp, i: (cur(hp, i), hp)),
        pl.BlockSpec(blk, lambda hp, i: (cur(hp, i), 0)),
        pl.BlockSpec(blk, lambda hp, i: (cur(hp, i), 0)),
    ]
    args = [q_all, kv_all, kv_all, kv_all, kv_all, do, o, lse, cos2, ss2]
    if dep is not None:
        in_specs.append(ANY)
        args.append(dep)
    aliases = {}
    if prev is not None:
        in_specs += [ANY, ANY, ANY]
        aliases = {len(args): 0, len(args) + 1: 1, len(args) + 2: 2}
        args += list(prev)
    wide = jax.ShapeDtypeStruct((T, NB * Dm), BF)
    out = pl.BlockSpec(blk, lambda hp, i: (cur(hp, i), g * HP + hp))
    return pl.pallas_call(
        body, name=name,
        grid=(HP, nblk),
        in_specs=in_specs,
        out_specs=[out, out, out],
        out_shape=[wide, wide, wide],
        scratch_shapes=[pltpu.VMEM(blk, F32), pltpu.VMEM((pr + R, LANES), F32), pltpu.VMEM((pr + R, LANES), F32),
                        pltpu.VMEM(blk, F32), pltpu.VMEM(blk, F32), pltpu.VMEM(blk, F32),
                        pltpu.VMEM((d, BAND, LANES), F32), pltpu.VMEM((d, BAND, LANES), F32),
                        pltpu.VMEM((2, 2 * BAND, 2 * BAND), F32)],
        input_output_aliases=aliases,
        compiler_params=_cp("arbitrary", "arbitrary"),
    )(*args)


def _postnorm_bwd(dh, y, g_ref_val):
    r = lax.rsqrt(jnp.mean(y * y, axis=-1, keepdims=True) + RMS_EPS)
    yn = y * r
    dyn = dh * g_ref_val
    dy = r * (dyn - yn * jnp.mean(dyn * yn, axis=-1, keepdims=True))
    return dy, yn


def _after(body, n_in, dep):
    if dep is None:
        return body
    return lambda *refs: body(*refs[:n_in], *refs[n_in + 1:])


def _dep_spec(dep):
    return [] if dep is None else [ANY]


def _dep_arg(dep):
    return [] if dep is None else [dep]


def _postnorm_bwd_matmul(dh, y, gain, w3, widx, *, name, da_dtype, dep=None):
    T, D = dh.shape
    K = w3.shape[1]
    tm = min(ROW_BLOCK_WIDE, T)

    def body(dh_ref, y_ref, g_ref, w_ref, dy_ref, da_ref, dg_ref):
        i = pl.program_id(0)

        @pl.when(i == 0)
        def _():
            dg_ref[...] = jnp.zeros_like(dg_ref)

        dhv = dh_ref[...]
        dy, yn = _postnorm_bwd(dhv, y_ref[...].astype(F32), g_ref[...])
        dg_ref[...] += jnp.sum(dhv * yn, axis=0, keepdims=True)
        dyb = dy.astype(BF)
        dy_ref[...] = dyb
        da = lax.dot_general(dyb, w_ref[...], (((1,), (1,)), ((), ())), preferred_element_type=F32)
        da_ref[...] = da.astype(da_dtype)

    return pl.pallas_call(
        _after(body, 4, dep), name=name,
        grid=(T // tm,),
        in_specs=[
            pl.BlockSpec((tm, D), lambda i: (i, 0)),
            pl.BlockSpec((tm, D), lambda i: (i, 0)),
            pl.BlockSpec((1, D), lambda i: (0, 0)),
            pl.BlockSpec((None, K, D), lambda i: (widx, 0, 0)),
        ] + _dep_spec(dep),
        out_specs=[pl.BlockSpec((tm, D), lambda i: (i, 0)),
                   pl.BlockSpec((tm, K), lambda i: (i, 0)),
                   pl.BlockSpec((1, D), lambda i: (0, 0))],
        out_shape=[jax.ShapeDtypeStruct((T, D), BF), jax.ShapeDtypeStruct((T, K), da_dtype),
                   jax.ShapeDtypeStruct((1, D), F32)],
        compiler_params=_cp("arbitrary"),
    )(dh, y, gain, w3, *_dep_arg(dep))


def _postnorm_bwd_swiglu(dh, y, gain, wd3, layer, g, u, *, name, dep=None):
    T, D = dh.shape
    F = wd3.shape[1]
    nf = F // 2
    tm = min(ROW_BLOCK, T)

    def body(dh_ref, y_ref, g_ref, w_ref, gg_ref, uu_ref, dy_ref, dgo_ref, duo_ref, dgain_ref, dys):
        i = pl.program_id(0)
        j = pl.program_id(1)

        @pl.when((i == 0) & (j == 0))
        def _():
            dgain_ref[...] = jnp.zeros_like(dgain_ref)

        @pl.when(j == 0)
        def _():
            dhv = dh_ref[...]
            dy, yn = _postnorm_bwd(dhv, y_ref[...].astype(F32), g_ref[...])
            dgain_ref[...] += jnp.sum(dhv * yn, axis=0, keepdims=True)
            dyb = dy.astype(BF)
            dys[...] = dyb
            dy_ref[...] = dyb

        for c0, cw in _col_chunks(nf):
            cols = slice(c0, c0 + cw)
            da = lax.dot_general(dys[...], w_ref[cols, :], (((1,), (1,)), ((), ())), preferred_element_type=F32)
            gv = gg_ref[:, cols].astype(F32)
            uv = uu_ref[:, cols].astype(F32)
            sg = _sigmoid(gv)
            silu = gv * sg
            dgo_ref[:, cols] = (da * uv * (sg + silu * (1.0 - sg))).astype(BF)
            duo_ref[:, cols] = (da * silu).astype(BF)

    act = jax.ShapeDtypeStruct((T, F), BF)
    return pl.pallas_call(
        _after(body, 6, dep), name=name,
        grid=(T // tm, 2),
        in_specs=[
            pl.BlockSpec((tm, D), lambda i, j: (i, 0)),
            pl.BlockSpec((tm, D), lambda i, j: (i, 0)),
            pl.BlockSpec((1, D), lambda i, j: (0, 0)),
            pl.BlockSpec((None, nf, D), lambda i, j: (layer, j, 0)),
            pl.BlockSpec((tm, nf), lambda i, j: (i, j)),
            pl.BlockSpec((tm, nf), lambda i, j: (i, j)),
        ] + _dep_spec(dep),
        out_specs=[pl.BlockSpec((tm, D), lambda i, j: (i, 0)),
                   pl.BlockSpec((tm, nf), lambda i, j: (i, j)),
                   pl.BlockSpec((tm, nf), lambda i, j: (i, j)),
                   pl.BlockSpec((1, D), lambda i, j: (0, 0))],
        out_shape=[jax.ShapeDtypeStruct((T, D), BF), act, act, jax.ShapeDtypeStruct((1, D), F32)],
        scratch_shapes=[pltpu.VMEM((tm, D), BF)],
        compiler_params=_cp("arbitrary", "arbitrary"),
    )(dh, y, gain, wd3, g, u, *_dep_arg(dep))


def _matmul_prenorm_bwd(dzs, wg, layer, h, gain, dh_in, *, name):
    T, D = h.shape
    n = wg.shape[2]
    pair = len(dzs) == 2
    tm = min(ROW_BLOCK if pair else ROW_BLOCK_WIDE, T)
    nj = N_CHIPS // 2 if pair else N_CHIPS

    def body(*refs):
        dz_refs = refs[:len(dzs)]
        w_refs = refs[len(dzs):2 * len(dzs)]
        h_ref, g_ref, dhi_ref, dh_ref, dg_ref, acc = refs[2 * len(dzs):]
        i = pl.program_id(0)
        j = pl.program_id(1)

        @pl.when((i == 0) & (j == 0))
        def _():
            dg_ref[...] = jnp.zeros_like(dg_ref)

        part = None
        for dz_ref, w_ref in zip(dz_refs, w_refs):
            t = lax.dot_general(dz_ref[...].astype(BF), w_ref[...], (((1,), (1,)), ((), ())),
                                preferred_element_type=F32)
            part = t if part is None else part + t

        @pl.when(j == 0)
        def _():
            acc[...] = part

        @pl.when(j > 0)
        def _():
            acc[...] += part

        @pl.when(j == nj - 1)
        def _():
            dhn = acc[...]
            hv = h_ref[...]
            r = lax.rsqrt(jnp.mean(hv * hv, axis=-1, keepdims=True) + RMS_EPS)
            xh = hv * r
            dg_ref[...] += jnp.sum(dhn * xh, axis=0, keepdims=True)
            dxn = dhn * g_ref[...]
            dh_ref[...] = dhi_ref[...] + r * (dxn - xh * jnp.mean(dxn * xh, axis=-1, keepdims=True))

    in_specs = [pl.BlockSpec((tm, n), lambda i, j: (i, j)) for _ in dzs]
    if pair:
        in_specs += [pl.BlockSpec((None, D, n), lambda i, j: (j, layer, 0)),
                     pl.BlockSpec((None, D, n), lambda i, j: (j + nj, layer, 0))]
    else:
        in_specs += [pl.BlockSpec((None, D, n), lambda i, j: (j, layer, 0))]
    in_specs += [pl.BlockSpec((tm, D), lambda i, j: (i, 0)),
                 pl.BlockSpec((1, D), lambda i, j: (0, 0)),
                 pl.BlockSpec((tm, D), lambda i, j: (i, 0))]
    return pl.pallas_call(
        body, name=name,
        grid=(T // tm, nj),
        in_specs=in_specs,
        out_specs=[pl.BlockSpec((tm, D), lambda i, j: (i, 0)), pl.BlockSpec((1, D), lambda i, j: (0, 0))],
        out_shape=[jax.ShapeDtypeStruct((T, D), F32), jax.ShapeDtypeStruct((1, D), F32)],
        scratch_shapes=[pltpu.VMEM((tm, D), F32)],
        compiler_params=_cp("arbitrary", "arbitrary"),
    )(*dzs, *([wg] * len(dzs)), h, gain, dh_in)


def _grad_matmul(a, b, out_shape3, tme, tne, out_index, prev, *, name, dep=None):
    T, M = a.shape
    N = b.shape[1]
    tk = min(ROW_BLOCK_WIDE, T)
    nk = T // tk

    def body(a_ref, b_ref, *rest):
        o_ref, acc = rest[-2:]
        k = pl.program_id(2)
        part = jnp.dot(a_ref[...].astype(BF).T, b_ref[...].astype(BF), preferred_element_type=F32)

        @pl.when(k == 0)
        def _():
            acc[...] = part

        @pl.when(k > 0)
        def _():
            acc[...] += part

        @pl.when(k == nk - 1)
        def _():
            o_ref[...] = acc[...].astype(BF)

    in_specs = [pl.BlockSpec((tk, tme), lambda i, j, k: (k, i)),
                pl.BlockSpec((tk, tne), lambda i, j, k: (k, j))]
    args = [a, b]
    aliases = {}
    if prev is not None:
        in_specs.append(ANY)
        args.append(prev)
        aliases = {2: 0}
    in_specs += _dep_spec(dep)
    args += _dep_arg(dep)
    return pl.pallas_call(
        body, name=name,
        grid=(M // tme, N // tne, nk),
        in_specs=in_specs,
        out_specs=pl.BlockSpec((None, tme, tne), lambda i, j, k: out_index(i, j)),
        out_shape=jax.ShapeDtypeStruct(out_shape3, BF),
        scratch_shapes=[pltpu.VMEM((tme, tne), F32)],
        input_output_aliases=aliases,
        compiler_params=_cp("parallel", "parallel", "arbitrary"),
    )(*args)


def _row_tile(R, cap=512):
    fit = [t for t in range(16, min(R, cap) + 1, 16) if R % t == 0]
    return max(fit) if fit else R


def _cast_place(w3, layer, where, dtype, *, name, dep=None):
    _, R, C = w3.shape
    tr = _row_tile(R)

    def body(s_ref, w_ref, o_ref):
        o_ref[...] = w_ref[...].astype(o_ref.dtype)

    return pl.pallas_call(
        _after(body, 2, dep), name=name,
        grid_spec=pltpu.PrefetchScalarGridSpec(
            num_scalar_prefetch=1, grid=(R // tr,),
            in_specs=[pl.BlockSpec((None, tr, C), lambda i, s: (layer, i, 0))] + _dep_spec(dep),
            out_specs=pl.BlockSpec((None, tr, C), lambda i, s: (s[0], i, 0))),
        out_shape=jax.ShapeDtypeStruct((N_CHIPS, R, C), dtype),
        compiler_params=_cp("arbitrary"),
    )(where, w3, *_dep_arg(dep))


def _pair_sum(dw, theirs, where, *, name):
    G, rh, C = theirs.shape
    tr = _row_tile(rh)
    nr = rh // tr

    def body(s_ref, a_ref, b_ref, o_ref):
        o_ref[...] = (a_ref[...].astype(F32) + b_ref[...].astype(F32)).astype(BF)

    mine = pl.BlockSpec((None, tr, C), lambda g, i, s: (g, s[1] * nr + i, 0))
    spec = pl.BlockSpec((None, tr, C), lambda g, i, s: (g, i, 0))
    return pl.pallas_call(
        body, name=name,
        grid_spec=pltpu.PrefetchScalarGridSpec(
            num_scalar_prefetch=1, grid=(G, nr), in_specs=[mine, spec], out_specs=spec),
        out_shape=jax.ShapeDtypeStruct((G, rh, C), BF),
        compiler_params=_cp("arbitrary", "arbitrary"),
    )(where, dw, theirs)


def _chip_sum(landed, parts, where, total_rows, row_off, prev, *, name):
    G, rh, C = landed.shape
    tr = _row_tile(rh)
    nr = rh // tr
    base = row_off // tr

    def body(s_ref, l_ref, p_ref, *rest):
        o_ref = rest[-1]
        for j in range(G):
            def own(j=j):
                v = p_ref[...].astype(F32)
                o_ref[...] = v if j == 0 else o_ref[...] + v

            def other(j=j):
                v = l_ref[j].astype(F32)
                o_ref[...] = v if j == 0 else o_ref[...] + v

            pl.when(s_ref[0] == j)(own)
            pl.when(s_ref[0] != j)(other)

    in_specs = [pl.BlockSpec((G, tr, C), lambda i, s: (0, i, 0)),
                pl.BlockSpec((None, tr, C), lambda i, s: (s[0], i, 0))]
    args = [where, landed, parts]
    aliases = {}
    if prev is not None:
        in_specs.append(ANY)
        args.append(prev)
        aliases = {3: 0}
    return pl.pallas_call(
        body, name=name,
        grid_spec=pltpu.PrefetchScalarGridSpec(
            num_scalar_prefetch=1, grid=(nr,),
            in_specs=in_specs,
            out_specs=pl.BlockSpec((tr, C), lambda i, s: (base + s[1] * nr + i, 0))),
        out_shape=jax.ShapeDtypeStruct((total_rows, C), F32),
        input_output_aliases=aliases,
        compiler_params=_cp("arbitrary"),
    )(*args)


def _adamw(w, g, m, v, *, name, emit_grad=False):
    R, C = w.shape
    tr = _row_tile(R, cap=max(16, ADAMW_BLOCK_BYTES // (4 * C)))
    n_out = 4 if emit_grad else 3

    def body(w_ref, g_ref, m_ref, v_ref, d_ref, mo_ref, vo_ref, *go_ref):
        gv = g_ref[...]
        if emit_grad:
            go_ref[0][...] = gv
        mn = ADAM_B1 * m_ref[...] + (1.0 - ADAM_B1) * gv
        vn = ADAM_B2 * v_ref[...] + (1.0 - ADAM_B2) * jnp.square(gv)
        m_hat = mn / (1.0 - ADAM_B1 ** ADAM_STEP)
        v_hat = vn / (1.0 - ADAM_B2 ** ADAM_STEP)
        d_ref[...] = -ADAM_LR * (m_hat / (jnp.sqrt(v_hat) + ADAM_EPS) + ADAM_WD * w_ref[...])
        mo_ref[...] = mn
        vo_ref[...] = vn

    spec = pl.BlockSpec((tr, C), lambda i: (i, 0))
    shp = jax.ShapeDtypeStruct((R, C), F32)
    return pl.pallas_call(
        body, name=name, grid=(R // tr,), in_specs=[spec] * 4, out_specs=[spec] * n_out,
        out_shape=[shp] * n_out, compiler_params=_cp("parallel"),
    )(w, g, m, v)


def _place():
    x = lax.axis_index("x")
    y = lax.axis_index("y")
    c = lax.axis_index("c")
    chips = [(1 - x, y), (x, 1 - y), (1 - x, 1 - y)]
    return x, y, c, chips


def _chunk_rows(rows, row_bytes, align):
    if rows <= align:
        return rows
    cands = [r for r in range(align, rows + 1, align) if rows % r == 0]
    fit = [r for r in cands if r * row_bytes <= DMA_CHUNK_BYTES]
    return max(fit) if fit else min(cands)


def _row_align(dtype):
    return 8 * (4 // jnp.dtype(dtype).itemsize)


def _start_chunks(make, rows, rc):
    for r0 in range(0, rows, rc):
        make(r0, rc).start()


def _piece_rows(ref, piece, j, h, r0=0, n=None):
    _, lead, off, rows = piece
    rh = rows // 2
    n = rh if n is None else n
    if lead is not None:
        return ref.at[lead, j, pl.ds(h * rh + r0, n)]
    return ref.at[j, pl.ds(off + h * rh + r0, n)]


def _piece_chunk(arr, piece):
    rh = piece[3] // 2
    return rh, _chunk_rows(rh, arr.shape[-1] * arr.dtype.itemsize, _row_align(arr.dtype))


def _ag_start(arrays, taps, groups, *, name):
    na = len(arrays)
    ng = len(groups)
    nt = 0 if taps is None else 1
    n_sem = [3 * len(grp) + (3 if nt and g == 0 else 0) for g, grp in enumerate(groups)]

    def body(*refs):
        ins = refs[:na]
        taps_ref = refs[na] if nt else None
        sems = refs[na + nt:na + nt + 2 * ng]
        token = refs[-1]
        token[...] = jnp.zeros_like(token)
        x, y, c, chips = _place()
        myj = 2 * x + y
        for g, grp in enumerate(groups):
            ssem, rsem = sems[2 * g], sems[2 * g + 1]
            for idx, piece in enumerate(grp):
                ref = ins[piece[0]]
                rh, rc = _piece_chunk(arrays[piece[0]], piece)
                for k, (px, py) in enumerate(chips):
                    def send(r0, n, ref=ref, piece=piece, idx=idx, k=k, px=px, py=py, ssem=ssem, rsem=rsem):
                        part = _piece_rows(ref, piece, myj, c, r0, n)
                        return pltpu.make_async_remote_copy(
                            src_ref=part, dst_ref=part, send_sem=ssem.at[3 * idx + k], recv_sem=rsem.at[3 * idx + k],
                            device_id=(px, py, c), device_id_type=MESH)
                    _start_chunks(send, rh, rc)
            if nt and g == 0:
                for k, (px, py) in enumerate(chips):
                    pltpu.make_async_remote_copy(
                        src_ref=taps_ref.at[myj], dst_ref=taps_ref.at[myj],
                        send_sem=ssem.at[3 * len(grp) + k], recv_sem=rsem.at[3 * len(grp) + k],
                        device_id=(px, py, c), device_id_type=MESH).start()

    sem_shapes = []
    for n in n_sem:
        sem_shapes += [pltpu.SemaphoreType.DMA((n,)), pltpu.SemaphoreType.DMA((n,))]
    ops = list(arrays) + ([taps] if nt else [])
    bufs = [pltpu.HBM(a.shape, a.dtype) for a in ops]
    outs = pl.pallas_call(
        body, name=name,
        out_shape=(*sem_shapes, *bufs, jax.ShapeDtypeStruct((8, LANES), F32)),
        in_specs=[HBM] * (na + nt),
        out_specs=(*([SEM] * (2 * ng)), *([HBM] * (na + nt)), pl.BlockSpec(memory_space=pltpu.VMEM)),
        input_output_aliases={i: 2 * ng + i for i in range(na + nt)},
        compiler_params=pltpu.CompilerParams(has_side_effects=EFFECT),
    )(*[_in_hbm(a) for a in ops])
    sems = [(outs[2 * g], outs[2 * g + 1]) for g in range(ng)]
    return sems, list(outs[2 * ng:2 * ng + na]), (outs[2 * ng + na] if nt else None), outs[-1]


def _ag_wait(sems, vals, taps, group, after, *, name):
    nv = len(vals)
    extra = ([taps] if taps is not None else [])
    nb = nv + len(extra)

    def body(*refs):
        bufs = refs[:nb]
        ssem, rsem = refs[nb], refs[nb + 1]
        x, y, c, chips = _place()
        for idx, piece in enumerate(group):
            for k, (px, py) in enumerate(chips):
                got = _piece_rows(bufs[piece[0]], piece, 2 * px + py, c)
                cp = pltpu.make_async_remote_copy(
                    src_ref=got, dst_ref=got, send_sem=ssem.at[3 * idx + k], recv_sem=rsem.at[3 * idx + k],
                    device_id=(px, py, c), device_id_type=MESH)
                cp.wait_send()
                cp.wait_recv()
        if taps is not None:
            for k, (px, py) in enumerate(chips):
                got = bufs[nv].at[2 * px + py]
                cp = pltpu.make_async_remote_copy(
                    src_ref=got, dst_ref=got, send_sem=ssem.at[3 * len(group) + k],
                    recv_sem=rsem.at[3 * len(group) + k], device_id=(px, py, c), device_id_type=MESH)
                cp.wait_send()
                cp.wait_recv()

    ops = list(vals) + extra
    shapes = [pltpu.HBM(a.shape, a.dtype) for a in ops]
    outs = pl.pallas_call(
        body, name=name,
        out_shape=tuple(shapes),
        in_specs=[HBM] * nb + [SEM, SEM] + _dep_spec(after),
        out_specs=[HBM] * nb,
        input_output_aliases={i: i for i in range(nb)},
        compiler_params=pltpu.CompilerParams(has_side_effects=EFFECT),
    )(*ops, sems[0], sems[1], *_dep_arg(after))
    return list(outs[:nv]), (outs[nv] if taps is not None else None)


def _ag_forward(vals, group, *, name):
    nv = len(vals)
    npc = len(group)

    def body(*refs):
        bufs = refs[nv:2 * nv]
        fsem, gsem = refs[2 * nv:]
        x, y, c, chips = _place()
        sib = (x, y, 1 - c)
        sent = []
        for idx, piece in enumerate(group):
            rh, rc = _piece_chunk(vals[piece[0]], piece)
            for k, (px, py) in enumerate(chips):
                def fwd(r0, n, piece=piece, idx=idx, k=k, pj=2 * px + py):
                    part = _piece_rows(bufs[piece[0]], piece, pj, c, r0, n)
                    return pltpu.make_async_remote_copy(
                        src_ref=part, dst_ref=part, send_sem=fsem.at[3 * idx + k], recv_sem=gsem.at[3 * idx + k],
                        device_id=sib, device_id_type=MESH)
                _start_chunks(fwd, rh, rc)
                sent.append(fwd(0, rh))
        for idx, piece in enumerate(group):
            for k, (px, py) in enumerate(chips):
                theirs = _piece_rows(bufs[piece[0]], piece, 2 * px + py, 1 - c)
                pltpu.make_async_remote_copy(
                    src_ref=theirs, dst_ref=theirs, send_sem=fsem.at[3 * idx + k], recv_sem=gsem.at[3 * idx + k],
                    device_id=sib, device_id_type=MESH).wait_recv()
        for cp in sent:
            cp.wait_send()

    return pl.pallas_call(
        body, name=name,
        in_specs=[ANY] * nv, out_specs=[ANY] * nv,
        out_shape=[jax.ShapeDtypeStruct(v.shape, v.dtype) for v in vals],
        input_output_aliases={i: i for i in range(nv)},
        scratch_shapes=[pltpu.SemaphoreType.DMA((3 * npc,)), pltpu.SemaphoreType.DMA((3 * npc,))],
    )(*vals)


def _sibling_swap(dws, *, name):
    nm = len(dws)
    shapes = [jax.ShapeDtypeStruct((dw.shape[0], dw.shape[1] // 2, dw.shape[2]), dw.dtype) for dw in dws]

    def body(*refs):
        ins = refs[:nm]
        theirs = refs[nm:2 * nm]
        ssem, rsem = refs[2 * nm:]
        x, y, c, _ = _place()
        sib = (x, y, 1 - c)
        cps = []
        for m in range(nm):
            G, rh, cols = shapes[m].shape
            rc = _chunk_rows(rh, cols * shapes[m].dtype.itemsize, _row_align(shapes[m].dtype))
            for j in range(G):
                _start_chunks(lambda r0, n, m=m, j=j, rh=rh: pltpu.make_async_remote_copy(
                    src_ref=ins[m].at[j, pl.ds((1 - c) * rh + r0, n)],
                    dst_ref=theirs[m].at[j, pl.ds(r0, n)], send_sem=ssem.at[m], recv_sem=rsem.at[m],
                    device_id=sib, device_id_type=MESH), rh, rc)
            cps.append(pltpu.make_async_remote_copy(
                src_ref=ins[m].at[:, pl.ds((1 - c) * rh, rh), :], dst_ref=theirs[m],
                send_sem=ssem.at[m], recv_sem=rsem.at[m], device_id=sib, device_id_type=MESH))
        for cp in cps:
            cp.wait()

    return pl.pallas_call(
        body, name=name,
        in_specs=[ANY] * nm, out_specs=[ANY] * nm, out_shape=shapes,
        scratch_shapes=[pltpu.SemaphoreType.DMA((nm,)), pltpu.SemaphoreType.DMA((nm,))],
    )(*dws)


HBM = pl.BlockSpec(memory_space=pltpu.HBM)
SEM = pl.BlockSpec(memory_space=pltpu.SEMAPHORE)
EFFECT = pltpu.SideEffectType.DATAFLOW_SIDE_EFFECTING


def _in_hbm(a):
    return pltpu.with_memory_space_constraint(a, pltpu.HBM)


def _swap_start(dws, *, name):
    nm = len(dws)
    lands = [(dw.shape[0], dw.shape[1] // 2, dw.shape[2]) for dw in dws]

    def body(*refs):
        ins = refs[:nm]
        lnd = refs[nm:2 * nm]
        ssem, rsem = refs[2 * nm:2 * nm + 2]
        token = refs[-1]
        x, y, c, _ = _place()
        sib = (x, y, 1 - c)
        for m in range(nm):
            G, rh, cols = lands[m]
            rc = _chunk_rows(rh, cols * dws[m].dtype.itemsize, _row_align(dws[m].dtype))
            for j in range(G):
                _start_chunks(lambda r0, n, m=m, j=j, rh=rh: pltpu.make_async_remote_copy(
                    src_ref=ins[m].at[j, pl.ds((1 - c) * rh + r0, n)], dst_ref=lnd[m].at[j, pl.ds(r0, n)],
                    send_sem=ssem.at[m], recv_sem=rsem.at[m], device_id=sib, device_id_type=MESH), rh, rc)
        token[...] = jnp.zeros_like(token)

    src = [pltpu.HBM(dw.shape, dw.dtype) for dw in dws]
    dst = [pltpu.HBM(s, dw.dtype) for s, dw in zip(lands, dws)]
    outs = pl.pallas_call(
        body, name=name,
        out_shape=(pltpu.SemaphoreType.DMA((nm,)), pltpu.SemaphoreType.DMA((nm,)), *src, *dst,
                   jax.ShapeDtypeStruct((8, LANES), F32)),
        in_specs=[HBM] * (2 * nm),
        out_specs=(SEM, SEM, *([HBM] * (2 * nm)), pl.BlockSpec(memory_space=pltpu.VMEM)),
        input_output_aliases={i: 2 + i for i in range(2 * nm)},
        compiler_params=pltpu.CompilerParams(has_side_effects=EFFECT),
    )(*[_in_hbm(dw) for dw in dws], *[_in_hbm(lax.empty(s, dw.dtype)) for s, dw in zip(lands, dws)])
    return (outs[0], outs[1], list(outs[2:2 + nm]), list(outs[2 + nm:2 + 2 * nm])), outs[-1]


def _swap_wait(handle, after, *, name):
    ssem_in, rsem_in, dws, lands = handle
    nm = len(dws)

    def body(*refs):
        ins = refs[:nm]
        lnd = refs[nm:2 * nm]
        ssem, rsem = refs[2 * nm:2 * nm + 2]
        x, y, c, _ = _place()
        for m in range(nm):
            rh = lands[m].shape[1]
            cp = pltpu.make_async_remote_copy(
                src_ref=ins[m].at[:, pl.ds((1 - c) * rh, rh), :], dst_ref=lnd[m],
                send_sem=ssem.at[m], recv_sem=rsem.at[m], device_id=(x, y, 1 - c), device_id_type=MESH)
            cp.wait_send()
            cp.wait_recv()

    bufs = [pltpu.HBM(a.shape, a.dtype) for a in list(dws) + list(lands)]
    outs = pl.pallas_call(
        body, name=name,
        out_shape=tuple(bufs),
        in_specs=[HBM] * (2 * nm) + [SEM, SEM, ANY],
        out_specs=[HBM] * (2 * nm),
        input_output_aliases={i: i for i in range(2 * nm)},
        compiler_params=pltpu.CompilerParams(has_side_effects=EFFECT),
    )(*dws, *lands, ssem_in, rsem_in, after)
    return list(outs[:nm]), list(outs[nm:])


def _exchange_start(parts, *, name):
    nm = len(parts)

    def body(*refs):
        ins = refs[:nm]
        lands = refs[nm:2 * nm]
        ssem, rsem = refs[2 * nm:2 * nm + 2]
        token = refs[-1]
        x, y, c, chips = _place()
        myj = 2 * x + y
        for m in range(nm):
            _, rh, cols = parts[m].shape
            rc = _chunk_rows(rh, cols * parts[m].dtype.itemsize, _row_align(parts[m].dtype))
            for k, (px, py) in enumerate(chips):
                _start_chunks(lambda r0, n, m=m, k=k, px=px, py=py: pltpu.make_async_remote_copy(
                    src_ref=ins[m].at[2 * px + py, pl.ds(r0, n)], dst_ref=lands[m].at[myj, pl.ds(r0, n)],
                    send_sem=ssem.at[3 * m + k], recv_sem=rsem.at[3 * m + k],
                    device_id=(px, py, c), device_id_type=MESH), rh, rc)
        token[...] = jnp.zeros_like(token)

    bufs = [pltpu.HBM(p.shape, p.dtype) for p in parts]
    outs = pl.pallas_call(
        body, name=name,
        out_shape=(pltpu.SemaphoreType.DMA((3 * nm,)), pltpu.SemaphoreType.DMA((3 * nm,)), *bufs, *bufs,
                   jax.ShapeDtypeStruct((8, LANES), F32)),
        in_specs=[HBM] * (2 * nm),
        out_specs=(SEM, SEM, *([HBM] * (2 * nm)), pl.BlockSpec(memory_space=pltpu.VMEM)),
        input_output_aliases={i: 2 + i for i in range(2 * nm)},
        compiler_params=pltpu.CompilerParams(has_side_effects=EFFECT),
    )(*[_in_hbm(p) for p in parts], *[_in_hbm(lax.empty(p.shape, p.dtype)) for p in parts])
    return (outs[0], outs[1], list(outs[2:2 + nm]), list(outs[2 + nm:2 + 2 * nm])), outs[-1]


def _exchange_wait(handle, after, *, name):
    ssem_in, rsem_in, parts, lands = handle
    nm = len(parts)

    def body(*refs):
        ins = refs[:nm]
        lnd = refs[nm:2 * nm]
        ssem, rsem = refs[2 * nm:2 * nm + 2]
        x, y, c, chips = _place()
        for m in range(nm):
            for k, (px, py) in enumerate(chips):
                pj = 2 * px + py
                cp = pltpu.make_async_remote_copy(
                    src_ref=ins[m].at[pj], dst_ref=lnd[m].at[pj],
                    send_sem=ssem.at[3 * m + k], recv_sem=rsem.at[3 * m + k],
                    device_id=(px, py, c), device_id_type=MESH)
                cp.wait_send()
                cp.wait_recv()

    bufs = [pltpu.HBM(p.shape, p.dtype) for p in parts]
    outs = pl.pallas_call(
        body, name=name,
        out_shape=(*bufs, *bufs),
        in_specs=[HBM] * (2 * nm) + [SEM, SEM, ANY],
        out_specs=[HBM] * (2 * nm),
        input_output_aliases={i: i for i in range(2 * nm)},
        compiler_params=pltpu.CompilerParams(has_side_effects=EFFECT),
    )(*parts, *lands, ssem_in, rsem_in, after)
    return list(outs[nm:]), list(outs[:nm])


def _sibling_join(grads, regions, *, name):
    nm = len(grads)
    nr = len(regions)
    shapes = [jax.ShapeDtypeStruct(g.shape, g.dtype) for g in grads]

    def body(*refs):
        outs = refs[nm:2 * nm]
        ssem, rsem = refs[2 * nm:]
        x, y, c, _ = _place()
        sib = (x, y, 1 - c)
        cps = []
        for i, (m, off, rows) in enumerate(regions):
            rh, cols = rows // 2, grads[m].shape[1]
            rc = _chunk_rows(rh, cols * grads[m].dtype.itemsize, _row_align(grads[m].dtype))

            def send(r0, n, i=i, m=m, off=off, rh=rh):
                part = outs[m].at[pl.ds(off + c * rh + r0, n)]
                return pltpu.make_async_remote_copy(
                    src_ref=part, dst_ref=part, send_sem=ssem.at[i], recv_sem=rsem.at[i],
                    device_id=sib, device_id_type=MESH)
            _start_chunks(send, rh, rc)
            cps.append(send(0, rh))
        for i, (m, off, rows) in enumerate(regions):
            rh = rows // 2
            cps[i].wait_send()
            got = outs[m].at[pl.ds(off + (1 - c) * rh, rh)]
            pltpu.make_async_remote_copy(
                src_ref=got, dst_ref=got, send_sem=ssem.at[i], recv_sem=rsem.at[i],
                device_id=sib, device_id_type=MESH).wait_recv()

    return pl.pallas_call(
        body, name=name,
        in_specs=[ANY] * nm, out_specs=[ANY] * nm, out_shape=shapes,
        input_output_aliases={i: i for i in range(nm)},
        scratch_shapes=[pltpu.SemaphoreType.DMA((nr,)), pltpu.SemaphoreType.DMA((nr,))],
    )(*grads)


def _all_reduce_small(pack):
    R, C = pack.shape

    def body(in_ref, out_ref, slots, ssem, rsem):
        x, y, c, _ = _place()
        me = 4 * x + 2 * y + c
        slots[me] = in_ref[...]
        cps = []
        for k in range(1, N_DEV):
            dx, dy, dc = (k >> 2) & 1, (k >> 1) & 1, k & 1
            peer = (x ^ dx, y ^ dy, c ^ dc)
            cp = pltpu.make_async_remote_copy(
                src_ref=in_ref, dst_ref=slots.at[me], send_sem=ssem.at[k], recv_sem=rsem.at[k],
                device_id=peer, device_id_type=MESH)
            cp.start()
            cps.append(cp)
        for k in range(1, N_DEV):
            dx, dy, dc = (k >> 2) & 1, (k >> 1) & 1, k & 1
            got = slots.at[4 * (x ^ dx) + 2 * (y ^ dy) + (c ^ dc)]
            pltpu.make_async_remote_copy(
                src_ref=got, dst_ref=got, send_sem=ssem.at[k], recv_sem=rsem.at[k],
                device_id=(x ^ dx, y ^ dy, c ^ dc), device_id_type=MESH).wait_recv()
        for cp in cps:
            cp.wait_send()
        acc = slots[0]
        for s in range(1, N_DEV):
            acc = acc + slots[s]
        out_ref[...] = acc

    return pl.pallas_call(
        body, name="ar_small",
        in_specs=[pl.BlockSpec(memory_space=pltpu.VMEM)],
        out_specs=pl.BlockSpec(memory_space=pltpu.VMEM),
        out_shape=jax.ShapeDtypeStruct((R, C), F32),
        scratch_shapes=[pltpu.VMEM((N_DEV, R, C), F32),
                        pltpu.SemaphoreType.DMA((N_DEV,)), pltpu.SemaphoreType.DMA((N_DEV,))],
    )(pack)


def kernel(x, positions, mix_norm_pre, mix_norm_post, ffn_norm_pre, ffn_norm_post, ffn_w_gate_up, ffn_w_down, conv_w_in, conv_w, conv_w_out, kv_norm, w_kv, w_q, w_o, loss_target, m_mix_norm_pre, m_mix_norm_post, m_ffn_norm_pre, m_ffn_norm_post, m_ffn_w_gate_up, m_ffn_w_down, m_conv_w_in, m_conv_w, m_conv_w_out, m_kv_norm, m_w_kv, m_w_q, m_w_o, v_mix_norm_pre, v_mix_norm_post, v_ffn_norm_pre, v_ffn_norm_post, v_ffn_w_gate_up, v_ffn_w_down, v_conv_w_in, v_conv_w, v_conv_w_out, v_kv_norm, v_w_kv, v_w_q, v_w_o):
    T, D = x.shape[1], x.shape[2]
    L = ffn_w_gate_up.shape[0]
    n_gu = ffn_w_gate_up.shape[2]
    f_sh = ffn_w_down.shape[1]
    F = N_CHIPS * f_sh
    x0 = x[0]
    tgt = loss_target[0]

    half = HEAD_DIM // 2
    inv_freq = ROPE_THETA ** (-jnp.arange(half, dtype=F32) / half)
    ang = positions[0].astype(F32)[:, None] * inv_freq
    cosv, sinv = jnp.cos(ang), jnp.sin(ang)
    cos2 = jnp.tile(cosv, (1, LANES // half))
    ss2 = jnp.tile(jnp.concatenate([-sinv, sinv], axis=1), (1, LANES // HEAD_DIM))

    def as2d(a):
        return a.reshape(-1, a.shape[-1])

    big = [ffn_w_gate_up, ffn_w_down, conv_w_in, conv_w_out, w_kv, w_q, w_o]
    big_m = [m_ffn_w_gate_up, m_ffn_w_down, m_conv_w_in, m_conv_w_out, m_w_kv, m_w_q, m_w_o]
    big_v = [v_ffn_w_gate_up, v_ffn_w_down, v_conv_w_in, v_conv_w_out, v_w_kv, v_w_q, v_w_o]
    chip = 2 * lax.axis_index("x") + lax.axis_index("y")
    where = jnp.stack([chip, lax.axis_index("c")]).astype(jnp.int32)
    tc = conv_w.shape[2]
    cw_pad = jnp.concatenate([conv_w[0], jnp.zeros((8 - conv_w.shape[1], tc), F32)], axis=0)

    GU0, GU1, WD0, WD1, WCI, WCO, WKV, WQ, WO = range(9)
    shards = [(ffn_w_gate_up, 0), (ffn_w_gate_up, 1), (ffn_w_down, 0), (ffn_w_down, 1), (conv_w_in, 0),
              (conv_w_out, 0), (w_kv[None], 0), (w_q, 0), (w_o, 0)]
    ag_groups = [
        [(WCI, None, 0, D), (WCO, None, 0, D // N_CHIPS)],
        [(GU0, None, 0, D), (WD0, None, 0, f_sh)],
        [(WKV, None, 0, D), (WQ, None, 0, D)],
        [(WO, None, 0, D // N_CHIPS), (GU1, None, 0, D), (WD1, None, 0, f_sh)],
    ]

    def localised(group, idxs):
        return [(idxs.index(p[0]),) + p[1:] for p in group]

    cur = [None] * len(shards)
    first = [WCI, WCO]
    sems0, vals, taps, token = _ag_start(
        [_cast_place(*shards[i], where, BF, name=f"place{i}") for i in first],
        _cast_place(cw_pad[None], 0, where, F32, name="place_taps"), [localised(ag_groups[0], first)],
        name="ag_start0")
    for i, v in zip(first, vals):
        cur[i] = v
    rest = [i for i in range(len(shards)) if i not in first]
    sems1, vals, _, _ = _ag_start(
        [_cast_place(*shards[i], where, BF, name=f"place{i}", dep=token) for i in rest], None,
        [localised(g, rest) for g in ag_groups[1:]], name="ag_start1")
    for i, v in zip(rest, vals):
        cur[i] = v
    ag_sems = sems0 + sems1

    def gather_group(g, after):
        nonlocal taps
        idxs = sorted({p[0] for p in ag_groups[g]})
        local = localised(ag_groups[g], idxs)
        vals, landed_taps = _ag_wait(ag_sems[g], [cur[i] for i in idxs], taps if g == 0 else None, local, after,
                                     name=f"ag_wait{g}")
        if g == 0:
            taps = landed_taps
        vals = _ag_forward(vals, local, name=f"ag_forward{g}")
        for i, v in zip(idxs, vals):
            cur[i] = v

    def row(a, i):
        return a[i:i + 1]

    gather_group(0, None)
    wci, wco, cw = cur[WCI], cur[WCO].reshape(1, D, D), taps
    z, hn_m0 = _norm_matmul(x0, row(mix_norm_pre, 0), wci, cos2, ss2, name="f0_conv_in",
                            rope_shards=0, scale=1.0, out_dtype=BF)
    vmix = _conv_fwd(z, cw, name="f0_conv")
    y0, h1 = _matmul_postnorm(vmix, wco, 0, row(mix_norm_post, 0), x0, name="f0_conv_out")
    gather_group(1, h1)
    wgu0, wd0 = cur[GU0], cur[WD0].reshape(1, F, D)
    g0, u0, a0, hn_f0 = _norm_swiglu(h1, row(ffn_norm_pre, 0), wgu0, 0, name="f0_gate_up")
    f0, h2 = _matmul_postnorm(a0, wd0, 0, row(ffn_norm_post, 0), h1, name="f0_down")

    gather_group(2, h2)
    wkv, wq = cur[WKV], cur[WQ]
    kv_all, hn_kv = _norm_matmul(h2, kv_norm.reshape(1, D), wkv, cos2, ss2, name="f1_kv",
                                 rope_shards=N_CHIPS // 2, scale=1.0, out_dtype=BF)
    q_all, hn_m1 = _norm_matmul(h2, row(mix_norm_pre, 1), wq, cos2, ss2, name="f1_q",
                                rope_shards=N_CHIPS, scale=HEAD_DIM ** -0.5, out_dtype=BF)
    o_att, lse = _attn_fwd(q_all, kv_all, name="f1_attn")
    gather_group(3, o_att)
    wgu1, wd1, wo = cur[GU1], cur[WD1].reshape(1, F, D), cur[WO].reshape(1, D, D)
    y1, h3 = _matmul_postnorm(o_att, wo, 0, row(mix_norm_post, 1), h2, name="f1_attn_out")
    g1, u1, a1, hn_f1 = _norm_swiglu(h3, row(ffn_norm_pre, 1), wgu1, 0, name="f1_gate_up")
    f1, dh4, sq = _matmul_postnorm_loss(a1, wd1, 0, row(ffn_norm_post, 1), h3, tgt, name="f1_down_loss")
    loss_part = 0.5 * sq[0, 0] / D

    gu_shape = (N_CHIPS, D, n_gu)
    in_chips = lambda a: a.reshape(N_CHIPS, -1, a.shape[-1])

    def scatter_start(dws, tag):
        theirs = _sibling_swap(dws, name=f"rs_swap_{tag}")
        parts = [_pair_sum(dw, t, where, name=f"rs_pair_sum_{tag}{i}") for i, (dw, t) in enumerate(zip(dws, theirs))]
        return _exchange_start(parts, name=f"rs_exchange_start_{tag}")

    def scatter_go(swap, after, tag):
        dws, theirs = _swap_wait(swap, after, name=f"rs_swap_wait_{tag}")
        parts = [_pair_sum(dw, t, where, name=f"rs_pair_sum_{tag}{i}") for i, (dw, t) in enumerate(zip(dws, theirs))]
        return _exchange_start(parts, name=f"rs_exchange_start_{tag}")

    dyf1, dg1, du1, d_ffn_post1 = _postnorm_bwd_swiglu(dh4, f1, row(ffn_norm_post, 1), wd1, 0, g1, u1,
                                                       name="b1_down")
    dwd1 = _grad_matmul(a1, dyf1, (2, F // 2, D), F // 2, D, lambda i, j: (i, 0, 0), None, name="b1_dw_down")
    dwgu1 = _grad_matmul(hn_f1, dg1, gu_shape, D, n_gu, lambda i, j: (j, 0, 0), None, name="b1_dw_gate")
    dwgu1 = _grad_matmul(hn_f1, du1, gu_shape, D, n_gu, lambda i, j: (j + 2, 0, 0), dwgu1, name="b1_dw_up")
    dh3, d_ffn_pre1 = _matmul_prenorm_bwd((dg1, du1), wgu1, 0, h3, row(ffn_norm_pre, 1), dh4, name="b1_gate_up")

    dy1, do, d_mix_post1 = _postnorm_bwd_matmul(dh3, y1, row(mix_norm_post, 1), wo, 0, name="b1_attn_out",
                                                da_dtype=F32)
    dwo = _grad_matmul(o_att, dy1, (1, D, D), D, D, lambda i, j: (0, 0, 0), None, name="b1_dw_o")
    swap_a, token = _swap_start([dwgu1, in_chips(dwd1), in_chips(dwo)], name="rs_swap_start_a")
    prev = None
    for gi, (window, dil) in enumerate(BRANCHES):
        prev = _attn_bwd(q_all, kv_all, do, o_att, lse, cos2, ss2, gi, dil, prev, name=f"b1_attn{gi}", dep=token)
        token = None
        if gi == 0:
            rs_a, token = scatter_go(swap_a, prev[0], "a")
    dq_all, dk_all, dv_all = prev
    n_q = wq.shape[2]
    n_kv = wkv.shape[2]
    dwq = _grad_matmul(hn_m1, dq_all, (N_CHIPS, D, n_q), D, n_q, lambda i, j: (j, 0, 0), None, name="b1_dw_q")
    dwkv = _grad_matmul(hn_kv, dk_all, (N_CHIPS, D, n_kv), D, n_kv, lambda i, j: (j, 0, 0), None, name="b1_dw_k")
    dwkv = _grad_matmul(hn_kv, dv_all, (N_CHIPS, D, n_kv), D, n_kv, lambda i, j: (j + 2, 0, 0), dwkv, name="b1_dw_v")
    dh2, d_mix_pre1 = _matmul_prenorm_bwd((dq_all,), wq, 0, h2, row(mix_norm_pre, 1), dh3, name="b1_q")
    dh2, d_kv_norm = _matmul_prenorm_bwd((dk_all, dv_all), wkv, 0, h2, kv_norm.reshape(1, D), dh2, name="b1_kv")
    swap_b, token = _swap_start([dwkv, dwq], name="rs_swap_start_b")

    dyf0, dg0, du0, d_ffn_post0 = _postnorm_bwd_swiglu(dh2, f0, row(ffn_norm_post, 0), wd0, 0, g0, u0,
                                                       name="b0_down", dep=token)
    rs_b, token = scatter_go(swap_b, dyf0, "b")
    dwd0 = _grad_matmul(a0, dyf0, (2, F // 2, D), F // 2, D, lambda i, j: (i, 0, 0), None, name="b0_dw_down",
                        dep=token)
    dwgu0 = _grad_matmul(hn_f0, dg0, gu_shape, D, n_gu, lambda i, j: (j, 0, 0), None, name="b0_dw_gate")
    dwgu0 = _grad_matmul(hn_f0, du0, gu_shape, D, n_gu, lambda i, j: (j + 2, 0, 0), dwgu0, name="b0_dw_up")
    dh1, d_ffn_pre0 = _matmul_prenorm_bwd((dg0, du0), wgu0, 0, h1, row(ffn_norm_pre, 0), dh2, name="b0_gate_up")
    swap_c, token = _swap_start([dwgu0, in_chips(dwd0)], name="rs_swap_start_c")

    dy0, dvmix, d_mix_post0 = _postnorm_bwd_matmul(dh1, y0, row(mix_norm_post, 0), wco, 0, name="b0_conv_out",
                                                   da_dtype=BF, dep=token)
    rs_c, token = scatter_go(swap_c, dy0, "c")
    dwco = _grad_matmul(vmix, dy0, (1, D, D), D, D, lambda i, j: (0, 0, 0), None, name="b0_dw_conv_out",
                        dep=token)
    dz, dcw = _conv_bwd(z, cw, dvmix, name="b0_conv")
    n_ci = wci.shape[2]
    dwci = _grad_matmul(hn_m0, dz, (N_CHIPS, D, n_ci), D, n_ci, lambda i, j: (j, 0, 0), None, name="b0_dw_conv_in")
    dx, d_mix_pre0 = _matmul_prenorm_bwd((dz,), wci, 0, x0, row(mix_norm_pre, 0), dh1, name="b0_conv_in")

    pack = jnp.concatenate([
        d_mix_pre0, d_mix_pre1, d_mix_post0, d_mix_post1, d_ffn_pre0, d_ffn_pre1, d_ffn_post0, d_ffn_post1,
        d_kv_norm, dcw[0:3], jnp.full((1, D), loss_part, F32),
        jnp.zeros((SMALL_ROWS - 13, D), F32)], axis=0)
    red = _all_reduce_small(pack)
    loss = red[12, 0]
    myj = 2 * lax.axis_index("x") + lax.axis_index("y")
    g_conv_w = lax.dynamic_slice(red, (9, myj * tc), (3, tc))

    zeros7 = jnp.zeros((SMALL_ROWS - 9, D), F32)
    w_small = jnp.concatenate([mix_norm_pre, mix_norm_post, ffn_norm_pre, ffn_norm_post, kv_norm.reshape(1, D), zeros7], axis=0)
    m_small = jnp.concatenate([m_mix_norm_pre, m_mix_norm_post, m_ffn_norm_pre, m_ffn_norm_post, m_kv_norm.reshape(1, D), zeros7], axis=0)
    v_small = jnp.concatenate([v_mix_norm_pre, v_mix_norm_post, v_ffn_norm_pre, v_ffn_norm_post, v_kv_norm.reshape(1, D), zeros7], axis=0)
    d_small, nm_small, nv_small = _adamw(w_small, red, m_small, v_small, name="adamw_small")

    pad5 = jnp.zeros((5, tc), F32)
    d_cw, nm_cw, nv_cw = _adamw(cw_pad, jnp.concatenate([g_conv_w, pad5], axis=0),
                                jnp.concatenate([m_conv_w[0], pad5], axis=0),
                                jnp.concatenate([v_conv_w[0], pad5], axis=0), name="adamw_conv_w")

    rs_d, _ = scatter_start([dwci, in_chips(dwco)], "d")

    pieces = {"a": [(0, D), (1, f_sh), (6, 0)], "b": [(4, 0), (5, 0)], "c": [(0, 0), (1, 0)], "d": [(2, 0), (3, 0)]}
    grads2d = [None] * len(big)
    big_out = [None] * len(big)

    def finish(groups, after, tag):
        regions, idxs = [], []
        for gtag, handle in groups:
            landed, parts = _exchange_wait(handle, after, name=f"rs_exchange_wait_{gtag}")
            for i, (l, p, (wi, off)) in enumerate(zip(landed, parts, pieces[gtag])):
                total = as2d(big[wi]).shape[0]
                grads2d[wi] = _chip_sum(l, p, where, total, off, grads2d[wi], name=f"rs_chip_sum_{gtag}{i}")
                if wi not in idxs:
                    idxs.append(wi)
                regions.append((idxs.index(wi), off, 2 * l.shape[1]))
        joined = _sibling_join([grads2d[wi] for wi in idxs], regions, name=f"rs_sibling_join_{tag}")
        for wi, gr in zip(idxs, joined):
            w = big[wi]
            d_, m_, v_, g_ = _adamw(as2d(w), gr, as2d(big_m[wi]), as2d(big_v[wi]), name=f"adamw{wi}",
                                    emit_grad=True)
            big_out[wi] = (g_.reshape(w.shape), d_.reshape(w.shape), m_.reshape(w.shape), v_.reshape(w.shape))

    finish([("a", rs_a), ("b", rs_b), ("c", rs_c)], dx, "abc")
    finish([("d", rs_d)], big_out[0][1], "d")

    def small(a):
        return (a[0:2], a[2:4], a[4:6], a[6:8])

    def assemble(sm, cwv, kind):
        pre, post, fpre, fpost = small(sm)
        b = [t[kind] for t in big_out]
        return [pre, post, fpre, fpost, b[0], b[1], b[2], cwv[0:3].reshape(conv_w.shape), b[3],
                sm[8], b[4], b[5].reshape(w_q.shape), b[6].reshape(w_o.shape)]

    grads = assemble(red, jnp.concatenate([g_conv_w, pad5], axis=0), 0)
    deltas = assemble(d_small, d_cw, 1)
    new_m = assemble(nm_small, nm_cw, 2)
    new_v = assemble(nv_small, nv_cw, 3)
    return (loss, dx.reshape(x.shape), *grads, *deltas, *new_m, *new_v)
```

```python
import functools

import jax
import jax.numpy as jnp
from jax import lax
from jax.experimental import pallas as pl
from jax.experimental.pallas import tpu as pltpu

HEAD_DIM = 64
BAND = 128
BRANCHES = ((128, 1), (512, 4), (2048, 16))
ROPE_THETA = 10000.0
RMS_EPS = 1e-6
NEG_INF = -1e30
ADAM_LR = 0.001
ADAM_B1 = 0.9
ADAM_B2 = 0.999
ADAM_EPS = 1e-08
ADAM_WD = 0.01
ADAM_STEP = 10

N_CHIPS = 4
N_DEV = 8
LANES = 128
MXU_COLS = 256
ROW_BLOCK = 512
ROW_BLOCK_WIDE = 1024
GRAD_CHUNK = 2048
ATTN_FWD_UNROLL = 16
ATTN_BWD_UNROLL = 8
ATTN_BLOCK_ROWS = 2048
VMEM_LIMIT = 56 * 1024 * 1024
SMALL_ROWS = 16
ADAMW_BLOCK_BYTES = 1024 * 1024
DMA_CHUNK_BYTES = 512 * 1024

BF = jnp.bfloat16
F32 = jnp.float32
MESH = pl.DeviceIdType.MESH
ANY = pl.BlockSpec(memory_space=pl.ANY)


def _cp(*sem):
    return pltpu.CompilerParams(dimension_semantics=sem, vmem_limit_bytes=VMEM_LIMIT)


def _rot_half(t, first):
    return jnp.where(first, pltpu.roll(t, 96, 1), pltpu.roll(t, 32, 1))


def _sigmoid(x):
    return pl.reciprocal(1.0 + jnp.exp(-x), approx=True)


def _col_chunks(n):
    return [(c0, min(MXU_COLS, n - c0)) for c0 in range(0, n, MXU_COLS)]


def _first_half_mask(rows):
    lane = lax.broadcasted_iota(jnp.int32, (rows, LANES), 1)
    return (lane % HEAD_DIM) < (HEAD_DIM // 2)


def _normed_rows(j, rows, x_ref, g_ref, xn_ref, xs, last_start, tm):
    @pl.when(j == 0)
    def _():
        xv = x_ref[...]
        r = lax.rsqrt(jnp.mean(xv * xv, axis=-1, keepdims=True) + RMS_EPS)
        xn = (xv * r * g_ref[...]).astype(BF)
        xs[rows, :] = xn
        xn_ref[...] = xn

    @pl.when(j > 0)
    def _():
        xn_ref[...] = xs[pl.ds(last_start, tm), :]


def _norm_matmul(x, gain, wg, cos2, ss2, *, name, rope_shards, scale, out_dtype):
    T, D = x.shape
    n = wg.shape[2]
    tm = min(ROW_BLOCK_WIDE, T)
    ni = T // tm

    def body(x_ref, g_ref, w_ref, cos_ref, ss_ref, y_ref, xn_ref, xs):
        j = pl.program_id(0)
        rows = pl.ds(pl.multiple_of(pl.program_id(1) * tm, tm), tm)
        _normed_rows(j, rows, x_ref, g_ref, xn_ref, xs, (ni - 1) * tm, tm)
        acc = jnp.dot(xs[rows, :], w_ref[...], preferred_element_type=F32)

        def plain():
            y_ref[...] = acc.astype(out_dtype)

        def rope():
            cosv = cos_ref[...]
            ssv = ss_ref[...]
            first = _first_half_mask(tm)
            for ci in range(n // LANES):
                t = acc[:, ci * LANES:(ci + 1) * LANES]
                y = (t * cosv + _rot_half(t, first) * ssv) * scale
                y_ref[:, ci * LANES:(ci + 1) * LANES] = y.astype(out_dtype)

        if rope_shards == 0:
            plain()
        elif rope_shards == N_CHIPS:
            rope()
        else:
            pl.when(j < rope_shards)(rope)
            pl.when(j >= rope_shards)(plain)

    first_pass = lambda j, i: (jnp.where(j == 0, i, ni - 1), 0)
    return pl.pallas_call(
        body, name=name,
        grid=(N_CHIPS, ni),
        in_specs=[
            pl.BlockSpec((tm, D), first_pass),
            pl.BlockSpec((1, D), lambda j, i: (0, 0)),
            pl.BlockSpec((None, D, n), lambda j, i: (j, 0, 0)),
            pl.BlockSpec((tm, LANES), lambda j, i: (i, 0)),
            pl.BlockSpec((tm, LANES), lambda j, i: (i, 0)),
        ],
        out_specs=[
            pl.BlockSpec((tm, n), lambda j, i: (i, j)),
            pl.BlockSpec((tm, D), first_pass),
        ],
        out_shape=[jax.ShapeDtypeStruct((T, N_CHIPS * n), out_dtype),
                   jax.ShapeDtypeStruct((T, D), BF)],
        scratch_shapes=[pltpu.VMEM((T, D), BF)],
        compiler_params=_cp("arbitrary", "arbitrary"),
    )(x, gain, wg, cos2, ss2)


def _norm_swiglu(x, gain, wg, layer, *, name):
    T, D = x.shape
    n = wg.shape[2]
    tm = min(ROW_BLOCK, T)

    ni = T // tm

    def body(x_ref, g_ref, wg_ref, wu_ref, go_ref, uo_ref, ao_ref, xn_ref, xs):
        j = pl.program_id(0)
        rows = pl.ds(pl.multiple_of(pl.program_id(1) * tm, tm), tm)
        _normed_rows(j, rows, x_ref, g_ref, xn_ref, xs, (ni - 1) * tm, tm)
        g = jnp.dot(xs[rows, :], wg_ref[...], preferred_element_type=F32)
        u = jnp.dot(xs[rows, :], wu_ref[...], preferred_element_type=F32)
        go_ref[...] = g.astype(BF)
        uo_ref[...] = u.astype(BF)
        ao_ref[...] = (g * _sigmoid(g) * u).astype(BF)

    half = N_CHIPS // 2
    first_pass = lambda j, i: (jnp.where(j == 0, i, ni - 1), 0)
    act = jax.ShapeDtypeStruct((T, half * n), BF)
    return pl.pallas_call(
        body, name=name,
        grid=(half, ni),
        in_specs=[
            pl.BlockSpec((tm, D), first_pass),
            pl.BlockSpec((1, D), lambda j, i: (0, 0)),
            pl.BlockSpec((None, D, n), lambda j, i: (j, layer, 0)),
            pl.BlockSpec((None, D, n), lambda j, i: (j + half, layer, 0)),
        ],
        out_specs=[
            pl.BlockSpec((tm, n), lambda j, i: (i, j)),
            pl.BlockSpec((tm, n), lambda j, i: (i, j)),
            pl.BlockSpec((tm, n), lambda j, i: (i, j)),
            pl.BlockSpec((tm, D), first_pass),
        ],
        out_shape=[act, act, act, jax.ShapeDtypeStruct((T, D), BF)],
        scratch_shapes=[pltpu.VMEM((T, D), BF)],
        compiler_params=_cp("arbitrary", "arbitrary"),
    )(x, gain, wg, wg)


def _matmul_postnorm(a, w3, widx, gain, h_old, *, name):
    T, K = a.shape
    D = w3.shape[2]
    tm = min(ROW_BLOCK_WIDE, T)

    def body(a_ref, w_ref, g_ref, h_ref, y_ref, hn_ref):
        y = jnp.dot(a_ref[...].astype(BF), w_ref[...], preferred_element_type=F32)
        y_ref[...] = y.astype(BF)
        r = lax.rsqrt(jnp.mean(y * y, axis=-1, keepdims=True) + RMS_EPS)
        hn_ref[...] = h_ref[...] + y * r * g_ref[...]

    return pl.pallas_call(
        body, name=name,
        grid=(T // tm,),
        in_specs=[
            pl.BlockSpec((tm, K), lambda i: (i, 0)),
            pl.BlockSpec((None, K, D), lambda i: (widx, 0, 0)),
            pl.BlockSpec((1, D), lambda i: (0, 0)),
            pl.BlockSpec((tm, D), lambda i: (i, 0)),
        ],
        out_specs=[pl.BlockSpec((tm, D), lambda i: (i, 0)),
                   pl.BlockSpec((tm, D), lambda i: (i, 0))],
        out_shape=[jax.ShapeDtypeStruct((T, D), BF), jax.ShapeDtypeStruct((T, D), F32)],
        compiler_params=_cp("parallel"),
    )(a, w3, gain, h_old)


def _matmul_postnorm_loss(a, w3, widx, gain, h_old, target, *, name):
    T, K = a.shape
    D = w3.shape[2]
    tm = min(ROW_BLOCK, T)

    def body(a_ref, w_ref, g_ref, h_ref, t_ref, y_ref, dh_ref, s_ref):
        @pl.when(pl.program_id(0) == 0)
        def _():
            s_ref[...] = jnp.zeros_like(s_ref)

        y = jnp.dot(a_ref[...].astype(BF), w_ref[...], preferred_element_type=F32)
        y_ref[...] = y.astype(BF)
        r = lax.rsqrt(jnp.mean(y * y, axis=-1, keepdims=True) + RMS_EPS)
        e = (h_ref[...] + y * r * g_ref[...]) - t_ref[...]
        dh_ref[...] = e * (1.0 / D)
        s_ref[...] += jnp.sum(e * e)

    rows = pl.BlockSpec((tm, D), lambda i: (i, 0))
    return pl.pallas_call(
        body, name=name,
        grid=(T // tm,),
        in_specs=[
            pl.BlockSpec((tm, K), lambda i: (i, 0)),
            pl.BlockSpec((None, K, D), lambda i: (widx, 0, 0)),
            pl.BlockSpec((1, D), lambda i: (0, 0)),
            rows, rows,
        ],
        out_specs=[rows, rows, pl.BlockSpec((8, LANES), lambda i: (0, 0))],
        out_shape=[jax.ShapeDtypeStruct((T, D), BF), jax.ShapeDtypeStruct((T, D), F32),
                   jax.ShapeDtypeStruct((8, LANES), F32)],
        compiler_params=_cp("arbitrary"),
    )(a, w3, gain, h_old, target)


def _shift_down(u, k):
    row = lax.broadcasted_iota(jnp.int32, u.shape, 0)
    return jnp.where(row >= k, pltpu.roll(u, k, 0), 0.0)


def _shift_up(u, k):
    T = u.shape[0]
    row = lax.broadcasted_iota(jnp.int32, u.shape, 0)
    return jnp.where(row < T - k, pltpu.roll(u, T - k, 0), 0.0)


def _conv_fwd(z, cw, *, name):
    T = z.shape[0]
    D = z.shape[1] // 3
    tc = cw.shape[2]
    nb = D // tc

    def body(b_ref, c_ref, h_ref, w_ref, o_ref):
        u = c_ref[...].astype(F32) * h_ref[...].astype(F32)
        w = w_ref[...]
        conv = w[2:3] * u + w[1:2] * _shift_down(u, 1) + w[0:1] * _shift_down(u, 2)
        o_ref[...] = (b_ref[...].astype(F32) * conv).astype(BF)

    return pl.pallas_call(
        body, name=name,
        grid=(nb,),
        in_specs=[
            pl.BlockSpec((T, tc), lambda j: (0, j)),
            pl.BlockSpec((T, tc), lambda j: (0, nb + j)),
            pl.BlockSpec((T, tc), lambda j: (0, 2 * nb + j)),
            pl.BlockSpec((None, 8, tc), lambda j: (j, 0, 0)),
        ],
        out_specs=pl.BlockSpec((T, tc), lambda j: (0, j)),
        out_shape=jax.ShapeDtypeStruct((T, D), BF),
        compiler_params=_cp("parallel"),
    )(z, z, z, cw)


def _conv_bwd(z, cw, dv, *, name):
    T = z.shape[0]
    D = z.shape[1] // 3
    tc = LANES
    nb = D // tc
    per = cw.shape[2] // tc

    def body(b_ref, c_ref, h_ref, w_ref, dv_ref, dz_ref, dw_ref, stage, sems):
        j = pl.program_id(0)
        slot = j % 2

        def slab(p, jj, s):
            col = pl.multiple_of((p * nb + jj) * tc, tc)
            return pltpu.make_async_copy(stage.at[s, p], dz_ref.at[:, pl.ds(col, tc)], sems.at[s, p])

        @pl.when(j >= 2)
        def _():
            for p in range(3):
                slab(p, j - 2, slot).wait()

        c = c_ref[...].astype(F32)
        h = h_ref[...].astype(F32)
        u = c * h
        u1 = _shift_down(u, 1)
        u2 = _shift_down(u, 2)
        w = w_ref[...]
        dvv = dv_ref[...].astype(F32)
        dconv = dvv * b_ref[...].astype(F32)
        du = w[2:3] * dconv + w[1:2] * _shift_up(dconv, 1) + w[0:1] * _shift_up(dconv, 2)
        rows = lax.broadcasted_iota(jnp.int32, (8, tc), 0)
        dw_ref[...] = jnp.where(rows == 0, jnp.sum(dconv * u2, axis=0, keepdims=True),
                                jnp.where(rows == 1, jnp.sum(dconv * u1, axis=0, keepdims=True),
                                          jnp.where(rows == 2, jnp.sum(dconv * u, axis=0, keepdims=True), 0.0)))
        stage[slot, 0] = (dvv * (w[2:3] * u + w[1:2] * u1 + w[0:1] * u2)).astype(BF)
        stage[slot, 1] = (du * h).astype(BF)
        stage[slot, 2] = (du * c).astype(BF)
        for p in range(3):
            slab(p, j, slot).start()

        @pl.when(j == nb - 1)
        def _():
            for p in range(3):
                slab(p, j, slot).wait()
            if nb > 1:
                for p in range(3):
                    slab(p, j - 1, 1 - slot).wait()

    return pl.pallas_call(
        body, name=name,
        grid=(nb,),
        in_specs=[
            pl.BlockSpec((T, tc), lambda j: (0, j)),
            pl.BlockSpec((T, tc), lambda j: (0, nb + j)),
            pl.BlockSpec((T, tc), lambda j: (0, 2 * nb + j)),
            pl.BlockSpec((None, 8, tc), lambda j: (j // per, 0, j % per)),
            pl.BlockSpec((T, tc), lambda j: (0, j)),
        ],
        out_specs=[ANY, pl.BlockSpec((8, tc), lambda j: (0, j))],
        out_shape=[jax.ShapeDtypeStruct((T, 3 * D), BF), jax.ShapeDtypeStruct((8, D), F32)],
        scratch_shapes=[pltpu.VMEM((2, 3, T, tc), BF), pltpu.SemaphoreType.DMA((2, 3))],
        compiler_params=_cp("arbitrary"),
    )(z, z, z, cw, dv)


def _strided(base, count, d):
    return pl.ds(base, count, stride=d) if d > 1 else pl.ds(pl.multiple_of(base, BAND), count)


def _fill_band_bias(bias):
    qi = lax.broadcasted_iota(jnp.int32, (2 * BAND, 2 * BAND), 0) % BAND
    kj = lax.broadcasted_iota(jnp.int32, (2 * BAND, 2 * BAND), 1)
    dist = qi + BAND - kj
    band = (dist >= 0) & (dist <= BAND)
    bias[0] = jnp.where(band & (kj >= BAND), 0.0, NEG_INF)
    bias[1] = jnp.where(band, 0.0, NEG_INF)


def _attn_block_rows(T):
    return min(ATTN_BLOCK_ROWS, T)


def _head_mask():
    lane = lax.broadcasted_iota(jnp.int32, (2 * BAND, LANES), 1)
    row = lax.broadcasted_iota(jnp.int32, (2 * BAND, LANES), 0)
    return (lane < HEAD_DIM) == (row < BAND)


def _attn_fwd(q_all, kv_all, *, name):
    T = q_all.shape[0]
    NB = len(BRANCHES)
    Dm = q_all.shape[1] // NB
    HP = Dm // LANES
    R = _attn_block_rows(T)
    units = R // BAND
    dmax = max(d for _, d in BRANCHES)

    def body(*refs):
        ins = refs[:5 * NB]
        o_ref, l_ref, qbuf, kbuf, vbuf, o_s, l_s, bias = refs[5 * NB:]
        n = pl.program_id(0)
        pl.when((n == 0) & (pl.program_id(1) == 0))(lambda: _fill_band_bias(bias))
        hm = _head_mask()
        low = lax.broadcasted_iota(jnp.int32, (BAND, LANES), 1) < HEAD_DIM

        for g, (_, d) in enumerate(BRANCHES):
            q_ref, kp_ref, kc_ref, vp_ref, vc_ref = ins[5 * g:5 * g + 5]
            pr = BAND * d
            qbuf[...] = q_ref[...].astype(F32)
            kbuf[0:pr, :] = kp_ref[...].astype(F32)
            kbuf[pr:pr + R, :] = kc_ref[...].astype(F32)
            vbuf[0:pr, :] = vp_ref[...].astype(F32)
            vbuf[pr:pr + R, :] = vc_ref[...].astype(F32)

            def unit(u, carry, g=g, d=d, pr=pr):
                sub = u // d
                base = sub * pr + (u - sub * d)
                q = qbuf[_strided(base, BAND, d), :]
                q2 = jnp.where(hm, jnp.concatenate([q, q], axis=0), 0.0).astype(BF)
                k2 = kbuf[_strided(base, 2 * BAND, d), :].astype(BF)
                v2 = vbuf[_strided(base, 2 * BAND, d), :].astype(BF)
                s = lax.dot_general(q2, k2, (((1,), (1,)), ((), ())), preferred_element_type=F32)
                s = s + bias[((n > 0) | (sub > 0)).astype(jnp.int32)]
                m = jnp.max(s, axis=-1, keepdims=True)
                p = jnp.exp(s - m)
                l = jnp.sum(p, axis=-1, keepdims=True)
                pv = jnp.dot(p.astype(BF), v2, preferred_element_type=F32) * (1.0 / l)
                lse = m + jnp.log(l)
                o_s[g, _strided(base, BAND, d), :] = jnp.where(low, pv[:BAND], pv[BAND:])
                l_s[g, _strided(base, BAND, d), :] = jnp.where(low, lse[:BAND], lse[BAND:])
                return carry

            lax.fori_loop(0, units, unit, 0, unroll=min(ATTN_FWD_UNROLL, units))

        def merge(i, carry):
            sl = pl.ds(pl.multiple_of(i * BAND, BAND), BAND)
            lv = [l_s[g, sl, :] for g in range(NB)]
            m = functools.reduce(jnp.maximum, lv)
            e = [jnp.exp(v - m) for v in lv]
            tot = functools.reduce(jnp.add, e)
            inv = 1.0 / tot
            o_ref[sl, :] = functools.reduce(jnp.add, [(e[g] * inv) * o_s[g, sl, :] for g in range(NB)])
            l_ref[sl, :] = m + jnp.log(tot)
            return carry

        lax.fori_loop(0, units, merge, 0)

    in_specs, args = [], []
    for g, (_, d) in enumerate(BRANCHES):
        per = R // (BAND * d)
        for col, rows, idx in (
                (g * HP, R, lambda n, hp: n),
                (g * HP, BAND * d, lambda n, hp, per=per: jnp.maximum(n * per - 1, 0)),
                (g * HP, R, lambda n, hp: n),
                ((NB + g) * HP, BAND * d, lambda n, hp, per=per: jnp.maximum(n * per - 1, 0)),
                ((NB + g) * HP, R, lambda n, hp: n)):
            in_specs.append(pl.BlockSpec((rows, LANES), lambda n, hp, col=col, idx=idx: (idx(n, hp), col + hp)))
        args += [q_all, kv_all, kv_all, kv_all, kv_all]
    out = pl.BlockSpec((R, LANES), lambda n, hp: (n, hp))
    return pl.pallas_call(
        body, name=name,
        grid=(T // R, HP),
        in_specs=in_specs,
        out_specs=[out, out],
        out_shape=[jax.ShapeDtypeStruct((T, Dm), F32), jax.ShapeDtypeStruct((T, Dm), F32)],
        scratch_shapes=[pltpu.VMEM((R, LANES), F32),
                        pltpu.VMEM((BAND * dmax + R, LANES), F32), pltpu.VMEM((BAND * dmax + R, LANES), F32),
                        pltpu.VMEM((NB, R, LANES), F32), pltpu.VMEM((NB, R, LANES), F32),
                        pltpu.VMEM((2, 2 * BAND, 2 * BAND), F32)],
        compiler_params=_cp("arbitrary", "arbitrary"),
    )(*args)


def _attn_bwd(q_all, kv_all, do, o, lse, cos2, ss2, g, d, prev, *, name, dep=None):
    T = q_all.shape[0]
    NB = len(BRANCHES)
    Dm = q_all.shape[1] // NB
    HP = Dm // LANES
    R = _attn_block_rows(T)
    nblk = T // R
    units = R // BAND
    pr = BAND * d
    per = R // pr
    scale = HEAD_DIM ** -0.5

    def rope_bwd(t, cosv, ssv, first):
        return t * cosv - _rot_half(t, first) * ssv

    def body(q_ref, kp_ref, kc_ref, vp_ref, vc_ref, do_ref, o_ref, l_ref, cos_ref, ss_ref, *rest):
        dq_ref, dk_ref, dv_ref, qbuf, kbuf, vbuf, dq_s, dk_s, dv_s, pend_k, pend_v, bias = rest[-12:]
        i = pl.program_id(1)
        n = nblk - 1 - i
        pl.when((i == 0) & (pl.program_id(0) == 0))(lambda: _fill_band_bias(bias))
        hm = _head_mask()
        low = lax.broadcasted_iota(jnp.int32, (BAND, LANES), 1) < HEAD_DIM

        qbuf[...] = q_ref[...].astype(F32)
        kbuf[0:pr, :] = kp_ref[...].astype(F32)
        kbuf[pr:pr + R, :] = kc_ref[...].astype(F32)
        vbuf[0:pr, :] = vp_ref[...].astype(F32)
        vbuf[pr:pr + R, :] = vc_ref[...].astype(F32)

        @pl.when(i == 0)
        def _():
            pend_k[...] = jnp.zeros_like(pend_k)
            pend_v[...] = jnp.zeros_like(pend_v)

        def unit(u, carry):
            sub = per - 1 - u // d
            cls = u % d
            base = sub * pr + cls
            sl = _strided(base, BAND, d)
            sl2 = _strided(base, 2 * BAND, d)
            q = qbuf[sl, :]
            dov = do_ref[sl, :]
            ov = o_ref[sl, :]
            lv = l_ref[sl, :]
            q2 = jnp.where(hm, jnp.concatenate([q, q], axis=0), 0.0).astype(BF)
            do2 = jnp.where(hm, jnp.concatenate([dov, dov], axis=0), 0.0)
            oo = dov * ov
            delta = jnp.sum(jnp.where(hm, jnp.concatenate([oo, oo], axis=0), 0.0), axis=-1, keepdims=True)
            lse2 = jnp.concatenate([lv[:, 0:1], lv[:, HEAD_DIM:HEAD_DIM + 1]], axis=0)
            do2 = do2.astype(BF)
            k2 = kbuf[sl2, :].astype(BF)
            v2 = vbuf[sl2, :].astype(BF)
            s = lax.dot_general(q2, k2, (((1,), (1,)), ((), ())), preferred_element_type=F32)
            p = jnp.exp(s + bias[((n > 0) | (sub > 0)).astype(jnp.int32)] - lse2)
            dp = lax.dot_general(do2, v2, (((1,), (1,)), ((), ())), preferred_element_type=F32)
            ds = (p * (dp - delta)).astype(BF)
            dq2 = jnp.dot(ds, k2, preferred_element_type=F32)
            dq = jnp.where(low, dq2[:BAND], dq2[BAND:])
            dq_s[sl, :] = dq
            tn = (((0,), (0,)), ((), ()))
            dk2 = lax.dot_general(ds, q2, tn, preferred_element_type=F32)
            dv2 = lax.dot_general(p.astype(BF), do2, tn, preferred_element_type=F32)
            dk_s[sl, :] = dk2[BAND:] + pend_k[cls]
            dv_s[sl, :] = dv2[BAND:] + pend_v[cls]
            pend_k[cls] = dk2[:BAND]
            pend_v[cls] = dv2[:BAND]
            return carry

        lax.fori_loop(0, units, unit, 0, unroll=min(ATTN_BWD_UNROLL, units))

        whole = _first_half_mask(R)
        dq_ref[...] = (rope_bwd(dq_s[...], cos_ref[...], ss_ref[...], whole) * scale).astype(BF)
        dk_ref[...] = rope_bwd(dk_s[...], cos_ref[...], ss_ref[...], whole).astype(BF)
        dv_ref[...] = dv_s[...].astype(BF)

    blk = (R, LANES)
    pblk = (pr, LANES)
    cur = lambda hp, i: nblk - 1 - i
    prv = lambda hp, i: jnp.maximum((nblk - 1 - i) * per - 1, 0)
    in_specs = [
        pl.BlockSpec(blk, lambda hp, i: (cur(hp, i), g * HP + hp)),
        pl.BlockSpec(pblk, lambda hp, i: (prv(hp, i), g * HP + hp)),
        pl.BlockSpec(blk, lambda hp, i: (cur(hp, i), g * HP + hp)),
        pl.BlockSpec(pblk, lambda hp, i: (prv(hp, i), (NB + g) * HP + hp)),
        pl.BlockSpec(blk, lambda hp, i: (cur(hp, i), (NB + g) * HP + hp)),
        pl.BlockSpec(blk, lambda hp, i: (cur(hp, i), hp)),
        pl.BlockSpec(blk, lambda hp, i: (cur(hp, i), hp)),
        pl.BlockSpec(blk, lambda hp, i: (cur(hp, i), hp)),
        pl.BlockSpec(blk, lambda hp, i: (cur(hp, i), 0)),
        pl.BlockSpec(blk, lambda hp, i: (cur(hp, i), 0)),
    ]
    args = [q_all, kv_all, kv_all, kv_all, kv_all, do, o, lse, cos2, ss2]
    if dep is not None:
        in_specs.append(ANY)
        args.append(dep)
    aliases = {}
    if prev is not None:
        in_specs += [ANY, ANY, ANY]
        aliases = {len(args): 0, len(args) + 1: 1, len(args) + 2: 2}
        args += list(prev)
    wide = jax.ShapeDtypeStruct((T, NB * Dm), BF)
    out = pl.BlockSpec(blk, lambda hp, i: (cur(hp, i), g * HP + hp))
    return pl.pallas_call(
        body, name=name,
        grid=(HP, nblk),
        in_specs=in_specs,
        out_specs=[out, out, out],
        out_shape=[wide, wide, wide],
        scratch_shapes=[pltpu.VMEM(blk, F32), pltpu.VMEM((pr + R, LANES), F32), pltpu.VMEM((pr + R, LANES), F32),
                        pltpu.VMEM(blk, F32), pltpu.VMEM(blk, F32), pltpu.VMEM(blk, F32),
                        pltpu.VMEM((d, BAND, LANES), F32), pltpu.VMEM((d, BAND, LANES), F32),
                        pltpu.VMEM((2, 2 * BAND, 2 * BAND), F32)],
        input_output_aliases=aliases,
        compiler_params=_cp("arbitrary", "arbitrary"),
    )(*args)


def _postnorm_bwd(dh, y, g_ref_val):
    r = lax.rsqrt(jnp.mean(y * y, axis=-1, keepdims=True) + RMS_EPS)
    yn = y * r
    dyn = dh * g_ref_val
    dy = r * (dyn - yn * jnp.mean(dyn * yn, axis=-1, keepdims=True))
    return dy, yn


def _after(body, n_in, dep):
    if dep is None:
        return body
    return lambda *refs: body(*refs[:n_in], *refs[n_in + 1:])


def _dep_spec(dep):
    return [] if dep is None else [ANY]


def _dep_arg(dep):
    return [] if dep is None else [dep]


def _postnorm_bwd_matmul(dh, y, gain, w3, widx, *, name, da_dtype, dep=None):
    T, D = dh.shape
    K = w3.shape[1]
    tm = min(ROW_BLOCK_WIDE, T)

    def body(dh_ref, y_ref, g_ref, w_ref, dy_ref, da_ref, dg_ref):
        i = pl.program_id(0)

        @pl.when(i == 0)
        def _():
            dg_ref[...] = jnp.zeros_like(dg_ref)

        dhv = dh_ref[...]
        dy, yn = _postnorm_bwd(dhv, y_ref[...].astype(F32), g_ref[...])
        dg_ref[...] += jnp.sum(dhv * yn, axis=0, keepdims=True)
        dyb = dy.astype(BF)
        dy_ref[...] = dyb
        da = lax.dot_general(dyb, w_ref[...], (((1,), (1,)), ((), ())), preferred_element_type=F32)
        da_ref[...] = da.astype(da_dtype)

    return pl.pallas_call(
        _after(body, 4, dep), name=name,
        grid=(T // tm,),
        in_specs=[
            pl.BlockSpec((tm, D), lambda i: (i, 0)),
            pl.BlockSpec((tm, D), lambda i: (i, 0)),
            pl.BlockSpec((1, D), lambda i: (0, 0)),
            pl.BlockSpec((None, K, D), lambda i: (widx, 0, 0)),
        ] + _dep_spec(dep),
        out_specs=[pl.BlockSpec((tm, D), lambda i: (i, 0)),
                   pl.BlockSpec((tm, K), lambda i: (i, 0)),
                   pl.BlockSpec((1, D), lambda i: (0, 0))],
        out_shape=[jax.ShapeDtypeStruct((T, D), BF), jax.ShapeDtypeStruct((T, K), da_dtype),
                   jax.ShapeDtypeStruct((1, D), F32)],
        compiler_params=_cp("arbitrary"),
    )(dh, y, gain, w3, *_dep_arg(dep))


def _postnorm_bwd_swiglu(dh, y, gain, wd3, layer, g, u, *, name, dep=None):
    T, D = dh.shape
    F = wd3.shape[1]
    nf = F // 2
    tm = min(ROW_BLOCK, T)

    def body(dh_ref, y_ref, g_ref, w_ref, gg_ref, uu_ref, dy_ref, dgo_ref, duo_ref, dgain_ref, dys):
        i = pl.program_id(0)
        j = pl.program_id(1)

        @pl.when((i == 0) & (j == 0))
        def _():
            dgain_ref[...] = jnp.zeros_like(dgain_ref)

        @pl.when(j == 0)
        def _():
            dhv = dh_ref[...]
            dy, yn = _postnorm_bwd(dhv, y_ref[...].astype(F32), g_ref[...])
            dgain_ref[...] += jnp.sum(dhv * yn, axis=0, keepdims=True)
            dyb = dy.astype(BF)
            dys[...] = dyb
            dy_ref[...] = dyb

        for c0, cw in _col_chunks(nf):
            cols = slice(c0, c0 + cw)
            da = lax.dot_general(dys[...], w_ref[cols, :], (((1,), (1,)), ((), ())), preferred_element_type=F32)
            gv = gg_ref[:, cols].astype(F32)
            uv = uu_ref[:, cols].astype(F32)
            sg = _sigmoid(gv)
            silu = gv * sg
            dgo_ref[:, cols] = (da * uv * (sg + silu * (1.0 - sg))).astype(BF)
            duo_ref[:, cols] = (da * silu).astype(BF)

    act = jax.ShapeDtypeStruct((T, F), BF)
    return pl.pallas_call(
        _after(body, 6, dep), name=name,
        grid=(T // tm, 2),
        in_specs=[
            pl.BlockSpec((tm, D), lambda i, j: (i, 0)),
            pl.BlockSpec((tm, D), lambda i, j: (i, 0)),
            pl.BlockSpec((1, D), lambda i, j: (0, 0)),
            pl.BlockSpec((None, nf, D), lambda i, j: (layer, j, 0)),
            pl.BlockSpec((tm, nf), lambda i, j: (i, j)),
            pl.BlockSpec((tm, nf), lambda i, j: (i, j)),
        ] + _dep_spec(dep),
        out_specs=[pl.BlockSpec((tm, D), lambda i, j: (i, 0)),
                   pl.BlockSpec((tm, nf), lambda i, j: (i, j)),
                   pl.BlockSpec((tm, nf), lambda i, j: (i, j)),
                   pl.BlockSpec((1, D), lambda i, j: (0, 0))],
        out_shape=[jax.ShapeDtypeStruct((T, D), BF), act, act, jax.ShapeDtypeStruct((1, D), F32)],
        scratch_shapes=[pltpu.VMEM((tm, D), BF)],
        compiler_params=_cp("arbitrary", "arbitrary"),
    )(dh, y, gain, wd3, g, u, *_dep_arg(dep))


def _matmul_prenorm_bwd(dzs, wg, layer, h, gain, dh_in, *, name):
    T, D = h.shape
    n = wg.shape[2]
    pair = len(dzs) == 2
    tm = min(ROW_BLOCK if pair else ROW_BLOCK_WIDE, T)
    nj = N_CHIPS // 2 if pair else N_CHIPS

    def body(*refs):
        dz_refs = refs[:len(dzs)]
        w_refs = refs[len(dzs):2 * len(dzs)]
        h_ref, g_ref, dhi_ref, dh_ref, dg_ref, acc = refs[2 * len(dzs):]
        i = pl.program_id(0)
        j = pl.program_id(1)

        @pl.when((i == 0) & (j == 0))
        def _():
            dg_ref[...] = jnp.zeros_like(dg_ref)

        part = None
        for dz_ref, w_ref in zip(dz_refs, w_refs):
            t = lax.dot_general(dz_ref[...].astype(BF), w_ref[...], (((1,), (1,)), ((), ())),
                                preferred_element_type=F32)
            part = t if part is None else part + t

        @pl.when(j == 0)
        def _():
            acc[...] = part

        @pl.when(j > 0)
        def _():
            acc[...] += part

        @pl.when(j == nj - 1)
        def _():
            dhn = acc[...]
            hv = h_ref[...]
            r = lax.rsqrt(jnp.mean(hv * hv, axis=-1, keepdims=True) + RMS_EPS)
            xh = hv * r
            dg_ref[...] += jnp.sum(dhn * xh, axis=0, keepdims=True)
            dxn = dhn * g_ref[...]
            dh_ref[...] = dhi_ref[...] + r * (dxn - xh * jnp.mean(dxn * xh, axis=-1, keepdims=True))

    in_specs = [pl.BlockSpec((tm, n), lambda i, j: (i, j)) for _ in dzs]
    if pair:
        in_specs += [pl.BlockSpec((None, D, n), lambda i, j: (j, layer, 0)),
                     pl.BlockSpec((None, D, n), lambda i, j: (j + nj, layer, 0))]
    else:
        in_specs += [pl.BlockSpec((None, D, n), lambda i, j: (j, layer, 0))]
    in_specs += [pl.BlockSpec((tm, D), lambda i, j: (i, 0)),
                 pl.BlockSpec((1, D), lambda i, j: (0, 0)),
                 pl.BlockSpec((tm, D), lambda i, j: (i, 0))]
    return pl.pallas_call(
        body, name=name,
        grid=(T // tm, nj),
        in_specs=in_specs,
        out_specs=[pl.BlockSpec((tm, D), lambda i, j: (i, 0)), pl.BlockSpec((1, D), lambda i, j: (0, 0))],
        out_shape=[jax.ShapeDtypeStruct((T, D), F32), jax.ShapeDtypeStruct((1, D), F32)],
        scratch_shapes=[pltpu.VMEM((tm, D), F32)],
        compiler_params=_cp("arbitrary", "arbitrary"),
    )(*dzs, *([wg] * len(dzs)), h, gain, dh_in)


def _grad_matmul(a, b, out_shape3, tme, tne, out_index, prev, *, name, dep=None):
    T, M = a.shape
    N = b.shape[1]
    tk = min(GRAD_CHUNK, T)
    nk = T // tk

    def body(a_ref, b_ref, *rest):
        o_ref, acc = rest[-2:]
        k = pl.program_id(2)
        part = jnp.dot(a_ref[...].astype(BF).T, b_ref[...].astype(BF), preferred_element_type=F32)

        @pl.when(k == 0)
        def _():
            acc[...] = part

        @pl.when(k > 0)
        def _():
            acc[...] += part

        @pl.when(k == nk - 1)
        def _():
            o_ref[...] = acc[...].astype(BF)

    in_specs = [pl.BlockSpec((tk, tme), lambda i, j, k: (k, i)),
                pl.BlockSpec((tk, tne), lambda i, j, k: (k, j))]
    args = [a, b]
    aliases = {}
    if prev is not None:
        in_specs.append(ANY)
        args.append(prev)
        aliases = {2: 0}
    in_specs += _dep_spec(dep)
    args += _dep_arg(dep)
    return pl.pallas_call(
        body, name=name,
        grid=(M // tme, N // tne, nk),
        in_specs=in_specs,
        out_specs=pl.BlockSpec((None, tme, tne), lambda i, j, k: out_index(i, j)),
        out_shape=jax.ShapeDtypeStruct(out_shape3, BF),
        scratch_shapes=[pltpu.VMEM((tme, tne), F32)],
        input_output_aliases=aliases,
        compiler_params=_cp("parallel", "parallel", "arbitrary"),
    )(*args)


def _row_tile(R, cap=512):
    fit = [t for t in range(16, min(R, cap) + 1, 16) if R % t == 0]
    return max(fit) if fit else R


def _cast_place(w3, layer, where, dtype, *, name, dep=None):
    _, R, C = w3.shape
    tr = _row_tile(R)

    def body(s_ref, w_ref, o_ref):
        o_ref[...] = w_ref[...].astype(o_ref.dtype)

    return pl.pallas_call(
        _after(body, 2, dep), name=name,
        grid_spec=pltpu.PrefetchScalarGridSpec(
            num_scalar_prefetch=1, grid=(R // tr,),
            in_specs=[pl.BlockSpec((None, tr, C), lambda i, s: (layer, i, 0))] + _dep_spec(dep),
            out_specs=pl.BlockSpec((None, tr, C), lambda i, s: (s[0], i, 0))),
        out_shape=jax.ShapeDtypeStruct((N_CHIPS, R, C), dtype),
        compiler_params=_cp("arbitrary"),
    )(where, w3, *_dep_arg(dep))


def _pair_sum(dw, theirs, where, *, name):
    G, rh, C = theirs.shape
    tr = _row_tile(rh)
    nr = rh // tr

    def body(s_ref, a_ref, b_ref, o_ref):
        o_ref[...] = (a_ref[...].astype(F32) + b_ref[...].astype(F32)).astype(BF)

    mine = pl.BlockSpec((None, tr, C), lambda g, i, s: (g, s[1] * nr + i, 0))
    spec = pl.BlockSpec((None, tr, C), lambda g, i, s: (g, i, 0))
    return pl.pallas_call(
        body, name=name,
        grid_spec=pltpu.PrefetchScalarGridSpec(
            num_scalar_prefetch=1, grid=(G, nr), in_specs=[mine, spec], out_specs=spec),
        out_shape=jax.ShapeDtypeStruct((G, rh, C), BF),
        compiler_params=_cp("arbitrary", "arbitrary"),
    )(where, dw, theirs)


def _chip_sum(landed, parts, where, total_rows, row_off, prev, *, name):
    G, rh, C = landed.shape
    tr = _row_tile(rh)
    nr = rh // tr
    base = row_off // tr

    def body(s_ref, l_ref, p_ref, *rest):
        o_ref = rest[-1]
        for j in range(G):
            def own(j=j):
                v = p_ref[...].astype(F32)
                o_ref[...] = v if j == 0 else o_ref[...] + v

            def other(j=j):
                v = l_ref[j].astype(F32)
                o_ref[...] = v if j == 0 else o_ref[...] + v

            pl.when(s_ref[0] == j)(own)
            pl.when(s_ref[0] != j)(other)

    in_specs = [pl.BlockSpec((G, tr, C), lambda i, s: (0, i, 0)),
                pl.BlockSpec((None, tr, C), lambda i, s: (s[0], i, 0))]
    args = [where, landed, parts]
    aliases = {}
    if prev is not None:
        in_specs.append(ANY)
        args.append(prev)
        aliases = {3: 0}
    return pl.pallas_call(
        body, name=name,
        grid_spec=pltpu.PrefetchScalarGridSpec(
            num_scalar_prefetch=1, grid=(nr,),
            in_specs=in_specs,
            out_specs=pl.BlockSpec((tr, C), lambda i, s: (base + s[1] * nr + i, 0))),
        out_shape=jax.ShapeDtypeStruct((total_rows, C), F32),
        input_output_aliases=aliases,
        compiler_params=_cp("arbitrary"),
    )(*args)


def _adamw(w, g, m, v, *, name, emit_grad=False):
    R, C = w.shape
    tr = _row_tile(R, cap=max(16, ADAMW_BLOCK_BYTES // (4 * C)))
    n_out = 4 if emit_grad else 3

    def body(w_ref, g_ref, m_ref, v_ref, d_ref, mo_ref, vo_ref, *go_ref):
        gv = g_ref[...]
        if emit_grad:
            go_ref[0][...] = gv
        mn = ADAM_B1 * m_ref[...] + (1.0 - ADAM_B1) * gv
        vn = ADAM_B2 * v_ref[...] + (1.0 - ADAM_B2) * jnp.square(gv)
        m_hat = mn / (1.0 - ADAM_B1 ** ADAM_STEP)
        v_hat = vn / (1.0 - ADAM_B2 ** ADAM_STEP)
        d_ref[...] = -ADAM_LR * (m_hat / (jnp.sqrt(v_hat) + ADAM_EPS) + ADAM_WD * w_ref[...])
        mo_ref[...] = mn
        vo_ref[...] = vn

    spec = pl.BlockSpec((tr, C), lambda i: (i, 0))
    shp = jax.ShapeDtypeStruct((R, C), F32)
    return pl.pallas_call(
        body, name=name, grid=(R // tr,), in_specs=[spec] * 4, out_specs=[spec] * n_out,
        out_shape=[shp] * n_out, compiler_params=_cp("parallel"),
    )(w, g, m, v)


def _place():
    x = lax.axis_index("x")
    y = lax.axis_index("y")
    c = lax.axis_index("c")
    chips = [(1 - x, y), (x, 1 - y), (1 - x, 1 - y)]
    return x, y, c, chips


def _chunk_rows(rows, row_bytes, align):
    if rows <= align:
        return rows
    cands = [r for r in range(align, rows + 1, align) if rows % r == 0]
    fit = [r for r in cands if r * row_bytes <= DMA_CHUNK_BYTES]
    return max(fit) if fit else min(cands)


def _row_align(dtype):
    return 8 * (4 // jnp.dtype(dtype).itemsize)


def _start_chunks(make, rows, rc):
    for r0 in range(0, rows, rc):
        make(r0, rc).start()


def _piece_rows(ref, piece, j, h, r0=0, n=None):
    _, lead, off, rows = piece
    rh = rows // 2
    n = rh if n is None else n
    if lead is not None:
        return ref.at[lead, j, pl.ds(h * rh + r0, n)]
    return ref.at[j, pl.ds(off + h * rh + r0, n)]


def _piece_chunk(arr, piece):
    rh = piece[3] // 2
    return rh, _chunk_rows(rh, arr.shape[-1] * arr.dtype.itemsize, _row_align(arr.dtype))


def _ag_start(arrays, taps, groups, *, name):
    na = len(arrays)
    ng = len(groups)
    nt = 0 if taps is None else 1
    n_sem = [3 * len(grp) + (3 if nt and g == 0 else 0) for g, grp in enumerate(groups)]

    def body(*refs):
        ins = refs[:na]
        taps_ref = refs[na] if nt else None
        sems = refs[na + nt:na + nt + 2 * ng]
        token = refs[-1]
        token[...] = jnp.zeros_like(token)
        x, y, c, chips = _place()
        myj = 2 * x + y
        for g, grp in enumerate(groups):
            ssem, rsem = sems[2 * g], sems[2 * g + 1]
            for idx, piece in enumerate(grp):
                ref = ins[piece[0]]
                rh, rc = _piece_chunk(arrays[piece[0]], piece)
                for k, (px, py) in enumerate(chips):
                    def send(r0, n, ref=ref, piece=piece, idx=idx, k=k, px=px, py=py, ssem=ssem, rsem=rsem):
                        part = _piece_rows(ref, piece, myj, c, r0, n)
                        return pltpu.make_async_remote_copy(
                            src_ref=part, dst_ref=part, send_sem=ssem.at[3 * idx + k], recv_sem=rsem.at[3 * idx + k],
                            device_id=(px, py, c), device_id_type=MESH)
                    _start_chunks(send, rh, rc)
            if nt and g == 0:
                for k, (px, py) in enumerate(chips):
                    pltpu.make_async_remote_copy(
                        src_ref=taps_ref.at[myj], dst_ref=taps_ref.at[myj],
                        send_sem=ssem.at[3 * len(grp) + k], recv_sem=rsem.at[3 * len(grp) + k],
                        device_id=(px, py, c), device_id_type=MESH).start()

    sem_shapes = []
    for n in n_sem:
        sem_shapes += [pltpu.SemaphoreType.DMA((n,)), pltpu.SemaphoreType.DMA((n,))]
    ops = list(arrays) + ([taps] if nt else [])
    bufs = [pltpu.HBM(a.shape, a.dtype) for a in ops]
    outs = pl.pallas_call(
        body, name=name,
        out_shape=(*sem_shapes, *bufs, jax.ShapeDtypeStruct((8, LANES), F32)),
        in_specs=[HBM] * (na + nt),
        out_specs=(*([SEM] * (2 * ng)), *([HBM] * (na + nt)), pl.BlockSpec(memory_space=pltpu.VMEM)),
        input_output_aliases={i: 2 * ng + i for i in range(na + nt)},
        compiler_params=pltpu.CompilerParams(has_side_effects=EFFECT),
    )(*[_in_hbm(a) for a in ops])
    sems = [(outs[2 * g], outs[2 * g + 1]) for g in range(ng)]
    return sems, list(outs[2 * ng:2 * ng + na]), (outs[2 * ng + na] if nt else None), outs[-1]


def _ag_wait(sems, vals, taps, group, after, *, name):
    nv = len(vals)
    extra = ([taps] if taps is not None else [])
    nb = nv + len(extra)

    def body(*refs):
        bufs = refs[:nb]
        ssem, rsem = refs[nb], refs[nb + 1]
        x, y, c, chips = _place()
        for idx, piece in enumerate(group):
            for k, (px, py) in enumerate(chips):
                got = _piece_rows(bufs[piece[0]], piece, 2 * px + py, c)
                cp = pltpu.make_async_remote_copy(
                    src_ref=got, dst_ref=got, send_sem=ssem.at[3 * idx + k], recv_sem=rsem.at[3 * idx + k],
                    device_id=(px, py, c), device_id_type=MESH)
                cp.wait_send()
                cp.wait_recv()
        if taps is not None:
            for k, (px, py) in enumerate(chips):
                got = bufs[nv].at[2 * px + py]
                cp = pltpu.make_async_remote_copy(
                    src_ref=got, dst_ref=got, send_sem=ssem.at[3 * len(group) + k],
                    recv_sem=rsem.at[3 * len(group) + k], device_id=(px, py, c), device_id_type=MESH)
                cp.wait_send()
                cp.wait_recv()

    ops = list(vals) + extra
    shapes = [pltpu.HBM(a.shape, a.dtype) for a in ops]
    outs = pl.pallas_call(
        body, name=name,
        out_shape=tuple(shapes),
        in_specs=[HBM] * nb + [SEM, SEM] + _dep_spec(after),
        out_specs=[HBM] * nb,
        input_output_aliases={i: i for i in range(nb)},
        compiler_params=pltpu.CompilerParams(has_side_effects=EFFECT),
    )(*ops, sems[0], sems[1], *_dep_arg(after))
    return list(outs[:nv]), (outs[nv] if taps is not None else None)


def _ag_forward(vals, group, *, name):
    nv = len(vals)
    npc = len(group)

    def body(*refs):
        bufs = refs[nv:2 * nv]
        fsem, gsem = refs[2 * nv:]
        x, y, c, chips = _place()
        sib = (x, y, 1 - c)
        sent = []
        for idx, piece in enumerate(group):
            rh, rc = _piece_chunk(vals[piece[0]], piece)
            for k, (px, py) in enumerate(chips):
                def fwd(r0, n, piece=piece, idx=idx, k=k, pj=2 * px + py):
                    part = _piece_rows(bufs[piece[0]], piece, pj, c, r0, n)
                    return pltpu.make_async_remote_copy(
                        src_ref=part, dst_ref=part, send_sem=fsem.at[3 * idx + k], recv_sem=gsem.at[3 * idx + k],
                        device_id=sib, device_id_type=MESH)
                _start_chunks(fwd, rh, rc)
                sent.append(fwd(0, rh))
        for idx, piece in enumerate(group):
            for k, (px, py) in enumerate(chips):
                theirs = _piece_rows(bufs[piece[0]], piece, 2 * px + py, 1 - c)
                pltpu.make_async_remote_copy(
                    src_ref=theirs, dst_ref=theirs, send_sem=fsem.at[3 * idx + k], recv_sem=gsem.at[3 * idx + k],
                    device_id=sib, device_id_type=MESH).wait_recv()
        for cp in sent:
            cp.wait_send()

    return pl.pallas_call(
        body, name=name,
        in_specs=[ANY] * nv, out_specs=[ANY] * nv,
        out_shape=[jax.ShapeDtypeStruct(v.shape, v.dtype) for v in vals],
        input_output_aliases={i: i for i in range(nv)},
        scratch_shapes=[pltpu.SemaphoreType.DMA((3 * npc,)), pltpu.SemaphoreType.DMA((3 * npc,))],
    )(*vals)


def _sibling_swap(dws, *, name):
    nm = len(dws)
    shapes = [jax.ShapeDtypeStruct((dw.shape[0], dw.shape[1] // 2, dw.shape[2]), dw.dtype) for dw in dws]

    def body(*refs):
        ins = refs[:nm]
        theirs = refs[nm:2 * nm]
        ssem, rsem = refs[2 * nm:]
        x, y, c, _ = _place()
        sib = (x, y, 1 - c)
        cps = []
        for m in range(nm):
            G, rh, cols = shapes[m].shape
            rc = _chunk_rows(rh, cols * shapes[m].dtype.itemsize, _row_align(shapes[m].dtype))
            for j in range(G):
                _start_chunks(lambda r0, n, m=m, j=j, rh=rh: pltpu.make_async_remote_copy(
                    src_ref=ins[m].at[j, pl.ds((1 - c) * rh + r0, n)],
                    dst_ref=theirs[m].at[j, pl.ds(r0, n)], send_sem=ssem.at[m], recv_sem=rsem.at[m],
                    device_id=sib, device_id_type=MESH), rh, rc)
            cps.append(pltpu.make_async_remote_copy(
                src_ref=ins[m].at[:, pl.ds((1 - c) * rh, rh), :], dst_ref=theirs[m],
                send_sem=ssem.at[m], recv_sem=rsem.at[m], device_id=sib, device_id_type=MESH))
        for cp in cps:
            cp.wait()

    return pl.pallas_call(
        body, name=name,
        in_specs=[ANY] * nm, out_specs=[ANY] * nm, out_shape=shapes,
        scratch_shapes=[pltpu.SemaphoreType.DMA((nm,)), pltpu.SemaphoreType.DMA((nm,))],
    )(*dws)


HBM = pl.BlockSpec(memory_space=pltpu.HBM)
SEM = pl.BlockSpec(memory_space=pltpu.SEMAPHORE)
EFFECT = pltpu.SideEffectType.DATAFLOW_SIDE_EFFECTING


def _in_hbm(a):
    return pltpu.with_memory_space_constraint(a, pltpu.HBM)


def _swap_start(dws, *, name):
    nm = len(dws)
    lands = [(dw.shape[0], dw.shape[1] // 2, dw.shape[2]) for dw in dws]

    def body(*refs):
        ins = refs[:nm]
        lnd = refs[nm:2 * nm]
        ssem, rsem = refs[2 * nm:2 * nm + 2]
        token = refs[-1]
        x, y, c, _ = _place()
        sib = (x, y, 1 - c)
        for m in range(nm):
            G, rh, cols = lands[m]
            rc = _chunk_rows(rh, cols * dws[m].dtype.itemsize, _row_align(dws[m].dtype))
            for j in range(G):
                _start_chunks(lambda r0, n, m=m, j=j, rh=rh: pltpu.make_async_remote_copy(
                    src_ref=ins[m].at[j, pl.ds((1 - c) * rh + r0, n)], dst_ref=lnd[m].at[j, pl.ds(r0, n)],
                    send_sem=ssem.at[m], recv_sem=rsem.at[m], device_id=sib, device_id_type=MESH), rh, rc)
        token[...] = jnp.zeros_like(token)

    src = [pltpu.HBM(dw.shape, dw.dtype) for dw in dws]
    dst = [pltpu.HBM(s, dw.dtype) for s, dw in zip(lands, dws)]
    outs = pl.pallas_call(
        body, name=name,
        out_shape=(pltpu.SemaphoreType.DMA((nm,)), pltpu.SemaphoreType.DMA((nm,)), *src, *dst,
                   jax.ShapeDtypeStruct((8, LANES), F32)),
        in_specs=[HBM] * (2 * nm),
        out_specs=(SEM, SEM, *([HBM] * (2 * nm)), pl.BlockSpec(memory_space=pltpu.VMEM)),
        input_output_aliases={i: 2 + i for i in range(2 * nm)},
        compiler_params=pltpu.CompilerParams(has_side_effects=EFFECT),
    )(*[_in_hbm(dw) for dw in dws], *[_in_hbm(lax.empty(s, dw.dtype)) for s, dw in zip(lands, dws)])
    return (outs[0], outs[1], list(outs[2:2 + nm]), list(outs[2 + nm:2 + 2 * nm])), outs[-1]


def _swap_wait(handle, after, *, name):
    ssem_in, rsem_in, dws, lands = handle
    nm = len(dws)

    def body(*refs):
        ins = refs[:nm]
        lnd = refs[nm:2 * nm]
        ssem, rsem = refs[2 * nm:2 * nm + 2]
        x, y, c, _ = _place()
        for m in range(nm):
            rh = lands[m].shape[1]
            cp = pltpu.make_async_remote_copy(
                src_ref=ins[m].at[:, pl.ds((1 - c) * rh, rh), :], dst_ref=lnd[m],
                send_sem=ssem.at[m], recv_sem=rsem.at[m], device_id=(x, y, 1 - c), device_id_type=MESH)
            cp.wait_send()
            cp.wait_recv()

    bufs = [pltpu.HBM(a.shape, a.dtype) for a in list(dws) + list(lands)]
    outs = pl.pallas_call(
        body, name=name,
        out_shape=tuple(bufs),
        in_specs=[HBM] * (2 * nm) + [SEM, SEM, ANY],
        out_specs=[HBM] * (2 * nm),
        input_output_aliases={i: i for i in range(2 * nm)},
        compiler_params=pltpu.CompilerParams(has_side_effects=EFFECT),
    )(*dws, *lands, ssem_in, rsem_in, after)
    return list(outs[:nm]), list(outs[nm:])


def _exchange_start(parts, *, name):
    nm = len(parts)

    def body(*refs):
        ins = refs[:nm]
        lands = refs[nm:2 * nm]
        ssem, rsem = refs[2 * nm:2 * nm + 2]
        token = refs[-1]
        x, y, c, chips = _place()
        myj = 2 * x + y
        for m in range(nm):
            _, rh, cols = parts[m].shape
            rc = _chunk_rows(rh, cols * parts[m].dtype.itemsize, _row_align(parts[m].dtype))
            for k, (px, py) in enumerate(chips):
                _start_chunks(lambda r0, n, m=m, k=k, px=px, py=py: pltpu.make_async_remote_copy(
                    src_ref=ins[m].at[2 * px + py, pl.ds(r0, n)], dst_ref=lands[m].at[myj, pl.ds(r0, n)],
                    send_sem=ssem.at[3 * m + k], recv_sem=rsem.at[3 * m + k],
                    device_id=(px, py, c), device_id_type=MESH), rh, rc)
        token[...] = jnp.zeros_like(token)

    bufs = [pltpu.HBM(p.shape, p.dtype) for p in parts]
    outs = pl.pallas_call(
        body, name=name,
        out_shape=(pltpu.SemaphoreType.DMA((3 * nm,)), pltpu.SemaphoreType.DMA((3 * nm,)), *bufs, *bufs,
                   jax.ShapeDtypeStruct((8, LANES), F32)),
        in_specs=[HBM] * (2 * nm),
        out_specs=(SEM, SEM, *([HBM] * (2 * nm)), pl.BlockSpec(memory_space=pltpu.VMEM)),
        input_output_aliases={i: 2 + i for i in range(2 * nm)},
        compiler_params=pltpu.CompilerParams(has_side_effects=EFFECT),
    )(*[_in_hbm(p) for p in parts], *[_in_hbm(lax.empty(p.shape, p.dtype)) for p in parts])
    return (outs[0], outs[1], list(outs[2:2 + nm]), list(outs[2 + nm:2 + 2 * nm])), outs[-1]


def _exchange_wait(handle, after, *, name):
    ssem_in, rsem_in, parts, lands = handle
    nm = len(parts)

    def body(*refs):
        ins = refs[:nm]
        lnd = refs[nm:2 * nm]
        ssem, rsem = refs[2 * nm:2 * nm + 2]
        x, y, c, chips = _place()
        for m in range(nm):
            for k, (px, py) in enumerate(chips):
                pj = 2 * px + py
                cp = pltpu.make_async_remote_copy(
                    src_ref=ins[m].at[pj], dst_ref=lnd[m].at[pj],
                    send_sem=ssem.at[3 * m + k], recv_sem=rsem.at[3 * m + k],
                    device_id=(px, py, c), device_id_type=MESH)
                cp.wait_send()
                cp.wait_recv()

    bufs = [pltpu.HBM(p.shape, p.dtype) for p in parts]
    outs = pl.pallas_call(
        body, name=name,
        out_shape=(*bufs, *bufs),
        in_specs=[HBM] * (2 * nm) + [SEM, SEM, ANY],
        out_specs=[HBM] * (2 * nm),
        input_output_aliases={i: i for i in range(2 * nm)},
        compiler_params=pltpu.CompilerParams(has_side_effects=EFFECT),
    )(*parts, *lands, ssem_in, rsem_in, after)
    return list(outs[nm:]), list(outs[:nm])


def _sibling_join(grads, regions, *, name):
    nm = len(grads)
    nr = len(regions)
    shapes = [jax.ShapeDtypeStruct(g.shape, g.dtype) for g in grads]

    def body(*refs):
        outs = refs[nm:2 * nm]
        ssem, rsem = refs[2 * nm:]
        x, y, c, _ = _place()
        sib = (x, y, 1 - c)
        cps = []
        for i, (m, off, rows) in enumerate(regions):
            rh, cols = rows // 2, grads[m].shape[1]
            rc = _chunk_rows(rh, cols * grads[m].dtype.itemsize, _row_align(grads[m].dtype))

            def send(r0, n, i=i, m=m, off=off, rh=rh):
                part = outs[m].at[pl.ds(off + c * rh + r0, n)]
                return pltpu.make_async_remote_copy(
                    src_ref=part, dst_ref=part, send_sem=ssem.at[i], recv_sem=rsem.at[i],
                    device_id=sib, device_id_type=MESH)
            _start_chunks(send, rh, rc)
            cps.append(send(0, rh))
        for i, (m, off, rows) in enumerate(regions):
            rh = rows // 2
            cps[i].wait_send()
            got = outs[m].at[pl.ds(off + (1 - c) * rh, rh)]
            pltpu.make_async_remote_copy(
                src_ref=got, dst_ref=got, send_sem=ssem.at[i], recv_sem=rsem.at[i],
                device_id=sib, device_id_type=MESH).wait_recv()

    return pl.pallas_call(
        body, name=name,
        in_specs=[ANY] * nm, out_specs=[ANY] * nm, out_shape=shapes,
        input_output_aliases={i: i for i in range(nm)},
        scratch_shapes=[pltpu.SemaphoreType.DMA((nr,)), pltpu.SemaphoreType.DMA((nr,))],
    )(*grads)


def _all_reduce_small(pack):
    R, C = pack.shape

    def body(in_ref, out_ref, slots, ssem, rsem):
        x, y, c, _ = _place()
        me = 4 * x + 2 * y + c
        slots[me] = in_ref[...]
        cps = []
        for k in range(1, N_DEV):
            dx, dy, dc = (k >> 2) & 1, (k >> 1) & 1, k & 1
            peer = (x ^ dx, y ^ dy, c ^ dc)
            cp = pltpu.make_async_remote_copy(
                src_ref=in_ref, dst_ref=slots.at[me], send_sem=ssem.at[k], recv_sem=rsem.at[k],
                device_id=peer, device_id_type=MESH)
            cp.start()
            cps.append(cp)
        for k in range(1, N_DEV):
            dx, dy, dc = (k >> 2) & 1, (k >> 1) & 1, k & 1
            got = slots.at[4 * (x ^ dx) + 2 * (y ^ dy) + (c ^ dc)]
            pltpu.make_async_remote_copy(
                src_ref=got, dst_ref=got, send_sem=ssem.at[k], recv_sem=rsem.at[k],
                device_id=(x ^ dx, y ^ dy, c ^ dc), device_id_type=MESH).wait_recv()
        for cp in cps:
            cp.wait_send()
        acc = slots[0]
        for s in range(1, N_DEV):
            acc = acc + slots[s]
        out_ref[...] = acc

    return pl.pallas_call(
        body, name="ar_small",
        in_specs=[pl.BlockSpec(memory_space=pltpu.VMEM)],
        out_specs=pl.BlockSpec(memory_space=pltpu.VMEM),
        out_shape=jax.ShapeDtypeStruct((R, C), F32),
        scratch_shapes=[pltpu.VMEM((N_DEV, R, C), F32),
                        pltpu.SemaphoreType.DMA((N_DEV,)), pltpu.SemaphoreType.DMA((N_DEV,))],
    )(pack)


def kernel(x, positions, mix_norm_pre, mix_norm_post, ffn_norm_pre, ffn_norm_post, ffn_w_gate_up, ffn_w_down, conv_w_in, conv_w, conv_w_out, kv_norm, w_kv, w_q, w_o, loss_target, m_mix_norm_pre, m_mix_norm_post, m_ffn_norm_pre, m_ffn_norm_post, m_ffn_w_gate_up, m_ffn_w_down, m_conv_w_in, m_conv_w, m_conv_w_out, m_kv_norm, m_w_kv, m_w_q, m_w_o, v_mix_norm_pre, v_mix_norm_post, v_ffn_norm_pre, v_ffn_norm_post, v_ffn_w_gate_up, v_ffn_w_down, v_conv_w_in, v_conv_w, v_conv_w_out, v_kv_norm, v_w_kv, v_w_q, v_w_o):
    T, D = x.shape[1], x.shape[2]
    L = ffn_w_gate_up.shape[0]
    n_gu = ffn_w_gate_up.shape[2]
    f_sh = ffn_w_down.shape[1]
    F = N_CHIPS * f_sh
    x0 = x[0]
    tgt = loss_target[0]

    half = HEAD_DIM // 2
    inv_freq = ROPE_THETA ** (-jnp.arange(half, dtype=F32) / half)
    ang = positions[0].astype(F32)[:, None] * inv_freq
    cosv, sinv = jnp.cos(ang), jnp.sin(ang)
    cos2 = jnp.tile(cosv, (1, LANES // half))
    ss2 = jnp.tile(jnp.concatenate([-sinv, sinv], axis=1), (1, LANES // HEAD_DIM))

    def as2d(a):
        return a.reshape(-1, a.shape[-1])

    big = [ffn_w_gate_up, ffn_w_down, conv_w_in, conv_w_out, w_kv, w_q, w_o]
    big_m = [m_ffn_w_gate_up, m_ffn_w_down, m_conv_w_in, m_conv_w_out, m_w_kv, m_w_q, m_w_o]
    big_v = [v_ffn_w_gate_up, v_ffn_w_down, v_conv_w_in, v_conv_w_out, v_w_kv, v_w_q, v_w_o]
    chip = 2 * lax.axis_index("x") + lax.axis_index("y")
    where = jnp.stack([chip, lax.axis_index("c")]).astype(jnp.int32)
    tc = conv_w.shape[2]
    cw_pad = jnp.concatenate([conv_w[0], jnp.zeros((8 - conv_w.shape[1], tc), F32)], axis=0)

    GU0, GU1, WD0, WD1, WCI, WCO, WKV, WQ, WO = range(9)
    shards = [(ffn_w_gate_up, 0), (ffn_w_gate_up, 1), (ffn_w_down, 0), (ffn_w_down, 1), (conv_w_in, 0),
              (conv_w_out, 0), (w_kv[None], 0), (w_q, 0), (w_o, 0)]
    ag_groups = [
        [(WCI, None, 0, D), (WCO, None, 0, D // N_CHIPS)],
        [(GU0, None, 0, D), (WD0, None, 0, f_sh)],
        [(WKV, None, 0, D), (WQ, None, 0, D)],
        [(WO, None, 0, D // N_CHIPS), (GU1, None, 0, D), (WD1, None, 0, f_sh)],
    ]

    def localised(group, idxs):
        return [(idxs.index(p[0]),) + p[1:] for p in group]

    cur = [None] * len(shards)
    first = [WCI, WCO]
    sems0, vals, taps, token = _ag_start(
        [_cast_place(*shards[i], where, BF, name=f"place{i}") for i in first],
        _cast_place(cw_pad[None], 0, where, F32, name="place_taps"), [localised(ag_groups[0], first)],
        name="ag_start0")
    for i, v in zip(first, vals):
        cur[i] = v
    rest = [i for i in range(len(shards)) if i not in first]
    sems1, vals, _, _ = _ag_start(
        [_cast_place(*shards[i], where, BF, name=f"place{i}", dep=token) for i in rest], None,
        [localised(g, rest) for g in ag_groups[1:]], name="ag_start1")
    for i, v in zip(rest, vals):
        cur[i] = v
    ag_sems = sems0 + sems1

    def gather_group(g, after):
        nonlocal taps
        idxs = sorted({p[0] for p in ag_groups[g]})
        local = localised(ag_groups[g], idxs)
        vals, landed_taps = _ag_wait(ag_sems[g], [cur[i] for i in idxs], taps if g == 0 else None, local, after,
                                     name=f"ag_wait{g}")
        if g == 0:
            taps = landed_taps
        vals = _ag_forward(vals, local, name=f"ag_forward{g}")
        for i, v in zip(idxs, vals):
            cur[i] = v

    def row(a, i):
        return a[i:i + 1]

    gather_group(0, None)
    wci, wco, cw = cur[WCI], cur[WCO].reshape(1, D, D), taps
    z, hn_m0 = _norm_matmul(x0, row(mix_norm_pre, 0), wci, cos2, ss2, name="f0_conv_in",
                            rope_shards=0, scale=1.0, out_dtype=BF)
    vmix = _conv_fwd(z, cw, name="f0_conv")
    y0, h1 = _matmul_postnorm(vmix, wco, 0, row(mix_norm_post, 0), x0, name="f0_conv_out")
    gather_group(1, h1)
    wgu0, wd0 = cur[GU0], cur[WD0].reshape(1, F, D)
    g0, u0, a0, hn_f0 = _norm_swiglu(h1, row(ffn_norm_pre, 0), wgu0, 0, name="f0_gate_up")
    f0, h2 = _matmul_postnorm(a0, wd0, 0, row(ffn_norm_post, 0), h1, name="f0_down")

    gather_group(2, h2)
    wkv, wq = cur[WKV], cur[WQ]
    kv_all, hn_kv = _norm_matmul(h2, kv_norm.reshape(1, D), wkv, cos2, ss2, name="f1_kv",
                                 rope_shards=N_CHIPS // 2, scale=1.0, out_dtype=BF)
    q_all, hn_m1 = _norm_matmul(h2, row(mix_norm_pre, 1), wq, cos2, ss2, name="f1_q",
                                rope_shards=N_CHIPS, scale=HEAD_DIM ** -0.5, out_dtype=BF)
    o_att, lse = _attn_fwd(q_all, kv_all, name="f1_attn")
    gather_group(3, o_att)
    wgu1, wd1, wo = cur[GU1], cur[WD1].reshape(1, F, D), cur[WO].reshape(1, D, D)
    y1, h3 = _matmul_postnorm(o_att, wo, 0, row(mix_norm_post, 1), h2, name="f1_attn_out")
    g1, u1, a1, hn_f1 = _norm_swiglu(h3, row(ffn_norm_pre, 1), wgu1, 0, name="f1_gate_up")
    f1, dh4, sq = _matmul_postnorm_loss(a1, wd1, 0, row(ffn_norm_post, 1), h3, tgt, name="f1_down_loss")
    loss_part = 0.5 * sq[0, 0] / D

    gu_shape = (N_CHIPS, D, n_gu)
    in_chips = lambda a: a.reshape(N_CHIPS, -1, a.shape[-1])

    def scatter_start(dws, tag):
        theirs = _sibling_swap(dws, name=f"rs_swap_{tag}")
        parts = [_pair_sum(dw, t, where, name=f"rs_pair_sum_{tag}{i}") for i, (dw, t) in enumerate(zip(dws, theirs))]
        return _exchange_start(parts, name=f"rs_exchange_start_{tag}")

    def scatter_go(swap, after, tag):
        dws, theirs = _swap_wait(swap, after, name=f"rs_swap_wait_{tag}")
        parts = [_pair_sum(dw, t, where, name=f"rs_pair_sum_{tag}{i}") for i, (dw, t) in enumerate(zip(dws, theirs))]
        return _exchange_start(parts, name=f"rs_exchange_start_{tag}")

    dyf1, dg1, du1, d_ffn_post1 = _postnorm_bwd_swiglu(dh4, f1, row(ffn_norm_post, 1), wd1, 0, g1, u1,
                                                       name="b1_down")
    dwd1 = _grad_matmul(a1, dyf1, (2, F // 2, D), F // 2, D, lambda i, j: (i, 0, 0), None, name="b1_dw_down")
    dwgu1 = _grad_matmul(hn_f1, dg1, gu_shape, D, n_gu, lambda i, j: (j, 0, 0), None, name="b1_dw_gate")
    dwgu1 = _grad_matmul(hn_f1, du1, gu_shape, D, n_gu, lambda i, j: (j + 2, 0, 0), dwgu1, name="b1_dw_up")
    dh3, d_ffn_pre1 = _matmul_prenorm_bwd((dg1, du1), wgu1, 0, h3, row(ffn_norm_pre, 1), dh4, name="b1_gate_up")

    dy1, do, d_mix_post1 = _postnorm_bwd_matmul(dh3, y1, row(mix_norm_post, 1), wo, 0, name="b1_attn_out",
                                                da_dtype=F32)
    dwo = _grad_matmul(o_att, dy1, (1, D, D), D, D, lambda i, j: (0, 0, 0), None, name="b1_dw_o")
    swap_a, token = _swap_start([dwgu1, in_chips(dwd1), in_chips(dwo)], name="rs_swap_start_a")
    prev = None
    for gi, (window, dil) in enumerate(BRANCHES):
        prev = _attn_bwd(q_all, kv_all, do, o_att, lse, cos2, ss2, gi, dil, prev, name=f"b1_attn{gi}", dep=token)
        token = None
        if gi == 0:
            rs_a, token = scatter_go(swap_a, prev[0], "a")
    dq_all, dk_all, dv_all = prev
    n_q = wq.shape[2]
    n_kv = wkv.shape[2]
    dwq = _grad_matmul(hn_m1, dq_all, (N_CHIPS, D, n_q), D, n_q, lambda i, j: (j, 0, 0), None, name="b1_dw_q")
    dwkv = _grad_matmul(hn_kv, dk_all, (N_CHIPS, D, n_kv), D, n_kv, lambda i, j: (j, 0, 0), None, name="b1_dw_k")
    dwkv = _grad_matmul(hn_kv, dv_all, (N_CHIPS, D, n_kv), D, n_kv, lambda i, j: (j + 2, 0, 0), dwkv, name="b1_dw_v")
    dh2, d_mix_pre1 = _matmul_prenorm_bwd((dq_all,), wq, 0, h2, row(mix_norm_pre, 1), dh3, name="b1_q")
    dh2, d_kv_norm = _matmul_prenorm_bwd((dk_all, dv_all), wkv, 0, h2, kv_norm.reshape(1, D), dh2, name="b1_kv")
    swap_b, token = _swap_start([dwkv, dwq], name="rs_swap_start_b")

    dyf0, dg0, du0, d_ffn_post0 = _postnorm_bwd_swiglu(dh2, f0, row(ffn_norm_post, 0), wd0, 0, g0, u0,
                                                       name="b0_down", dep=token)
    rs_b, token = scatter_go(swap_b, dyf0, "b")
    dwd0 = _grad_matmul(a0, dyf0, (2, F // 2, D), F // 2, D, lambda i, j: (i, 0, 0), None, name="b0_dw_down",
                        dep=token)
    dwgu0 = _grad_matmul(hn_f0, dg0, gu_shape, D, n_gu, lambda i, j: (j, 0, 0), None, name="b0_dw_gate")
    dwgu0 = _grad_matmul(hn_f0, du0, gu_shape, D, n_gu, lambda i, j: (j + 2, 0, 0), dwgu0, name="b0_dw_up")
    dh1, d_ffn_pre0 = _matmul_prenorm_bwd((dg0, du0), wgu0, 0, h1, row(ffn_norm_pre, 0), dh2, name="b0_gate_up")
    swap_c, token = _swap_start([dwgu0, in_chips(dwd0)], name="rs_swap_start_c")

    dy0, dvmix, d_mix_post0 = _postnorm_bwd_matmul(dh1, y0, row(mix_norm_post, 0), wco, 0, name="b0_conv_out",
                                                   da_dtype=BF, dep=token)
    rs_c, token = scatter_go(swap_c, dy0, "c")
    dwco = _grad_matmul(vmix, dy0, (1, D, D), D, D, lambda i, j: (0, 0, 0), None, name="b0_dw_conv_out",
                        dep=token)
    dz, dcw = _conv_bwd(z, cw, dvmix, name="b0_conv")
    n_ci = wci.shape[2]
    dwci = _grad_matmul(hn_m0, dz, (N_CHIPS, D, n_ci), D, n_ci, lambda i, j: (j, 0, 0), None, name="b0_dw_conv_in")
    dx, d_mix_pre0 = _matmul_prenorm_bwd((dz,), wci, 0, x0, row(mix_norm_pre, 0), dh1, name="b0_conv_in")

    pack = jnp.concatenate([
        d_mix_pre0, d_mix_pre1, d_mix_post0, d_mix_post1, d_ffn_pre0, d_ffn_pre1, d_ffn_post0, d_ffn_post1,
        d_kv_norm, dcw[0:3], jnp.full((1, D), loss_part, F32),
        jnp.zeros((SMALL_ROWS - 13, D), F32)], axis=0)
    red = _all_reduce_small(pack)
    loss = red[12, 0]
    myj = 2 * lax.axis_index("x") + lax.axis_index("y")
    g_conv_w = lax.dynamic_slice(red, (9, myj * tc), (3, tc))

    zeros7 = jnp.zeros((SMALL_ROWS - 9, D), F32)
    w_small = jnp.concatenate([mix_norm_pre, mix_norm_post, ffn_norm_pre, ffn_norm_post, kv_norm.reshape(1, D), zeros7], axis=0)
    m_small = jnp.concatenate([m_mix_norm_pre, m_mix_norm_post, m_ffn_norm_pre, m_ffn_norm_post, m_kv_norm.reshape(1, D), zeros7], axis=0)
    v_small = jnp.concatenate([v_mix_norm_pre, v_mix_norm_post, v_ffn_norm_pre, v_ffn_norm_post, v_kv_norm.reshape(1, D), zeros7], axis=0)
    d_small, nm_small, nv_small = _adamw(w_small, red, m_small, v_small, name="adamw_small")

    pad5 = jnp.zeros((5, tc), F32)
    d_cw, nm_cw, nv_cw = _adamw(cw_pad, jnp.concatenate([g_conv_w, pad5], axis=0),
                                jnp.concatenate([m_conv_w[0], pad5], axis=0),
                                jnp.concatenate([v_conv_w[0], pad5], axis=0), name="adamw_conv_w")

    rs_d, _ = scatter_start([dwci, in_chips(dwco)], "d")

    pieces = {"a": [(0, D), (1, f_sh), (6, 0)], "b": [(4, 0), (5, 0)], "c": [(0, 0), (1, 0)], "d": [(2, 0), (3, 0)]}
    grads2d = [None] * len(big)
    big_out = [None] * len(big)

    def finish(groups, after, tag):
        regions, idxs = [], []
        for gtag, handle in groups:
            landed, parts = _exchange_wait(handle, after, name=f"rs_exchange_wait_{gtag}")
            for i, (l, p, (wi, off)) in enumerate(zip(landed, parts, pieces[gtag])):
                total = as2d(big[wi]).shape[0]
                grads2d[wi] = _chip_sum(l, p, where, total, off, grads2d[wi], name=f"rs_chip_sum_{gtag}{i}")
                if wi not in idxs:
                    idxs.append(wi)
                regions.append((idxs.index(wi), off, 2 * l.shape[1]))
        joined = _sibling_join([grads2d[wi] for wi in idxs], regions, name=f"rs_sibling_join_{tag}")
        for wi, gr in zip(idxs, joined):
            w = big[wi]
            d_, m_, v_, g_ = _adamw(as2d(w), gr, as2d(big_m[wi]), as2d(big_v[wi]), name=f"adamw{wi}",
                                    emit_grad=True)
            big_out[wi] = (g_.reshape(w.shape), d_.reshape(w.shape), m_.reshape(w.shape), v_.reshape(w.shape))

    finish([("a", rs_a), ("b", rs_b), ("c", rs_c)], dx, "abc")
    finish([("d", rs_d)], big_out[0][1], "d")

    def small(a):
        return (a[0:2], a[2:4], a[4:6], a[6:8])

    def assemble(sm, cwv, kind):
        pre, post, fpre, fpost = small(sm)
        b = [t[kind] for t in big_out]
        return [pre, post, fpre, fpost, b[0], b[1], b[2], cwv[0:3].reshape(conv_w.shape), b[3],
                sm[8], b[4], b[5].reshape(w_q.shape), b[6].reshape(w_o.shape)]

    grads = assemble(red, jnp.concatenate([g_conv_w, pad5], axis=0), 0)
    deltas = assemble(d_small, d_cw, 1)
    new_m = assemble(nm_small, nm_cw, 2)
    new_v = assemble(nv_small, nv_cw, 3)
    return (loss, dx.reshape(x.shape), *grads, *deltas, *new_m, *new_v)
```

```python
import functools

import jax
import jax.numpy as jnp
from jax import lax
from jax.experimental import pallas as pl
from jax.experimental.pallas import tpu as pltpu

HEAD_DIM = 64
BAND = 128
BRANCHES = ((128, 1), (512, 4), (2048, 16))
ROPE_THETA = 10000.0
RMS_EPS = 1e-6
NEG_INF = -1e30
ADAM_LR = 0.001
ADAM_B1 = 0.9
ADAM_B2 = 0.999
ADAM_EPS = 1e-08
ADAM_WD = 0.01
ADAM_STEP = 10

N_CHIPS = 4
N_DEV = 8
LANES = 128
MXU_COLS = 256
ROW_BLOCK = 512
ROW_BLOCK_WIDE = 1024
GRAD_CHUNK = 2048
ATTN_FWD_UNROLL = 16
ATTN_BWD_UNROLL = 8
ATTN_BLOCK_ROWS = 2048
VMEM_LIMIT = 56 * 1024 * 1024
SMALL_ROWS = 16
ADAMW_BLOCK_BYTES = 1024 * 1024
DMA_CHUNK_BYTES = 512 * 1024

BF = jnp.bfloat16
F32 = jnp.float32
MESH = pl.DeviceIdType.MESH
ANY = pl.BlockSpec(memory_space=pl.ANY)


def _cp(*sem):
    return pltpu.CompilerParams(dimension_semantics=sem, vmem_limit_bytes=VMEM_LIMIT)


def _rot_half(t, first):
    return jnp.where(first, pltpu.roll(t, 96, 1), pltpu.roll(t, 32, 1))


def _sigmoid(x):
    return pl.reciprocal(1.0 + jnp.exp(-x), approx=True)


def _col_chunks(n):
    return [(c0, min(MXU_COLS, n - c0)) for c0 in range(0, n, MXU_COLS)]


def _first_half_mask(rows):
    lane = lax.broadcasted_iota(jnp.int32, (rows, LANES), 1)
    return (lane % HEAD_DIM) < (HEAD_DIM // 2)


def _normed_rows(j, rows, x_ref, g_ref, xn_ref, xs, last_start, tm):
    @pl.when(j == 0)
    def _():
        xv = x_ref[...]
        r = lax.rsqrt(jnp.mean(xv * xv, axis=-1, keepdims=True) + RMS_EPS)
        xn = (xv * r * g_ref[...]).astype(BF)
        xs[rows, :] = xn
        xn_ref[...] = xn

    @pl.when(j > 0)
    def _():
        xn_ref[...] = xs[pl.ds(last_start, tm), :]


def _norm_matmul(x, gain, wg, cos2, ss2, *, name, rope_shards, scale, out_dtype):
    T, D = x.shape
    n = wg.shape[2]
    tm = min(ROW_BLOCK_WIDE, T)
    ni = T // tm

    def body(x_ref, g_ref, w_ref, cos_ref, ss_ref, y_ref, xn_ref, xs):
        j = pl.program_id(0)
        rows = pl.ds(pl.multiple_of(pl.program_id(1) * tm, tm), tm)
        _normed_rows(j, rows, x_ref, g_ref, xn_ref, xs, (ni - 1) * tm, tm)
        acc = jnp.dot(xs[rows, :], w_ref[...], preferred_element_type=F32)

        def plain():
            y_ref[...] = acc.astype(out_dtype)

        def rope():
            cosv = cos_ref[...]
            ssv = ss_ref[...]
            first = _first_half_mask(tm)
            for ci in range(n // LANES):
                t = acc[:, ci * LANES:(ci + 1) * LANES]
                y = (t * cosv + _rot_half(t, first) * ssv) * scale
                y_ref[:, ci * LANES:(ci + 1) * LANES] = y.astype(out_dtype)

        if rope_shards == 0:
            plain()
        elif rope_shards == N_CHIPS:
            rope()
        else:
            pl.when(j < rope_shards)(rope)
            pl.when(j >= rope_shards)(plain)

    first_pass = lambda j, i: (jnp.where(j == 0, i, ni - 1), 0)
    return pl.pallas_call(
        body, name=name,
        grid=(N_CHIPS, ni),
        in_specs=[
            pl.BlockSpec((tm, D), first_pass),
            pl.BlockSpec((1, D), lambda j, i: (0, 0)),
            pl.BlockSpec((None, D, n), lambda j, i: (j, 0, 0)),
            pl.BlockSpec((tm, LANES), lambda j, i: (i, 0)),
            pl.BlockSpec((tm, LANES), lambda j, i: (i, 0)),
        ],
        out_specs=[
            pl.BlockSpec((tm, n), lambda j, i: (i, j)),
            pl.BlockSpec((tm, D), first_pass),
        ],
        out_shape=[jax.ShapeDtypeStruct((T, N_CHIPS * n), out_dtype),
                   jax.ShapeDtypeStruct((T, D), BF)],
        scratch_shapes=[pltpu.VMEM((T, D), BF)],
        compiler_params=_cp("arbitrary", "arbitrary"),
    )(x, gain, wg, cos2, ss2)


def _norm_swiglu(x, gain, wg, layer, *, name):
    T, D = x.shape
    n = wg.shape[2]
    tm = min(ROW_BLOCK, T)

    ni = T // tm

    def body(x_ref, g_ref, wg_ref, wu_ref, go_ref, uo_ref, ao_ref, xn_ref, xs):
        j = pl.program_id(0)
        rows = pl.ds(pl.multiple_of(pl.program_id(1) * tm, tm), tm)
        _normed_rows(j, rows, x_ref, g_ref, xn_ref, xs, (ni - 1) * tm, tm)
        g = jnp.dot(xs[rows, :], wg_ref[...], preferred_element_type=F32)
        u = jnp.dot(xs[rows, :], wu_ref[...], preferred_element_type=F32)
        go_ref[...] = g.astype(BF)
        uo_ref[...] = u.astype(BF)
        ao_ref[...] = (g * _sigmoid(g) * u).astype(BF)

    half = N_CHIPS // 2
    first_pass = lambda j, i: (jnp.where(j == 0, i, ni - 1), 0)
    act = jax.ShapeDtypeStruct((T, half * n), BF)
    return pl.pallas_call(
        body, name=name,
        grid=(half, ni),
        in_specs=[
            pl.BlockSpec((tm, D), first_pass),
            pl.BlockSpec((1, D), lambda j, i: (0, 0)),
            pl.BlockSpec((None, D, n), lambda j, i: (j, layer, 0)),
            pl.BlockSpec((None, D, n), lambda j, i: (j + half, layer, 0)),
        ],
        out_specs=[
            pl.BlockSpec((tm, n), lambda j, i: (i, j)),
            pl.BlockSpec((tm, n), lambda j, i: (i, j)),
            pl.BlockSpec((tm, n), lambda j, i: (i, j)),
            pl.BlockSpec((tm, D), first_pass),
        ],
        out_shape=[act, act, act, jax.ShapeDtypeStruct((T, D), BF)],
        scratch_shapes=[pltpu.VMEM((T, D), BF)],
        compiler_params=_cp("arbitrary", "arbitrary"),
    )(x, gain, wg, wg)


def _matmul_postnorm(a, w3, widx, gain, h_old, *, name):
    T, K = a.shape
    D = w3.shape[2]
    tm = min(ROW_BLOCK_WIDE, T)

    def body(a_ref, w_ref, g_ref, h_ref, y_ref, hn_ref):
        y = jnp.dot(a_ref[...].astype(BF), w_ref[...], preferred_element_type=F32)
        y_ref[...] = y.astype(BF)
        r = lax.rsqrt(jnp.mean(y * y, axis=-1, keepdims=True) + RMS_EPS)
        hn_ref[...] = h_ref[...] + y * r * g_ref[...]

    return pl.pallas_call(
        body, name=name,
        grid=(T // tm,),
        in_specs=[
            pl.BlockSpec((tm, K), lambda i: (i, 0)),
            pl.BlockSpec((None, K, D), lambda i: (widx, 0, 0)),
            pl.BlockSpec((1, D), lambda i: (0, 0)),
            pl.BlockSpec((tm, D), lambda i: (i, 0)),
        ],
        out_specs=[pl.BlockSpec((tm, D), lambda i: (i, 0)),
                   pl.BlockSpec((tm, D), lambda i: (i, 0))],
        out_shape=[jax.ShapeDtypeStruct((T, D), BF), jax.ShapeDtypeStruct((T, D), F32)],
        compiler_params=_cp("parallel"),
    )(a, w3, gain, h_old)


def _matmul_postnorm_loss(a, w3, widx, gain, h_old, target, *, name):
    T, K = a.shape
    D = w3.shape[2]
    tm = min(ROW_BLOCK, T)

    def body(a_ref, w_ref, g_ref, h_ref, t_ref, y_ref, dh_ref, s_ref):
        @pl.when(pl.program_id(0) == 0)
        def _():
            s_ref[...] = jnp.zeros_like(s_ref)

        y = jnp.dot(a_ref[...].astype(BF), w_ref[...], preferred_element_type=F32)
        y_ref[...] = y.astype(BF)
        r = lax.rsqrt(jnp.mean(y * y, axis=-1, keepdims=True) + RMS_EPS)
        e = (h_ref[...] + y * r * g_ref[...]) - t_ref[...]
        dh_ref[...] = e * (1.0 / D)
        s_ref[...] += jnp.sum(e * e)

    rows = pl.BlockSpec((tm, D), lambda i: (i, 0))
    return pl.pallas_call(
        body, name=name,
        grid=(T // tm,),
        in_specs=[
            pl.BlockSpec((tm, K), lambda i: (i, 0)),
            pl.BlockSpec((None, K, D), lambda i: (widx, 0, 0)),
            pl.BlockSpec((1, D), lambda i: (0, 0)),
            rows, rows,
        ],
        out_specs=[rows, rows, pl.BlockSpec((8, LANES), lambda i: (0, 0))],
        out_shape=[jax.ShapeDtypeStruct((T, D), BF), jax.ShapeDtypeStruct((T, D), F32),
                   jax.ShapeDtypeStruct((8, LANES), F32)],
        compiler_params=_cp("arbitrary"),
    )(a, w3, gain, h_old, target)


def _shift_down(u, k):
    row = lax.broadcasted_iota(jnp.int32, u.shape, 0)
    return jnp.where(row >= k, pltpu.roll(u, k, 0), 0.0)


def _shift_up(u, k):
    T = u.shape[0]
    row = lax.broadcasted_iota(jnp.int32, u.shape, 0)
    return jnp.where(row < T - k, pltpu.roll(u, T - k, 0), 0.0)


def _conv_fwd(z, cw, *, name):
    T = z.shape[0]
    D = z.shape[1] // 3
    tc = cw.shape[2]
    nb = D // tc

    def body(b_ref, c_ref, h_ref, w_ref, o_ref):
        u = c_ref[...].astype(F32) * h_ref[...].astype(F32)
        w = w_ref[...]
        conv = w[2:3] * u + w[1:2] * _shift_down(u, 1) + w[0:1] * _shift_down(u, 2)
        o_ref[...] = (b_ref[...].astype(F32) * conv).astype(BF)

    return pl.pallas_call(
        body, name=name,
        grid=(nb,),
        in_specs=[
            pl.BlockSpec((T, tc), lambda j: (0, j)),
            pl.BlockSpec((T, tc), lambda j: (0, nb + j)),
            pl.BlockSpec((T, tc), lambda j: (0, 2 * nb + j)),
            pl.BlockSpec((None, 8, tc), lambda j: (j, 0, 0)),
        ],
        out_specs=pl.BlockSpec((T, tc), lambda j: (0, j)),
        out_shape=jax.ShapeDtypeStruct((T, D), BF),
        compiler_params=_cp("parallel"),
    )(z, z, z, cw)


def _conv_bwd(z, cw, dv, *, name):
    T = z.shape[0]
    D = z.shape[1] // 3
    tc = LANES
    nb = D // tc
    per = cw.shape[2] // tc

    def body(b_ref, c_ref, h_ref, w_ref, dv_ref, dz_ref, dw_ref, stage, sems):
        j = pl.program_id(0)
        slot = j % 2

        def slab(p, jj, s):
            col = pl.multiple_of((p * nb + jj) * tc, tc)
            return pltpu.make_async_copy(stage.at[s, p], dz_ref.at[:, pl.ds(col, tc)], sems.at[s, p])

        @pl.when(j >= 2)
        def _():
            for p in range(3):
                slab(p, j - 2, slot).wait()

        c = c_ref[...].astype(F32)
        h = h_ref[...].astype(F32)
        u = c * h
        u1 = _shift_down(u, 1)
        u2 = _shift_down(u, 2)
        w = w_ref[...]
        dvv = dv_ref[...].astype(F32)
        dconv = dvv * b_ref[...].astype(F32)
        du = w[2:3] * dconv + w[1:2] * _shift_up(dconv, 1) + w[0:1] * _shift_up(dconv, 2)
        rows = lax.broadcasted_iota(jnp.int32, (8, tc), 0)
        dw_ref[...] = jnp.where(rows == 0, jnp.sum(dconv * u2, axis=0, keepdims=True),
                                jnp.where(rows == 1, jnp.sum(dconv * u1, axis=0, keepdims=True),
                                          jnp.where(rows == 2, jnp.sum(dconv * u, axis=0, keepdims=True), 0.0)))
        stage[slot, 0] = (dvv * (w[2:3] * u + w[1:2] * u1 + w[0:1] * u2)).astype(BF)
        stage[slot, 1] = (du * h).astype(BF)
        stage[slot, 2] = (du * c).astype(BF)
        for p in range(3):
            slab(p, j, slot).start()

        @pl.when(j == nb - 1)
        def _():
            for p in range(3):
                slab(p, j, slot).wait()
            if nb > 1:
                for p in range(3):
                    slab(p, j - 1, 1 - slot).wait()

    return pl.pallas_call(
        body, name=name,
        grid=(nb,),
        in_specs=[
            pl.BlockSpec((T, tc), lambda j: (0, j)),
            pl.BlockSpec((T, tc), lambda j: (0, nb + j)),
            pl.BlockSpec((T, tc), lambda j: (0, 2 * nb + j)),
            pl.BlockSpec((None, 8, tc), lambda j: (j // per, 0, j % per)),
            pl.BlockSpec((T, tc), lambda j: (0, j)),
        ],
        out_specs=[ANY, pl.BlockSpec((8, tc), lambda j: (0, j))],
        out_shape=[jax.ShapeDtypeStruct((T, 3 * D), BF), jax.ShapeDtypeStruct((8, D), F32)],
        scratch_shapes=[pltpu.VMEM((2, 3, T, tc), BF), pltpu.SemaphoreType.DMA((2, 3))],
        compiler_params=_cp("arbitrary"),
    )(z, z, z, cw, dv)


def _strided(base, count, d):
    return pl.ds(base, count, stride=d) if d > 1 else pl.ds(pl.multiple_of(base, BAND), count)


def _fill_band_bias(bias):
    qi = lax.broadcasted_iota(jnp.int32, (2 * BAND, 2 * BAND), 0) % BAND
    kj = lax.broadcasted_iota(jnp.int32, (2 * BAND, 2 * BAND), 1)
    dist = qi + BAND - kj
    band = (dist >= 0) & (dist <= BAND)
    bias[0] = jnp.where(band & (kj >= BAND), 0.0, NEG_INF)
    bias[1] = jnp.where(band, 0.0, NEG_INF)


def _attn_block_rows(T):
    return min(ATTN_BLOCK_ROWS, T)


def _head_mask():
    lane = lax.broadcasted_iota(jnp.int32, (2 * BAND, LANES), 1)
    row = lax.broadcasted_iota(jnp.int32, (2 * BAND, LANES), 0)
    return (lane < HEAD_DIM) == (row < BAND)


def _attn_fwd(q_all, kv_all, *, name):
    T = q_all.shape[0]
    NB = len(BRANCHES)
    Dm = q_all.shape[1] // NB
    HP = Dm // LANES
    R = _attn_block_rows(T)
    units = R // BAND
    dmax = max(d for _, d in BRANCHES)

    def body(*refs):
        ins = refs[:5 * NB]
        o_ref, l_ref, qbuf, kbuf, vbuf, o_s, l_s, bias = refs[5 * NB:]
        n = pl.program_id(0)
        pl.when((n == 0) & (pl.program_id(1) == 0))(lambda: _fill_band_bias(bias))
        hm = _head_mask()
        low = lax.broadcasted_iota(jnp.int32, (BAND, LANES), 1) < HEAD_DIM

        for g, (_, d) in enumerate(BRANCHES):
            q_ref, kp_ref, kc_ref, vp_ref, vc_ref = ins[5 * g:5 * g + 5]
            pr = BAND * d
            qbuf[...] = q_ref[...].astype(F32)
            kbuf[0:pr, :] = kp_ref[...].astype(F32)
            kbuf[pr:pr + R, :] = kc_ref[...].astype(F32)
            vbuf[0:pr, :] = vp_ref[...].astype(F32)
            vbuf[pr:pr + R, :] = vc_ref[...].astype(F32)

            def unit(u, carry, g=g, d=d, pr=pr):
                sub = u // d
                base = sub * pr + (u - sub * d)
                q = qbuf[_strided(base, BAND, d), :]
                q2 = jnp.where(hm, jnp.concatenate([q, q], axis=0), 0.0).astype(BF)
                k2 = kbuf[_strided(base, 2 * BAND, d), :].astype(BF)
                v2 = vbuf[_strided(base, 2 * BAND, d), :].astype(BF)
                s = lax.dot_general(q2, k2, (((1,), (1,)), ((), ())), preferred_element_type=F32)
                s = s + bias[((n > 0) | (sub > 0)).astype(jnp.int32)]
                m = jnp.max(s, axis=-1, keepdims=True)
                p = jnp.exp(s - m)
                l = jnp.sum(p, axis=-1, keepdims=True)
                pv = jnp.dot(p.astype(BF), v2, preferred_element_type=F32) * (1.0 / l)
                lse = m + jnp.log(l)
                o_s[g, _strided(base, BAND, d), :] = jnp.where(low, pv[:BAND], pv[BAND:])
                l_s[g, _strided(base, BAND, d), :] = jnp.where(low, lse[:BAND], lse[BAND:])
                return carry

            lax.fori_loop(0, units, unit, 0, unroll=min(ATTN_FWD_UNROLL, units))

        def merge(i, carry):
            sl = pl.ds(pl.multiple_of(i * BAND, BAND), BAND)
            lv = [l_s[g, sl, :] for g in range(NB)]
            m = functools.reduce(jnp.maximum, lv)
            e = [jnp.exp(v - m) for v in lv]
            tot = functools.reduce(jnp.add, e)
            inv = 1.0 / tot
            o_ref[sl, :] = functools.reduce(jnp.add, [(e[g] * inv) * o_s[g, sl, :] for g in range(NB)])
            l_ref[sl, :] = m + jnp.log(tot)
            return carry

        lax.fori_loop(0, units, merge, 0)

    in_specs, args = [], []
    for g, (_, d) in enumerate(BRANCHES):
        per = R // (BAND * d)
        for col, rows, idx in (
                (g * HP, R, lambda n, hp: n),
                (g * HP, BAND * d, lambda n, hp, per=per: jnp.maximum(n * per - 1, 0)),
                (g * HP, R, lambda n, hp: n),
                ((NB + g) * HP, BAND * d, lambda n, hp, per=per: jnp.maximum(n * per - 1, 0)),
                ((NB + g) * HP, R, lambda n, hp: n)):
            in_specs.append(pl.BlockSpec((rows, LANES), lambda n, hp, col=col, idx=idx: (idx(n, hp), col + hp)))
        args += [q_all, kv_all, kv_all, kv_all, kv_all]
    out = pl.BlockSpec((R, LANES), lambda n, hp: (n, hp))
    return pl.pallas_call(
        body, name=name,
        grid=(T // R, HP),
        in_specs=in_specs,
        out_specs=[out, out],
        out_shape=[jax.ShapeDtypeStruct((T, Dm), F32), jax.ShapeDtypeStruct((T, Dm), F32)],
        scratch_shapes=[pltpu.VMEM((R, LANES), F32),
                        pltpu.VMEM((BAND * dmax + R, LANES), F32), pltpu.VMEM((BAND * dmax + R, LANES), F32),
                        pltpu.VMEM((NB, R, LANES), F32), pltpu.VMEM((NB, R, LANES), F32),
                        pltpu.VMEM((2, 2 * BAND, 2 * BAND), F32)],
        compiler_params=_cp("arbitrary", "arbitrary"),
    )(*args)


def _attn_bwd(q_all, kv_all, do, o, lse, cos2, ss2, g, d, prev, *, name, dep=None):
    T = q_all.shape[0]
    NB = len(BRANCHES)
    Dm = q_all.shape[1] // NB
    HP = Dm // LANES
    R = _attn_block_rows(T)
    nblk = T // R
    units = R // BAND
    pr = BAND * d
    per = R // pr
    scale = HEAD_DIM ** -0.5

    def rope_bwd(t, cosv, ssv, first):
        return t * cosv - _rot_half(t, first) * ssv

    def body(q_ref, kp_ref, kc_ref, vp_ref, vc_ref, do_ref, o_ref, l_ref, cos_ref, ss_ref, *rest):
        dq_ref, dk_ref, dv_ref, qbuf, kbuf, vbuf, dq_s, dk_s, dv_s, pend_k, pend_v, bias = rest[-12:]
        i = pl.program_id(1)
        n = nblk - 1 - i
        pl.when((i == 0) & (pl.program_id(0) == 0))(lambda: _fill_band_bias(bias))
        hm = _head_mask()
        low = lax.broadcasted_iota(jnp.int32, (BAND, LANES), 1) < HEAD_DIM

        qbuf[...] = q_ref[...].astype(F32)
        kbuf[0:pr, :] = kp_ref[...].astype(F32)
        kbuf[pr:pr + R, :] = kc_ref[...].astype(F32)
        vbuf[0:pr, :] = vp_ref[...].astype(F32)
        vbuf[pr:pr + R, :] = vc_ref[...].astype(F32)

        @pl.when(i == 0)
        def _():
            pend_k[...] = jnp.zeros_like(pend_k)
            pend_v[...] = jnp.zeros_like(pend_v)

        def unit(u, carry):
            sub = per - 1 - u // d
            cls = u % d
            base = sub * pr + cls
            sl = _strided(base, BAND, d)
            sl2 = _strided(base, 2 * BAND, d)
            q = qbuf[sl, :]
            dov = do_ref[sl, :]
            ov = o_ref[sl, :]
            lv = l_ref[sl, :]
            q2 = jnp.where(hm, jnp.concatenate([q, q], axis=0), 0.0).astype(BF)
            do2 = jnp.where(hm, jnp.concatenate([dov, dov], axis=0), 0.0)
            oo = dov * ov
            delta = jnp.sum(jnp.where(hm, jnp.concatenate([oo, oo], axis=0), 0.0), axis=-1, keepdims=True)
            lse2 = jnp.concatenate([lv[:, 0:1], lv[:, HEAD_DIM:HEAD_DIM + 1]], axis=0)
            do2 = do2.astype(BF)
            k2 = kbuf[sl2, :].astype(BF)
            v2 = vbuf[sl2, :].astype(BF)
            s = lax.dot_general(q2, k2, (((1,), (1,)), ((), ())), preferred_element_type=F32)
            p = jnp.exp(s + bias[((n > 0) | (sub > 0)).astype(jnp.int32)] - lse2)
            dp = lax.dot_general(do2, v2, (((1,), (1,)), ((), ())), preferred_element_type=F32)
            ds = (p * (dp - delta)).astype(BF)
            dq2 = jnp.dot(ds, k2, preferred_element_type=F32)
            dq = jnp.where(low, dq2[:BAND], dq2[BAND:])
            dq_s[sl, :] = dq
            tn = (((0,), (0,)), ((), ()))
            dk2 = lax.dot_general(ds, q2, tn, preferred_element_type=F32)
            dv2 = lax.dot_general(p.astype(BF), do2, tn, preferred_element_type=F32)
            dk_s[sl, :] = dk2[BAND:] + pend_k[cls]
            dv_s[sl, :] = dv2[BAND:] + pend_v[cls]
            pend_k[cls] = dk2[:BAND]
            pend_v[cls] = dv2[:BAND]
            return carry

        lax.fori_loop(0, units, unit, 0, unroll=min(ATTN_BWD_UNROLL, units))

        whole = _first_half_mask(R)
        dq_ref[...] = (rope_bwd(dq_s[...], cos_ref[...], ss_ref[...], whole) * scale).astype(BF)
        dk_ref[...] = rope_bwd(dk_s[...], cos_ref[...], ss_ref[...], whole).astype(BF)
        dv_ref[...] = dv_s[...].astype(BF)

    blk = (R, LANES)
    pblk = (pr, LANES)
    cur = lambda hp, i: nblk - 1 - i
    prv = lambda hp, i: jnp.maximum((nblk - 1 - i) * per - 1, 0)
    in_specs = [
        pl.BlockSpec(blk, lambda hp, i: (cur(hp, i), g * HP + hp)),
        pl.BlockSpec(pblk, lambda hp, i: (prv(hp, i), g * HP + hp)),
        pl.BlockSpec(blk, lambda hp, i: (cur(hp, i), g * HP + hp)),
        pl.BlockSpec(pblk, lambda hp, i: (prv(hp, i), (NB + g) * HP + hp)),
        pl.BlockSpec(blk, lambda hp, i: (cur(hp, i), (NB + g) * HP + hp)),
        pl.BlockSpec(blk, lambda hp, i: (cur(hp, i), hp)),
        pl.BlockSpec(blk, lambda hp, i: (cur(hp, i), hp)),
        pl.BlockSpec(blk, lambda hp, i: (cur(hp, i), hp)),
        pl.BlockSpec(blk, lambda hp, i: (cur(hp, i), 0)),
        pl.BlockSpec(blk, lambda hp, i: (cur(hp, i), 0)),
    ]
    args = [q_all, kv_all, kv_all, kv_all, kv_all, do, o, lse, cos2, ss2]
    if dep is not None:
        in_specs.append(ANY)
        args.append(dep)
    aliases = {}
    if prev is not None:
        in_specs += [ANY, ANY, ANY]
        aliases = {len(args): 0, len(args) + 1: 1, len(args) + 2: 2}
        args += list(prev)
    wide = jax.ShapeDtypeStruct((T, NB * Dm), BF)
    out = pl.BlockSpec(blk, lambda hp, i: (cur(hp, i), g * HP + hp))
    return pl.pallas_call(
        body, name=name,
        grid=(HP, nblk),
        in_specs=in_specs,
        out_specs=[out, out, out],
        out_shape=[wide, wide, wide],
        scratch_shapes=[pltpu.VMEM(blk, F32), pltpu.VMEM((pr + R, LANES), F32), pltpu.VMEM((pr + R, LANES), F32),
                        pltpu.VMEM(blk, F32), pltpu.VMEM(blk, F32), pltpu.VMEM(blk, F32),
                        pltpu.VMEM((d, BAND, LANES), F32), pltpu.VMEM((d, BAND, LANES), F32),
                        pltpu.VMEM((2, 2 * BAND, 2 * BAND), F32)],
        input_output_aliases=aliases,
        compiler_params=_cp("arbitrary", "arbitrary"),
    )(*args)


def _postnorm_bwd(dh, y, g_ref_val):
    r = lax.rsqrt(jnp.mean(y * y, axis=-1, keepdims=True) + RMS_EPS)
    yn = y * r
    dyn = dh * g_ref_val
    dy = r * (dyn - yn * jnp.mean(dyn * yn, axis=-1, keepdims=True))
    return dy, yn


def _after(body, n_in, dep):
    if dep is None:
        return body
    return lambda *refs: body(*refs[:n_in], *refs[n_in + 1:])


def _dep_spec(dep):
    return [] if dep is None else [ANY]


def _dep_arg(dep):
    return [] if dep is None else [dep]


def _postnorm_bwd_matmul(dh, y, gain, w3, widx, *, name, da_dtype, dep=None):
    T, D = dh.shape
    K = w3.shape[1]
    tm = min(ROW_BLOCK_WIDE, T)

    def body(dh_ref, y_ref, g_ref, w_ref, dy_ref, da_ref, dg_ref):
        i = pl.program_id(0)

        @pl.when(i == 0)
        def _():
            dg_ref[...] = jnp.zeros_like(dg_ref)

        dhv = dh_ref[...]
        dy, yn = _postnorm_bwd(dhv, y_ref[...].astype(F32), g_ref[...])
        dg_ref[...] += jnp.sum(dhv * yn, axis=0, keepdims=True)
        dyb = dy.astype(BF)
        dy_ref[...] = dyb
        da = lax.dot_general(dyb, w_ref[...], (((1,), (1,)), ((), ())), preferred_element_type=F32)
        da_ref[...] = da.astype(da_dtype)

    return pl.pallas_call(
        _after(body, 4, dep), name=name,
        grid=(T // tm,),
        in_specs=[
            pl.BlockSpec((tm, D), lambda i: (i, 0)),
            pl.BlockSpec((tm, D), lambda i: (i, 0)),
            pl.BlockSpec((1, D), lambda i: (0, 0)),
            pl.BlockSpec((None, K, D), lambda i: (widx, 0, 0)),
        ] + _dep_spec(dep),
        out_specs=[pl.BlockSpec((tm, D), lambda i: (i, 0)),
                   pl.BlockSpec((tm, K), lambda i: (i, 0)),
                   pl.BlockSpec((1, D), lambda i: (0, 0))],
        out_shape=[jax.ShapeDtypeStruct((T, D), BF), jax.ShapeDtypeStruct((T, K), da_dtype),
                   jax.ShapeDtypeStruct((1, D), F32)],
        compiler_params=_cp("arbitrary"),
    )(dh, y, gain, w3, *_dep_arg(dep))


def _postnorm_bwd_swiglu(dh, y, gain, wd3, layer, g, u, *, name, dep=None):
    T, D = dh.shape
    F = wd3.shape[1]
    nf = F // 2
    tm = min(ROW_BLOCK, T)

    def body(dh_ref, y_ref, g_ref, w_ref, gg_ref, uu_ref, dy_ref, dgo_ref, duo_ref, dgain_ref, dys):
        i = pl.program_id(0)
        j = pl.program_id(1)

        @pl.when((i == 0) & (j == 0))
        def _():
            dgain_ref[...] = jnp.zeros_like(dgain_ref)

        @pl.when(j == 0)
        def _():
            dhv = dh_ref[...]
            dy, yn = _postnorm_bwd(dhv, y_ref[...].astype(F32), g_ref[...])
            dgain_ref[...] += jnp.sum(dhv * yn, axis=0, keepdims=True)
            dyb = dy.astype(BF)
            dys[...] = dyb
            dy_ref[...] = dyb

        for c0, cw in _col_chunks(nf):
            cols = slice(c0, c0 + cw)
            da = lax.dot_general(dys[...], w_ref[cols, :], (((1,), (1,)), ((), ())), preferred_element_type=F32)
            gv = gg_ref[:, cols].astype(F32)
            uv = uu_ref[:, cols].astype(F32)
            sg = _sigmoid(gv)
            silu = gv * sg
            dgo_ref[:, cols] = (da * uv * (sg + silu * (1.0 - sg))).astype(BF)
            duo_ref[:, cols] = (da * silu).astype(BF)

    act = jax.ShapeDtypeStruct((T, F), BF)
    return pl.pallas_call(
        _after(body, 6, dep), name=name,
        grid=(T // tm, 2),
        in_specs=[
            pl.BlockSpec((tm, D), lambda i, j: (i, 0)),
            pl.BlockSpec((tm, D), lambda i, j: (i, 0)),
            pl.BlockSpec((1, D), lambda i, j: (0, 0)),
            pl.BlockSpec((None, nf, D), lambda i, j: (layer, j, 0)),
            pl.BlockSpec((tm, nf), lambda i, j: (i, j)),
            pl.BlockSpec((tm, nf), lambda i, j: (i, j)),
        ] + _dep_spec(dep),
        out_specs=[pl.BlockSpec((tm, D), lambda i, j: (i, 0)),
                   pl.BlockSpec((tm, nf), lambda i, j: (i, j)),
                   pl.BlockSpec((tm, nf), lambda i, j: (i, j)),
                   pl.BlockSpec((1, D), lambda i, j: (0, 0))],
        out_shape=[jax.ShapeDtypeStruct((T, D), BF), act, act, jax.ShapeDtypeStruct((1, D), F32)],
        scratch_shapes=[pltpu.VMEM((tm, D), BF)],
        compiler_params=_cp("arbitrary", "arbitrary"),
    )(dh, y, gain, wd3, g, u, *_dep_arg(dep))


def _matmul_prenorm_bwd(dzs, wg, layer, h, gain, dh_in, *, name):
    T, D = h.shape
    n = wg.shape[2]
    tm = min(ROW_BLOCK, T)
    per = N_CHIPS // len(dzs)

    def body(*refs):
        dz_refs = refs[:len(dzs)]
        w_ref, h_ref, g_ref, dhi_ref, dh_ref, dg_ref = refs[len(dzs):]

        @pl.when(pl.program_id(0) == 0)
        def _():
            dg_ref[...] = jnp.zeros_like(dg_ref)

        dhn = None
        for j in range(N_CHIPS):
            dz = dz_refs[j // per][:, (j % per) * n:(j % per + 1) * n]
            t = lax.dot_general(dz.astype(BF), w_ref[j], (((1,), (1,)), ((), ())), preferred_element_type=F32)
            dhn = t if dhn is None else dhn + t
        hv = h_ref[...]
        r = lax.rsqrt(jnp.mean(hv * hv, axis=-1, keepdims=True) + RMS_EPS)
        xh = hv * r
        dg_ref[...] += jnp.sum(dhn * xh, axis=0, keepdims=True)
        dxn = dhn * g_ref[...]
        dh_ref[...] = dhi_ref[...] + r * (dxn - xh * jnp.mean(dxn * xh, axis=-1, keepdims=True))

    rows = pl.BlockSpec((tm, D), lambda i: (i, 0))
    in_specs = [pl.BlockSpec((tm, per * n), lambda i: (i, 0)) for _ in dzs]
    in_specs += [pl.BlockSpec((N_CHIPS, D, n), lambda i: (0, layer, 0), pipeline_mode=pl.Buffered(1)),
                 rows, pl.BlockSpec((1, D), lambda i: (0, 0)), rows]
    return pl.pallas_call(
        body, name=name,
        grid=(T // tm,),
        in_specs=in_specs,
        out_specs=[rows, pl.BlockSpec((1, D), lambda i: (0, 0))],
        out_shape=[jax.ShapeDtypeStruct((T, D), F32), jax.ShapeDtypeStruct((1, D), F32)],
        compiler_params=_cp("arbitrary"),
    )(*dzs, wg, h, gain, dh_in)


def _grad_matmul(a, b, out_shape3, tme, tne, out_index, prev, *, name, dep=None):
    T, M = a.shape
    N = b.shape[1]
    tk = min(GRAD_CHUNK, T)
    nk = T // tk

    def body(a_ref, b_ref, *rest):
        o_ref, acc = rest[-2:]
        k = pl.program_id(2)
        part = jnp.dot(a_ref[...].astype(BF).T, b_ref[...].astype(BF), preferred_element_type=F32)

        @pl.when(k == 0)
        def _():
            acc[...] = part

        @pl.when(k > 0)
        def _():
            acc[...] += part

        @pl.when(k == nk - 1)
        def _():
            o_ref[...] = acc[...].astype(BF)

    in_specs = [pl.BlockSpec((tk, tme), lambda i, j, k: (k, i)),
                pl.BlockSpec((tk, tne), lambda i, j, k: (k, j))]
    args = [a, b]
    aliases = {}
    if prev is not None:
        in_specs.append(ANY)
        args.append(prev)
        aliases = {2: 0}
    in_specs += _dep_spec(dep)
    args += _dep_arg(dep)
    return pl.pallas_call(
        body, name=name,
        grid=(M // tme, N // tne, nk),
        in_specs=in_specs,
        out_specs=pl.BlockSpec((None, tme, tne), lambda i, j, k: out_index(i, j)),
        out_shape=jax.ShapeDtypeStruct(out_shape3, BF),
        scratch_shapes=[pltpu.VMEM((tme, tne), F32)],
        input_output_aliases=aliases,
        compiler_params=_cp("parallel", "parallel", "arbitrary"),
    )(*args)


def _row_tile(R, cap=512):
    fit = [t for t in range(16, min(R, cap) + 1, 16) if R % t == 0]
    return max(fit) if fit else R


def _cast_place(w3, layer, where, dtype, *, name, dep=None):
    _, R, C = w3.shape
    tr = _row_tile(R)

    def body(s_ref, w_ref, o_ref):
        o_ref[...] = w_ref[...].astype(o_ref.dtype)

    return pl.pallas_call(
        _after(body, 2, dep), name=name,
        grid_spec=pltpu.PrefetchScalarGridSpec(
            num_scalar_prefetch=1, grid=(R // tr,),
            in_specs=[pl.BlockSpec((None, tr, C), lambda i, s: (layer, i, 0))] + _dep_spec(dep),
            out_specs=pl.BlockSpec((None, tr, C), lambda i, s: (s[0], i, 0))),
        out_shape=jax.ShapeDtypeStruct((N_CHIPS, R, C), dtype),
        compiler_params=_cp("arbitrary"),
    )(where, w3, *_dep_arg(dep))


def _pair_sum(dw, theirs, where, *, name):
    G, rh, C = theirs.shape
    tr = _row_tile(rh)
    nr = rh // tr

    def body(s_ref, a_ref, b_ref, o_ref):
        o_ref[...] = (a_ref[...].astype(F32) + b_ref[...].astype(F32)).astype(BF)

    mine = pl.BlockSpec((None, tr, C), lambda g, i, s: (g, s[1] * nr + i, 0))
    spec = pl.BlockSpec((None, tr, C), lambda g, i, s: (g, i, 0))
    return pl.pallas_call(
        body, name=name,
        grid_spec=pltpu.PrefetchScalarGridSpec(
            num_scalar_prefetch=1, grid=(G, nr), in_specs=[mine, spec], out_specs=spec),
        out_shape=jax.ShapeDtypeStruct((G, rh, C), BF),
        compiler_params=_cp("arbitrary", "arbitrary"),
    )(where, dw, theirs)


def _chip_sum(landed, parts, where, total_rows, row_off, prev, *, name):
    G, rh, C = landed.shape
    tr = _row_tile(rh)
    nr = rh // tr
    base = row_off // tr

    def body(s_ref, l_ref, p_ref, *rest):
        o_ref = rest[-1]
        for j in range(G):
            def own(j=j):
                v = p_ref[...].astype(F32)
                o_ref[...] = v if j == 0 else o_ref[...] + v

            def other(j=j):
                v = l_ref[j].astype(F32)
                o_ref[...] = v if j == 0 else o_ref[...] + v

            pl.when(s_ref[0] == j)(own)
            pl.when(s_ref[0] != j)(other)

    in_specs = [pl.BlockSpec((G, tr, C), lambda i, s: (0, i, 0)),
                pl.BlockSpec((None, tr, C), lambda i, s: (s[0], i, 0))]
    args = [where, landed, parts]
    aliases = {}
    if prev is not None:
        in_specs.append(ANY)
        args.append(prev)
        aliases = {3: 0}
    return pl.pallas_call(
        body, name=name,
        grid_spec=pltpu.PrefetchScalarGridSpec(
            num_scalar_prefetch=1, grid=(nr,),
            in_specs=in_specs,
            out_specs=pl.BlockSpec((tr, C), lambda i, s: (base + s[1] * nr + i, 0))),
        out_shape=jax.ShapeDtypeStruct((total_rows, C), F32),
        input_output_aliases=aliases,
        compiler_params=_cp("arbitrary"),
    )(*args)


def _adamw(w, g, m, v, *, name, emit_grad=False):
    R, C = w.shape
    tr = _row_tile(R, cap=max(16, ADAMW_BLOCK_BYTES // (4 * C)))
    n_out = 4 if emit_grad else 3

    def body(w_ref, g_ref, m_ref, v_ref, d_ref, mo_ref, vo_ref, *go_ref):
        gv = g_ref[...]
        if emit_grad:
            go_ref[0][...] = gv
        mn = ADAM_B1 * m_ref[...] + (1.0 - ADAM_B1) * gv
        vn = ADAM_B2 * v_ref[...] + (1.0 - ADAM_B2) * jnp.square(gv)
        m_hat = mn / (1.0 - ADAM_B1 ** ADAM_STEP)
        v_hat = vn / (1.0 - ADAM_B2 ** ADAM_STEP)
        d_ref[...] = -ADAM_LR * (m_hat / (jnp.sqrt(v_hat) + ADAM_EPS) + ADAM_WD * w_ref[...])
        mo_ref[...] = mn
        vo_ref[...] = vn

    spec = pl.BlockSpec((tr, C), lambda i: (i, 0))
    shp = jax.ShapeDtypeStruct((R, C), F32)
    return pl.pallas_call(
        body, name=name, grid=(R // tr,), in_specs=[spec] * 4, out_specs=[spec] * n_out,
        out_shape=[shp] * n_out, compiler_params=_cp("parallel"),
    )(w, g, m, v)


def _place():
    x = lax.axis_index("x")
    y = lax.axis_index("y")
    c = lax.axis_index("c")
    chips = [(1 - x, y), (x, 1 - y), (1 - x, 1 - y)]
    return x, y, c, chips


def _chunk_rows(rows, row_bytes, align):
    if rows <= align:
        return rows
    cands = [r for r in range(align, rows + 1, align) if rows % r == 0]
    fit = [r for r in cands if r * row_bytes <= DMA_CHUNK_BYTES]
    return max(fit) if fit else min(cands)


def _row_align(dtype):
    return 8 * (4 // jnp.dtype(dtype).itemsize)


def _start_chunks(make, rows, rc):
    for r0 in range(0, rows, rc):
        make(r0, rc).start()


def _piece_rows(ref, piece, j, h, r0=0, n=None):
    _, lead, off, rows = piece
    rh = rows // 2
    n = rh if n is None else n
    if lead is not None:
        return ref.at[lead, j, pl.ds(h * rh + r0, n)]
    return ref.at[j, pl.ds(off + h * rh + r0, n)]


def _piece_chunk(arr, piece):
    rh = piece[3] // 2
    return rh, _chunk_rows(rh, arr.shape[-1] * arr.dtype.itemsize, _row_align(arr.dtype))


def _ag_start(arrays, taps, groups, *, name):
    na = len(arrays)
    ng = len(groups)
    nt = 0 if taps is None else 1
    n_sem = [3 * len(grp) + (3 if nt and g == 0 else 0) for g, grp in enumerate(groups)]

    def body(*refs):
        ins = refs[:na]
        taps_ref = refs[na] if nt else None
        sems = refs[na + nt:na + nt + 2 * ng]
        token = refs[-1]
        token[...] = jnp.zeros_like(token)
        x, y, c, chips = _place()
        myj = 2 * x + y
        for g, grp in enumerate(groups):
            ssem, rsem = sems[2 * g], sems[2 * g + 1]
            for idx, piece in enumerate(grp):
                ref = ins[piece[0]]
                rh, rc = _piece_chunk(arrays[piece[0]], piece)
                for k, (px, py) in enumerate(chips):
                    def send(r0, n, ref=ref, piece=piece, idx=idx, k=k, px=px, py=py, ssem=ssem, rsem=rsem):
                        part = _piece_rows(ref, piece, myj, c, r0, n)
                        return pltpu.make_async_remote_copy(
                            src_ref=part, dst_ref=part, send_sem=ssem.at[3 * idx + k], recv_sem=rsem.at[3 * idx + k],
                            device_id=(px, py, c), device_id_type=MESH)
                    _start_chunks(send, rh, rc)
            if nt and g == 0:
                for k, (px, py) in enumerate(chips):
                    pltpu.make_async_remote_copy(
                        src_ref=taps_ref.at[myj], dst_ref=taps_ref.at[myj],
                        send_sem=ssem.at[3 * len(grp) + k], recv_sem=rsem.at[3 * len(grp) + k],
                        device_id=(px, py, c), device_id_type=MESH).start()

    sem_shapes = []
    for n in n_sem:
        sem_shapes += [pltpu.SemaphoreType.DMA((n,)), pltpu.SemaphoreType.DMA((n,))]
    ops = list(arrays) + ([taps] if nt else [])
    bufs = [pltpu.HBM(a.shape, a.dtype) for a in ops]
    outs = pl.pallas_call(
        body, name=name,
        out_shape=(*sem_shapes, *bufs, jax.ShapeDtypeStruct((8, LANES), F32)),
        in_specs=[HBM] * (na + nt),
        out_specs=(*([SEM] * (2 * ng)), *([HBM] * (na + nt)), pl.BlockSpec(memory_space=pltpu.VMEM)),
        input_output_aliases={i: 2 * ng + i for i in range(na + nt)},
        compiler_params=pltpu.CompilerParams(has_side_effects=EFFECT),
    )(*[_in_hbm(a) for a in ops])
    sems = [(outs[2 * g], outs[2 * g + 1]) for g in range(ng)]
    return sems, list(outs[2 * ng:2 * ng + na]), (outs[2 * ng + na] if nt else None), outs[-1]


def _ag_wait(sems, vals, taps, group, after, *, name):
    nv = len(vals)
    extra = ([taps] if taps is not None else [])
    nb = nv + len(extra)

    def body(*refs):
        bufs = refs[:nb]
        ssem, rsem = refs[nb], refs[nb + 1]
        x, y, c, chips = _place()
        for idx, piece in enumerate(group):
            for k, (px, py) in enumerate(chips):
                got = _piece_rows(bufs[piece[0]], piece, 2 * px + py, c)
                cp = pltpu.make_async_remote_copy(
                    src_ref=got, dst_ref=got, send_sem=ssem.at[3 * idx + k], recv_sem=rsem.at[3 * idx + k],
                    device_id=(px, py, c), device_id_type=MESH)
                cp.wait_send()
                cp.wait_recv()
        if taps is not None:
            for k, (px, py) in enumerate(chips):
                got = bufs[nv].at[2 * px + py]
                cp = pltpu.make_async_remote_copy(
                    src_ref=got, dst_ref=got, send_sem=ssem.at[3 * len(group) + k],
                    recv_sem=rsem.at[3 * len(group) + k], device_id=(px, py, c), device_id_type=MESH)
                cp.wait_send()
                cp.wait_recv()

    ops = list(vals) + extra
    shapes = [pltpu.HBM(a.shape, a.dtype) for a in ops]
    outs = pl.pallas_call(
        body, name=name,
        out_shape=tuple(shapes),
        in_specs=[HBM] * nb + [SEM, SEM] + _dep_spec(after),
        out_specs=[HBM] * nb,
        input_output_aliases={i: i for i in range(nb)},
        compiler_params=pltpu.CompilerParams(has_side_effects=EFFECT),
    )(*ops, sems[0], sems[1], *_dep_arg(after))
    return list(outs[:nv]), (outs[nv] if taps is not None else None)


def _ag_forward(vals, group, *, name):
    nv = len(vals)
    npc = len(group)

    def body(*refs):
        bufs = refs[nv:2 * nv]
        fsem, gsem = refs[2 * nv:]
        x, y, c, chips = _place()
        sib = (x, y, 1 - c)
        sent = []
        for idx, piece in enumerate(group):
            rh, rc = _piece_chunk(vals[piece[0]], piece)
            for k, (px, py) in enumerate(chips):
                def fwd(r0, n, piece=piece, idx=idx, k=k, pj=2 * px + py):
                    part = _piece_rows(bufs[piece[0]], piece, pj, c, r0, n)
                    return pltpu.make_async_remote_copy(
                        src_ref=part, dst_ref=part, send_sem=fsem.at[3 * idx + k], recv_sem=gsem.at[3 * idx + k],
                        device_id=sib, device_id_type=MESH)
                _start_chunks(fwd, rh, rc)
                sent.append(fwd(0, rh))
        for idx, piece in enumerate(group):
            for k, (px, py) in enumerate(chips):
                theirs = _piece_rows(bufs[piece[0]], piece, 2 * px + py, 1 - c)
                pltpu.make_async_remote_copy(
                    src_ref=theirs, dst_ref=theirs, send_sem=fsem.at[3 * idx + k], recv_sem=gsem.at[3 * idx + k],
                    device_id=sib, device_id_type=MESH).wait_recv()
        for cp in sent:
            cp.wait_send()

    return pl.pallas_call(
        body, name=name,
        in_specs=[ANY] * nv, out_specs=[ANY] * nv,
        out_shape=[jax.ShapeDtypeStruct(v.shape, v.dtype) for v in vals],
        input_output_aliases={i: i for i in range(nv)},
        scratch_shapes=[pltpu.SemaphoreType.DMA((3 * npc,)), pltpu.SemaphoreType.DMA((3 * npc,))],
    )(*vals)


def _sibling_swap(dws, *, name):
    nm = len(dws)
    shapes = [jax.ShapeDtypeStruct((dw.shape[0], dw.shape[1] // 2, dw.shape[2]), dw.dtype) for dw in dws]

    def body(*refs):
        ins = refs[:nm]
        theirs = refs[nm:2 * nm]
        ssem, rsem = refs[2 * nm:]
        x, y, c, _ = _place()
        sib = (x, y, 1 - c)
        cps = []
        for m in range(nm):
            G, rh, cols = shapes[m].shape
            rc = _chunk_rows(rh, cols * shapes[m].dtype.itemsize, _row_align(shapes[m].dtype))
            for j in range(G):
                _start_chunks(lambda r0, n, m=m, j=j, rh=rh: pltpu.make_async_remote_copy(
                    src_ref=ins[m].at[j, pl.ds((1 - c) * rh + r0, n)],
                    dst_ref=theirs[m].at[j, pl.ds(r0, n)], send_sem=ssem.at[m], recv_sem=rsem.at[m],
                    device_id=sib, device_id_type=MESH), rh, rc)
            cps.append(pltpu.make_async_remote_copy(
                src_ref=ins[m].at[:, pl.ds((1 - c) * rh, rh), :], dst_ref=theirs[m],
                send_sem=ssem.at[m], recv_sem=rsem.at[m], device_id=sib, device_id_type=MESH))
        for cp in cps:
            cp.wait()

    return pl.pallas_call(
        body, name=name,
        in_specs=[ANY] * nm, out_specs=[ANY] * nm, out_shape=shapes,
        scratch_shapes=[pltpu.SemaphoreType.DMA((nm,)), pltpu.SemaphoreType.DMA((nm,))],
    )(*dws)


HBM = pl.BlockSpec(memory_space=pltpu.HBM)
SEM = pl.BlockSpec(memory_space=pltpu.SEMAPHORE)
EFFECT = pltpu.SideEffectType.DATAFLOW_SIDE_EFFECTING


def _in_hbm(a):
    return pltpu.with_memory_space_constraint(a, pltpu.HBM)


def _swap_start(dws, *, name):
    nm = len(dws)
    lands = [(dw.shape[0], dw.shape[1] // 2, dw.shape[2]) for dw in dws]

    def body(*refs):
        ins = refs[:nm]
        lnd = refs[nm:2 * nm]
        ssem, rsem = refs[2 * nm:2 * nm + 2]
        token = refs[-1]
        x, y, c, _ = _place()
        sib = (x, y, 1 - c)
        for m in range(nm):
            G, rh, cols = lands[m]
            rc = _chunk_rows(rh, cols * dws[m].dtype.itemsize, _row_align(dws[m].dtype))
            for j in range(G):
                _start_chunks(lambda r0, n, m=m, j=j, rh=rh: pltpu.make_async_remote_copy(
                    src_ref=ins[m].at[j, pl.ds((1 - c) * rh + r0, n)], dst_ref=lnd[m].at[j, pl.ds(r0, n)],
                    send_sem=ssem.at[m], recv_sem=rsem.at[m], device_id=sib, device_id_type=MESH), rh, rc)
        token[...] = jnp.zeros_like(token)

    src = [pltpu.HBM(dw.shape, dw.dtype) for dw in dws]
    dst = [pltpu.HBM(s, dw.dtype) for s, dw in zip(lands, dws)]
    outs = pl.pallas_call(
        body, name=name,
        out_shape=(pltpu.SemaphoreType.DMA((nm,)), pltpu.SemaphoreType.DMA((nm,)), *src, *dst,
                   jax.ShapeDtypeStruct((8, LANES), F32)),
        in_specs=[HBM] * (2 * nm),
        out_specs=(SEM, SEM, *([HBM] * (2 * nm)), pl.BlockSpec(memory_space=pltpu.VMEM)),
        input_output_aliases={i: 2 + i for i in range(2 * nm)},
        compiler_params=pltpu.CompilerParams(has_side_effects=EFFECT),
    )(*[_in_hbm(dw) for dw in dws], *[_in_hbm(lax.empty(s, dw.dtype)) for s, dw in zip(lands, dws)])
    return (outs[0], outs[1], list(outs[2:2 + nm]), list(outs[2 + nm:2 + 2 * nm])), outs[-1]


def _swap_wait(handle, after, *, name):
    ssem_in, rsem_in, dws, lands = handle
    nm = len(dws)

    def body(*refs):
        ins = refs[:nm]
        lnd = refs[nm:2 * nm]
        ssem, rsem = refs[2 * nm:2 * nm + 2]
        x, y, c, _ = _place()
        for m in range(nm):
            rh = lands[m].shape[1]
            cp = pltpu.make_async_remote_copy(
                src_ref=ins[m].at[:, pl.ds((1 - c) * rh, rh), :], dst_ref=lnd[m],
                send_sem=ssem.at[m], recv_sem=rsem.at[m], device_id=(x, y, 1 - c), device_id_type=MESH)
            cp.wait_send()
            cp.wait_recv()

    bufs = [pltpu.HBM(a.shape, a.dtype) for a in list(dws) + list(lands)]
    outs = pl.pallas_call(
        body, name=name,
        out_shape=tuple(bufs),
        in_specs=[HBM] * (2 * nm) + [SEM, SEM, ANY],
        out_specs=[HBM] * (2 * nm),
        input_output_aliases={i: i for i in range(2 * nm)},
        compiler_params=pltpu.CompilerParams(has_side_effects=EFFECT),
    )(*dws, *lands, ssem_in, rsem_in, after)
    return list(outs[:nm]), list(outs[nm:])


def _exchange_start(parts, *, name):
    nm = len(parts)

    def body(*refs):
        ins = refs[:nm]
        lands = refs[nm:2 * nm]
        ssem, rsem = refs[2 * nm:2 * nm + 2]
        token = refs[-1]
        x, y, c, chips = _place()
        myj = 2 * x + y
        for m in range(nm):
            _, rh, cols = parts[m].shape
            rc = _chunk_rows(rh, cols * parts[m].dtype.itemsize, _row_align(parts[m].dtype))
            for k, (px, py) in enumerate(chips):
                _start_chunks(lambda r0, n, m=m, k=k, px=px, py=py: pltpu.make_async_remote_copy(
                    src_ref=ins[m].at[2 * px + py, pl.ds(r0, n)], dst_ref=lands[m].at[myj, pl.ds(r0, n)],
                    send_sem=ssem.at[3 * m + k], recv_sem=rsem.at[3 * m + k],
                    device_id=(px, py, c), device_id_type=MESH), rh, rc)
        token[...] = jnp.zeros_like(token)

    bufs = [pltpu.HBM(p.shape, p.dtype) for p in parts]
    outs = pl.pallas_call(
        body, name=name,
        out_shape=(pltpu.SemaphoreType.DMA((3 * nm,)), pltpu.SemaphoreType.DMA((3 * nm,)), *bufs, *bufs,
                   jax.ShapeDtypeStruct((8, LANES), F32)),
        in_specs=[HBM] * (2 * nm),
        out_specs=(SEM, SEM, *([HBM] * (2 * nm)), pl.BlockSpec(memory_space=pltpu.VMEM)),
        input_output_aliases={i: 2 + i for i in range(2 * nm)},
        compiler_params=pltpu.CompilerParams(has_side_effects=EFFECT),
    )(*[_in_hbm(p) for p in parts], *[_in_hbm(lax.empty(p.shape, p.dtype)) for p in parts])
    return (outs[0], outs[1], list(outs[2:2 + nm]), list(outs[2 + nm:2 + 2 * nm])), outs[-1]


def _exchange_wait(handle, after, *, name):
    ssem_in, rsem_in, parts, lands = handle
    nm = len(parts)

    def body(*refs):
        ins = refs[:nm]
        lnd = refs[nm:2 * nm]
        ssem, rsem = refs[2 * nm:2 * nm + 2]
        x, y, c, chips = _place()
        for m in range(nm):
            for k, (px, py) in enumerate(chips):
                pj = 2 * px + py
                cp = pltpu.make_async_remote_copy(
                    src_ref=ins[m].at[pj], dst_ref=lnd[m].at[pj],
                    send_sem=ssem.at[3 * m + k], recv_sem=rsem.at[3 * m + k],
                    device_id=(px, py, c), device_id_type=MESH)
                cp.wait_send()
                cp.wait_recv()

    bufs = [pltpu.HBM(p.shape, p.dtype) for p in parts]
    outs = pl.pallas_call(
        body, name=name,
        out_shape=(*bufs, *bufs),
        in_specs=[HBM] * (2 * nm) + [SEM, SEM, ANY],
        out_specs=[HBM] * (2 * nm),
        input_output_aliases={i: i for i in range(2 * nm)},
        compiler_params=pltpu.CompilerParams(has_side_effects=EFFECT),
    )(*parts, *lands, ssem_in, rsem_in, after)
    return list(outs[nm:]), list(outs[:nm])


def _sibling_join(grads, regions, *, name):
    nm = len(grads)
    nr = len(regions)
    shapes = [jax.ShapeDtypeStruct(g.shape, g.dtype) for g in grads]

    def body(*refs):
        outs = refs[nm:2 * nm]
        ssem, rsem = refs[2 * nm:]
        x, y, c, _ = _place()
        sib = (x, y, 1 - c)
        cps = []
        for i, (m, off, rows) in enumerate(regions):
            rh, cols = rows // 2, grads[m].shape[1]
            rc = _chunk_rows(rh, cols * grads[m].dtype.itemsize, _row_align(grads[m].dtype))

            def send(r0, n, i=i, m=m, off=off, rh=rh):
                part = outs[m].at[pl.ds(off + c * rh + r0, n)]
                return pltpu.make_async_remote_copy(
                    src_ref=part, dst_ref=part, send_sem=ssem.at[i], recv_sem=rsem.at[i],
                    device_id=sib, device_id_type=MESH)
            _start_chunks(send, rh, rc)
            cps.append(send(0, rh))
        for i, (m, off, rows) in enumerate(regions):
            rh = rows // 2
            cps[i].wait_send()
            got = outs[m].at[pl.ds(off + (1 - c) * rh, rh)]
            pltpu.make_async_remote_copy(
                src_ref=got, dst_ref=got, send_sem=ssem.at[i], recv_sem=rsem.at[i],
                device_id=sib, device_id_type=MESH).wait_recv()

    return pl.pallas_call(
        body, name=name,
        in_specs=[ANY] * nm, out_specs=[ANY] * nm, out_shape=shapes,
        input_output_aliases={i: i for i in range(nm)},
        scratch_shapes=[pltpu.SemaphoreType.DMA((nr,)), pltpu.SemaphoreType.DMA((nr,))],
    )(*grads)


def _all_reduce_small(pack):
    R, C = pack.shape

    def body(in_ref, out_ref, slots, ssem, rsem):
        x, y, c, _ = _place()
        me = 4 * x + 2 * y + c
        slots[me] = in_ref[...]
        cps = []
        for k in range(1, N_DEV):
            dx, dy, dc = (k >> 2) & 1, (k >> 1) & 1, k & 1
            peer = (x ^ dx, y ^ dy, c ^ dc)
            cp = pltpu.make_async_remote_copy(
                src_ref=in_ref, dst_ref=slots.at[me], send_sem=ssem.at[k], recv_sem=rsem.at[k],
                device_id=peer, device_id_type=MESH)
            cp.start()
            cps.append(cp)
        for k in range(1, N_DEV):
            dx, dy, dc = (k >> 2) & 1, (k >> 1) & 1, k & 1
            got = slots.at[4 * (x ^ dx) + 2 * (y ^ dy) + (c ^ dc)]
            pltpu.make_async_remote_copy(
                src_ref=got, dst_ref=got, send_sem=ssem.at[k], recv_sem=rsem.at[k],
                device_id=(x ^ dx, y ^ dy, c ^ dc), device_id_type=MESH).wait_recv()
        for cp in cps:
            cp.wait_send()
        acc = slots[0]
        for s in range(1, N_DEV):
            acc = acc + slots[s]
        out_ref[...] = acc

    return pl.pallas_call(
        body, name="ar_small",
        in_specs=[pl.BlockSpec(memory_space=pltpu.VMEM)],
        out_specs=pl.BlockSpec(memory_space=pltpu.VMEM),
        out_shape=jax.ShapeDtypeStruct((R, C), F32),
        scratch_shapes=[pltpu.VMEM((N_DEV, R, C), F32),
                        pltpu.SemaphoreType.DMA((N_DEV,)), pltpu.SemaphoreType.DMA((N_DEV,))],
    )(pack)


def kernel(x, positions, mix_norm_pre, mix_norm_post, ffn_norm_pre, ffn_norm_post, ffn_w_gate_up, ffn_w_down, conv_w_in, conv_w, conv_w_out, kv_norm, w_kv, w_q, w_o, loss_target, m_mix_norm_pre, m_mix_norm_post, m_ffn_norm_pre, m_ffn_norm_post, m_ffn_w_gate_up, m_ffn_w_down, m_conv_w_in, m_conv_w, m_conv_w_out, m_kv_norm, m_w_kv, m_w_q, m_w_o, v_mix_norm_pre, v_mix_norm_post, v_ffn_norm_pre, v_ffn_norm_post, v_ffn_w_gate_up, v_ffn_w_down, v_conv_w_in, v_conv_w, v_conv_w_out, v_kv_norm, v_w_kv, v_w_q, v_w_o):
    T, D = x.shape[1], x.shape[2]
    L = ffn_w_gate_up.shape[0]
    n_gu = ffn_w_gate_up.shape[2]
    f_sh = ffn_w_down.shape[1]
    F = N_CHIPS * f_sh
    x0 = x[0]
    tgt = loss_target[0]

    half = HEAD_DIM // 2
    inv_freq = ROPE_THETA ** (-jnp.arange(half, dtype=F32) / half)
    ang = positions[0].astype(F32)[:, None] * inv_freq
    cosv, sinv = jnp.cos(ang), jnp.sin(ang)
    cos2 = jnp.tile(cosv, (1, LANES // half))
    ss2 = jnp.tile(jnp.concatenate([-sinv, sinv], axis=1), (1, LANES // HEAD_DIM))

    def as2d(a):
        return a.reshape(-1, a.shape[-1])

    big = [ffn_w_gate_up, ffn_w_down, conv_w_in, conv_w_out, w_kv, w_q, w_o]
    big_m = [m_ffn_w_gate_up, m_ffn_w_down, m_conv_w_in, m_conv_w_out, m_w_kv, m_w_q, m_w_o]
    big_v = [v_ffn_w_gate_up, v_ffn_w_down, v_conv_w_in, v_conv_w_out, v_w_kv, v_w_q, v_w_o]
    chip = 2 * lax.axis_index("x") + lax.axis_index("y")
    where = jnp.stack([chip, lax.axis_index("c")]).astype(jnp.int32)
    tc = conv_w.shape[2]
    cw_pad = jnp.concatenate([conv_w[0], jnp.zeros((8 - conv_w.shape[1], tc), F32)], axis=0)

    GU0, GU1, WD0, WD1, WCI, WCO, WKV, WQ, WO = range(9)
    shards = [(ffn_w_gate_up, 0), (ffn_w_gate_up, 1), (ffn_w_down, 0), (ffn_w_down, 1), (conv_w_in, 0),
              (conv_w_out, 0), (w_kv[None], 0), (w_q, 0), (w_o, 0)]
    ag_groups = [
        [(WCI, None, 0, D), (WCO, None, 0, D // N_CHIPS)],
        [(GU0, None, 0, D), (WD0, None, 0, f_sh)],
        [(WKV, None, 0, D), (WQ, None, 0, D)],
        [(WO, None, 0, D // N_CHIPS), (GU1, None, 0, D), (WD1, None, 0, f_sh)],
    ]

    def localised(group, idxs):
        return [(idxs.index(p[0]),) + p[1:] for p in group]

    cur = [None] * len(shards)
    first = [WCI, WCO]
    sems0, vals, taps, token = _ag_start(
        [_cast_place(*shards[i], where, BF, name=f"place{i}") for i in first],
        _cast_place(cw_pad[None], 0, where, F32, name="place_taps"), [localised(ag_groups[0], first)],
        name="ag_start0")
    for i, v in zip(first, vals):
        cur[i] = v
    rest = [i for i in range(len(shards)) if i not in first]
    sems1, vals, _, _ = _ag_start(
        [_cast_place(*shards[i], where, BF, name=f"place{i}", dep=token) for i in rest], None,
        [localised(g, rest) for g in ag_groups[1:]], name="ag_start1")
    for i, v in zip(rest, vals):
        cur[i] = v
    ag_sems = sems0 + sems1

    def gather_group(g, after):
        nonlocal taps
        idxs = sorted({p[0] for p in ag_groups[g]})
        local = localised(ag_groups[g], idxs)
        vals, landed_taps = _ag_wait(ag_sems[g], [cur[i] for i in idxs], taps if g == 0 else None, local, after,
                                     name=f"ag_wait{g}")
        if g == 0:
            taps = landed_taps
        vals = _ag_forward(vals, local, name=f"ag_forward{g}")
        for i, v in zip(idxs, vals):
            cur[i] = v

    def row(a, i):
        return a[i:i + 1]

    gather_group(0, None)
    wci, wco, cw = cur[WCI], cur[WCO].reshape(1, D, D), taps
    z, hn_m0 = _norm_matmul(x0, row(mix_norm_pre, 0), wci, cos2, ss2, name="f0_conv_in",
                            rope_shards=0, scale=1.0, out_dtype=BF)
    vmix = _conv_fwd(z, cw, name="f0_conv")
    y0, h1 = _matmul_postnorm(vmix, wco, 0, row(mix_norm_post, 0), x0, name="f0_conv_out")
    gather_group(1, h1)
    wgu0, wd0 = cur[GU0], cur[WD0].reshape(1, F, D)
    g0, u0, a0, hn_f0 = _norm_swiglu(h1, row(ffn_norm_pre, 0), wgu0, 0, name="f0_gate_up")
    f0, h2 = _matmul_postnorm(a0, wd0, 0, row(ffn_norm_post, 0), h1, name="f0_down")

    gather_group(2, h2)
    wkv, wq = cur[WKV], cur[WQ]
    kv_all, hn_kv = _norm_matmul(h2, kv_norm.reshape(1, D), wkv, cos2, ss2, name="f1_kv",
                                 rope_shards=N_CHIPS // 2, scale=1.0, out_dtype=BF)
    q_all, hn_m1 = _norm_matmul(h2, row(mix_norm_pre, 1), wq, cos2, ss2, name="f1_q",
                                rope_shards=N_CHIPS, scale=HEAD_DIM ** -0.5, out_dtype=BF)
    o_att, lse = _attn_fwd(q_all, kv_all, name="f1_attn")
    gather_group(3, o_att)
    wgu1, wd1, wo = cur[GU1], cur[WD1].reshape(1, F, D), cur[WO].reshape(1, D, D)
    y1, h3 = _matmul_postnorm(o_att, wo, 0, row(mix_norm_post, 1), h2, name="f1_attn_out")
    g1, u1, a1, hn_f1 = _norm_swiglu(h3, row(ffn_norm_pre, 1), wgu1, 0, name="f1_gate_up")
    f1, dh4, sq = _matmul_postnorm_loss(a1, wd1, 0, row(ffn_norm_post, 1), h3, tgt, name="f1_down_loss")
    loss_part = 0.5 * sq[0, 0] / D

    gu_shape = (N_CHIPS, D, n_gu)
    in_chips = lambda a: a.reshape(N_CHIPS, -1, a.shape[-1])

    def scatter_start(dws, tag):
        theirs = _sibling_swap(dws, name=f"rs_swap_{tag}")
        parts = [_pair_sum(dw, t, where, name=f"rs_pair_sum_{tag}{i}") for i, (dw, t) in enumerate(zip(dws, theirs))]
        return _exchange_start(parts, name=f"rs_exchange_start_{tag}")

    def scatter_go(swap, after, tag):
        dws, theirs = _swap_wait(swap, after, name=f"rs_swap_wait_{tag}")
        parts = [_pair_sum(dw, t, where, name=f"rs_pair_sum_{tag}{i}") for i, (dw, t) in enumerate(zip(dws, theirs))]
        return _exchange_start(parts, name=f"rs_exchange_start_{tag}")

    dyf1, dg1, du1, d_ffn_post1 = _postnorm_bwd_swiglu(dh4, f1, row(ffn_norm_post, 1), wd1, 0, g1, u1,
                                                       name="b1_down")
    dwd1 = _grad_matmul(a1, dyf1, (2, F // 2, D), F // 2, D, lambda i, j: (i, 0, 0), None, name="b1_dw_down")
    dwgu1 = _grad_matmul(hn_f1, dg1, gu_shape, D, n_gu, lambda i, j: (j, 0, 0), None, name="b1_dw_gate")
    dwgu1 = _grad_matmul(hn_f1, du1, gu_shape, D, n_gu, lambda i, j: (j + 2, 0, 0), dwgu1, name="b1_dw_up")
    dh3, d_ffn_pre1 = _matmul_prenorm_bwd((dg1, du1), wgu1, 0, h3, row(ffn_norm_pre, 1), dh4, name="b1_gate_up")

    dy1, do, d_mix_post1 = _postnorm_bwd_matmul(dh3, y1, row(mix_norm_post, 1), wo, 0, name="b1_attn_out",
                                                da_dtype=F32)
    dwo = _grad_matmul(o_att, dy1, (1, D, D), D, D, lambda i, j: (0, 0, 0), None, name="b1_dw_o")
    swap_a, token = _swap_start([dwgu1, in_chips(dwd1), in_chips(dwo)], name="rs_swap_start_a")
    prev = None
    for gi, (window, dil) in enumerate(BRANCHES):
        prev = _attn_bwd(q_all, kv_all, do, o_att, lse, cos2, ss2, gi, dil, prev, name=f"b1_attn{gi}", dep=token)
        token = None
        if gi == 0:
            rs_a, token = scatter_go(swap_a, prev[0], "a")
    dq_all, dk_all, dv_all = prev
    n_q = wq.shape[2]
    n_kv = wkv.shape[2]
    dwq = _grad_matmul(hn_m1, dq_all, (N_CHIPS, D, n_q), D, n_q, lambda i, j: (j, 0, 0), None, name="b1_dw_q")
    dwkv = _grad_matmul(hn_kv, dk_all, (N_CHIPS, D, n_kv), D, n_kv, lambda i, j: (j, 0, 0), None, name="b1_dw_k")
    dwkv = _grad_matmul(hn_kv, dv_all, (N_CHIPS, D, n_kv), D, n_kv, lambda i, j: (j + 2, 0, 0), dwkv, name="b1_dw_v")
    dh2, d_mix_pre1 = _matmul_prenorm_bwd((dq_all,), wq, 0, h2, row(mix_norm_pre, 1), dh3, name="b1_q")
    dh2, d_kv_norm = _matmul_prenorm_bwd((dk_all, dv_all), wkv, 0, h2, kv_norm.reshape(1, D), dh2, name="b1_kv")
    swap_b, token = _swap_start([dwkv, dwq], name="rs_swap_start_b")

    dyf0, dg0, du0, d_ffn_post0 = _postnorm_bwd_swiglu(dh2, f0, row(ffn_norm_post, 0), wd0, 0, g0, u0,
                                                       name="b0_down", dep=token)
    rs_b, token = scatter_go(swap_b, dyf0, "b")
    dwd0 = _grad_matmul(a0, dyf0, (2, F // 2, D), F // 2, D, lambda i, j: (i, 0, 0), None, name="b0_dw_down",
                        dep=token)
    dwgu0 = _grad_matmul(hn_f0, dg0, gu_shape, D, n_gu, lambda i, j: (j, 0, 0), None, name="b0_dw_gate")
    dwgu0 = _grad_matmul(hn_f0, du0, gu_shape, D, n_gu, lambda i, j: (j + 2, 0, 0), dwgu0, name="b0_dw_up")
    dh1, d_ffn_pre0 = _matmul_prenorm_bwd((dg0, du0), wgu0, 0, h1, row(ffn_norm_pre, 0), dh2, name="b0_gate_up")
    swap_c, token = _swap_start([dwgu0, in_chips(dwd0)], name="rs_swap_start_c")

    dy0, dvmix, d_mix_post0 = _postnorm_bwd_matmul(dh1, y0, row(mix_norm_post, 0), wco, 0, name="b0_conv_out",
                                                   da_dtype=BF, dep=token)
    rs_c, token = scatter_go(swap_c, dy0, "c")
    dwco = _grad_matmul(vmix, dy0, (1, D, D), D, D, lambda i, j: (0, 0, 0), None, name="b0_dw_conv_out",
                        dep=token)
    dz, dcw = _conv_bwd(z, cw, dvmix, name="b0_conv")
    n_ci = wci.shape[2]
    dwci = _grad_matmul(hn_m0, dz, (N_CHIPS, D, n_ci), D, n_ci, lambda i, j: (j, 0, 0), None, name="b0_dw_conv_in")
    dx, d_mix_pre0 = _matmul_prenorm_bwd((dz,), wci, 0, x0, row(mix_norm_pre, 0), dh1, name="b0_conv_in")

    pack = jnp.concatenate([
        d_mix_pre0, d_mix_pre1, d_mix_post0, d_mix_post1, d_ffn_pre0, d_ffn_pre1, d_ffn_post0, d_ffn_post1,
        d_kv_norm, dcw[0:3], jnp.full((1, D), loss_part, F32),
        jnp.zeros((SMALL_ROWS - 13, D), F32)], axis=0)
    red = _all_reduce_small(pack)
    loss = red[12, 0]
    myj = 2 * lax.axis_index("x") + lax.axis_index("y")
    g_conv_w = lax.dynamic_slice(red, (9, myj * tc), (3, tc))

    zeros7 = jnp.zeros((SMALL_ROWS - 9, D), F32)
    w_small = jnp.concatenate([mix_norm_pre, mix_norm_post, ffn_norm_pre, ffn_norm_post, kv_norm.reshape(1, D), zeros7], axis=0)
    m_small = jnp.concatenate([m_mix_norm_pre, m_mix_norm_post, m_ffn_norm_pre, m_ffn_norm_post, m_kv_norm.reshape(1, D), zeros7], axis=0)
    v_small = jnp.concatenate([v_mix_norm_pre, v_mix_norm_post, v_ffn_norm_pre, v_ffn_norm_post, v_kv_norm.reshape(1, D), zeros7], axis=0)
    d_small, nm_small, nv_small = _adamw(w_small, red, m_small, v_small, name="adamw_small")

    pad5 = jnp.zeros((5, tc), F32)
    d_cw, nm_cw, nv_cw = _adamw(cw_pad, jnp.concatenate([g_conv_w, pad5], axis=0),
                                jnp.concatenate([m_conv_w[0], pad5], axis=0),
                                jnp.concatenate([v_conv_w[0], pad5], axis=0), name="adamw_conv_w")

    rs_d, _ = scatter_start([dwci, in_chips(dwco)], "d")

    pieces = {"a": [(0, D), (1, f_sh), (6, 0)], "b": [(4, 0), (5, 0)], "c": [(0, 0), (1, 0)], "d": [(2, 0), (3, 0)]}
    grads2d = [None] * len(big)
    big_out = [None] * len(big)

    def finish(groups, after, tag):
        regions, idxs = [], []
        for gtag, handle in groups:
            landed, parts = _exchange_wait(handle, after, name=f"rs_exchange_wait_{gtag}")
            for i, (l, p, (wi, off)) in enumerate(zip(landed, parts, pieces[gtag])):
                total = as2d(big[wi]).shape[0]
                grads2d[wi] = _chip_sum(l, p, where, total, off, grads2d[wi], name=f"rs_chip_sum_{gtag}{i}")
                if wi not in idxs:
                    idxs.append(wi)
                regions.append((idxs.index(wi), off, 2 * l.shape[1]))
        joined = _sibling_join([grads2d[wi] for wi in idxs], regions, name=f"rs_sibling_join_{tag}")
        for wi, gr in zip(idxs, joined):
            w = big[wi]
            d_, m_, v_, g_ = _adamw(as2d(w), gr, as2d(big_m[wi]), as2d(big_v[wi]), name=f"adamw{wi}",
                                    emit_grad=True)
            big_out[wi] = (g_.reshape(w.shape), d_.reshape(w.shape), m_.reshape(w.shape), v_.reshape(w.shape))

    finish([("a", rs_a), ("b", rs_b), ("c", rs_c)], dx, "abc")
    finish([("d", rs_d)], big_out[0][1], "d")

    def small(a):
        return (a[0:2], a[2:4], a[4:6], a[6:8])

    def assemble(sm, cwv, kind):
        pre, post, fpre, fpost = small(sm)
        b = [t[kind] for t in big_out]
        return [pre, post, fpre, fpost, b[0], b[1], b[2], cwv[0:3].reshape(conv_w.shape), b[3],
                sm[8], b[4], b[5].reshape(w_q.shape), b[6].reshape(w_o.shape)]

    grads = assemble(red, jnp.concatenate([g_conv_w, pad5], axis=0), 0)
    deltas = assemble(d_small, d_cw, 1)
    new_m = assemble(nm_small, nm_cw, 2)
    new_v = assemble(nv_small, nv_cw, 3)
    return (loss, dx.reshape(x.shape), *grads, *deltas, *new_m, *new_v)
```

```python
import functools

import jax
import jax.numpy as jnp
from jax import lax
from jax.experimental import pallas as pl
from jax.experimental.pallas import tpu as pltpu

HEAD_DIM = 64
BAND = 128
BRANCHES = ((128, 1), (512, 4), (2048, 16))
ROPE_THETA = 10000.0
RMS_EPS = 1e-6
NEG_INF = -1e30
ADAM_LR = 0.001
ADAM_B1 = 0.9
ADAM_B2 = 0.999
ADAM_EPS = 1e-08
ADAM_WD = 0.01
ADAM_STEP = 10

N_CHIPS = 4
N_DEV = 8
LANES = 128
ROW_BLOCK = 512
ROW_BLOCK_WIDE = 1024
GRAD_CHUNK = 2048
ATTN_FWD_UNROLL = 16
ATTN_BWD_UNROLL = 8
ATTN_BLOCK_ROWS = 2048
VMEM_LIMIT = 56 * 1024 * 1024
SMALL_ROWS = 16
ADAMW_BLOCK_BYTES = 1024 * 1024
DMA_CHUNK_BYTES = 512 * 1024

BF = jnp.bfloat16
F32 = jnp.float32
MESH = pl.DeviceIdType.MESH
ANY = pl.BlockSpec(memory_space=pl.ANY)


def _cp(*sem):
    return pltpu.CompilerParams(dimension_semantics=sem, vmem_limit_bytes=VMEM_LIMIT)


def _rot_half(t, first):
    return jnp.where(first, pltpu.roll(t, 96, 1), pltpu.roll(t, 32, 1))


def _sigmoid(x):
    return pl.reciprocal(1.0 + jnp.exp(-x), approx=True)


def _first_half_mask(rows):
    lane = lax.broadcasted_iota(jnp.int32, (rows, LANES), 1)
    return (lane % HEAD_DIM) < (HEAD_DIM // 2)


def _normed_rows(j, rows, x_ref, g_ref, xn_ref, xs, last_start, tm):
    @pl.when(j == 0)
    def _():
        xv = x_ref[...]
        r = lax.rsqrt(jnp.mean(xv * xv, axis=-1, keepdims=True) + RMS_EPS)
        xn = (xv * r * g_ref[...]).astype(BF)
        xs[rows, :] = xn
        xn_ref[...] = xn

    @pl.when(j > 0)
    def _():
        xn_ref[...] = xs[pl.ds(last_start, tm), :]


def _norm_matmul(x, gain, wg, cos2, ss2, *, name, rope_shards, scale, out_dtype):
    T, D = x.shape
    n = wg.shape[2]
    tm = min(ROW_BLOCK_WIDE, T)
    ni = T // tm

    def body(x_ref, g_ref, w_ref, cos_ref, ss_ref, y_ref, xn_ref, xs):
        j = pl.program_id(0)
        rows = pl.ds(pl.multiple_of(pl.program_id(1) * tm, tm), tm)
        _normed_rows(j, rows, x_ref, g_ref, xn_ref, xs, (ni - 1) * tm, tm)
        acc = jnp.dot(xs[rows, :], w_ref[...], preferred_element_type=F32)

        def plain():
            y_ref[...] = acc.astype(out_dtype)

        def rope():
            cosv = cos_ref[...]
            ssv = ss_ref[...]
            first = _first_half_mask(tm)
            for ci in range(n // LANES):
                t = acc[:, ci * LANES:(ci + 1) * LANES]
                y = (t * cosv + _rot_half(t, first) * ssv) * scale
                y_ref[:, ci * LANES:(ci + 1) * LANES] = y.astype(out_dtype)

        if rope_shards == 0:
            plain()
        elif rope_shards == N_CHIPS:
            rope()
        else:
            pl.when(j < rope_shards)(rope)
            pl.when(j >= rope_shards)(plain)

    first_pass = lambda j, i: (jnp.where(j == 0, i, ni - 1), 0)
    return pl.pallas_call(
        body, name=name,
        grid=(N_CHIPS, ni),
        in_specs=[
            pl.BlockSpec((tm, D), first_pass),
            pl.BlockSpec((1, D), lambda j, i: (0, 0)),
            pl.BlockSpec((None, D, n), lambda j, i: (j, 0, 0)),
            pl.BlockSpec((tm, LANES), lambda j, i: (i, 0)),
            pl.BlockSpec((tm, LANES), lambda j, i: (i, 0)),
        ],
        out_specs=[
            pl.BlockSpec((tm, n), lambda j, i: (i, j)),
            pl.BlockSpec((tm, D), first_pass),
        ],
        out_shape=[jax.ShapeDtypeStruct((T, N_CHIPS * n), out_dtype),
                   jax.ShapeDtypeStruct((T, D), BF)],
        scratch_shapes=[pltpu.VMEM((T, D), BF)],
        compiler_params=_cp("arbitrary", "arbitrary"),
    )(x, gain, wg, cos2, ss2)


def _norm_swiglu(x, gain, wg, layer, *, name):
    T, D = x.shape
    n = wg.shape[2]
    tm = min(ROW_BLOCK, T)
    ni = T // tm

    def body(x_ref, g_ref, wg_ref, wu_ref, so_ref, uto_ref, ao_ref, xn_ref, xs):
        j = pl.program_id(0)
        rows = pl.ds(pl.multiple_of(pl.program_id(1) * tm, tm), tm)
        _normed_rows(j, rows, x_ref, g_ref, xn_ref, xs, (ni - 1) * tm, tm)
        g = jnp.dot(xs[rows, :], wg_ref[...], preferred_element_type=F32)
        u = jnp.dot(xs[rows, :], wu_ref[...], preferred_element_type=F32)
        sg = _sigmoid(g)
        s = g * sg
        so_ref[...] = s.astype(BF)
        uto_ref[...] = (u * (sg + s * (1.0 - sg))).astype(BF)
        ao_ref[...] = (s * u).astype(BF)

    half = N_CHIPS // 2
    first_pass = lambda j, i: (jnp.where(j == 0, i, ni - 1), 0)
    act = jax.ShapeDtypeStruct((T, half * n), BF)
    return pl.pallas_call(
        body, name=name,
        grid=(half, ni),
        in_specs=[
            pl.BlockSpec((tm, D), first_pass),
            pl.BlockSpec((1, D), lambda j, i: (0, 0)),
            pl.BlockSpec((None, D, n), lambda j, i: (j, layer, 0)),
            pl.BlockSpec((None, D, n), lambda j, i: (j + half, layer, 0)),
        ],
        out_specs=[
            pl.BlockSpec((tm, n), lambda j, i: (i, j)),
            pl.BlockSpec((tm, n), lambda j, i: (i, j)),
            pl.BlockSpec((tm, n), lambda j, i: (i, j)),
            pl.BlockSpec((tm, D), first_pass),
        ],
        out_shape=[act, act, act, jax.ShapeDtypeStruct((T, D), BF)],
        scratch_shapes=[pltpu.VMEM((T, D), BF)],
        compiler_params=_cp("arbitrary", "arbitrary"),
    )(x, gain, wg, wg)


def _matmul_postnorm(a, w3, widx, gain, h_old, *, name):
    T, K = a.shape
    D = w3.shape[2]
    tm = min(ROW_BLOCK_WIDE, T)

    def body(a_ref, w_ref, g_ref, h_ref, y_ref, hn_ref):
        y = jnp.dot(a_ref[...].astype(BF), w_ref[...], preferred_element_type=F32)
        y_ref[...] = y.astype(BF)
        r = lax.rsqrt(jnp.mean(y * y, axis=-1, keepdims=True) + RMS_EPS)
        hn_ref[...] = h_ref[...] + y * r * g_ref[...]

    return pl.pallas_call(
        body, name=name,
        grid=(T // tm,),
        in_specs=[
            pl.BlockSpec((tm, K), lambda i: (i, 0)),
            pl.BlockSpec((None, K, D), lambda i: (widx, 0, 0)),
            pl.BlockSpec((1, D), lambda i: (0, 0)),
            pl.BlockSpec((tm, D), lambda i: (i, 0)),
        ],
        out_specs=[pl.BlockSpec((tm, D), lambda i: (i, 0)),
                   pl.BlockSpec((tm, D), lambda i: (i, 0))],
        out_shape=[jax.ShapeDtypeStruct((T, D), BF), jax.ShapeDtypeStruct((T, D), F32)],
        compiler_params=_cp("parallel"),
    )(a, w3, gain, h_old)


def _matmul_postnorm_loss(a, w3, widx, gain, h_old, target, *, name):
    T, K = a.shape
    D = w3.shape[2]
    tm = min(ROW_BLOCK, T)

    def body(a_ref, w_ref, g_ref, h_ref, t_ref, y_ref, dh_ref, s_ref):
        @pl.when(pl.program_id(0) == 0)
        def _():
            s_ref[...] = jnp.zeros_like(s_ref)

        y = jnp.dot(a_ref[...].astype(BF), w_ref[...], preferred_element_type=F32)
        y_ref[...] = y.astype(BF)
        r = lax.rsqrt(jnp.mean(y * y, axis=-1, keepdims=True) + RMS_EPS)
        e = (h_ref[...] + y * r * g_ref[...]) - t_ref[...]
        dh_ref[...] = e * (1.0 / D)
        s_ref[...] += jnp.sum(e * e)

    rows = pl.BlockSpec((tm, D), lambda i: (i, 0))
    return pl.pallas_call(
        body, name=name,
        grid=(T // tm,),
        in_specs=[
            pl.BlockSpec((tm, K), lambda i: (i, 0)),
            pl.BlockSpec((None, K, D), lambda i: (widx, 0, 0)),
            pl.BlockSpec((1, D), lambda i: (0, 0)),
            rows, rows,
        ],
        out_specs=[rows, rows, pl.BlockSpec((8, LANES), lambda i: (0, 0))],
        out_shape=[jax.ShapeDtypeStruct((T, D), BF), jax.ShapeDtypeStruct((T, D), F32),
                   jax.ShapeDtypeStruct((8, LANES), F32)],
        compiler_params=_cp("arbitrary"),
    )(a, w3, gain, h_old, target)


def _shift_down(u, k):
    row = lax.broadcasted_iota(jnp.int32, u.shape, 0)
    return jnp.where(row >= k, pltpu.roll(u, k, 0), 0.0)


def _shift_up(u, k):
    T = u.shape[0]
    row = lax.broadcasted_iota(jnp.int32, u.shape, 0)
    return jnp.where(row < T - k, pltpu.roll(u, T - k, 0), 0.0)


def _conv_fwd(z, cw, *, name):
    T = z.shape[0]
    D = z.shape[1] // 3
    tc = cw.shape[2]
    nb = D // tc

    def body(b_ref, c_ref, h_ref, w_ref, o_ref):
        u = c_ref[...].astype(F32) * h_ref[...].astype(F32)
        w = w_ref[...]
        conv = w[2:3] * u + w[1:2] * _shift_down(u, 1) + w[0:1] * _shift_down(u, 2)
        o_ref[...] = (b_ref[...].astype(F32) * conv).astype(BF)

    return pl.pallas_call(
        body, name=name,
        grid=(nb,),
        in_specs=[
            pl.BlockSpec((T, tc), lambda j: (0, j)),
            pl.BlockSpec((T, tc), lambda j: (0, nb + j)),
            pl.BlockSpec((T, tc), lambda j: (0, 2 * nb + j)),
            pl.BlockSpec((None, 8, tc), lambda j: (j, 0, 0)),
        ],
        out_specs=pl.BlockSpec((T, tc), lambda j: (0, j)),
        out_shape=jax.ShapeDtypeStruct((T, D), BF),
        compiler_params=_cp("parallel"),
    )(z, z, z, cw)


def _conv_bwd(z, cw, dv, *, name):
    T = z.shape[0]
    D = z.shape[1] // 3
    tc = LANES
    nb = D // tc
    per = cw.shape[2] // tc

    def body(b_ref, c_ref, h_ref, w_ref, dv_ref, dz_ref, dw_ref, stage, sems):
        j = pl.program_id(0)
        slot = j % 2

        def slab(p, jj, s):
            col = pl.multiple_of((p * nb + jj) * tc, tc)
            return pltpu.make_async_copy(stage.at[s, p], dz_ref.at[:, pl.ds(col, tc)], sems.at[s, p])

        @pl.when(j >= 2)
        def _():
            for p in range(3):
                slab(p, j - 2, slot).wait()

        c = c_ref[...].astype(F32)
        h = h_ref[...].astype(F32)
        u = c * h
        u1 = _shift_down(u, 1)
        u2 = _shift_down(u, 2)
        w = w_ref[...]
        dvv = dv_ref[...].astype(F32)
        dconv = dvv * b_ref[...].astype(F32)
        du = w[2:3] * dconv + w[1:2] * _shift_up(dconv, 1) + w[0:1] * _shift_up(dconv, 2)
        rows = lax.broadcasted_iota(jnp.int32, (8, tc), 0)
        dw_ref[...] = jnp.where(rows == 0, jnp.sum(dconv * u2, axis=0, keepdims=True),
                                jnp.where(rows == 1, jnp.sum(dconv * u1, axis=0, keepdims=True),
                                          jnp.where(rows == 2, jnp.sum(dconv * u, axis=0, keepdims=True), 0.0)))
        stage[slot, 0] = (dvv * (w[2:3] * u + w[1:2] * u1 + w[0:1] * u2)).astype(BF)
        stage[slot, 1] = (du * h).astype(BF)
        stage[slot, 2] = (du * c).astype(BF)
        for p in range(3):
            slab(p, j, slot).start()

        @pl.when(j == nb - 1)
        def _():
            for p in range(3):
                slab(p, j, slot).wait()
            if nb > 1:
                for p in range(3):
                    slab(p, j - 1, 1 - slot).wait()

    return pl.pallas_call(
        body, name=name,
        grid=(nb,),
        in_specs=[
            pl.BlockSpec((T, tc), lambda j: (0, j)),
            pl.BlockSpec((T, tc), lambda j: (0, nb + j)),
            pl.BlockSpec((T, tc), lambda j: (0, 2 * nb + j)),
            pl.BlockSpec((None, 8, tc), lambda j: (j // per, 0, j % per)),
            pl.BlockSpec((T, tc), lambda j: (0, j)),
        ],
        out_specs=[ANY, pl.BlockSpec((8, tc), lambda j: (0, j))],
        out_shape=[jax.ShapeDtypeStruct((T, 3 * D), BF), jax.ShapeDtypeStruct((8, D), F32)],
        scratch_shapes=[pltpu.VMEM((2, 3, T, tc), BF), pltpu.SemaphoreType.DMA((2, 3))],
        compiler_params=_cp("arbitrary"),
    )(z, z, z, cw, dv)


def _strided(base, count, d):
    return pl.ds(base, count, stride=d) if d > 1 else pl.ds(pl.multiple_of(base, BAND), count)


def _fill_band_bias(bias):
    qi = lax.broadcasted_iota(jnp.int32, (2 * BAND, 2 * BAND), 0) % BAND
    kj = lax.broadcasted_iota(jnp.int32, (2 * BAND, 2 * BAND), 1)
    dist = qi + BAND - kj
    band = (dist >= 0) & (dist <= BAND)
    bias[0] = jnp.where(band & (kj >= BAND), 0.0, NEG_INF)
    bias[1] = jnp.where(band, 0.0, NEG_INF)


def _attn_block_rows(T):
    return min(ATTN_BLOCK_ROWS, T)


def _head_mask():
    lane = lax.broadcasted_iota(jnp.int32, (2 * BAND, LANES), 1)
    row = lax.broadcasted_iota(jnp.int32, (2 * BAND, LANES), 0)
    return (lane < HEAD_DIM) == (row < BAND)


def _attn_fwd(q_all, kv_all, *, name):
    T = q_all.shape[0]
    NB = len(BRANCHES)
    Dm = q_all.shape[1] // NB
    HP = Dm // LANES
    R = _attn_block_rows(T)
    units = R // BAND
    dmax = max(d for _, d in BRANCHES)

    def body(*refs):
        ins = refs[:5 * NB]
        o_ref, l_ref, qbuf, kbuf, vbuf, o_s, l_s, bias = refs[5 * NB:]
        n = pl.program_id(0)
        pl.when((n == 0) & (pl.program_id(1) == 0))(lambda: _fill_band_bias(bias))
        hm = _head_mask()
        low = lax.broadcasted_iota(jnp.int32, (BAND, LANES), 1) < HEAD_DIM

        for g, (_, d) in enumerate(BRANCHES):
            q_ref, kp_ref, kc_ref, vp_ref, vc_ref = ins[5 * g:5 * g + 5]
            pr = BAND * d
            qbuf[...] = q_ref[...].astype(F32)
            kbuf[0:pr, :] = kp_ref[...].astype(F32)
            kbuf[pr:pr + R, :] = kc_ref[...].astype(F32)
            vbuf[0:pr, :] = vp_ref[...].astype(F32)
            vbuf[pr:pr + R, :] = vc_ref[...].astype(F32)

            def unit(u, carry, g=g, d=d, pr=pr):
                sub = u // d
                base = sub * pr + (u - sub * d)
                q = qbuf[_strided(base, BAND, d), :]
                q2 = jnp.where(hm, jnp.concatenate([q, q], axis=0), 0.0).astype(BF)
                k2 = kbuf[_strided(base, 2 * BAND, d), :].astype(BF)
                v2 = vbuf[_strided(base, 2 * BAND, d), :].astype(BF)
                s = lax.dot_general(q2, k2, (((1,), (1,)), ((), ())), preferred_element_type=F32)
                s = s + bias[((n > 0) | (sub > 0)).astype(jnp.int32)]
                m = jnp.max(s, axis=-1, keepdims=True)
                p = jnp.exp(s - m)
                l = jnp.sum(p, axis=-1, keepdims=True)
                pv = jnp.dot(p.astype(BF), v2, preferred_element_type=F32) * (1.0 / l)
                lse = m + jnp.log(l)
                o_s[g, _strided(base, BAND, d), :] = jnp.where(low, pv[:BAND], pv[BAND:])
                l_s[g, _strided(base, BAND, d), :] = jnp.where(low, lse[:BAND], lse[BAND:])
                return carry

            lax.fori_loop(0, units, unit, 0, unroll=min(ATTN_FWD_UNROLL, units))

        def merge(i, carry):
            sl = pl.ds(pl.multiple_of(i * BAND, BAND), BAND)
            lv = [l_s[g, sl, :] for g in range(NB)]
            m = functools.reduce(jnp.maximum, lv)
            e = [jnp.exp(v - m) for v in lv]
            tot = functools.reduce(jnp.add, e)
            inv = 1.0 / tot
            o_ref[sl, :] = functools.reduce(jnp.add, [(e[g] * inv) * o_s[g, sl, :] for g in range(NB)])
            l_ref[sl, :] = m + jnp.log(tot)
            return carry

        lax.fori_loop(0, units, merge, 0)

    in_specs, args = [], []
    for g, (_, d) in enumerate(BRANCHES):
        per = R // (BAND * d)
        for col, rows, idx in (
                (g * HP, R, lambda n, hp: n),
                (g * HP, BAND * d, lambda n, hp, per=per: jnp.maximum(n * per - 1, 0)),
                (g * HP, R, lambda n, hp: n),
                ((NB + g) * HP, BAND * d, lambda n, hp, per=per: jnp.maximum(n * per - 1, 0)),
                ((NB + g) * HP, R, lambda n, hp: n)):
            in_specs.append(pl.BlockSpec((rows, LANES), lambda n, hp, col=col, idx=idx: (idx(n, hp), col + hp)))
        args += [q_all, kv_all, kv_all, kv_all, kv_all]
    out = pl.BlockSpec((R, LANES), lambda n, hp: (n, hp))
    return pl.pallas_call(
        body, name=name,
        grid=(T // R, HP),
        in_specs=in_specs,
        out_specs=[out, out],
        out_shape=[jax.ShapeDtypeStruct((T, Dm), F32), jax.ShapeDtypeStruct((T, Dm), F32)],
        scratch_shapes=[pltpu.VMEM((R, LANES), F32),
                        pltpu.VMEM((BAND * dmax + R, LANES), F32), pltpu.VMEM((BAND * dmax + R, LANES), F32),
                        pltpu.VMEM((NB, R, LANES), F32), pltpu.VMEM((NB, R, LANES), F32),
                        pltpu.VMEM((2, 2 * BAND, 2 * BAND), F32)],
        compiler_params=_cp("arbitrary", "arbitrary"),
    )(*args)


def _attn_bwd(q_all, kv_all, do, o, lse, cos2, ss2, g, d, prev, *, name, dep=None):
    T = q_all.shape[0]
    NB = len(BRANCHES)
    Dm = q_all.shape[1] // NB
    HP = Dm // LANES
    R = _attn_block_rows(T)
    nblk = T // R
    units = R // BAND
    pr = BAND * d
    per = R // pr
    scale = HEAD_DIM ** -0.5

    def rope_bwd(t, cosv, ssv, first):
        return t * cosv - _rot_half(t, first) * ssv

    def body(q_ref, kp_ref, kc_ref, vp_ref, vc_ref, do_ref, o_ref, l_ref, cos_ref, ss_ref, *rest):
        dq_ref, dk_ref, dv_ref, qbuf, kbuf, vbuf, dq_s, dk_s, dv_s, pend_k, pend_v, bias = rest[-12:]
        i = pl.program_id(1)
        n = nblk - 1 - i
        pl.when((i == 0) & (pl.program_id(0) == 0))(lambda: _fill_band_bias(bias))
        hm = _head_mask()
        low = lax.broadcasted_iota(jnp.int32, (BAND, LANES), 1) < HEAD_DIM

        qbuf[...] = q_ref[...].astype(F32)
        kbuf[0:pr, :] = kp_ref[...].astype(F32)
        kbuf[pr:pr + R, :] = kc_ref[...].astype(F32)
        vbuf[0:pr, :] = vp_ref[...].astype(F32)
        vbuf[pr:pr + R, :] = vc_ref[...].astype(F32)

        @pl.when(i == 0)
        def _():
            pend_k[...] = jnp.zeros_like(pend_k)
            pend_v[...] = jnp.zeros_like(pend_v)

        def unit(u, carry):
            sub = per - 1 - u // d
            cls = u % d
            base = sub * pr + cls
            sl = _strided(base, BAND, d)
            sl2 = _strided(base, 2 * BAND, d)
            q = qbuf[sl, :]
            dov = do_ref[sl, :]
            ov = o_ref[sl, :]
            lv = l_ref[sl, :]
            q2 = jnp.where(hm, jnp.concatenate([q, q], axis=0), 0.0).astype(BF)
            do2 = jnp.where(hm, jnp.concatenate([dov, dov], axis=0), 0.0)
            oo = dov * ov
            delta = jnp.sum(jnp.where(hm, jnp.concatenate([oo, oo], axis=0), 0.0), axis=-1, keepdims=True)
            lse2 = jnp.concatenate([lv[:, 0:1], lv[:, HEAD_DIM:HEAD_DIM + 1]], axis=0)
            do2 = do2.astype(BF)
            k2 = kbuf[sl2, :].astype(BF)
            v2 = vbuf[sl2, :].astype(BF)
            s = lax.dot_general(q2, k2, (((1,), (1,)), ((), ())), preferred_element_type=F32)
            p = jnp.exp(s + bias[((n > 0) | (sub > 0)).astype(jnp.int32)] - lse2)
            dp = lax.dot_general(do2, v2, (((1,), (1,)), ((), ())), preferred_element_type=F32)
            ds = (p * (dp - delta)).astype(BF)
            dq2 = jnp.dot(ds, k2, preferred_element_type=F32)
            dq = jnp.where(low, dq2[:BAND], dq2[BAND:])
            dq_s[sl, :] = dq
            tn = (((0,), (0,)), ((), ()))
            dk2 = lax.dot_general(ds, q2, tn, preferred_element_type=F32)
            dv2 = lax.dot_general(p.astype(BF), do2, tn, preferred_element_type=F32)
            dk_s[sl, :] = dk2[BAND:] + pend_k[cls]
            dv_s[sl, :] = dv2[BAND:] + pend_v[cls]
            pend_k[cls] = dk2[:BAND]
            pend_v[cls] = dv2[:BAND]
            return carry

        lax.fori_loop(0, units, unit, 0, unroll=min(ATTN_BWD_UNROLL, units))

        whole = _first_half_mask(R)
        dq_ref[...] = (rope_bwd(dq_s[...], cos_ref[...], ss_ref[...], whole) * scale).astype(BF)
        dk_ref[...] = rope_bwd(dk_s[...], cos_ref[...], ss_ref[...], whole).astype(BF)
        dv_ref[...] = dv_s[...].astype(BF)

    blk = (R, LANES)
    pblk = (pr, LANES)
    cur = lambda hp, i: nblk - 1 - i
    prv = lambda hp, i: jnp.maximum((nblk - 1 - i) * per - 1, 0)
    in_specs = [
        pl.BlockSpec(blk, lambda hp, i: (cur(hp, i), g * HP + hp)),
        pl.BlockSpec(pblk, lambda hp, i: (prv(hp, i), g * HP + hp)),
        pl.BlockSpec(blk, lambda hp, i: (cur(hp, i), g * HP + hp)),
        pl.BlockSpec(pblk, lambda hp, i: (prv(hp, i), (NB + g) * HP + hp)),
        pl.BlockSpec(blk, lambda hp, i: (cur(hp, i), (NB + g) * HP + hp)),
        pl.BlockSpec(blk, lambda hp, i: (cur(hp, i), hp)),
        pl.BlockSpec(blk, lambda hp, i: (cur(hp, i), hp)),
        pl.BlockSpec(blk, lambda hp, i: (cur(hp, i), hp)),
        pl.BlockSpec(blk, lambda hp, i: (cur(hp, i), 0)),
        pl.BlockSpec(blk, lambda hp, i: (cur(hp, i), 0)),
    ]
    args = [q_all, kv_all, kv_all, kv_all, kv_all, do, o, lse, cos2, ss2]
    if dep is not None:
        in_specs.append(ANY)
        args.append(dep)
    aliases = {}
    if prev is not None:
        in_specs += [ANY, ANY, ANY]
        aliases = {len(args): 0, len(args) + 1: 1, len(args) + 2: 2}
        args += list(prev)
    wide = jax.ShapeDtypeStruct((T, NB * Dm), BF)
    out = pl.BlockSpec(blk, lambda hp, i: (cur(hp, i), g * HP + hp))
    return pl.pallas_call(
        body, name=name,
        grid=(HP, nblk),
        in_specs=in_specs,
        out_specs=[out, out, out],
        out_shape=[wide, wide, wide],
        scratch_shapes=[pltpu.VMEM(blk, F32), pltpu.VMEM((pr + R, LANES), F32), pltpu.VMEM((pr + R, LANES), F32),
                        pltpu.VMEM(blk, F32), pltpu.VMEM(blk, F32), pltpu.VMEM(blk, F32),
                        pltpu.VMEM((d, BAND, LANES), F32), pltpu.VMEM((d, BAND, LANES), F32),
                        pltpu.VMEM((2, 2 * BAND, 2 * BAND), F32)],
        input_output_aliases=aliases,
        compiler_params=_cp("arbitrary", "arbitrary"),
    )(*args)


def _postnorm_bwd(dh, y, g_ref_val):
    r = lax.rsqrt(jnp.mean(y * y, axis=-1, keepdims=True) + RMS_EPS)
    yn = y * r
    dyn = dh * g_ref_val
    dy = r * (dyn - yn * jnp.mean(dyn * yn, axis=-1, keepdims=True))
    return dy, yn


def _after(body, n_in, dep):
    if dep is None:
        return body
    return lambda *refs: body(*refs[:n_in], *refs[n_in + 1:])


def _dep_spec(dep):
    return [] if dep is None else [ANY]


def _dep_arg(dep):
    return [] if dep is None else [dep]


def _postnorm_bwd_matmul(dh, y, gain, w3, widx, *, name, da_dtype, dep=None):
    T, D = dh.shape
    K = w3.shape[1]
    tm = min(ROW_BLOCK_WIDE, T)

    def body(dh_ref, y_ref, g_ref, w_ref, dy_ref, da_ref, dg_ref):
        i = pl.program_id(0)

        @pl.when(i == 0)
        def _():
            dg_ref[...] = jnp.zeros_like(dg_ref)

        dhv = dh_ref[...]
        dy, yn = _postnorm_bwd(dhv, y_ref[...].astype(F32), g_ref[...])
        dg_ref[...] += jnp.sum(dhv * yn, axis=0, keepdims=True)
        dyb = dy.astype(BF)
        dy_ref[...] = dyb
        da = lax.dot_general(dyb, w_ref[...], (((1,), (1,)), ((), ())), preferred_element_type=F32)
        da_ref[...] = da.astype(da_dtype)

    return pl.pallas_call(
        _after(body, 4, dep), name=name,
        grid=(T // tm,),
        in_specs=[
            pl.BlockSpec((tm, D), lambda i: (i, 0)),
            pl.BlockSpec((tm, D), lambda i: (i, 0)),
            pl.BlockSpec((1, D), lambda i: (0, 0)),
            pl.BlockSpec((None, K, D), lambda i: (widx, 0, 0)),
        ] + _dep_spec(dep),
        out_specs=[pl.BlockSpec((tm, D), lambda i: (i, 0)),
                   pl.BlockSpec((tm, K), lambda i: (i, 0)),
                   pl.BlockSpec((1, D), lambda i: (0, 0))],
        out_shape=[jax.ShapeDtypeStruct((T, D), BF), jax.ShapeDtypeStruct((T, K), da_dtype),
                   jax.ShapeDtypeStruct((1, D), F32)],
        compiler_params=_cp("arbitrary"),
    )(dh, y, gain, w3, *_dep_arg(dep))


def _postnorm_bwd_swiglu(dh, y, gain, wd3, layer, s, ut, *, name, dep=None):
    T, D = dh.shape
    F = wd3.shape[1]
    nf = F // 2
    tm = min(ROW_BLOCK, T)

    def body(dh_ref, y_ref, g_ref, w_ref, s_ref, ut_ref, dy_ref, dgo_ref, duo_ref, dgain_ref, dys):
        i = pl.program_id(0)
        j = pl.program_id(1)

        @pl.when((i == 0) & (j == 0))
        def _():
            dgain_ref[...] = jnp.zeros_like(dgain_ref)

        @pl.when(j == 0)
        def _():
            dhv = dh_ref[...]
            dy, yn = _postnorm_bwd(dhv, y_ref[...].astype(F32), g_ref[...])
            dgain_ref[...] += jnp.sum(dhv * yn, axis=0, keepdims=True)
            dyb = dy.astype(BF)
            dys[...] = dyb
            dy_ref[...] = dyb

        da = lax.dot_general(dys[...], w_ref[...], (((1,), (1,)), ((), ())), preferred_element_type=F32)
        dgo_ref[...] = (da * ut_ref[...].astype(F32)).astype(BF)
        duo_ref[...] = (da * s_ref[...].astype(F32)).astype(BF)

    act = jax.ShapeDtypeStruct((T, F), BF)
    return pl.pallas_call(
        _after(body, 6, dep), name=name,
        grid=(T // tm, 2),
        in_specs=[
            pl.BlockSpec((tm, D), lambda i, j: (i, 0)),
            pl.BlockSpec((tm, D), lambda i, j: (i, 0)),
            pl.BlockSpec((1, D), lambda i, j: (0, 0)),
            pl.BlockSpec((None, nf, D), lambda i, j: (layer, j, 0)),
            pl.BlockSpec((tm, nf), lambda i, j: (i, j)),
            pl.BlockSpec((tm, nf), lambda i, j: (i, j)),
        ] + _dep_spec(dep),
        out_specs=[pl.BlockSpec((tm, D), lambda i, j: (i, 0)),
                   pl.BlockSpec((tm, nf), lambda i, j: (i, j)),
                   pl.BlockSpec((tm, nf), lambda i, j: (i, j)),
                   pl.BlockSpec((1, D), lambda i, j: (0, 0))],
        out_shape=[jax.ShapeDtypeStruct((T, D), BF), act, act, jax.ShapeDtypeStruct((1, D), F32)],
        scratch_shapes=[pltpu.VMEM((tm, D), BF)],
        compiler_params=_cp("arbitrary", "arbitrary"),
    )(dh, y, gain, wd3, s, ut, *_dep_arg(dep))


def _matmul_prenorm_bwd(dzs, wg, layer, h, gain, dh_in, *, name):
    T, D = h.shape
    n = wg.shape[2]
    tm = min(ROW_BLOCK, T)
    per = N_CHIPS // len(dzs)

    def body(*refs):
        dz_refs = refs[:len(dzs)]
        w_ref, h_ref, g_ref, dhi_ref, dh_ref, dg_ref = refs[len(dzs):]

        @pl.when(pl.program_id(0) == 0)
        def _():
            dg_ref[...] = jnp.zeros_like(dg_ref)

        dhn = None
        for j in range(N_CHIPS):
            dz = dz_refs[j // per][:, (j % per) * n:(j % per + 1) * n]
            t = lax.dot_general(dz.astype(BF), w_ref[j], (((1,), (1,)), ((), ())), preferred_element_type=F32)
            dhn = t if dhn is None else dhn + t
        hv = h_ref[...]
        r = lax.rsqrt(jnp.mean(hv * hv, axis=-1, keepdims=True) + RMS_EPS)
        xh = hv * r
        dg_ref[...] += jnp.sum(dhn * xh, axis=0, keepdims=True)
        dxn = dhn * g_ref[...]
        dh_ref[...] = dhi_ref[...] + r * (dxn - xh * jnp.mean(dxn * xh, axis=-1, keepdims=True))

    rows = pl.BlockSpec((tm, D), lambda i: (i, 0))
    in_specs = [pl.BlockSpec((tm, per * n), lambda i: (i, 0)) for _ in dzs]
    in_specs += [pl.BlockSpec((N_CHIPS, D, n), lambda i: (0, layer, 0), pipeline_mode=pl.Buffered(1)),
                 rows, pl.BlockSpec((1, D), lambda i: (0, 0)), rows]
    return pl.pallas_call(
        body, name=name,
        grid=(T // tm,),
        in_specs=in_specs,
        out_specs=[rows, pl.BlockSpec((1, D), lambda i: (0, 0))],
        out_shape=[jax.ShapeDtypeStruct((T, D), F32), jax.ShapeDtypeStruct((1, D), F32)],
        compiler_params=_cp("arbitrary"),
    )(*dzs, wg, h, gain, dh_in)


def _grad_matmul(a, b, out_shape3, tme, tne, out_index, prev, *, name, dep=None):
    T, M = a.shape
    N = b.shape[1]
    tk = min(GRAD_CHUNK, T)
    nk = T // tk

    def body(a_ref, b_ref, *rest):
        o_ref, acc = rest[-2:]
        k = pl.program_id(2)
        part = jnp.dot(a_ref[...].astype(BF).T, b_ref[...].astype(BF), preferred_element_type=F32)

        @pl.when(k == 0)
        def _():
            acc[...] = part

        @pl.when(k > 0)
        def _():
            acc[...] += part

        @pl.when(k == nk - 1)
        def _():
            o_ref[...] = acc[...].astype(BF)

    in_specs = [pl.BlockSpec((tk, tme), lambda i, j, k: (k, i)),
                pl.BlockSpec((tk, tne), lambda i, j, k: (k, j))]
    args = [a, b]
    aliases = {}
    if prev is not None:
        in_specs.append(ANY)
        args.append(prev)
        aliases = {2: 0}
    in_specs += _dep_spec(dep)
    args += _dep_arg(dep)
    return pl.pallas_call(
        body, name=name,
        grid=(M // tme, N // tne, nk),
        in_specs=in_specs,
        out_specs=pl.BlockSpec((None, tme, tne), lambda i, j, k: out_index(i, j)),
        out_shape=jax.ShapeDtypeStruct(out_shape3, BF),
        scratch_shapes=[pltpu.VMEM((tme, tne), F32)],
        input_output_aliases=aliases,
        compiler_params=_cp("parallel", "parallel", "arbitrary"),
    )(*args)


def _row_tile(R, cap=512):
    fit = [t for t in range(16, min(R, cap) + 1, 16) if R % t == 0]
    return max(fit) if fit else R


def _cast_place(w3, layer, where, dtype, *, name, dep=None):
    _, R, C = w3.shape
    tr = _row_tile(R)

    def body(s_ref, w_ref, o_ref):
        o_ref[...] = w_ref[...].astype(o_ref.dtype)

    return pl.pallas_call(
        _after(body, 2, dep), name=name,
        grid_spec=pltpu.PrefetchScalarGridSpec(
            num_scalar_prefetch=1, grid=(R // tr,),
            in_specs=[pl.BlockSpec((None, tr, C), lambda i, s: (layer, i, 0))] + _dep_spec(dep),
            out_specs=pl.BlockSpec((None, tr, C), lambda i, s: (s[0], i, 0))),
        out_shape=jax.ShapeDtypeStruct((N_CHIPS, R, C), dtype),
        compiler_params=_cp("arbitrary"),
    )(where, w3, *_dep_arg(dep))


def _pair_sum(dw, theirs, where, *, name):
    G, rh, C = theirs.shape
    tr = _row_tile(rh)
    nr = rh // tr

    def body(s_ref, a_ref, b_ref, o_ref):
        o_ref[...] = (a_ref[...].astype(F32) + b_ref[...].astype(F32)).astype(BF)

    mine = pl.BlockSpec((None, tr, C), lambda g, i, s: (g, s[1] * nr + i, 0))
    spec = pl.BlockSpec((None, tr, C), lambda g, i, s: (g, i, 0))
    return pl.pallas_call(
        body, name=name,
        grid_spec=pltpu.PrefetchScalarGridSpec(
            num_scalar_prefetch=1, grid=(G, nr), in_specs=[mine, spec], out_specs=spec),
        out_shape=jax.ShapeDtypeStruct((G, rh, C), BF),
        compiler_params=_cp("arbitrary", "arbitrary"),
    )(where, dw, theirs)


def _chip_sum(landed, parts, where, total_rows, row_off, prev, *, name):
    G, rh, C = landed.shape
    tr = _row_tile(rh)
    nr = rh // tr
    base = row_off // tr

    def body(s_ref, l_ref, p_ref, *rest):
        o_ref = rest[-1]
        for j in range(G):
            def own(j=j):
                v = p_ref[...].astype(F32)
                o_ref[...] = v if j == 0 else o_ref[...] + v

            def other(j=j):
                v = l_ref[j].astype(F32)
                o_ref[...] = v if j == 0 else o_ref[...] + v

            pl.when(s_ref[0] == j)(own)
            pl.when(s_ref[0] != j)(other)

    in_specs = [pl.BlockSpec((G, tr, C), lambda i, s: (0, i, 0)),
                pl.BlockSpec((None, tr, C), lambda i, s: (s[0], i, 0))]
    args = [where, landed, parts]
    aliases = {}
    if prev is not None:
        in_specs.append(ANY)
        args.append(prev)
        aliases = {3: 0}
    return pl.pallas_call(
        body, name=name,
        grid_spec=pltpu.PrefetchScalarGridSpec(
            num_scalar_prefetch=1, grid=(nr,),
            in_specs=in_specs,
            out_specs=pl.BlockSpec((tr, C), lambda i, s: (base + s[1] * nr + i, 0))),
        out_shape=jax.ShapeDtypeStruct((total_rows, C), F32),
        input_output_aliases=aliases,
        compiler_params=_cp("arbitrary"),
    )(*args)


def _adamw(w, g, m, v, *, name, emit_grad=False):
    R, C = w.shape
    tr = _row_tile(R, cap=max(16, ADAMW_BLOCK_BYTES // (4 * C)))
    n_out = 4 if emit_grad else 3

    def body(w_ref, g_ref, m_ref, v_ref, d_ref, mo_ref, vo_ref, *go_ref):
        gv = g_ref[...]
        if emit_grad:
            go_ref[0][...] = gv
        mn = ADAM_B1 * m_ref[...] + (1.0 - ADAM_B1) * gv
        vn = ADAM_B2 * v_ref[...] + (1.0 - ADAM_B2) * jnp.square(gv)
        m_hat = mn / (1.0 - ADAM_B1 ** ADAM_STEP)
        v_hat = vn / (1.0 - ADAM_B2 ** ADAM_STEP)
        d_ref[...] = -ADAM_LR * (m_hat / (jnp.sqrt(v_hat) + ADAM_EPS) + ADAM_WD * w_ref[...])
        mo_ref[...] = mn
        vo_ref[...] = vn

    spec = pl.BlockSpec((tr, C), lambda i: (i, 0))
    shp = jax.ShapeDtypeStruct((R, C), F32)
    return pl.pallas_call(
        body, name=name, grid=(R // tr,), in_specs=[spec] * 4, out_specs=[spec] * n_out,
        out_shape=[shp] * n_out, compiler_params=_cp("parallel"),
    )(w, g, m, v)


def _place():
    x = lax.axis_index("x")
    y = lax.axis_index("y")
    c = lax.axis_index("c")
    chips = [(1 - x, y), (x, 1 - y), (1 - x, 1 - y)]
    return x, y, c, chips


def _chunk_rows(rows, row_bytes, align):
    if rows <= align:
        return rows
    cands = [r for r in range(align, rows + 1, align) if rows % r == 0]
    fit = [r for r in cands if r * row_bytes <= DMA_CHUNK_BYTES]
    return max(fit) if fit else min(cands)


def _row_align(dtype):
    return 8 * (4 // jnp.dtype(dtype).itemsize)


def _start_chunks(make, rows, rc):
    for r0 in range(0, rows, rc):
        make(r0, rc).start()


def _piece_rows(ref, piece, j, h, r0=0, n=None):
    _, lead, off, rows = piece
    rh = rows // 2
    n = rh if n is None else n
    if lead is not None:
        return ref.at[lead, j, pl.ds(h * rh + r0, n)]
    return ref.at[j, pl.ds(off + h * rh + r0, n)]


def _piece_chunk(arr, piece):
    rh = piece[3] // 2
    return rh, _chunk_rows(rh, arr.shape[-1] * arr.dtype.itemsize, _row_align(arr.dtype))


def _ag_start(arrays, taps, groups, *, name):
    na = len(arrays)
    ng = len(groups)
    nt = 0 if taps is None else 1
    n_sem = [3 * len(grp) + (3 if nt and g == 0 else 0) for g, grp in enumerate(groups)]

    def body(*refs):
        ins = refs[:na]
        taps_ref = refs[na] if nt else None
        sems = refs[na + nt:na + nt + 2 * ng]
        token = refs[-1]
        token[...] = jnp.zeros_like(token)
        x, y, c, chips = _place()
        myj = 2 * x + y
        for g, grp in enumerate(groups):
            ssem, rsem = sems[2 * g], sems[2 * g + 1]
            for idx, piece in enumerate(grp):
                ref = ins[piece[0]]
                rh, rc = _piece_chunk(arrays[piece[0]], piece)
                for k, (px, py) in enumerate(chips):
                    def send(r0, n, ref=ref, piece=piece, idx=idx, k=k, px=px, py=py, ssem=ssem, rsem=rsem):
                        part = _piece_rows(ref, piece, myj, c, r0, n)
                        return pltpu.make_async_remote_copy(
                            src_ref=part, dst_ref=part, send_sem=ssem.at[3 * idx + k], recv_sem=rsem.at[3 * idx + k],
                            device_id=(px, py, c), device_id_type=MESH)
                    _start_chunks(send, rh, rc)
            if nt and g == 0:
                for k, (px, py) in enumerate(chips):
                    pltpu.make_async_remote_copy(
                        src_ref=taps_ref.at[myj], dst_ref=taps_ref.at[myj],
                        send_sem=ssem.at[3 * len(grp) + k], recv_sem=rsem.at[3 * len(grp) + k],
                        device_id=(px, py, c), device_id_type=MESH).start()

    sem_shapes = []
    for n in n_sem:
        sem_shapes += [pltpu.SemaphoreType.DMA((n,)), pltpu.SemaphoreType.DMA((n,))]
    ops = list(arrays) + ([taps] if nt else [])
    bufs = [pltpu.HBM(a.shape, a.dtype) for a in ops]
    outs = pl.pallas_call(
        body, name=name,
        out_shape=(*sem_shapes, *bufs, jax.ShapeDtypeStruct((8, LANES), F32)),
        in_specs=[HBM] * (na + nt),
        out_specs=(*([SEM] * (2 * ng)), *([HBM] * (na + nt)), pl.BlockSpec(memory_space=pltpu.VMEM)),
        input_output_aliases={i: 2 * ng + i for i in range(na + nt)},
        compiler_params=pltpu.CompilerParams(has_side_effects=EFFECT),
    )(*[_in_hbm(a) for a in ops])
    sems = [(outs[2 * g], outs[2 * g + 1]) for g in range(ng)]
    return sems, list(outs[2 * ng:2 * ng + na]), (outs[2 * ng + na] if nt else None), outs[-1]


def _ag_wait(sems, vals, taps, group, after, *, name):
    nv = len(vals)
    extra = ([taps] if taps is not None else [])
    nb = nv + len(extra)

    def body(*refs):
        bufs = refs[:nb]
        ssem, rsem = refs[nb], refs[nb + 1]
        x, y, c, chips = _place()
        for idx, piece in enumerate(group):
            for k, (px, py) in enumerate(chips):
                got = _piece_rows(bufs[piece[0]], piece, 2 * px + py, c)
                cp = pltpu.make_async_remote_copy(
                    src_ref=got, dst_ref=got, send_sem=ssem.at[3 * idx + k], recv_sem=rsem.at[3 * idx + k],
                    device_id=(px, py, c), device_id_type=MESH)
                cp.wait_send()
                cp.wait_recv()
        if taps is not None:
            for k, (px, py) in enumerate(chips):
                got = bufs[nv].at[2 * px + py]
                cp = pltpu.make_async_remote_copy(
                    src_ref=got, dst_ref=got, send_sem=ssem.at[3 * len(group) + k],
                    recv_sem=rsem.at[3 * len(group) + k], device_id=(px, py, c), device_id_type=MESH)
                cp.wait_send()
                cp.wait_recv()

    ops = list(vals) + extra
    shapes = [pltpu.HBM(a.shape, a.dtype) for a in ops]
    outs = pl.pallas_call(
        body, name=name,
        out_shape=tuple(shapes),
        in_specs=[HBM] * nb + [SEM, SEM] + _dep_spec(after),
        out_specs=[HBM] * nb,
        input_output_aliases={i: i for i in range(nb)},
        compiler_params=pltpu.CompilerParams(has_side_effects=EFFECT),
    )(*ops, sems[0], sems[1], *_dep_arg(after))
    return list(outs[:nv]), (outs[nv] if taps is not None else None)


def _ag_forward(vals, group, *, name):
    nv = len(vals)
    npc = len(group)

    def body(*refs):
        bufs = refs[nv:2 * nv]
        fsem, gsem = refs[2 * nv:]
        x, y, c, chips = _place()
        sib = (x, y, 1 - c)
        sent = []
        for idx, piece in enumerate(group):
            rh, rc = _piece_chunk(vals[piece[0]], piece)
            for k, (px, py) in enumerate(chips):
                def fwd(r0, n, piece=piece, idx=idx, k=k, pj=2 * px + py):
                    part = _piece_rows(bufs[piece[0]], piece, pj, c, r0, n)
                    return pltpu.make_async_remote_copy(
                        src_ref=part, dst_ref=part, send_sem=fsem.at[3 * idx + k], recv_sem=gsem.at[3 * idx + k],
                        device_id=sib, device_id_type=MESH)
                _start_chunks(fwd, rh, rc)
                sent.append(fwd(0, rh))
        for idx, piece in enumerate(group):
            for k, (px, py) in enumerate(chips):
                theirs = _piece_rows(bufs[piece[0]], piece, 2 * px + py, 1 - c)
                pltpu.make_async_remote_copy(
                    src_ref=theirs, dst_ref=theirs, send_sem=fsem.at[3 * idx + k], recv_sem=gsem.at[3 * idx + k],
                    device_id=sib, device_id_type=MESH).wait_recv()
        for cp in sent:
            cp.wait_send()

    return pl.pallas_call(
        body, name=name,
        in_specs=[ANY] * nv, out_specs=[ANY] * nv,
        out_shape=[jax.ShapeDtypeStruct(v.shape, v.dtype) for v in vals],
        input_output_aliases={i: i for i in range(nv)},
        scratch_shapes=[pltpu.SemaphoreType.DMA((3 * npc,)), pltpu.SemaphoreType.DMA((3 * npc,))],
    )(*vals)


def _sibling_swap(dws, *, name):
    nm = len(dws)
    shapes = [jax.ShapeDtypeStruct((dw.shape[0], dw.shape[1] // 2, dw.shape[2]), dw.dtype) for dw in dws]

    def body(*refs):
        ins = refs[:nm]
        theirs = refs[nm:2 * nm]
        ssem, rsem = refs[2 * nm:]
        x, y, c, _ = _place()
        sib = (x, y, 1 - c)
        cps = []
        for m in range(nm):
            G, rh, cols = shapes[m].shape
            rc = _chunk_rows(rh, cols * shapes[m].dtype.itemsize, _row_align(shapes[m].dtype))
            for j in range(G):
                _start_chunks(lambda r0, n, m=m, j=j, rh=rh: pltpu.make_async_remote_copy(
                    src_ref=ins[m].at[j, pl.ds((1 - c) * rh + r0, n)],
                    dst_ref=theirs[m].at[j, pl.ds(r0, n)], send_sem=ssem.at[m], recv_sem=rsem.at[m],
                    device_id=sib, device_id_type=MESH), rh, rc)
            cps.append(pltpu.make_async_remote_copy(
                src_ref=ins[m].at[:, pl.ds((1 - c) * rh, rh), :], dst_ref=theirs[m],
                send_sem=ssem.at[m], recv_sem=rsem.at[m], device_id=sib, device_id_type=MESH))
        for cp in cps:
            cp.wait()

    return pl.pallas_call(
        body, name=name,
        in_specs=[ANY] * nm, out_specs=[ANY] * nm, out_shape=shapes,
        scratch_shapes=[pltpu.SemaphoreType.DMA((nm,)), pltpu.SemaphoreType.DMA((nm,))],
    )(*dws)


HBM = pl.BlockSpec(memory_space=pltpu.HBM)
SEM = pl.BlockSpec(memory_space=pltpu.SEMAPHORE)
EFFECT = pltpu.SideEffectType.DATAFLOW_SIDE_EFFECTING


def _in_hbm(a):
    return pltpu.with_memory_space_constraint(a, pltpu.HBM)


def _swap_start(dws, *, name):
    nm = len(dws)
    lands = [(dw.shape[0], dw.shape[1] // 2, dw.shape[2]) for dw in dws]

    def body(*refs):
        ins = refs[:nm]
        lnd = refs[nm:2 * nm]
        ssem, rsem = refs[2 * nm:2 * nm + 2]
        token = refs[-1]
        x, y, c, _ = _place()
        sib = (x, y, 1 - c)
        for m in range(nm):
            G, rh, cols = lands[m]
            rc = _chunk_rows(rh, cols * dws[m].dtype.itemsize, _row_align(dws[m].dtype))
            for j in range(G):
                _start_chunks(lambda r0, n, m=m, j=j, rh=rh: pltpu.make_async_remote_copy(
                    src_ref=ins[m].at[j, pl.ds((1 - c) * rh + r0, n)], dst_ref=lnd[m].at[j, pl.ds(r0, n)],
                    send_sem=ssem.at[m], recv_sem=rsem.at[m], device_id=sib, device_id_type=MESH), rh, rc)
        token[...] = jnp.zeros_like(token)

    src = [pltpu.HBM(dw.shape, dw.dtype) for dw in dws]
    dst = [pltpu.HBM(s, dw.dtype) for s, dw in zip(lands, dws)]
    outs = pl.pallas_call(
        body, name=name,
        out_shape=(pltpu.SemaphoreType.DMA((nm,)), pltpu.SemaphoreType.DMA((nm,)), *src, *dst,
                   jax.ShapeDtypeStruct((8, LANES), F32)),
        in_specs=[HBM] * (2 * nm),
        out_specs=(SEM, SEM, *([HBM] * (2 * nm)), pl.BlockSpec(memory_space=pltpu.VMEM)),
        input_output_aliases={i: 2 + i for i in range(2 * nm)},
        compiler_params=pltpu.CompilerParams(has_side_effects=EFFECT),
    )(*[_in_hbm(dw) for dw in dws], *[_in_hbm(lax.empty(s, dw.dtype)) for s, dw in zip(lands, dws)])
    return (outs[0], outs[1], list(outs[2:2 + nm]), list(outs[2 + nm:2 + 2 * nm])), outs[-1]


def _swap_wait(handle, after, *, name):
    ssem_in, rsem_in, dws, lands = handle
    nm = len(dws)

    def body(*refs):
        ins = refs[:nm]
        lnd = refs[nm:2 * nm]
        ssem, rsem = refs[2 * nm:2 * nm + 2]
        x, y, c, _ = _place()
        for m in range(nm):
            rh = lands[m].shape[1]
            cp = pltpu.make_async_remote_copy(
                src_ref=ins[m].at[:, pl.ds((1 - c) * rh, rh), :], dst_ref=lnd[m],
                send_sem=ssem.at[m], recv_sem=rsem.at[m], device_id=(x, y, 1 - c), device_id_type=MESH)
            cp.wait_send()
            cp.wait_recv()

    bufs = [pltpu.HBM(a.shape, a.dtype) for a in list(dws) + list(lands)]
    outs = pl.pallas_call(
        body, name=name,
        out_shape=tuple(bufs),
        in_specs=[HBM] * (2 * nm) + [SEM, SEM, ANY],
        out_specs=[HBM] * (2 * nm),
        input_output_aliases={i: i for i in range(2 * nm)},
        compiler_params=pltpu.CompilerParams(has_side_effects=EFFECT),
    )(*dws, *lands, ssem_in, rsem_in, after)
    return list(outs[:nm]), list(outs[nm:])


def _exchange_start(parts, *, name):
    nm = len(parts)

    def body(*refs):
        ins = refs[:nm]
        lands = refs[nm:2 * nm]
        ssem, rsem = refs[2 * nm:2 * nm + 2]
        token = refs[-1]
        x, y, c, chips = _place()
        myj = 2 * x + y
        for m in range(nm):
            _, rh, cols = parts[m].shape
            rc = _chunk_rows(rh, cols * parts[m].dtype.itemsize, _row_align(parts[m].dtype))
            for k, (px, py) in enumerate(chips):
                _start_chunks(lambda r0, n, m=m, k=k, px=px, py=py: pltpu.make_async_remote_copy(
                    src_ref=ins[m].at[2 * px + py, pl.ds(r0, n)], dst_ref=lands[m].at[myj, pl.ds(r0, n)],
                    send_sem=ssem.at[3 * m + k], recv_sem=rsem.at[3 * m + k],
                    device_id=(px, py, c), device_id_type=MESH), rh, rc)
        token[...] = jnp.zeros_like(token)

    bufs = [pltpu.HBM(p.shape, p.dtype) for p in parts]
    outs = pl.pallas_call(
        body, name=name,
        out_shape=(pltpu.SemaphoreType.DMA((3 * nm,)), pltpu.SemaphoreType.DMA((3 * nm,)), *bufs, *bufs,
                   jax.ShapeDtypeStruct((8, LANES), F32)),
        in_specs=[HBM] * (2 * nm),
        out_specs=(SEM, SEM, *([HBM] * (2 * nm)), pl.BlockSpec(memory_space=pltpu.VMEM)),
        input_output_aliases={i: 2 + i for i in range(2 * nm)},
        compiler_params=pltpu.CompilerParams(has_side_effects=EFFECT),
    )(*[_in_hbm(p) for p in parts], *[_in_hbm(lax.empty(p.shape, p.dtype)) for p in parts])
    return (outs[0], outs[1], list(outs[2:2 + nm]), list(outs[2 + nm:2 + 2 * nm])), outs[-1]


def _exchange_wait(handle, after, *, name):
    ssem_in, rsem_in, parts, lands = handle
    nm = len(parts)

    def body(*refs):
        ins = refs[:nm]
        lnd = refs[nm:2 * nm]
        ssem, rsem = refs[2 * nm:2 * nm + 2]
        x, y, c, chips = _place()
        for m in range(nm):
            for k, (px, py) in enumerate(chips):
                pj = 2 * px + py
                cp = pltpu.make_async_remote_copy(
                    src_ref=ins[m].at[pj], dst_ref=lnd[m].at[pj],
                    send_sem=ssem.at[3 * m + k], recv_sem=rsem.at[3 * m + k],
                    device_id=(px, py, c), device_id_type=MESH)
                cp.wait_send()
                cp.wait_recv()

    bufs = [pltpu.HBM(p.shape, p.dtype) for p in parts]
    outs = pl.pallas_call(
        body, name=name,
        out_shape=(*bufs, *bufs),
        in_specs=[HBM] * (2 * nm) + [SEM, SEM, ANY],
        out_specs=[HBM] * (2 * nm),
        input_output_aliases={i: i for i in range(2 * nm)},
        compiler_params=pltpu.CompilerParams(has_side_effects=EFFECT),
    )(*parts, *lands, ssem_in, rsem_in, after)
    return list(outs[nm:]), list(outs[:nm])


def _sibling_join(grads, regions, *, name):
    nm = len(grads)
    nr = len(regions)
    shapes = [jax.ShapeDtypeStruct(g.shape, g.dtype) for g in grads]

    def body(*refs):
        outs = refs[nm:2 * nm]
        ssem, rsem = refs[2 * nm:]
        x, y, c, _ = _place()
        sib = (x, y, 1 - c)
        cps = []
        for i, (m, off, rows) in enumerate(regions):
            rh, cols = rows // 2, grads[m].shape[1]
            rc = _chunk_rows(rh, cols * grads[m].dtype.itemsize, _row_align(grads[m].dtype))

            def send(r0, n, i=i, m=m, off=off, rh=rh):
                part = outs[m].at[pl.ds(off + c * rh + r0, n)]
                return pltpu.make_async_remote_copy(
                    src_ref=part, dst_ref=part, send_sem=ssem.at[i], recv_sem=rsem.at[i],
                    device_id=sib, device_id_type=MESH)
            _start_chunks(send, rh, rc)
            cps.append(send(0, rh))
        for i, (m, off, rows) in enumerate(regions):
            rh = rows // 2
            cps[i].wait_send()
            got = outs[m].at[pl.ds(off + (1 - c) * rh, rh)]
            pltpu.make_async_remote_copy(
                src_ref=got, dst_ref=got, send_sem=ssem.at[i], recv_sem=rsem.at[i],
                device_id=sib, device_id_type=MESH).wait_recv()

    return pl.pallas_call(
        body, name=name,
        in_specs=[ANY] * nm, out_specs=[ANY] * nm, out_shape=shapes,
        input_output_aliases={i: i for i in range(nm)},
        scratch_shapes=[pltpu.SemaphoreType.DMA((nr,)), pltpu.SemaphoreType.DMA((nr,))],
    )(*grads)


def _all_reduce_small(pack):
    R, C = pack.shape

    def body(in_ref, out_ref, slots, ssem, rsem):
        x, y, c, _ = _place()
        me = 4 * x + 2 * y + c
        slots[me] = in_ref[...]
        cps = []
        for k in range(1, N_DEV):
            dx, dy, dc = (k >> 2) & 1, (k >> 1) & 1, k & 1
            peer = (x ^ dx, y ^ dy, c ^ dc)
            cp = pltpu.make_async_remote_copy(
                src_ref=in_ref, dst_ref=slots.at[me], send_sem=ssem.at[k], recv_sem=rsem.at[k],
                device_id=peer, device_id_type=MESH)
            cp.start()
            cps.append(cp)
        for k in range(1, N_DEV):
            dx, dy, dc = (k >> 2) & 1, (k >> 1) & 1, k & 1
            got = slots.at[4 * (x ^ dx) + 2 * (y ^ dy) + (c ^ dc)]
            pltpu.make_async_remote_copy(
                src_ref=got, dst_ref=got, send_sem=ssem.at[k], recv_sem=rsem.at[k],
                device_id=(x ^ dx, y ^ dy, c ^ dc), device_id_type=MESH).wait_recv()
        for cp in cps:
            cp.wait_send()
        acc = slots[0]
        for s in range(1, N_DEV):
            acc = acc + slots[s]
        out_ref[...] = acc

    return pl.pallas_call(
        body, name="ar_small",
        in_specs=[pl.BlockSpec(memory_space=pltpu.VMEM)],
        out_specs=pl.BlockSpec(memory_space=pltpu.VMEM),
        out_shape=jax.ShapeDtypeStruct((R, C), F32),
        scratch_shapes=[pltpu.VMEM((N_DEV, R, C), F32),
                        pltpu.SemaphoreType.DMA((N_DEV,)), pltpu.SemaphoreType.DMA((N_DEV,))],
    )(pack)


def kernel(x, positions, mix_norm_pre, mix_norm_post, ffn_norm_pre, ffn_norm_post, ffn_w_gate_up, ffn_w_down, conv_w_in, conv_w, conv_w_out, kv_norm, w_kv, w_q, w_o, loss_target, m_mix_norm_pre, m_mix_norm_post, m_ffn_norm_pre, m_ffn_norm_post, m_ffn_w_gate_up, m_ffn_w_down, m_conv_w_in, m_conv_w, m_conv_w_out, m_kv_norm, m_w_kv, m_w_q, m_w_o, v_mix_norm_pre, v_mix_norm_post, v_ffn_norm_pre, v_ffn_norm_post, v_ffn_w_gate_up, v_ffn_w_down, v_conv_w_in, v_conv_w, v_conv_w_out, v_kv_norm, v_w_kv, v_w_q, v_w_o):
    T, D = x.shape[1], x.shape[2]
    L = ffn_w_gate_up.shape[0]
    n_gu = ffn_w_gate_up.shape[2]
    f_sh = ffn_w_down.shape[1]
    F = N_CHIPS * f_sh
    x0 = x[0]
    tgt = loss_target[0]

    half = HEAD_DIM // 2
    inv_freq = ROPE_THETA ** (-jnp.arange(half, dtype=F32) / half)
    ang = positions[0].astype(F32)[:, None] * inv_freq
    cosv, sinv = jnp.cos(ang), jnp.sin(ang)
    cos2 = jnp.tile(cosv, (1, LANES // half))
    ss2 = jnp.tile(jnp.concatenate([-sinv, sinv], axis=1), (1, LANES // HEAD_DIM))

    def as2d(a):
        return a.reshape(-1, a.shape[-1])

    big = [ffn_w_gate_up, ffn_w_down, conv_w_in, conv_w_out, w_kv, w_q, w_o]
    big_m = [m_ffn_w_gate_up, m_ffn_w_down, m_conv_w_in, m_conv_w_out, m_w_kv, m_w_q, m_w_o]
    big_v = [v_ffn_w_gate_up, v_ffn_w_down, v_conv_w_in, v_conv_w_out, v_w_kv, v_w_q, v_w_o]
    chip = 2 * lax.axis_index("x") + lax.axis_index("y")
    where = jnp.stack([chip, lax.axis_index("c")]).astype(jnp.int32)
    tc = conv_w.shape[2]
    cw_pad = jnp.concatenate([conv_w[0], jnp.zeros((8 - conv_w.shape[1], tc), F32)], axis=0)

    GU0, GU1, WD0, WD1, WCI, WCO, WKV, WQ, WO = range(9)
    shards = [(ffn_w_gate_up, 0), (ffn_w_gate_up, 1), (ffn_w_down, 0), (ffn_w_down, 1), (conv_w_in, 0),
              (conv_w_out, 0), (w_kv[None], 0), (w_q, 0), (w_o, 0)]
    ag_groups = [
        [(WCI, None, 0, D), (WCO, None, 0, D // N_CHIPS)],
        [(GU0, None, 0, D), (WD0, None, 0, f_sh)],
        [(WKV, None, 0, D), (WQ, None, 0, D)],
        [(WO, None, 0, D // N_CHIPS), (GU1, None, 0, D), (WD1, None, 0, f_sh)],
    ]

    def localised(group, idxs):
        return [(idxs.index(p[0]),) + p[1:] for p in group]

    cur = [None] * len(shards)
    first = [WCI, WCO]
    sems0, vals, taps, token = _ag_start(
        [_cast_place(*shards[i], where, BF, name=f"place{i}") for i in first],
        _cast_place(cw_pad[None], 0, where, F32, name="place_taps"), [localised(ag_groups[0], first)],
        name="ag_start0")
    for i, v in zip(first, vals):
        cur[i] = v
    rest = [i for i in range(len(shards)) if i not in first]
    sems1, vals, _, _ = _ag_start(
        [_cast_place(*shards[i], where, BF, name=f"place{i}", dep=token) for i in rest], None,
        [localised(g, rest) for g in ag_groups[1:]], name="ag_start1")
    for i, v in zip(rest, vals):
        cur[i] = v
    ag_sems = sems0 + sems1

    def gather_group(g, after):
        nonlocal taps
        idxs = sorted({p[0] for p in ag_groups[g]})
        local = localised(ag_groups[g], idxs)
        vals, landed_taps = _ag_wait(ag_sems[g], [cur[i] for i in idxs], taps if g == 0 else None, local, after,
                                     name=f"ag_wait{g}")
        if g == 0:
            taps = landed_taps
        vals = _ag_forward(vals, local, name=f"ag_forward{g}")
        for i, v in zip(idxs, vals):
            cur[i] = v

    def row(a, i):
        return a[i:i + 1]

    gather_group(0, None)
    wci, wco, cw = cur[WCI], cur[WCO].reshape(1, D, D), taps
    z, hn_m0 = _norm_matmul(x0, row(mix_norm_pre, 0), wci, cos2, ss2, name="f0_conv_in",
                            rope_shards=0, scale=1.0, out_dtype=BF)
    vmix = _conv_fwd(z, cw, name="f0_conv")
    y0, h1 = _matmul_postnorm(vmix, wco, 0, row(mix_norm_post, 0), x0, name="f0_conv_out")
    gather_group(1, h1)
    wgu0, wd0 = cur[GU0], cur[WD0].reshape(1, F, D)
    s0, ut0, a0, hn_f0 = _norm_swiglu(h1, row(ffn_norm_pre, 0), wgu0, 0, name="f0_gate_up")
    f0, h2 = _matmul_postnorm(a0, wd0, 0, row(ffn_norm_post, 0), h1, name="f0_down")

    gather_group(2, h2)
    wkv, wq = cur[WKV], cur[WQ]
    kv_all, hn_kv = _norm_matmul(h2, kv_norm.reshape(1, D), wkv, cos2, ss2, name="f1_kv",
                                 rope_shards=N_CHIPS // 2, scale=1.0, out_dtype=BF)
    q_all, hn_m1 = _norm_matmul(h2, row(mix_norm_pre, 1), wq, cos2, ss2, name="f1_q",
                                rope_shards=N_CHIPS, scale=HEAD_DIM ** -0.5, out_dtype=BF)
    o_att, lse = _attn_fwd(q_all, kv_all, name="f1_attn")
    gather_group(3, o_att)
    wgu1, wd1, wo = cur[GU1], cur[WD1].reshape(1, F, D), cur[WO].reshape(1, D, D)
    y1, h3 = _matmul_postnorm(o_att, wo, 0, row(mix_norm_post, 1), h2, name="f1_attn_out")
    s1, ut1, a1, hn_f1 = _norm_swiglu(h3, row(ffn_norm_pre, 1), wgu1, 0, name="f1_gate_up")
    f1, dh4, sq = _matmul_postnorm_loss(a1, wd1, 0, row(ffn_norm_post, 1), h3, tgt, name="f1_down_loss")
    loss_part = 0.5 * sq[0, 0] / D

    gu_shape = (N_CHIPS, D, n_gu)
    in_chips = lambda a: a.reshape(N_CHIPS, -1, a.shape[-1])

    def scatter_start(dws, tag):
        theirs = _sibling_swap(dws, name=f"rs_swap_{tag}")
        parts = [_pair_sum(dw, t, where, name=f"rs_pair_sum_{tag}{i}") for i, (dw, t) in enumerate(zip(dws, theirs))]
        return _exchange_start(parts, name=f"rs_exchange_start_{tag}")

    def scatter_go(swap, after, tag):
        dws, theirs = _swap_wait(swap, after, name=f"rs_swap_wait_{tag}")
        parts = [_pair_sum(dw, t, where, name=f"rs_pair_sum_{tag}{i}") for i, (dw, t) in enumerate(zip(dws, theirs))]
        return _exchange_start(parts, name=f"rs_exchange_start_{tag}")

    dyf1, dg1, du1, d_ffn_post1 = _postnorm_bwd_swiglu(dh4, f1, row(ffn_norm_post, 1), wd1, 0, s1, ut1,
                                                       name="b1_down")
    dwd1 = _grad_matmul(a1, dyf1, (2, F // 2, D), F // 2, D, lambda i, j: (i, 0, 0), None, name="b1_dw_down")
    dwgu1 = _grad_matmul(hn_f1, dg1, gu_shape, D, n_gu, lambda i, j: (j, 0, 0), None, name="b1_dw_gate")
    dwgu1 = _grad_matmul(hn_f1, du1, gu_shape, D, n_gu, lambda i, j: (j + 2, 0, 0), dwgu1, name="b1_dw_up")
    dh3, d_ffn_pre1 = _matmul_prenorm_bwd((dg1, du1), wgu1, 0, h3, row(ffn_norm_pre, 1), dh4, name="b1_gate_up")

    dy1, do, d_mix_post1 = _postnorm_bwd_matmul(dh3, y1, row(mix_norm_post, 1), wo, 0, name="b1_attn_out",
                                                da_dtype=F32)
    dwo = _grad_matmul(o_att, dy1, (1, D, D), D, D, lambda i, j: (0, 0, 0), None, name="b1_dw_o")
    swap_a, token = _swap_start([dwgu1, in_chips(dwd1), in_chips(dwo)], name="rs_swap_start_a")
    prev = None
    for gi, (window, dil) in enumerate(BRANCHES):
        prev = _attn_bwd(q_all, kv_all, do, o_att, lse, cos2, ss2, gi, dil, prev, name=f"b1_attn{gi}", dep=token)
        token = None
        if gi == 0:
            rs_a, token = scatter_go(swap_a, prev[0], "a")
    dq_all, dk_all, dv_all = prev
    n_q = wq.shape[2]
    n_kv = wkv.shape[2]
    dwq = _grad_matmul(hn_m1, dq_all, (N_CHIPS, D, n_q), D, n_q, lambda i, j: (j, 0, 0), None, name="b1_dw_q")
    dwkv = _grad_matmul(hn_kv, dk_all, (N_CHIPS, D, n_kv), D, n_kv, lambda i, j: (j, 0, 0), None, name="b1_dw_k")
    dwkv = _grad_matmul(hn_kv, dv_all, (N_CHIPS, D, n_kv), D, n_kv, lambda i, j: (j + 2, 0, 0), dwkv, name="b1_dw_v")
    dh2, d_mix_pre1 = _matmul_prenorm_bwd((dq_all,), wq, 0, h2, row(mix_norm_pre, 1), dh3, name="b1_q")
    dh2, d_kv_norm = _matmul_prenorm_bwd((dk_all, dv_all), wkv, 0, h2, kv_norm.reshape(1, D), dh2, name="b1_kv")
    swap_b, token = _swap_start([dwkv, dwq], name="rs_swap_start_b")

    dyf0, dg0, du0, d_ffn_post0 = _postnorm_bwd_swiglu(dh2, f0, row(ffn_norm_post, 0), wd0, 0, s0, ut0,
                                                       name="b0_down", dep=token)
    rs_b, token = scatter_go(swap_b, dyf0, "b")
    dwd0 = _grad_matmul(a0, dyf0, (2, F // 2, D), F // 2, D, lambda i, j: (i, 0, 0), None, name="b0_dw_down",
                        dep=token)
    dwgu0 = _grad_matmul(hn_f0, dg0, gu_shape, D, n_gu, lambda i, j: (j, 0, 0), None, name="b0_dw_gate")
    dwgu0 = _grad_matmul(hn_f0, du0, gu_shape, D, n_gu, lambda i, j: (j + 2, 0, 0), dwgu0, name="b0_dw_up")
    dh1, d_ffn_pre0 = _matmul_prenorm_bwd((dg0, du0), wgu0, 0, h1, row(ffn_norm_pre, 0), dh2, name="b0_gate_up")
    swap_c, token = _swap_start([dwgu0, in_chips(dwd0)], name="rs_swap_start_c")

    dy0, dvmix, d_mix_post0 = _postnorm_bwd_matmul(dh1, y0, row(mix_norm_post, 0), wco, 0, name="b0_conv_out",
                                                   da_dtype=BF, dep=token)
    rs_c, token = scatter_go(swap_c, dy0, "c")
    dwco = _grad_matmul(vmix, dy0, (1, D, D), D, D, lambda i, j: (0, 0, 0), None, name="b0_dw_conv_out",
                        dep=token)
    dz, dcw = _conv_bwd(z, cw, dvmix, name="b0_conv")
    n_ci = wci.shape[2]
    dwci = _grad_matmul(hn_m0, dz, (N_CHIPS, D, n_ci), D, n_ci, lambda i, j: (j, 0, 0), None, name="b0_dw_conv_in")
    dx, d_mix_pre0 = _matmul_prenorm_bwd((dz,), wci, 0, x0, row(mix_norm_pre, 0), dh1, name="b0_conv_in")

    pack = jnp.concatenate([
        d_mix_pre0, d_mix_pre1, d_mix_post0, d_mix_post1, d_ffn_pre0, d_ffn_pre1, d_ffn_post0, d_ffn_post1,
        d_kv_norm, dcw[0:3], jnp.full((1, D), loss_part, F32),
        jnp.zeros((SMALL_ROWS - 13, D), F32)], axis=0)
    red = _all_reduce_small(pack)
    loss = red[12, 0]
    myj = 2 * lax.axis_index("x") + lax.axis_index("y")
    g_conv_w = lax.dynamic_slice(red, (9, myj * tc), (3, tc))

    zeros7 = jnp.zeros((SMALL_ROWS - 9, D), F32)
    w_small = jnp.concatenate([mix_norm_pre, mix_norm_post, ffn_norm_pre, ffn_norm_post, kv_norm.reshape(1, D), zeros7], axis=0)
    m_small = jnp.concatenate([m_mix_norm_pre, m_mix_norm_post, m_ffn_norm_pre, m_ffn_norm_post, m_kv_norm.reshape(1, D), zeros7], axis=0)
    v_small = jnp.concatenate([v_mix_norm_pre, v_mix_norm_post, v_ffn_norm_pre, v_ffn_norm_post, v_kv_norm.reshape(1, D), zeros7], axis=0)
    d_small, nm_small, nv_small = _adamw(w_small, red, m_small, v_small, name="adamw_small")

    pad5 = jnp.zeros((5, tc), F32)
    d_cw, nm_cw, nv_cw = _adamw(cw_pad, jnp.concatenate([g_conv_w, pad5], axis=0),
                                jnp.concatenate([m_conv_w[0], pad5], axis=0),
                                jnp.concatenate([v_conv_w[0], pad5], axis=0), name="adamw_conv_w")

    rs_d, _ = scatter_start([dwci, in_chips(dwco)], "d")

    pieces = {"a": [(0, D), (1, f_sh), (6, 0)], "b": [(4, 0), (5, 0)], "c": [(0, 0), (1, 0)], "d": [(2, 0), (3, 0)]}
    grads2d = [None] * len(big)
    big_out = [None] * len(big)

    def finish(groups, after, tag):
        regions, idxs = [], []
        for gtag, handle in groups:
            landed, parts = _exchange_wait(handle, after, name=f"rs_exchange_wait_{gtag}")
            for i, (l, p, (wi, off)) in enumerate(zip(landed, parts, pieces[gtag])):
                total = as2d(big[wi]).shape[0]
                grads2d[wi] = _chip_sum(l, p, where, total, off, grads2d[wi], name=f"rs_chip_sum_{gtag}{i}")
                if wi not in idxs:
                    idxs.append(wi)
                regions.append((idxs.index(wi), off, 2 * l.shape[1]))
        joined = _sibling_join([grads2d[wi] for wi in idxs], regions, name=f"rs_sibling_join_{tag}")
        for wi, gr in zip(idxs, joined):
            w = big[wi]
            d_, m_, v_, g_ = _adamw(as2d(w), gr, as2d(big_m[wi]), as2d(big_v[wi]), name=f"adamw{wi}",
                                    emit_grad=True)
            big_out[wi] = (g_.reshape(w.shape), d_.reshape(w.shape), m_.reshape(w.shape), v_.reshape(w.shape))

    finish([("a", rs_a), ("b", rs_b), ("c", rs_c)], dx, "abc")
    finish([("d", rs_d)], big_out[0][1], "d")

    def small(a):
        return (a[0:2], a[2:4], a[4:6], a[6:8])

    def assemble(sm, cwv, kind):
        pre, post, fpre, fpost = small(sm)
        b = [t[kind] for t in big_out]
        return [pre, post, fpre, fpost, b[0], b[1], b[2], cwv[0:3].reshape(conv_w.shape), b[3],
                sm[8], b[4], b[5].reshape(w_q.shape), b[6].reshape(w_o.shape)]

    grads = assemble(red, jnp.concatenate([g_conv_w, pad5], axis=0), 0)
    deltas = assemble(d_small, d_cw, 1)
    new_m = assemble(nm_small, nm_cw, 2)
    new_v = assemble(nv_small, nv_cw, 3)
    return (loss, dx.reshape(x.shape), *grads, *deltas, *new_m, *new_v)
```

```python
import functools

import jax
import jax.numpy as jnp
from jax import lax
from jax.experimental import pallas as pl
from jax.experimental.pallas import tpu as pltpu

HEAD_DIM = 64
BAND = 128
BRANCHES = ((128, 1), (512, 4), (2048, 16))
ROPE_THETA = 10000.0
RMS_EPS = 1e-6
NEG_INF = -1e30
ADAM_LR = 0.001
ADAM_B1 = 0.9
ADAM_B2 = 0.999
ADAM_EPS = 1e-08
ADAM_WD = 0.01
ADAM_STEP = 10

N_CHIPS = 4
N_DEV = 8
LANES = 128
ROW_BLOCK = 512
ROW_BLOCK_WIDE = 1024
GRAD_CHUNK = 2048
ATTN_FWD_UNROLL = 16
ATTN_BWD_UNROLL = 8
ATTN_BLOCK_ROWS = 2048
VMEM_LIMIT = 56 * 1024 * 1024
SMALL_ROWS = 16
ADAMW_BLOCK_BYTES = 1024 * 1024
DMA_CHUNK_BYTES = 512 * 1024

BF = jnp.bfloat16
F32 = jnp.float32
MESH = pl.DeviceIdType.MESH
ANY = pl.BlockSpec(memory_space=pl.ANY)


def _cp(*sem):
    return pltpu.CompilerParams(dimension_semantics=sem, vmem_limit_bytes=VMEM_LIMIT)


def _rot_half(t, first):
    return jnp.where(first, pltpu.roll(t, 96, 1), pltpu.roll(t, 32, 1))


def _sigmoid(x):
    return pl.reciprocal(1.0 + jnp.exp(-x), approx=True)


def _first_half_mask(rows):
    lane = lax.broadcasted_iota(jnp.int32, (rows, LANES), 1)
    return (lane % HEAD_DIM) < (HEAD_DIM // 2)


def _normed_rows(j, rows, x_ref, g_ref, xn_ref, xs, last_start, tm):
    @pl.when(j == 0)
    def _():
        xv = x_ref[...]
        r = lax.rsqrt(jnp.mean(xv * xv, axis=-1, keepdims=True) + RMS_EPS)
        xn = (xv * r * g_ref[...]).astype(BF)
        xs[rows, :] = xn
        xn_ref[...] = xn

    @pl.when(j > 0)
    def _():
        xn_ref[...] = xs[pl.ds(last_start, tm), :]


def _norm_matmul(x, gain, wg, cos2, ss2, *, name, rope_shards, scale, out_dtype):
    T, D = x.shape
    n = wg.shape[2]
    tm = min(ROW_BLOCK_WIDE, T)
    ni = T // tm

    def body(x_ref, g_ref, w_ref, cos_ref, ss_ref, y_ref, xn_ref, xs):
        j = pl.program_id(0)
        rows = pl.ds(pl.multiple_of(pl.program_id(1) * tm, tm), tm)
        _normed_rows(j, rows, x_ref, g_ref, xn_ref, xs, (ni - 1) * tm, tm)
        acc = jnp.dot(xs[rows, :], w_ref[...], preferred_element_type=F32)

        def plain():
            y_ref[...] = acc.astype(out_dtype)

        def rope():
            cosv = cos_ref[...]
            ssv = ss_ref[...]
            first = _first_half_mask(tm)
            for ci in range(n // LANES):
                t = acc[:, ci * LANES:(ci + 1) * LANES]
                y = (t * cosv + _rot_half(t, first) * ssv) * scale
                y_ref[:, ci * LANES:(ci + 1) * LANES] = y.astype(out_dtype)

        if rope_shards == 0:
            plain()
        elif rope_shards == N_CHIPS:
            rope()
        else:
            pl.when(j < rope_shards)(rope)
            pl.when(j >= rope_shards)(plain)

    first_pass = lambda j, i: (jnp.where(j == 0, i, ni - 1), 0)
    return pl.pallas_call(
        body, name=name,
        grid=(N_CHIPS, ni),
        in_specs=[
            pl.BlockSpec((tm, D), first_pass),
            pl.BlockSpec((1, D), lambda j, i: (0, 0)),
            pl.BlockSpec((None, D, n), lambda j, i: (j, 0, 0)),
            pl.BlockSpec((tm, LANES), lambda j, i: (i, 0)),
            pl.BlockSpec((tm, LANES), lambda j, i: (i, 0)),
        ],
        out_specs=[
            pl.BlockSpec((tm, n), lambda j, i: (i, j)),
            pl.BlockSpec((tm, D), first_pass),
        ],
        out_shape=[jax.ShapeDtypeStruct((T, N_CHIPS * n), out_dtype),
                   jax.ShapeDtypeStruct((T, D), BF)],
        scratch_shapes=[pltpu.VMEM((T, D), BF)],
        compiler_params=_cp("arbitrary", "arbitrary"),
    )(x, gain, wg, cos2, ss2)


def _norm_swiglu(x, gain, wg, layer, *, name):
    T, D = x.shape
    n = wg.shape[2]
    tm = min(ROW_BLOCK, T)
    ni = T // tm

    def body(x_ref, g_ref, wg_ref, wu_ref, so_ref, uto_ref, ao_ref, xn_ref, xs):
        j = pl.program_id(0)
        rows = pl.ds(pl.multiple_of(pl.program_id(1) * tm, tm), tm)
        _normed_rows(j, rows, x_ref, g_ref, xn_ref, xs, (ni - 1) * tm, tm)
        g = jnp.dot(xs[rows, :], wg_ref[...], preferred_element_type=F32)
        u = jnp.dot(xs[rows, :], wu_ref[...], preferred_element_type=F32)
        sg = _sigmoid(g)
        s = g * sg
        so_ref[...] = s.astype(BF)
        uto_ref[...] = (u * (sg + s * (1.0 - sg))).astype(BF)
        ao_ref[...] = (s * u).astype(BF)

    half = N_CHIPS // 2
    first_pass = lambda j, i: (jnp.where(j == 0, i, ni - 1), 0)
    act = jax.ShapeDtypeStruct((T, half * n), BF)
    return pl.pallas_call(
        body, name=name,
        grid=(half, ni),
        in_specs=[
            pl.BlockSpec((tm, D), first_pass),
            pl.BlockSpec((1, D), lambda j, i: (0, 0)),
            pl.BlockSpec((None, D, n), lambda j, i: (j, layer, 0)),
            pl.BlockSpec((None, D, n), lambda j, i: (j + half, layer, 0)),
        ],
        out_specs=[
            pl.BlockSpec((tm, n), lambda j, i: (i, j)),
            pl.BlockSpec((tm, n), lambda j, i: (i, j)),
            pl.BlockSpec((tm, n), lambda j, i: (i, j)),
            pl.BlockSpec((tm, D), first_pass),
        ],
        out_shape=[act, act, act, jax.ShapeDtypeStruct((T, D), BF)],
        scratch_shapes=[pltpu.VMEM((T, D), BF)],
        compiler_params=_cp("arbitrary", "arbitrary"),
    )(x, gain, wg, wg)


def _matmul_postnorm(a, w3, widx, gain, h_old, *, name):
    T, K = a.shape
    D = w3.shape[2]
    tm = min(ROW_BLOCK_WIDE, T)

    def body(a_ref, w_ref, g_ref, h_ref, y_ref, hn_ref):
        y = jnp.dot(a_ref[...].astype(BF), w_ref[...], preferred_element_type=F32)
        y_ref[...] = y.astype(BF)
        r = lax.rsqrt(jnp.mean(y * y, axis=-1, keepdims=True) + RMS_EPS)
        hn_ref[...] = h_ref[...] + y * r * g_ref[...]

    return pl.pallas_call(
        body, name=name,
        grid=(T // tm,),
        in_specs=[
            pl.BlockSpec((tm, K), lambda i: (i, 0)),
            pl.BlockSpec((None, K, D), lambda i: (widx, 0, 0)),
            pl.BlockSpec((1, D), lambda i: (0, 0)),
            pl.BlockSpec((tm, D), lambda i: (i, 0)),
        ],
        out_specs=[pl.BlockSpec((tm, D), lambda i: (i, 0)),
                   pl.BlockSpec((tm, D), lambda i: (i, 0))],
        out_shape=[jax.ShapeDtypeStruct((T, D), BF), jax.ShapeDtypeStruct((T, D), F32)],
        compiler_params=_cp("parallel"),
    )(a, w3, gain, h_old)


def _matmul_postnorm_loss(a, w3, widx, gain, h_old, target, *, name):
    T, K = a.shape
    D = w3.shape[2]
    tm = min(ROW_BLOCK, T)

    def body(a_ref, w_ref, g_ref, h_ref, t_ref, y_ref, dh_ref, s_ref):
        @pl.when(pl.program_id(0) == 0)
        def _():
            s_ref[...] = jnp.zeros_like(s_ref)

        y = jnp.dot(a_ref[...].astype(BF), w_ref[...], preferred_element_type=F32)
        y_ref[...] = y.astype(BF)
        r = lax.rsqrt(jnp.mean(y * y, axis=-1, keepdims=True) + RMS_EPS)
        e = (h_ref[...] + y * r * g_ref[...]) - t_ref[...]
        dh_ref[...] = e * (1.0 / D)
        s_ref[...] += jnp.sum(e * e)

    rows = pl.BlockSpec((tm, D), lambda i: (i, 0))
    return pl.pallas_call(
        body, name=name,
        grid=(T // tm,),
        in_specs=[
            pl.BlockSpec((tm, K), lambda i: (i, 0)),
            pl.BlockSpec((None, K, D), lambda i: (widx, 0, 0)),
            pl.BlockSpec((1, D), lambda i: (0, 0)),
            rows, rows,
        ],
        out_specs=[rows, rows, pl.BlockSpec((8, LANES), lambda i: (0, 0))],
        out_shape=[jax.ShapeDtypeStruct((T, D), BF), jax.ShapeDtypeStruct((T, D), F32),
                   jax.ShapeDtypeStruct((8, LANES), F32)],
        compiler_params=_cp("arbitrary"),
    )(a, w3, gain, h_old, target)


def _shift_down(u, k):
    row = lax.broadcasted_iota(jnp.int32, u.shape, 0)
    return jnp.where(row >= k, pltpu.roll(u, k, 0), 0.0)


def _shift_up(u, k):
    T = u.shape[0]
    row = lax.broadcasted_iota(jnp.int32, u.shape, 0)
    return jnp.where(row < T - k, pltpu.roll(u, T - k, 0), 0.0)


def _conv_fwd(z, cw, *, name):
    T = z.shape[0]
    D = z.shape[1] // 3
    tc = cw.shape[2]
    nb = D // tc

    def body(b_ref, c_ref, h_ref, w_ref, o_ref):
        u = c_ref[...].astype(F32) * h_ref[...].astype(F32)
        w = w_ref[...]
        conv = w[2:3] * u + w[1:2] * _shift_down(u, 1) + w[0:1] * _shift_down(u, 2)
        o_ref[...] = (b_ref[...].astype(F32) * conv).astype(BF)

    return pl.pallas_call(
        body, name=name,
        grid=(nb,),
        in_specs=[
            pl.BlockSpec((T, tc), lambda j: (0, j)),
            pl.BlockSpec((T, tc), lambda j: (0, nb + j)),
            pl.BlockSpec((T, tc), lambda j: (0, 2 * nb + j)),
            pl.BlockSpec((None, 8, tc), lambda j: (j, 0, 0)),
        ],
        out_specs=pl.BlockSpec((T, tc), lambda j: (0, j)),
        out_shape=jax.ShapeDtypeStruct((T, D), BF),
        compiler_params=_cp("parallel"),
    )(z, z, z, cw)


def _conv_bwd(z, cw, dv, *, name):
    T = z.shape[0]
    D = z.shape[1] // 3
    tc = LANES
    nb = D // tc
    per = cw.shape[2] // tc

    def body(b_ref, c_ref, h_ref, w_ref, dv_ref, dz_ref, dw_ref, stage, sems):
        j = pl.program_id(0)
        slot = j % 2

        def slab(p, jj, s):
            col = pl.multiple_of((p * nb + jj) * tc, tc)
            return pltpu.make_async_copy(stage.at[s, p], dz_ref.at[:, pl.ds(col, tc)], sems.at[s, p])

        @pl.when(j >= 2)
        def _():
            for p in range(3):
                slab(p, j - 2, slot).wait()

        c = c_ref[...].astype(F32)
        h = h_ref[...].astype(F32)
        u = c * h
        u1 = _shift_down(u, 1)
        u2 = _shift_down(u, 2)
        w = w_ref[...]
        dvv = dv_ref[...].astype(F32)
        dconv = dvv * b_ref[...].astype(F32)
        du = w[2:3] * dconv + w[1:2] * _shift_up(dconv, 1) + w[0:1] * _shift_up(dconv, 2)
        rows = lax.broadcasted_iota(jnp.int32, (8, tc), 0)
        dw_ref[...] = jnp.where(rows == 0, jnp.sum(dconv * u2, axis=0, keepdims=True),
                                jnp.where(rows == 1, jnp.sum(dconv * u1, axis=0, keepdims=True),
                                          jnp.where(rows == 2, jnp.sum(dconv * u, axis=0, keepdims=True), 0.0)))
        stage[slot, 0] = (dvv * (w[2:3] * u + w[1:2] * u1 + w[0:1] * u2)).astype(BF)
        stage[slot, 1] = (du * h).astype(BF)
        stage[slot, 2] = (du * c).astype(BF)
        for p in range(3):
            slab(p, j, slot).start()

        @pl.when(j == nb - 1)
        def _():
            for p in range(3):
                slab(p, j, slot).wait()
            if nb > 1:
                for p in range(3):
                    slab(p, j - 1, 1 - slot).wait()

    return pl.pallas_call(
        body, name=name,
        grid=(nb,),
        in_specs=[
            pl.BlockSpec((T, tc), lambda j: (0, j)),
            pl.BlockSpec((T, tc), lambda j: (0, nb + j)),
            pl.BlockSpec((T, tc), lambda j: (0, 2 * nb + j)),
            pl.BlockSpec((None, 8, tc), lambda j: (j // per, 0, j % per)),
            pl.BlockSpec((T, tc), lambda j: (0, j)),
        ],
        out_specs=[ANY, pl.BlockSpec((8, tc), lambda j: (0, j))],
        out_shape=[jax.ShapeDtypeStruct((T, 3 * D), BF), jax.ShapeDtypeStruct((8, D), F32)],
        scratch_shapes=[pltpu.VMEM((2, 3, T, tc), BF), pltpu.SemaphoreType.DMA((2, 3))],
        compiler_params=_cp("arbitrary"),
    )(z, z, z, cw, dv)


def _strided(base, count, d):
    return pl.ds(base, count, stride=d) if d > 1 else pl.ds(pl.multiple_of(base, BAND), count)


def _fill_band_bias(bias):
    qi = lax.broadcasted_iota(jnp.int32, (2 * BAND, 2 * BAND), 0) % BAND
    kj = lax.broadcasted_iota(jnp.int32, (2 * BAND, 2 * BAND), 1)
    dist = qi + BAND - kj
    band = (dist >= 0) & (dist <= BAND)
    bias[0] = jnp.where(band & (kj >= BAND), 0.0, NEG_INF)
    bias[1] = jnp.where(band, 0.0, NEG_INF)


def _attn_block_rows(T):
    return min(ATTN_BLOCK_ROWS, T)


def _head_mask():
    lane = lax.broadcasted_iota(jnp.int32, (2 * BAND, LANES), 1)
    row = lax.broadcasted_iota(jnp.int32, (2 * BAND, LANES), 0)
    return (lane < HEAD_DIM) == (row < BAND)


def _attn_fwd(q_all, kv_all, *, name):
    T = q_all.shape[0]
    NB = len(BRANCHES)
    Dm = q_all.shape[1] // NB
    HP = Dm // LANES
    R = _attn_block_rows(T)
    units = R // BAND
    dmax = max(d for _, d in BRANCHES)

    def body(*refs):
        ins = refs[:5 * NB]
        o_ref, l_ref, qbuf, kbuf, vbuf, o_s, l_s, bias = refs[5 * NB:]
        n = pl.program_id(0)
        pl.when((n == 0) & (pl.program_id(1) == 0))(lambda: _fill_band_bias(bias))
        hm = _head_mask()
        low = lax.broadcasted_iota(jnp.int32, (BAND, LANES), 1) < HEAD_DIM

        for g, (_, d) in enumerate(BRANCHES):
            q_ref, kp_ref, kc_ref, vp_ref, vc_ref = ins[5 * g:5 * g + 5]
            pr = BAND * d
            qbuf[...] = q_ref[...].astype(F32)
            kbuf[0:pr, :] = kp_ref[...].astype(F32)
            kbuf[pr:pr + R, :] = kc_ref[...].astype(F32)
            vbuf[0:pr, :] = vp_ref[...].astype(F32)
            vbuf[pr:pr + R, :] = vc_ref[...].astype(F32)

            def unit(u, carry, g=g, d=d, pr=pr):
                sub = u // d
                base = sub * pr + (u - sub * d)
                q = qbuf[_strided(base, BAND, d), :]
                q2 = jnp.where(hm, jnp.concatenate([q, q], axis=0), 0.0).astype(BF)
                k2 = kbuf[_strided(base, 2 * BAND, d), :].astype(BF)
                v2 = vbuf[_strided(base, 2 * BAND, d), :].astype(BF)
                s = lax.dot_general(q2, k2, (((1,), (1,)), ((), ())), preferred_element_type=F32)
                s = s + bias[((n > 0) | (sub > 0)).astype(jnp.int32)]
                m = jnp.max(s, axis=-1, keepdims=True)
                p = jnp.exp(s - m)
                l = jnp.sum(p, axis=-1, keepdims=True)
                pv = jnp.dot(p.astype(BF), v2, preferred_element_type=F32) * (1.0 / l)
                lse = m + jnp.log(l)
                o_s[g, _strided(base, BAND, d), :] = jnp.where(low, pv[:BAND], pv[BAND:])
                l_s[g, _strided(base, BAND, d), :] = jnp.where(low, lse[:BAND], lse[BAND:])
                return carry

            lax.fori_loop(0, units, unit, 0, unroll=min(ATTN_FWD_UNROLL, units))

        def merge(i, carry):
            sl = pl.ds(pl.multiple_of(i * BAND, BAND), BAND)
            lv = [l_s[g, sl, :] for g in range(NB)]
            m = functools.reduce(jnp.maximum, lv)
            e = [jnp.exp(v - m) for v in lv]
            tot = functools.reduce(jnp.add, e)
            inv = 1.0 / tot
            o_ref[sl, :] = functools.reduce(jnp.add, [(e[g] * inv) * o_s[g, sl, :] for g in range(NB)])
            l_ref[sl, :] = m + jnp.log(tot)
            return carry

        lax.fori_loop(0, units, merge, 0)

    in_specs, args = [], []
    for g, (_, d) in enumerate(BRANCHES):
        per = R // (BAND * d)
        for col, rows, idx in (
                (g * HP, R, lambda n, hp: n),
                (g * HP, BAND * d, lambda n, hp, per=per: jnp.maximum(n * per - 1, 0)),
                (g * HP, R, lambda n, hp: n),
                ((NB + g) * HP, BAND * d, lambda n, hp, per=per: jnp.maximum(n * per - 1, 0)),
                ((NB + g) * HP, R, lambda n, hp: n)):
            in_specs.append(pl.BlockSpec((rows, LANES), lambda n, hp, col=col, idx=idx: (idx(n, hp), col + hp)))
        args += [q_all, kv_all, kv_all, kv_all, kv_all]
    out = pl.BlockSpec((R, LANES), lambda n, hp: (n, hp))
    return pl.pallas_call(
        body, name=name,
        grid=(T // R, HP),
        in_specs=in_specs,
        out_specs=[out, out],
        out_shape=[jax.ShapeDtypeStruct((T, Dm), F32), jax.ShapeDtypeStruct((T, Dm), F32)],
        scratch_shapes=[pltpu.VMEM((R, LANES), F32),
                        pltpu.VMEM((BAND * dmax + R, LANES), F32), pltpu.VMEM((BAND * dmax + R, LANES), F32),
                        pltpu.VMEM((NB, R, LANES), F32), pltpu.VMEM((NB, R, LANES), F32),
                        pltpu.VMEM((2, 2 * BAND, 2 * BAND), F32)],
        compiler_params=_cp("arbitrary", "arbitrary"),
    )(*args)


def _attn_bwd(q_all, kv_all, do, o, lse, cos2, ss2, g, d, prev, *, name, dep=None):
    T = q_all.shape[0]
    NB = len(BRANCHES)
    Dm = q_all.shape[1] // NB
    HP = Dm // LANES
    R = _attn_block_rows(T)
    nblk = T // R
    units = R // BAND
    pr = BAND * d
    per = R // pr
    scale = HEAD_DIM ** -0.5

    def rope_bwd(t, cosv, ssv, first):
        return t * cosv - _rot_half(t, first) * ssv

    def body(q_ref, kp_ref, kc_ref, vp_ref, vc_ref, do_ref, o_ref, l_ref, cos_ref, ss_ref, *rest):
        dq_ref, dk_ref, dv_ref, qbuf, kbuf, vbuf, dq_s, dk_s, dv_s, pend_k, pend_v, bias = rest[-12:]
        i = pl.program_id(1)
        n = nblk - 1 - i
        pl.when((i == 0) & (pl.program_id(0) == 0))(lambda: _fill_band_bias(bias))
        hm = _head_mask()
        low = lax.broadcasted_iota(jnp.int32, (BAND, LANES), 1) < HEAD_DIM

        qbuf[...] = q_ref[...].astype(F32)
        kbuf[0:pr, :] = kp_ref[...].astype(F32)
        kbuf[pr:pr + R, :] = kc_ref[...].astype(F32)
        vbuf[0:pr, :] = vp_ref[...].astype(F32)
        vbuf[pr:pr + R, :] = vc_ref[...].astype(F32)

        @pl.when(i == 0)
        def _():
            pend_k[...] = jnp.zeros_like(pend_k)
            pend_v[...] = jnp.zeros_like(pend_v)

        def unit(u, carry):
            sub = per - 1 - u // d
            cls = u % d
            base = sub * pr + cls
            sl = _strided(base, BAND, d)
            sl2 = _strided(base, 2 * BAND, d)
            q = qbuf[sl, :]
            dov = do_ref[sl, :]
            ov = o_ref[sl, :]
            lv = l_ref[sl, :]
            q2 = jnp.where(hm, jnp.concatenate([q, q], axis=0), 0.0).astype(BF)
            do2 = jnp.where(hm, jnp.concatenate([dov, dov], axis=0), 0.0)
            oo = dov * ov
            delta = jnp.sum(jnp.where(hm, jnp.concatenate([oo, oo], axis=0), 0.0), axis=-1, keepdims=True)
            lse2 = jnp.concatenate([lv[:, 0:1], lv[:, HEAD_DIM:HEAD_DIM + 1]], axis=0)
            do2 = do2.astype(BF)
            k2 = kbuf[sl2, :].astype(BF)
            v2 = vbuf[sl2, :].astype(BF)
            s = lax.dot_general(q2, k2, (((1,), (1,)), ((), ())), preferred_element_type=F32)
            p = jnp.exp(s + bias[((n > 0) | (sub > 0)).astype(jnp.int32)] - lse2)
            dp = lax.dot_general(do2, v2, (((1,), (1,)), ((), ())), preferred_element_type=F32)
            ds = (p * (dp - delta)).astype(BF)
            dq2 = jnp.dot(ds, k2, preferred_element_type=F32)
            dq = jnp.where(low, dq2[:BAND], dq2[BAND:])
            dq_s[sl, :] = dq
            tn = (((0,), (0,)), ((), ()))
            dk2 = lax.dot_general(ds, q2, tn, preferred_element_type=F32)
            dv2 = lax.dot_general(p.astype(BF), do2, tn, preferred_element_type=F32)
            dk_s[sl, :] = dk2[BAND:] + pend_k[cls]
            dv_s[sl, :] = dv2[BAND:] + pend_v[cls]
            pend_k[cls] = dk2[:BAND]
            pend_v[cls] = dv2[:BAND]
            return carry

        lax.fori_loop(0, units, unit, 0, unroll=min(ATTN_BWD_UNROLL, units))

        whole = _first_half_mask(R)
        dq_ref[...] = (rope_bwd(dq_s[...], cos_ref[...], ss_ref[...], whole) * scale).astype(BF)
        dk_ref[...] = rope_bwd(dk_s[...], cos_ref[...], ss_ref[...], whole).astype(BF)
        dv_ref[...] = dv_s[...].astype(BF)

    blk = (R, LANES)
    pblk = (pr, LANES)
    cur = lambda hp, i: nblk - 1 - i
    prv = lambda hp, i: jnp.maximum((nblk - 1 - i) * per - 1, 0)
    in_specs = [
        pl.BlockSpec(blk, lambda hp, i: (cur(hp, i), g * HP + hp)),
        pl.BlockSpec(pblk, lambda hp, i: (prv(hp, i), g * HP + hp)),
        pl.BlockSpec(blk, lambda hp, i: (cur(hp, i), g * HP + hp)),
        pl.BlockSpec(pblk, lambda hp, i: (prv(hp, i), (NB + g) * HP + hp)),
        pl.BlockSpec(blk, lambda hp, i: (cur(hp, i), (NB + g) * HP + hp)),
        pl.BlockSpec(blk, lambda hp, i: (cur(hp, i), hp)),
        pl.BlockSpec(blk, lambda hp, i: (cur(hp, i), hp)),
        pl.BlockSpec(blk, lambda hp, i: (cur(hp, i), hp)),
        pl.BlockSpec(blk, lambda hp, i: (cur(hp, i), 0)),
        pl.BlockSpec(blk, lambda hp, i: (cur(hp, i), 0)),
    ]
    args = [q_all, kv_all, kv_all, kv_all, kv_all, do, o, lse, cos2, ss2]
    if dep is not None:
        in_specs.append(ANY)
        args.append(dep)
    aliases = {}
    if prev is not None:
        in_specs += [ANY, ANY, ANY]
        aliases = {len(args): 0, len(args) + 1: 1, len(args) + 2: 2}
        args += list(prev)
    wide = jax.ShapeDtypeStruct((T, NB * Dm), BF)
    out = pl.BlockSpec(blk, lambda hp, i: (cur(hp, i), g * HP + hp))
    return pl.pallas_call(
        body, name=name,
        grid=(HP, nblk),
        in_specs=in_specs,
        out_specs=[out, out, out],
        out_shape=[wide, wide, wide],
        scratch_shapes=[pltpu.VMEM(blk, F32), pltpu.VMEM((pr + R, LANES), F32), pltpu.VMEM((pr + R, LANES), F32),
                        pltpu.VMEM(blk, F32), pltpu.VMEM(blk, F32), pltpu.VMEM(blk, F32),
                        pltpu.VMEM((d, BAND, LANES), F32), pltpu.VMEM((d, BAND, LANES), F32),
                        pltpu.VMEM((2, 2 * BAND, 2 * BAND), F32)],
        input_output_aliases=aliases,
        compiler_params=_cp("arbitrary", "arbitrary"),
    )(*args)


def _postnorm_bwd(dh, y, g_ref_val):
    r = lax.rsqrt(jnp.mean(y * y, axis=-1, keepdims=True) + RMS_EPS)
    yn = y * r
    dyn = dh * g_ref_val
    dy = r * (dyn - yn * jnp.mean(dyn * yn, axis=-1, keepdims=True))
    return dy, yn


def _after(body, n_in, dep):
    if dep is None:
        return body
    return lambda *refs: body(*refs[:n_in], *refs[n_in + 1:])


def _dep_spec(dep):
    return [] if dep is None else [ANY]


def _dep_arg(dep):
    return [] if dep is None else [dep]


def _postnorm_bwd_matmul(dh, y, gain, w3, widx, *, name, da_dtype, dep=None):
    T, D = dh.shape
    K = w3.shape[1]
    tm = min(ROW_BLOCK_WIDE, T)

    def body(dh_ref, y_ref, g_ref, w_ref, dy_ref, da_ref, dg_ref):
        i = pl.program_id(0)

        @pl.when(i == 0)
        def _():
            dg_ref[...] = jnp.zeros_like(dg_ref)

        dhv = dh_ref[...]
        dy, yn = _postnorm_bwd(dhv, y_ref[...].astype(F32), g_ref[...])
        dg_ref[...] += jnp.sum(dhv * yn, axis=0, keepdims=True)
        dyb = dy.astype(BF)
        dy_ref[...] = dyb
        da = lax.dot_general(dyb, w_ref[...], (((1,), (1,)), ((), ())), preferred_element_type=F32)
        da_ref[...] = da.astype(da_dtype)

    return pl.pallas_call(
        _after(body, 4, dep), name=name,
        grid=(T // tm,),
        in_specs=[
            pl.BlockSpec((tm, D), lambda i: (i, 0)),
            pl.BlockSpec((tm, D), lambda i: (i, 0)),
            pl.BlockSpec((1, D), lambda i: (0, 0)),
            pl.BlockSpec((None, K, D), lambda i: (widx, 0, 0)),
        ] + _dep_spec(dep),
        out_specs=[pl.BlockSpec((tm, D), lambda i: (i, 0)),
                   pl.BlockSpec((tm, K), lambda i: (i, 0)),
                   pl.BlockSpec((1, D), lambda i: (0, 0))],
        out_shape=[jax.ShapeDtypeStruct((T, D), BF), jax.ShapeDtypeStruct((T, K), da_dtype),
                   jax.ShapeDtypeStruct((1, D), F32)],
        compiler_params=_cp("arbitrary"),
    )(dh, y, gain, w3, *_dep_arg(dep))


def _postnorm_bwd_swiglu(dh, y, gain, wd3, layer, s, ut, *, name, dep=None):
    T, D = dh.shape
    F = wd3.shape[1]
    tm = min(ROW_BLOCK, T)

    def body(dh_ref, y_ref, g_ref, w_ref, s_ref, ut_ref, dy_ref, dgo_ref, duo_ref, dgain_ref):
        @pl.when(pl.program_id(0) == 0)
        def _():
            dgain_ref[...] = jnp.zeros_like(dgain_ref)

        dhv = dh_ref[...]
        dy, yn = _postnorm_bwd(dhv, y_ref[...].astype(F32), g_ref[...])
        dgain_ref[...] += jnp.sum(dhv * yn, axis=0, keepdims=True)
        dyb = dy.astype(BF)
        dy_ref[...] = dyb
        da = lax.dot_general(dyb, w_ref[...], (((1,), (1,)), ((), ())), preferred_element_type=F32)
        dgo_ref[...] = (da * ut_ref[...].astype(F32)).astype(BF)
        duo_ref[...] = (da * s_ref[...].astype(F32)).astype(BF)

    rows = pl.BlockSpec((tm, D), lambda i: (i, 0))
    wide = pl.BlockSpec((tm, F), lambda i: (i, 0))
    act = jax.ShapeDtypeStruct((T, F), BF)
    return pl.pallas_call(
        _after(body, 6, dep), name=name,
        grid=(T // tm,),
        in_specs=[
            rows, rows,
            pl.BlockSpec((1, D), lambda i: (0, 0)),
            pl.BlockSpec((None, F, D), lambda i: (layer, 0, 0), pipeline_mode=pl.Buffered(1)),
            wide, wide,
        ] + _dep_spec(dep),
        out_specs=[rows, wide, wide, pl.BlockSpec((1, D), lambda i: (0, 0))],
        out_shape=[jax.ShapeDtypeStruct((T, D), BF), act, act, jax.ShapeDtypeStruct((1, D), F32)],
        compiler_params=_cp("arbitrary"),
    )(dh, y, gain, wd3, s, ut, *_dep_arg(dep))


def _matmul_prenorm_bwd(dzs, wg, layer, h, gain, dh_in, *, name):
    T, D = h.shape
    n = wg.shape[2]
    tm = min(ROW_BLOCK, T)
    per = N_CHIPS // len(dzs)

    def body(*refs):
        dz_refs = refs[:len(dzs)]
        w_ref, h_ref, g_ref, dhi_ref, dh_ref, dg_ref = refs[len(dzs):]

        @pl.when(pl.program_id(0) == 0)
        def _():
            dg_ref[...] = jnp.zeros_like(dg_ref)

        dhn = None
        for j in range(N_CHIPS):
            dz = dz_refs[j // per][:, (j % per) * n:(j % per + 1) * n]
            t = lax.dot_general(dz.astype(BF), w_ref[j], (((1,), (1,)), ((), ())), preferred_element_type=F32)
            dhn = t if dhn is None else dhn + t
        hv = h_ref[...]
        r = lax.rsqrt(jnp.mean(hv * hv, axis=-1, keepdims=True) + RMS_EPS)
        xh = hv * r
        dg_ref[...] += jnp.sum(dhn * xh, axis=0, keepdims=True)
        dxn = dhn * g_ref[...]
        dh_ref[...] = dhi_ref[...] + r * (dxn - xh * jnp.mean(dxn * xh, axis=-1, keepdims=True))

    rows = pl.BlockSpec((tm, D), lambda i: (i, 0))
    in_specs = [pl.BlockSpec((tm, per * n), lambda i: (i, 0)) for _ in dzs]
    in_specs += [pl.BlockSpec((N_CHIPS, D, n), lambda i: (0, layer, 0), pipeline_mode=pl.Buffered(1)),
                 rows, pl.BlockSpec((1, D), lambda i: (0, 0)), rows]
    return pl.pallas_call(
        body, name=name,
        grid=(T // tm,),
        in_specs=in_specs,
        out_specs=[rows, pl.BlockSpec((1, D), lambda i: (0, 0))],
        out_shape=[jax.ShapeDtypeStruct((T, D), F32), jax.ShapeDtypeStruct((1, D), F32)],
        compiler_params=_cp("arbitrary"),
    )(*dzs, wg, h, gain, dh_in)


def _grad_matmul(a, b, out_shape3, tme, tne, out_index, prev, *, name, dep=None):
    T, M = a.shape
    N = b.shape[1]
    tk = min(GRAD_CHUNK, T)
    nk = T // tk

    def body(a_ref, b_ref, *rest):
        o_ref, acc = rest[-2:]
        k = pl.program_id(2)
        part = jnp.dot(a_ref[...].astype(BF).T, b_ref[...].astype(BF), preferred_element_type=F32)

        @pl.when(k == 0)
        def _():
            acc[...] = part

        @pl.when(k > 0)
        def _():
            acc[...] += part

        @pl.when(k == nk - 1)
        def _():
            o_ref[...] = acc[...].astype(BF)

    in_specs = [pl.BlockSpec((tk, tme), lambda i, j, k: (k, i)),
                pl.BlockSpec((tk, tne), lambda i, j, k: (k, j))]
    args = [a, b]
    aliases = {}
    if prev is not None:
        in_specs.append(ANY)
        args.append(prev)
        aliases = {2: 0}
    in_specs += _dep_spec(dep)
    args += _dep_arg(dep)
    return pl.pallas_call(
        body, name=name,
        grid=(M // tme, N // tne, nk),
        in_specs=in_specs,
        out_specs=pl.BlockSpec((None, tme, tne), lambda i, j, k: out_index(i, j)),
        out_shape=jax.ShapeDtypeStruct(out_shape3, BF),
        scratch_shapes=[pltpu.VMEM((tme, tne), F32)],
        input_output_aliases=aliases,
        compiler_params=_cp("parallel", "parallel", "arbitrary"),
    )(*args)


def _row_tile(R, cap=512):
    fit = [t for t in range(16, min(R, cap) + 1, 16) if R % t == 0]
    return max(fit) if fit else R


def _cast_place(w3, layer, where, dtype, *, name, dep=None):
    _, R, C = w3.shape
    tr = _row_tile(R)

    def body(s_ref, w_ref, o_ref):
        o_ref[...] = w_ref[...].astype(o_ref.dtype)

    return pl.pallas_call(
        _after(body, 2, dep), name=name,
        grid_spec=pltpu.PrefetchScalarGridSpec(
            num_scalar_prefetch=1, grid=(R // tr,),
            in_specs=[pl.BlockSpec((None, tr, C), lambda i, s: (layer, i, 0))] + _dep_spec(dep),
            out_specs=pl.BlockSpec((None, tr, C), lambda i, s: (s[0], i, 0))),
        out_shape=jax.ShapeDtypeStruct((N_CHIPS, R, C), dtype),
        compiler_params=_cp("arbitrary"),
    )(where, w3, *_dep_arg(dep))


def _pair_sum(dw, theirs, where, *, name):
    G, rh, C = theirs.shape
    tr = _row_tile(rh)
    nr = rh // tr

    def body(s_ref, a_ref, b_ref, o_ref):
        o_ref[...] = (a_ref[...].astype(F32) + b_ref[...].astype(F32)).astype(BF)

    mine = pl.BlockSpec((None, tr, C), lambda g, i, s: (g, s[1] * nr + i, 0))
    spec = pl.BlockSpec((None, tr, C), lambda g, i, s: (g, i, 0))
    return pl.pallas_call(
        body, name=name,
        grid_spec=pltpu.PrefetchScalarGridSpec(
            num_scalar_prefetch=1, grid=(G, nr), in_specs=[mine, spec], out_specs=spec),
        out_shape=jax.ShapeDtypeStruct((G, rh, C), BF),
        compiler_params=_cp("arbitrary", "arbitrary"),
    )(where, dw, theirs)


def _chip_sum(landed, parts, where, total_rows, row_off, prev, *, name):
    G, rh, C = landed.shape
    tr = _row_tile(rh)
    nr = rh // tr
    base = row_off // tr

    def body(s_ref, l_ref, p_ref, *rest):
        o_ref = rest[-1]
        for j in range(G):
            def own(j=j):
                v = p_ref[...].astype(F32)
                o_ref[...] = v if j == 0 else o_ref[...] + v

            def other(j=j):
                v = l_ref[j].astype(F32)
                o_ref[...] = v if j == 0 else o_ref[...] + v

            pl.when(s_ref[0] == j)(own)
            pl.when(s_ref[0] != j)(other)

    in_specs = [pl.BlockSpec((G, tr, C), lambda i, s: (0, i, 0)),
                pl.BlockSpec((None, tr, C), lambda i, s: (s[0], i, 0))]
    args = [where, landed, parts]
    aliases = {}
    if prev is not None:
        in_specs.append(ANY)
        args.append(prev)
        aliases = {3: 0}
    return pl.pallas_call(
        body, name=name,
        grid_spec=pltpu.PrefetchScalarGridSpec(
            num_scalar_prefetch=1, grid=(nr,),
            in_specs=in_specs,
            out_specs=pl.BlockSpec((tr, C), lambda i, s: (base + s[1] * nr + i, 0))),
        out_shape=jax.ShapeDtypeStruct((total_rows, C), F32),
        input_output_aliases=aliases,
        compiler_params=_cp("arbitrary"),
    )(*args)


def _adamw(w, g, m, v, *, name, emit_grad=False):
    R, C = w.shape
    tr = _row_tile(R, cap=max(16, ADAMW_BLOCK_BYTES // (4 * C)))
    n_out = 4 if emit_grad else 3

    def body(w_ref, g_ref, m_ref, v_ref, d_ref, mo_ref, vo_ref, *go_ref):
        gv = g_ref[...]
        if emit_grad:
            go_ref[0][...] = gv
        mn = ADAM_B1 * m_ref[...] + (1.0 - ADAM_B1) * gv
        vn = ADAM_B2 * v_ref[...] + (1.0 - ADAM_B2) * jnp.square(gv)
        m_hat = mn / (1.0 - ADAM_B1 ** ADAM_STEP)
        v_hat = vn / (1.0 - ADAM_B2 ** ADAM_STEP)
        d_ref[...] = -ADAM_LR * (m_hat / (jnp.sqrt(v_hat) + ADAM_EPS) + ADAM_WD * w_ref[...])
        mo_ref[...] = mn
        vo_ref[...] = vn

    spec = pl.BlockSpec((tr, C), lambda i: (i, 0))
    shp = jax.ShapeDtypeStruct((R, C), F32)
    return pl.pallas_call(
        body, name=name, grid=(R // tr,), in_specs=[spec] * 4, out_specs=[spec] * n_out,
        out_shape=[shp] * n_out, compiler_params=_cp("parallel"),
    )(w, g, m, v)


def _place():
    x = lax.axis_index("x")
    y = lax.axis_index("y")
    c = lax.axis_index("c")
    chips = [(1 - x, y), (x, 1 - y), (1 - x, 1 - y)]
    return x, y, c, chips


def _chunk_rows(rows, row_bytes, align):
    if rows <= align:
        return rows
    cands = [r for r in range(align, rows + 1, align) if rows % r == 0]
    fit = [r for r in cands if r * row_bytes <= DMA_CHUNK_BYTES]
    return max(fit) if fit else min(cands)


def _row_align(dtype):
    return 8 * (4 // jnp.dtype(dtype).itemsize)


def _start_chunks(make, rows, rc):
    for r0 in range(0, rows, rc):
        make(r0, rc).start()


def _piece_rows(ref, piece, j, h, r0=0, n=None):
    _, lead, off, rows = piece
    rh = rows // 2
    n = rh if n is None else n
    if lead is not None:
        return ref.at[lead, j, pl.ds(h * rh + r0, n)]
    return ref.at[j, pl.ds(off + h * rh + r0, n)]


def _piece_chunk(arr, piece):
    rh = piece[3] // 2
    return rh, _chunk_rows(rh, arr.shape[-1] * arr.dtype.itemsize, _row_align(arr.dtype))


def _ag_start(arrays, taps, groups, *, name):
    na = len(arrays)
    ng = len(groups)
    nt = 0 if taps is None else 1
    n_sem = [3 * len(grp) + (3 if nt and g == 0 else 0) for g, grp in enumerate(groups)]

    def body(*refs):
        ins = refs[:na]
        taps_ref = refs[na] if nt else None
        sems = refs[na + nt:na + nt + 2 * ng]
        token = refs[-1]
        token[...] = jnp.zeros_like(token)
        x, y, c, chips = _place()
        myj = 2 * x + y
        for g, grp in enumerate(groups):
            ssem, rsem = sems[2 * g], sems[2 * g + 1]
            for idx, piece in enumerate(grp):
                ref = ins[piece[0]]
                rh, rc = _piece_chunk(arrays[piece[0]], piece)
                for k, (px, py) in enumerate(chips):
                    def send(r0, n, ref=ref, piece=piece, idx=idx, k=k, px=px, py=py, ssem=ssem, rsem=rsem):
                        part = _piece_rows(ref, piece, myj, c, r0, n)
                        return pltpu.make_async_remote_copy(
                            src_ref=part, dst_ref=part, send_sem=ssem.at[3 * idx + k], recv_sem=rsem.at[3 * idx + k],
                            device_id=(px, py, c), device_id_type=MESH)
                    _start_chunks(send, rh, rc)
            if nt and g == 0:
                for k, (px, py) in enumerate(chips):
                    pltpu.make_async_remote_copy(
                        src_ref=taps_ref.at[myj], dst_ref=taps_ref.at[myj],
                        send_sem=ssem.at[3 * len(grp) + k], recv_sem=rsem.at[3 * len(grp) + k],
                        device_id=(px, py, c), device_id_type=MESH).start()

    sem_shapes = []
    for n in n_sem:
        sem_shapes += [pltpu.SemaphoreType.DMA((n,)), pltpu.SemaphoreType.DMA((n,))]
    ops = list(arrays) + ([taps] if nt else [])
    bufs = [pltpu.HBM(a.shape, a.dtype) for a in ops]
    outs = pl.pallas_call(
        body, name=name,
        out_shape=(*sem_shapes, *bufs, jax.ShapeDtypeStruct((8, LANES), F32)),
        in_specs=[HBM] * (na + nt),
        out_specs=(*([SEM] * (2 * ng)), *([HBM] * (na + nt)), pl.BlockSpec(memory_space=pltpu.VMEM)),
        input_output_aliases={i: 2 * ng + i for i in range(na + nt)},
        compiler_params=pltpu.CompilerParams(has_side_effects=EFFECT),
    )(*[_in_hbm(a) for a in ops])
    sems = [(outs[2 * g], outs[2 * g + 1]) for g in range(ng)]
    return sems, list(outs[2 * ng:2 * ng + na]), (outs[2 * ng + na] if nt else None), outs[-1]


def _ag_wait(sems, vals, taps, group, after, *, name):
    nv = len(vals)
    extra = ([taps] if taps is not None else [])
    nb = nv + len(extra)

    def body(*refs):
        bufs = refs[:nb]
        ssem, rsem = refs[nb], refs[nb + 1]
        x, y, c, chips = _place()
        for idx, piece in enumerate(group):
            for k, (px, py) in enumerate(chips):
                got = _piece_rows(bufs[piece[0]], piece, 2 * px + py, c)
                cp = pltpu.make_async_remote_copy(
                    src_ref=got, dst_ref=got, send_sem=ssem.at[3 * idx + k], recv_sem=rsem.at[3 * idx + k],
                    device_id=(px, py, c), device_id_type=MESH)
                cp.wait_send()
                cp.wait_recv()
        if taps is not None:
            for k, (px, py) in enumerate(chips):
                got = bufs[nv].at[2 * px + py]
                cp = pltpu.make_async_remote_copy(
                    src_ref=got, dst_ref=got, send_sem=ssem.at[3 * len(group) + k],
                    recv_sem=rsem.at[3 * len(group) + k], device_id=(px, py, c), device_id_type=MESH)
                cp.wait_send()
                cp.wait_recv()

    ops = list(vals) + extra
    shapes = [pltpu.HBM(a.shape, a.dtype) for a in ops]
    outs = pl.pallas_call(
        body, name=name,
        out_shape=tuple(shapes),
        in_specs=[HBM] * nb + [SEM, SEM] + _dep_spec(after),
        out_specs=[HBM] * nb,
        input_output_aliases={i: i for i in range(nb)},
        compiler_params=pltpu.CompilerParams(has_side_effects=EFFECT),
    )(*ops, sems[0], sems[1], *_dep_arg(after))
    return list(outs[:nv]), (outs[nv] if taps is not None else None)


def _ag_forward(vals, group, *, name):
    nv = len(vals)
    npc = len(group)

    def body(*refs):
        bufs = refs[nv:2 * nv]
        fsem, gsem = refs[2 * nv:]
        x, y, c, chips = _place()
        sib = (x, y, 1 - c)
        sent = []
        for idx, piece in enumerate(group):
            rh, rc = _piece_chunk(vals[piece[0]], piece)
            for k, (px, py) in enumerate(chips):
                def fwd(r0, n, piece=piece, idx=idx, k=k, pj=2 * px + py):
                    part = _piece_rows(bufs[piece[0]], piece, pj, c, r0, n)
                    return pltpu.make_async_remote_copy(
                        src_ref=part, dst_ref=part, send_sem=fsem.at[3 * idx + k], recv_sem=gsem.at[3 * idx + k],
                        device_id=sib, device_id_type=MESH)
                _start_chunks(fwd, rh, rc)
                sent.append(fwd(0, rh))
        for idx, piece in enumerate(group):
            for k, (px, py) in enumerate(chips):
                theirs = _piece_rows(bufs[piece[0]], piece, 2 * px + py, 1 - c)
                pltpu.make_async_remote_copy(
                    src_ref=theirs, dst_ref=theirs, send_sem=fsem.at[3 * idx + k], recv_sem=gsem.at[3 * idx + k],
                    device_id=sib, device_id_type=MESH).wait_recv()
        for cp in sent:
            cp.wait_send()

    return pl.pallas_call(
        body, name=name,
        in_specs=[ANY] * nv, out_specs=[ANY] * nv,
        out_shape=[jax.ShapeDtypeStruct(v.shape, v.dtype) for v in vals],
        input_output_aliases={i: i for i in range(nv)},
        scratch_shapes=[pltpu.SemaphoreType.DMA((3 * npc,)), pltpu.SemaphoreType.DMA((3 * npc,))],
    )(*vals)


def _sibling_swap(dws, *, name):
    nm = len(dws)
    shapes = [jax.ShapeDtypeStruct((dw.shape[0], dw.shape[1] // 2, dw.shape[2]), dw.dtype) for dw in dws]

    def body(*refs):
        ins = refs[:nm]
        theirs = refs[nm:2 * nm]
        ssem, rsem = refs[2 * nm:]
        x, y, c, _ = _place()
        sib = (x, y, 1 - c)
        cps = []
        for m in range(nm):
            G, rh, cols = shapes[m].shape
            rc = _chunk_rows(rh, cols * shapes[m].dtype.itemsize, _row_align(shapes[m].dtype))
            for j in range(G):
                _start_chunks(lambda r0, n, m=m, j=j, rh=rh: pltpu.make_async_remote_copy(
                    src_ref=ins[m].at[j, pl.ds((1 - c) * rh + r0, n)],
                    dst_ref=theirs[m].at[j, pl.ds(r0, n)], send_sem=ssem.at[m], recv_sem=rsem.at[m],
                    device_id=sib, device_id_type=MESH), rh, rc)
            cps.append(pltpu.make_async_remote_copy(
                src_ref=ins[m].at[:, pl.ds((1 - c) * rh, rh), :], dst_ref=theirs[m],
                send_sem=ssem.at[m], recv_sem=rsem.at[m], device_id=sib, device_id_type=MESH))
        for cp in cps:
            cp.wait()

    return pl.pallas_call(
        body, name=name,
        in_specs=[ANY] * nm, out_specs=[ANY] * nm, out_shape=shapes,
        scratch_shapes=[pltpu.SemaphoreType.DMA((nm,)), pltpu.SemaphoreType.DMA((nm,))],
    )(*dws)


HBM = pl.BlockSpec(memory_space=pltpu.HBM)
SEM = pl.BlockSpec(memory_space=pltpu.SEMAPHORE)
EFFECT = pltpu.SideEffectType.DATAFLOW_SIDE_EFFECTING


def _in_hbm(a):
    return pltpu.with_memory_space_constraint(a, pltpu.HBM)


def _swap_start(dws, *, name):
    nm = len(dws)
    lands = [(dw.shape[0], dw.shape[1] // 2, dw.shape[2]) for dw in dws]

    def body(*refs):
        ins = refs[:nm]
        lnd = refs[nm:2 * nm]
        ssem, rsem = refs[2 * nm:2 * nm + 2]
        token = refs[-1]
        x, y, c, _ = _place()
        sib = (x, y, 1 - c)
        for m in range(nm):
            G, rh, cols = lands[m]
            rc = _chunk_rows(rh, cols * dws[m].dtype.itemsize, _row_align(dws[m].dtype))
            for j in range(G):
                _start_chunks(lambda r0, n, m=m, j=j, rh=rh: pltpu.make_async_remote_copy(
                    src_ref=ins[m].at[j, pl.ds((1 - c) * rh + r0, n)], dst_ref=lnd[m].at[j, pl.ds(r0, n)],
                    send_sem=ssem.at[m], recv_sem=rsem.at[m], device_id=sib, device_id_type=MESH), rh, rc)
        token[...] = jnp.zeros_like(token)

    src = [pltpu.HBM(dw.shape, dw.dtype) for dw in dws]
    dst = [pltpu.HBM(s, dw.dtype) for s, dw in zip(lands, dws)]
    outs = pl.pallas_call(
        body, name=name,
        out_shape=(pltpu.SemaphoreType.DMA((nm,)), pltpu.SemaphoreType.DMA((nm,)), *src, *dst,
                   jax.ShapeDtypeStruct((8, LANES), F32)),
        in_specs=[HBM] * (2 * nm),
        out_specs=(SEM, SEM, *([HBM] * (2 * nm)), pl.BlockSpec(memory_space=pltpu.VMEM)),
        input_output_aliases={i: 2 + i for i in range(2 * nm)},
        compiler_params=pltpu.CompilerParams(has_side_effects=EFFECT),
    )(*[_in_hbm(dw) for dw in dws], *[_in_hbm(lax.empty(s, dw.dtype)) for s, dw in zip(lands, dws)])
    return (outs[0], outs[1], list(outs[2:2 + nm]), list(outs[2 + nm:2 + 2 * nm])), outs[-1]


def _swap_wait(handle, after, *, name):
    ssem_in, rsem_in, dws, lands = handle
    nm = len(dws)

    def body(*refs):
        ins = refs[:nm]
        lnd = refs[nm:2 * nm]
        ssem, rsem = refs[2 * nm:2 * nm + 2]
        x, y, c, _ = _place()
        for m in range(nm):
            rh = lands[m].shape[1]
            cp = pltpu.make_async_remote_copy(
                src_ref=ins[m].at[:, pl.ds((1 - c) * rh, rh), :], dst_ref=lnd[m],
                send_sem=ssem.at[m], recv_sem=rsem.at[m], device_id=(x, y, 1 - c), device_id_type=MESH)
            cp.wait_send()
            cp.wait_recv()

    bufs = [pltpu.HBM(a.shape, a.dtype) for a in list(dws) + list(lands)]
    outs = pl.pallas_call(
        body, name=name,
        out_shape=tuple(bufs),
        in_specs=[HBM] * (2 * nm) + [SEM, SEM, ANY],
        out_specs=[HBM] * (2 * nm),
        input_output_aliases={i: i for i in range(2 * nm)},
        compiler_params=pltpu.CompilerParams(has_side_effects=EFFECT),
    )(*dws, *lands, ssem_in, rsem_in, after)
    return list(outs[:nm]), list(outs[nm:])


def _exchange_start(parts, *, name):
    nm = len(parts)

    def body(*refs):
        ins = refs[:nm]
        lands = refs[nm:2 * nm]
        ssem, rsem = refs[2 * nm:2 * nm + 2]
        token = refs[-1]
        x, y, c, chips = _place()
        myj = 2 * x + y
        for m in range(nm):
            _, rh, cols = parts[m].shape
            rc = _chunk_rows(rh, cols * parts[m].dtype.itemsize, _row_align(parts[m].dtype))
            for k, (px, py) in enumerate(chips):
                _start_chunks(lambda r0, n, m=m, k=k, px=px, py=py: pltpu.make_async_remote_copy(
                    src_ref=ins[m].at[2 * px + py, pl.ds(r0, n)], dst_ref=lands[m].at[myj, pl.ds(r0, n)],
                    send_sem=ssem.at[3 * m + k], recv_sem=rsem.at[3 * m + k],
                    device_id=(px, py, c), device_id_type=MESH), rh, rc)
        token[...] = jnp.zeros_like(token)

    bufs = [pltpu.HBM(p.shape, p.dtype) for p in parts]
    outs = pl.pallas_call(
        body, name=name,
        out_shape=(pltpu.SemaphoreType.DMA((3 * nm,)), pltpu.SemaphoreType.DMA((3 * nm,)), *bufs, *bufs,
                   jax.ShapeDtypeStruct((8, LANES), F32)),
        in_specs=[HBM] * (2 * nm),
        out_specs=(SEM, SEM, *([HBM] * (2 * nm)), pl.BlockSpec(memory_space=pltpu.VMEM)),
        input_output_aliases={i: 2 + i for i in range(2 * nm)},
        compiler_params=pltpu.CompilerParams(has_side_effects=EFFECT),
    )(*[_in_hbm(p) for p in parts], *[_in_hbm(lax.empty(p.shape, p.dtype)) for p in parts])
    return (outs[0], outs[1], list(outs[2:2 + nm]), list(outs[2 + nm:2 + 2 * nm])), outs[-1]


def _exchange_wait(handle, after, *, name):
    ssem_in, rsem_in, parts, lands = handle
    nm = len(parts)

    def body(*refs):
        ins = refs[:nm]
        lnd = refs[nm:2 * nm]
        ssem, rsem = refs[2 * nm:2 * nm + 2]
        x, y, c, chips = _place()
        for m in range(nm):
            for k, (px, py) in enumerate(chips):
                pj = 2 * px + py
                cp = pltpu.make_async_remote_copy(
                    src_ref=ins[m].at[pj], dst_ref=lnd[m].at[pj],
                    send_sem=ssem.at[3 * m + k], recv_sem=rsem.at[3 * m + k],
                    device_id=(px, py, c), device_id_type=MESH)
                cp.wait_send()
                cp.wait_recv()

    bufs = [pltpu.HBM(p.shape, p.dtype) for p in parts]
    outs = pl.pallas_call(
        body, name=name,
        out_shape=(*bufs, *bufs),
        in_specs=[HBM] * (2 * nm) + [SEM, SEM, ANY],
        out_specs=[HBM] * (2 * nm),
        input_output_aliases={i: i for i in range(2 * nm)},
        compiler_params=pltpu.CompilerParams(has_side_effects=EFFECT),
    )(*parts, *lands, ssem_in, rsem_in, after)
    return list(outs[nm:]), list(outs[:nm])


def _sibling_join(grads, regions, *, name):
    nm = len(grads)
    nr = len(regions)
    shapes = [jax.ShapeDtypeStruct(g.shape, g.dtype) for g in grads]

    def body(*refs):
        outs = refs[nm:2 * nm]
        ssem, rsem = refs[2 * nm:]
        x, y, c, _ = _place()
        sib = (x, y, 1 - c)
        cps = []
        for i, (m, off, rows) in enumerate(regions):
            rh, cols = rows // 2, grads[m].shape[1]
            rc = _chunk_rows(rh, cols * grads[m].dtype.itemsize, _row_align(grads[m].dtype))

            def send(r0, n, i=i, m=m, off=off, rh=rh):
                part = outs[m].at[pl.ds(off + c * rh + r0, n)]
                return pltpu.make_async_remote_copy(
                    src_ref=part, dst_ref=part, send_sem=ssem.at[i], recv_sem=rsem.at[i],
                    device_id=sib, device_id_type=MESH)
            _start_chunks(send, rh, rc)
            cps.append(send(0, rh))
        for i, (m, off, rows) in enumerate(regions):
            rh = rows // 2
            cps[i].wait_send()
            got = outs[m].at[pl.ds(off + (1 - c) * rh, rh)]
            pltpu.make_async_remote_copy(
                src_ref=got, dst_ref=got, send_sem=ssem.at[i], recv_sem=rsem.at[i],
                device_id=sib, device_id_type=MESH).wait_recv()

    return pl.pallas_call(
        body, name=name,
        in_specs=[ANY] * nm, out_specs=[ANY] * nm, out_shape=shapes,
        input_output_aliases={i: i for i in range(nm)},
        scratch_shapes=[pltpu.SemaphoreType.DMA((nr,)), pltpu.SemaphoreType.DMA((nr,))],
    )(*grads)


def _all_reduce_small(pack):
    R, C = pack.shape

    def body(in_ref, out_ref, slots, ssem, rsem):
        x, y, c, _ = _place()
        me = 4 * x + 2 * y + c
        slots[me] = in_ref[...]
        cps = []
        for k in range(1, N_DEV):
            dx, dy, dc = (k >> 2) & 1, (k >> 1) & 1, k & 1
            peer = (x ^ dx, y ^ dy, c ^ dc)
            cp = pltpu.make_async_remote_copy(
                src_ref=in_ref, dst_ref=slots.at[me], send_sem=ssem.at[k], recv_sem=rsem.at[k],
                device_id=peer, device_id_type=MESH)
            cp.start()
            cps.append(cp)
        for k in range(1, N_DEV):
            dx, dy, dc = (k >> 2) & 1, (k >> 1) & 1, k & 1
            got = slots.at[4 * (x ^ dx) + 2 * (y ^ dy) + (c ^ dc)]
            pltpu.make_async_remote_copy(
                src_ref=got, dst_ref=got, send_sem=ssem.at[k], recv_sem=rsem.at[k],
                device_id=(x ^ dx, y ^ dy, c ^ dc), device_id_type=MESH).wait_recv()
        for cp in cps:
            cp.wait_send()
        acc = slots[0]
        for s in range(1, N_DEV):
            acc = acc + slots[s]
        out_ref[...] = acc

    return pl.pallas_call(
        body, name="ar_small",
        in_specs=[pl.BlockSpec(memory_space=pltpu.VMEM)],
        out_specs=pl.BlockSpec(memory_space=pltpu.VMEM),
        out_shape=jax.ShapeDtypeStruct((R, C), F32),
        scratch_shapes=[pltpu.VMEM((N_DEV, R, C), F32),
                        pltpu.SemaphoreType.DMA((N_DEV,)), pltpu.SemaphoreType.DMA((N_DEV,))],
    )(pack)


def kernel(x, positions, mix_norm_pre, mix_norm_post, ffn_norm_pre, ffn_norm_post, ffn_w_gate_up, ffn_w_down, conv_w_in, conv_w, conv_w_out, kv_norm, w_kv, w_q, w_o, loss_target, m_mix_norm_pre, m_mix_norm_post, m_ffn_norm_pre, m_ffn_norm_post, m_ffn_w_gate_up, m_ffn_w_down, m_conv_w_in, m_conv_w, m_conv_w_out, m_kv_norm, m_w_kv, m_w_q, m_w_o, v_mix_norm_pre, v_mix_norm_post, v_ffn_norm_pre, v_ffn_norm_post, v_ffn_w_gate_up, v_ffn_w_down, v_conv_w_in, v_conv_w, v_conv_w_out, v_kv_norm, v_w_kv, v_w_q, v_w_o):
    T, D = x.shape[1], x.shape[2]
    L = ffn_w_gate_up.shape[0]
    n_gu = ffn_w_gate_up.shape[2]
    f_sh = ffn_w_down.shape[1]
    F = N_CHIPS * f_sh
    x0 = x[0]
    tgt = loss_target[0]

    half = HEAD_DIM // 2
    inv_freq = ROPE_THETA ** (-jnp.arange(half, dtype=F32) / half)
    ang = positions[0].astype(F32)[:, None] * inv_freq
    cosv, sinv = jnp.cos(ang), jnp.sin(ang)
    cos2 = jnp.tile(cosv, (1, LANES // half))
    ss2 = jnp.tile(jnp.concatenate([-sinv, sinv], axis=1), (1, LANES // HEAD_DIM))

    def as2d(a):
        return a.reshape(-1, a.shape[-1])

    big = [ffn_w_gate_up, ffn_w_down, conv_w_in, conv_w_out, w_kv, w_q, w_o]
    big_m = [m_ffn_w_gate_up, m_ffn_w_down, m_conv_w_in, m_conv_w_out, m_w_kv, m_w_q, m_w_o]
    big_v = [v_ffn_w_gate_up, v_ffn_w_down, v_conv_w_in, v_conv_w_out, v_w_kv, v_w_q, v_w_o]
    chip = 2 * lax.axis_index("x") + lax.axis_index("y")
    where = jnp.stack([chip, lax.axis_index("c")]).astype(jnp.int32)
    tc = conv_w.shape[2]
    cw_pad = jnp.concatenate([conv_w[0], jnp.zeros((8 - conv_w.shape[1], tc), F32)], axis=0)

    GU0, GU1, WD0, WD1, WCI, WCO, WKV, WQ, WO = range(9)
    shards = [(ffn_w_gate_up, 0), (ffn_w_gate_up, 1), (ffn_w_down, 0), (ffn_w_down, 1), (conv_w_in, 0),
              (conv_w_out, 0), (w_kv[None], 0), (w_q, 0), (w_o, 0)]
    ag_groups = [
        [(WCI, None, 0, D), (WCO, None, 0, D // N_CHIPS)],
        [(GU0, None, 0, D), (WD0, None, 0, f_sh)],
        [(WKV, None, 0, D), (WQ, None, 0, D)],
        [(WO, None, 0, D // N_CHIPS), (GU1, None, 0, D), (WD1, None, 0, f_sh)],
    ]

    def localised(group, idxs):
        return [(idxs.index(p[0]),) + p[1:] for p in group]

    cur = [None] * len(shards)
    first = [WCI, WCO]
    sems0, vals, taps, token = _ag_start(
        [_cast_place(*shards[i], where, BF, name=f"place{i}") for i in first],
        _cast_place(cw_pad[None], 0, where, F32, name="place_taps"), [localised(ag_groups[0], first)],
        name="ag_start0")
    for i, v in zip(first, vals):
        cur[i] = v
    rest = [i for i in range(len(shards)) if i not in first]
    sems1, vals, _, _ = _ag_start(
        [_cast_place(*shards[i], where, BF, name=f"place{i}", dep=token) for i in rest], None,
        [localised(g, rest) for g in ag_groups[1:]], name="ag_start1")
    for i, v in zip(rest, vals):
        cur[i] = v
    ag_sems = sems0 + sems1

    def gather_group(g, after):
        nonlocal taps
        idxs = sorted({p[0] for p in ag_groups[g]})
        local = localised(ag_groups[g], idxs)
        vals, landed_taps = _ag_wait(ag_sems[g], [cur[i] for i in idxs], taps if g == 0 else None, local, after,
                                     name=f"ag_wait{g}")
        if g == 0:
            taps = landed_taps
        vals = _ag_forward(vals, local, name=f"ag_forward{g}")
        for i, v in zip(idxs, vals):
            cur[i] = v

    def row(a, i):
        return a[i:i + 1]

    gather_group(0, None)
    wci, wco, cw = cur[WCI], cur[WCO].reshape(1, D, D), taps
    z, hn_m0 = _norm_matmul(x0, row(mix_norm_pre, 0), wci, cos2, ss2, name="f0_conv_in",
                            rope_shards=0, scale=1.0, out_dtype=BF)
    vmix = _conv_fwd(z, cw, name="f0_conv")
    y0, h1 = _matmul_postnorm(vmix, wco, 0, row(mix_norm_post, 0), x0, name="f0_conv_out")
    gather_group(1, h1)
    wgu0, wd0 = cur[GU0], cur[WD0].reshape(1, F, D)
    s0, ut0, a0, hn_f0 = _norm_swiglu(h1, row(ffn_norm_pre, 0), wgu0, 0, name="f0_gate_up")
    f0, h2 = _matmul_postnorm(a0, wd0, 0, row(ffn_norm_post, 0), h1, name="f0_down")

    gather_group(2, h2)
    wkv, wq = cur[WKV], cur[WQ]
    kv_all, hn_kv = _norm_matmul(h2, kv_norm.reshape(1, D), wkv, cos2, ss2, name="f1_kv",
                                 rope_shards=N_CHIPS // 2, scale=1.0, out_dtype=BF)
    q_all, hn_m1 = _norm_matmul(h2, row(mix_norm_pre, 1), wq, cos2, ss2, name="f1_q",
                                rope_shards=N_CHIPS, scale=HEAD_DIM ** -0.5, out_dtype=BF)
    o_att, lse = _attn_fwd(q_all, kv_all, name="f1_attn")
    gather_group(3, o_att)
    wgu1, wd1, wo = cur[GU1], cur[WD1].reshape(1, F, D), cur[WO].reshape(1, D, D)
    y1, h3 = _matmul_postnorm(o_att, wo, 0, row(mix_norm_post, 1), h2, name="f1_attn_out")
    s1, ut1, a1, hn_f1 = _norm_swiglu(h3, row(ffn_norm_pre, 1), wgu1, 0, name="f1_gate_up")
    f1, dh4, sq = _matmul_postnorm_loss(a1, wd1, 0, row(ffn_norm_post, 1), h3, tgt, name="f1_down_loss")
    loss_part = 0.5 * sq[0, 0] / D

    gu_shape = (N_CHIPS, D, n_gu)
    in_chips = lambda a: a.reshape(N_CHIPS, -1, a.shape[-1])

    def scatter_start(dws, tag):
        theirs = _sibling_swap(dws, name=f"rs_swap_{tag}")
        parts = [_pair_sum(dw, t, where, name=f"rs_pair_sum_{tag}{i}") for i, (dw, t) in enumerate(zip(dws, theirs))]
        return _exchange_start(parts, name=f"rs_exchange_start_{tag}")

    def scatter_go(swap, after, tag):
        dws, theirs = _swap_wait(swap, after, name=f"rs_swap_wait_{tag}")
        parts = [_pair_sum(dw, t, where, name=f"rs_pair_sum_{tag}{i}") for i, (dw, t) in enumerate(zip(dws, theirs))]
        return _exchange_start(parts, name=f"rs_exchange_start_{tag}")

    dyf1, dg1, du1, d_ffn_post1 = _postnorm_bwd_swiglu(dh4, f1, row(ffn_norm_post, 1), wd1, 0, s1, ut1,
                                                       name="b1_down")
    dwd1 = _grad_matmul(a1, dyf1, (2, F // 2, D), F // 2, D, lambda i, j: (i, 0, 0), None, name="b1_dw_down")
    dwgu1 = _grad_matmul(hn_f1, dg1, gu_shape, D, n_gu, lambda i, j: (j, 0, 0), None, name="b1_dw_gate")
    dwgu1 = _grad_matmul(hn_f1, du1, gu_shape, D, n_gu, lambda i, j: (j + 2, 0, 0), dwgu1, name="b1_dw_up")
    dh3, d_ffn_pre1 = _matmul_prenorm_bwd((dg1, du1), wgu1, 0, h3, row(ffn_norm_pre, 1), dh4, name="b1_gate_up")

    dy1, do, d_mix_post1 = _postnorm_bwd_matmul(dh3, y1, row(mix_norm_post, 1), wo, 0, name="b1_attn_out",
                                                da_dtype=F32)
    dwo = _grad_matmul(o_att, dy1, (1, D, D), D, D, lambda i, j: (0, 0, 0), None, name="b1_dw_o")
    swap_a, token = _swap_start([dwgu1, in_chips(dwd1), in_chips(dwo)], name="rs_swap_start_a")
    prev = None
    for gi, (window, dil) in enumerate(BRANCHES):
        prev = _attn_bwd(q_all, kv_all, do, o_att, lse, cos2, ss2, gi, dil, prev, name=f"b1_attn{gi}", dep=token)
        token = None
        if gi == 0:
            rs_a, token = scatter_go(swap_a, prev[0], "a")
    dq_all, dk_all, dv_all = prev
    n_q = wq.shape[2]
    n_kv = wkv.shape[2]
    dwq = _grad_matmul(hn_m1, dq_all, (N_CHIPS, D, n_q), D, n_q, lambda i, j: (j, 0, 0), None, name="b1_dw_q")
    dwkv = _grad_matmul(hn_kv, dk_all, (N_CHIPS, D, n_kv), D, n_kv, lambda i, j: (j, 0, 0), None, name="b1_dw_k")
    dwkv = _grad_matmul(hn_kv, dv_all, (N_CHIPS, D, n_kv), D, n_kv, lambda i, j: (j + 2, 0, 0), dwkv, name="b1_dw_v")
    dh2, d_mix_pre1 = _matmul_prenorm_bwd((dq_all,), wq, 0, h2, row(mix_norm_pre, 1), dh3, name="b1_q")
    dh2, d_kv_norm = _matmul_prenorm_bwd((dk_all, dv_all), wkv, 0, h2, kv_norm.reshape(1, D), dh2, name="b1_kv")
    swap_b, token = _swap_start([dwkv, dwq], name="rs_swap_start_b")

    dyf0, dg0, du0, d_ffn_post0 = _postnorm_bwd_swiglu(dh2, f0, row(ffn_norm_post, 0), wd0, 0, s0, ut0,
                                                       name="b0_down", dep=token)
    rs_b, token = scatter_go(swap_b, dyf0, "b")
    dwd0 = _grad_matmul(a0, dyf0, (2, F // 2, D), F // 2, D, lambda i, j: (i, 0, 0), None, name="b0_dw_down",
                        dep=token)
    dwgu0 = _grad_matmul(hn_f0, dg0, gu_shape, D, n_gu, lambda i, j: (j, 0, 0), None, name="b0_dw_gate")
    dwgu0 = _grad_matmul(hn_f0, du0, gu_shape, D, n_gu, lambda i, j: (j + 2, 0, 0), dwgu0, name="b0_dw_up")
    dh1, d_ffn_pre0 = _matmul_prenorm_bwd((dg0, du0), wgu0, 0, h1, row(ffn_norm_pre, 0), dh2, name="b0_gate_up")
    swap_c, token = _swap_start([dwgu0, in_chips(dwd0)], name="rs_swap_start_c")

    dy0, dvmix, d_mix_post0 = _postnorm_bwd_matmul(dh1, y0, row(mix_norm_post, 0), wco, 0, name="b0_conv_out",
                                                   da_dtype=BF, dep=token)
    rs_c, token = scatter_go(swap_c, dy0, "c")
    dwco = _grad_matmul(vmix, dy0, (1, D, D), D, D, lambda i, j: (0, 0, 0), None, name="b0_dw_conv_out",
                        dep=token)
    dz, dcw = _conv_bwd(z, cw, dvmix, name="b0_conv")
    n_ci = wci.shape[2]
    dwci = _grad_matmul(hn_m0, dz, (N_CHIPS, D, n_ci), D, n_ci, lambda i, j: (j, 0, 0), None, name="b0_dw_conv_in")
    dx, d_mix_pre0 = _matmul_prenorm_bwd((dz,), wci, 0, x0, row(mix_norm_pre, 0), dh1, name="b0_conv_in")

    pack = jnp.concatenate([
        d_mix_pre0, d_mix_pre1, d_mix_post0, d_mix_post1, d_ffn_pre0, d_ffn_pre1, d_ffn_post0, d_ffn_post1,
        d_kv_norm, dcw[0:3], jnp.full((1, D), loss_part, F32),
        jnp.zeros((SMALL_ROWS - 13, D), F32)], axis=0)
    red = _all_reduce_small(pack)
    loss = red[12, 0]
    myj = 2 * lax.axis_index("x") + lax.axis_index("y")
    g_conv_w = lax.dynamic_slice(red, (9, myj * tc), (3, tc))

    zeros7 = jnp.zeros((SMALL_ROWS - 9, D), F32)
    w_small = jnp.concatenate([mix_norm_pre, mix_norm_post, ffn_norm_pre, ffn_norm_post, kv_norm.reshape(1, D), zeros7], axis=0)
    m_small = jnp.concatenate([m_mix_norm_pre, m_mix_norm_post, m_ffn_norm_pre, m_ffn_norm_post, m_kv_norm.reshape(1, D), zeros7], axis=0)
    v_small = jnp.concatenate([v_mix_norm_pre, v_mix_norm_post, v_ffn_norm_pre, v_ffn_norm_post, v_kv_norm.reshape(1, D), zeros7], axis=0)
    d_small, nm_small, nv_small = _adamw(w_small, red, m_small, v_small, name="adamw_small")

    pad5 = jnp.zeros((5, tc), F32)
    d_cw, nm_cw, nv_cw = _adamw(cw_pad, jnp.concatenate([g_conv_w, pad5], axis=0),
                                jnp.concatenate([m_conv_w[0], pad5], axis=0),
                                jnp.concatenate([v_conv_w[0], pad5], axis=0), name="adamw_conv_w")

    rs_d, _ = scatter_start([dwci, in_chips(dwco)], "d")

    pieces = {"a": [(0, D), (1, f_sh), (6, 0)], "b": [(4, 0), (5, 0)], "c": [(0, 0), (1, 0)], "d": [(2, 0), (3, 0)]}
    grads2d = [None] * len(big)
    big_out = [None] * len(big)

    def finish(groups, after, tag):
        regions, idxs = [], []
        for gtag, handle in groups:
            landed, parts = _exchange_wait(handle, after, name=f"rs_exchange_wait_{gtag}")
            for i, (l, p, (wi, off)) in enumerate(zip(landed, parts, pieces[gtag])):
                total = as2d(big[wi]).shape[0]
                grads2d[wi] = _chip_sum(l, p, where, total, off, grads2d[wi], name=f"rs_chip_sum_{gtag}{i}")
                if wi not in idxs:
                    idxs.append(wi)
                regions.append((idxs.index(wi), off, 2 * l.shape[1]))
        joined = _sibling_join([grads2d[wi] for wi in idxs], regions, name=f"rs_sibling_join_{tag}")
        for wi, gr in zip(idxs, joined):
            w = big[wi]
            d_, m_, v_, g_ = _adamw(as2d(w), gr, as2d(big_m[wi]), as2d(big_v[wi]), name=f"adamw{wi}",
                                    emit_grad=True)
            big_out[wi] = (g_.reshape(w.shape), d_.reshape(w.shape), m_.reshape(w.shape), v_.reshape(w.shape))

    finish([("a", rs_a), ("b", rs_b), ("c", rs_c)], dx, "abc")
    finish([("d", rs_d)], big_out[0][1], "d")

    def small(a):
        return (a[0:2], a[2:4], a[4:6], a[6:8])

    def assemble(sm, cwv, kind):
        pre, post, fpre, fpost = small(sm)
        b = [t[kind] for t in big_out]
        return [pre, post, fpre, fpost, b[0], b[1], b[2], cwv[0:3].reshape(conv_w.shape), b[3],
                sm[8], b[4], b[5].reshape(w_q.shape), b[6].reshape(w_o.shape)]

    grads = assemble(red, jnp.concatenate([g_conv_w, pad5], axis=0), 0)
    deltas = assemble(d_small, d_cw, 1)
    new_m = assemble(nm_small, nm_cw, 2)
    new_v = assemble(nv_small, nv_cw, 3)
    return (loss, dx.reshape(x.shape), *grads, *deltas, *new_m, *new_v)
```

```python
import functools

import jax
import jax.numpy as jnp
from jax import lax
from jax.experimental import pallas as pl
from jax.experimental.pallas import tpu as pltpu

HEAD_DIM = 64
BAND = 128
BRANCHES = ((128, 1), (512, 4), (2048, 16))
ROPE_THETA = 10000.0
RMS_EPS = 1e-6
NEG_INF = -1e30
ADAM_LR = 0.001
ADAM_B1 = 0.9
ADAM_B2 = 0.999
ADAM_EPS = 1e-08
ADAM_WD = 0.01
ADAM_STEP = 10

N_CHIPS = 4
N_DEV = 8
LANES = 128
ROW_BLOCK = 512
ROW_BLOCK_WIDE = 1024
GRAD_CHUNK = 2048
ATTN_FWD_UNROLL = 16
ATTN_BWD_UNROLL = 8
ATTN_BLOCK_ROWS = 2048
VMEM_LIMIT = 56 * 1024 * 1024
SMALL_ROWS = 16
ADAMW_BLOCK_BYTES = 1024 * 1024
DMA_CHUNK_BYTES = 512 * 1024

BF = jnp.bfloat16
F32 = jnp.float32
MESH = pl.DeviceIdType.MESH
ANY = pl.BlockSpec(memory_space=pl.ANY)


def _cp(*sem):
    return pltpu.CompilerParams(dimension_semantics=sem, vmem_limit_bytes=VMEM_LIMIT)


def _rot_half(t, first):
    return jnp.where(first, pltpu.roll(t, 96, 1), pltpu.roll(t, 32, 1))


def _sigmoid(x):
    return pl.reciprocal(1.0 + jnp.exp(-x), approx=True)


def _first_half_mask(rows):
    lane = lax.broadcasted_iota(jnp.int32, (rows, LANES), 1)
    return (lane % HEAD_DIM) < (HEAD_DIM // 2)


def _normed_rows(j, rows, x_ref, g_ref, xn_ref, xs, last_start, tm):
    @pl.when(j == 0)
    def _():
        xv = x_ref[...]
        r = lax.rsqrt(jnp.mean(xv * xv, axis=-1, keepdims=True) + RMS_EPS)
        xn = (xv * r * g_ref[...]).astype(BF)
        xs[rows, :] = xn
        xn_ref[...] = xn

    @pl.when(j > 0)
    def _():
        xn_ref[...] = xs[pl.ds(last_start, tm), :]


def _norm_matmul(x, gain, wg, cos2, ss2, *, name, rope_shards, scale, out_dtype):
    T, D = x.shape
    n = wg.shape[2]
    tm = min(ROW_BLOCK_WIDE, T)
    ni = T // tm

    def body(x_ref, g_ref, w_ref, cos_ref, ss_ref, y_ref, xn_ref, xs):
        j = pl.program_id(0)
        rows = pl.ds(pl.multiple_of(pl.program_id(1) * tm, tm), tm)
        _normed_rows(j, rows, x_ref, g_ref, xn_ref, xs, (ni - 1) * tm, tm)
        acc = jnp.dot(xs[rows, :], w_ref[...], preferred_element_type=F32)

        def plain():
            y_ref[...] = acc.astype(out_dtype)

        def rope():
            cosv = cos_ref[...]
            ssv = ss_ref[...]
            first = _first_half_mask(tm)
            for ci in range(n // LANES):
                t = acc[:, ci * LANES:(ci + 1) * LANES]
                y = (t * cosv + _rot_half(t, first) * ssv) * scale
                y_ref[:, ci * LANES:(ci + 1) * LANES] = y.astype(out_dtype)

        if rope_shards == 0:
            plain()
        elif rope_shards == N_CHIPS:
            rope()
        else:
            pl.when(j < rope_shards)(rope)
            pl.when(j >= rope_shards)(plain)

    first_pass = lambda j, i: (jnp.where(j == 0, i, ni - 1), 0)
    return pl.pallas_call(
        body, name=name,
        grid=(N_CHIPS, ni),
        in_specs=[
            pl.BlockSpec((tm, D), first_pass),
            pl.BlockSpec((1, D), lambda j, i: (0, 0)),
            pl.BlockSpec((None, D, n), lambda j, i: (j, 0, 0)),
            pl.BlockSpec((tm, LANES), lambda j, i: (i, 0)),
            pl.BlockSpec((tm, LANES), lambda j, i: (i, 0)),
        ],
        out_specs=[
            pl.BlockSpec((tm, n), lambda j, i: (i, j)),
            pl.BlockSpec((tm, D), first_pass),
        ],
        out_shape=[jax.ShapeDtypeStruct((T, N_CHIPS * n), out_dtype),
                   jax.ShapeDtypeStruct((T, D), BF)],
        scratch_shapes=[pltpu.VMEM((T, D), BF)],
        compiler_params=_cp("arbitrary", "arbitrary"),
    )(x, gain, wg, cos2, ss2)


def _norm_swiglu(x, gain, wg, layer, *, name):
    T, D = x.shape
    n = wg.shape[2]
    tm = min(ROW_BLOCK, T)
    ni = T // tm

    def body(x_ref, g_ref, wg_ref, wu_ref, so_ref, uto_ref, ao_ref, xn_ref, xs):
        j = pl.program_id(0)
        rows = pl.ds(pl.multiple_of(pl.program_id(1) * tm, tm), tm)
        _normed_rows(j, rows, x_ref, g_ref, xn_ref, xs, (ni - 1) * tm, tm)
        g = jnp.dot(xs[rows, :], wg_ref[...], preferred_element_type=F32)
        u = jnp.dot(xs[rows, :], wu_ref[...], preferred_element_type=F32)
        sg = _sigmoid(g)
        s = g * sg
        so_ref[...] = s.astype(BF)
        uto_ref[...] = (u * (sg + s * (1.0 - sg))).astype(BF)
        ao_ref[...] = (s * u).astype(BF)

    half = N_CHIPS // 2
    first_pass = lambda j, i: (jnp.where(j == 0, i, ni - 1), 0)
    act = jax.ShapeDtypeStruct((T, half * n), BF)
    return pl.pallas_call(
        body, name=name,
        grid=(half, ni),
        in_specs=[
            pl.BlockSpec((tm, D), first_pass),
            pl.BlockSpec((1, D), lambda j, i: (0, 0)),
            pl.BlockSpec((None, D, n), lambda j, i: (j, layer, 0)),
            pl.BlockSpec((None, D, n), lambda j, i: (j + half, layer, 0)),
        ],
        out_specs=[
            pl.BlockSpec((tm, n), lambda j, i: (i, j)),
            pl.BlockSpec((tm, n), lambda j, i: (i, j)),
            pl.BlockSpec((tm, n), lambda j, i: (i, j)),
            pl.BlockSpec((tm, D), first_pass),
        ],
        out_shape=[act, act, act, jax.ShapeDtypeStruct((T, D), BF)],
        scratch_shapes=[pltpu.VMEM((T, D), BF)],
        compiler_params=_cp("arbitrary", "arbitrary"),
    )(x, gain, wg, wg)


def _matmul_postnorm(a, w3, widx, gain, h_old, *, name):
    T, K = a.shape
    D = w3.shape[2]
    tm = min(ROW_BLOCK_WIDE, T)

    def body(a_ref, w_ref, g_ref, h_ref, y_ref, hn_ref):
        y = jnp.dot(a_ref[...].astype(BF), w_ref[...], preferred_element_type=F32)
        y_ref[...] = y.astype(BF)
        r = lax.rsqrt(jnp.mean(y * y, axis=-1, keepdims=True) + RMS_EPS)
        hn_ref[...] = h_ref[...] + y * r * g_ref[...]

    return pl.pallas_call(
        body, name=name,
        grid=(T // tm,),
        in_specs=[
            pl.BlockSpec((tm, K), lambda i: (i, 0)),
            pl.BlockSpec((None, K, D), lambda i: (widx, 0, 0)),
            pl.BlockSpec((1, D), lambda i: (0, 0)),
            pl.BlockSpec((tm, D), lambda i: (i, 0)),
        ],
        out_specs=[pl.BlockSpec((tm, D), lambda i: (i, 0)),
                   pl.BlockSpec((tm, D), lambda i: (i, 0))],
        out_shape=[jax.ShapeDtypeStruct((T, D), BF), jax.ShapeDtypeStruct((T, D), F32)],
        compiler_params=_cp("parallel"),
    )(a, w3, gain, h_old)


def _matmul_postnorm_loss(a, w3, widx, gain, h_old, target, *, name):
    T, K = a.shape
    D = w3.shape[2]
    tm = min(ROW_BLOCK, T)

    def body(a_ref, w_ref, g_ref, h_ref, t_ref, y_ref, dh_ref, s_ref):
        @pl.when(pl.program_id(0) == 0)
        def _():
            s_ref[...] = jnp.zeros_like(s_ref)

        y = jnp.dot(a_ref[...].astype(BF), w_ref[...], preferred_element_type=F32)
        y_ref[...] = y.astype(BF)
        r = lax.rsqrt(jnp.mean(y * y, axis=-1, keepdims=True) + RMS_EPS)
        e = (h_ref[...] + y * r * g_ref[...]) - t_ref[...]
        dh_ref[...] = e * (1.0 / D)
        s_ref[...] += jnp.sum(e * e)

    rows = pl.BlockSpec((tm, D), lambda i: (i, 0))
    return pl.pallas_call(
        body, name=name,
        grid=(T // tm,),
        in_specs=[
            pl.BlockSpec((tm, K), lambda i: (i, 0)),
            pl.BlockSpec((None, K, D), lambda i: (widx, 0, 0)),
            pl.BlockSpec((1, D), lambda i: (0, 0)),
            rows, rows,
        ],
        out_specs=[rows, rows, pl.BlockSpec((8, LANES), lambda i: (0, 0))],
        out_shape=[jax.ShapeDtypeStruct((T, D), BF), jax.ShapeDtypeStruct((T, D), F32),
                   jax.ShapeDtypeStruct((8, LANES), F32)],
        compiler_params=_cp("arbitrary"),
    )(a, w3, gain, h_old, target)


def _shift_down(u, k):
    row = lax.broadcasted_iota(jnp.int32, u.shape, 0)
    return jnp.where(row >= k, pltpu.roll(u, k, 0), 0.0)


def _shift_up(u, k):
    T = u.shape[0]
    row = lax.broadcasted_iota(jnp.int32, u.shape, 0)
    return jnp.where(row < T - k, pltpu.roll(u, T - k, 0), 0.0)


def _conv_fwd(z, cw, *, name):
    T = z.shape[0]
    D = z.shape[1] // 3
    tc = cw.shape[2]
    nb = D // tc

    def body(b_ref, c_ref, h_ref, w_ref, o_ref):
        u = c_ref[...].astype(F32) * h_ref[...].astype(F32)
        w = w_ref[...]
        conv = w[2:3] * u + w[1:2] * _shift_down(u, 1) + w[0:1] * _shift_down(u, 2)
        o_ref[...] = (b_ref[...].astype(F32) * conv).astype(BF)

    return pl.pallas_call(
        body, name=name,
        grid=(nb,),
        in_specs=[
            pl.BlockSpec((T, tc), lambda j: (0, j)),
            pl.BlockSpec((T, tc), lambda j: (0, nb + j)),
            pl.BlockSpec((T, tc), lambda j: (0, 2 * nb + j)),
            pl.BlockSpec((None, 8, tc), lambda j: (j, 0, 0)),
        ],
        out_specs=pl.BlockSpec((T, tc), lambda j: (0, j)),
        out_shape=jax.ShapeDtypeStruct((T, D), BF),
        compiler_params=_cp("parallel"),
    )(z, z, z, cw)


def _conv_bwd(z, cw, dv, *, name):
    T = z.shape[0]
    D = z.shape[1] // 3
    tc = LANES
    nb = D // tc
    per = cw.shape[2] // tc

    def body(b_ref, c_ref, h_ref, w_ref, dv_ref, dz_ref, dw_ref, stage, sems):
        j = pl.program_id(0)
        slot = j % 2

        def slab(p, jj, s):
            col = pl.multiple_of((p * nb + jj) * tc, tc)
            return pltpu.make_async_copy(stage.at[s, p], dz_ref.at[:, pl.ds(col, tc)], sems.at[s, p])

        @pl.when(j >= 2)
        def _():
            for p in range(3):
                slab(p, j - 2, slot).wait()

        c = c_ref[...].astype(F32)
        h = h_ref[...].astype(F32)
        u = c * h
        u1 = _shift_down(u, 1)
        u2 = _shift_down(u, 2)
        w = w_ref[...]
        dvv = dv_ref[...].astype(F32)
        dconv = dvv * b_ref[...].astype(F32)
        du = w[2:3] * dconv + w[1:2] * _shift_up(dconv, 1) + w[0:1] * _shift_up(dconv, 2)
        rows = lax.broadcasted_iota(jnp.int32, (8, tc), 0)
        dw_ref[...] = jnp.where(rows == 0, jnp.sum(dconv * u2, axis=0, keepdims=True),
                                jnp.where(rows == 1, jnp.sum(dconv * u1, axis=0, keepdims=True),
                                          jnp.where(rows == 2, jnp.sum(dconv * u, axis=0, keepdims=True), 0.0)))
        stage[slot, 0] = (dvv * (w[2:3] * u + w[1:2] * u1 + w[0:1] * u2)).astype(BF)
        stage[slot, 1] = (du * h).astype(BF)
        stage[slot, 2] = (du * c).astype(BF)
        for p in range(3):
            slab(p, j, slot).start()

        @pl.when(j == nb - 1)
        def _():
            for p in range(3):
                slab(p, j, slot).wait()
            if nb > 1:
                for p in range(3):
                    slab(p, j - 1, 1 - slot).wait()

    return pl.pallas_call(
        body, name=name,
        grid=(nb,),
        in_specs=[
            pl.BlockSpec((T, tc), lambda j: (0, j)),
            pl.BlockSpec((T, tc), lambda j: (0, nb + j)),
            pl.BlockSpec((T, tc), lambda j: (0, 2 * nb + j)),
            pl.BlockSpec((None, 8, tc), lambda j: (j // per, 0, j % per)),
            pl.BlockSpec((T, tc), lambda j: (0, j)),
        ],
        out_specs=[ANY, pl.BlockSpec((8, tc), lambda j: (0, j))],
        out_shape=[jax.ShapeDtypeStruct((T, 3 * D), BF), jax.ShapeDtypeStruct((8, D), F32)],
        scratch_shapes=[pltpu.VMEM((2, 3, T, tc), BF), pltpu.SemaphoreType.DMA((2, 3))],
        compiler_params=_cp("arbitrary"),
    )(z, z, z, cw, dv)


def _strided(base, count, d):
    return pl.ds(base, count, stride=d) if d > 1 else pl.ds(pl.multiple_of(base, BAND), count)


def _fill_band_bias(bias):
    qi = lax.broadcasted_iota(jnp.int32, (2 * BAND, 2 * BAND), 0) % BAND
    kj = lax.broadcasted_iota(jnp.int32, (2 * BAND, 2 * BAND), 1)
    dist = qi + BAND - kj
    band = (dist >= 0) & (dist <= BAND)
    bias[0] = jnp.where(band & (kj >= BAND), 0.0, NEG_INF)
    bias[1] = jnp.where(band, 0.0, NEG_INF)


def _attn_block_rows(T):
    return min(ATTN_BLOCK_ROWS, T)


def _head_mask():
    lane = lax.broadcasted_iota(jnp.int32, (2 * BAND, LANES), 1)
    row = lax.broadcasted_iota(jnp.int32, (2 * BAND, LANES), 0)
    return (lane < HEAD_DIM) == (row < BAND)


def _attn_fwd(q_all, kv_all, *, name):
    T = q_all.shape[0]
    NB = len(BRANCHES)
    Dm = q_all.shape[1] // NB
    HP = Dm // LANES
    R = _attn_block_rows(T)
    units = R // BAND
    dmax = max(d for _, d in BRANCHES)

    def body(*refs):
        ins = refs[:5 * NB]
        o_ref, l_ref, qbuf, kbuf, vbuf, o_s, l_s, bias = refs[5 * NB:]
        n = pl.program_id(0)
        pl.when((n == 0) & (pl.program_id(1) == 0))(lambda: _fill_band_bias(bias))
        hm = _head_mask()
        low = lax.broadcasted_iota(jnp.int32, (BAND, LANES), 1) < HEAD_DIM

        for g, (_, d) in enumerate(BRANCHES):
            q_ref, kp_ref, kc_ref, vp_ref, vc_ref = ins[5 * g:5 * g + 5]
            pr = BAND * d
            qbuf[...] = q_ref[...].astype(F32)
            kbuf[0:pr, :] = kp_ref[...].astype(F32)
            kbuf[pr:pr + R, :] = kc_ref[...].astype(F32)
            vbuf[0:pr, :] = vp_ref[...].astype(F32)
            vbuf[pr:pr + R, :] = vc_ref[...].astype(F32)

            def unit(u, carry, g=g, d=d, pr=pr):
                sub = u // d
                base = sub * pr + (u - sub * d)
                q = qbuf[_strided(base, BAND, d), :]
                q2 = jnp.where(hm, jnp.concatenate([q, q], axis=0), 0.0).astype(BF)
                k2 = kbuf[_strided(base, 2 * BAND, d), :].astype(BF)
                v2 = vbuf[_strided(base, 2 * BAND, d), :].astype(BF)
                s = lax.dot_general(q2, k2, (((1,), (1,)), ((), ())), preferred_element_type=F32)
                s = s + bias[((n > 0) | (sub > 0)).astype(jnp.int32)]
                m = jnp.max(s, axis=-1, keepdims=True)
                p = jnp.exp(s - m)
                l = jnp.sum(p, axis=-1, keepdims=True)
                pv = jnp.dot(p.astype(BF), v2, preferred_element_type=F32) * (1.0 / l)
                lse = m + jnp.log(l)
                o_s[g, _strided(base, BAND, d), :] = jnp.where(low, pv[:BAND], pv[BAND:])
                l_s[g, _strided(base, BAND, d), :] = jnp.where(low, lse[:BAND], lse[BAND:])
                return carry

            lax.fori_loop(0, units, unit, 0, unroll=min(ATTN_FWD_UNROLL, units))

        def merge(i, carry):
            sl = pl.ds(pl.multiple_of(i * BAND, BAND), BAND)
            lv = [l_s[g, sl, :] for g in range(NB)]
            m = functools.reduce(jnp.maximum, lv)
            e = [jnp.exp(v - m) for v in lv]
            tot = functools.reduce(jnp.add, e)
            inv = 1.0 / tot
            o_ref[sl, :] = functools.reduce(jnp.add, [(e[g] * inv) * o_s[g, sl, :] for g in range(NB)])
            l_ref[sl, :] = m + jnp.log(tot)
            return carry

        lax.fori_loop(0, units, merge, 0)

    in_specs, args = [], []
    for g, (_, d) in enumerate(BRANCHES):
        per = R // (BAND * d)
        for col, rows, idx in (
                (g * HP, R, lambda n, hp: n),
                (g * HP, BAND * d, lambda n, hp, per=per: jnp.maximum(n * per - 1, 0)),
                (g * HP, R, lambda n, hp: n),
                ((NB + g) * HP, BAND * d, lambda n, hp, per=per: jnp.maximum(n * per - 1, 0)),
                ((NB + g) * HP, R, lambda n, hp: n)):
            in_specs.append(pl.BlockSpec((rows, LANES), lambda n, hp, col=col, idx=idx: (idx(n, hp), col + hp)))
        args += [q_all, kv_all, kv_all, kv_all, kv_all]
    out = pl.BlockSpec((R, LANES), lambda n, hp: (n, hp))
    return pl.pallas_call(
        body, name=name,
        grid=(T // R, HP),
        in_specs=in_specs,
        out_specs=[out, out],
        out_shape=[jax.ShapeDtypeStruct((T, Dm), F32), jax.ShapeDtypeStruct((T, Dm), F32)],
        scratch_shapes=[pltpu.VMEM((R, LANES), F32),
                        pltpu.VMEM((BAND * dmax + R, LANES), F32), pltpu.VMEM((BAND * dmax + R, LANES), F32),
                        pltpu.VMEM((NB, R, LANES), F32), pltpu.VMEM((NB, R, LANES), F32),
                        pltpu.VMEM((2, 2 * BAND, 2 * BAND), F32)],
        compiler_params=_cp("arbitrary", "arbitrary"),
    )(*args)


def _attn_bwd(q_all, kv_all, do, o, lse, cos2, ss2, g, d, prev, *, name, dep=None):
    T = q_all.shape[0]
    NB = len(BRANCHES)
    Dm = q_all.shape[1] // NB
    HP = Dm // LANES
    R = _attn_block_rows(T)
    nblk = T // R
    units = R // BAND
    pr = BAND * d
    per = R // pr
    scale = HEAD_DIM ** -0.5

    def rope_bwd(t, cosv, ssv, first):
        return t * cosv - _rot_half(t, first) * ssv

    def body(q_ref, kp_ref, kc_ref, vp_ref, vc_ref, do_ref, o_ref, l_ref, cos_ref, ss_ref, *rest):
        dq_ref, dk_ref, dv_ref, qbuf, kbuf, vbuf, dq_s, dk_s, dv_s, pend_k, pend_v, bias = rest[-12:]
        i = pl.program_id(1)
        n = nblk - 1 - i
        pl.when((i == 0) & (pl.program_id(0) == 0))(lambda: _fill_band_bias(bias))
        hm = _head_mask()
        low = lax.broadcasted_iota(jnp.int32, (BAND, LANES), 1) < HEAD_DIM

        qbuf[...] = q_ref[...].astype(F32)
        kbuf[0:pr, :] = kp_ref[...].astype(F32)
        kbuf[pr:pr + R, :] = kc_ref[...].astype(F32)
        vbuf[0:pr, :] = vp_ref[...].astype(F32)
        vbuf[pr:pr + R, :] = vc_ref[...].astype(F32)

        @pl.when(i == 0)
        def _():
            pend_k[...] = jnp.zeros_like(pend_k)
            pend_v[...] = jnp.zeros_like(pend_v)

        def unit(u, carry):
            sub = per - 1 - u // d
            cls = u % d
            base = sub * pr + cls
            sl = _strided(base, BAND, d)
            sl2 = _strided(base, 2 * BAND, d)
            q = qbuf[sl, :]
            dov = do_ref[sl, :]
            ov = o_ref[sl, :]
            lv = l_ref[sl, :]
            q2 = jnp.where(hm, jnp.concatenate([q, q], axis=0), 0.0).astype(BF)
            do2 = jnp.where(hm, jnp.concatenate([dov, dov], axis=0), 0.0)
            oo = dov * ov
            delta = jnp.sum(jnp.where(hm, jnp.concatenate([oo, oo], axis=0), 0.0), axis=-1, keepdims=True)
            lse2 = jnp.concatenate([lv[:, 0:1], lv[:, HEAD_DIM:HEAD_DIM + 1]], axis=0)
            do2 = do2.astype(BF)
            k2 = kbuf[sl2, :].astype(BF)
            v2 = vbuf[sl2, :].astype(BF)
            s = lax.dot_general(q2, k2, (((1,), (1,)), ((), ())), preferred_element_type=F32)
            p = jnp.exp(s + bias[((n > 0) | (sub > 0)).astype(jnp.int32)] - lse2)
            dp = lax.dot_general(do2, v2, (((1,), (1,)), ((), ())), preferred_element_type=F32)
            ds = (p * (dp - delta)).astype(BF)
            dq2 = jnp.dot(ds, k2, preferred_element_type=F32)
            dq = jnp.where(low, dq2[:BAND], dq2[BAND:])
            dq_s[sl, :] = dq
            tn = (((0,), (0,)), ((), ()))
            dk2 = lax.dot_general(ds, q2, tn, preferred_element_type=F32)
            dv2 = lax.dot_general(p.astype(BF), do2, tn, preferred_element_type=F32)
            dk_s[sl, :] = dk2[BAND:] + pend_k[cls]
            dv_s[sl, :] = dv2[BAND:] + pend_v[cls]
            pend_k[cls] = dk2[:BAND]
            pend_v[cls] = dv2[:BAND]
            return carry

        lax.fori_loop(0, units, unit, 0, unroll=min(ATTN_BWD_UNROLL, units))

        whole = _first_half_mask(R)
        dq_ref[...] = (rope_bwd(dq_s[...], cos_ref[...], ss_ref[...], whole) * scale).astype(BF)
        dk_ref[...] = rope_bwd(dk_s[...], cos_ref[...], ss_ref[...], whole).astype(BF)
        dv_ref[...] = dv_s[...].astype(BF)

    blk = (R, LANES)
    pblk = (pr, LANES)
    cur = lambda hp, i: nblk - 1 - i
    prv = lambda hp, i: jnp.maximum((nblk - 1 - i) * per - 1, 0)
    in_specs = [
        pl.BlockSpec(blk, lambda hp, i: (cur(hp, i), g * HP + hp)),
        pl.BlockSpec(pblk, lambda hp, i: (prv(hp, i), g * HP + hp)),
        pl.BlockSpec(blk, lambda hp, i: (cur(hp, i), g * HP + hp)),
        pl.BlockSpec(pblk, lambda hp, i: (prv(hp, i), (NB + g) * HP + hp)),
        pl.BlockSpec(blk, lambda hp, i: (cur(hp, i), (NB + g) * HP + hp)),
        pl.BlockSpec(blk, lambda hp, i: (cur(hp, i), hp)),
        pl.BlockSpec(blk, lambda hp, i: (cur(hp, i), hp)),
        pl.BlockSpec(blk, lambda hp, i: (cur(hp, i), hp)),
        pl.BlockSpec(blk, lambda hp, i: (cur(hp, i), 0)),
        pl.BlockSpec(blk, lambda hp, i: (cur(hp, i), 0)),
    ]
    args = [q_all, kv_all, kv_all, kv_all, kv_all, do, o, lse, cos2, ss2]
    if dep is not None:
        in_specs.append(ANY)
        args.append(dep)
    aliases = {}
    if prev is not None:
        in_specs += [ANY, ANY, ANY]
        aliases = {len(args): 0, len(args) + 1: 1, len(args) + 2: 2}
        args += list(prev)
    wide = jax.ShapeDtypeStruct((T, NB * Dm), BF)
    out = pl.BlockSpec(blk, lambda hp, i: (cur(hp, i), g * HP + hp))
    return pl.pallas_call(
        body, name=name,
        grid=(HP, nblk),
        in_specs=in_specs,
        out_specs=[out, out, out],
        out_shape=[wide, wide, wide],
        scratch_shapes=[pltpu.VMEM(blk, F32), pltpu.VMEM((pr + R, LANES), F32), pltpu.VMEM((pr + R, LANES), F32),
                        pltpu.VMEM(blk, F32), pltpu.VMEM(blk, F32), pltpu.VMEM(blk, F32),
                        pltpu.VMEM((d, BAND, LANES), F32), pltpu.VMEM((d, BAND, LANES), F32),
                        pltpu.VMEM((2, 2 * BAND, 2 * BAND), F32)],
        input_output_aliases=aliases,
        compiler_params=_cp("arbitrary", "arbitrary"),
    )(*args)


def _postnorm_bwd(dh, y, g_ref_val):
    r = lax.rsqrt(jnp.mean(y * y, axis=-1, keepdims=True) + RMS_EPS)
    yn = y * r
    dyn = dh * g_ref_val
    dy = r * (dyn - yn * jnp.mean(dyn * yn, axis=-1, keepdims=True))
    return dy, yn


def _after(body, n_in, dep):
    if dep is None:
        return body
    return lambda *refs: body(*refs[:n_in], *refs[n_in + 1:])


def _dep_spec(dep):
    return [] if dep is None else [ANY]


def _dep_arg(dep):
    return [] if dep is None else [dep]


def _postnorm_bwd_matmul(dh, y, gain, w3, widx, *, name, da_dtype, dep=None):
    T, D = dh.shape
    K = w3.shape[1]
    tm = min(ROW_BLOCK_WIDE, T)

    def body(dh_ref, y_ref, g_ref, w_ref, dy_ref, da_ref, dg_ref):
        i = pl.program_id(0)

        @pl.when(i == 0)
        def _():
            dg_ref[...] = jnp.zeros_like(dg_ref)

        dhv = dh_ref[...]
        dy, yn = _postnorm_bwd(dhv, y_ref[...].astype(F32), g_ref[...])
        dg_ref[...] += jnp.sum(dhv * yn, axis=0, keepdims=True)
        dyb = dy.astype(BF)
        dy_ref[...] = dyb
        da = lax.dot_general(dyb, w_ref[...], (((1,), (1,)), ((), ())), preferred_element_type=F32)
        da_ref[...] = da.astype(da_dtype)

    return pl.pallas_call(
        _after(body, 4, dep), name=name,
        grid=(T // tm,),
        in_specs=[
            pl.BlockSpec((tm, D), lambda i: (i, 0)),
            pl.BlockSpec((tm, D), lambda i: (i, 0)),
            pl.BlockSpec((1, D), lambda i: (0, 0)),
            pl.BlockSpec((None, K, D), lambda i: (widx, 0, 0)),
        ] + _dep_spec(dep),
        out_specs=[pl.BlockSpec((tm, D), lambda i: (i, 0)),
                   pl.BlockSpec((tm, K), lambda i: (i, 0)),
                   pl.BlockSpec((1, D), lambda i: (0, 0))],
        out_shape=[jax.ShapeDtypeStruct((T, D), BF), jax.ShapeDtypeStruct((T, K), da_dtype),
                   jax.ShapeDtypeStruct((1, D), F32)],
        compiler_params=_cp("arbitrary"),
    )(dh, y, gain, w3, *_dep_arg(dep))


def _postnorm_bwd_swiglu(dh, y, gain, wd3, layer, s, ut, *, name, dep=None):
    T, D = dh.shape
    F = wd3.shape[1]
    tm = min(ROW_BLOCK, T)

    def body(dh_ref, y_ref, g_ref, w_ref, s_ref, ut_ref, dy_ref, dgo_ref, duo_ref, dgain_ref):
        @pl.when(pl.program_id(0) == 0)
        def _():
            dgain_ref[...] = jnp.zeros_like(dgain_ref)

        dhv = dh_ref[...]
        dy, yn = _postnorm_bwd(dhv, y_ref[...].astype(F32), g_ref[...])
        dgain_ref[...] += jnp.sum(dhv * yn, axis=0, keepdims=True)
        dyb = dy.astype(BF)
        dy_ref[...] = dyb
        da = lax.dot_general(dyb, w_ref[...], (((1,), (1,)), ((), ())), preferred_element_type=F32)
        dgo_ref[...] = (da * ut_ref[...].astype(F32)).astype(BF)
        duo_ref[...] = (da * s_ref[...].astype(F32)).astype(BF)

    rows = pl.BlockSpec((tm, D), lambda i: (i, 0))
    wide = pl.BlockSpec((tm, F), lambda i: (i, 0))
    act = jax.ShapeDtypeStruct((T, F), BF)
    return pl.pallas_call(
        _after(body, 6, dep), name=name,
        grid=(T // tm,),
        in_specs=[
            rows, rows,
            pl.BlockSpec((1, D), lambda i: (0, 0)),
            pl.BlockSpec((None, F, D), lambda i: (layer, 0, 0), pipeline_mode=pl.Buffered(1)),
            wide, wide,
        ] + _dep_spec(dep),
        out_specs=[rows, wide, wide, pl.BlockSpec((1, D), lambda i: (0, 0))],
        out_shape=[jax.ShapeDtypeStruct((T, D), BF), act, act, jax.ShapeDtypeStruct((1, D), F32)],
        compiler_params=_cp("arbitrary"),
    )(dh, y, gain, wd3, s, ut, *_dep_arg(dep))


def _matmul_prenorm_bwd(dzs, wg, layer, h, gain, dh_in, *, name):
    T, D = h.shape
    n = wg.shape[2]
    tm = min(ROW_BLOCK, T)
    per = N_CHIPS // len(dzs)

    def body(*refs):
        dz_refs = refs[:len(dzs)]
        w_ref, h_ref, g_ref, dhi_ref, dh_ref, dg_ref = refs[len(dzs):]

        @pl.when(pl.program_id(0) == 0)
        def _():
            dg_ref[...] = jnp.zeros_like(dg_ref)

        dhn = None
        for j in range(N_CHIPS):
            dz = dz_refs[j // per][:, (j % per) * n:(j % per + 1) * n]
            t = lax.dot_general(dz.astype(BF), w_ref[j], (((1,), (1,)), ((), ())), preferred_element_type=F32)
            dhn = t if dhn is None else dhn + t
        hv = h_ref[...]
        r = lax.rsqrt(jnp.mean(hv * hv, axis=-1, keepdims=True) + RMS_EPS)
        xh = hv * r
        dg_ref[...] += jnp.sum(dhn * xh, axis=0, keepdims=True)
        dxn = dhn * g_ref[...]
        dh_ref[...] = dhi_ref[...] + r * (dxn - xh * jnp.mean(dxn * xh, axis=-1, keepdims=True))

    rows = pl.BlockSpec((tm, D), lambda i: (i, 0))
    in_specs = [pl.BlockSpec((tm, per * n), lambda i: (i, 0)) for _ in dzs]
    in_specs += [pl.BlockSpec((N_CHIPS, D, n), lambda i: (0, layer, 0), pipeline_mode=pl.Buffered(1)),
                 rows, pl.BlockSpec((1, D), lambda i: (0, 0)), rows]
    return pl.pallas_call(
        body, name=name,
        grid=(T // tm,),
        in_specs=in_specs,
        out_specs=[rows, pl.BlockSpec((1, D), lambda i: (0, 0))],
        out_shape=[jax.ShapeDtypeStruct((T, D), F32), jax.ShapeDtypeStruct((1, D), F32)],
        compiler_params=_cp("arbitrary"),
    )(*dzs, wg, h, gain, dh_in)


def _grad_matmul(a, b, out_shape3, tme, tne, out_index, prev, *, name, dep=None):
    T, M = a.shape
    N = b.shape[1]
    tk = min(GRAD_CHUNK, T)
    nk = T // tk

    def body(a_ref, b_ref, *rest):
        o_ref, acc = rest[-2:]
        k = pl.program_id(2)
        part = jnp.dot(a_ref[...].astype(BF).T, b_ref[...].astype(BF), preferred_element_type=F32)

        @pl.when(k == 0)
        def _():
            acc[...] = part

        @pl.when(k > 0)
        def _():
            acc[...] += part

        @pl.when(k == nk - 1)
        def _():
            o_ref[...] = acc[...].astype(BF)

    in_specs = [pl.BlockSpec((tk, tme), lambda i, j, k: (k, i)),
                pl.BlockSpec((tk, tne), lambda i, j, k: (k, j))]
    args = [a, b]
    aliases = {}
    if prev is not None:
        in_specs.append(ANY)
        args.append(prev)
        aliases = {2: 0}
    in_specs += _dep_spec(dep)
    args += _dep_arg(dep)
    return pl.pallas_call(
        body, name=name,
        grid=(M // tme, N // tne, nk),
        in_specs=in_specs,
        out_specs=pl.BlockSpec((None, tme, tne), lambda i, j, k: out_index(i, j)),
        out_shape=jax.ShapeDtypeStruct(out_shape3, BF),
        scratch_shapes=[pltpu.VMEM((tme, tne), F32)],
        input_output_aliases=aliases,
        compiler_params=_cp("parallel", "parallel", "arbitrary"),
    )(*args)


def _row_tile(R, cap=512):
    fit = [t for t in range(16, min(R, cap) + 1, 16) if R % t == 0]
    return max(fit) if fit else R


def _cast_place(w3, layer, where, dtype, *, name, dep=None):
    _, R, C = w3.shape
    tr = _row_tile(R)

    def body(s_ref, w_ref, o_ref):
        o_ref[...] = w_ref[...].astype(o_ref.dtype)

    return pl.pallas_call(
        _after(body, 2, dep), name=name,
        grid_spec=pltpu.PrefetchScalarGridSpec(
            num_scalar_prefetch=1, grid=(R // tr,),
            in_specs=[pl.BlockSpec((None, tr, C), lambda i, s: (layer, i, 0))] + _dep_spec(dep),
            out_specs=pl.BlockSpec((None, tr, C), lambda i, s: (s[0], i, 0))),
        out_shape=jax.ShapeDtypeStruct((N_CHIPS, R, C), dtype),
        compiler_params=_cp("arbitrary"),
    )(where, w3, *_dep_arg(dep))


def _pair_sum(dw, theirs, where, *, name):
    G, rh, C = theirs.shape
    tr = _row_tile(rh)
    nr = rh // tr

    def body(s_ref, a_ref, b_ref, o_ref):
        o_ref[...] = (a_ref[...].astype(F32) + b_ref[...].astype(F32)).astype(BF)

    mine = pl.BlockSpec((None, tr, C), lambda g, i, s: (g, s[1] * nr + i, 0))
    spec = pl.BlockSpec((None, tr, C), lambda g, i, s: (g, i, 0))
    return pl.pallas_call(
        body, name=name,
        grid_spec=pltpu.PrefetchScalarGridSpec(
            num_scalar_prefetch=1, grid=(G, nr), in_specs=[mine, spec], out_specs=spec),
        out_shape=jax.ShapeDtypeStruct((G, rh, C), BF),
        compiler_params=_cp("arbitrary", "arbitrary"),
    )(where, dw, theirs)


def _chip_sum(landed, parts, where, total_rows, row_off, prev, *, name):
    G, rh, C = landed.shape
    tr = _row_tile(rh)
    nr = rh // tr
    base = row_off // tr

    def body(s_ref, l_ref, p_ref, *rest):
        o_ref = rest[-1]
        for j in range(G):
            def own(j=j):
                v = p_ref[...].astype(F32)
                o_ref[...] = v if j == 0 else o_ref[...] + v

            def other(j=j):
                v = l_ref[j].astype(F32)
                o_ref[...] = v if j == 0 else o_ref[...] + v

            pl.when(s_ref[0] == j)(own)
            pl.when(s_ref[0] != j)(other)

    in_specs = [pl.BlockSpec((G, tr, C), lambda i, s: (0, i, 0)),
                pl.BlockSpec((None, tr, C), lambda i, s: (s[0], i, 0))]
    args = [where, landed, parts]
    aliases = {}
    if prev is not None:
        in_specs.append(ANY)
        args.append(prev)
        aliases = {3: 0}
    return pl.pallas_call(
        body, name=name,
        grid_spec=pltpu.PrefetchScalarGridSpec(
            num_scalar_prefetch=1, grid=(nr,),
            in_specs=in_specs,
            out_specs=pl.BlockSpec((tr, C), lambda i, s: (base + s[1] * nr + i, 0))),
        out_shape=jax.ShapeDtypeStruct((total_rows, C), F32),
        input_output_aliases=aliases,
        compiler_params=_cp("arbitrary"),
    )(*args)


def _adamw(w, g, m, v, *, name, emit_grad=False):
    R, C = w.shape
    tr = _row_tile(R, cap=max(16, ADAMW_BLOCK_BYTES // (4 * C)))
    n_out = 4 if emit_grad else 3

    def body(w_ref, g_ref, m_ref, v_ref, d_ref, mo_ref, vo_ref, *go_ref):
        gv = g_ref[...]
        if emit_grad:
            go_ref[0][...] = gv
        mn = ADAM_B1 * m_ref[...] + (1.0 - ADAM_B1) * gv
        vn = ADAM_B2 * v_ref[...] + (1.0 - ADAM_B2) * jnp.square(gv)
        m_hat = mn / (1.0 - ADAM_B1 ** ADAM_STEP)
        v_hat = vn / (1.0 - ADAM_B2 ** ADAM_STEP)
        d_ref[...] = -ADAM_LR * (m_hat / (jnp.sqrt(v_hat) + ADAM_EPS) + ADAM_WD * w_ref[...])
        mo_ref[...] = mn
        vo_ref[...] = vn

    spec = pl.BlockSpec((tr, C), lambda i: (i, 0))
    shp = jax.ShapeDtypeStruct((R, C), F32)
    return pl.pallas_call(
        body, name=name, grid=(R // tr,), in_specs=[spec] * 4, out_specs=[spec] * n_out,
        out_shape=[shp] * n_out, compiler_params=_cp("parallel"),
    )(w, g, m, v)


def _place():
    x = lax.axis_index("x")
    y = lax.axis_index("y")
    c = lax.axis_index("c")
    chips = [(1 - x, y), (x, 1 - y), (1 - x, 1 - y)]
    return x, y, c, chips


def _chunk_rows(rows, row_bytes, align):
    if rows <= align:
        return rows
    cands = [r for r in range(align, rows + 1, align) if rows % r == 0]
    fit = [r for r in cands if r * row_bytes <= DMA_CHUNK_BYTES]
    return max(fit) if fit else min(cands)


def _row_align(dtype):
    return 8 * (4 // jnp.dtype(dtype).itemsize)


def _start_chunks(make, rows, rc):
    for r0 in range(0, rows, rc):
        make(r0, rc).start()


def _piece_rows(ref, piece, j, h, r0=0, n=None):
    _, lead, off, rows = piece
    rh = rows // 2
    n = rh if n is None else n
    if lead is not None:
        return ref.at[lead, j, pl.ds(h * rh + r0, n)]
    return ref.at[j, pl.ds(off + h * rh + r0, n)]


def _piece_chunk(arr, piece):
    rh = piece[3] // 2
    return rh, _chunk_rows(rh, arr.shape[-1] * arr.dtype.itemsize, _row_align(arr.dtype))


def _ag_start(arrays, taps, groups, *, name):
    na = len(arrays)
    ng = len(groups)
    nt = 0 if taps is None else 1
    n_sem = [3 * len(grp) + (3 if nt and g == 0 else 0) for g, grp in enumerate(groups)]

    def body(*refs):
        ins = refs[:na]
        taps_ref = refs[na] if nt else None
        sems = refs[na + nt:na + nt + 2 * ng]
        token = refs[-1]
        token[...] = jnp.zeros_like(token)
        x, y, c, chips = _place()
        myj = 2 * x + y
        for g, grp in enumerate(groups):
            ssem, rsem = sems[2 * g], sems[2 * g + 1]
            for idx, piece in enumerate(grp):
                ref = ins[piece[0]]
                rh, rc = _piece_chunk(arrays[piece[0]], piece)
                for k, (px, py) in enumerate(chips):
                    def send(r0, n, ref=ref, piece=piece, idx=idx, k=k, px=px, py=py, ssem=ssem, rsem=rsem):
                        part = _piece_rows(ref, piece, myj, c, r0, n)
                        return pltpu.make_async_remote_copy(
                            src_ref=part, dst_ref=part, send_sem=ssem.at[3 * idx + k], recv_sem=rsem.at[3 * idx + k],
                            device_id=(px, py, c), device_id_type=MESH)
                    _start_chunks(send, rh, rc)
            if nt and g == 0:
                for k, (px, py) in enumerate(chips):
                    pltpu.make_async_remote_copy(
                        src_ref=taps_ref.at[myj], dst_ref=taps_ref.at[myj],
                        send_sem=ssem.at[3 * len(grp) + k], recv_sem=rsem.at[3 * len(grp) + k],
                        device_id=(px, py, c), device_id_type=MESH).start()

    sem_shapes = []
    for n in n_sem:
        sem_shapes += [pltpu.SemaphoreType.DMA((n,)), pltpu.SemaphoreType.DMA((n,))]
    ops = list(arrays) + ([taps] if nt else [])
    bufs = [pltpu.HBM(a.shape, a.dtype) for a in ops]
    outs = pl.pallas_call(
        body, name=name,
        out_shape=(*sem_shapes, *bufs, jax.ShapeDtypeStruct((8, LANES), F32)),
        in_specs=[HBM] * (na + nt),
        out_specs=(*([SEM] * (2 * ng)), *([HBM] * (na + nt)), pl.BlockSpec(memory_space=pltpu.VMEM)),
        input_output_aliases={i: 2 * ng + i for i in range(na + nt)},
        compiler_params=pltpu.CompilerParams(has_side_effects=EFFECT),
    )(*[_in_hbm(a) for a in ops])
    sems = [(outs[2 * g], outs[2 * g + 1]) for g in range(ng)]
    return sems, list(outs[2 * ng:2 * ng + na]), (outs[2 * ng + na] if nt else None), outs[-1]


def _ag_wait(sems, vals, taps, group, after, *, name):
    nv = len(vals)
    extra = ([taps] if taps is not None else [])
    nb = nv + len(extra)

    def body(*refs):
        bufs = refs[:nb]
        ssem, rsem = refs[nb], refs[nb + 1]
        x, y, c, chips = _place()
        for idx, piece in enumerate(group):
            for k, (px, py) in enumerate(chips):
                got = _piece_rows(bufs[piece[0]], piece, 2 * px + py, c)
                cp = pltpu.make_async_remote_copy(
                    src_ref=got, dst_ref=got, send_sem=ssem.at[3 * idx + k], recv_sem=rsem.at[3 * idx + k],
                    device_id=(px, py, c), device_id_type=MESH)
                cp.wait_send()
                cp.wait_recv()
        if taps is not None:
            for k, (px, py) in enumerate(chips):
                got = bufs[nv].at[2 * px + py]
                cp = pltpu.make_async_remote_copy(
                    src_ref=got, dst_ref=got, send_sem=ssem.at[3 * len(group) + k],
                    recv_sem=rsem.at[3 * len(group) + k], device_id=(px, py, c), device_id_type=MESH)
                cp.wait_send()
                cp.wait_recv()

    ops = list(vals) + extra
    shapes = [pltpu.HBM(a.shape, a.dtype) for a in ops]
    outs = pl.pallas_call(
        body, name=name,
        out_shape=tuple(shapes),
        in_specs=[HBM] * nb + [SEM, SEM] + _dep_spec(after),
        out_specs=[HBM] * nb,
        input_output_aliases={i: i for i in range(nb)},
        compiler_params=pltpu.CompilerParams(has_side_effects=EFFECT),
    )(*ops, sems[0], sems[1], *_dep_arg(after))
    return list(outs[:nv]), (outs[nv] if taps is not None else None)


def _ag_forward(vals, group, *, name):
    nv = len(vals)
    npc = len(group)

    def body(*refs):
        bufs = refs[nv:2 * nv]
        fsem, gsem = refs[2 * nv:]
        x, y, c, chips = _place()
        sib = (x, y, 1 - c)
        sent = []
        for idx, piece in enumerate(group):
            rh, rc = _piece_chunk(vals[piece[0]], piece)
            for k, (px, py) in enumerate(chips):
                def fwd(r0, n, piece=piece, idx=idx, k=k, pj=2 * px + py):
                    part = _piece_rows(bufs[piece[0]], piece, pj, c, r0, n)
                    return pltpu.make_async_remote_copy(
                        src_ref=part, dst_ref=part, send_sem=fsem.at[3 * idx + k], recv_sem=gsem.at[3 * idx + k],
                        device_id=sib, device_id_type=MESH)
                _start_chunks(fwd, rh, rc)
                sent.append(fwd(0, rh))
        for idx, piece in enumerate(group):
            for k, (px, py) in enumerate(chips):
                theirs = _piece_rows(bufs[piece[0]], piece, 2 * px + py, 1 - c)
                pltpu.make_async_remote_copy(
                    src_ref=theirs, dst_ref=theirs, send_sem=fsem.at[3 * idx + k], recv_sem=gsem.at[3 * idx + k],
                    device_id=sib, device_id_type=MESH).wait_recv()
        for cp in sent:
            cp.wait_send()

    return pl.pallas_call(
        body, name=name,
        in_specs=[ANY] * nv, out_specs=[ANY] * nv,
        out_shape=[jax.ShapeDtypeStruct(v.shape, v.dtype) for v in vals],
        input_output_aliases={i: i for i in range(nv)},
        scratch_shapes=[pltpu.SemaphoreType.DMA((3 * npc,)), pltpu.SemaphoreType.DMA((3 * npc,))],
    )(*vals)


def _sibling_swap(dws, *, name):
    nm = len(dws)
    shapes = [jax.ShapeDtypeStruct((dw.shape[0], dw.shape[1] // 2, dw.shape[2]), dw.dtype) for dw in dws]

    def body(*refs):
        ins = refs[:nm]
        theirs = refs[nm:2 * nm]
        ssem, rsem = refs[2 * nm:]
        x, y, c, _ = _place()
        sib = (x, y, 1 - c)
        cps = []
        for m in range(nm):
            G, rh, cols = shapes[m].shape
            rc = _chunk_rows(rh, cols * shapes[m].dtype.itemsize, _row_align(shapes[m].dtype))
            for j in range(G):
                _start_chunks(lambda r0, n, m=m, j=j, rh=rh: pltpu.make_async_remote_copy(
                    src_ref=ins[m].at[j, pl.ds((1 - c) * rh + r0, n)],
                    dst_ref=theirs[m].at[j, pl.ds(r0, n)], send_sem=ssem.at[m], recv_sem=rsem.at[m],
                    device_id=sib, device_id_type=MESH), rh, rc)
            cps.append(pltpu.make_async_remote_copy(
                src_ref=ins[m].at[:, pl.ds((1 - c) * rh, rh), :], dst_ref=theirs[m],
                send_sem=ssem.at[m], recv_sem=rsem.at[m], device_id=sib, device_id_type=MESH))
        for cp in cps:
            cp.wait()

    return pl.pallas_call(
        body, name=name,
        in_specs=[ANY] * nm, out_specs=[ANY] * nm, out_shape=shapes,
        scratch_shapes=[pltpu.SemaphoreType.DMA((nm,)), pltpu.SemaphoreType.DMA((nm,))],
    )(*dws)


HBM = pl.BlockSpec(memory_space=pltpu.HBM)
SEM = pl.BlockSpec(memory_space=pltpu.SEMAPHORE)
EFFECT = pltpu.SideEffectType.DATAFLOW_SIDE_EFFECTING


def _in_hbm(a):
    return pltpu.with_memory_space_constraint(a, pltpu.HBM)


def _swap_start(dws, *, name):
    nm = len(dws)
    lands = [(dw.shape[0], dw.shape[1] // 2, dw.shape[2]) for dw in dws]

    def body(*refs):
        ins = refs[:nm]
        lnd = refs[nm:2 * nm]
        ssem, rsem = refs[2 * nm:2 * nm + 2]
        token = refs[-1]
        x, y, c, _ = _place()
        sib = (x, y, 1 - c)
        for m in range(nm):
            G, rh, cols = lands[m]
            rc = _chunk_rows(rh, cols * dws[m].dtype.itemsize, _row_align(dws[m].dtype))
            for j in range(G):
                _start_chunks(lambda r0, n, m=m, j=j, rh=rh: pltpu.make_async_remote_copy(
                    src_ref=ins[m].at[j, pl.ds((1 - c) * rh + r0, n)], dst_ref=lnd[m].at[j, pl.ds(r0, n)],
                    send_sem=ssem.at[m], recv_sem=rsem.at[m], device_id=sib, device_id_type=MESH), rh, rc)
        token[...] = jnp.zeros_like(token)

    src = [pltpu.HBM(dw.shape, dw.dtype) for dw in dws]
    dst = [pltpu.HBM(s, dw.dtype) for s, dw in zip(lands, dws)]
    outs = pl.pallas_call(
        body, name=name,
        out_shape=(pltpu.SemaphoreType.DMA((nm,)), pltpu.SemaphoreType.DMA((nm,)), *src, *dst,
                   jax.ShapeDtypeStruct((8, LANES), F32)),
        in_specs=[HBM] * (2 * nm),
        out_specs=(SEM, SEM, *([HBM] * (2 * nm)), pl.BlockSpec(memory_space=pltpu.VMEM)),
        input_output_aliases={i: 2 + i for i in range(2 * nm)},
        compiler_params=pltpu.CompilerParams(has_side_effects=EFFECT),
    )(*[_in_hbm(dw) for dw in dws], *[_in_hbm(lax.empty(s, dw.dtype)) for s, dw in zip(lands, dws)])
    return (outs[0], outs[1], list(outs[2:2 + nm]), list(outs[2 + nm:2 + 2 * nm])), outs[-1]


def _swap_wait(handle, after, *, name):
    ssem_in, rsem_in, dws, lands = handle
    nm = len(dws)

    def body(*refs):
        ins = refs[:nm]
        lnd = refs[nm:2 * nm]
        ssem, rsem = refs[2 * nm:2 * nm + 2]
        x, y, c, _ = _place()
        for m in range(nm):
            rh = lands[m].shape[1]
            cp = pltpu.make_async_remote_copy(
                src_ref=ins[m].at[:, pl.ds((1 - c) * rh, rh), :], dst_ref=lnd[m],
                send_sem=ssem.at[m], recv_sem=rsem.at[m], device_id=(x, y, 1 - c), device_id_type=MESH)
            cp.wait_send()
            cp.wait_recv()

    bufs = [pltpu.HBM(a.shape, a.dtype) for a in list(dws) + list(lands)]
    outs = pl.pallas_call(
        body, name=name,
        out_shape=tuple(bufs),
        in_specs=[HBM] * (2 * nm) + [SEM, SEM, ANY],
        out_specs=[HBM] * (2 * nm),
        input_output_aliases={i: i for i in range(2 * nm)},
        compiler_params=pltpu.CompilerParams(has_side_effects=EFFECT),
    )(*dws, *lands, ssem_in, rsem_in, after)
    return list(outs[:nm]), list(outs[nm:])


def _exchange_start(parts, *, name):
    nm = len(parts)

    def body(*refs):
        ins = refs[:nm]
        lands = refs[nm:2 * nm]
        ssem, rsem = refs[2 * nm:2 * nm + 2]
        token = refs[-1]
        x, y, c, chips = _place()
        myj = 2 * x + y
        for m in range(nm):
            _, rh, cols = parts[m].shape
            rc = _chunk_rows(rh, cols * parts[m].dtype.itemsize, _row_align(parts[m].dtype))
            for k, (px, py) in enumerate(chips):
                _start_chunks(lambda r0, n, m=m, k=k, px=px, py=py: pltpu.make_async_remote_copy(
                    src_ref=ins[m].at[2 * px + py, pl.ds(r0, n)], dst_ref=lands[m].at[myj, pl.ds(r0, n)],
                    send_sem=ssem.at[3 * m + k], recv_sem=rsem.at[3 * m + k],
                    device_id=(px, py, c), device_id_type=MESH), rh, rc)
        token[...] = jnp.zeros_like(token)

    bufs = [pltpu.HBM(p.shape, p.dtype) for p in parts]
    outs = pl.pallas_call(
        body, name=name,
        out_shape=(pltpu.SemaphoreType.DMA((3 * nm,)), pltpu.SemaphoreType.DMA((3 * nm,)), *bufs, *bufs,
                   jax.ShapeDtypeStruct((8, LANES), F32)),
        in_specs=[HBM] * (2 * nm),
        out_specs=(SEM, SEM, *([HBM] * (2 * nm)), pl.BlockSpec(memory_space=pltpu.VMEM)),
        input_output_aliases={i: 2 + i for i in range(2 * nm)},
        compiler_params=pltpu.CompilerParams(has_side_effects=EFFECT),
    )(*[_in_hbm(p) for p in parts], *[_in_hbm(lax.empty(p.shape, p.dtype)) for p in parts])
    return (outs[0], outs[1], list(outs[2:2 + nm]), list(outs[2 + nm:2 + 2 * nm])), outs[-1]


def _exchange_wait(handle, after, *, name):
    ssem_in, rsem_in, parts, lands = handle
    nm = len(parts)

    def body(*refs):
        ins = refs[:nm]
        lnd = refs[nm:2 * nm]
        ssem, rsem = refs[2 * nm:2 * nm + 2]
        x, y, c, chips = _place()
        for m in range(nm):
            for k, (px, py) in enumerate(chips):
                pj = 2 * px + py
                cp = pltpu.make_async_remote_copy(
                    src_ref=ins[m].at[pj], dst_ref=lnd[m].at[pj],
                    send_sem=ssem.at[3 * m + k], recv_sem=rsem.at[3 * m + k],
                    device_id=(px, py, c), device_id_type=MESH)
                cp.wait_send()
                cp.wait_recv()

    bufs = [pltpu.HBM(p.shape, p.dtype) for p in parts]
    outs = pl.pallas_call(
        body, name=name,
        out_shape=(*bufs, *bufs),
        in_specs=[HBM] * (2 * nm) + [SEM, SEM, ANY],
        out_specs=[HBM] * (2 * nm),
        input_output_aliases={i: i for i in range(2 * nm)},
        compiler_params=pltpu.CompilerParams(has_side_effects=EFFECT),
    )(*parts, *lands, ssem_in, rsem_in, after)
    return list(outs[nm:]), list(outs[:nm])


def _sibling_join(grads, regions, *, name):
    nm = len(grads)
    nr = len(regions)
    shapes = [jax.ShapeDtypeStruct(g.shape, g.dtype) for g in grads]

    def body(*refs):
        outs = refs[nm:2 * nm]
        ssem, rsem = refs[2 * nm:]
        x, y, c, _ = _place()
        sib = (x, y, 1 - c)
        cps = []
        for i, (m, off, rows) in enumerate(regions):
            rh, cols = rows // 2, grads[m].shape[1]
            rc = _chunk_rows(rh, cols * grads[m].dtype.itemsize, _row_align(grads[m].dtype))

            def send(r0, n, i=i, m=m, off=off, rh=rh):
                part = outs[m].at[pl.ds(off + c * rh + r0, n)]
                return pltpu.make_async_remote_copy(
                    src_ref=part, dst_ref=part, send_sem=ssem.at[i], recv_sem=rsem.at[i],
                    device_id=sib, device_id_type=MESH)
            _start_chunks(send, rh, rc)
            cps.append(send(0, rh))
        for i, (m, off, rows) in enumerate(regions):
            rh = rows // 2
            cps[i].wait_send()
            got = outs[m].at[pl.ds(off + (1 - c) * rh, rh)]
            pltpu.make_async_remote_copy(
                src_ref=got, dst_ref=got, send_sem=ssem.at[i], recv_sem=rsem.at[i],
                device_id=sib, device_id_type=MESH).wait_recv()

    return pl.pallas_call(
        body, name=name,
        in_specs=[ANY] * nm, out_specs=[ANY] * nm, out_shape=shapes,
        input_output_aliases={i: i for i in range(nm)},
        scratch_shapes=[pltpu.SemaphoreType.DMA((nr,)), pltpu.SemaphoreType.DMA((nr,))],
    )(*grads)


def _join_start(grads, regions, *, name):
    nm = len(grads)
    nr = len(regions)

    def body(*refs):
        ins = refs[:nm]
        ssem, rsem = refs[nm:nm + 2]
        token = refs[-1]
        x, y, c, _ = _place()
        sib = (x, y, 1 - c)
        for i, (m, off, rows) in enumerate(regions):
            rh, cols = rows // 2, grads[m].shape[1]
            rc = _chunk_rows(rh, cols * grads[m].dtype.itemsize, _row_align(grads[m].dtype))

            def send(r0, n, i=i, m=m, off=off, rh=rh):
                part = ins[m].at[pl.ds(off + c * rh + r0, n)]
                return pltpu.make_async_remote_copy(
                    src_ref=part, dst_ref=part, send_sem=ssem.at[i], recv_sem=rsem.at[i],
                    device_id=sib, device_id_type=MESH)
            _start_chunks(send, rh, rc)
        token[...] = jnp.zeros_like(token)

    bufs = [pltpu.HBM(g.shape, g.dtype) for g in grads]
    outs = pl.pallas_call(
        body, name=name,
        out_shape=(pltpu.SemaphoreType.DMA((nr,)), pltpu.SemaphoreType.DMA((nr,)), *bufs,
                   jax.ShapeDtypeStruct((8, LANES), F32)),
        in_specs=[HBM] * nm,
        out_specs=(SEM, SEM, *([HBM] * nm), pl.BlockSpec(memory_space=pltpu.VMEM)),
        input_output_aliases={i: 2 + i for i in range(nm)},
        compiler_params=pltpu.CompilerParams(has_side_effects=EFFECT),
    )(*[_in_hbm(g) for g in grads])
    return (outs[0], outs[1], list(outs[2:2 + nm])), outs[-1]


def _join_wait(handle, regions, after, *, name):
    ssem_in, rsem_in, grads = handle
    nm = len(grads)

    def body(*refs):
        ins = refs[:nm]
        ssem, rsem = refs[nm:nm + 2]
        x, y, c, _ = _place()
        for i, (m, off, rows) in enumerate(regions):
            rh = rows // 2
            cp = pltpu.make_async_remote_copy(
                src_ref=ins[m].at[pl.ds(off + c * rh, rh)], dst_ref=ins[m].at[pl.ds(off + (1 - c) * rh, rh)],
                send_sem=ssem.at[i], recv_sem=rsem.at[i], device_id=(x, y, 1 - c), device_id_type=MESH)
            cp.wait_send()
            cp.wait_recv()

    bufs = [pltpu.HBM(g.shape, g.dtype) for g in grads]
    outs = pl.pallas_call(
        body, name=name,
        out_shape=tuple(bufs),
        in_specs=[HBM] * nm + [SEM, SEM, ANY],
        out_specs=[HBM] * nm,
        input_output_aliases={i: i for i in range(nm)},
        compiler_params=pltpu.CompilerParams(has_side_effects=EFFECT),
    )(*grads, ssem_in, rsem_in, after)
    return list(outs)


def _all_reduce_small(pack):
    R, C = pack.shape

    def body(in_ref, out_ref, slots, ssem, rsem):
        x, y, c, _ = _place()
        me = 4 * x + 2 * y + c
        slots[me] = in_ref[...]
        cps = []
        for k in range(1, N_DEV):
            dx, dy, dc = (k >> 2) & 1, (k >> 1) & 1, k & 1
            peer = (x ^ dx, y ^ dy, c ^ dc)
            cp = pltpu.make_async_remote_copy(
                src_ref=in_ref, dst_ref=slots.at[me], send_sem=ssem.at[k], recv_sem=rsem.at[k],
                device_id=peer, device_id_type=MESH)
            cp.start()
            cps.append(cp)
        for k in range(1, N_DEV):
            dx, dy, dc = (k >> 2) & 1, (k >> 1) & 1, k & 1
            got = slots.at[4 * (x ^ dx) + 2 * (y ^ dy) + (c ^ dc)]
            pltpu.make_async_remote_copy(
                src_ref=got, dst_ref=got, send_sem=ssem.at[k], recv_sem=rsem.at[k],
                device_id=(x ^ dx, y ^ dy, c ^ dc), device_id_type=MESH).wait_recv()
        for cp in cps:
            cp.wait_send()
        acc = slots[0]
        for s in range(1, N_DEV):
            acc = acc + slots[s]
        out_ref[...] = acc

    return pl.pallas_call(
        body, name="ar_small",
        in_specs=[pl.BlockSpec(memory_space=pltpu.VMEM)],
        out_specs=pl.BlockSpec(memory_space=pltpu.VMEM),
        out_shape=jax.ShapeDtypeStruct((R, C), F32),
        scratch_shapes=[pltpu.VMEM((N_DEV, R, C), F32),
                        pltpu.SemaphoreType.DMA((N_DEV,)), pltpu.SemaphoreType.DMA((N_DEV,))],
    )(pack)


def kernel(x, positions, mix_norm_pre, mix_norm_post, ffn_norm_pre, ffn_norm_post, ffn_w_gate_up, ffn_w_down, conv_w_in, conv_w, conv_w_out, kv_norm, w_kv, w_q, w_o, loss_target, m_mix_norm_pre, m_mix_norm_post, m_ffn_norm_pre, m_ffn_norm_post, m_ffn_w_gate_up, m_ffn_w_down, m_conv_w_in, m_conv_w, m_conv_w_out, m_kv_norm, m_w_kv, m_w_q, m_w_o, v_mix_norm_pre, v_mix_norm_post, v_ffn_norm_pre, v_ffn_norm_post, v_ffn_w_gate_up, v_ffn_w_down, v_conv_w_in, v_conv_w, v_conv_w_out, v_kv_norm, v_w_kv, v_w_q, v_w_o):
    T, D = x.shape[1], x.shape[2]
    L = ffn_w_gate_up.shape[0]
    n_gu = ffn_w_gate_up.shape[2]
    f_sh = ffn_w_down.shape[1]
    F = N_CHIPS * f_sh
    x0 = x[0]
    tgt = loss_target[0]

    half = HEAD_DIM // 2
    inv_freq = ROPE_THETA ** (-jnp.arange(half, dtype=F32) / half)
    ang = positions[0].astype(F32)[:, None] * inv_freq
    cosv, sinv = jnp.cos(ang), jnp.sin(ang)
    cos2 = jnp.tile(cosv, (1, LANES // half))
    ss2 = jnp.tile(jnp.concatenate([-sinv, sinv], axis=1), (1, LANES // HEAD_DIM))

    def as2d(a):
        return a.reshape(-1, a.shape[-1])

    big = [ffn_w_gate_up, ffn_w_down, conv_w_in, conv_w_out, w_kv, w_q, w_o]
    big_m = [m_ffn_w_gate_up, m_ffn_w_down, m_conv_w_in, m_conv_w_out, m_w_kv, m_w_q, m_w_o]
    big_v = [v_ffn_w_gate_up, v_ffn_w_down, v_conv_w_in, v_conv_w_out, v_w_kv, v_w_q, v_w_o]
    chip = 2 * lax.axis_index("x") + lax.axis_index("y")
    where = jnp.stack([chip, lax.axis_index("c")]).astype(jnp.int32)
    tc = conv_w.shape[2]
    cw_pad = jnp.concatenate([conv_w[0], jnp.zeros((8 - conv_w.shape[1], tc), F32)], axis=0)

    GU0, GU1, WD0, WD1, WCI, WCO, WKV, WQ, WO = range(9)
    shards = [(ffn_w_gate_up, 0), (ffn_w_gate_up, 1), (ffn_w_down, 0), (ffn_w_down, 1), (conv_w_in, 0),
              (conv_w_out, 0), (w_kv[None], 0), (w_q, 0), (w_o, 0)]
    ag_groups = [
        [(WCI, None, 0, D), (WCO, None, 0, D // N_CHIPS)],
        [(GU0, None, 0, D), (WD0, None, 0, f_sh)],
        [(WKV, None, 0, D), (WQ, None, 0, D)],
        [(WO, None, 0, D // N_CHIPS), (GU1, None, 0, D), (WD1, None, 0, f_sh)],
    ]

    def localised(group, idxs):
        return [(idxs.index(p[0]),) + p[1:] for p in group]

    cur = [None] * len(shards)
    first = [WCI, WCO]
    sems0, vals, taps, token = _ag_start(
        [_cast_place(*shards[i], where, BF, name=f"place{i}") for i in first],
        _cast_place(cw_pad[None], 0, where, F32, name="place_taps"), [localised(ag_groups[0], first)],
        name="ag_start0")
    for i, v in zip(first, vals):
        cur[i] = v
    rest = [i for i in range(len(shards)) if i not in first]
    sems1, vals, _, _ = _ag_start(
        [_cast_place(*shards[i], where, BF, name=f"place{i}", dep=token) for i in rest], None,
        [localised(g, rest) for g in ag_groups[1:]], name="ag_start1")
    for i, v in zip(rest, vals):
        cur[i] = v
    ag_sems = sems0 + sems1

    def gather_group(g, after):
        nonlocal taps
        idxs = sorted({p[0] for p in ag_groups[g]})
        local = localised(ag_groups[g], idxs)
        vals, landed_taps = _ag_wait(ag_sems[g], [cur[i] for i in idxs], taps if g == 0 else None, local, after,
                                     name=f"ag_wait{g}")
        if g == 0:
            taps = landed_taps
        vals = _ag_forward(vals, local, name=f"ag_forward{g}")
        for i, v in zip(idxs, vals):
            cur[i] = v

    def row(a, i):
        return a[i:i + 1]

    gather_group(0, None)
    wci, wco, cw = cur[WCI], cur[WCO].reshape(1, D, D), taps
    z, hn_m0 = _norm_matmul(x0, row(mix_norm_pre, 0), wci, cos2, ss2, name="f0_conv_in",
                            rope_shards=0, scale=1.0, out_dtype=BF)
    vmix = _conv_fwd(z, cw, name="f0_conv")
    y0, h1 = _matmul_postnorm(vmix, wco, 0, row(mix_norm_post, 0), x0, name="f0_conv_out")
    gather_group(1, h1)
    wgu0, wd0 = cur[GU0], cur[WD0].reshape(1, F, D)
    s0, ut0, a0, hn_f0 = _norm_swiglu(h1, row(ffn_norm_pre, 0), wgu0, 0, name="f0_gate_up")
    f0, h2 = _matmul_postnorm(a0, wd0, 0, row(ffn_norm_post, 0), h1, name="f0_down")

    gather_group(2, h2)
    wkv, wq = cur[WKV], cur[WQ]
    kv_all, hn_kv = _norm_matmul(h2, kv_norm.reshape(1, D), wkv, cos2, ss2, name="f1_kv",
                                 rope_shards=N_CHIPS // 2, scale=1.0, out_dtype=BF)
    q_all, hn_m1 = _norm_matmul(h2, row(mix_norm_pre, 1), wq, cos2, ss2, name="f1_q",
                                rope_shards=N_CHIPS, scale=HEAD_DIM ** -0.5, out_dtype=BF)
    o_att, lse = _attn_fwd(q_all, kv_all, name="f1_attn")
    gather_group(3, o_att)
    wgu1, wd1, wo = cur[GU1], cur[WD1].reshape(1, F, D), cur[WO].reshape(1, D, D)
    y1, h3 = _matmul_postnorm(o_att, wo, 0, row(mix_norm_post, 1), h2, name="f1_attn_out")
    s1, ut1, a1, hn_f1 = _norm_swiglu(h3, row(ffn_norm_pre, 1), wgu1, 0, name="f1_gate_up")
    f1, dh4, sq = _matmul_postnorm_loss(a1, wd1, 0, row(ffn_norm_post, 1), h3, tgt, name="f1_down_loss")
    loss_part = 0.5 * sq[0, 0] / D

    gu_shape = (N_CHIPS, D, n_gu)
    in_chips = lambda a: a.reshape(N_CHIPS, -1, a.shape[-1])

    def scatter_start(dws, tag):
        theirs = _sibling_swap(dws, name=f"rs_swap_{tag}")
        parts = [_pair_sum(dw, t, where, name=f"rs_pair_sum_{tag}{i}") for i, (dw, t) in enumerate(zip(dws, theirs))]
        return _exchange_start(parts, name=f"rs_exchange_start_{tag}")

    def scatter_go(swap, after, tag):
        dws, theirs = _swap_wait(swap, after, name=f"rs_swap_wait_{tag}")
        parts = [_pair_sum(dw, t, where, name=f"rs_pair_sum_{tag}{i}") for i, (dw, t) in enumerate(zip(dws, theirs))]
        return _exchange_start(parts, name=f"rs_exchange_start_{tag}")

    dyf1, dg1, du1, d_ffn_post1 = _postnorm_bwd_swiglu(dh4, f1, row(ffn_norm_post, 1), wd1, 0, s1, ut1,
                                                       name="b1_down")
    dwd1 = _grad_matmul(a1, dyf1, (2, F // 2, D), F // 2, D, lambda i, j: (i, 0, 0), None, name="b1_dw_down")
    dwgu1 = _grad_matmul(hn_f1, dg1, gu_shape, D, n_gu, lambda i, j: (j, 0, 0), None, name="b1_dw_gate")
    dwgu1 = _grad_matmul(hn_f1, du1, gu_shape, D, n_gu, lambda i, j: (j + 2, 0, 0), dwgu1, name="b1_dw_up")
    dh3, d_ffn_pre1 = _matmul_prenorm_bwd((dg1, du1), wgu1, 0, h3, row(ffn_norm_pre, 1), dh4, name="b1_gate_up")

    dy1, do, d_mix_post1 = _postnorm_bwd_matmul(dh3, y1, row(mix_norm_post, 1), wo, 0, name="b1_attn_out",
                                                da_dtype=F32)
    dwo = _grad_matmul(o_att, dy1, (1, D, D), D, D, lambda i, j: (0, 0, 0), None, name="b1_dw_o")
    swap_a, token = _swap_start([dwgu1, in_chips(dwd1), in_chips(dwo)], name="rs_swap_start_a")
    prev = None
    for gi, (window, dil) in enumerate(BRANCHES):
        prev = _attn_bwd(q_all, kv_all, do, o_att, lse, cos2, ss2, gi, dil, prev, name=f"b1_attn{gi}", dep=token)
        token = None
        if gi == 0:
            rs_a, token = scatter_go(swap_a, prev[0], "a")
    dq_all, dk_all, dv_all = prev
    n_q = wq.shape[2]
    n_kv = wkv.shape[2]
    dwq = _grad_matmul(hn_m1, dq_all, (N_CHIPS, D, n_q), D, n_q, lambda i, j: (j, 0, 0), None, name="b1_dw_q")
    dwkv = _grad_matmul(hn_kv, dk_all, (N_CHIPS, D, n_kv), D, n_kv, lambda i, j: (j, 0, 0), None, name="b1_dw_k")
    dwkv = _grad_matmul(hn_kv, dv_all, (N_CHIPS, D, n_kv), D, n_kv, lambda i, j: (j + 2, 0, 0), dwkv, name="b1_dw_v")
    dh2, d_mix_pre1 = _matmul_prenorm_bwd((dq_all,), wq, 0, h2, row(mix_norm_pre, 1), dh3, name="b1_q")
    dh2, d_kv_norm = _matmul_prenorm_bwd((dk_all, dv_all), wkv, 0, h2, kv_norm.reshape(1, D), dh2, name="b1_kv")
    swap_b, token = _swap_start([dwkv, dwq], name="rs_swap_start_b")

    dyf0, dg0, du0, d_ffn_post0 = _postnorm_bwd_swiglu(dh2, f0, row(ffn_norm_post, 0), wd0, 0, s0, ut0,
                                                       name="b0_down", dep=token)
    rs_b, token = scatter_go(swap_b, dyf0, "b")
    dwd0 = _grad_matmul(a0, dyf0, (2, F // 2, D), F // 2, D, lambda i, j: (i, 0, 0), None, name="b0_dw_down",
                        dep=token)
    dwgu0 = _grad_matmul(hn_f0, dg0, gu_shape, D, n_gu, lambda i, j: (j, 0, 0), None, name="b0_dw_gate")
    dwgu0 = _grad_matmul(hn_f0, du0, gu_shape, D, n_gu, lambda i, j: (j + 2, 0, 0), dwgu0, name="b0_dw_up")
    dh1, d_ffn_pre0 = _matmul_prenorm_bwd((dg0, du0), wgu0, 0, h1, row(ffn_norm_pre, 0), dh2, name="b0_gate_up")
    swap_c, token = _swap_start([dwgu0, in_chips(dwd0)], name="rs_swap_start_c")

    dy0, dvmix, d_mix_post0 = _postnorm_bwd_matmul(dh1, y0, row(mix_norm_post, 0), wco, 0, name="b0_conv_out",
                                                   da_dtype=BF, dep=token)
    rs_c, token = scatter_go(swap_c, dy0, "c")
    dwco = _grad_matmul(vmix, dy0, (1, D, D), D, D, lambda i, j: (0, 0, 0), None, name="b0_dw_conv_out",
                        dep=token)
    dz, dcw = _conv_bwd(z, cw, dvmix, name="b0_conv")
    n_ci = wci.shape[2]
    dwci = _grad_matmul(hn_m0, dz, (N_CHIPS, D, n_ci), D, n_ci, lambda i, j: (j, 0, 0), None, name="b0_dw_conv_in")
    dx, d_mix_pre0 = _matmul_prenorm_bwd((dz,), wci, 0, x0, row(mix_norm_pre, 0), dh1, name="b0_conv_in")

    rs_d, _ = scatter_start([dwci, in_chips(dwco)], "d")

    pieces = {"a": [(0, D), (1, f_sh), (6, 0)], "b": [(4, 0), (5, 0)], "c": [(0, 0), (1, 0)], "d": [(2, 0), (3, 0)]}
    grads2d = [None] * len(big)
    big_out = [None] * len(big)

    def reduce_groups(groups, after):
        regions, idxs = [], []
        for gtag, handle in groups:
            landed, parts = _exchange_wait(handle, after, name=f"rs_exchange_wait_{gtag}")
            for i, (l, p, (wi, off)) in enumerate(zip(landed, parts, pieces[gtag])):
                total = as2d(big[wi]).shape[0]
                grads2d[wi] = _chip_sum(l, p, where, total, off, grads2d[wi], name=f"rs_chip_sum_{gtag}{i}")
                if wi not in idxs:
                    idxs.append(wi)
                regions.append((idxs.index(wi), off, 2 * l.shape[1]))
        return idxs, regions

    def apply(idxs, joined):
        for wi, gr in zip(idxs, joined):
            w = big[wi]
            d_, m_, v_, g_ = _adamw(as2d(w), gr, as2d(big_m[wi]), as2d(big_v[wi]), name=f"adamw{wi}",
                                    emit_grad=True)
            big_out[wi] = (g_.reshape(w.shape), d_.reshape(w.shape), m_.reshape(w.shape), v_.reshape(w.shape))

    idxs_abc, regions_abc = reduce_groups([("a", rs_a), ("b", rs_b), ("c", rs_c)], dx)
    join_abc, token = _join_start([grads2d[wi] for wi in idxs_abc], regions_abc, name="rs_join_start_abc")

    pack = jnp.concatenate([
        d_mix_pre0, d_mix_pre1, d_mix_post0, d_mix_post1, d_ffn_pre0, d_ffn_pre1, d_ffn_post0, d_ffn_post1,
        d_kv_norm, dcw[0:3], jnp.full((1, D), loss_part + token[0, 0], F32),
        jnp.zeros((SMALL_ROWS - 13, D), F32)], axis=0)
    red = _all_reduce_small(pack)
    loss = red[12, 0]
    myj = 2 * lax.axis_index("x") + lax.axis_index("y")
    g_conv_w = lax.dynamic_slice(red, (9, myj * tc), (3, tc))

    zeros7 = jnp.zeros((SMALL_ROWS - 9, D), F32)
    w_small = jnp.concatenate([mix_norm_pre, mix_norm_post, ffn_norm_pre, ffn_norm_post, kv_norm.reshape(1, D), zeros7], axis=0)
    m_small = jnp.concatenate([m_mix_norm_pre, m_mix_norm_post, m_ffn_norm_pre, m_ffn_norm_post, m_kv_norm.reshape(1, D), zeros7], axis=0)
    v_small = jnp.concatenate([v_mix_norm_pre, v_mix_norm_post, v_ffn_norm_pre, v_ffn_norm_post, v_kv_norm.reshape(1, D), zeros7], axis=0)
    d_small, nm_small, nv_small = _adamw(w_small, red, m_small, v_small, name="adamw_small")

    pad5 = jnp.zeros((5, tc), F32)
    d_cw, nm_cw, nv_cw = _adamw(cw_pad, jnp.concatenate([g_conv_w, pad5], axis=0),
                                jnp.concatenate([m_conv_w[0], pad5], axis=0),
                                jnp.concatenate([v_conv_w[0], pad5], axis=0), name="adamw_conv_w")

    idxs_d, regions_d = reduce_groups([("d", rs_d)], d_small)
    apply(idxs_d, _sibling_join([grads2d[wi] for wi in idxs_d], regions_d, name="rs_sibling_join_d"))
    apply(idxs_abc, _join_wait(join_abc, regions_abc, big_out[idxs_d[0]][1], name="rs_join_wait_abc"))

    def small(a):
        return (a[0:2], a[2:4], a[4:6], a[6:8])

    def assemble(sm, cwv, kind):
        pre, post, fpre, fpost = small(sm)
        b = [t[kind] for t in big_out]
        return [pre, post, fpre, fpost, b[0], b[1], b[2], cwv[0:3].reshape(conv_w.shape), b[3],
                sm[8], b[4], b[5].reshape(w_q.shape), b[6].reshape(w_o.shape)]

    grads = assemble(red, jnp.concatenate([g_conv_w, pad5], axis=0), 0)
    deltas = assemble(d_small, d_cw, 1)
    new_m = assemble(nm_small, nm_cw, 2)
    new_v = assemble(nv_small, nv_cw, 3)
    return (loss, dx.reshape(x.shape), *grads, *deltas, *new_m, *new_v)
```

```python
import functools

import jax
import jax.numpy as jnp
from jax import lax
from jax.experimental import pallas as pl
from jax.experimental.pallas import tpu as pltpu

HEAD_DIM = 64
BAND = 128
BRANCHES = ((128, 1), (512, 4), (2048, 16))
ROPE_THETA = 10000.0
RMS_EPS = 1e-6
NEG_INF = -1e30
ADAM_LR = 0.001
ADAM_B1 = 0.9
ADAM_B2 = 0.999
ADAM_EPS = 1e-08
ADAM_WD = 0.01
ADAM_STEP = 10

N_CHIPS = 4
N_DEV = 8
LANES = 128
ROW_BLOCK = 512
ROW_BLOCK_WIDE = 1024
GRAD_CHUNK = 2048
ATTN_FWD_UNROLL = 16
ATTN_BWD_UNROLL = 8
ATTN_BLOCK_ROWS = 2048
VMEM_LIMIT = 56 * 1024 * 1024
SMALL_ROWS = 16
ADAMW_BLOCK_BYTES = 2048 * 1024
DMA_CHUNK_BYTES = 512 * 1024

BF = jnp.bfloat16
F32 = jnp.float32
MESH = pl.DeviceIdType.MESH
ANY = pl.BlockSpec(memory_space=pl.ANY)


def _cp(*sem):
    return pltpu.CompilerParams(dimension_semantics=sem, vmem_limit_bytes=VMEM_LIMIT)


def _rot_half(t, first):
    return jnp.where(first, pltpu.roll(t, 96, 1), pltpu.roll(t, 32, 1))


def _sigmoid(x):
    return pl.reciprocal(1.0 + jnp.exp(-x), approx=True)


def _first_half_mask(rows):
    lane = lax.broadcasted_iota(jnp.int32, (rows, LANES), 1)
    return (lane % HEAD_DIM) < (HEAD_DIM // 2)


def _normed_rows(j, rows, x_ref, g_ref, xn_ref, xs, last_start, tm):
    @pl.when(j == 0)
    def _():
        xv = x_ref[...]
        r = lax.rsqrt(jnp.mean(xv * xv, axis=-1, keepdims=True) + RMS_EPS)
        xn = (xv * r * g_ref[...]).astype(BF)
        xs[rows, :] = xn
        xn_ref[...] = xn

    @pl.when(j > 0)
    def _():
        xn_ref[...] = xs[pl.ds(last_start, tm), :]


def _norm_matmul(x, gain, wg, cos2, ss2, *, name, rope_shards, scale, out_dtype):
    T, D = x.shape
    n = wg.shape[2]
    tm = min(ROW_BLOCK_WIDE, T)
    ni = T // tm

    def body(x_ref, g_ref, w_ref, cos_ref, ss_ref, y_ref, xn_ref, xs):
        j = pl.program_id(0)
        rows = pl.ds(pl.multiple_of(pl.program_id(1) * tm, tm), tm)
        _normed_rows(j, rows, x_ref, g_ref, xn_ref, xs, (ni - 1) * tm, tm)
        acc = jnp.dot(xs[rows, :], w_ref[...], preferred_element_type=F32)

        def plain():
            y_ref[...] = acc.astype(out_dtype)

        def rope():
            cosv = cos_ref[...]
            ssv = ss_ref[...]
            first = _first_half_mask(tm)
            for ci in range(n // LANES):
                t = acc[:, ci * LANES:(ci + 1) * LANES]
                y = (t * cosv + _rot_half(t, first) * ssv) * scale
                y_ref[:, ci * LANES:(ci + 1) * LANES] = y.astype(out_dtype)

        if rope_shards == 0:
            plain()
        elif rope_shards == N_CHIPS:
            rope()
        else:
            pl.when(j < rope_shards)(rope)
            pl.when(j >= rope_shards)(plain)

    first_pass = lambda j, i: (jnp.where(j == 0, i, ni - 1), 0)
    return pl.pallas_call(
        body, name=name,
        grid=(N_CHIPS, ni),
        in_specs=[
            pl.BlockSpec((tm, D), first_pass),
            pl.BlockSpec((1, D), lambda j, i: (0, 0)),
            pl.BlockSpec((None, D, n), lambda j, i: (j, 0, 0)),
            pl.BlockSpec((tm, LANES), lambda j, i: (i, 0)),
            pl.BlockSpec((tm, LANES), lambda j, i: (i, 0)),
        ],
        out_specs=[
            pl.BlockSpec((tm, n), lambda j, i: (i, j)),
            pl.BlockSpec((tm, D), first_pass),
        ],
        out_shape=[jax.ShapeDtypeStruct((T, N_CHIPS * n), out_dtype),
                   jax.ShapeDtypeStruct((T, D), BF)],
        scratch_shapes=[pltpu.VMEM((T, D), BF)],
        compiler_params=_cp("arbitrary", "arbitrary"),
    )(x, gain, wg, cos2, ss2)


def _norm_swiglu(x, gain, wg, layer, *, name):
    T, D = x.shape
    n = wg.shape[2]
    tm = min(ROW_BLOCK, T)
    ni = T // tm

    def body(x_ref, g_ref, wg_ref, wu_ref, so_ref, uto_ref, ao_ref, xn_ref, xs):
        j = pl.program_id(0)
        rows = pl.ds(pl.multiple_of(pl.program_id(1) * tm, tm), tm)
        _normed_rows(j, rows, x_ref, g_ref, xn_ref, xs, (ni - 1) * tm, tm)
        g = jnp.dot(xs[rows, :], wg_ref[...], preferred_element_type=F32)
        u = jnp.dot(xs[rows, :], wu_ref[...], preferred_element_type=F32)
        sg = _sigmoid(g)
        s = g * sg
        so_ref[...] = s.astype(BF)
        uto_ref[...] = (u * (sg + s * (1.0 - sg))).astype(BF)
        ao_ref[...] = (s * u).astype(BF)

    half = N_CHIPS // 2
    first_pass = lambda j, i: (jnp.where(j == 0, i, ni - 1), 0)
    act = jax.ShapeDtypeStruct((T, half * n), BF)
    return pl.pallas_call(
        body, name=name,
        grid=(half, ni),
        in_specs=[
            pl.BlockSpec((tm, D), first_pass),
            pl.BlockSpec((1, D), lambda j, i: (0, 0)),
            pl.BlockSpec((None, D, n), lambda j, i: (j, layer, 0)),
            pl.BlockSpec((None, D, n), lambda j, i: (j + half, layer, 0)),
        ],
        out_specs=[
            pl.BlockSpec((tm, n), lambda j, i: (i, j)),
            pl.BlockSpec((tm, n), lambda j, i: (i, j)),
            pl.BlockSpec((tm, n), lambda j, i: (i, j)),
            pl.BlockSpec((tm, D), first_pass),
        ],
        out_shape=[act, act, act, jax.ShapeDtypeStruct((T, D), BF)],
        scratch_shapes=[pltpu.VMEM((T, D), BF)],
        compiler_params=_cp("arbitrary", "arbitrary"),
    )(x, gain, wg, wg)


def _matmul_postnorm(a, w3, widx, gain, h_old, *, name):
    T, K = a.shape
    D = w3.shape[2]
    tm = min(ROW_BLOCK_WIDE, T)

    def body(a_ref, w_ref, g_ref, h_ref, y_ref, hn_ref):
        y = jnp.dot(a_ref[...].astype(BF), w_ref[...], preferred_element_type=F32)
        y_ref[...] = y.astype(BF)
        r = lax.rsqrt(jnp.mean(y * y, axis=-1, keepdims=True) + RMS_EPS)
        hn_ref[...] = h_ref[...] + y * r * g_ref[...]

    return pl.pallas_call(
        body, name=name,
        grid=(T // tm,),
        in_specs=[
            pl.BlockSpec((tm, K), lambda i: (i, 0)),
            pl.BlockSpec((None, K, D), lambda i: (widx, 0, 0)),
            pl.BlockSpec((1, D), lambda i: (0, 0)),
            pl.BlockSpec((tm, D), lambda i: (i, 0)),
        ],
        out_specs=[pl.BlockSpec((tm, D), lambda i: (i, 0)),
                   pl.BlockSpec((tm, D), lambda i: (i, 0))],
        out_shape=[jax.ShapeDtypeStruct((T, D), BF), jax.ShapeDtypeStruct((T, D), F32)],
        compiler_params=_cp("parallel"),
    )(a, w3, gain, h_old)


def _matmul_postnorm_loss(a, w3, widx, gain, h_old, target, *, name):
    T, K = a.shape
    D = w3.shape[2]
    tm = min(ROW_BLOCK, T)

    def body(a_ref, w_ref, g_ref, h_ref, t_ref, y_ref, dh_ref, s_ref):
        @pl.when(pl.program_id(0) == 0)
        def _():
            s_ref[...] = jnp.zeros_like(s_ref)

        y = jnp.dot(a_ref[...].astype(BF), w_ref[...], preferred_element_type=F32)
        y_ref[...] = y.astype(BF)
        r = lax.rsqrt(jnp.mean(y * y, axis=-1, keepdims=True) + RMS_EPS)
        e = (h_ref[...] + y * r * g_ref[...]) - t_ref[...]
        dh_ref[...] = e * (1.0 / D)
        s_ref[...] += jnp.sum(e * e)

    rows = pl.BlockSpec((tm, D), lambda i: (i, 0))
    return pl.pallas_call(
        body, name=name,
        grid=(T // tm,),
        in_specs=[
            pl.BlockSpec((tm, K), lambda i: (i, 0)),
            pl.BlockSpec((None, K, D), lambda i: (widx, 0, 0)),
            pl.BlockSpec((1, D), lambda i: (0, 0)),
            rows, rows,
        ],
        out_specs=[rows, rows, pl.BlockSpec((8, LANES), lambda i: (0, 0))],
        out_shape=[jax.ShapeDtypeStruct((T, D), BF), jax.ShapeDtypeStruct((T, D), F32),
                   jax.ShapeDtypeStruct((8, LANES), F32)],
        compiler_params=_cp("arbitrary"),
    )(a, w3, gain, h_old, target)


def _shift_down(u, k):
    row = lax.broadcasted_iota(jnp.int32, u.shape, 0)
    return jnp.where(row >= k, pltpu.roll(u, k, 0), 0.0)


def _shift_up(u, k):
    T = u.shape[0]
    row = lax.broadcasted_iota(jnp.int32, u.shape, 0)
    return jnp.where(row < T - k, pltpu.roll(u, T - k, 0), 0.0)


def _conv_fwd(z, cw, *, name):
    T = z.shape[0]
    D = z.shape[1] // 3
    tc = cw.shape[2]
    nb = D // tc

    def body(b_ref, c_ref, h_ref, w_ref, o_ref):
        u = c_ref[...].astype(F32) * h_ref[...].astype(F32)
        w = w_ref[...]
        conv = w[2:3] * u + w[1:2] * _shift_down(u, 1) + w[0:1] * _shift_down(u, 2)
        o_ref[...] = (b_ref[...].astype(F32) * conv).astype(BF)

    return pl.pallas_call(
        body, name=name,
        grid=(nb,),
        in_specs=[
            pl.BlockSpec((T, tc), lambda j: (0, j)),
            pl.BlockSpec((T, tc), lambda j: (0, nb + j)),
            pl.BlockSpec((T, tc), lambda j: (0, 2 * nb + j)),
            pl.BlockSpec((None, 8, tc), lambda j: (j, 0, 0)),
        ],
        out_specs=pl.BlockSpec((T, tc), lambda j: (0, j)),
        out_shape=jax.ShapeDtypeStruct((T, D), BF),
        compiler_params=_cp("parallel"),
    )(z, z, z, cw)


def _conv_bwd(z, cw, dv, *, name):
    T = z.shape[0]
    D = z.shape[1] // 3
    tc = LANES
    nb = D // tc
    per = cw.shape[2] // tc

    def body(b_ref, c_ref, h_ref, w_ref, dv_ref, dz_ref, dw_ref, stage, sems):
        j = pl.program_id(0)
        slot = j % 2

        def slab(p, jj, s):
            col = pl.multiple_of((p * nb + jj) * tc, tc)
            return pltpu.make_async_copy(stage.at[s, p], dz_ref.at[:, pl.ds(col, tc)], sems.at[s, p])

        @pl.when(j >= 2)
        def _():
            for p in range(3):
                slab(p, j - 2, slot).wait()

        c = c_ref[...].astype(F32)
        h = h_ref[...].astype(F32)
        u = c * h
        u1 = _shift_down(u, 1)
        u2 = _shift_down(u, 2)
        w = w_ref[...]
        dvv = dv_ref[...].astype(F32)
        dconv = dvv * b_ref[...].astype(F32)
        du = w[2:3] * dconv + w[1:2] * _shift_up(dconv, 1) + w[0:1] * _shift_up(dconv, 2)
        rows = lax.broadcasted_iota(jnp.int32, (8, tc), 0)
        dw_ref[...] = jnp.where(rows == 0, jnp.sum(dconv * u2, axis=0, keepdims=True),
                                jnp.where(rows == 1, jnp.sum(dconv * u1, axis=0, keepdims=True),
                                          jnp.where(rows == 2, jnp.sum(dconv * u, axis=0, keepdims=True), 0.0)))
        stage[slot, 0] = (dvv * (w[2:3] * u + w[1:2] * u1 + w[0:1] * u2)).astype(BF)
        stage[slot, 1] = (du * h).astype(BF)
        stage[slot, 2] = (du * c).astype(BF)
        for p in range(3):
            slab(p, j, slot).start()

        @pl.when(j == nb - 1)
        def _():
            for p in range(3):
                slab(p, j, slot).wait()
            if nb > 1:
                for p in range(3):
                    slab(p, j - 1, 1 - slot).wait()

    return pl.pallas_call(
        body, name=name,
        grid=(nb,),
        in_specs=[
            pl.BlockSpec((T, tc), lambda j: (0, j)),
            pl.BlockSpec((T, tc), lambda j: (0, nb + j)),
            pl.BlockSpec((T, tc), lambda j: (0, 2 * nb + j)),
            pl.BlockSpec((None, 8, tc), lambda j: (j // per, 0, j % per)),
            pl.BlockSpec((T, tc), lambda j: (0, j)),
        ],
        out_specs=[ANY, pl.BlockSpec((8, tc), lambda j: (0, j))],
        out_shape=[jax.ShapeDtypeStruct((T, 3 * D), BF), jax.ShapeDtypeStruct((8, D), F32)],
        scratch_shapes=[pltpu.VMEM((2, 3, T, tc), BF), pltpu.SemaphoreType.DMA((2, 3))],
        compiler_params=_cp("arbitrary"),
    )(z, z, z, cw, dv)


def _strided(base, count, d):
    return pl.ds(base, count, stride=d) if d > 1 else pl.ds(pl.multiple_of(base, BAND), count)


def _fill_band_bias(bias):
    qi = lax.broadcasted_iota(jnp.int32, (2 * BAND, 2 * BAND), 0) % BAND
    kj = lax.broadcasted_iota(jnp.int32, (2 * BAND, 2 * BAND), 1)
    dist = qi + BAND - kj
    band = (dist >= 0) & (dist <= BAND)
    bias[0] = jnp.where(band & (kj >= BAND), 0.0, NEG_INF)
    bias[1] = jnp.where(band, 0.0, NEG_INF)


def _attn_block_rows(T):
    return min(ATTN_BLOCK_ROWS, T)


def _head_mask():
    lane = lax.broadcasted_iota(jnp.int32, (2 * BAND, LANES), 1)
    row = lax.broadcasted_iota(jnp.int32, (2 * BAND, LANES), 0)
    return (lane < HEAD_DIM) == (row < BAND)


def _attn_fwd(q_all, kv_all, *, name):
    T = q_all.shape[0]
    NB = len(BRANCHES)
    Dm = q_all.shape[1] // NB
    HP = Dm // LANES
    R = _attn_block_rows(T)
    units = R // BAND
    dmax = max(d for _, d in BRANCHES)

    def body(*refs):
        ins = refs[:5 * NB]
        o_ref, l_ref, qbuf, kbuf, vbuf, o_s, l_s, bias = refs[5 * NB:]
        n = pl.program_id(0)
        pl.when((n == 0) & (pl.program_id(1) == 0))(lambda: _fill_band_bias(bias))
        hm = _head_mask()
        low = lax.broadcasted_iota(jnp.int32, (BAND, LANES), 1) < HEAD_DIM

        for g, (_, d) in enumerate(BRANCHES):
            q_ref, kp_ref, kc_ref, vp_ref, vc_ref = ins[5 * g:5 * g + 5]
            pr = BAND * d
            qbuf[...] = q_ref[...].astype(F32)
            kbuf[0:pr, :] = kp_ref[...].astype(F32)
            kbuf[pr:pr + R, :] = kc_ref[...].astype(F32)
            vbuf[0:pr, :] = vp_ref[...].astype(F32)
            vbuf[pr:pr + R, :] = vc_ref[...].astype(F32)

            def unit(u, carry, g=g, d=d, pr=pr):
                sub = u // d
                base = sub * pr + (u - sub * d)
                q = qbuf[_strided(base, BAND, d), :]
                q2 = jnp.where(hm, jnp.concatenate([q, q], axis=0), 0.0).astype(BF)
                k2 = kbuf[_strided(base, 2 * BAND, d), :].astype(BF)
                v2 = vbuf[_strided(base, 2 * BAND, d), :].astype(BF)
                s = lax.dot_general(q2, k2, (((1,), (1,)), ((), ())), preferred_element_type=F32)
                s = s + bias[((n > 0) | (sub > 0)).astype(jnp.int32)]
                m = jnp.max(s, axis=-1, keepdims=True)
                p = jnp.exp(s - m)
                l = jnp.sum(p, axis=-1, keepdims=True)
                pv = jnp.dot(p.astype(BF), v2, preferred_element_type=F32) * (1.0 / l)
                lse = m + jnp.log(l)
                o_s[g, _strided(base, BAND, d), :] = jnp.where(low, pv[:BAND], pv[BAND:])
                l_s[g, _strided(base, BAND, d), :] = jnp.where(low, lse[:BAND], lse[BAND:])
                return carry

            lax.fori_loop(0, units, unit, 0, unroll=min(ATTN_FWD_UNROLL, units))

        def merge(i, carry):
            sl = pl.ds(pl.multiple_of(i * BAND, BAND), BAND)
            lv = [l_s[g, sl, :] for g in range(NB)]
            m = functools.reduce(jnp.maximum, lv)
            e = [jnp.exp(v - m) for v in lv]
            tot = functools.reduce(jnp.add, e)
            inv = 1.0 / tot
            o_ref[sl, :] = functools.reduce(jnp.add, [(e[g] * inv) * o_s[g, sl, :] for g in range(NB)])
            l_ref[sl, :] = m + jnp.log(tot)
            return carry

        lax.fori_loop(0, units, merge, 0)

    in_specs, args = [], []
    for g, (_, d) in enumerate(BRANCHES):
        per = R // (BAND * d)
        for col, rows, idx in (
                (g * HP, R, lambda n, hp: n),
                (g * HP, BAND * d, lambda n, hp, per=per: jnp.maximum(n * per - 1, 0)),
                (g * HP, R, lambda n, hp: n),
                ((NB + g) * HP, BAND * d, lambda n, hp, per=per: jnp.maximum(n * per - 1, 0)),
                ((NB + g) * HP, R, lambda n, hp: n)):
            in_specs.append(pl.BlockSpec((rows, LANES), lambda n, hp, col=col, idx=idx: (idx(n, hp), col + hp)))
        args += [q_all, kv_all, kv_all, kv_all, kv_all]
    out = pl.BlockSpec((R, LANES), lambda n, hp: (n, hp))
    return pl.pallas_call(
        body, name=name,
        grid=(T // R, HP),
        in_specs=in_specs,
        out_specs=[out, out],
        out_shape=[jax.ShapeDtypeStruct((T, Dm), F32), jax.ShapeDtypeStruct((T, Dm), F32)],
        scratch_shapes=[pltpu.VMEM((R, LANES), F32),
                        pltpu.VMEM((BAND * dmax + R, LANES), F32), pltpu.VMEM((BAND * dmax + R, LANES), F32),
                        pltpu.VMEM((NB, R, LANES), F32), pltpu.VMEM((NB, R, LANES), F32),
                        pltpu.VMEM((2, 2 * BAND, 2 * BAND), F32)],
        compiler_params=_cp("arbitrary", "arbitrary"),
    )(*args)


def _attn_bwd(q_all, kv_all, do, o, lse, cos2, ss2, g, d, prev, *, name, dep=None):
    T = q_all.shape[0]
    NB = len(BRANCHES)
    Dm = q_all.shape[1] // NB
    HP = Dm // LANES
    R = _attn_block_rows(T)
    nblk = T // R
    units = R // BAND
    pr = BAND * d
    per = R // pr
    scale = HEAD_DIM ** -0.5

    def rope_bwd(t, cosv, ssv, first):
        return t * cosv - _rot_half(t, first) * ssv

    def body(q_ref, kp_ref, kc_ref, vp_ref, vc_ref, do_ref, o_ref, l_ref, cos_ref, ss_ref, *rest):
        dq_ref, dk_ref, dv_ref, qbuf, kbuf, vbuf, dq_s, dk_s, dv_s, pend_k, pend_v, bias = rest[-12:]
        i = pl.program_id(1)
        n = nblk - 1 - i
        pl.when((i == 0) & (pl.program_id(0) == 0))(lambda: _fill_band_bias(bias))
        hm = _head_mask()
        low = lax.broadcasted_iota(jnp.int32, (BAND, LANES), 1) < HEAD_DIM

        qbuf[...] = q_ref[...].astype(F32)
        kbuf[0:pr, :] = kp_ref[...].astype(F32)
        kbuf[pr:pr + R, :] = kc_ref[...].astype(F32)
        vbuf[0:pr, :] = vp_ref[...].astype(F32)
        vbuf[pr:pr + R, :] = vc_ref[...].astype(F32)

        @pl.when(i == 0)
        def _():
            pend_k[...] = jnp.zeros_like(pend_k)
            pend_v[...] = jnp.zeros_like(pend_v)

        def unit(u, carry):
            sub = per - 1 - u // d
            cls = u % d
            base = sub * pr + cls
            sl = _strided(base, BAND, d)
            sl2 = _strided(base, 2 * BAND, d)
            q = qbuf[sl, :]
            dov = do_ref[sl, :]
            ov = o_ref[sl, :]
            lv = l_ref[sl, :]
            q2 = jnp.where(hm, jnp.concatenate([q, q], axis=0), 0.0).astype(BF)
            do2 = jnp.where(hm, jnp.concatenate([dov, dov], axis=0), 0.0)
            oo = dov * ov
            delta = jnp.sum(jnp.where(hm, jnp.concatenate([oo, oo], axis=0), 0.0), axis=-1, keepdims=True)
            lse2 = jnp.concatenate([lv[:, 0:1], lv[:, HEAD_DIM:HEAD_DIM + 1]], axis=0)
            do2 = do2.astype(BF)
            k2 = kbuf[sl2, :].astype(BF)
            v2 = vbuf[sl2, :].astype(BF)
            s = lax.dot_general(q2, k2, (((1,), (1,)), ((), ())), preferred_element_type=F32)
            p = jnp.exp(s + bias[((n > 0) | (sub > 0)).astype(jnp.int32)] - lse2)
            dp = lax.dot_general(do2, v2, (((1,), (1,)), ((), ())), preferred_element_type=F32)
            ds = (p * (dp - delta)).astype(BF)
            dq2 = jnp.dot(ds, k2, preferred_element_type=F32)
            dq = jnp.where(low, dq2[:BAND], dq2[BAND:])
            dq_s[sl, :] = dq
            tn = (((0,), (0,)), ((), ()))
            dk2 = lax.dot_general(ds, q2, tn, preferred_element_type=F32)
            dv2 = lax.dot_general(p.astype(BF), do2, tn, preferred_element_type=F32)
            dk_s[sl, :] = dk2[BAND:] + pend_k[cls]
            dv_s[sl, :] = dv2[BAND:] + pend_v[cls]
            pend_k[cls] = dk2[:BAND]
            pend_v[cls] = dv2[:BAND]
            return carry

        lax.fori_loop(0, units, unit, 0, unroll=min(ATTN_BWD_UNROLL, units))

        whole = _first_half_mask(R)
        dq_ref[...] = (rope_bwd(dq_s[...], cos_ref[...], ss_ref[...], whole) * scale).astype(BF)
        dk_ref[...] = rope_bwd(dk_s[...], cos_ref[...], ss_ref[...], whole).astype(BF)
        dv_ref[...] = dv_s[...].astype(BF)

    blk = (R, LANES)
    pblk = (pr, LANES)
    cur = lambda hp, i: nblk - 1 - i
    prv = lambda hp, i: jnp.maximum((nblk - 1 - i) * per - 1, 0)
    in_specs = [
        pl.BlockSpec(blk, lambda hp, i: (cur(hp, i), g * HP + hp)),
        pl.BlockSpec(pblk, lambda hp, i: (prv(hp, i), g * HP + hp)),
        pl.BlockSpec(blk, lambda hp, i: (cur(hp, i), g * HP + hp)),
        pl.BlockSpec(pblk, lambda hp, i: (prv(hp, i), (NB + g) * HP + hp)),
        pl.BlockSpec(blk, lambda hp, i: (cur(hp, i), (NB + g) * HP + hp)),
        pl.BlockSpec(blk, lambda hp, i: (cur(hp, i), hp)),
        pl.BlockSpec(blk, lambda hp, i: (cur(hp, i), hp)),
        pl.BlockSpec(blk, lambda hp, i: (cur(hp, i), hp)),
        pl.BlockSpec(blk, lambda hp, i: (cur(hp, i), 0)),
        pl.BlockSpec(blk, lambda hp, i: (cur(hp, i), 0)),
    ]
    args = [q_all, kv_all, kv_all, kv_all, kv_all, do, o, lse, cos2, ss2]
    if dep is not None:
        in_specs.append(ANY)
        args.append(dep)
    aliases = {}
    if prev is not None:
        in_specs += [ANY, ANY, ANY]
        aliases = {len(args): 0, len(args) + 1: 1, len(args) + 2: 2}
        args += list(prev)
    wide = jax.ShapeDtypeStruct((T, NB * Dm), BF)
    out = pl.BlockSpec(blk, lambda hp, i: (cur(hp, i), g * HP + hp))
    return pl.pallas_call(
        body, name=name,
        grid=(HP, nblk),
        in_specs=in_specs,
        out_specs=[out, out, out],
        out_shape=[wide, wide, wide],
        scratch_shapes=[pltpu.VMEM(blk, F32), pltpu.VMEM((pr + R, LANES), F32), pltpu.VMEM((pr + R, LANES), F32),
                        pltpu.VMEM(blk, F32), pltpu.VMEM(blk, F32), pltpu.VMEM(blk, F32),
                        pltpu.VMEM((d, BAND, LANES), F32), pltpu.VMEM((d, BAND, LANES), F32),
                        pltpu.VMEM((2, 2 * BAND, 2 * BAND), F32)],
        input_output_aliases=aliases,
        compiler_params=_cp("arbitrary", "arbitrary"),
    )(*args)


def _postnorm_bwd(dh, y, g_ref_val):
    r = lax.rsqrt(jnp.mean(y * y, axis=-1, keepdims=True) + RMS_EPS)
    yn = y * r
    dyn = dh * g_ref_val
    dy = r * (dyn - yn * jnp.mean(dyn * yn, axis=-1, keepdims=True))
    return dy, yn


def _after(body, n_in, dep):
    if dep is None:
        return body
    return lambda *refs: body(*refs[:n_in], *refs[n_in + 1:])


def _dep_spec(dep):
    return [] if dep is None else [ANY]


def _dep_arg(dep):
    return [] if dep is None else [dep]


def _postnorm_bwd_matmul(dh, y, gain, w3, widx, *, name, da_dtype, dep=None):
    T, D = dh.shape
    K = w3.shape[1]
    tm = min(ROW_BLOCK_WIDE, T)

    def body(dh_ref, y_ref, g_ref, w_ref, dy_ref, da_ref, dg_ref):
        i = pl.program_id(0)

        @pl.when(i == 0)
        def _():
            dg_ref[...] = jnp.zeros_like(dg_ref)

        dhv = dh_ref[...]
        dy, yn = _postnorm_bwd(dhv, y_ref[...].astype(F32), g_ref[...])
        dg_ref[...] += jnp.sum(dhv * yn, axis=0, keepdims=True)
        dyb = dy.astype(BF)
        dy_ref[...] = dyb
        da = lax.dot_general(dyb, w_ref[...], (((1,), (1,)), ((), ())), preferred_element_type=F32)
        da_ref[...] = da.astype(da_dtype)

    return pl.pallas_call(
        _after(body, 4, dep), name=name,
        grid=(T // tm,),
        in_specs=[
            pl.BlockSpec((tm, D), lambda i: (i, 0)),
            pl.BlockSpec((tm, D), lambda i: (i, 0)),
            pl.BlockSpec((1, D), lambda i: (0, 0)),
            pl.BlockSpec((None, K, D), lambda i: (widx, 0, 0)),
        ] + _dep_spec(dep),
        out_specs=[pl.BlockSpec((tm, D), lambda i: (i, 0)),
                   pl.BlockSpec((tm, K), lambda i: (i, 0)),
                   pl.BlockSpec((1, D), lambda i: (0, 0))],
        out_shape=[jax.ShapeDtypeStruct((T, D), BF), jax.ShapeDtypeStruct((T, K), da_dtype),
                   jax.ShapeDtypeStruct((1, D), F32)],
        compiler_params=_cp("arbitrary"),
    )(dh, y, gain, w3, *_dep_arg(dep))


def _postnorm_bwd_swiglu(dh, y, gain, wd3, layer, s, ut, *, name, dep=None):
    T, D = dh.shape
    F = wd3.shape[1]
    tm = min(ROW_BLOCK, T)

    def body(dh_ref, y_ref, g_ref, w_ref, s_ref, ut_ref, dy_ref, dgo_ref, duo_ref, dgain_ref):
        @pl.when(pl.program_id(0) == 0)
        def _():
            dgain_ref[...] = jnp.zeros_like(dgain_ref)

        dhv = dh_ref[...]
        dy, yn = _postnorm_bwd(dhv, y_ref[...].astype(F32), g_ref[...])
        dgain_ref[...] += jnp.sum(dhv * yn, axis=0, keepdims=True)
        dyb = dy.astype(BF)
        dy_ref[...] = dyb
        da = lax.dot_general(dyb, w_ref[...], (((1,), (1,)), ((), ())), preferred_element_type=F32)
        dgo_ref[...] = (da * ut_ref[...].astype(F32)).astype(BF)
        duo_ref[...] = (da * s_ref[...].astype(F32)).astype(BF)

    rows = pl.BlockSpec((tm, D), lambda i: (i, 0))
    wide = pl.BlockSpec((tm, F), lambda i: (i, 0))
    act = jax.ShapeDtypeStruct((T, F), BF)
    return pl.pallas_call(
        _after(body, 6, dep), name=name,
        grid=(T // tm,),
        in_specs=[
            rows, rows,
            pl.BlockSpec((1, D), lambda i: (0, 0)),
            pl.BlockSpec((None, F, D), lambda i: (layer, 0, 0), pipeline_mode=pl.Buffered(1)),
            wide, wide,
        ] + _dep_spec(dep),
        out_specs=[rows, wide, wide, pl.BlockSpec((1, D), lambda i: (0, 0))],
        out_shape=[jax.ShapeDtypeStruct((T, D), BF), act, act, jax.ShapeDtypeStruct((1, D), F32)],
        compiler_params=_cp("arbitrary"),
    )(dh, y, gain, wd3, s, ut, *_dep_arg(dep))


def _matmul_prenorm_bwd(dzs, wg, layer, h, gain, dh_in, *, name):
    T, D = h.shape
    n = wg.shape[2]
    tm = min(ROW_BLOCK, T)
    per = N_CHIPS // len(dzs)

    def body(*refs):
        dz_refs = refs[:len(dzs)]
        w_ref, h_ref, g_ref, dhi_ref, dh_ref, dg_ref = refs[len(dzs):]

        @pl.when(pl.program_id(0) == 0)
        def _():
            dg_ref[...] = jnp.zeros_like(dg_ref)

        dhn = None
        for j in range(N_CHIPS):
            dz = dz_refs[j // per][:, (j % per) * n:(j % per + 1) * n]
            t = lax.dot_general(dz.astype(BF), w_ref[j], (((1,), (1,)), ((), ())), preferred_element_type=F32)
            dhn = t if dhn is None else dhn + t
        hv = h_ref[...]
        r = lax.rsqrt(jnp.mean(hv * hv, axis=-1, keepdims=True) + RMS_EPS)
        xh = hv * r
        dg_ref[...] += jnp.sum(dhn * xh, axis=0, keepdims=True)
        dxn = dhn * g_ref[...]
        dh_ref[...] = dhi_ref[...] + r * (dxn - xh * jnp.mean(dxn * xh, axis=-1, keepdims=True))

    rows = pl.BlockSpec((tm, D), lambda i: (i, 0))
    in_specs = [pl.BlockSpec((tm, per * n), lambda i: (i, 0)) for _ in dzs]
    in_specs += [pl.BlockSpec((N_CHIPS, D, n), lambda i: (0, layer, 0), pipeline_mode=pl.Buffered(1)),
                 rows, pl.BlockSpec((1, D), lambda i: (0, 0)), rows]
    return pl.pallas_call(
        body, name=name,
        grid=(T // tm,),
        in_specs=in_specs,
        out_specs=[rows, pl.BlockSpec((1, D), lambda i: (0, 0))],
        out_shape=[jax.ShapeDtypeStruct((T, D), F32), jax.ShapeDtypeStruct((1, D), F32)],
        compiler_params=_cp("arbitrary"),
    )(*dzs, wg, h, gain, dh_in)


def _grad_matmul(a, b, out_shape3, tme, tne, out_index, prev, *, name, dep=None):
    T, M = a.shape
    N = b.shape[1]
    tk = min(GRAD_CHUNK, T)
    nk = T // tk

    def body(a_ref, b_ref, *rest):
        o_ref, acc = rest[-2:]
        k = pl.program_id(2)
        part = jnp.dot(a_ref[...].astype(BF).T, b_ref[...].astype(BF), preferred_element_type=F32)

        @pl.when(k == 0)
        def _():
            acc[...] = part

        @pl.when(k > 0)
        def _():
            acc[...] += part

        @pl.when(k == nk - 1)
        def _():
            o_ref[...] = acc[...].astype(BF)

    in_specs = [pl.BlockSpec((tk, tme), lambda i, j, k: (k, i)),
                pl.BlockSpec((tk, tne), lambda i, j, k: (k, j))]
    args = [a, b]
    aliases = {}
    if prev is not None:
        in_specs.append(ANY)
        args.append(prev)
        aliases = {2: 0}
    in_specs += _dep_spec(dep)
    args += _dep_arg(dep)
    return pl.pallas_call(
        body, name=name,
        grid=(M // tme, N // tne, nk),
        in_specs=in_specs,
        out_specs=pl.BlockSpec((None, tme, tne), lambda i, j, k: out_index(i, j)),
        out_shape=jax.ShapeDtypeStruct(out_shape3, BF),
        scratch_shapes=[pltpu.VMEM((tme, tne), F32)],
        input_output_aliases=aliases,
        compiler_params=_cp("parallel", "parallel", "arbitrary"),
    )(*args)


def _row_tile(R, cap=512):
    fit = [t for t in range(16, min(R, cap) + 1, 16) if R % t == 0]
    return max(fit) if fit else R


def _cast_place(w3, layer, where, dtype, *, name, dep=None):
    _, R, C = w3.shape
    tr = _row_tile(R)

    def body(s_ref, w_ref, o_ref):
        o_ref[...] = w_ref[...].astype(o_ref.dtype)

    return pl.pallas_call(
        _after(body, 2, dep), name=name,
        grid_spec=pltpu.PrefetchScalarGridSpec(
            num_scalar_prefetch=1, grid=(R // tr,),
            in_specs=[pl.BlockSpec((None, tr, C), lambda i, s: (layer, i, 0))] + _dep_spec(dep),
            out_specs=pl.BlockSpec((None, tr, C), lambda i, s: (s[0], i, 0))),
        out_shape=jax.ShapeDtypeStruct((N_CHIPS, R, C), dtype),
        compiler_params=_cp("arbitrary"),
    )(where, w3, *_dep_arg(dep))


def _pair_sum(dw, theirs, where, *, name):
    G, rh, C = theirs.shape
    tr = _row_tile(rh)
    nr = rh // tr

    def body(s_ref, a_ref, b_ref, o_ref):
        o_ref[...] = (a_ref[...].astype(F32) + b_ref[...].astype(F32)).astype(BF)

    mine = pl.BlockSpec((None, tr, C), lambda g, i, s: (g, s[1] * nr + i, 0))
    spec = pl.BlockSpec((None, tr, C), lambda g, i, s: (g, i, 0))
    return pl.pallas_call(
        body, name=name,
        grid_spec=pltpu.PrefetchScalarGridSpec(
            num_scalar_prefetch=1, grid=(G, nr), in_specs=[mine, spec], out_specs=spec),
        out_shape=jax.ShapeDtypeStruct((G, rh, C), BF),
        compiler_params=_cp("arbitrary", "arbitrary"),
    )(where, dw, theirs)


def _chip_sum(landed, parts, where, total_rows, row_off, prev, *, name):
    G, rh, C = landed.shape
    tr = _row_tile(rh)
    nr = rh // tr
    base = row_off // tr

    def body(s_ref, l_ref, p_ref, *rest):
        o_ref = rest[-1]
        for j in range(G):
            def own(j=j):
                v = p_ref[...].astype(F32)
                o_ref[...] = v if j == 0 else o_ref[...] + v

            def other(j=j):
                v = l_ref[j].astype(F32)
                o_ref[...] = v if j == 0 else o_ref[...] + v

            pl.when(s_ref[0] == j)(own)
            pl.when(s_ref[0] != j)(other)

    in_specs = [pl.BlockSpec((G, tr, C), lambda i, s: (0, i, 0)),
                pl.BlockSpec((None, tr, C), lambda i, s: (s[0], i, 0))]
    args = [where, landed, parts]
    aliases = {}
    if prev is not None:
        in_specs.append(ANY)
        args.append(prev)
        aliases = {3: 0}
    return pl.pallas_call(
        body, name=name,
        grid_spec=pltpu.PrefetchScalarGridSpec(
            num_scalar_prefetch=1, grid=(nr,),
            in_specs=in_specs,
            out_specs=pl.BlockSpec((tr, C), lambda i, s: (base + s[1] * nr + i, 0))),
        out_shape=jax.ShapeDtypeStruct((total_rows, C), F32),
        input_output_aliases=aliases,
        compiler_params=_cp("arbitrary"),
    )(*args)


def _adamw(w, g, m, v, *, name, emit_grad=False):
    R, C = w.shape
    tr = _row_tile(R, cap=max(16, ADAMW_BLOCK_BYTES // (4 * C)))
    n_out = 4 if emit_grad else 3

    def body(w_ref, g_ref, m_ref, v_ref, d_ref, mo_ref, vo_ref, *go_ref):
        gv = g_ref[...]
        if emit_grad:
            go_ref[0][...] = gv
        mn = ADAM_B1 * m_ref[...] + (1.0 - ADAM_B1) * gv
        vn = ADAM_B2 * v_ref[...] + (1.0 - ADAM_B2) * jnp.square(gv)
        m_hat = mn / (1.0 - ADAM_B1 ** ADAM_STEP)
        v_hat = vn / (1.0 - ADAM_B2 ** ADAM_STEP)
        d_ref[...] = -ADAM_LR * (m_hat / (jnp.sqrt(v_hat) + ADAM_EPS) + ADAM_WD * w_ref[...])
        mo_ref[...] = mn
        vo_ref[...] = vn

    spec = pl.BlockSpec((tr, C), lambda i: (i, 0))
    shp = jax.ShapeDtypeStruct((R, C), F32)
    return pl.pallas_call(
        body, name=name, grid=(R // tr,), in_specs=[spec] * 4, out_specs=[spec] * n_out,
        out_shape=[shp] * n_out, compiler_params=_cp("parallel"),
    )(w, g, m, v)


def _place():
    x = lax.axis_index("x")
    y = lax.axis_index("y")
    c = lax.axis_index("c")
    chips = [(1 - x, y), (x, 1 - y), (1 - x, 1 - y)]
    return x, y, c, chips


def _chunk_rows(rows, row_bytes, align):
    if rows <= align:
        return rows
    cands = [r for r in range(align, rows + 1, align) if rows % r == 0]
    fit = [r for r in cands if r * row_bytes <= DMA_CHUNK_BYTES]
    return max(fit) if fit else min(cands)


def _row_align(dtype):
    return 8 * (4 // jnp.dtype(dtype).itemsize)


def _start_chunks(make, rows, rc):
    for r0 in range(0, rows, rc):
        make(r0, rc).start()


def _piece_rows(ref, piece, j, h, r0=0, n=None):
    _, lead, off, rows = piece
    rh = rows // 2
    n = rh if n is None else n
    if lead is not None:
        return ref.at[lead, j, pl.ds(h * rh + r0, n)]
    return ref.at[j, pl.ds(off + h * rh + r0, n)]


def _piece_chunk(arr, piece):
    rh = piece[3] // 2
    return rh, _chunk_rows(rh, arr.shape[-1] * arr.dtype.itemsize, _row_align(arr.dtype))


def _ag_start(arrays, taps, groups, *, name):
    na = len(arrays)
    ng = len(groups)
    nt = 0 if taps is None else 1
    n_sem = [3 * len(grp) + (3 if nt and g == 0 else 0) for g, grp in enumerate(groups)]

    def body(*refs):
        ins = refs[:na]
        taps_ref = refs[na] if nt else None
        sems = refs[na + nt:na + nt + 2 * ng]
        token = refs[-1]
        token[...] = jnp.zeros_like(token)
        x, y, c, chips = _place()
        myj = 2 * x + y
        for g, grp in enumerate(groups):
            ssem, rsem = sems[2 * g], sems[2 * g + 1]
            for idx, piece in enumerate(grp):
                ref = ins[piece[0]]
                rh, rc = _piece_chunk(arrays[piece[0]], piece)
                for k, (px, py) in enumerate(chips):
                    def send(r0, n, ref=ref, piece=piece, idx=idx, k=k, px=px, py=py, ssem=ssem, rsem=rsem):
                        part = _piece_rows(ref, piece, myj, c, r0, n)
                        return pltpu.make_async_remote_copy(
                            src_ref=part, dst_ref=part, send_sem=ssem.at[3 * idx + k], recv_sem=rsem.at[3 * idx + k],
                            device_id=(px, py, c), device_id_type=MESH)
                    _start_chunks(send, rh, rc)
            if nt and g == 0:
                for k, (px, py) in enumerate(chips):
                    pltpu.make_async_remote_copy(
                        src_ref=taps_ref.at[myj], dst_ref=taps_ref.at[myj],
                        send_sem=ssem.at[3 * len(grp) + k], recv_sem=rsem.at[3 * len(grp) + k],
                        device_id=(px, py, c), device_id_type=MESH).start()

    sem_shapes = []
    for n in n_sem:
        sem_shapes += [pltpu.SemaphoreType.DMA((n,)), pltpu.SemaphoreType.DMA((n,))]
    ops = list(arrays) + ([taps] if nt else [])
    bufs = [pltpu.HBM(a.shape, a.dtype) for a in ops]
    outs = pl.pallas_call(
        body, name=name,
        out_shape=(*sem_shapes, *bufs, jax.ShapeDtypeStruct((8, LANES), F32)),
        in_specs=[HBM] * (na + nt),
        out_specs=(*([SEM] * (2 * ng)), *([HBM] * (na + nt)), pl.BlockSpec(memory_space=pltpu.VMEM)),
        input_output_aliases={i: 2 * ng + i for i in range(na + nt)},
        compiler_params=pltpu.CompilerParams(has_side_effects=EFFECT),
    )(*[_in_hbm(a) for a in ops])
    sems = [(outs[2 * g], outs[2 * g + 1]) for g in range(ng)]
    return sems, list(outs[2 * ng:2 * ng + na]), (outs[2 * ng + na] if nt else None), outs[-1]


def _ag_wait(sems, vals, taps, group, after, *, name):
    nv = len(vals)
    extra = ([taps] if taps is not None else [])
    nb = nv + len(extra)

    def body(*refs):
        bufs = refs[:nb]
        ssem, rsem = refs[nb], refs[nb + 1]
        x, y, c, chips = _place()
        for idx, piece in enumerate(group):
            for k, (px, py) in enumerate(chips):
                got = _piece_rows(bufs[piece[0]], piece, 2 * px + py, c)
                cp = pltpu.make_async_remote_copy(
                    src_ref=got, dst_ref=got, send_sem=ssem.at[3 * idx + k], recv_sem=rsem.at[3 * idx + k],
                    device_id=(px, py, c), device_id_type=MESH)
                cp.wait_send()
                cp.wait_recv()
        if taps is not None:
            for k, (px, py) in enumerate(chips):
                got = bufs[nv].at[2 * px + py]
                cp = pltpu.make_async_remote_copy(
                    src_ref=got, dst_ref=got, send_sem=ssem.at[3 * len(group) + k],
                    recv_sem=rsem.at[3 * len(group) + k], device_id=(px, py, c), device_id_type=MESH)
                cp.wait_send()
                cp.wait_recv()

    ops = list(vals) + extra
    shapes = [pltpu.HBM(a.shape, a.dtype) for a in ops]
    outs = pl.pallas_call(
        body, name=name,
        out_shape=tuple(shapes),
        in_specs=[HBM] * nb + [SEM, SEM] + _dep_spec(after),
        out_specs=[HBM] * nb,
        input_output_aliases={i: i for i in range(nb)},
        compiler_params=pltpu.CompilerParams(has_side_effects=EFFECT),
    )(*ops, sems[0], sems[1], *_dep_arg(after))
    return list(outs[:nv]), (outs[nv] if taps is not None else None)


def _ag_forward(vals, group, *, name):
    nv = len(vals)
    npc = len(group)

    def body(*refs):
        bufs = refs[nv:2 * nv]
        fsem, gsem = refs[2 * nv:]
        x, y, c, chips = _place()
        sib = (x, y, 1 - c)
        sent = []
        for idx, piece in enumerate(group):
            rh, rc = _piece_chunk(vals[piece[0]], piece)
            for k, (px, py) in enumerate(chips):
                def fwd(r0, n, piece=piece, idx=idx, k=k, pj=2 * px + py):
                    part = _piece_rows(bufs[piece[0]], piece, pj, c, r0, n)
                    return pltpu.make_async_remote_copy(
                        src_ref=part, dst_ref=part, send_sem=fsem.at[3 * idx + k], recv_sem=gsem.at[3 * idx + k],
                        device_id=sib, device_id_type=MESH)
                _start_chunks(fwd, rh, rc)
                sent.append(fwd(0, rh))
        for idx, piece in enumerate(group):
            for k, (px, py) in enumerate(chips):
                theirs = _piece_rows(bufs[piece[0]], piece, 2 * px + py, 1 - c)
                pltpu.make_async_remote_copy(
                    src_ref=theirs, dst_ref=theirs, send_sem=fsem.at[3 * idx + k], recv_sem=gsem.at[3 * idx + k],
                    device_id=sib, device_id_type=MESH).wait_recv()
        for cp in sent:
            cp.wait_send()

    return pl.pallas_call(
        body, name=name,
        in_specs=[ANY] * nv, out_specs=[ANY] * nv,
        out_shape=[jax.ShapeDtypeStruct(v.shape, v.dtype) for v in vals],
        input_output_aliases={i: i for i in range(nv)},
        scratch_shapes=[pltpu.SemaphoreType.DMA((3 * npc,)), pltpu.SemaphoreType.DMA((3 * npc,))],
    )(*vals)


def _sibling_swap(dws, *, name):
    nm = len(dws)
    shapes = [jax.ShapeDtypeStruct((dw.shape[0], dw.shape[1] // 2, dw.shape[2]), dw.dtype) for dw in dws]

    def body(*refs):
        ins = refs[:nm]
        theirs = refs[nm:2 * nm]
        ssem, rsem = refs[2 * nm:]
        x, y, c, _ = _place()
        sib = (x, y, 1 - c)
        cps = []
        for m in range(nm):
            G, rh, cols = shapes[m].shape
            rc = _chunk_rows(rh, cols * shapes[m].dtype.itemsize, _row_align(shapes[m].dtype))
            for j in range(G):
                _start_chunks(lambda r0, n, m=m, j=j, rh=rh: pltpu.make_async_remote_copy(
                    src_ref=ins[m].at[j, pl.ds((1 - c) * rh + r0, n)],
                    dst_ref=theirs[m].at[j, pl.ds(r0, n)], send_sem=ssem.at[m], recv_sem=rsem.at[m],
                    device_id=sib, device_id_type=MESH), rh, rc)
            cps.append(pltpu.make_async_remote_copy(
                src_ref=ins[m].at[:, pl.ds((1 - c) * rh, rh), :], dst_ref=theirs[m],
                send_sem=ssem.at[m], recv_sem=rsem.at[m], device_id=sib, device_id_type=MESH))
        for cp in cps:
            cp.wait()

    return pl.pallas_call(
        body, name=name,
        in_specs=[ANY] * nm, out_specs=[ANY] * nm, out_shape=shapes,
        scratch_shapes=[pltpu.SemaphoreType.DMA((nm,)), pltpu.SemaphoreType.DMA((nm,))],
    )(*dws)


HBM = pl.BlockSpec(memory_space=pltpu.HBM)
SEM = pl.BlockSpec(memory_space=pltpu.SEMAPHORE)
EFFECT = pltpu.SideEffectType.DATAFLOW_SIDE_EFFECTING


def _in_hbm(a):
    return pltpu.with_memory_space_constraint(a, pltpu.HBM)


def _swap_start(dws, *, name):
    nm = len(dws)
    lands = [(dw.shape[0], dw.shape[1] // 2, dw.shape[2]) for dw in dws]

    def body(*refs):
        ins = refs[:nm]
        lnd = refs[nm:2 * nm]
        ssem, rsem = refs[2 * nm:2 * nm + 2]
        token = refs[-1]
        x, y, c, _ = _place()
        sib = (x, y, 1 - c)
        for m in range(nm):
            G, rh, cols = lands[m]
            rc = _chunk_rows(rh, cols * dws[m].dtype.itemsize, _row_align(dws[m].dtype))
            for j in range(G):
                _start_chunks(lambda r0, n, m=m, j=j, rh=rh: pltpu.make_async_remote_copy(
                    src_ref=ins[m].at[j, pl.ds((1 - c) * rh + r0, n)], dst_ref=lnd[m].at[j, pl.ds(r0, n)],
                    send_sem=ssem.at[m], recv_sem=rsem.at[m], device_id=sib, device_id_type=MESH), rh, rc)
        token[...] = jnp.zeros_like(token)

    src = [pltpu.HBM(dw.shape, dw.dtype) for dw in dws]
    dst = [pltpu.HBM(s, dw.dtype) for s, dw in zip(lands, dws)]
    outs = pl.pallas_call(
        body, name=name,
        out_shape=(pltpu.SemaphoreType.DMA((nm,)), pltpu.SemaphoreType.DMA((nm,)), *src, *dst,
                   jax.ShapeDtypeStruct((8, LANES), F32)),
        in_specs=[HBM] * (2 * nm),
        out_specs=(SEM, SEM, *([HBM] * (2 * nm)), pl.BlockSpec(memory_space=pltpu.VMEM)),
        input_output_aliases={i: 2 + i for i in range(2 * nm)},
        compiler_params=pltpu.CompilerParams(has_side_effects=EFFECT),
    )(*[_in_hbm(dw) for dw in dws], *[_in_hbm(lax.empty(s, dw.dtype)) for s, dw in zip(lands, dws)])
    return (outs[0], outs[1], list(outs[2:2 + nm]), list(outs[2 + nm:2 + 2 * nm])), outs[-1]


def _swap_wait(handle, after, *, name):
    ssem_in, rsem_in, dws, lands = handle
    nm = len(dws)

    def body(*refs):
        ins = refs[:nm]
        lnd = refs[nm:2 * nm]
        ssem, rsem = refs[2 * nm:2 * nm + 2]
        x, y, c, _ = _place()
        for m in range(nm):
            rh = lands[m].shape[1]
            cp = pltpu.make_async_remote_copy(
                src_ref=ins[m].at[:, pl.ds((1 - c) * rh, rh), :], dst_ref=lnd[m],
                send_sem=ssem.at[m], recv_sem=rsem.at[m], device_id=(x, y, 1 - c), device_id_type=MESH)
            cp.wait_send()
            cp.wait_recv()

    bufs = [pltpu.HBM(a.shape, a.dtype) for a in list(dws) + list(lands)]
    outs = pl.pallas_call(
        body, name=name,
        out_shape=tuple(bufs),
        in_specs=[HBM] * (2 * nm) + [SEM, SEM, ANY],
        out_specs=[HBM] * (2 * nm),
        input_output_aliases={i: i for i in range(2 * nm)},
        compiler_params=pltpu.CompilerParams(has_side_effects=EFFECT),
    )(*dws, *lands, ssem_in, rsem_in, after)
    return list(outs[:nm]), list(outs[nm:])


def _exchange_start(parts, *, name):
    nm = len(parts)

    def body(*refs):
        ins = refs[:nm]
        lands = refs[nm:2 * nm]
        ssem, rsem = refs[2 * nm:2 * nm + 2]
        token = refs[-1]
        x, y, c, chips = _place()
        myj = 2 * x + y
        for m in range(nm):
            _, rh, cols = parts[m].shape
            rc = _chunk_rows(rh, cols * parts[m].dtype.itemsize, _row_align(parts[m].dtype))
            for k, (px, py) in enumerate(chips):
                _start_chunks(lambda r0, n, m=m, k=k, px=px, py=py: pltpu.make_async_remote_copy(
                    src_ref=ins[m].at[2 * px + py, pl.ds(r0, n)], dst_ref=lands[m].at[myj, pl.ds(r0, n)],
                    send_sem=ssem.at[3 * m + k], recv_sem=rsem.at[3 * m + k],
                    device_id=(px, py, c), device_id_type=MESH), rh, rc)
        token[...] = jnp.zeros_like(token)

    bufs = [pltpu.HBM(p.shape, p.dtype) for p in parts]
    outs = pl.pallas_call(
        body, name=name,
        out_shape=(pltpu.SemaphoreType.DMA((3 * nm,)), pltpu.SemaphoreType.DMA((3 * nm,)), *bufs, *bufs,
                   jax.ShapeDtypeStruct((8, LANES), F32)),
        in_specs=[HBM] * (2 * nm),
        out_specs=(SEM, SEM, *([HBM] * (2 * nm)), pl.BlockSpec(memory_space=pltpu.VMEM)),
        input_output_aliases={i: 2 + i for i in range(2 * nm)},
        compiler_params=pltpu.CompilerParams(has_side_effects=EFFECT),
    )(*[_in_hbm(p) for p in parts], *[_in_hbm(lax.empty(p.shape, p.dtype)) for p in parts])
    return (outs[0], outs[1], list(outs[2:2 + nm]), list(outs[2 + nm:2 + 2 * nm])), outs[-1]


def _exchange_wait(handle, after, *, name):
    ssem_in, rsem_in, parts, lands = handle
    nm = len(parts)

    def body(*refs):
        ins = refs[:nm]
        lnd = refs[nm:2 * nm]
        ssem, rsem = refs[2 * nm:2 * nm + 2]
        x, y, c, chips = _place()
        for m in range(nm):
            for k, (px, py) in enumerate(chips):
                pj = 2 * px + py
                cp = pltpu.make_async_remote_copy(
                    src_ref=ins[m].at[pj], dst_ref=lnd[m].at[pj],
                    send_sem=ssem.at[3 * m + k], recv_sem=rsem.at[3 * m + k],
                    device_id=(px, py, c), device_id_type=MESH)
                cp.wait_send()
                cp.wait_recv()

    bufs = [pltpu.HBM(p.shape, p.dtype) for p in parts]
    outs = pl.pallas_call(
        body, name=name,
        out_shape=(*bufs, *bufs),
        in_specs=[HBM] * (2 * nm) + [SEM, SEM, ANY],
        out_specs=[HBM] * (2 * nm),
        input_output_aliases={i: i for i in range(2 * nm)},
        compiler_params=pltpu.CompilerParams(has_side_effects=EFFECT),
    )(*parts, *lands, ssem_in, rsem_in, after)
    return list(outs[nm:]), list(outs[:nm])


def _sibling_join(grads, regions, *, name):
    nm = len(grads)
    nr = len(regions)
    shapes = [jax.ShapeDtypeStruct(g.shape, g.dtype) for g in grads]

    def body(*refs):
        outs = refs[nm:2 * nm]
        ssem, rsem = refs[2 * nm:]
        x, y, c, _ = _place()
        sib = (x, y, 1 - c)
        cps = []
        for i, (m, off, rows) in enumerate(regions):
            rh, cols = rows // 2, grads[m].shape[1]
            rc = _chunk_rows(rh, cols * grads[m].dtype.itemsize, _row_align(grads[m].dtype))

            def send(r0, n, i=i, m=m, off=off, rh=rh):
                part = outs[m].at[pl.ds(off + c * rh + r0, n)]
                return pltpu.make_async_remote_copy(
                    src_ref=part, dst_ref=part, send_sem=ssem.at[i], recv_sem=rsem.at[i],
                    device_id=sib, device_id_type=MESH)
            _start_chunks(send, rh, rc)
            cps.append(send(0, rh))
        for i, (m, off, rows) in enumerate(regions):
            rh = rows // 2
            cps[i].wait_send()
            got = outs[m].at[pl.ds(off + (1 - c) * rh, rh)]
            pltpu.make_async_remote_copy(
                src_ref=got, dst_ref=got, send_sem=ssem.at[i], recv_sem=rsem.at[i],
                device_id=sib, device_id_type=MESH).wait_recv()

    return pl.pallas_call(
        body, name=name,
        in_specs=[ANY] * nm, out_specs=[ANY] * nm, out_shape=shapes,
        input_output_aliases={i: i for i in range(nm)},
        scratch_shapes=[pltpu.SemaphoreType.DMA((nr,)), pltpu.SemaphoreType.DMA((nr,))],
    )(*grads)


def _all_reduce_small(pack):
    R, C = pack.shape

    def body(in_ref, out_ref, slots, ssem, rsem):
        x, y, c, _ = _place()
        me = 4 * x + 2 * y + c
        slots[me] = in_ref[...]
        cps = []
        for k in range(1, N_DEV):
            dx, dy, dc = (k >> 2) & 1, (k >> 1) & 1, k & 1
            peer = (x ^ dx, y ^ dy, c ^ dc)
            cp = pltpu.make_async_remote_copy(
                src_ref=in_ref, dst_ref=slots.at[me], send_sem=ssem.at[k], recv_sem=rsem.at[k],
                device_id=peer, device_id_type=MESH)
            cp.start()
            cps.append(cp)
        for k in range(1, N_DEV):
            dx, dy, dc = (k >> 2) & 1, (k >> 1) & 1, k & 1
            got = slots.at[4 * (x ^ dx) + 2 * (y ^ dy) + (c ^ dc)]
            pltpu.make_async_remote_copy(
                src_ref=got, dst_ref=got, send_sem=ssem.at[k], recv_sem=rsem.at[k],
                device_id=(x ^ dx, y ^ dy, c ^ dc), device_id_type=MESH).wait_recv()
        for cp in cps:
            cp.wait_send()
        acc = slots[0]
        for s in range(1, N_DEV):
            acc = acc + slots[s]
        out_ref[...] = acc

    return pl.pallas_call(
        body, name="ar_small",
        in_specs=[pl.BlockSpec(memory_space=pltpu.VMEM)],
        out_specs=pl.BlockSpec(memory_space=pltpu.VMEM),
        out_shape=jax.ShapeDtypeStruct((R, C), F32),
        scratch_shapes=[pltpu.VMEM((N_DEV, R, C), F32),
                        pltpu.SemaphoreType.DMA((N_DEV,)), pltpu.SemaphoreType.DMA((N_DEV,))],
    )(pack)


def kernel(x, positions, mix_norm_pre, mix_norm_post, ffn_norm_pre, ffn_norm_post, ffn_w_gate_up, ffn_w_down, conv_w_in, conv_w, conv_w_out, kv_norm, w_kv, w_q, w_o, loss_target, m_mix_norm_pre, m_mix_norm_post, m_ffn_norm_pre, m_ffn_norm_post, m_ffn_w_gate_up, m_ffn_w_down, m_conv_w_in, m_conv_w, m_conv_w_out, m_kv_norm, m_w_kv, m_w_q, m_w_o, v_mix_norm_pre, v_mix_norm_post, v_ffn_norm_pre, v_ffn_norm_post, v_ffn_w_gate_up, v_ffn_w_down, v_conv_w_in, v_conv_w, v_conv_w_out, v_kv_norm, v_w_kv, v_w_q, v_w_o):
    T, D = x.shape[1], x.shape[2]
    L = ffn_w_gate_up.shape[0]
    n_gu = ffn_w_gate_up.shape[2]
    f_sh = ffn_w_down.shape[1]
    F = N_CHIPS * f_sh
    x0 = x[0]
    tgt = loss_target[0]

    half = HEAD_DIM // 2
    inv_freq = ROPE_THETA ** (-jnp.arange(half, dtype=F32) / half)
    ang = positions[0].astype(F32)[:, None] * inv_freq
    cosv, sinv = jnp.cos(ang), jnp.sin(ang)
    cos2 = jnp.tile(cosv, (1, LANES // half))
    ss2 = jnp.tile(jnp.concatenate([-sinv, sinv], axis=1), (1, LANES // HEAD_DIM))

    def as2d(a):
        return a.reshape(-1, a.shape[-1])

    big = [ffn_w_gate_up, ffn_w_down, conv_w_in, conv_w_out, w_kv, w_q, w_o]
    big_m = [m_ffn_w_gate_up, m_ffn_w_down, m_conv_w_in, m_conv_w_out, m_w_kv, m_w_q, m_w_o]
    big_v = [v_ffn_w_gate_up, v_ffn_w_down, v_conv_w_in, v_conv_w_out, v_w_kv, v_w_q, v_w_o]
    chip = 2 * lax.axis_index("x") + lax.axis_index("y")
    where = jnp.stack([chip, lax.axis_index("c")]).astype(jnp.int32)
    tc = conv_w.shape[2]
    cw_pad = jnp.concatenate([conv_w[0], jnp.zeros((8 - conv_w.shape[1], tc), F32)], axis=0)

    GU0, GU1, WD0, WD1, WCI, WCO, WKV, WQ, WO = range(9)
    shards = [(ffn_w_gate_up, 0), (ffn_w_gate_up, 1), (ffn_w_down, 0), (ffn_w_down, 1), (conv_w_in, 0),
              (conv_w_out, 0), (w_kv[None], 0), (w_q, 0), (w_o, 0)]
    ag_groups = [
        [(WCI, None, 0, D), (WCO, None, 0, D // N_CHIPS)],
        [(GU0, None, 0, D), (WD0, None, 0, f_sh)],
        [(WKV, None, 0, D), (WQ, None, 0, D)],
        [(WO, None, 0, D // N_CHIPS), (GU1, None, 0, D), (WD1, None, 0, f_sh)],
    ]

    def localised(group, idxs):
        return [(idxs.index(p[0]),) + p[1:] for p in group]

    cur = [None] * len(shards)
    first = [WCI, WCO]
    sems0, vals, taps, token = _ag_start(
        [_cast_place(*shards[i], where, BF, name=f"place{i}") for i in first],
        _cast_place(cw_pad[None], 0, where, F32, name="place_taps"), [localised(ag_groups[0], first)],
        name="ag_start0")
    for i, v in zip(first, vals):
        cur[i] = v
    rest = [i for i in range(len(shards)) if i not in first]
    sems1, vals, _, _ = _ag_start(
        [_cast_place(*shards[i], where, BF, name=f"place{i}", dep=token) for i in rest], None,
        [localised(g, rest) for g in ag_groups[1:]], name="ag_start1")
    for i, v in zip(rest, vals):
        cur[i] = v
    ag_sems = sems0 + sems1

    def gather_group(g, after):
        nonlocal taps
        idxs = sorted({p[0] for p in ag_groups[g]})
        local = localised(ag_groups[g], idxs)
        vals, landed_taps = _ag_wait(ag_sems[g], [cur[i] for i in idxs], taps if g == 0 else None, local, after,
                                     name=f"ag_wait{g}")
        if g == 0:
            taps = landed_taps
        vals = _ag_forward(vals, local, name=f"ag_forward{g}")
        for i, v in zip(idxs, vals):
            cur[i] = v

    def row(a, i):
        return a[i:i + 1]

    gather_group(0, None)
    wci, wco, cw = cur[WCI], cur[WCO].reshape(1, D, D), taps
    z, hn_m0 = _norm_matmul(x0, row(mix_norm_pre, 0), wci, cos2, ss2, name="f0_conv_in",
                            rope_shards=0, scale=1.0, out_dtype=BF)
    vmix = _conv_fwd(z, cw, name="f0_conv")
    y0, h1 = _matmul_postnorm(vmix, wco, 0, row(mix_norm_post, 0), x0, name="f0_conv_out")
    gather_group(1, h1)
    wgu0, wd0 = cur[GU0], cur[WD0].reshape(1, F, D)
    s0, ut0, a0, hn_f0 = _norm_swiglu(h1, row(ffn_norm_pre, 0), wgu0, 0, name="f0_gate_up")
    f0, h2 = _matmul_postnorm(a0, wd0, 0, row(ffn_norm_post, 0), h1, name="f0_down")

    gather_group(2, h2)
    wkv, wq = cur[WKV], cur[WQ]
    kv_all, hn_kv = _norm_matmul(h2, kv_norm.reshape(1, D), wkv, cos2, ss2, name="f1_kv",
                                 rope_shards=N_CHIPS // 2, scale=1.0, out_dtype=BF)
    q_all, hn_m1 = _norm_matmul(h2, row(mix_norm_pre, 1), wq, cos2, ss2, name="f1_q",
                                rope_shards=N_CHIPS, scale=HEAD_DIM ** -0.5, out_dtype=BF)
    o_att, lse = _attn_fwd(q_all, kv_all, name="f1_attn")
    gather_group(3, o_att)
    wgu1, wd1, wo = cur[GU1], cur[WD1].reshape(1, F, D), cur[WO].reshape(1, D, D)
    y1, h3 = _matmul_postnorm(o_att, wo, 0, row(mix_norm_post, 1), h2, name="f1_attn_out")
    s1, ut1, a1, hn_f1 = _norm_swiglu(h3, row(ffn_norm_pre, 1), wgu1, 0, name="f1_gate_up")
    f1, dh4, sq = _matmul_postnorm_loss(a1, wd1, 0, row(ffn_norm_post, 1), h3, tgt, name="f1_down_loss")
    loss_part = 0.5 * sq[0, 0] / D

    gu_shape = (N_CHIPS, D, n_gu)
    in_chips = lambda a: a.reshape(N_CHIPS, -1, a.shape[-1])

    def scatter_start(dws, tag):
        theirs = _sibling_swap(dws, name=f"rs_swap_{tag}")
        parts = [_pair_sum(dw, t, where, name=f"rs_pair_sum_{tag}{i}") for i, (dw, t) in enumerate(zip(dws, theirs))]
        return _exchange_start(parts, name=f"rs_exchange_start_{tag}")

    def scatter_go(swap, after, tag):
        dws, theirs = _swap_wait(swap, after, name=f"rs_swap_wait_{tag}")
        parts = [_pair_sum(dw, t, where, name=f"rs_pair_sum_{tag}{i}") for i, (dw, t) in enumerate(zip(dws, theirs))]
        return _exchange_start(parts, name=f"rs_exchange_start_{tag}")

    dyf1, dg1, du1, d_ffn_post1 = _postnorm_bwd_swiglu(dh4, f1, row(ffn_norm_post, 1), wd1, 0, s1, ut1,
                                                       name="b1_down")
    dwd1 = _grad_matmul(a1, dyf1, (2, F // 2, D), F // 2, D, lambda i, j: (i, 0, 0), None, name="b1_dw_down")
    dwgu1 = _grad_matmul(hn_f1, dg1, gu_shape, D, n_gu, lambda i, j: (j, 0, 0), None, name="b1_dw_gate")
    dwgu1 = _grad_matmul(hn_f1, du1, gu_shape, D, n_gu, lambda i, j: (j + 2, 0, 0), dwgu1, name="b1_dw_up")
    dh3, d_ffn_pre1 = _matmul_prenorm_bwd((dg1, du1), wgu1, 0, h3, row(ffn_norm_pre, 1), dh4, name="b1_gate_up")

    dy1, do, d_mix_post1 = _postnorm_bwd_matmul(dh3, y1, row(mix_norm_post, 1), wo, 0, name="b1_attn_out",
                                                da_dtype=F32)
    dwo = _grad_matmul(o_att, dy1, (1, D, D), D, D, lambda i, j: (0, 0, 0), None, name="b1_dw_o")
    swap_a, token = _swap_start([dwgu1, in_chips(dwd1), in_chips(dwo)], name="rs_swap_start_a")
    prev = None
    for gi, (window, dil) in enumerate(BRANCHES):
        prev = _attn_bwd(q_all, kv_all, do, o_att, lse, cos2, ss2, gi, dil, prev, name=f"b1_attn{gi}", dep=token)
        token = None
        if gi == 0:
            rs_a, token = scatter_go(swap_a, prev[0], "a")
    dq_all, dk_all, dv_all = prev
    n_q = wq.shape[2]
    n_kv = wkv.shape[2]
    dwq = _grad_matmul(hn_m1, dq_all, (N_CHIPS, D, n_q), D, n_q, lambda i, j: (j, 0, 0), None, name="b1_dw_q")
    dwkv = _grad_matmul(hn_kv, dk_all, (N_CHIPS, D, n_kv), D, n_kv, lambda i, j: (j, 0, 0), None, name="b1_dw_k")
    dwkv = _grad_matmul(hn_kv, dv_all, (N_CHIPS, D, n_kv), D, n_kv, lambda i, j: (j + 2, 0, 0), dwkv, name="b1_dw_v")
    dh2, d_mix_pre1 = _matmul_prenorm_bwd((dq_all,), wq, 0, h2, row(mix_norm_pre, 1), dh3, name="b1_q")
    dh2, d_kv_norm = _matmul_prenorm_bwd((dk_all, dv_all), wkv, 0, h2, kv_norm.reshape(1, D), dh2, name="b1_kv")
    swap_b, token = _swap_start([dwkv, dwq], name="rs_swap_start_b")

    dyf0, dg0, du0, d_ffn_post0 = _postnorm_bwd_swiglu(dh2, f0, row(ffn_norm_post, 0), wd0, 0, s0, ut0,
                                                       name="b0_down", dep=token)
    rs_b, token = scatter_go(swap_b, dyf0, "b")
    dwd0 = _grad_matmul(a0, dyf0, (2, F // 2, D), F // 2, D, lambda i, j: (i, 0, 0), None, name="b0_dw_down",
                        dep=token)
    dwgu0 = _grad_matmul(hn_f0, dg0, gu_shape, D, n_gu, lambda i, j: (j, 0, 0), None, name="b0_dw_gate")
    dwgu0 = _grad_matmul(hn_f0, du0, gu_shape, D, n_gu, lambda i, j: (j + 2, 0, 0), dwgu0, name="b0_dw_up")
    dh1, d_ffn_pre0 = _matmul_prenorm_bwd((dg0, du0), wgu0, 0, h1, row(ffn_norm_pre, 0), dh2, name="b0_gate_up")
    swap_c, token = _swap_start([dwgu0, in_chips(dwd0)], name="rs_swap_start_c")

    dy0, dvmix, d_mix_post0 = _postnorm_bwd_matmul(dh1, y0, row(mix_norm_post, 0), wco, 0, name="b0_conv_out",
                                                   da_dtype=BF, dep=token)
    rs_c, token = scatter_go(swap_c, dy0, "c")
    dwco = _grad_matmul(vmix, dy0, (1, D, D), D, D, lambda i, j: (0, 0, 0), None, name="b0_dw_conv_out",
                        dep=token)
    dz, dcw = _conv_bwd(z, cw, dvmix, name="b0_conv")
    n_ci = wci.shape[2]
    dwci = _grad_matmul(hn_m0, dz, (N_CHIPS, D, n_ci), D, n_ci, lambda i, j: (j, 0, 0), None, name="b0_dw_conv_in")
    dx, d_mix_pre0 = _matmul_prenorm_bwd((dz,), wci, 0, x0, row(mix_norm_pre, 0), dh1, name="b0_conv_in")

    pack = jnp.concatenate([
        d_mix_pre0, d_mix_pre1, d_mix_post0, d_mix_post1, d_ffn_pre0, d_ffn_pre1, d_ffn_post0, d_ffn_post1,
        d_kv_norm, dcw[0:3], jnp.full((1, D), loss_part, F32),
        jnp.zeros((SMALL_ROWS - 13, D), F32)], axis=0)
    red = _all_reduce_small(pack)
    loss = red[12, 0]
    myj = 2 * lax.axis_index("x") + lax.axis_index("y")
    g_conv_w = lax.dynamic_slice(red, (9, myj * tc), (3, tc))

    zeros7 = jnp.zeros((SMALL_ROWS - 9, D), F32)
    w_small = jnp.concatenate([mix_norm_pre, mix_norm_post, ffn_norm_pre, ffn_norm_post, kv_norm.reshape(1, D), zeros7], axis=0)
    m_small = jnp.concatenate([m_mix_norm_pre, m_mix_norm_post, m_ffn_norm_pre, m_ffn_norm_post, m_kv_norm.reshape(1, D), zeros7], axis=0)
    v_small = jnp.concatenate([v_mix_norm_pre, v_mix_norm_post, v_ffn_norm_pre, v_ffn_norm_post, v_kv_norm.reshape(1, D), zeros7], axis=0)
    d_small, nm_small, nv_small = _adamw(w_small, red, m_small, v_small, name="adamw_small")

    pad5 = jnp.zeros((5, tc), F32)
    d_cw, nm_cw, nv_cw = _adamw(cw_pad, jnp.concatenate([g_conv_w, pad5], axis=0),
                                jnp.concatenate([m_conv_w[0], pad5], axis=0),
                                jnp.concatenate([v_conv_w[0], pad5], axis=0), name="adamw_conv_w")

    rs_d, _ = scatter_start([dwci, in_chips(dwco)], "d")

    pieces = {"a": [(0, D), (1, f_sh), (6, 0)], "b": [(4, 0), (5, 0)], "c": [(0, 0), (1, 0)], "d": [(2, 0), (3, 0)]}
    grads2d = [None] * len(big)
    big_out = [None] * len(big)

    def finish(groups, after, tag):
        regions, idxs = [], []
        for gtag, handle in groups:
            landed, parts = _exchange_wait(handle, after, name=f"rs_exchange_wait_{gtag}")
            for i, (l, p, (wi, off)) in enumerate(zip(landed, parts, pieces[gtag])):
                total = as2d(big[wi]).shape[0]
                grads2d[wi] = _chip_sum(l, p, where, total, off, grads2d[wi], name=f"rs_chip_sum_{gtag}{i}")
                if wi not in idxs:
                    idxs.append(wi)
                regions.append((idxs.index(wi), off, 2 * l.shape[1]))
        joined = _sibling_join([grads2d[wi] for wi in idxs], regions, name=f"rs_sibling_join_{tag}")
        for wi, gr in zip(idxs, joined):
            w = big[wi]
            d_, m_, v_, g_ = _adamw(as2d(w), gr, as2d(big_m[wi]), as2d(big_v[wi]), name=f"adamw{wi}",
                                    emit_grad=True)
            big_out[wi] = (g_.reshape(w.shape), d_.reshape(w.shape), m_.reshape(w.shape), v_.reshape(w.shape))

    finish([("a", rs_a), ("b", rs_b), ("c", rs_c)], dx, "abc")
    finish([("d", rs_d)], big_out[0][1], "d")

    def small(a):
        return (a[0:2], a[2:4], a[4:6], a[6:8])

    def assemble(sm, cwv, kind):
        pre, post, fpre, fpost = small(sm)
        b = [t[kind] for t in big_out]
        return [pre, post, fpre, fpost, b[0], b[1], b[2], cwv[0:3].reshape(conv_w.shape), b[3],
                sm[8], b[4], b[5].reshape(w_q.shape), b[6].reshape(w_o.shape)]

    grads = assemble(red, jnp.concatenate([g_conv_w, pad5], axis=0), 0)
    deltas = assemble(d_small, d_cw, 1)
    new_m = assemble(nm_small, nm_cw, 2)
    new_v = assemble(nv_small, nv_cw, 3)
    return (loss, dx.reshape(x.shape), *grads, *deltas, *new_m, *new_v)
```

```python
import functools

import jax
import jax.numpy as jnp
from jax import lax
from jax.experimental import pallas as pl
from jax.experimental.pallas import tpu as pltpu

HEAD_DIM = 64
BAND = 128
BRANCHES = ((128, 1), (512, 4), (2048, 16))
ROPE_THETA = 10000.0
RMS_EPS = 1e-6
NEG_INF = -1e30
ADAM_LR = 0.001
ADAM_B1 = 0.9
ADAM_B2 = 0.999
ADAM_EPS = 1e-08
ADAM_WD = 0.01
ADAM_STEP = 10

N_CHIPS = 4
N_DEV = 8
LANES = 128
ROW_BLOCK = 512
ROW_BLOCK_WIDE = 1024
GRAD_CHUNK = 2048
ATTN_FWD_UNROLL = 16
ATTN_BWD_UNROLL = 8
ATTN_BLOCK_ROWS = 2048
VMEM_LIMIT = 56 * 1024 * 1024
SMALL_ROWS = 16
STREAM_BUFFERS = 3
ADAMW_BLOCK_BYTES = 1024 * 1024
DMA_CHUNK_BYTES = 512 * 1024

BF = jnp.bfloat16
F32 = jnp.float32
MESH = pl.DeviceIdType.MESH
ANY = pl.BlockSpec(memory_space=pl.ANY)


def _cp(*sem):
    return pltpu.CompilerParams(dimension_semantics=sem, vmem_limit_bytes=VMEM_LIMIT)


def _rot_half(t, first):
    return jnp.where(first, pltpu.roll(t, 96, 1), pltpu.roll(t, 32, 1))


def _sigmoid(x):
    return pl.reciprocal(1.0 + jnp.exp(-x), approx=True)


def _first_half_mask(rows):
    lane = lax.broadcasted_iota(jnp.int32, (rows, LANES), 1)
    return (lane % HEAD_DIM) < (HEAD_DIM // 2)


def _normed_rows(j, rows, x_ref, g_ref, xn_ref, xs, last_start, tm):
    @pl.when(j == 0)
    def _():
        xv = x_ref[...]
        r = lax.rsqrt(jnp.mean(xv * xv, axis=-1, keepdims=True) + RMS_EPS)
        xn = (xv * r * g_ref[...]).astype(BF)
        xs[rows, :] = xn
        xn_ref[...] = xn

    @pl.when(j > 0)
    def _():
        xn_ref[...] = xs[pl.ds(last_start, tm), :]


def _norm_matmul(x, gain, wg, cos2, ss2, *, name, rope_shards, scale, out_dtype):
    T, D = x.shape
    n = wg.shape[2]
    tm = min(ROW_BLOCK_WIDE, T)
    ni = T // tm

    def body(x_ref, g_ref, w_ref, cos_ref, ss_ref, y_ref, xn_ref, xs):
        j = pl.program_id(0)
        rows = pl.ds(pl.multiple_of(pl.program_id(1) * tm, tm), tm)
        _normed_rows(j, rows, x_ref, g_ref, xn_ref, xs, (ni - 1) * tm, tm)
        acc = jnp.dot(xs[rows, :], w_ref[...], preferred_element_type=F32)

        def plain():
            y_ref[...] = acc.astype(out_dtype)

        def rope():
            cosv = cos_ref[...]
            ssv = ss_ref[...]
            first = _first_half_mask(tm)
            for ci in range(n // LANES):
                t = acc[:, ci * LANES:(ci + 1) * LANES]
                y = (t * cosv + _rot_half(t, first) * ssv) * scale
                y_ref[:, ci * LANES:(ci + 1) * LANES] = y.astype(out_dtype)

        if rope_shards == 0:
            plain()
        elif rope_shards == N_CHIPS:
            rope()
        else:
            pl.when(j < rope_shards)(rope)
            pl.when(j >= rope_shards)(plain)

    first_pass = lambda j, i: (jnp.where(j == 0, i, ni - 1), 0)
    return pl.pallas_call(
        body, name=name,
        grid=(N_CHIPS, ni),
        in_specs=[
            pl.BlockSpec((tm, D), first_pass),
            pl.BlockSpec((1, D), lambda j, i: (0, 0)),
            pl.BlockSpec((None, D, n), lambda j, i: (j, 0, 0)),
            pl.BlockSpec((tm, LANES), lambda j, i: (i, 0)),
            pl.BlockSpec((tm, LANES), lambda j, i: (i, 0)),
        ],
        out_specs=[
            pl.BlockSpec((tm, n), lambda j, i: (i, j)),
            pl.BlockSpec((tm, D), first_pass),
        ],
        out_shape=[jax.ShapeDtypeStruct((T, N_CHIPS * n), out_dtype),
                   jax.ShapeDtypeStruct((T, D), BF)],
        scratch_shapes=[pltpu.VMEM((T, D), BF)],
        compiler_params=_cp("arbitrary", "arbitrary"),
    )(x, gain, wg, cos2, ss2)


def _norm_swiglu(x, gain, wg, layer, *, name):
    T, D = x.shape
    n = wg.shape[2]
    tm = min(ROW_BLOCK, T)
    ni = T // tm

    def body(x_ref, g_ref, wg_ref, wu_ref, so_ref, uto_ref, ao_ref, xn_ref, xs):
        j = pl.program_id(0)
        rows = pl.ds(pl.multiple_of(pl.program_id(1) * tm, tm), tm)
        _normed_rows(j, rows, x_ref, g_ref, xn_ref, xs, (ni - 1) * tm, tm)
        g = jnp.dot(xs[rows, :], wg_ref[...], preferred_element_type=F32)
        u = jnp.dot(xs[rows, :], wu_ref[...], preferred_element_type=F32)
        sg = _sigmoid(g)
        s = g * sg
        so_ref[...] = s.astype(BF)
        uto_ref[...] = (u * (sg + s * (1.0 - sg))).astype(BF)
        ao_ref[...] = (s * u).astype(BF)

    half = N_CHIPS // 2
    first_pass = lambda j, i: (jnp.where(j == 0, i, ni - 1), 0)
    act = jax.ShapeDtypeStruct((T, half * n), BF)
    return pl.pallas_call(
        body, name=name,
        grid=(half, ni),
        in_specs=[
            pl.BlockSpec((tm, D), first_pass),
            pl.BlockSpec((1, D), lambda j, i: (0, 0)),
            pl.BlockSpec((None, D, n), lambda j, i: (j, layer, 0)),
            pl.BlockSpec((None, D, n), lambda j, i: (j + half, layer, 0)),
        ],
        out_specs=[
            pl.BlockSpec((tm, n), lambda j, i: (i, j)),
            pl.BlockSpec((tm, n), lambda j, i: (i, j)),
            pl.BlockSpec((tm, n), lambda j, i: (i, j)),
            pl.BlockSpec((tm, D), first_pass),
        ],
        out_shape=[act, act, act, jax.ShapeDtypeStruct((T, D), BF)],
        scratch_shapes=[pltpu.VMEM((T, D), BF)],
        compiler_params=_cp("arbitrary", "arbitrary"),
    )(x, gain, wg, wg)


def _matmul_postnorm(a, w3, widx, gain, h_old, *, name):
    T, K = a.shape
    D = w3.shape[2]
    tm = min(ROW_BLOCK_WIDE, T)

    def body(a_ref, w_ref, g_ref, h_ref, y_ref, hn_ref):
        y = jnp.dot(a_ref[...].astype(BF), w_ref[...], preferred_element_type=F32)
        y_ref[...] = y.astype(BF)
        r = lax.rsqrt(jnp.mean(y * y, axis=-1, keepdims=True) + RMS_EPS)
        hn_ref[...] = h_ref[...] + y * r * g_ref[...]

    return pl.pallas_call(
        body, name=name,
        grid=(T // tm,),
        in_specs=[
            pl.BlockSpec((tm, K), lambda i: (i, 0)),
            pl.BlockSpec((None, K, D), lambda i: (widx, 0, 0)),
            pl.BlockSpec((1, D), lambda i: (0, 0)),
            pl.BlockSpec((tm, D), lambda i: (i, 0)),
        ],
        out_specs=[pl.BlockSpec((tm, D), lambda i: (i, 0)),
                   pl.BlockSpec((tm, D), lambda i: (i, 0))],
        out_shape=[jax.ShapeDtypeStruct((T, D), BF), jax.ShapeDtypeStruct((T, D), F32)],
        compiler_params=_cp("parallel"),
    )(a, w3, gain, h_old)


def _matmul_postnorm_loss(a, w3, widx, gain, h_old, target, *, name):
    T, K = a.shape
    D = w3.shape[2]
    tm = min(ROW_BLOCK, T)

    def body(a_ref, w_ref, g_ref, h_ref, t_ref, y_ref, dh_ref, s_ref):
        @pl.when(pl.program_id(0) == 0)
        def _():
            s_ref[...] = jnp.zeros_like(s_ref)

        y = jnp.dot(a_ref[...].astype(BF), w_ref[...], preferred_element_type=F32)
        y_ref[...] = y.astype(BF)
        r = lax.rsqrt(jnp.mean(y * y, axis=-1, keepdims=True) + RMS_EPS)
        e = (h_ref[...] + y * r * g_ref[...]) - t_ref[...]
        dh_ref[...] = e * (1.0 / D)
        s_ref[...] += jnp.sum(e * e)

    rows = pl.BlockSpec((tm, D), lambda i: (i, 0))
    return pl.pallas_call(
        body, name=name,
        grid=(T // tm,),
        in_specs=[
            pl.BlockSpec((tm, K), lambda i: (i, 0)),
            pl.BlockSpec((None, K, D), lambda i: (widx, 0, 0)),
            pl.BlockSpec((1, D), lambda i: (0, 0)),
            rows, rows,
        ],
        out_specs=[rows, rows, pl.BlockSpec((8, LANES), lambda i: (0, 0))],
        out_shape=[jax.ShapeDtypeStruct((T, D), BF), jax.ShapeDtypeStruct((T, D), F32),
                   jax.ShapeDtypeStruct((8, LANES), F32)],
        compiler_params=_cp("arbitrary"),
    )(a, w3, gain, h_old, target)


def _shift_down(u, k):
    row = lax.broadcasted_iota(jnp.int32, u.shape, 0)
    return jnp.where(row >= k, pltpu.roll(u, k, 0), 0.0)


def _shift_up(u, k):
    T = u.shape[0]
    row = lax.broadcasted_iota(jnp.int32, u.shape, 0)
    return jnp.where(row < T - k, pltpu.roll(u, T - k, 0), 0.0)


def _conv_fwd(z, cw, *, name):
    T = z.shape[0]
    D = z.shape[1] // 3
    tc = cw.shape[2]
    nb = D // tc

    def body(b_ref, c_ref, h_ref, w_ref, o_ref):
        u = c_ref[...].astype(F32) * h_ref[...].astype(F32)
        w = w_ref[...]
        conv = w[2:3] * u + w[1:2] * _shift_down(u, 1) + w[0:1] * _shift_down(u, 2)
        o_ref[...] = (b_ref[...].astype(F32) * conv).astype(BF)

    return pl.pallas_call(
        body, name=name,
        grid=(nb,),
        in_specs=[
            pl.BlockSpec((T, tc), lambda j: (0, j)),
            pl.BlockSpec((T, tc), lambda j: (0, nb + j)),
            pl.BlockSpec((T, tc), lambda j: (0, 2 * nb + j)),
            pl.BlockSpec((None, 8, tc), lambda j: (j, 0, 0)),
        ],
        out_specs=pl.BlockSpec((T, tc), lambda j: (0, j)),
        out_shape=jax.ShapeDtypeStruct((T, D), BF),
        compiler_params=_cp("parallel"),
    )(z, z, z, cw)


def _conv_bwd(z, cw, dv, *, name):
    T = z.shape[0]
    D = z.shape[1] // 3
    tc = LANES
    nb = D // tc
    per = cw.shape[2] // tc

    def body(b_ref, c_ref, h_ref, w_ref, dv_ref, dz_ref, dw_ref, stage, sems):
        j = pl.program_id(0)
        slot = j % 2

        def slab(p, jj, s):
            col = pl.multiple_of((p * nb + jj) * tc, tc)
            return pltpu.make_async_copy(stage.at[s, p], dz_ref.at[:, pl.ds(col, tc)], sems.at[s, p])

        @pl.when(j >= 2)
        def _():
            for p in range(3):
                slab(p, j - 2, slot).wait()

        c = c_ref[...].astype(F32)
        h = h_ref[...].astype(F32)
        u = c * h
        u1 = _shift_down(u, 1)
        u2 = _shift_down(u, 2)
        w = w_ref[...]
        dvv = dv_ref[...].astype(F32)
        dconv = dvv * b_ref[...].astype(F32)
        du = w[2:3] * dconv + w[1:2] * _shift_up(dconv, 1) + w[0:1] * _shift_up(dconv, 2)
        rows = lax.broadcasted_iota(jnp.int32, (8, tc), 0)
        dw_ref[...] = jnp.where(rows == 0, jnp.sum(dconv * u2, axis=0, keepdims=True),
                                jnp.where(rows == 1, jnp.sum(dconv * u1, axis=0, keepdims=True),
                                          jnp.where(rows == 2, jnp.sum(dconv * u, axis=0, keepdims=True), 0.0)))
        stage[slot, 0] = (dvv * (w[2:3] * u + w[1:2] * u1 + w[0:1] * u2)).astype(BF)
        stage[slot, 1] = (du * h).astype(BF)
        stage[slot, 2] = (du * c).astype(BF)
        for p in range(3):
            slab(p, j, slot).start()

        @pl.when(j == nb - 1)
        def _():
            for p in range(3):
                slab(p, j, slot).wait()
            if nb > 1:
                for p in range(3):
                    slab(p, j - 1, 1 - slot).wait()

    return pl.pallas_call(
        body, name=name,
        grid=(nb,),
        in_specs=[
            pl.BlockSpec((T, tc), lambda j: (0, j)),
            pl.BlockSpec((T, tc), lambda j: (0, nb + j)),
            pl.BlockSpec((T, tc), lambda j: (0, 2 * nb + j)),
            pl.BlockSpec((None, 8, tc), lambda j: (j // per, 0, j % per)),
            pl.BlockSpec((T, tc), lambda j: (0, j)),
        ],
        out_specs=[ANY, pl.BlockSpec((8, tc), lambda j: (0, j))],
        out_shape=[jax.ShapeDtypeStruct((T, 3 * D), BF), jax.ShapeDtypeStruct((8, D), F32)],
        scratch_shapes=[pltpu.VMEM((2, 3, T, tc), BF), pltpu.SemaphoreType.DMA((2, 3))],
        compiler_params=_cp("arbitrary"),
    )(z, z, z, cw, dv)


def _strided(base, count, d):
    return pl.ds(base, count, stride=d) if d > 1 else pl.ds(pl.multiple_of(base, BAND), count)


def _fill_band_bias(bias):
    qi = lax.broadcasted_iota(jnp.int32, (2 * BAND, 2 * BAND), 0) % BAND
    kj = lax.broadcasted_iota(jnp.int32, (2 * BAND, 2 * BAND), 1)
    dist = qi + BAND - kj
    band = (dist >= 0) & (dist <= BAND)
    bias[0] = jnp.where(band & (kj >= BAND), 0.0, NEG_INF)
    bias[1] = jnp.where(band, 0.0, NEG_INF)


def _attn_block_rows(T):
    return min(ATTN_BLOCK_ROWS, T)


def _head_mask():
    lane = lax.broadcasted_iota(jnp.int32, (2 * BAND, LANES), 1)
    row = lax.broadcasted_iota(jnp.int32, (2 * BAND, LANES), 0)
    return (lane < HEAD_DIM) == (row < BAND)


def _attn_fwd(q_all, kv_all, *, name):
    T = q_all.shape[0]
    NB = len(BRANCHES)
    Dm = q_all.shape[1] // NB
    HP = Dm // LANES
    R = _attn_block_rows(T)
    units = R // BAND
    dmax = max(d for _, d in BRANCHES)

    def body(*refs):
        ins = refs[:5 * NB]
        o_ref, l_ref, qbuf, kbuf, vbuf, o_s, l_s, bias = refs[5 * NB:]
        n = pl.program_id(0)
        pl.when((n == 0) & (pl.program_id(1) == 0))(lambda: _fill_band_bias(bias))
        hm = _head_mask()
        low = lax.broadcasted_iota(jnp.int32, (BAND, LANES), 1) < HEAD_DIM

        for g, (_, d) in enumerate(BRANCHES):
            q_ref, kp_ref, kc_ref, vp_ref, vc_ref = ins[5 * g:5 * g + 5]
            pr = BAND * d
            qbuf[...] = q_ref[...].astype(F32)
            kbuf[0:pr, :] = kp_ref[...].astype(F32)
            kbuf[pr:pr + R, :] = kc_ref[...].astype(F32)
            vbuf[0:pr, :] = vp_ref[...].astype(F32)
            vbuf[pr:pr + R, :] = vc_ref[...].astype(F32)

            def unit(u, carry, g=g, d=d, pr=pr):
                sub = u // d
                base = sub * pr + (u - sub * d)
                q = qbuf[_strided(base, BAND, d), :]
                q2 = jnp.where(hm, jnp.concatenate([q, q], axis=0), 0.0).astype(BF)
                k2 = kbuf[_strided(base, 2 * BAND, d), :].astype(BF)
                v2 = vbuf[_strided(base, 2 * BAND, d), :].astype(BF)
                s = lax.dot_general(q2, k2, (((1,), (1,)), ((), ())), preferred_element_type=F32)
                s = s + bias[((n > 0) | (sub > 0)).astype(jnp.int32)]
                m = jnp.max(s, axis=-1, keepdims=True)
                p = jnp.exp(s - m)
                l = jnp.sum(p, axis=-1, keepdims=True)
                pv = jnp.dot(p.astype(BF), v2, preferred_element_type=F32) * (1.0 / l)
                lse = m + jnp.log(l)
                o_s[g, _strided(base, BAND, d), :] = jnp.where(low, pv[:BAND], pv[BAND:])
                l_s[g, _strided(base, BAND, d), :] = jnp.where(low, lse[:BAND], lse[BAND:])
                return carry

            lax.fori_loop(0, units, unit, 0, unroll=min(ATTN_FWD_UNROLL, units))

        def merge(i, carry):
            sl = pl.ds(pl.multiple_of(i * BAND, BAND), BAND)
            lv = [l_s[g, sl, :] for g in range(NB)]
            m = functools.reduce(jnp.maximum, lv)
            e = [jnp.exp(v - m) for v in lv]
            tot = functools.reduce(jnp.add, e)
            inv = 1.0 / tot
            o_ref[sl, :] = functools.reduce(jnp.add, [(e[g] * inv) * o_s[g, sl, :] for g in range(NB)])
            l_ref[sl, :] = m + jnp.log(tot)
            return carry

        lax.fori_loop(0, units, merge, 0)

    in_specs, args = [], []
    for g, (_, d) in enumerate(BRANCHES):
        per = R // (BAND * d)
        for col, rows, idx in (
                (g * HP, R, lambda n, hp: n),
                (g * HP, BAND * d, lambda n, hp, per=per: jnp.maximum(n * per - 1, 0)),
                (g * HP, R, lambda n, hp: n),
                ((NB + g) * HP, BAND * d, lambda n, hp, per=per: jnp.maximum(n * per - 1, 0)),
                ((NB + g) * HP, R, lambda n, hp: n)):
            in_specs.append(pl.BlockSpec((rows, LANES), lambda n, hp, col=col, idx=idx: (idx(n, hp), col + hp)))
        args += [q_all, kv_all, kv_all, kv_all, kv_all]
    out = pl.BlockSpec((R, LANES), lambda n, hp: (n, hp))
    return pl.pallas_call(
        body, name=name,
        grid=(T // R, HP),
        in_specs=in_specs,
        out_specs=[out, out],
        out_shape=[jax.ShapeDtypeStruct((T, Dm), F32), jax.ShapeDtypeStruct((T, Dm), F32)],
        scratch_shapes=[pltpu.VMEM((R, LANES), F32),
                        pltpu.VMEM((BAND * dmax + R, LANES), F32), pltpu.VMEM((BAND * dmax + R, LANES), F32),
                        pltpu.VMEM((NB, R, LANES), F32), pltpu.VMEM((NB, R, LANES), F32),
                        pltpu.VMEM((2, 2 * BAND, 2 * BAND), F32)],
        compiler_params=_cp("arbitrary", "arbitrary"),
    )(*args)


def _attn_bwd(q_all, kv_all, do, o, lse, cos2, ss2, g, d, prev, *, name, dep=None):
    T = q_all.shape[0]
    NB = len(BRANCHES)
    Dm = q_all.shape[1] // NB
    HP = Dm // LANES
    R = _attn_block_rows(T)
    nblk = T // R
    units = R // BAND
    pr = BAND * d
    per = R // pr
    scale = HEAD_DIM ** -0.5

    def rope_bwd(t, cosv, ssv, first):
        return t * cosv - _rot_half(t, first) * ssv

    def body(q_ref, kp_ref, kc_ref, vp_ref, vc_ref, do_ref, o_ref, l_ref, cos_ref, ss_ref, *rest):
        dq_ref, dk_ref, dv_ref, qbuf, kbuf, vbuf, dq_s, dk_s, dv_s, pend_k, pend_v, bias = rest[-12:]
        i = pl.program_id(1)
        n = nblk - 1 - i
        pl.when((i == 0) & (pl.program_id(0) == 0))(lambda: _fill_band_bias(bias))
        hm = _head_mask()
        low = lax.broadcasted_iota(jnp.int32, (BAND, LANES), 1) < HEAD_DIM

        qbuf[...] = q_ref[...].astype(F32)
        kbuf[0:pr, :] = kp_ref[...].astype(F32)
        kbuf[pr:pr + R, :] = kc_ref[...].astype(F32)
        vbuf[0:pr, :] = vp_ref[...].astype(F32)
        vbuf[pr:pr + R, :] = vc_ref[...].astype(F32)

        @pl.when(i == 0)
        def _():
            pend_k[...] = jnp.zeros_like(pend_k)
            pend_v[...] = jnp.zeros_like(pend_v)

        def unit(u, carry):
            sub = per - 1 - u // d
            cls = u % d
            base = sub * pr + cls
            sl = _strided(base, BAND, d)
            sl2 = _strided(base, 2 * BAND, d)
            q = qbuf[sl, :]
            dov = do_ref[sl, :]
            ov = o_ref[sl, :]
            lv = l_ref[sl, :]
            q2 = jnp.where(hm, jnp.concatenate([q, q], axis=0), 0.0).astype(BF)
            do2 = jnp.where(hm, jnp.concatenate([dov, dov], axis=0), 0.0)
            oo = dov * ov
            delta = jnp.sum(jnp.where(hm, jnp.concatenate([oo, oo], axis=0), 0.0), axis=-1, keepdims=True)
            lse2 = jnp.concatenate([lv[:, 0:1], lv[:, HEAD_DIM:HEAD_DIM + 1]], axis=0)
            do2 = do2.astype(BF)
            k2 = kbuf[sl2, :].astype(BF)
            v2 = vbuf[sl2, :].astype(BF)
            s = lax.dot_general(q2, k2, (((1,), (1,)), ((), ())), preferred_element_type=F32)
            p = jnp.exp(s + bias[((n > 0) | (sub > 0)).astype(jnp.int32)] - lse2)
            dp = lax.dot_general(do2, v2, (((1,), (1,)), ((), ())), preferred_element_type=F32)
            ds = (p * (dp - delta)).astype(BF)
            dq2 = jnp.dot(ds, k2, preferred_element_type=F32)
            dq = jnp.where(low, dq2[:BAND], dq2[BAND:])
            dq_s[sl, :] = dq
            tn = (((0,), (0,)), ((), ()))
            dk2 = lax.dot_general(ds, q2, tn, preferred_element_type=F32)
            dv2 = lax.dot_general(p.astype(BF), do2, tn, preferred_element_type=F32)
            dk_s[sl, :] = dk2[BAND:] + pend_k[cls]
            dv_s[sl, :] = dv2[BAND:] + pend_v[cls]
            pend_k[cls] = dk2[:BAND]
            pend_v[cls] = dv2[:BAND]
            return carry

        lax.fori_loop(0, units, unit, 0, unroll=min(ATTN_BWD_UNROLL, units))

        whole = _first_half_mask(R)
        dq_ref[...] = (rope_bwd(dq_s[...], cos_ref[...], ss_ref[...], whole) * scale).astype(BF)
        dk_ref[...] = rope_bwd(dk_s[...], cos_ref[...], ss_ref[...], whole).astype(BF)
        dv_ref[...] = dv_s[...].astype(BF)

    blk = (R, LANES)
    pblk = (pr, LANES)
    cur = lambda hp, i: nblk - 1 - i
    prv = lambda hp, i: jnp.maximum((nblk - 1 - i) * per - 1, 0)
    in_specs = [
        pl.BlockSpec(blk, lambda hp, i: (cur(hp, i), g * HP + hp)),
        pl.BlockSpec(pblk, lambda hp, i: (prv(hp, i), g * HP + hp)),
        pl.BlockSpec(blk, lambda hp, i: (cur(hp, i), g * HP + hp)),
        pl.BlockSpec(pblk, lambda hp, i: (prv(hp, i), (NB + g) * HP + hp)),
        pl.BlockSpec(blk, lambda hp, i: (cur(hp, i), (NB + g) * HP + hp)),
        pl.BlockSpec(blk, lambda hp, i: (cur(hp, i), hp)),
        pl.BlockSpec(blk, lambda hp, i: (cur(hp, i), hp)),
        pl.BlockSpec(blk, lambda hp, i: (cur(hp, i), hp)),
        pl.BlockSpec(blk, lambda hp, i: (cur(hp, i), 0)),
        pl.BlockSpec(blk, lambda hp, i: (cur(hp, i), 0)),
    ]
    args = [q_all, kv_all, kv_all, kv_all, kv_all, do, o, lse, cos2, ss2]
    if dep is not None:
        in_specs.append(ANY)
        args.append(dep)
    aliases = {}
    if prev is not None:
        in_specs += [ANY, ANY, ANY]
        aliases = {len(args): 0, len(args) + 1: 1, len(args) + 2: 2}
        args += list(prev)
    wide = jax.ShapeDtypeStruct((T, NB * Dm), BF)
    out = pl.BlockSpec(blk, lambda hp, i: (cur(hp, i), g * HP + hp))
    return pl.pallas_call(
        body, name=name,
        grid=(HP, nblk),
        in_specs=in_specs,
        out_specs=[out, out, out],
        out_shape=[wide, wide, wide],
        scratch_shapes=[pltpu.VMEM(blk, F32), pltpu.VMEM((pr + R, LANES), F32), pltpu.VMEM((pr + R, LANES), F32),
                        pltpu.VMEM(blk, F32), pltpu.VMEM(blk, F32), pltpu.VMEM(blk, F32),
                        pltpu.VMEM((d, BAND, LANES), F32), pltpu.VMEM((d, BAND, LANES), F32),
                        pltpu.VMEM((2, 2 * BAND, 2 * BAND), F32)],
        input_output_aliases=aliases,
        compiler_params=_cp("arbitrary", "arbitrary"),
    )(*args)


def _postnorm_bwd(dh, y, g_ref_val):
    r = lax.rsqrt(jnp.mean(y * y, axis=-1, keepdims=True) + RMS_EPS)
    yn = y * r
    dyn = dh * g_ref_val
    dy = r * (dyn - yn * jnp.mean(dyn * yn, axis=-1, keepdims=True))
    return dy, yn


def _after(body, n_in, dep):
    if dep is None:
        return body
    return lambda *refs: body(*refs[:n_in], *refs[n_in + 1:])


def _dep_spec(dep):
    return [] if dep is None else [ANY]


def _dep_arg(dep):
    return [] if dep is None else [dep]


def _postnorm_bwd_matmul(dh, y, gain, w3, widx, *, name, da_dtype, dep=None):
    T, D = dh.shape
    K = w3.shape[1]
    tm = min(ROW_BLOCK_WIDE, T)

    def body(dh_ref, y_ref, g_ref, w_ref, dy_ref, da_ref, dg_ref):
        i = pl.program_id(0)

        @pl.when(i == 0)
        def _():
            dg_ref[...] = jnp.zeros_like(dg_ref)

        dhv = dh_ref[...]
        dy, yn = _postnorm_bwd(dhv, y_ref[...].astype(F32), g_ref[...])
        dg_ref[...] += jnp.sum(dhv * yn, axis=0, keepdims=True)
        dyb = dy.astype(BF)
        dy_ref[...] = dyb
        da = lax.dot_general(dyb, w_ref[...], (((1,), (1,)), ((), ())), preferred_element_type=F32)
        da_ref[...] = da.astype(da_dtype)

    return pl.pallas_call(
        _after(body, 4, dep), name=name,
        grid=(T // tm,),
        in_specs=[
            pl.BlockSpec((tm, D), lambda i: (i, 0)),
            pl.BlockSpec((tm, D), lambda i: (i, 0)),
            pl.BlockSpec((1, D), lambda i: (0, 0)),
            pl.BlockSpec((None, K, D), lambda i: (widx, 0, 0)),
        ] + _dep_spec(dep),
        out_specs=[pl.BlockSpec((tm, D), lambda i: (i, 0)),
                   pl.BlockSpec((tm, K), lambda i: (i, 0)),
                   pl.BlockSpec((1, D), lambda i: (0, 0))],
        out_shape=[jax.ShapeDtypeStruct((T, D), BF), jax.ShapeDtypeStruct((T, K), da_dtype),
                   jax.ShapeDtypeStruct((1, D), F32)],
        compiler_params=_cp("arbitrary"),
    )(dh, y, gain, w3, *_dep_arg(dep))


def _postnorm_bwd_swiglu(dh, y, gain, wd3, layer, s, ut, *, name, dep=None):
    T, D = dh.shape
    F = wd3.shape[1]
    tm = min(ROW_BLOCK, T)

    def body(dh_ref, y_ref, g_ref, w_ref, s_ref, ut_ref, dy_ref, dgo_ref, duo_ref, dgain_ref):
        @pl.when(pl.program_id(0) == 0)
        def _():
            dgain_ref[...] = jnp.zeros_like(dgain_ref)

        dhv = dh_ref[...]
        dy, yn = _postnorm_bwd(dhv, y_ref[...].astype(F32), g_ref[...])
        dgain_ref[...] += jnp.sum(dhv * yn, axis=0, keepdims=True)
        dyb = dy.astype(BF)
        dy_ref[...] = dyb
        da = lax.dot_general(dyb, w_ref[...], (((1,), (1,)), ((), ())), preferred_element_type=F32)
        dgo_ref[...] = (da * ut_ref[...].astype(F32)).astype(BF)
        duo_ref[...] = (da * s_ref[...].astype(F32)).astype(BF)

    rows = pl.BlockSpec((tm, D), lambda i: (i, 0))
    wide = pl.BlockSpec((tm, F), lambda i: (i, 0))
    act = jax.ShapeDtypeStruct((T, F), BF)
    return pl.pallas_call(
        _after(body, 6, dep), name=name,
        grid=(T // tm,),
        in_specs=[
            rows, rows,
            pl.BlockSpec((1, D), lambda i: (0, 0)),
            pl.BlockSpec((None, F, D), lambda i: (layer, 0, 0), pipeline_mode=pl.Buffered(1)),
            wide, wide,
        ] + _dep_spec(dep),
        out_specs=[rows, wide, wide, pl.BlockSpec((1, D), lambda i: (0, 0))],
        out_shape=[jax.ShapeDtypeStruct((T, D), BF), act, act, jax.ShapeDtypeStruct((1, D), F32)],
        compiler_params=_cp("arbitrary"),
    )(dh, y, gain, wd3, s, ut, *_dep_arg(dep))


def _matmul_prenorm_bwd(dzs, wg, layer, h, gain, dh_in, *, name):
    T, D = h.shape
    n = wg.shape[2]
    tm = min(ROW_BLOCK, T)
    per = N_CHIPS // len(dzs)

    def body(*refs):
        dz_refs = refs[:len(dzs)]
        w_ref, h_ref, g_ref, dhi_ref, dh_ref, dg_ref = refs[len(dzs):]

        @pl.when(pl.program_id(0) == 0)
        def _():
            dg_ref[...] = jnp.zeros_like(dg_ref)

        dhn = None
        for j in range(N_CHIPS):
            dz = dz_refs[j // per][:, (j % per) * n:(j % per + 1) * n]
            t = lax.dot_general(dz.astype(BF), w_ref[j], (((1,), (1,)), ((), ())), preferred_element_type=F32)
            dhn = t if dhn is None else dhn + t
        hv = h_ref[...]
        r = lax.rsqrt(jnp.mean(hv * hv, axis=-1, keepdims=True) + RMS_EPS)
        xh = hv * r
        dg_ref[...] += jnp.sum(dhn * xh, axis=0, keepdims=True)
        dxn = dhn * g_ref[...]
        dh_ref[...] = dhi_ref[...] + r * (dxn - xh * jnp.mean(dxn * xh, axis=-1, keepdims=True))

    rows = pl.BlockSpec((tm, D), lambda i: (i, 0))
    in_specs = [pl.BlockSpec((tm, per * n), lambda i: (i, 0)) for _ in dzs]
    in_specs += [pl.BlockSpec((N_CHIPS, D, n), lambda i: (0, layer, 0), pipeline_mode=pl.Buffered(1)),
                 rows, pl.BlockSpec((1, D), lambda i: (0, 0)), rows]
    return pl.pallas_call(
        body, name=name,
        grid=(T // tm,),
        in_specs=in_specs,
        out_specs=[rows, pl.BlockSpec((1, D), lambda i: (0, 0))],
        out_shape=[jax.ShapeDtypeStruct((T, D), F32), jax.ShapeDtypeStruct((1, D), F32)],
        compiler_params=_cp("arbitrary"),
    )(*dzs, wg, h, gain, dh_in)


def _grad_matmul(a, b, out_shape3, tme, tne, out_index, prev, *, name, dep=None):
    T, M = a.shape
    N = b.shape[1]
    tk = min(GRAD_CHUNK, T)
    nk = T // tk

    def body(a_ref, b_ref, *rest):
        o_ref, acc = rest[-2:]
        k = pl.program_id(2)
        part = jnp.dot(a_ref[...].astype(BF).T, b_ref[...].astype(BF), preferred_element_type=F32)

        @pl.when(k == 0)
        def _():
            acc[...] = part

        @pl.when(k > 0)
        def _():
            acc[...] += part

        @pl.when(k == nk - 1)
        def _():
            o_ref[...] = acc[...].astype(BF)

    in_specs = [pl.BlockSpec((tk, tme), lambda i, j, k: (k, i)),
                pl.BlockSpec((tk, tne), lambda i, j, k: (k, j))]
    args = [a, b]
    aliases = {}
    if prev is not None:
        in_specs.append(ANY)
        args.append(prev)
        aliases = {2: 0}
    in_specs += _dep_spec(dep)
    args += _dep_arg(dep)
    return pl.pallas_call(
        body, name=name,
        grid=(M // tme, N // tne, nk),
        in_specs=in_specs,
        out_specs=pl.BlockSpec((None, tme, tne), lambda i, j, k: out_index(i, j)),
        out_shape=jax.ShapeDtypeStruct(out_shape3, BF),
        scratch_shapes=[pltpu.VMEM((tme, tne), F32)],
        input_output_aliases=aliases,
        compiler_params=_cp("parallel", "parallel", "arbitrary"),
    )(*args)


def _row_tile(R, cap=512):
    fit = [t for t in range(16, min(R, cap) + 1, 16) if R % t == 0]
    return max(fit) if fit else R


def _cast_place(w3, layer, where, dtype, *, name, dep=None):
    _, R, C = w3.shape
    tr = _row_tile(R)

    def body(s_ref, w_ref, o_ref):
        o_ref[...] = w_ref[...].astype(o_ref.dtype)

    return pl.pallas_call(
        _after(body, 2, dep), name=name,
        grid_spec=pltpu.PrefetchScalarGridSpec(
            num_scalar_prefetch=1, grid=(R // tr,),
            in_specs=[pl.BlockSpec((None, tr, C), lambda i, s: (layer, i, 0))] + _dep_spec(dep),
            out_specs=pl.BlockSpec((None, tr, C), lambda i, s: (s[0], i, 0))),
        out_shape=jax.ShapeDtypeStruct((N_CHIPS, R, C), dtype),
        compiler_params=_cp("arbitrary"),
    )(where, w3, *_dep_arg(dep))


def _pair_sum(dw, theirs, where, *, name):
    G, rh, C = theirs.shape
    tr = _row_tile(rh)
    nr = rh // tr

    def body(s_ref, a_ref, b_ref, o_ref):
        o_ref[...] = (a_ref[...].astype(F32) + b_ref[...].astype(F32)).astype(BF)

    mine = pl.BlockSpec((None, tr, C), lambda g, i, s: (g, s[1] * nr + i, 0))
    spec = pl.BlockSpec((None, tr, C), lambda g, i, s: (g, i, 0))
    return pl.pallas_call(
        body, name=name,
        grid_spec=pltpu.PrefetchScalarGridSpec(
            num_scalar_prefetch=1, grid=(G, nr), in_specs=[mine, spec], out_specs=spec),
        out_shape=jax.ShapeDtypeStruct((G, rh, C), BF),
        compiler_params=_cp("arbitrary", "arbitrary"),
    )(where, dw, theirs)


def _chip_sum(landed, parts, where, total_rows, row_off, prev, *, name):
    G, rh, C = landed.shape
    tr = _row_tile(rh)
    nr = rh // tr
    base = row_off // tr

    def body(s_ref, l_ref, p_ref, *rest):
        o_ref = rest[-1]
        for j in range(G):
            def own(j=j):
                v = p_ref[...].astype(F32)
                o_ref[...] = v if j == 0 else o_ref[...] + v

            def other(j=j):
                v = l_ref[j].astype(F32)
                o_ref[...] = v if j == 0 else o_ref[...] + v

            pl.when(s_ref[0] == j)(own)
            pl.when(s_ref[0] != j)(other)

    in_specs = [pl.BlockSpec((G, tr, C), lambda i, s: (0, i, 0)),
                pl.BlockSpec((None, tr, C), lambda i, s: (s[0], i, 0))]
    args = [where, landed, parts]
    aliases = {}
    if prev is not None:
        in_specs.append(ANY)
        args.append(prev)
        aliases = {3: 0}
    return pl.pallas_call(
        body, name=name,
        grid_spec=pltpu.PrefetchScalarGridSpec(
            num_scalar_prefetch=1, grid=(nr,),
            in_specs=in_specs,
            out_specs=pl.BlockSpec((tr, C), lambda i, s: (base + s[1] * nr + i, 0))),
        out_shape=jax.ShapeDtypeStruct((total_rows, C), F32),
        input_output_aliases=aliases,
        compiler_params=_cp("arbitrary"),
    )(*args)


def _adamw(w, g, m, v, *, name, emit_grad=False):
    R, C = w.shape
    tr = _row_tile(R, cap=max(16, ADAMW_BLOCK_BYTES // (4 * C)))
    n_out = 4 if emit_grad else 3

    def body(w_ref, g_ref, m_ref, v_ref, d_ref, mo_ref, vo_ref, *go_ref):
        gv = g_ref[...]
        if emit_grad:
            go_ref[0][...] = gv
        mn = ADAM_B1 * m_ref[...] + (1.0 - ADAM_B1) * gv
        vn = ADAM_B2 * v_ref[...] + (1.0 - ADAM_B2) * jnp.square(gv)
        m_hat = mn / (1.0 - ADAM_B1 ** ADAM_STEP)
        v_hat = vn / (1.0 - ADAM_B2 ** ADAM_STEP)
        d_ref[...] = -ADAM_LR * (m_hat / (jnp.sqrt(v_hat) + ADAM_EPS) + ADAM_WD * w_ref[...])
        mo_ref[...] = mn
        vo_ref[...] = vn

    spec = pl.BlockSpec((tr, C), lambda i: (i, 0))
    shp = jax.ShapeDtypeStruct((R, C), F32)
    return pl.pallas_call(
        body, name=name, grid=(R // tr,), in_specs=[spec] * 4, out_specs=[spec] * n_out,
        out_shape=[shp] * n_out, compiler_params=_cp("parallel"),
    )(w, g, m, v)


def _adamw_stream(w, g, m, v, *, name):
    R, C = w.shape
    tr = _row_tile(R, cap=max(16, ADAMW_BLOCK_BYTES // (4 * C)))
    nblk = R // tr
    nb = STREAM_BUFFERS

    def body(w_ref, g_ref, m_ref, v_ref, d_ref, mo_ref, vo_ref, go_ref, inb, outb, isem, osem):
        ins = (w_ref, g_ref, m_ref, v_ref)
        outs = (d_ref, mo_ref, vo_ref, go_ref)

        def load(b, k):
            return pltpu.make_async_copy(ins[k].at[pl.ds(b * tr, tr)], inb.at[b % nb, k], isem.at[b % nb, k])

        def store(b, k):
            return pltpu.make_async_copy(outb.at[b % 2, k], outs[k].at[pl.ds(b * tr, tr)], osem.at[b % 2, k])

        for b in range(min(nb - 1, nblk)):
            for k in range(4):
                load(b, k).start()
        for b in range(nblk):
            if b + nb - 1 < nblk:
                for k in range(4):
                    load(b + nb - 1, k).start()
            for k in range(4):
                load(b, k).wait()
            if b >= 2:
                for k in range(4):
                    store(b - 2, k).wait()
            s = b % nb
            gv = inb[s, 1]
            mn = ADAM_B1 * inb[s, 2] + (1.0 - ADAM_B1) * gv
            vn = ADAM_B2 * inb[s, 3] + (1.0 - ADAM_B2) * jnp.square(gv)
            m_hat = mn / (1.0 - ADAM_B1 ** ADAM_STEP)
            v_hat = vn / (1.0 - ADAM_B2 ** ADAM_STEP)
            outb[b % 2, 0] = -ADAM_LR * (m_hat / (jnp.sqrt(v_hat) + ADAM_EPS) + ADAM_WD * inb[s, 0])
            outb[b % 2, 1] = mn
            outb[b % 2, 2] = vn
            outb[b % 2, 3] = gv
            for k in range(4):
                store(b, k).start()
        for b in range(max(nblk - 2, 0), nblk):
            for k in range(4):
                store(b, k).wait()

    shp = jax.ShapeDtypeStruct((R, C), F32)
    return pl.pallas_call(
        body, name=name, in_specs=[ANY] * 4, out_specs=[ANY] * 4, out_shape=[shp] * 4,
        scratch_shapes=[pltpu.VMEM((nb, 4, tr, C), F32), pltpu.VMEM((2, 4, tr, C), F32),
                        pltpu.SemaphoreType.DMA((nb, 4)), pltpu.SemaphoreType.DMA((2, 4))],
        compiler_params=pltpu.CompilerParams(vmem_limit_bytes=VMEM_LIMIT),
    )(w, g, m, v)


def _place():
    x = lax.axis_index("x")
    y = lax.axis_index("y")
    c = lax.axis_index("c")
    chips = [(1 - x, y), (x, 1 - y), (1 - x, 1 - y)]
    return x, y, c, chips


def _chunk_rows(rows, row_bytes, align):
    if rows <= align:
        return rows
    cands = [r for r in range(align, rows + 1, align) if rows % r == 0]
    fit = [r for r in cands if r * row_bytes <= DMA_CHUNK_BYTES]
    return max(fit) if fit else min(cands)


def _row_align(dtype):
    return 8 * (4 // jnp.dtype(dtype).itemsize)


def _start_chunks(make, rows, rc):
    for r0 in range(0, rows, rc):
        make(r0, rc).start()


def _piece_rows(ref, piece, j, h, r0=0, n=None):
    _, lead, off, rows = piece
    rh = rows // 2
    n = rh if n is None else n
    if lead is not None:
        return ref.at[lead, j, pl.ds(h * rh + r0, n)]
    return ref.at[j, pl.ds(off + h * rh + r0, n)]


def _piece_chunk(arr, piece):
    rh = piece[3] // 2
    return rh, _chunk_rows(rh, arr.shape[-1] * arr.dtype.itemsize, _row_align(arr.dtype))


def _ag_start(arrays, taps, groups, *, name):
    na = len(arrays)
    ng = len(groups)
    nt = 0 if taps is None else 1
    n_sem = [3 * len(grp) + (3 if nt and g == 0 else 0) for g, grp in enumerate(groups)]

    def body(*refs):
        ins = refs[:na]
        taps_ref = refs[na] if nt else None
        sems = refs[na + nt:na + nt + 2 * ng]
        token = refs[-1]
        token[...] = jnp.zeros_like(token)
        x, y, c, chips = _place()
        myj = 2 * x + y
        for g, grp in enumerate(groups):
            ssem, rsem = sems[2 * g], sems[2 * g + 1]
            for idx, piece in enumerate(grp):
                ref = ins[piece[0]]
                rh, rc = _piece_chunk(arrays[piece[0]], piece)
                for k, (px, py) in enumerate(chips):
                    def send(r0, n, ref=ref, piece=piece, idx=idx, k=k, px=px, py=py, ssem=ssem, rsem=rsem):
                        part = _piece_rows(ref, piece, myj, c, r0, n)
                        return pltpu.make_async_remote_copy(
                            src_ref=part, dst_ref=part, send_sem=ssem.at[3 * idx + k], recv_sem=rsem.at[3 * idx + k],
                            device_id=(px, py, c), device_id_type=MESH)
                    _start_chunks(send, rh, rc)
            if nt and g == 0:
                for k, (px, py) in enumerate(chips):
                    pltpu.make_async_remote_copy(
                        src_ref=taps_ref.at[myj], dst_ref=taps_ref.at[myj],
                        send_sem=ssem.at[3 * len(grp) + k], recv_sem=rsem.at[3 * len(grp) + k],
                        device_id=(px, py, c), device_id_type=MESH).start()

    sem_shapes = []
    for n in n_sem:
        sem_shapes += [pltpu.SemaphoreType.DMA((n,)), pltpu.SemaphoreType.DMA((n,))]
    ops = list(arrays) + ([taps] if nt else [])
    bufs = [pltpu.HBM(a.shape, a.dtype) for a in ops]
    outs = pl.pallas_call(
        body, name=name,
        out_shape=(*sem_shapes, *bufs, jax.ShapeDtypeStruct((8, LANES), F32)),
        in_specs=[HBM] * (na + nt),
        out_specs=(*([SEM] * (2 * ng)), *([HBM] * (na + nt)), pl.BlockSpec(memory_space=pltpu.VMEM)),
        input_output_aliases={i: 2 * ng + i for i in range(na + nt)},
        compiler_params=pltpu.CompilerParams(has_side_effects=EFFECT),
    )(*[_in_hbm(a) for a in ops])
    sems = [(outs[2 * g], outs[2 * g + 1]) for g in range(ng)]
    return sems, list(outs[2 * ng:2 * ng + na]), (outs[2 * ng + na] if nt else None), outs[-1]


def _ag_wait(sems, vals, taps, group, after, *, name):
    nv = len(vals)
    extra = ([taps] if taps is not None else [])
    nb = nv + len(extra)

    def body(*refs):
        bufs = refs[:nb]
        ssem, rsem = refs[nb], refs[nb + 1]
        x, y, c, chips = _place()
        for idx, piece in enumerate(group):
            for k, (px, py) in enumerate(chips):
                got = _piece_rows(bufs[piece[0]], piece, 2 * px + py, c)
                cp = pltpu.make_async_remote_copy(
                    src_ref=got, dst_ref=got, send_sem=ssem.at[3 * idx + k], recv_sem=rsem.at[3 * idx + k],
                    device_id=(px, py, c), device_id_type=MESH)
                cp.wait_send()
                cp.wait_recv()
        if taps is not None:
            for k, (px, py) in enumerate(chips):
                got = bufs[nv].at[2 * px + py]
                cp = pltpu.make_async_remote_copy(
                    src_ref=got, dst_ref=got, send_sem=ssem.at[3 * len(group) + k],
                    recv_sem=rsem.at[3 * len(group) + k], device_id=(px, py, c), device_id_type=MESH)
                cp.wait_send()
                cp.wait_recv()

    ops = list(vals) + extra
    shapes = [pltpu.HBM(a.shape, a.dtype) for a in ops]
    outs = pl.pallas_call(
        body, name=name,
        out_shape=tuple(shapes),
        in_specs=[HBM] * nb + [SEM, SEM] + _dep_spec(after),
        out_specs=[HBM] * nb,
        input_output_aliases={i: i for i in range(nb)},
        compiler_params=pltpu.CompilerParams(has_side_effects=EFFECT),
    )(*ops, sems[0], sems[1], *_dep_arg(after))
    return list(outs[:nv]), (outs[nv] if taps is not None else None)


def _ag_forward(vals, group, *, name):
    nv = len(vals)
    npc = len(group)

    def body(*refs):
        bufs = refs[nv:2 * nv]
        fsem, gsem = refs[2 * nv:]
        x, y, c, chips = _place()
        sib = (x, y, 1 - c)
        sent = []
        for idx, piece in enumerate(group):
            rh, rc = _piece_chunk(vals[piece[0]], piece)
            for k, (px, py) in enumerate(chips):
                def fwd(r0, n, piece=piece, idx=idx, k=k, pj=2 * px + py):
                    part = _piece_rows(bufs[piece[0]], piece, pj, c, r0, n)
                    return pltpu.make_async_remote_copy(
                        src_ref=part, dst_ref=part, send_sem=fsem.at[3 * idx + k], recv_sem=gsem.at[3 * idx + k],
                        device_id=sib, device_id_type=MESH)
                _start_chunks(fwd, rh, rc)
                sent.append(fwd(0, rh))
        for idx, piece in enumerate(group):
            for k, (px, py) in enumerate(chips):
                theirs = _piece_rows(bufs[piece[0]], piece, 2 * px + py, 1 - c)
                pltpu.make_async_remote_copy(
                    src_ref=theirs, dst_ref=theirs, send_sem=fsem.at[3 * idx + k], recv_sem=gsem.at[3 * idx + k],
                    device_id=sib, device_id_type=MESH).wait_recv()
        for cp in sent:
            cp.wait_send()

    return pl.pallas_call(
        body, name=name,
        in_specs=[ANY] * nv, out_specs=[ANY] * nv,
        out_shape=[jax.ShapeDtypeStruct(v.shape, v.dtype) for v in vals],
        input_output_aliases={i: i for i in range(nv)},
        scratch_shapes=[pltpu.SemaphoreType.DMA((3 * npc,)), pltpu.SemaphoreType.DMA((3 * npc,))],
    )(*vals)


def _sibling_swap(dws, *, name):
    nm = len(dws)
    shapes = [jax.ShapeDtypeStruct((dw.shape[0], dw.shape[1] // 2, dw.shape[2]), dw.dtype) for dw in dws]

    def body(*refs):
        ins = refs[:nm]
        theirs = refs[nm:2 * nm]
        ssem, rsem = refs[2 * nm:]
        x, y, c, _ = _place()
        sib = (x, y, 1 - c)
        cps = []
        for m in range(nm):
            G, rh, cols = shapes[m].shape
            rc = _chunk_rows(rh, cols * shapes[m].dtype.itemsize, _row_align(shapes[m].dtype))
            for j in range(G):
                _start_chunks(lambda r0, n, m=m, j=j, rh=rh: pltpu.make_async_remote_copy(
                    src_ref=ins[m].at[j, pl.ds((1 - c) * rh + r0, n)],
                    dst_ref=theirs[m].at[j, pl.ds(r0, n)], send_sem=ssem.at[m], recv_sem=rsem.at[m],
                    device_id=sib, device_id_type=MESH), rh, rc)
            cps.append(pltpu.make_async_remote_copy(
                src_ref=ins[m].at[:, pl.ds((1 - c) * rh, rh), :], dst_ref=theirs[m],
                send_sem=ssem.at[m], recv_sem=rsem.at[m], device_id=sib, device_id_type=MESH))
        for cp in cps:
            cp.wait()

    return pl.pallas_call(
        body, name=name,
        in_specs=[ANY] * nm, out_specs=[ANY] * nm, out_shape=shapes,
        scratch_shapes=[pltpu.SemaphoreType.DMA((nm,)), pltpu.SemaphoreType.DMA((nm,))],
    )(*dws)


HBM = pl.BlockSpec(memory_space=pltpu.HBM)
SEM = pl.BlockSpec(memory_space=pltpu.SEMAPHORE)
EFFECT = pltpu.SideEffectType.DATAFLOW_SIDE_EFFECTING


def _in_hbm(a):
    return pltpu.with_memory_space_constraint(a, pltpu.HBM)


def _swap_start(dws, *, name):
    nm = len(dws)
    lands = [(dw.shape[0], dw.shape[1] // 2, dw.shape[2]) for dw in dws]

    def body(*refs):
        ins = refs[:nm]
        lnd = refs[nm:2 * nm]
        ssem, rsem = refs[2 * nm:2 * nm + 2]
        token = refs[-1]
        x, y, c, _ = _place()
        sib = (x, y, 1 - c)
        for m in range(nm):
            G, rh, cols = lands[m]
            rc = _chunk_rows(rh, cols * dws[m].dtype.itemsize, _row_align(dws[m].dtype))
            for j in range(G):
                _start_chunks(lambda r0, n, m=m, j=j, rh=rh: pltpu.make_async_remote_copy(
                    src_ref=ins[m].at[j, pl.ds((1 - c) * rh + r0, n)], dst_ref=lnd[m].at[j, pl.ds(r0, n)],
                    send_sem=ssem.at[m], recv_sem=rsem.at[m], device_id=sib, device_id_type=MESH), rh, rc)
        token[...] = jnp.zeros_like(token)

    src = [pltpu.HBM(dw.shape, dw.dtype) for dw in dws]
    dst = [pltpu.HBM(s, dw.dtype) for s, dw in zip(lands, dws)]
    outs = pl.pallas_call(
        body, name=name,
        out_shape=(pltpu.SemaphoreType.DMA((nm,)), pltpu.SemaphoreType.DMA((nm,)), *src, *dst,
                   jax.ShapeDtypeStruct((8, LANES), F32)),
        in_specs=[HBM] * (2 * nm),
        out_specs=(SEM, SEM, *([HBM] * (2 * nm)), pl.BlockSpec(memory_space=pltpu.VMEM)),
        input_output_aliases={i: 2 + i for i in range(2 * nm)},
        compiler_params=pltpu.CompilerParams(has_side_effects=EFFECT),
    )(*[_in_hbm(dw) for dw in dws], *[_in_hbm(lax.empty(s, dw.dtype)) for s, dw in zip(lands, dws)])
    return (outs[0], outs[1], list(outs[2:2 + nm]), list(outs[2 + nm:2 + 2 * nm])), outs[-1]


def _swap_wait(handle, after, *, name):
    ssem_in, rsem_in, dws, lands = handle
    nm = len(dws)

    def body(*refs):
        ins = refs[:nm]
        lnd = refs[nm:2 * nm]
        ssem, rsem = refs[2 * nm:2 * nm + 2]
        x, y, c, _ = _place()
        for m in range(nm):
            rh = lands[m].shape[1]
            cp = pltpu.make_async_remote_copy(
                src_ref=ins[m].at[:, pl.ds((1 - c) * rh, rh), :], dst_ref=lnd[m],
                send_sem=ssem.at[m], recv_sem=rsem.at[m], device_id=(x, y, 1 - c), device_id_type=MESH)
            cp.wait_send()
            cp.wait_recv()

    bufs = [pltpu.HBM(a.shape, a.dtype) for a in list(dws) + list(lands)]
    outs = pl.pallas_call(
        body, name=name,
        out_shape=tuple(bufs),
        in_specs=[HBM] * (2 * nm) + [SEM, SEM, ANY],
        out_specs=[HBM] * (2 * nm),
        input_output_aliases={i: i for i in range(2 * nm)},
        compiler_params=pltpu.CompilerParams(has_side_effects=EFFECT),
    )(*dws, *lands, ssem_in, rsem_in, after)
    return list(outs[:nm]), list(outs[nm:])


def _exchange_start(parts, *, name):
    nm = len(parts)

    def body(*refs):
        ins = refs[:nm]
        lands = refs[nm:2 * nm]
        ssem, rsem = refs[2 * nm:2 * nm + 2]
        token = refs[-1]
        x, y, c, chips = _place()
        myj = 2 * x + y
        for m in range(nm):
            _, rh, cols = parts[m].shape
            rc = _chunk_rows(rh, cols * parts[m].dtype.itemsize, _row_align(parts[m].dtype))
            for k, (px, py) in enumerate(chips):
                _start_chunks(lambda r0, n, m=m, k=k, px=px, py=py: pltpu.make_async_remote_copy(
                    src_ref=ins[m].at[2 * px + py, pl.ds(r0, n)], dst_ref=lands[m].at[myj, pl.ds(r0, n)],
                    send_sem=ssem.at[3 * m + k], recv_sem=rsem.at[3 * m + k],
                    device_id=(px, py, c), device_id_type=MESH), rh, rc)
        token[...] = jnp.zeros_like(token)

    bufs = [pltpu.HBM(p.shape, p.dtype) for p in parts]
    outs = pl.pallas_call(
        body, name=name,
        out_shape=(pltpu.SemaphoreType.DMA((3 * nm,)), pltpu.SemaphoreType.DMA((3 * nm,)), *bufs, *bufs,
                   jax.ShapeDtypeStruct((8, LANES), F32)),
        in_specs=[HBM] * (2 * nm),
        out_specs=(SEM, SEM, *([HBM] * (2 * nm)), pl.BlockSpec(memory_space=pltpu.VMEM)),
        input_output_aliases={i: 2 + i for i in range(2 * nm)},
        compiler_params=pltpu.CompilerParams(has_side_effects=EFFECT),
    )(*[_in_hbm(p) for p in parts], *[_in_hbm(lax.empty(p.shape, p.dtype)) for p in parts])
    return (outs[0], outs[1], list(outs[2:2 + nm]), list(outs[2 + nm:2 + 2 * nm])), outs[-1]


def _exchange_wait(handle, after, *, name):
    ssem_in, rsem_in, parts, lands = handle
    nm = len(parts)

    def body(*refs):
        ins = refs[:nm]
        lnd = refs[nm:2 * nm]
        ssem, rsem = refs[2 * nm:2 * nm + 2]
        x, y, c, chips = _place()
        for m in range(nm):
            for k, (px, py) in enumerate(chips):
                pj = 2 * px + py
                cp = pltpu.make_async_remote_copy(
                    src_ref=ins[m].at[pj], dst_ref=lnd[m].at[pj],
                    send_sem=ssem.at[3 * m + k], recv_sem=rsem.at[3 * m + k],
                    device_id=(px, py, c), device_id_type=MESH)
                cp.wait_send()
                cp.wait_recv()

    bufs = [pltpu.HBM(p.shape, p.dtype) for p in parts]
    outs = pl.pallas_call(
        body, name=name,
        out_shape=(*bufs, *bufs),
        in_specs=[HBM] * (2 * nm) + [SEM, SEM, ANY],
        out_specs=[HBM] * (2 * nm),
        input_output_aliases={i: i for i in range(2 * nm)},
        compiler_params=pltpu.CompilerParams(has_side_effects=EFFECT),
    )(*parts, *lands, ssem_in, rsem_in, after)
    return list(outs[nm:]), list(outs[:nm])


def _sibling_join(grads, regions, *, name):
    nm = len(grads)
    nr = len(regions)
    shapes = [jax.ShapeDtypeStruct(g.shape, g.dtype) for g in grads]

    def body(*refs):
        outs = refs[nm:2 * nm]
        ssem, rsem = refs[2 * nm:]
        x, y, c, _ = _place()
        sib = (x, y, 1 - c)
        cps = []
        for i, (m, off, rows) in enumerate(regions):
            rh, cols = rows // 2, grads[m].shape[1]
            rc = _chunk_rows(rh, cols * grads[m].dtype.itemsize, _row_align(grads[m].dtype))

            def send(r0, n, i=i, m=m, off=off, rh=rh):
                part = outs[m].at[pl.ds(off + c * rh + r0, n)]
                return pltpu.make_async_remote_copy(
                    src_ref=part, dst_ref=part, send_sem=ssem.at[i], recv_sem=rsem.at[i],
                    device_id=sib, device_id_type=MESH)
            _start_chunks(send, rh, rc)
            cps.append(send(0, rh))
        for i, (m, off, rows) in enumerate(regions):
            rh = rows // 2
            cps[i].wait_send()
            got = outs[m].at[pl.ds(off + (1 - c) * rh, rh)]
            pltpu.make_async_remote_copy(
                src_ref=got, dst_ref=got, send_sem=ssem.at[i], recv_sem=rsem.at[i],
                device_id=sib, device_id_type=MESH).wait_recv()

    return pl.pallas_call(
        body, name=name,
        in_specs=[ANY] * nm, out_specs=[ANY] * nm, out_shape=shapes,
        input_output_aliases={i: i for i in range(nm)},
        scratch_shapes=[pltpu.SemaphoreType.DMA((nr,)), pltpu.SemaphoreType.DMA((nr,))],
    )(*grads)


def _all_reduce_small(pack):
    R, C = pack.shape

    def body(in_ref, out_ref, slots, ssem, rsem):
        x, y, c, _ = _place()
        me = 4 * x + 2 * y + c
        slots[me] = in_ref[...]
        cps = []
        for k in range(1, N_DEV):
            dx, dy, dc = (k >> 2) & 1, (k >> 1) & 1, k & 1
            peer = (x ^ dx, y ^ dy, c ^ dc)
            cp = pltpu.make_async_remote_copy(
                src_ref=in_ref, dst_ref=slots.at[me], send_sem=ssem.at[k], recv_sem=rsem.at[k],
                device_id=peer, device_id_type=MESH)
            cp.start()
            cps.append(cp)
        for k in range(1, N_DEV):
            dx, dy, dc = (k >> 2) & 1, (k >> 1) & 1, k & 1
            got = slots.at[4 * (x ^ dx) + 2 * (y ^ dy) + (c ^ dc)]
            pltpu.make_async_remote_copy(
                src_ref=got, dst_ref=got, send_sem=ssem.at[k], recv_sem=rsem.at[k],
                device_id=(x ^ dx, y ^ dy, c ^ dc), device_id_type=MESH).wait_recv()
        for cp in cps:
            cp.wait_send()
        acc = slots[0]
        for s in range(1, N_DEV):
            acc = acc + slots[s]
        out_ref[...] = acc

    return pl.pallas_call(
        body, name="ar_small",
        in_specs=[pl.BlockSpec(memory_space=pltpu.VMEM)],
        out_specs=pl.BlockSpec(memory_space=pltpu.VMEM),
        out_shape=jax.ShapeDtypeStruct((R, C), F32),
        scratch_shapes=[pltpu.VMEM((N_DEV, R, C), F32),
                        pltpu.SemaphoreType.DMA((N_DEV,)), pltpu.SemaphoreType.DMA((N_DEV,))],
    )(pack)


def kernel(x, positions, mix_norm_pre, mix_norm_post, ffn_norm_pre, ffn_norm_post, ffn_w_gate_up, ffn_w_down, conv_w_in, conv_w, conv_w_out, kv_norm, w_kv, w_q, w_o, loss_target, m_mix_norm_pre, m_mix_norm_post, m_ffn_norm_pre, m_ffn_norm_post, m_ffn_w_gate_up, m_ffn_w_down, m_conv_w_in, m_conv_w, m_conv_w_out, m_kv_norm, m_w_kv, m_w_q, m_w_o, v_mix_norm_pre, v_mix_norm_post, v_ffn_norm_pre, v_ffn_norm_post, v_ffn_w_gate_up, v_ffn_w_down, v_conv_w_in, v_conv_w, v_conv_w_out, v_kv_norm, v_w_kv, v_w_q, v_w_o):
    T, D = x.shape[1], x.shape[2]
    L = ffn_w_gate_up.shape[0]
    n_gu = ffn_w_gate_up.shape[2]
    f_sh = ffn_w_down.shape[1]
    F = N_CHIPS * f_sh
    x0 = x[0]
    tgt = loss_target[0]

    half = HEAD_DIM // 2
    inv_freq = ROPE_THETA ** (-jnp.arange(half, dtype=F32) / half)
    ang = positions[0].astype(F32)[:, None] * inv_freq
    cosv, sinv = jnp.cos(ang), jnp.sin(ang)
    cos2 = jnp.tile(cosv, (1, LANES // half))
    ss2 = jnp.tile(jnp.concatenate([-sinv, sinv], axis=1), (1, LANES // HEAD_DIM))

    def as2d(a):
        return a.reshape(-1, a.shape[-1])

    big = [ffn_w_gate_up, ffn_w_down, conv_w_in, conv_w_out, w_kv, w_q, w_o]
    big_m = [m_ffn_w_gate_up, m_ffn_w_down, m_conv_w_in, m_conv_w_out, m_w_kv, m_w_q, m_w_o]
    big_v = [v_ffn_w_gate_up, v_ffn_w_down, v_conv_w_in, v_conv_w_out, v_w_kv, v_w_q, v_w_o]
    chip = 2 * lax.axis_index("x") + lax.axis_index("y")
    where = jnp.stack([chip, lax.axis_index("c")]).astype(jnp.int32)
    tc = conv_w.shape[2]
    cw_pad = jnp.concatenate([conv_w[0], jnp.zeros((8 - conv_w.shape[1], tc), F32)], axis=0)

    GU0, GU1, WD0, WD1, WCI, WCO, WKV, WQ, WO = range(9)
    shards = [(ffn_w_gate_up, 0), (ffn_w_gate_up, 1), (ffn_w_down, 0), (ffn_w_down, 1), (conv_w_in, 0),
              (conv_w_out, 0), (w_kv[None], 0), (w_q, 0), (w_o, 0)]
    ag_groups = [
        [(WCI, None, 0, D), (WCO, None, 0, D // N_CHIPS)],
        [(GU0, None, 0, D), (WD0, None, 0, f_sh)],
        [(WKV, None, 0, D), (WQ, None, 0, D)],
        [(WO, None, 0, D // N_CHIPS), (GU1, None, 0, D), (WD1, None, 0, f_sh)],
    ]

    def localised(group, idxs):
        return [(idxs.index(p[0]),) + p[1:] for p in group]

    cur = [None] * len(shards)
    first = [WCI, WCO]
    sems0, vals, taps, token = _ag_start(
        [_cast_place(*shards[i], where, BF, name=f"place{i}") for i in first],
        _cast_place(cw_pad[None], 0, where, F32, name="place_taps"), [localised(ag_groups[0], first)],
        name="ag_start0")
    for i, v in zip(first, vals):
        cur[i] = v
    rest = [i for i in range(len(shards)) if i not in first]
    sems1, vals, _, _ = _ag_start(
        [_cast_place(*shards[i], where, BF, name=f"place{i}", dep=token) for i in rest], None,
        [localised(g, rest) for g in ag_groups[1:]], name="ag_start1")
    for i, v in zip(rest, vals):
        cur[i] = v
    ag_sems = sems0 + sems1

    def gather_group(g, after):
        nonlocal taps
        idxs = sorted({p[0] for p in ag_groups[g]})
        local = localised(ag_groups[g], idxs)
        vals, landed_taps = _ag_wait(ag_sems[g], [cur[i] for i in idxs], taps if g == 0 else None, local, after,
                                     name=f"ag_wait{g}")
        if g == 0:
            taps = landed_taps
        vals = _ag_forward(vals, local, name=f"ag_forward{g}")
        for i, v in zip(idxs, vals):
            cur[i] = v

    def row(a, i):
        return a[i:i + 1]

    gather_group(0, None)
    wci, wco, cw = cur[WCI], cur[WCO].reshape(1, D, D), taps
    z, hn_m0 = _norm_matmul(x0, row(mix_norm_pre, 0), wci, cos2, ss2, name="f0_conv_in",
                            rope_shards=0, scale=1.0, out_dtype=BF)
    vmix = _conv_fwd(z, cw, name="f0_conv")
    y0, h1 = _matmul_postnorm(vmix, wco, 0, row(mix_norm_post, 0), x0, name="f0_conv_out")
    gather_group(1, h1)
    wgu0, wd0 = cur[GU0], cur[WD0].reshape(1, F, D)
    s0, ut0, a0, hn_f0 = _norm_swiglu(h1, row(ffn_norm_pre, 0), wgu0, 0, name="f0_gate_up")
    f0, h2 = _matmul_postnorm(a0, wd0, 0, row(ffn_norm_post, 0), h1, name="f0_down")

    gather_group(2, h2)
    wkv, wq = cur[WKV], cur[WQ]
    kv_all, hn_kv = _norm_matmul(h2, kv_norm.reshape(1, D), wkv, cos2, ss2, name="f1_kv",
                                 rope_shards=N_CHIPS // 2, scale=1.0, out_dtype=BF)
    q_all, hn_m1 = _norm_matmul(h2, row(mix_norm_pre, 1), wq, cos2, ss2, name="f1_q",
                                rope_shards=N_CHIPS, scale=HEAD_DIM ** -0.5, out_dtype=BF)
    o_att, lse = _attn_fwd(q_all, kv_all, name="f1_attn")
    gather_group(3, o_att)
    wgu1, wd1, wo = cur[GU1], cur[WD1].reshape(1, F, D), cur[WO].reshape(1, D, D)
    y1, h3 = _matmul_postnorm(o_att, wo, 0, row(mix_norm_post, 1), h2, name="f1_attn_out")
    s1, ut1, a1, hn_f1 = _norm_swiglu(h3, row(ffn_norm_pre, 1), wgu1, 0, name="f1_gate_up")
    f1, dh4, sq = _matmul_postnorm_loss(a1, wd1, 0, row(ffn_norm_post, 1), h3, tgt, name="f1_down_loss")
    loss_part = 0.5 * sq[0, 0] / D

    gu_shape = (N_CHIPS, D, n_gu)
    in_chips = lambda a: a.reshape(N_CHIPS, -1, a.shape[-1])

    def scatter_start(dws, tag):
        theirs = _sibling_swap(dws, name=f"rs_swap_{tag}")
        parts = [_pair_sum(dw, t, where, name=f"rs_pair_sum_{tag}{i}") for i, (dw, t) in enumerate(zip(dws, theirs))]
        return _exchange_start(parts, name=f"rs_exchange_start_{tag}")

    def scatter_go(swap, after, tag):
        dws, theirs = _swap_wait(swap, after, name=f"rs_swap_wait_{tag}")
        parts = [_pair_sum(dw, t, where, name=f"rs_pair_sum_{tag}{i}") for i, (dw, t) in enumerate(zip(dws, theirs))]
        return _exchange_start(parts, name=f"rs_exchange_start_{tag}")

    dyf1, dg1, du1, d_ffn_post1 = _postnorm_bwd_swiglu(dh4, f1, row(ffn_norm_post, 1), wd1, 0, s1, ut1,
                                                       name="b1_down")
    dwd1 = _grad_matmul(a1, dyf1, (2, F // 2, D), F // 2, D, lambda i, j: (i, 0, 0), None, name="b1_dw_down")
    dwgu1 = _grad_matmul(hn_f1, dg1, gu_shape, D, n_gu, lambda i, j: (j, 0, 0), None, name="b1_dw_gate")
    dwgu1 = _grad_matmul(hn_f1, du1, gu_shape, D, n_gu, lambda i, j: (j + 2, 0, 0), dwgu1, name="b1_dw_up")
    dh3, d_ffn_pre1 = _matmul_prenorm_bwd((dg1, du1), wgu1, 0, h3, row(ffn_norm_pre, 1), dh4, name="b1_gate_up")

    dy1, do, d_mix_post1 = _postnorm_bwd_matmul(dh3, y1, row(mix_norm_post, 1), wo, 0, name="b1_attn_out",
                                                da_dtype=F32)
    dwo = _grad_matmul(o_att, dy1, (1, D, D), D, D, lambda i, j: (0, 0, 0), None, name="b1_dw_o")
    swap_a, token = _swap_start([dwgu1, in_chips(dwd1), in_chips(dwo)], name="rs_swap_start_a")
    prev = None
    for gi, (window, dil) in enumerate(BRANCHES):
        prev = _attn_bwd(q_all, kv_all, do, o_att, lse, cos2, ss2, gi, dil, prev, name=f"b1_attn{gi}", dep=token)
        token = None
        if gi == 0:
            rs_a, token = scatter_go(swap_a, prev[0], "a")
    dq_all, dk_all, dv_all = prev
    n_q = wq.shape[2]
    n_kv = wkv.shape[2]
    dwq = _grad_matmul(hn_m1, dq_all, (N_CHIPS, D, n_q), D, n_q, lambda i, j: (j, 0, 0), None, name="b1_dw_q")
    dwkv = _grad_matmul(hn_kv, dk_all, (N_CHIPS, D, n_kv), D, n_kv, lambda i, j: (j, 0, 0), None, name="b1_dw_k")
    dwkv = _grad_matmul(hn_kv, dv_all, (N_CHIPS, D, n_kv), D, n_kv, lambda i, j: (j + 2, 0, 0), dwkv, name="b1_dw_v")
    dh2, d_mix_pre1 = _matmul_prenorm_bwd((dq_all,), wq, 0, h2, row(mix_norm_pre, 1), dh3, name="b1_q")
    dh2, d_kv_norm = _matmul_prenorm_bwd((dk_all, dv_all), wkv, 0, h2, kv_norm.reshape(1, D), dh2, name="b1_kv")
    swap_b, token = _swap_start([dwkv, dwq], name="rs_swap_start_b")

    dyf0, dg0, du0, d_ffn_post0 = _postnorm_bwd_swiglu(dh2, f0, row(ffn_norm_post, 0), wd0, 0, s0, ut0,
                                                       name="b0_down", dep=token)
    rs_b, token = scatter_go(swap_b, dyf0, "b")
    dwd0 = _grad_matmul(a0, dyf0, (2, F // 2, D), F // 2, D, lambda i, j: (i, 0, 0), None, name="b0_dw_down",
                        dep=token)
    dwgu0 = _grad_matmul(hn_f0, dg0, gu_shape, D, n_gu, lambda i, j: (j, 0, 0), None, name="b0_dw_gate")
    dwgu0 = _grad_matmul(hn_f0, du0, gu_shape, D, n_gu, lambda i, j: (j + 2, 0, 0), dwgu0, name="b0_dw_up")
    dh1, d_ffn_pre0 = _matmul_prenorm_bwd((dg0, du0), wgu0, 0, h1, row(ffn_norm_pre, 0), dh2, name="b0_gate_up")
    swap_c, token = _swap_start([dwgu0, in_chips(dwd0)], name="rs_swap_start_c")

    dy0, dvmix, d_mix_post0 = _postnorm_bwd_matmul(dh1, y0, row(mix_norm_post, 0), wco, 0, name="b0_conv_out",
                                                   da_dtype=BF, dep=token)
    rs_c, token = scatter_go(swap_c, dy0, "c")
    dwco = _grad_matmul(vmix, dy0, (1, D, D), D, D, lambda i, j: (0, 0, 0), None, name="b0_dw_conv_out",
                        dep=token)
    dz, dcw = _conv_bwd(z, cw, dvmix, name="b0_conv")
    n_ci = wci.shape[2]
    dwci = _grad_matmul(hn_m0, dz, (N_CHIPS, D, n_ci), D, n_ci, lambda i, j: (j, 0, 0), None, name="b0_dw_conv_in")
    dx, d_mix_pre0 = _matmul_prenorm_bwd((dz,), wci, 0, x0, row(mix_norm_pre, 0), dh1, name="b0_conv_in")

    pack = jnp.concatenate([
        d_mix_pre0, d_mix_pre1, d_mix_post0, d_mix_post1, d_ffn_pre0, d_ffn_pre1, d_ffn_post0, d_ffn_post1,
        d_kv_norm, dcw[0:3], jnp.full((1, D), loss_part, F32),
        jnp.zeros((SMALL_ROWS - 13, D), F32)], axis=0)
    red = _all_reduce_small(pack)
    loss = red[12, 0]
    myj = 2 * lax.axis_index("x") + lax.axis_index("y")
    g_conv_w = lax.dynamic_slice(red, (9, myj * tc), (3, tc))

    zeros7 = jnp.zeros((SMALL_ROWS - 9, D), F32)
    w_small = jnp.concatenate([mix_norm_pre, mix_norm_post, ffn_norm_pre, ffn_norm_post, kv_norm.reshape(1, D), zeros7], axis=0)
    m_small = jnp.concatenate([m_mix_norm_pre, m_mix_norm_post, m_ffn_norm_pre, m_ffn_norm_post, m_kv_norm.reshape(1, D), zeros7], axis=0)
    v_small = jnp.concatenate([v_mix_norm_pre, v_mix_norm_post, v_ffn_norm_pre, v_ffn_norm_post, v_kv_norm.reshape(1, D), zeros7], axis=0)
    d_small, nm_small, nv_small = _adamw(w_small, red, m_small, v_small, name="adamw_small")

    pad5 = jnp.zeros((5, tc), F32)
    d_cw, nm_cw, nv_cw = _adamw(cw_pad, jnp.concatenate([g_conv_w, pad5], axis=0),
                                jnp.concatenate([m_conv_w[0], pad5], axis=0),
                                jnp.concatenate([v_conv_w[0], pad5], axis=0), name="adamw_conv_w")

    rs_d, _ = scatter_start([dwci, in_chips(dwco)], "d")

    pieces = {"a": [(0, D), (1, f_sh), (6, 0)], "b": [(4, 0), (5, 0)], "c": [(0, 0), (1, 0)], "d": [(2, 0), (3, 0)]}
    grads2d = [None] * len(big)
    big_out = [None] * len(big)

    def finish(groups, after, tag):
        regions, idxs = [], []
        for gtag, handle in groups:
            landed, parts = _exchange_wait(handle, after, name=f"rs_exchange_wait_{gtag}")
            for i, (l, p, (wi, off)) in enumerate(zip(landed, parts, pieces[gtag])):
                total = as2d(big[wi]).shape[0]
                grads2d[wi] = _chip_sum(l, p, where, total, off, grads2d[wi], name=f"rs_chip_sum_{gtag}{i}")
                if wi not in idxs:
                    idxs.append(wi)
                regions.append((idxs.index(wi), off, 2 * l.shape[1]))
        joined = _sibling_join([grads2d[wi] for wi in idxs], regions, name=f"rs_sibling_join_{tag}")
        for wi, gr in zip(idxs, joined):
            w = big[wi]
            d_, m_, v_, g_ = _adamw_stream(as2d(w), gr, as2d(big_m[wi]), as2d(big_v[wi]), name=f"adamw{wi}")
            big_out[wi] = (g_.reshape(w.shape), d_.reshape(w.shape), m_.reshape(w.shape), v_.reshape(w.shape))

    finish([("a", rs_a), ("b", rs_b), ("c", rs_c)], dx, "abc")
    finish([("d", rs_d)], big_out[0][1], "d")

    def small(a):
        return (a[0:2], a[2:4], a[4:6], a[6:8])

    def assemble(sm, cwv, kind):
        pre, post, fpre, fpost = small(sm)
        b = [t[kind] for t in big_out]
        return [pre, post, fpre, fpost, b[0], b[1], b[2], cwv[0:3].reshape(conv_w.shape), b[3],
                sm[8], b[4], b[5].reshape(w_q.shape), b[6].reshape(w_o.shape)]

    grads = assemble(red, jnp.concatenate([g_conv_w, pad5], axis=0), 0)
    deltas = assemble(d_small, d_cw, 1)
    new_m = assemble(nm_small, nm_cw, 2)
    new_v = assemble(nv_small, nv_cw, 3)
    return (loss, dx.reshape(x.shape), *grads, *deltas, *new_m, *new_v)
```
